```python
import math
import jax, jax.numpy as jnp
from jax import lax
import numpy as np

D_MODEL = 1024
BATCH = 16
SEQ = 256
DEPTH = 2
DEC_BATCH = 2
DEC_SEQ = 1024
PAST_LEN = 512

GRID_W = 64
N_EVEN = (DEPTH + 1) // 2
N_ODD = DEPTH // 2
EPS = 1e-6

SSD_HEADS = 16
SSD_HEAD_DIM = 64
D_SSD = SSD_HEADS * SSD_HEAD_DIM
SSD_GROUPS = 2
SSD_STATE = 128
SSD_CONV = 5
SSD_CHUNK = 128
CONV_DIM = D_SSD + 2 * SSD_GROUPS * SSD_STATE

ATT_HEADS = 16
ATT_KV_HEADS = 4
ATT_REP = ATT_HEADS // ATT_KV_HEADS
ATT_HEAD_DIM = 64
D_ATT = ATT_HEADS * ATT_HEAD_DIM
ATT_KV_DIM = ATT_KV_HEADS * ATT_HEAD_DIM
WINDOW = 128
ATT_BLOCK = 128
ATT_SCALE = ATT_HEAD_DIM ** -0.5
ROPE_BASE = 10000.0

GLA_HEADS = 4
GLA_DK = D_MODEL // 2
GLA_DV = D_MODEL
GLA_HEAD_K = GLA_DK // GLA_HEADS
GLA_HEAD_V = GLA_DV // GLA_HEADS
GLA_LOWRANK = 16
GLA_GATE_NORM = 16.0
GLA_CHUNK = 64

N_EXPERT_GROUPS = 4
EXPERTS_PER_GROUP = 4
N_EXPERTS = N_EXPERT_GROUPS * EXPERTS_PER_GROUP
TOP_K = 2
D_EXPERT = 512

EVEN_SPLITS = (D_SSD, D_SSD + CONV_DIM, D_SSD + CONV_DIM + 2 * SSD_HEADS,
               D_SSD + CONV_DIM + 2 * SSD_HEADS + D_ATT,
               D_SSD + CONV_DIM + 2 * SSD_HEADS + D_ATT + ATT_KV_DIM)
D_IN_EVEN = D_SSD + CONV_DIM + 2 * SSD_HEADS + D_ATT + 2 * ATT_KV_DIM
ODD_SPLITS = (GLA_DK, 2 * GLA_DK, 2 * GLA_DK + GLA_DV, 2 * GLA_DK + 2 * GLA_DV,
              2 * GLA_DK + 2 * GLA_DV + GLA_LOWRANK)
D_IN_ODD = 2 * GLA_DK + 2 * GLA_DV + 2 * GLA_LOWRANK

kernel_name = 'hybrid_ssd_swa_gla_hmoe_prefix_ctx'

F32 = jnp.float32


def rms_norm(x, w):
    xf = x.astype(F32)
    y = xf * lax.rsqrt(jnp.mean(xf * xf, axis=-1, keepdims=True) + EPS)
    return (y * w.astype(F32)).astype(x.dtype)


def adaln_params(cond, w, b):
    m = jax.nn.silu(cond) @ w + b
    return jnp.split(m[:, None, :], 6, axis=-1)


def modulate(x, norm_w, shift, scale):
    return rms_norm(x, norm_w) * (1 + scale) + shift


def flip(t):
    return jnp.flip(t, axis=1)


def centred_dwconv(x, w, b):
    pad = w.shape[0] // 2
    xp = jnp.pad(x, ((0, 0), (pad, pad), (0, 0)))
    y = lax.conv_general_dilated(xp, w[:, None, :], window_strides=(1,), padding='VALID',
                                 dimension_numbers=('NWC', 'WIO', 'NWC'),
                                 feature_group_count=x.shape[-1])
    return y + b


def ssd_scan(x, dt, A, Bm, Cm, h0):
    b, l, h, p = x.shape
    g, n = Bm.shape[2], Bm.shape[3]
    r = h // g
    L = SSD_CHUNK
    nc = l // L
    xc = (x.astype(F32) * dt[..., None]).reshape(b, nc, L, g, r, p)
    Bc = Bm.astype(F32).reshape(b, nc, L, g, n)
    Cc = Cm.astype(F32).reshape(b, nc, L, g, n)
    a_cs = jnp.cumsum((dt * A).reshape(b, nc, L, g, r), axis=2)
    causal = jnp.tril(jnp.ones((L, L), bool))[:, :, None, None]
    seg = a_cs[:, :, :, None] - a_cs[:, :, None, :]
    decay = jnp.exp(jnp.where(causal, seg, -jnp.inf))
    cb = jnp.einsum('bclgn,bcsgn->bclsg', Cc, Bc)
    y_diag = jnp.einsum('bclsg,bclsgr,bcsgrp->bclgrp', cb, decay, xc)
    to_end = jnp.exp(a_cs[:, :, -1:] - a_cs)
    chunk_states = jnp.einsum('bclgn,bclgr,bclgrp->bcgrpn', Bc, to_end, xc)
    chunk_decay = jnp.exp(a_cs[:, :, -1])

    def step(hprev, inp):
        st, dec = inp
        return dec[..., None, None] * hprev + st, hprev

    h_final, h_prev = lax.scan(step, h0.astype(F32).reshape(b, g, r, p, n),
                               (jnp.moveaxis(chunk_states, 1, 0), jnp.moveaxis(chunk_decay, 1, 0)))
    h_prev = jnp.moveaxis(h_prev, 0, 1)
    y_off = jnp.einsum('bclgn,bclgr,bcgrpn->bclgrp', Cc, jnp.exp(a_cs), h_prev)
    return (y_diag + y_off).reshape(b, l, h, p), h_final.reshape(b, h, p, n)


def ssd_mixer(z, xbc, dt_raw, conv_w, conv_b, dt_bias_f, dt_bias_b, a_log_f, a_log_b, d_skip, norm_w, h0_f, h0_b):
    b, l, _ = z.shape
    xbc = jax.nn.silu(centred_dwconv(xbc, conv_w, conv_b))
    xs, Bm, Cm = jnp.split(xbc, (D_SSD, D_SSD + SSD_GROUPS * SSD_STATE), axis=-1)
    xs = xs.reshape(b, l, SSD_HEADS, SSD_HEAD_DIM)
    Bm = Bm.reshape(b, l, SSD_GROUPS, SSD_STATE)
    Cm = Cm.reshape(b, l, SSD_GROUPS, SSD_STATE)
    dtf = dt_raw.astype(F32)
    dt_f = jax.nn.softplus(dtf[..., :SSD_HEADS] + dt_bias_f.astype(F32))
    dt_b = jax.nn.softplus(dtf[..., SSD_HEADS:] + dt_bias_b.astype(F32))
    A_f = -jnp.exp(a_log_f.astype(F32))
    A_b = -jnp.exp(a_log_b.astype(F32))
    y_f, s_f = ssd_scan(xs, dt_f, A_f, Bm, Cm, h0_f)
    y_b, s_b = ssd_scan(flip(xs), flip(dt_b), A_b, flip(Bm), flip(Cm), h0_b)
    y = y_f + flip(y_b) + d_skip.astype(F32)[:, None] * xs.astype(F32)
    y = rms_norm(y.reshape(b, l, D_SSD) * jax.nn.silu(z.astype(F32)), norm_w)
    return y.astype(z.dtype), s_f, s_b


def axial_rope(t, rows):
    half = t.shape[-1] // 2
    quarter = half // 2
    row = jnp.repeat(jnp.arange(rows), GRID_W)
    col = jnp.tile(jnp.arange(GRID_W), rows)
    inv = ROPE_BASE ** (-jnp.arange(quarter, dtype=F32) / quarter)

    def rot(u, pos):
        ang = pos.astype(F32)[:, None] * inv[None, :]
        cos = jnp.cos(ang)[None, :, None, :]
        sin = jnp.sin(ang)[None, :, None, :]
        u1, u2 = u[..., :quarter], u[..., quarter:]
        return jnp.concatenate([u1 * cos - u2 * sin, u1 * sin + u2 * cos], axis=-1)

    tf = t.astype(F32)
    return jnp.concatenate([rot(tf[..., :half], row), rot(tf[..., half:], col)], axis=-1).astype(t.dtype)


def sink_softmax(scores, sink):
    sk = jnp.broadcast_to(sink.astype(F32).reshape(ATT_KV_HEADS, ATT_REP)[None, :, :, None, None],
                          scores.shape[:-1] + (1,))
    p = jax.nn.softmax(jnp.concatenate([sk, scores], axis=-1), axis=-1)
    return p[..., 1:]


def context_attention(q, k, v, sink):
    b, l = q.shape[:2]
    nb = l // ATT_BLOCK
    qb = jnp.moveaxis(q.reshape(b, nb, ATT_BLOCK, ATT_KV_HEADS, ATT_REP, ATT_HEAD_DIM), 1, 0)
    vf = v.astype(F32)

    def one(q_blk):
        s = jnp.einsum('bqkrd,bskd->bkrqs', q_blk, k, preferred_element_type=F32) * ATT_SCALE
        p = sink_softmax(s, sink)
        return jnp.einsum('bkrqs,bskd->bqkrd', p, vf)

    o = lax.map(one, qb)
    return jnp.moveaxis(o, 0, 1).reshape(b, l, D_ATT).astype(q.dtype)


def latent_attention(q, q_nopos, k, v, ck, cv, sink):
    b, l = q.shape[:2]
    B = ATT_BLOCK
    nb = l // B
    n_ctx = ck.shape[1]
    shp = (b, nb, B, ATT_KV_HEADS, ATT_REP, ATT_HEAD_DIM)
    qb = jnp.moveaxis(q.reshape(shp), 1, 0)
    qnb = jnp.moveaxis(q_nopos.reshape(shp), 1, 0)

    def windows(t):
        tb = t.reshape(b, nb, B, ATT_KV_HEADS, ATT_HEAD_DIM)
        zero = jnp.zeros_like(tb[:, :1])
        win = jnp.concatenate([jnp.concatenate([zero, tb[:, :-1]], axis=1), tb,
                               jnp.concatenate([tb[:, 1:], zero], axis=1)], axis=2)
        return jnp.moveaxis(win, 1, 0)

    kw, vw = windows(k), windows(v)
    qpos = jnp.arange(B)
    kpos = jnp.arange(3 * B) - B
    band = jnp.abs(qpos[:, None] - kpos[None, :]) <= WINDOW
    cvf = cv.astype(F32)

    def one(args):
        q_blk, qn_blk, k_win, v_win, blk = args
        s_ctx = jnp.einsum('bqkrd,bskd->bkrqs', qn_blk, ck, preferred_element_type=F32) * ATT_SCALE
        s_win = jnp.einsum('bqkrd,bskd->bkrqs', q_blk, k_win, preferred_element_type=F32) * ATT_SCALE
        kabs = blk * B + kpos
        ok = band & ((kabs >= 0) & (kabs < l))[None, :]
        s_win = jnp.where(ok, s_win, -jnp.inf)
        p = sink_softmax(jnp.concatenate([s_ctx, s_win], axis=-1), sink)
        return (jnp.einsum('bkrqs,bskd->bqkrd', p[..., :n_ctx], cvf)
                + jnp.einsum('bkrqs,bskd->bqkrd', p[..., n_ctx:], v_win.astype(F32)))

    o = lax.map(one, (qb, qnb, kw, vw, jnp.arange(nb)))
    return jnp.moveaxis(o, 0, 1).reshape(b, l, D_ATT).astype(q.dtype)


def even_mixer_context(h, w_in, conv_w, conv_b, dtb_f, dtb_b, al_f, al_b, d_skip, ssd_nw, sink, w_out):
    b, l, _ = h.shape
    z, xbc, dtr, q, k, v = jnp.split(h @ w_in, EVEN_SPLITS, axis=-1)
    h0 = jnp.zeros((b, SSD_HEADS, SSD_HEAD_DIM, SSD_STATE), F32)
    y_ssd, s_f, s_b = ssd_mixer(z, xbc, dtr, conv_w, conv_b, dtb_f, dtb_b, al_f, al_b, d_skip, ssd_nw, h0, h0)
    q = q.reshape(b, l, ATT_HEADS, ATT_HEAD_DIM)
    k = k.reshape(b, l, ATT_KV_HEADS, ATT_HEAD_DIM)
    v = v.reshape(b, l, ATT_KV_HEADS, ATT_HEAD_DIM)
    y_att = context_attention(q, k, v, sink)
    out = jnp.concatenate([y_ssd, y_att], axis=-1) @ w_out
    return out, k, v, s_f, s_b


def even_mixer_latent(h, ck, cv, h0_f, h0_b, rows, w_in, conv_w, conv_b, dtb_f, dtb_b, al_f, al_b, d_skip, ssd_nw, sink, w_out):
    b, l, _ = h.shape
    z, xbc, dtr, q, k, v = jnp.split(h @ w_in, EVEN_SPLITS, axis=-1)
    y_ssd, _, _ = ssd_mixer(z, xbc, dtr, conv_w, conv_b, dtb_f, dtb_b, al_f, al_b, d_skip, ssd_nw, h0_f, h0_b)
    q = q.reshape(b, l, ATT_HEADS, ATT_HEAD_DIM)
    k = k.reshape(b, l, ATT_KV_HEADS, ATT_HEAD_DIM)
    v = v.reshape(b, l, ATT_KV_HEADS, ATT_HEAD_DIM)
    y_att = latent_attention(axial_rope(q, rows), q, axial_rope(k, rows), v, ck, cv, sink)
    return jnp.concatenate([y_ssd, y_att], axis=-1) @ w_out


def gla_scan(q, k, v, g, s0):
    b, l, h, dk = q.shape
    dv = v.shape[-1]
    C = GLA_CHUNK
    nc = l // C
    qc = q.astype(F32).reshape(b, nc, C, h, dk)
    kc = k.astype(F32).reshape(b, nc, C, h, dk)
    vc = v.astype(F32).reshape(b, nc, C, h, dv)
    cs = jnp.cumsum(g.reshape(b, nc, C, h, dk), axis=2)
    q_t = qc * jnp.exp(cs)
    k_t = kc * jnp.exp(-cs)
    causal = jnp.tril(jnp.ones((C, C), bool))
    att = jnp.where(causal, jnp.einsum('bclhd,bcshd->bchls', q_t, k_t), 0.0)
    y_intra = jnp.einsum('bchls,bcshv->bclhv', att, vc)
    chunk_kv = jnp.einsum('bcshd,bcshv->bchdv', kc * jnp.exp(cs[:, :, -1:] - cs), vc)
    chunk_decay = jnp.exp(cs[:, :, -1])

    def step(sprev, inp):
        kv, dec = inp
        return dec[..., None] * sprev + kv, sprev

    s_final, s_prev = lax.scan(step, s0.astype(F32),
                               (jnp.moveaxis(chunk_kv, 1, 0), jnp.moveaxis(chunk_decay, 1, 0)))
    y_inter = jnp.einsum('bclhd,bchdv->bclhv', q_t, jnp.moveaxis(s_prev, 0, 1))
    return (y_intra + y_inter).reshape(b, l, h, dv), s_final


def gla_layer(h, w_in, w_gk2_f, b_gk_f, w_gk2_b, b_gk_b, norm_w, w_out, s0_f, s0_b):
    b, l, _ = h.shape
    q, k, v, r, lr_f, lr_b = jnp.split(h @ w_in, ODD_SPLITS, axis=-1)
    q = q.reshape(b, l, GLA_HEADS, GLA_HEAD_K) * GLA_HEAD_K ** -0.5
    k = k.reshape(b, l, GLA_HEADS, GLA_HEAD_K)
    v = v.reshape(b, l, GLA_HEADS, GLA_HEAD_V)
    g_f = (jax.nn.log_sigmoid((lr_f @ w_gk2_f + b_gk_f).astype(F32)) / GLA_GATE_NORM).reshape(b, l, GLA_HEADS, GLA_HEAD_K)
    g_b = (jax.nn.log_sigmoid((lr_b @ w_gk2_b + b_gk_b).astype(F32)) / GLA_GATE_NORM).reshape(b, l, GLA_HEADS, GLA_HEAD_K)
    y_f, s_f = gla_scan(q, k, v, g_f, s0_f)
    y_b, s_b = gla_scan(flip(q), flip(k), flip(v), flip(g_b), s0_b)
    y = rms_norm(y_f + flip(y_b), norm_w).reshape(b, l, GLA_DV) * jax.nn.silu(r.astype(F32))
    return y.astype(h.dtype) @ w_out, s_f, s_b


def hier_moe(h, w_rg, b_rg, w_re, b_re, w_gate, w_up, w_down):
    b, l, d = h.shape
    t = h.reshape(-1, d)
    n_tok = t.shape[0]
    g_logit = (t @ w_rg + b_rg).astype(F32)
    g_sel = jnp.argmax(g_logit, axis=-1)
    g_prob = jnp.take_along_axis(jax.nn.softmax(g_logit, axis=-1), g_sel[:, None], axis=-1)
    e_logit = (t @ w_re + b_re).astype(F32).reshape(n_tok, N_EXPERT_GROUPS, EXPERTS_PER_GROUP)
    e_logit = jnp.take_along_axis(e_logit, g_sel[:, None, None], axis=1)[:, 0]
    top_v, top_i = lax.top_k(jax.nn.softmax(e_logit, axis=-1), TOP_K)
    top_v = top_v / jnp.sum(top_v, axis=-1, keepdims=True)
    local = jnp.einsum('tk,tke->te', top_v, jax.nn.one_hot(top_i, EXPERTS_PER_GROUP, dtype=F32))
    combine = (jax.nn.one_hot(g_sel, N_EXPERT_GROUPS, dtype=F32)[:, :, None] * local[:, None, :]
               * g_prob[:, :, None]).reshape(n_tok, N_EXPERTS).astype(t.dtype)
    hg = jnp.einsum('td,edf->tef', t, w_gate)
    hu = jnp.einsum('td,edf->tef', t, w_up)
    act = jax.nn.silu(hg) * hu * combine[:, :, None]
    return jnp.einsum('tef,efd->td', act, w_down).reshape(b, l, d)


def setup_inputs(seed: int = 0) -> dict:
    key = jax.random.key(seed)
    keys = iter(jax.random.split(key, 64))
    D = D_MODEL

    def nrm(shape, scale):
        return jax.random.normal(next(keys), shape, F32) * scale

    dt0 = jnp.exp(jax.random.uniform(next(keys), (2, N_EVEN, SSD_HEADS), F32,
                                     minval=math.log(1e-3), maxval=math.log(1e-1)))
    dt_bias = dt0 + jnp.log(-jnp.expm1(-dt0))
    a_log = jnp.log(jax.random.uniform(next(keys), (2, N_EVEN, SSD_HEADS), F32, minval=1.0, maxval=16.0))
    return {
        'x_prompt': nrm((BATCH, SEQ, D), 1.0),
        'x_sample': nrm((DEC_BATCH, DEC_SEQ, D), 1.0),
        'cache_k_attn': nrm((DEC_BATCH, N_EVEN, PAST_LEN, ATT_KV_HEADS, ATT_HEAD_DIM), 1.0),
        'cache_v_attn': nrm((DEC_BATCH, N_EVEN, PAST_LEN, ATT_KV_HEADS, ATT_HEAD_DIM), 1.0),
        'state_ssd_fwd': nrm((DEC_BATCH, N_EVEN, SSD_HEADS, SSD_HEAD_DIM, SSD_STATE), 0.1),
        'state_ssd_bwd': nrm((DEC_BATCH, N_EVEN, SSD_HEADS, SSD_HEAD_DIM, SSD_STATE), 0.1),
        'state_gla_fwd': nrm((DEC_BATCH, N_ODD, GLA_HEADS, GLA_HEAD_K, GLA_HEAD_V), 0.5),
        'state_gla_bwd': nrm((DEC_BATCH, N_ODD, GLA_HEADS, GLA_HEAD_K, GLA_HEAD_V), 0.5),
        'c': nrm((DEC_BATCH, D), 1.0),
        'c_ctx': nrm((D,), 1.0),
        'w_ada': nrm((DEPTH, D, 6 * D), 0.5 * D ** -0.5),
        'b_ada': nrm((DEPTH, 6 * D), 0.02),
        'norm_mix_w': 1.0 + nrm((DEPTH, D), 0.02),
        'norm_ffn_w': 1.0 + nrm((DEPTH, D), 0.02),
        'w_in_even': nrm((N_EVEN, D, D_IN_EVEN), D ** -0.5),
        'conv_w': nrm((N_EVEN, SSD_CONV, CONV_DIM), SSD_CONV ** -0.5),
        'conv_b': nrm((N_EVEN, CONV_DIM), 0.02),
        'dt_bias_fwd': dt_bias[0],
        'dt_bias_bwd': dt_bias[1],
        'a_log_fwd': a_log[0],
        'a_log_bwd': a_log[1],
        'd_skip': 1.0 + nrm((N_EVEN, SSD_HEADS), 0.1),
        'ssd_norm_w': 1.0 + nrm((N_EVEN, D_SSD), 0.02),
        'attn_sink': nrm((N_EVEN, ATT_HEADS), 0.5),
        'w_out_even': nrm((N_EVEN, D_SSD + D_ATT, D), (D_SSD + D_ATT) ** -0.5),
        'w_in_odd': nrm((N_ODD, D, D_IN_ODD), D ** -0.5),
        'w_gk2_fwd': nrm((N_ODD, GLA_LOWRANK, GLA_DK), GLA_LOWRANK ** -0.5),
        'b_gk_fwd': nrm((N_ODD, GLA_DK), 0.1),
        'w_gk2_bwd': nrm((N_ODD, GLA_LOWRANK, GLA_DK), GLA_LOWRANK ** -0.5),
        'b_gk_bwd': nrm((N_ODD, GLA_DK), 0.1),
        'gla_norm_w': 1.0 + nrm((N_ODD, GLA_HEAD_V), 0.02),
        'w_out_odd': nrm((N_ODD, GLA_DV, D), GLA_DV ** -0.5),
        'w_router_group': nrm((DEPTH, D, N_EXPERT_GROUPS), D ** -0.5),
        'b_router_group': nrm((DEPTH, N_EXPERT_GROUPS), 0.01),
        'w_router_expert': nrm((DEPTH, D, N_EXPERTS), D ** -0.5),
        'b_router_expert': nrm((DEPTH, N_EXPERTS), 0.01),
        'w_gate_exp': nrm((DEPTH, N_EXPERTS, D, D_EXPERT), D ** -0.5),
        'w_up_exp': nrm((DEPTH, N_EXPERTS, D, D_EXPERT), D ** -0.5),
        'w_down_exp': nrm((DEPTH, N_EXPERTS, D_EXPERT, D), D_EXPERT ** -0.5),
        'final_norm_w': 1.0 + nrm((D,), 0.02),
    }


def reference(x_prompt, x_sample, cache_k_attn, cache_v_attn, state_ssd_fwd, state_ssd_bwd,
              state_gla_fwd, state_gla_bwd, c, c_ctx, w_ada, b_ada, norm_mix_w, norm_ffn_w,
              w_in_even, conv_w, conv_b, dt_bias_fwd, dt_bias_bwd, a_log_fwd, a_log_bwd, d_skip,
              ssd_norm_w, attn_sink, w_out_even, w_in_odd, w_gk2_fwd, b_gk_fwd, w_gk2_bwd, b_gk_bwd,
              gla_norm_w, w_out_odd, w_router_group, b_router_group, w_router_expert, b_router_expert,
              w_gate_exp, w_up_exp, w_down_exp, final_norm_w):
    rows = x_sample.shape[1] // GRID_W
    xc, xl = x_prompt, x_sample
    ks, vs, sfs, sbs, gfs, gbs = [], [], [], [], [], []
    for i in range(DEPTH):
        j = i // 2
        mc = adaln_params(c_ctx[None, :], w_ada[i], b_ada[i])
        ml = adaln_params(c, w_ada[i], b_ada[i])
        hc = modulate(xc, norm_mix_w[i], mc[0], mc[1])
        hl = modulate(xl, norm_mix_w[i], ml[0], ml[1])
        if i % 2 == 0:
            oc, k_new, v_new, s_f, s_b = even_mixer_context(
                hc, w_in_even[j], conv_w[j], conv_b[j], dt_bias_fwd[j], dt_bias_bwd[j],
                a_log_fwd[j], a_log_bwd[j], d_skip[j], ssd_norm_w[j], attn_sink[j], w_out_even[j])
            ol = even_mixer_latent(
                hl, cache_k_attn[:, j], cache_v_attn[:, j], state_ssd_fwd[:, j], state_ssd_bwd[:, j], rows,
                w_in_even[j], conv_w[j], conv_b[j], dt_bias_fwd[j], dt_bias_bwd[j],
                a_log_fwd[j], a_log_bwd[j], d_skip[j], ssd_norm_w[j], attn_sink[j], w_out_even[j])
            ks.append(k_new)
            vs.append(v_new)
            sfs.append(s_f)
            sbs.append(s_b)
        else:
            z0 = jnp.zeros((xc.shape[0], GLA_HEADS, GLA_HEAD_K, GLA_HEAD_V), F32)
            oc, g_f, g_b = gla_layer(hc, w_in_odd[j], w_gk2_fwd[j], b_gk_fwd[j], w_gk2_bwd[j], b_gk_bwd[j],
                                     gla_norm_w[j], w_out_odd[j], z0, z0)
            ol, _, _ = gla_layer(hl, w_in_odd[j], w_gk2_fwd[j], b_gk_fwd[j], w_gk2_bwd[j], b_gk_bwd[j],
                                 gla_norm_w[j], w_out_odd[j], state_gla_fwd[:, j], state_gla_bwd[:, j])
            gfs.append(g_f)
            gbs.append(g_b)
        xc = xc + mc[2] * oc
        xl = xl + ml[2] * ol
        xc = xc + mc[5] * hier_moe(modulate(xc, norm_ffn_w[i], mc[3], mc[4]), w_router_group[i], b_router_group[i],
                                   w_router_expert[i], b_router_expert[i], w_gate_exp[i], w_up_exp[i], w_down_exp[i])
        xl = xl + ml[5] * hier_moe(modulate(xl, norm_ffn_w[i], ml[3], ml[4]), w_router_group[i], b_router_group[i],
                                   w_router_expert[i], b_router_expert[i], w_gate_exp[i], w_up_exp[i], w_down_exp[i])
    y_prompt = rms_norm(xc, final_norm_w)
    y_sample = rms_norm(xl, final_norm_w)
    new_cache_k_attn = jnp.stack(ks, axis=1)
    new_cache_v_attn = jnp.stack(vs, axis=1)
    new_state_ssd_fwd = jnp.stack(sfs, axis=1)
    new_state_ssd_bwd = jnp.stack(sbs, axis=1)
    new_state_gla_fwd = jnp.stack(gfs, axis=1)
    new_state_gla_bwd = jnp.stack(gbs, axis=1)
    return (y_prompt, y_sample, new_cache_k_attn, new_cache_v_attn, new_state_ssd_fwd, new_state_ssd_bwd, new_state_gla_fwd, new_state_gla_bwd)
```

```python
import functools
import math

import numpy as np
import jax
import jax.numpy as jnp
from jax import lax
from jax.experimental import pallas as pl
from jax.experimental.pallas import tpu as pltpu

F32 = jnp.float32
BF16 = jnp.bfloat16

D = 1024
BATCH, SEQ = 16, 256
DEC_BATCH, DEC_SEQ = 2, 1024
PAST_LEN = 512
GRID_W = 64
EPS = 1e-6
T_CTX = BATCH * SEQ
T_LAT = DEC_BATCH * DEC_SEQ
T = T_CTX + T_LAT

SSD_HEADS, SSD_P, SSD_N, SSD_GROUPS = 16, 64, 128, 2
SSD_CONV = 5
SSD_L = 128
D_SSD = SSD_HEADS * SSD_P
HEADS_PER_GROUP = SSD_HEADS // SSD_GROUPS
GROUP_W = HEADS_PER_GROUP * SSD_P

ATT_HEADS, ATT_KV, ATT_HD = 16, 4, 64
ATT_KV_DIM = ATT_KV * ATT_HD
WINDOW = 128
ATT_BLOCK = 128
ATT_SCALE = ATT_HD ** -0.5
ROPE_BASE = 10000.0

GLA_HEADS, GLA_DK, GLA_DV = 4, 128, 256
GLA_C = 64
GLA_GATE_NORM = 16.0
GLA_LOWRANK = 16

N_GROUPS, EXP_PER_GROUP = 4, 4
N_EXPERTS = 16
D_EXPERT = 512

LANES = 128
SUBLANES = 8
VMEM_LIMIT = 56 * 1024 * 1024

P0_Z, P0_X, P0_Q, P0_BC, P0_K, P0_V, P0_DT = 0, 1024, 2048, 3072, 3584, 3840, 4096
P0_W = 4224
P1_Q, P1_K, P1_V, P1_R, P1_LR = 0, 512, 1024, 2048, 3072
P1_W = 3200

MOE_TM = 256
MOE_TILES = (2 * T) // MOE_TM + N_EXPERTS
MOE_ROWS = MOE_TILES * MOE_TM
ACC_ROWS = (T + SUBLANES) * SUBLANES


def _cparams(n_axes, vmem=VMEM_LIMIT):
    return pltpu.CompilerParams(dimension_semantics=("arbitrary",) * n_axes, vmem_limit_bytes=vmem)


def _silu(x):
    return x / (1.0 + jnp.exp(-x))


def _softplus(x):
    return jnp.maximum(x, 0.0) + jnp.log(1.0 + jnp.exp(-jnp.abs(x)))


def _mm(a, b):
    return jnp.dot(a.astype(BF16), b.astype(BF16), preferred_element_type=F32)


def _mm_nt(a, b):
    return lax.dot_general(a.astype(BF16), b.astype(BF16), (((1,), (1,)), ((), ())),
                           preferred_element_type=F32)


def _mm_tn(a, b):
    return lax.dot_general(a.astype(BF16), b.astype(BF16), (((0,), (0,)), ((), ())),
                           preferred_element_type=F32)


def _rms(x, w):
    return x * lax.rsqrt(jnp.mean(x * x, axis=-1, keepdims=True) + EPS) * w


def _cumsum_rows(x, n):
    row = lax.broadcasted_iota(jnp.int32, x.shape, 0)
    s = 1
    while s < n:
        x = x + jnp.where(row >= s, pltpu.roll(x, s, 0), 0.0)
        s *= 2
    return x


def _mod_row(tok0):
    return jnp.where(tok0 < T_CTX, 0, 1 + (tok0 - T_CTX) // DEC_SEQ)


ADA_TN = 1536


def _adaln_kernel(c_ref, w_ref, b_ref, o_ref):
    s = _silu(c_ref[...])
    o_ref[0] = _mm(s, w_ref[0]) + b_ref[0]


def _adaln(cond8, w_ada, b_ada):
    depth = w_ada.shape[0]
    return pl.pallas_call(
        _adaln_kernel,
        grid=(depth, 6 * D // ADA_TN),
        in_specs=[
            pl.BlockSpec((SUBLANES, D), lambda l, j: (0, 0)),
            pl.BlockSpec((1, D, ADA_TN), lambda l, j: (l, 0, j)),
            pl.BlockSpec((1, 1, ADA_TN), lambda l, j: (l, 0, j)),
        ],
        out_specs=pl.BlockSpec((1, SUBLANES, ADA_TN), lambda l, j: (l, 0, j)),
        out_shape=jax.ShapeDtypeStruct((depth, SUBLANES, 6 * D), F32),
        compiler_params=_cparams(2),
        name="adaln",
    )(cond8, w_ada, b_ada.reshape(depth, 1, 6 * D))


def _mod_spec(layer, chunk, tm):
    return pl.BlockSpec((1, 1, D), lambda i, *_: ((layer * SUBLANES + _mod_row(i * tm)) * 6 + chunk, 0, 0))


def _tokmajor_to_std(ref, tm):
    return jnp.concatenate([ref[pl.ds(k, tm, stride=SUBLANES), :] for k in range(D // LANES)], axis=1)


PROJ_TM = 1024


def _modproj_kernel(*refs, merge, cast_w):
    if merge:
        x_ref, moe_ref, g2_ref, sh_ref, sc_ref, nw_ref, w_ref, o_ref, xo_ref, h_scr = refs
    else:
        x_ref, sh_ref, sc_ref, nw_ref, w_ref, o_ref, h_scr = refs

    @pl.when(pl.program_id(1) == 0)
    def _():
        x = x_ref[...]
        if merge:
            x = x + g2_ref[0] * _tokmajor_to_std(moe_ref, PROJ_TM)
            xo_ref[...] = x
        h = _rms(x, nw_ref[...]) * (1.0 + sc_ref[0]) + sh_ref[0]
        h_scr[...] = h.astype(BF16)

    w = w_ref[...]
    o_ref[...] = jnp.dot(h_scr[...], w.astype(BF16) if cast_w else w, preferred_element_type=F32)


def _modproj(x, mods, layer, norm_w, w, tn, moe=None):
    n = w.shape[1]
    tm = PROJ_TM
    merge = moe is not None
    grid = (T // tm, n // tn)
    xspec = pl.BlockSpec((tm, D), lambda i, j: (i, 0))
    in_specs = [xspec]
    args = [x]
    if merge:
        in_specs += [pl.BlockSpec((tm * SUBLANES, LANES), lambda i, j: (i, 0)), _mod_spec(layer - 1, 5, tm)]
        args += [moe, mods]
    in_specs += [_mod_spec(layer, 0, tm), _mod_spec(layer, 1, tm),
                 pl.BlockSpec((1, D), lambda i, j: (0, 0)),
                 pl.BlockSpec((D, tn), lambda i, j: (0, j))]
    args += [mods, mods, norm_w.reshape(1, D), w]
    out_specs = [pl.BlockSpec((tm, tn), lambda i, j: (i, j))]
    out_shape = [jax.ShapeDtypeStruct((T, n), F32)]
    if merge:
        out_specs.append(xspec)
        out_shape.append(jax.ShapeDtypeStruct((T, D), F32))
    res = pl.pallas_call(
        functools.partial(_modproj_kernel, merge=merge, cast_w=(w.dtype != BF16)),
        grid=grid, in_specs=in_specs, out_specs=out_specs, out_shape=out_shape,
        scratch_shapes=[pltpu.VMEM((tm, D), BF16)],
        compiler_params=_cparams(2),
        name=f"modproj{layer}",
    )(*args)
    return res if merge else res[0]


def _expand_heads(v, off):
    lo = lax.broadcasted_iota(jnp.int32, (v.shape[0], LANES), 1) < SSD_P
    tiles = []
    for q in range(SSD_HEADS // 2):
        a = jnp.broadcast_to(v[:, off + 2 * q:off + 2 * q + 1], (v.shape[0], LANES))
        b = jnp.broadcast_to(v[:, off + 2 * q + 1:off + 2 * q + 2], (v.shape[0], LANES))
        tiles.append(jnp.where(lo, a, b))
    return jnp.concatenate(tiles, axis=1)


def _ssd_kernel(*refs, seq, has_h0):
    if has_h0:
        (z_ref, x_ref, bc_ref, dt_ref, cwx_ref, cwbc_ref, cbx_ref, cbbc_ref, dtb_ref, alog_ref, dsk_ref,
         nw_ref, h0f_ref, h0b_ref, y_ref, sf_ref, sb_ref,
         xpad, bcpad, xc, bcc, a_scr, dt_scr, yacc, hf, hb) = refs
    else:
        (z_ref, x_ref, bc_ref, dt_ref, cwx_ref, cwbc_ref, cbx_ref, cbbc_ref, dtb_ref, alog_ref, dsk_ref,
         nw_ref, y_ref, sf_ref, sb_ref,
         xpad, bcpad, xc, bcc, a_scr, dt_scr, yacc, hf, hb) = refs
    L = SSD_L
    nc = seq // L
    pad = SUBLANES
    half = SSD_CONV // 2

    for buf, src, cw, cb, dst in ((xpad, x_ref, cwx_ref, cbx_ref, xc), (bcpad, bc_ref, cwbc_ref, cbbc_ref, bcc)):
        width = buf.shape[1]
        buf[0:pad, :] = jnp.zeros((pad, width), F32)
        buf[pad + seq:2 * pad + seq, :] = jnp.zeros((pad, width), F32)
        buf[pad:pad + seq, :] = src[...]
        for blk in range(nc):
            acc = jnp.broadcast_to(cb[...], (L, width))
            for j in range(SSD_CONV):
                r0 = pad - half + j + blk * L
                acc = acc + cw[j:j + 1, :] * buf[r0:r0 + L, :]
            dst[blk * L:(blk + 1) * L, :] = _silu(acc)

    lane = lax.broadcasted_iota(jnp.int32, (seq, LANES), 1)
    dts = jnp.where(lane < 2 * SSD_HEADS, _softplus(dt_ref[...] + dtb_ref[...]), 0.0)
    dt_scr[...] = dts
    a_scr[...] = dts * (-jnp.exp(alog_ref[...]))

    if has_h0:
        hf[...] = h0f_ref[...]
        hb[...] = h0b_ref[...]
    else:
        hf[...] = jnp.zeros(hf.shape, F32)
        hb[...] = jnp.zeros(hb.shape, F32)

    row = lax.broadcasted_iota(jnp.int32, (L, L), 0)
    col = lax.broadcasted_iota(jnp.int32, (L, L), 1)
    lane_l = lax.broadcasted_iota(jnp.int32, (L, LANES), 1)
    lo_half = lane_l < SSD_P

    def chunk(c, fwd, h_scr):
        off = 0 if fwd else SSD_HEADS
        r0 = pl.multiple_of(c * L, L)
        a = a_scr[pl.ds(r0, L), :]
        dt = dt_scr[pl.ds(r0, L), :]
        cs = _cumsum_rows(a, L)
        total = cs[L - 1:L, :]
        if fwd:
            u = cs
            rvec = jnp.exp(cs)
            ed = jnp.exp(total - cs) * dt
            keep = col <= row
        else:
            ex = cs - a
            u = -ex
            rvec = jnp.exp(total - ex)
            ed = jnp.exp(ex) * dt
            keep = col >= row
        ut = jnp.transpose(u)
        dtt = jnp.transpose(dt)
        tcol = jnp.transpose(jnp.broadcast_to(total, (L, LANES)))[:, 0:1]
        rexp = _expand_heads(rvec, off)
        edexp = _expand_heads(ed, off)
        x = xc[pl.ds(r0, L), :]
        bc = bcc[pl.ds(r0, L), :]
        outs = []
        for g in range(SSD_GROUPS):
            bg = bc[:, g * SSD_N:(g + 1) * SSD_N]
            cg = bc[:, SSD_GROUPS * SSD_N + g * SSD_N:SSD_GROUPS * SSD_N + (g + 1) * SSD_N]
            cbm = _mm_nt(cg, bg)
            hg = h_scr[g * GROUP_W:(g + 1) * GROUP_W, :]
            xg = x[:, g * GROUP_W:(g + 1) * GROUP_W]
            y_off = _mm_nt(cg, hg) * rexp[:, g * GROUP_W:(g + 1) * GROUP_W]
            tiles = []
            for p in range(HEADS_PER_GROUP // 2):
                xt = xg[:, p * LANES:(p + 1) * LANES]
                acc = None
                for s in range(2):
                    h = off + g * HEADS_PER_GROUP + 2 * p + s
                    seg = u[:, h:h + 1] - ut[h:h + 1, :]
                    m = cbm * jnp.exp(jnp.where(keep, seg, -jnp.inf)) * dtt[h:h + 1, :]
                    xm = jnp.where(lo_half if s == 0 else jnp.logical_not(lo_half), xt, 0.0)
                    d = _mm(m, xm)
                    acc = d if acc is None else acc + d
                tiles.append(acc)
            outs.append(y_off + jnp.concatenate(tiles, axis=1))
            decs = []
            for hh in range(HEADS_PER_GROUP):
                h = off + g * HEADS_PER_GROUP + hh
                decs.append(jnp.broadcast_to(jnp.exp(tcol[h:h + 1, :]), (SSD_P, SSD_N)))
            dec = jnp.concatenate(decs, axis=0)
            h_scr[g * GROUP_W:(g + 1) * GROUP_W, :] = dec * hg + _mm_tn(xg * edexp[:, g * GROUP_W:(g + 1) * GROUP_W], bg)
        return r0, x, jnp.concatenate(outs, axis=1)

    def fwd_body(c, carry):
        r0, _, y = chunk(c, True, hf)
        yacc[pl.ds(r0, L), :] = y
        return carry

    lax.fori_loop(0, nc, fwd_body, 0)

    def bwd_body(i, carry):
        c = nc - 1 - i
        r0, x, y = chunk(c, False, hb)
        y = yacc[pl.ds(r0, L), :] + y + dsk_ref[...] * x
        y = y * _silu(z_ref[pl.ds(r0, L), :])
        y_ref[pl.ds(r0, L), :] = _rms(y, nw_ref[...]).astype(y_ref.dtype)
        return carry

    lax.fori_loop(0, nc, bwd_body, 0)
    sf_ref[...] = hf[...]
    sb_ref[...] = hb[...]


def _ssd(p0, tok0, nseq, seq, cw, cb, dtb, alog, dskip, nw, h0f=None, h0b=None):
    has_h0 = h0f is not None
    b0 = tok0 // seq

    def cols(width, start):
        return pl.BlockSpec((seq, width), lambda s: (b0 + s, start // width))

    def full(shape):
        return pl.BlockSpec(shape, lambda s: (0,) * len(shape))

    in_specs = [cols(D_SSD, P0_Z), cols(D_SSD, P0_X), cols(512, P0_BC), cols(LANES, P0_DT),
                full((SSD_CONV, D_SSD)), full((SSD_CONV, 512)), full((1, D_SSD)), full((1, 512)),
                full((1, LANES)), full((1, LANES)), full((1, D_SSD)), full((1, D_SSD))]
    args = [p0, p0, p0, p0, cw[:, :D_SSD], cw[:, D_SSD:], cb[:D_SSD].reshape(1, -1), cb[D_SSD:].reshape(1, -1),
            dtb, alog, dskip, nw.reshape(1, -1)]
    st_spec = pl.BlockSpec((None, D_SSD, SSD_N), lambda s: (s, 0, 0))
    if has_h0:
        in_specs += [st_spec, st_spec]
        args += [h0f, h0b]
    st_shape = jax.ShapeDtypeStruct((nseq, D_SSD, SSD_N), F32)
    return pl.pallas_call(
        functools.partial(_ssd_kernel, seq=seq, has_h0=has_h0),
        grid=(nseq,), in_specs=in_specs,
        out_specs=[pl.BlockSpec((seq, D_SSD), lambda s: (s, 0)), st_spec, st_spec],
        out_shape=[jax.ShapeDtypeStruct((nseq * seq, D_SSD), BF16), st_shape, st_shape],
        scratch_shapes=[pltpu.VMEM((seq + 2 * SUBLANES, D_SSD), F32), pltpu.VMEM((seq + 2 * SUBLANES, 512), F32),
                        pltpu.VMEM((seq, D_SSD), F32), pltpu.VMEM((seq, 512), F32),
                        pltpu.VMEM((seq, LANES), F32), pltpu.VMEM((seq, LANES), F32),
                        pltpu.VMEM((seq, D_SSD), F32),
                        pltpu.VMEM((D_SSD, SSD_N), F32), pltpu.VMEM((D_SSD, SSD_N), F32)],
        compiler_params=_cparams(1),
        name=f"ssd{seq}",
    )(*args)


def _place_halves(tile, kv_in_high):
    lo = lax.broadcasted_iota(jnp.int32, tile.shape, 1) < ATT_HD
    swapped = pltpu.roll(tile, ATT_HD, 1)
    if kv_in_high:
        return jnp.where(lo, swapped, 0.0), jnp.where(lo, 0.0, tile)
    return jnp.where(lo, tile, 0.0), jnp.where(lo, 0.0, swapped)


def _sink_attend(score_parts, value_parts, sink):
    m = sink
    for s in score_parts:
        m = jnp.maximum(m, jnp.max(s, axis=-1, keepdims=True))
    denom = jnp.exp(sink - m)
    out = None
    for s, v in zip(score_parts, value_parts):
        p = jnp.exp(s - m)
        denom = denom + jnp.sum(p, axis=-1, keepdims=True)
        o = _mm(p, v)
        out = o if out is None else out + o
    return out * (1.0 / denom)


def _ctx_attn_kernel(q_ref, k_ref, v_ref, sink_ref, o_ref):
    sink = sink_ref[...]
    for j in range(ATT_KV):
        t, high = j // 2, (j % 2 == 1)
        k_lo, k_hi = _place_halves(k_ref[:, t * LANES:(t + 1) * LANES], high)
        v_lo, v_hi = _place_halves(v_ref[:, t * LANES:(t + 1) * LANES], high)
        for qt in range(2 * j, 2 * j + 2):
            q = q_ref[:, qt * LANES:(qt + 1) * LANES]
            acc = None
            for s, (kk, vv) in enumerate(((k_lo, v_lo), (k_hi, v_hi))):
                h = 2 * qt + s
                sc = _mm_nt(q, kk) * ATT_SCALE
                o = _sink_attend([sc], [vv], sink[:, h:h + 1])
                acc = o if acc is None else acc + o
            o_ref[:, qt * LANES:(qt + 1) * LANES] = acc.astype(o_ref.dtype)


def _ctx_attn(p0, sink):
    def cols(width, start):
        return pl.BlockSpec((SEQ, width), lambda b: (b, start // width))

    return pl.pallas_call(
        _ctx_attn_kernel,
        grid=(BATCH,),
        in_specs=[cols(D, P0_Q), cols(ATT_KV_DIM, P0_K), cols(ATT_KV_DIM, P0_V),
                  pl.BlockSpec((1, LANES), lambda b: (0, 0))],
        out_specs=pl.BlockSpec((SEQ, D), lambda b: (b, 0)),
        out_shape=jax.ShapeDtypeStruct((T_CTX, D), BF16),
        compiler_params=_cparams(1),
        name="ctx_attn",
    )(p0, p0, p0, sink)


def _rope_tables():
    quarter = ATT_HD // 4
    t = np.arange(DEC_SEQ)
    lane = np.arange(LANES)
    inv = ROPE_BASE ** (-(lane % quarter).astype(np.float64) / quarter)
    pos = np.where(((lane % ATT_HD) < ATT_HD // 2)[None, :], (t // GRID_W)[:, None], (t % GRID_W)[:, None])
    ang = pos * inv[None, :]
    first = ((lane % (2 * quarter)) < quarter)[None, :]
    cos, sin = np.cos(ang), np.sin(ang)
    return (jnp.asarray(cos, F32), jnp.asarray(np.where(first, -sin, 0.0), F32),
            jnp.asarray(np.where(first, 0.0, sin), F32))


def _rope(x, cos, sa, sb):
    quarter = ATT_HD // 4
    return x * cos + pltpu.roll(x, LANES - quarter, 1) * sa + pltpu.roll(x, quarter, 1) * sb


def _lat_attn_kernel(q_ref, kp_ref, kc_ref, kn_ref, vp_ref, vc_ref, vn_ref, ck_ref, cv_ref,
                     cos_ref, sa_ref, sb_ref, sink_ref, o_ref):
    blk = pl.program_id(1)
    nb = pl.num_programs(1)
    B = ATT_BLOCK
    sink = sink_ref[...]

    def tables(b):
        r0 = pl.multiple_of(b * B, B)
        return cos_ref[pl.ds(r0, B), :], sa_ref[pl.ds(r0, B), :], sb_ref[pl.ds(r0, B), :]

    tq = tables(blk)
    tk = [tables(jnp.maximum(blk - 1, 0)), tq, tables(jnp.minimum(blk + 1, nb - 1))]
    qpos = blk * B + lax.broadcasted_iota(jnp.int32, (B, 3 * B), 0)
    kabs = (blk - 1) * B + lax.broadcasted_iota(jnp.int32, (B, 3 * B), 1)
    ok = (jnp.abs(qpos - kabs) <= WINDOW) & (kabs >= 0) & (kabs < nb * B)

    for j in range(ATT_KV):
        t, high = j // 2, (j % 2 == 1)
        sl = slice(t * LANES, (t + 1) * LANES)
        kw = jnp.concatenate([_rope(r[:, sl], *tb) for r, tb in zip((kp_ref, kc_ref, kn_ref), tk)], axis=0)
        vw = jnp.concatenate([r[:, sl] for r in (vp_ref, vc_ref, vn_ref)], axis=0)
        k_lo, k_hi = _place_halves(kw, high)
        v_lo, v_hi = _place_halves(vw, high)
        ck_lo, ck_hi = _place_halves(ck_ref[:, sl], high)
        cv_lo, cv_hi = _place_halves(cv_ref[:, sl], high)
        for qt in range(2 * j, 2 * j + 2):
            q = q_ref[:, qt * LANES:(qt + 1) * LANES]
            qr = _rope(q, *tq)
            acc = None
            for s, (kk, vv, ckk, cvv) in enumerate(((k_lo, v_lo, ck_lo, cv_lo), (k_hi, v_hi, ck_hi, cv_hi))):
                h = 2 * qt + s
                s_ctx = _mm_nt(q, ckk) * ATT_SCALE
                s_win = jnp.where(ok, _mm_nt(qr, kk) * ATT_SCALE, -jnp.inf)
                o = _sink_attend([s_ctx, s_win], [cvv, vv], sink[:, h:h + 1])
                acc = o if acc is None else acc + o
            o_ref[:, qt * LANES:(qt + 1) * LANES] = acc.astype(o_ref.dtype)


def _lat_attn(p0, ck, cv, sink):
    nb = DEC_SEQ // ATT_BLOCK
    base = T_CTX // ATT_BLOCK

    def kv(start, shift):
        return pl.BlockSpec((ATT_BLOCK, ATT_KV_DIM),
                            lambda b, i: (base + b * nb + jnp.clip(i + shift, 0, nb - 1), start // ATT_KV_DIM))

    def full(shape):
        return pl.BlockSpec(shape, lambda b, i: (0,) * len(shape))

    cache = pl.BlockSpec((None, PAST_LEN, ATT_KV_DIM), lambda b, i: (b, 0, 0))
    cos, sa, sb = _rope_tables()
    return pl.pallas_call(
        _lat_attn_kernel,
        grid=(DEC_BATCH, nb),
        in_specs=[pl.BlockSpec((ATT_BLOCK, D), lambda b, i: (base + b * nb + i, P0_Q // D)),
                  kv(P0_K, -1), kv(P0_K, 0), kv(P0_K, 1), kv(P0_V, -1), kv(P0_V, 0), kv(P0_V, 1),
                  cache, cache, full((DEC_SEQ, LANES)), full((DEC_SEQ, LANES)), full((DEC_SEQ, LANES)),
                  full((1, LANES))],
        out_specs=pl.BlockSpec((ATT_BLOCK, D), lambda b, i: (b * nb + i, 0)),
        out_shape=jax.ShapeDtypeStruct((T_LAT, D), BF16),
        compiler_params=_cparams(2),
        name="lat_attn",
    )(p0, p0, p0, p0, p0, p0, p0, ck, cv, cos, sa, sb, sink)


def _log_sigmoid(x):
    return jnp.minimum(x, 0.0) - jnp.log(1.0 + jnp.exp(-jnp.abs(x)))


def _gla_kernel(*refs, seq, has_s0):
    if has_s0:
        (q_ref, k_ref, v_ref, r_ref, lr_ref, w2f_ref, w2b_ref, bf_ref, bb_ref, nw_ref, s0f_ref, s0b_ref,
         y_ref, sf_ref, sb_ref, gf, gb, yf, yb, stf, stb) = refs
    else:
        (q_ref, k_ref, v_ref, r_ref, lr_ref, w2f_ref, w2b_ref, bf_ref, bb_ref, nw_ref,
         y_ref, sf_ref, sb_ref, gf, gb, yf, yb, stf, stb) = refs
    C = GLA_C
    nc = seq // C
    lr = lr_ref[...]
    gf[...] = _log_sigmoid(_mm(lr, w2f_ref[...]) + bf_ref[...]) / GLA_GATE_NORM
    gb[...] = _log_sigmoid(_mm(lr, w2b_ref[...]) + bb_ref[...]) / GLA_GATE_NORM
    for h in range(GLA_HEADS):
        rows = slice(h * GLA_DV, (h + 1) * GLA_DV)
        if has_s0:
            stf[rows, :] = jnp.transpose(s0f_ref[h * GLA_DK:(h + 1) * GLA_DK, :])
            stb[rows, :] = jnp.transpose(s0b_ref[h * GLA_DK:(h + 1) * GLA_DK, :])
        else:
            stf[rows, :] = jnp.zeros((GLA_DV, GLA_DK), F32)
            stb[rows, :] = jnp.zeros((GLA_DV, GLA_DK), F32)

    row = lax.broadcasted_iota(jnp.int32, (C, C), 0)
    col = lax.broadcasted_iota(jnp.int32, (C, C), 1)
    qscale = GLA_DK ** -0.5

    def chunk(c, fwd):
        g_scr, y_scr, st = (gf, yf, stf) if fwd else (gb, yb, stb)
        r0 = pl.multiple_of(c * C, C)
        g = g_scr[pl.ds(r0, C), :]
        cs = _cumsum_rows(g, C)
        total = cs[C - 1:C, :]
        q = q_ref[pl.ds(r0, C), :] * qscale
        k = k_ref[pl.ds(r0, C), :]
        v = v_ref[pl.ds(r0, C), :]
        if fwd:
            qs, ks, ke = q * jnp.exp(cs), k * jnp.exp(-cs), k * jnp.exp(total - cs)
            keep = col <= row
        else:
            ex = cs - g
            qs, ks, ke = q * jnp.exp(total - ex), k * jnp.exp(ex - total), k * jnp.exp(ex)
            keep = col >= row
        dec = jnp.exp(total)
        for h in range(GLA_HEADS):
            kc = slice(h * GLA_DK, (h + 1) * GLA_DK)
            vc = slice(h * GLA_DV, (h + 1) * GLA_DV)
            s_t = st[vc, :]
            att = jnp.where(keep, _mm_nt(qs[:, kc], ks[:, kc]), 0.0)
            y_scr[pl.ds(r0, C), vc] = _mm(att, v[:, vc]) + _mm_nt(qs[:, kc], s_t)
            st[vc, :] = dec[:, kc] * s_t + _mm_tn(v[:, vc], ke[:, kc])

    def body(i, carry):
        chunk(i, True)
        chunk(nc - 1 - i, False)
        return carry

    lax.fori_loop(0, nc, body, 0)

    nw = nw_ref[...]
    for blk in range(seq // LANES):
        rs = slice(blk * LANES, (blk + 1) * LANES)
        y = yf[rs, :] + yb[rs, :]
        gate = _silu(r_ref[rs, :])
        for h in range(GLA_HEADS):
            vc = slice(h * GLA_DV, (h + 1) * GLA_DV)
            y_ref[rs, vc] = (_rms(y[:, vc], nw) * gate[:, vc]).astype(y_ref.dtype)
    for h in range(GLA_HEADS):
        rows = slice(h * GLA_DV, (h + 1) * GLA_DV)
        sf_ref[h * GLA_DK:(h + 1) * GLA_DK, :] = jnp.transpose(stf[rows, :])
        sb_ref[h * GLA_DK:(h + 1) * GLA_DK, :] = jnp.transpose(stb[rows, :])


def _gla(p1, tok0, nseq, seq, w2f, w2b, bgf, bgb, nw, s0f=None, s0b=None):
    has_s0 = s0f is not None
    b0 = tok0 // seq
    dk_all, dv_all = GLA_HEADS * GLA_DK, GLA_HEADS * GLA_DV

    def cols(width, start):
        return pl.BlockSpec((seq, width), lambda s: (b0 + s, start // width))

    def full(shape):
        return pl.BlockSpec(shape, lambda s: (0,) * len(shape))

    in_specs = [cols(dk_all, P1_Q), cols(dk_all, P1_K), cols(dv_all, P1_V), cols(dv_all, P1_R), cols(LANES, P1_LR),
                full((LANES, dk_all)), full((LANES, dk_all)), full((1, dk_all)), full((1, dk_all)), full((1, GLA_DV))]
    args = [p1, p1, p1, p1, p1, w2f, w2b, bgf.reshape(1, -1), bgb.reshape(1, -1), nw.reshape(1, -1)]
    st_spec = pl.BlockSpec((None, dk_all, GLA_DV), lambda s: (s, 0, 0))
    if has_s0:
        in_specs += [st_spec, st_spec]
        args += [s0f, s0b]
    st_shape = jax.ShapeDtypeStruct((nseq, dk_all, GLA_DV), F32)
    return pl.pallas_call(
        functools.partial(_gla_kernel, seq=seq, has_s0=has_s0),
        grid=(nseq,), in_specs=in_specs,
        out_specs=[pl.BlockSpec((seq, dv_all), lambda s: (s, 0)), st_spec, st_spec],
        out_shape=[jax.ShapeDtypeStruct((nseq * seq, dv_all), BF16), st_shape, st_shape],
        scratch_shapes=[pltpu.VMEM((seq, dk_all), F32), pltpu.VMEM((seq, dk_all), F32),
                        pltpu.VMEM((seq, dv_all), F32), pltpu.VMEM((seq, dv_all), F32),
                        pltpu.VMEM((dv_all, GLA_DK), F32), pltpu.VMEM((dv_all, GLA_DK), F32)],
        compiler_params=_cparams(1),
        name=f"gla{seq}",
    )(*args)


ROUTE_TM = 256


def _split_bf16(x):
    hi = x.astype(BF16)
    return hi, (x - hi.astype(F32)).astype(BF16)


def _outproj_kernel(*refs, n_in):
    y_refs = refs[:2 * n_in]
    (w_ref, x_ref, g1_ref, sh_ref, sc_ref, nw_ref, wr_ref, br_ref,
     xo_ref, tr_ref, route_ref, w_scr, wr_hi, wr_lo) = refs[2 * n_in:]
    tm = ROUTE_TM
    is_ctx = pl.program_id(0) < T_CTX // tm

    @pl.when(pl.program_id(0) == 0)
    def _():
        w_scr[...] = w_ref[...].astype(BF16)
        hi, lo = _split_bf16(wr_ref[...])
        wr_hi[...] = hi
        wr_lo[...] = lo

    o = None
    for i in range(n_in):
        y = jnp.where(is_ctx, y_refs[2 * i][...], y_refs[2 * i + 1][...])
        d = jnp.dot(y, w_scr[i * D:(i + 1) * D, :], preferred_element_type=F32)
        o = d if o is None else o + d
    x = x_ref[...] + g1_ref[0] * o
    xo_ref[...] = x
    t = _rms(x, nw_ref[...]) * (1.0 + sc_ref[0]) + sh_ref[0]
    for k in range(D // LANES):
        tr_ref[pl.ds(k, tm, stride=SUBLANES), :] = t[:, k * LANES:(k + 1) * LANES]

    t_hi, t_lo = _split_bf16(t)
    logit = (jnp.dot(t_hi, wr_hi[...], preferred_element_type=F32)
             + jnp.dot(t_lo, wr_hi[...], preferred_element_type=F32)
             + jnp.dot(t_hi, wr_lo[...], preferred_element_type=F32)) + br_ref[...]
    lane = lax.broadcasted_iota(jnp.int32, (tm, LANES), 1)
    lanef = lane.astype(F32)
    neg = -jnp.inf
    gl = jnp.where(lane < N_GROUPS, logit, neg)
    gmax = jnp.max(gl, axis=-1, keepdims=True)
    gsel = jnp.min(jnp.where(gl == gmax, lanef, float(LANES)), axis=-1, keepdims=True)
    gprob = 1.0 / jnp.sum(jnp.exp(gl - gmax), axis=-1, keepdims=True)
    first = N_GROUPS + EXP_PER_GROUP * gsel
    el = jnp.where((lanef >= first) & (lanef < first + EXP_PER_GROUP), logit, neg)
    m1 = jnp.max(el, axis=-1, keepdims=True)
    i1 = jnp.min(jnp.where(el == m1, lanef, float(LANES)), axis=-1, keepdims=True)
    el2 = jnp.where(lanef == i1, neg, el)
    m2 = jnp.max(el2, axis=-1, keepdims=True)
    i2 = jnp.min(jnp.where(el2 == m2, lanef, float(LANES)), axis=-1, keepdims=True)
    e2 = jnp.exp(m2 - m1)
    c1 = gprob / (1.0 + e2)
    c2 = gprob * e2 / (1.0 + e2)
    route_ref[...] = jnp.where(lane == 0, i1 - N_GROUPS,
                               jnp.where(lane == 1, i2 - N_GROUPS,
                                         jnp.where(lane == 2, c1, jnp.where(lane == 3, c2, 0.0))))


def _outproj_route(ys, w_out, x, mods, layer, norm_w, w_router, b_router):
    tm = ROUTE_TM
    n_in = len(ys) // 2
    kdim = w_out.shape[0]

    def full(shape):
        return pl.BlockSpec(shape, lambda i: (0,) * len(shape))

    tile = pl.BlockSpec((tm, D), lambda i: (i, 0))
    n_ctx = T_CTX // tm
    ctx_tile = pl.BlockSpec((tm, D), lambda i: (jnp.minimum(i, n_ctx - 1), 0))
    lat_tile = pl.BlockSpec((tm, D), lambda i: (jnp.maximum(i - n_ctx, 0), 0))
    in_specs = [ctx_tile, lat_tile] * n_in + [full((kdim, D)), tile, _mod_spec(layer, 2, tm), _mod_spec(layer, 3, tm),
                                _mod_spec(layer, 4, tm), full((1, D)), full((D, LANES)), full((1, LANES))]
    return pl.pallas_call(
        functools.partial(_outproj_kernel, n_in=n_in),
        grid=(T // tm,), in_specs=in_specs,
        out_specs=[tile, pl.BlockSpec((tm * SUBLANES, LANES), lambda i: (i, 0)),
                   pl.BlockSpec((tm, LANES), lambda i: (i, 0))],
        out_shape=[jax.ShapeDtypeStruct((T, D), F32), jax.ShapeDtypeStruct((T * SUBLANES, LANES), F32),
                   jax.ShapeDtypeStruct((T, LANES), F32)],
        scratch_shapes=[pltpu.VMEM((kdim, D), BF16), pltpu.VMEM((D, LANES), BF16), pltpu.VMEM((D, LANES), BF16)],
        compiler_params=_cparams(1),
        name=f"outproj{layer}",
    )(*ys, w_out, x, mods, mods, mods, norm_w.reshape(1, D), w_router, b_router)


def _moe_plan(route):
    tm = MOE_TM
    e1 = route[:, 0].astype(jnp.int32)
    e2 = route[:, 1].astype(jnp.int32)
    experts = jnp.arange(N_EXPERTS, dtype=jnp.int32)
    onehot = ((e1[:, None] == experts) | (e2[:, None] == experts)).astype(jnp.int32)
    csum = jnp.cumsum(onehot, axis=0)
    rank = csum - onehot
    counts = csum[-1]
    padded = ((counts + tm - 1) // tm) * tm
    ends = jnp.cumsum(padded)
    starts = ends - padded
    pos1 = starts[e1] + jnp.take_along_axis(rank, e1[:, None], axis=1)[:, 0]
    pos2 = starts[e2] + jnp.take_along_axis(rank, e2[:, None], axis=1)[:, 0]
    tok = jnp.arange(T, dtype=jnp.int32)
    rowmap = jnp.full((MOE_ROWS,), T, jnp.int32).at[pos1].set(tok).at[pos2].set(tok)
    crow = jnp.zeros((MOE_ROWS,), F32).at[pos1].set(route[:, 2]).at[pos2].set(route[:, 3])
    ntiles = ends[-1] // tm
    tile_start = jnp.arange(MOE_TILES, dtype=jnp.int32) * tm
    te = jnp.sum((tile_start[:, None] >= ends[None, :]).astype(jnp.int32), axis=1)
    last = jnp.max(jnp.where(counts > 0, experts, 0))
    meta = jnp.concatenate([jnp.minimum(te, last), ntiles[None]]).astype(jnp.int32)
    return rowmap, meta, crow.reshape(MOE_ROWS, 1)


def _expert_changed(meta_ref, j):
    return (j == 0) | (meta_ref[j] != meta_ref[jnp.maximum(j - 1, 0)])


def _moe_up_kernel(rowmap_ref, meta_ref, tr_hbm, crow_ref, wg_ref, wu_ref, a_ref, tr_scr, g_scr, wg_bf, wu_bf):
    j = pl.program_id(0)
    tm = MOE_TM

    @pl.when(j == 0)
    def _():
        pltpu.sync_copy(tr_hbm, tr_scr)

    @pl.when(j < meta_ref[MOE_TILES])
    def _():
        @pl.when(_expert_changed(meta_ref, j))
        def _():
            wg_bf[...] = wg_ref[...].astype(BF16)
            wu_bf[...] = wu_ref[...].astype(BF16)

        for mi in range(tm):
            tok = jnp.minimum(rowmap_ref[j * tm + mi], T - 1)
            g_scr[mi * SUBLANES:(mi + 1) * SUBLANES, :] = tr_scr[pl.ds(pl.multiple_of(tok * SUBLANES, SUBLANES), SUBLANES), :]
        x = _tokmajor_to_std(g_scr, tm).astype(BF16)
        g = jnp.dot(x, wg_bf[...], preferred_element_type=F32)
        u = jnp.dot(x, wu_bf[...], preferred_element_type=F32)
        a_ref[...] = (_silu(g) * u * crow_ref[...]).astype(a_ref.dtype)

    @pl.when(j >= meta_ref[MOE_TILES])
    def _():
        a_ref[...] = jnp.zeros(a_ref.shape, a_ref.dtype)


def _moe_down_kernel(rowmap_ref, meta_ref, a_ref, wd_ref, out_hbm, acc, yr, wd_bf):
    j = pl.program_id(0)
    tm = MOE_TM
    zrows = 512

    @pl.when(j == 0)
    def _():
        def zero(i, carry):
            acc[pl.ds(pl.multiple_of(i * zrows, zrows), zrows), :] = jnp.zeros((zrows, LANES), F32)
            return carry
        lax.fori_loop(0, ACC_ROWS // zrows, zero, 0)

    @pl.when(j < meta_ref[MOE_TILES])
    def _():
        @pl.when(_expert_changed(meta_ref, j))
        def _():
            wd_bf[...] = wd_ref[...].astype(BF16)

        y = jnp.dot(a_ref[...], wd_bf[...], preferred_element_type=F32)
        for k in range(D // LANES):
            yr[pl.ds(k, tm, stride=SUBLANES), :] = y[:, k * LANES:(k + 1) * LANES]
        for b in range(tm // SUBLANES):
            offs = [pl.multiple_of(rowmap_ref[j * tm + b * SUBLANES + i] * SUBLANES, SUBLANES) for i in range(SUBLANES)]
            olds = [acc[pl.ds(o, SUBLANES), :] for o in offs]
            for i, o in enumerate(offs):
                r = (b * SUBLANES + i) * SUBLANES
                acc[pl.ds(o, SUBLANES), :] = olds[i] + yr[r:r + SUBLANES, :]

    @pl.when(j == pl.num_programs(0) - 1)
    def _():
        pltpu.sync_copy(acc.at[0:T * SUBLANES, :], out_hbm)


def _moe(tr, route, layer, w_gate, w_up, w_down):
    tm = MOE_TM
    rowmap, meta, crow = _moe_plan(route)

    def wspec(shape):
        return pl.BlockSpec((None, None) + shape, lambda j, rm, mt: (layer, mt[j], 0, 0))

    act = pl.pallas_call(
        _moe_up_kernel,
        grid_spec=pltpu.PrefetchScalarGridSpec(
            num_scalar_prefetch=2, grid=(MOE_TILES,),
            in_specs=[pl.BlockSpec(memory_space=pl.ANY),
                      pl.BlockSpec((tm, 1), lambda j, rm, mt: (j, 0)),
                      wspec((D, D_EXPERT)), wspec((D, D_EXPERT))],
            out_specs=pl.BlockSpec((tm, D_EXPERT), lambda j, rm, mt: (j, 0)),
            scratch_shapes=[pltpu.VMEM((T * SUBLANES, LANES), F32), pltpu.VMEM((tm * SUBLANES, LANES), F32),
                            pltpu.VMEM((D, D_EXPERT), BF16), pltpu.VMEM((D, D_EXPERT), BF16)]),
        out_shape=jax.ShapeDtypeStruct((MOE_ROWS, D_EXPERT), BF16),
        compiler_params=_cparams(1),
        name=f"moe_up{layer}",
    )(rowmap, meta, tr, crow, w_gate, w_up)

    return pl.pallas_call(
        _moe_down_kernel,
        grid_spec=pltpu.PrefetchScalarGridSpec(
            num_scalar_prefetch=2, grid=(MOE_TILES,),
            in_specs=[pl.BlockSpec((tm, D_EXPERT), lambda j, rm, mt: (j, 0)), wspec((D_EXPERT, D))],
            out_specs=pl.BlockSpec(memory_space=pl.ANY),
            scratch_shapes=[pltpu.VMEM((ACC_ROWS, LANES), F32), pltpu.VMEM((tm * SUBLANES, LANES), F32),
                            pltpu.VMEM((D_EXPERT, D), BF16)]),
        out_shape=jax.ShapeDtypeStruct((T * SUBLANES, LANES), F32),
        compiler_params=_cparams(1),
        name=f"moe_down{layer}",
    )(rowmap, meta, act, w_down)


FINAL_TM = 512


def _final_kernel(x_ref, moe_ref, g2_ref, nw_ref, o_ref):
    x = x_ref[...] + g2_ref[0] * _tokmajor_to_std(moe_ref, FINAL_TM)
    o_ref[...] = _rms(x, nw_ref[...])


def _final(x, moe, mods, layer, norm_w):
    tm = FINAL_TM
    tile = pl.BlockSpec((tm, D), lambda i: (i, 0))
    return pl.pallas_call(
        _final_kernel,
        grid=(T // tm,),
        in_specs=[tile, pl.BlockSpec((tm * SUBLANES, LANES), lambda i: (i, 0)), _mod_spec(layer, 5, tm),
                  pl.BlockSpec((1, D), lambda i: (0, 0))],
        out_specs=tile,
        out_shape=jax.ShapeDtypeStruct((T, D), F32),
        compiler_params=_cparams(1),
        name="final_norm",
    )(x, moe, mods, norm_w.reshape(1, D))


def _pad_lanes(v):
    return jnp.pad(v.astype(F32), (0, LANES - v.shape[0])).reshape(1, LANES)


def kernel(x_prompt, x_sample, cache_k_attn, cache_v_attn, state_ssd_fwd, state_ssd_bwd, state_gla_fwd, state_gla_bwd, c, c_ctx, w_ada, b_ada, norm_mix_w, norm_ffn_w, w_in_even, conv_w, conv_b, dt_bias_fwd, dt_bias_bwd, a_log_fwd, a_log_bwd, d_skip, ssd_norm_w, attn_sink, w_out_even, w_in_odd, w_gk2_fwd, b_gk_fwd, w_gk2_bwd, b_gk_bwd, gla_norm_w, w_out_odd, w_router_group, b_router_group, w_router_expert, b_router_expert, w_gate_exp, w_up_exp, w_down_exp, final_norm_w):
    depth = w_ada.shape[0]
    assert depth == 2 and x_prompt.shape == (BATCH, SEQ, D) and x_sample.shape == (DEC_BATCH, DEC_SEQ, D)

    cond8 = jnp.concatenate([c_ctx[None, :], c, jnp.zeros((SUBLANES - 1 - DEC_BATCH, D), F32)], axis=0)
    mods = _adaln(cond8, w_ada, b_ada).reshape(depth * SUBLANES * 6, 1, D)
    x0 = jnp.concatenate([x_prompt.reshape(T_CTX, D), x_sample.reshape(T_LAT, D)], axis=0)

    def router_params(i):
        wr = jnp.concatenate([w_router_group[i], w_router_expert[i],
                              jnp.zeros((D, LANES - N_GROUPS - N_EXPERTS), F32)], axis=1)
        return wr, _pad_lanes(jnp.concatenate([b_router_group[i], b_router_expert[i]]))

    w = w_in_even[0]
    s_z, s_xbc, s_dt, s_q, s_k = 1024, 2560, 2592, 3616, 3872
    w0 = jnp.concatenate([w[:, :s_z], w[:, s_z:s_z + D_SSD], w[:, s_dt:s_q], w[:, s_z + D_SSD:s_xbc],
                          w[:, s_q:s_k], w[:, s_k:], w[:, s_xbc:s_dt],
                          jnp.zeros((D, LANES - 2 * SSD_HEADS), F32)], axis=1).astype(BF16)
    p0 = _modproj(x0, mods, 0, norm_mix_w[0], w0, P0_W // 3)

    dtb = _pad_lanes(jnp.concatenate([dt_bias_fwd[0], dt_bias_bwd[0]]))
    alog = _pad_lanes(jnp.concatenate([a_log_fwd[0], a_log_bwd[0]]))
    dskip = jnp.repeat(d_skip[0], SSD_P).reshape(1, D_SSD)
    ssd_args = (conv_w[0], conv_b[0], dtb, alog, dskip, ssd_norm_w[0])
    y_ssd_c, ssd_f, ssd_b = _ssd(p0, 0, BATCH, SEQ, *ssd_args)
    y_ssd_l, _, _ = _ssd(p0, T_CTX, DEC_BATCH, DEC_SEQ, *ssd_args,
                         h0f=state_ssd_fwd[:, 0].reshape(DEC_BATCH, D_SSD, SSD_N),
                         h0b=state_ssd_bwd[:, 0].reshape(DEC_BATCH, D_SSD, SSD_N))
    sink = _pad_lanes(attn_sink[0])
    y_att_c = _ctx_attn(p0, sink)
    y_att_l = _lat_attn(p0, cache_k_attn[:, 0].reshape(DEC_BATCH, PAST_LEN, ATT_KV_DIM),
                        cache_v_attn[:, 0].reshape(DEC_BATCH, PAST_LEN, ATT_KV_DIM), sink)
    xmid0, tr0, route0 = _outproj_route([y_ssd_c, y_ssd_l, y_att_c, y_att_l], w_out_even[0], x0, mods, 0,
                                        norm_ffn_w[0], *router_params(0))
    moe0 = _moe(tr0, route0, 0, w_gate_exp, w_up_exp, w_down_exp)

    w1 = jnp.concatenate([w_in_odd[0], jnp.zeros((D, P1_W - w_in_odd.shape[2]), F32)], axis=1)
    p1, x1 = _modproj(xmid0, mods, 1, norm_mix_w[1], w1, P1_W // 5, moe=moe0)
    dk_all = GLA_HEADS * GLA_DK
    w2f = jnp.zeros((LANES, dk_all), F32).at[:GLA_LOWRANK].set(w_gk2_fwd[0])
    w2b = jnp.zeros((LANES, dk_all), F32).at[GLA_LOWRANK:2 * GLA_LOWRANK].set(w_gk2_bwd[0])
    gla_args = (w2f, w2b, b_gk_fwd[0], b_gk_bwd[0], gla_norm_w[0])
    y_gla_c, gla_f, gla_b = _gla(p1, 0, BATCH, SEQ, *gla_args)
    y_gla_l, _, _ = _gla(p1, T_CTX, DEC_BATCH, DEC_SEQ, *gla_args,
                         s0f=state_gla_fwd[:, 0].reshape(DEC_BATCH, dk_all, GLA_DV),
                         s0b=state_gla_bwd[:, 0].reshape(DEC_BATCH, dk_all, GLA_DV))
    xmid1, tr1, route1 = _outproj_route([y_gla_c, y_gla_l], w_out_odd[0], x1, mods, 1,
                                        norm_ffn_w[1], *router_params(1))
    moe1 = _moe(tr1, route1, 1, w_gate_exp, w_up_exp, w_down_exp)
    y = _final(xmid1, moe1, mods, 1, final_norm_w)

    y_prompt = y[:T_CTX].reshape(BATCH, SEQ, D)
    y_sample = y[T_CTX:].reshape(DEC_BATCH, DEC_SEQ, D)
    new_k = p0[:T_CTX, P0_K:P0_K + ATT_KV_DIM].reshape(BATCH, 1, SEQ, ATT_KV, ATT_HD)
    new_v = p0[:T_CTX, P0_V:P0_V + ATT_KV_DIM].reshape(BATCH, 1, SEQ, ATT_KV, ATT_HD)
    return (y_prompt, y_sample, new_k, new_v,
            ssd_f.reshape(BATCH, 1, SSD_HEADS, SSD_P, SSD_N), ssd_b.reshape(BATCH, 1, SSD_HEADS, SSD_P, SSD_N),
            gla_f.reshape(BATCH, 1, GLA_HEADS, GLA_DK, GLA_DV), gla_b.reshape(BATCH, 1, GLA_HEADS, GLA_DK, GLA_DV))
```

```python
import functools
import math

import numpy as np
import jax
import jax.numpy as jnp
from jax import lax
from jax.experimental import pallas as pl
from jax.experimental.pallas import tpu as pltpu

F32 = jnp.float32
BF16 = jnp.bfloat16

D = 1024
BATCH, SEQ = 16, 256
DEC_BATCH, DEC_SEQ = 2, 1024
PAST_LEN = 512
GRID_W = 64
EPS = 1e-6
T_CTX = BATCH * SEQ
T_LAT = DEC_BATCH * DEC_SEQ
T = T_CTX + T_LAT

SSD_HEADS, SSD_P, SSD_N, SSD_GROUPS = 16, 64, 128, 2
SSD_CONV = 5
SSD_L = 128
D_SSD = SSD_HEADS * SSD_P
HEADS_PER_GROUP = SSD_HEADS // SSD_GROUPS
GROUP_W = HEADS_PER_GROUP * SSD_P

ATT_HEADS, ATT_KV, ATT_HD = 16, 4, 64
ATT_KV_DIM = ATT_KV * ATT_HD
WINDOW = 128
ATT_BLOCK = 128
ATT_SCALE = ATT_HD ** -0.5
ROPE_BASE = 10000.0

GLA_HEADS, GLA_DK, GLA_DV = 4, 128, 256
GLA_C = 64
GLA_GATE_NORM = 16.0
GLA_LOWRANK = 16

N_GROUPS, EXP_PER_GROUP = 4, 4
N_EXPERTS = 16
D_EXPERT = 512

LANES = 128
SUBLANES = 8
VMEM_LIMIT = 56 * 1024 * 1024

P0_Z, P0_X, P0_Q, P0_BC, P0_K, P0_V = 0, 1024, 2048, 3072, 3584, 3840
P0_W = 4096
P1_Q, P1_K, P1_V, P1_R = 0, 512, 1024, 2048
P1_W = 3072

MOE_TM = 256
MOE_TILES = (2 * T) // MOE_TM + N_EXPERTS
MOE_ROWS = MOE_TILES * MOE_TM
ACC_ROWS = (T + SUBLANES) * SUBLANES


def _cparams(n_axes, vmem=VMEM_LIMIT):
    return pltpu.CompilerParams(dimension_semantics=("arbitrary",) * n_axes, vmem_limit_bytes=vmem)


def _silu(x):
    return x / (1.0 + jnp.exp(-x))


def _softplus(x):
    return jnp.maximum(x, 0.0) + jnp.log(1.0 + jnp.exp(-jnp.abs(x)))


def _mm(a, b):
    return jnp.dot(a.astype(BF16), b.astype(BF16), preferred_element_type=F32)


def _mm_nt(a, b):
    return lax.dot_general(a.astype(BF16), b.astype(BF16), (((1,), (1,)), ((), ())),
                           preferred_element_type=F32)


def _mm_tn(a, b):
    return lax.dot_general(a.astype(BF16), b.astype(BF16), (((0,), (0,)), ((), ())),
                           preferred_element_type=F32)


def _rms(x, w):
    return x * lax.rsqrt(jnp.mean(x * x, axis=-1, keepdims=True) + EPS) * w


def _cumsum_rows(x, n):
    row = lax.broadcasted_iota(jnp.int32, x.shape, 0)
    s = 1
    while s < n:
        x = x + jnp.where(row >= s, pltpu.roll(x, s, 0), 0.0)
        s *= 2
    return x


def _mod_row(tok0):
    return jnp.where(tok0 < T_CTX, 0, 1 + (tok0 - T_CTX) // DEC_SEQ)


ADA_TN = 1536


def _adaln_kernel(c_ref, w_ref, b_ref, o_ref):
    s = _silu(c_ref[...])
    o_ref[0] = _mm(s, w_ref[0]) + b_ref[0]


def _adaln(cond8, w_ada, b_ada):
    depth = w_ada.shape[0]
    return pl.pallas_call(
        _adaln_kernel,
        grid=(depth, 6 * D // ADA_TN),
        in_specs=[
            pl.BlockSpec((SUBLANES, D), lambda l, j: (0, 0)),
            pl.BlockSpec((1, D, ADA_TN), lambda l, j: (l, 0, j)),
            pl.BlockSpec((1, 1, ADA_TN), lambda l, j: (l, 0, j)),
        ],
        out_specs=pl.BlockSpec((1, SUBLANES, ADA_TN), lambda l, j: (l, 0, j)),
        out_shape=jax.ShapeDtypeStruct((depth, SUBLANES, 6 * D), F32),
        compiler_params=_cparams(2),
        name="adaln",
    )(cond8, w_ada, b_ada.reshape(depth, 1, 6 * D))


def _mod_spec(layer, chunk, tm):
    return pl.BlockSpec((1, 1, D), lambda i, *_: ((layer * SUBLANES + _mod_row(i * tm)) * 6 + chunk, 0, 0))


def _tokmajor_to_std(ref, tm):
    return jnp.concatenate([ref[pl.ds(k, tm, stride=SUBLANES), :] for k in range(D // LANES)], axis=1)


PROJ_TM = 1024
PROJ_TN = 512


def _ctx_lat_specs(tm, width=D):
    n_ctx = T_CTX // tm
    return (pl.BlockSpec((tm, width), lambda i, *_: (jnp.minimum(i, n_ctx - 1), 0)),
            pl.BlockSpec((tm, width), lambda i, *_: (jnp.maximum(i - n_ctx, 0), 0)))


def _modproj_kernel(*refs, dual_x, n_main, has_tail):
    it = iter(refs)
    if dual_x:
        xc_ref, xl_ref = next(it), next(it)
    else:
        x_ref, moe_ref, g2_ref = next(it), next(it), next(it)
    sh_ref, sc_ref, nw_ref, wm_ref = next(it), next(it), next(it), next(it)
    wt_ref = next(it) if has_tail else None
    ws_ref, o_ref, os_ref = next(it), next(it), next(it)
    xo_ref = None if dual_x else next(it)
    h_scr = next(it)
    i, j = pl.program_id(0), pl.program_id(1)

    @pl.when(j == 0)
    def _():
        if dual_x:
            x = jnp.where(i < T_CTX // PROJ_TM, xc_ref[...], xl_ref[...])
        else:
            x = x_ref[...] + g2_ref[0] * _tokmajor_to_std(moe_ref, PROJ_TM)
            xo_ref[...] = x
        h = (_rms(x, nw_ref[...]) * (1.0 + sc_ref[0]) + sh_ref[0]).astype(BF16)
        h_scr[...] = h
        os_ref[...] = jnp.dot(h, ws_ref[...].astype(BF16), preferred_element_type=F32)

    def project(w_ref):
        o_ref[...] = jnp.dot(h_scr[...], w_ref[...].astype(BF16), preferred_element_type=F32)

    if has_tail:
        pl.when(j < n_main)(lambda: project(wm_ref))
        pl.when(j >= n_main)(lambda: project(wt_ref))
    else:
        project(wm_ref)


def _modproj(xs, mods, layer, norm_w, w_main, n_main, w_small, out_block, w_tail=None, moe=None):
    tm, tn = PROJ_TM, PROJ_TN
    dual_x = moe is None
    has_tail = w_tail is not None
    n_tiles = n_main + (w_tail.shape[1] // tn if has_tail else 0)
    tile = pl.BlockSpec((tm, D), lambda i, j: (i, 0))
    if dual_x:
        in_specs = list(_ctx_lat_specs(tm))
        args = list(xs)
    else:
        in_specs = [tile, pl.BlockSpec((tm * SUBLANES, LANES), lambda i, j: (i, 0)), _mod_spec(layer - 1, 5, tm)]
        args = [xs, moe, mods]
    in_specs += [_mod_spec(layer, 0, tm), _mod_spec(layer, 1, tm), pl.BlockSpec((1, D), lambda i, j: (0, 0)),
                 pl.BlockSpec((None, D, tn), lambda i, j: (0, 0, jnp.minimum(j, n_main - 1)))]
    args += [mods, mods, norm_w.reshape(1, D), w_main]
    if has_tail:
        in_specs.append(pl.BlockSpec((D, tn), lambda i, j: (0, jnp.maximum(j - n_main, 0))))
        args.append(w_tail)
    in_specs.append(pl.BlockSpec((D, LANES), lambda i, j: (0, 0)))
    args.append(w_small)
    out_specs = [pl.BlockSpec((tm, tn), lambda i, j: (i, out_block(j))), pl.BlockSpec((tm, LANES), lambda i, j: (i, 0))]
    out_shape = [jax.ShapeDtypeStruct((T, n_tiles * tn), F32), jax.ShapeDtypeStruct((T, LANES), F32)]
    if not dual_x:
        out_specs.append(tile)
        out_shape.append(jax.ShapeDtypeStruct((T, D), F32))
    return pl.pallas_call(
        functools.partial(_modproj_kernel, dual_x=dual_x, n_main=n_main, has_tail=has_tail),
        grid=(T // tm, n_tiles), in_specs=in_specs, out_specs=out_specs, out_shape=out_shape,
        scratch_shapes=[pltpu.VMEM((tm, D), BF16)],
        compiler_params=_cparams(2),
        name=f"modproj{layer}",
    )(*args)


def _expand_heads(v, off):
    lo = lax.broadcasted_iota(jnp.int32, (v.shape[0], LANES), 1) < SSD_P
    tiles = []
    for q in range(SSD_HEADS // 2):
        a = jnp.broadcast_to(v[:, off + 2 * q:off + 2 * q + 1], (v.shape[0], LANES))
        b = jnp.broadcast_to(v[:, off + 2 * q + 1:off + 2 * q + 2], (v.shape[0], LANES))
        tiles.append(jnp.where(lo, a, b))
    return jnp.concatenate(tiles, axis=1)


def _ssd_kernel(*refs, seq, has_h0):
    if has_h0:
        (z_ref, x_ref, bc_ref, dt_ref, cwx_ref, cwbc_ref, cbx_ref, cbbc_ref, dtb_ref, alog_ref, dsk_ref,
         nw_ref, h0f_ref, h0b_ref, y_ref, sf_ref, sb_ref,
         xpad, bcpad, xc, bcc, a_scr, dt_scr, yacc, hf, hb) = refs
    else:
        (z_ref, x_ref, bc_ref, dt_ref, cwx_ref, cwbc_ref, cbx_ref, cbbc_ref, dtb_ref, alog_ref, dsk_ref,
         nw_ref, y_ref, sf_ref, sb_ref,
         xpad, bcpad, xc, bcc, a_scr, dt_scr, yacc, hf, hb) = refs
    L = SSD_L
    nc = seq // L
    pad = SUBLANES
    half = SSD_CONV // 2

    for buf, src, cw, cb, dst in ((xpad, x_ref, cwx_ref, cbx_ref, xc), (bcpad, bc_ref, cwbc_ref, cbbc_ref, bcc)):
        width = buf.shape[1]
        buf[0:pad, :] = jnp.zeros((pad, width), F32)
        buf[pad + seq:2 * pad + seq, :] = jnp.zeros((pad, width), F32)
        buf[pad:pad + seq, :] = src[...]
        for blk in range(nc):
            acc = jnp.broadcast_to(cb[...], (L, width))
            for j in range(SSD_CONV):
                r0 = pad - half + j + blk * L
                acc = acc + cw[j:j + 1, :] * buf[r0:r0 + L, :]
            dst[blk * L:(blk + 1) * L, :] = _silu(acc)

    lane = lax.broadcasted_iota(jnp.int32, (seq, LANES), 1)
    dts = jnp.where(lane < 2 * SSD_HEADS, _softplus(dt_ref[...] + dtb_ref[...]), 0.0)
    dt_scr[...] = dts
    a_scr[...] = dts * (-jnp.exp(alog_ref[...]))

    if has_h0:
        hf[...] = h0f_ref[...]
        hb[...] = h0b_ref[...]
    else:
        hf[...] = jnp.zeros(hf.shape, F32)
        hb[...] = jnp.zeros(hb.shape, F32)

    row = lax.broadcasted_iota(jnp.int32, (L, L), 0)
    col = lax.broadcasted_iota(jnp.int32, (L, L), 1)
    lane_l = lax.broadcasted_iota(jnp.int32, (L, LANES), 1)
    lo_half = lane_l < SSD_P

    def chunk(c, fwd, h_scr):
        off = 0 if fwd else SSD_HEADS
        r0 = pl.multiple_of(c * L, L)
        a = a_scr[pl.ds(r0, L), :]
        dt = dt_scr[pl.ds(r0, L), :]
        cs = _cumsum_rows(a, L)
        total = cs[L - 1:L, :]
        if fwd:
            u = cs
            rvec = jnp.exp(cs)
            ed = jnp.exp(total - cs) * dt
            keep = col <= row
        else:
            ex = cs - a
            u = -ex
            rvec = jnp.exp(total - ex)
            ed = jnp.exp(ex) * dt
            keep = col >= row
        ut = jnp.transpose(u)
        dtt = jnp.transpose(dt)
        tcol = jnp.transpose(jnp.broadcast_to(total, (L, LANES)))[:, 0:1]
        rexp = _expand_heads(rvec, off)
        edexp = _expand_heads(ed, off)
        x = xc[pl.ds(r0, L), :]
        bc = bcc[pl.ds(r0, L), :]
        outs = []
        for g in range(SSD_GROUPS):
            bg = bc[:, g * SSD_N:(g + 1) * SSD_N]
            cg = bc[:, SSD_GROUPS * SSD_N + g * SSD_N:SSD_GROUPS * SSD_N + (g + 1) * SSD_N]
            cbm = _mm_nt(cg, bg)
            hg = h_scr[g * GROUP_W:(g + 1) * GROUP_W, :]
            xg = x[:, g * GROUP_W:(g + 1) * GROUP_W]
            y_off = _mm_nt(cg, hg) * rexp[:, g * GROUP_W:(g + 1) * GROUP_W]
            tiles = []
            for p in range(HEADS_PER_GROUP // 2):
                xt = xg[:, p * LANES:(p + 1) * LANES]
                acc = None
                for s in range(2):
                    h = off + g * HEADS_PER_GROUP + 2 * p + s
                    seg = u[:, h:h + 1] - ut[h:h + 1, :]
                    m = cbm * jnp.exp(jnp.where(keep, seg, -jnp.inf)) * dtt[h:h + 1, :]
                    xm = jnp.where(lo_half if s == 0 else jnp.logical_not(lo_half), xt, 0.0)
                    d = _mm(m, xm)
                    acc = d if acc is None else acc + d
                tiles.append(acc)
            outs.append(y_off + jnp.concatenate(tiles, axis=1))
            decs = []
            for hh in range(HEADS_PER_GROUP):
                h = off + g * HEADS_PER_GROUP + hh
                decs.append(jnp.broadcast_to(jnp.exp(tcol[h:h + 1, :]), (SSD_P, SSD_N)))
            dec = jnp.concatenate(decs, axis=0)
            h_scr[g * GROUP_W:(g + 1) * GROUP_W, :] = dec * hg + _mm_tn(xg * edexp[:, g * GROUP_W:(g + 1) * GROUP_W], bg)
        return r0, x, jnp.concatenate(outs, axis=1)

    def fwd_body(c, carry):
        r0, _, y = chunk(c, True, hf)
        yacc[pl.ds(r0, L), :] = y
        return carry

    lax.fori_loop(0, nc, fwd_body, 0)

    def bwd_body(i, carry):
        c = nc - 1 - i
        r0, x, y = chunk(c, False, hb)
        y = yacc[pl.ds(r0, L), :] + y + dsk_ref[...] * x
        y = y * _silu(z_ref[pl.ds(r0, L), :])
        y_ref[pl.ds(r0, L), :] = _rms(y, nw_ref[...]).astype(y_ref.dtype)
        return carry

    lax.fori_loop(0, nc, bwd_body, 0)
    sf_ref[...] = hf[...]
    sb_ref[...] = hb[...]


def _ssd(p0, p0dt, tok0, nseq, seq, cw, cb, dtb, alog, dskip, nw, h0f=None, h0b=None):
    has_h0 = h0f is not None
    b0 = tok0 // seq

    def cols(width, start):
        return pl.BlockSpec((seq, width), lambda s: (b0 + s, start // width))

    def full(shape):
        return pl.BlockSpec(shape, lambda s: (0,) * len(shape))

    in_specs = [cols(D_SSD, P0_Z), cols(D_SSD, P0_X), cols(512, P0_BC), cols(LANES, 0),
                full((SSD_CONV, D_SSD)), full((SSD_CONV, 512)), full((1, D_SSD)), full((1, 512)),
                full((1, LANES)), full((1, LANES)), full((1, D_SSD)), full((1, D_SSD))]
    args = [p0, p0, p0, p0dt, cw[:, :D_SSD], cw[:, D_SSD:], cb[:D_SSD].reshape(1, -1), cb[D_SSD:].reshape(1, -1),
            dtb, alog, dskip, nw.reshape(1, -1)]
    st_spec = pl.BlockSpec((None, D_SSD, SSD_N), lambda s: (s, 0, 0))
    if has_h0:
        in_specs += [st_spec, st_spec]
        args += [h0f, h0b]
    st_shape = jax.ShapeDtypeStruct((nseq, D_SSD, SSD_N), F32)
    return pl.pallas_call(
        functools.partial(_ssd_kernel, seq=seq, has_h0=has_h0),
        grid=(nseq,), in_specs=in_specs,
        out_specs=[pl.BlockSpec((seq, D_SSD), lambda s: (s, 0)), st_spec, st_spec],
        out_shape=[jax.ShapeDtypeStruct((nseq * seq, D_SSD), BF16), st_shape, st_shape],
        scratch_shapes=[pltpu.VMEM((seq + 2 * SUBLANES, D_SSD), F32), pltpu.VMEM((seq + 2 * SUBLANES, 512), F32),
                        pltpu.VMEM((seq, D_SSD), F32), pltpu.VMEM((seq, 512), F32),
                        pltpu.VMEM((seq, LANES), F32), pltpu.VMEM((seq, LANES), F32),
                        pltpu.VMEM((seq, D_SSD), F32),
                        pltpu.VMEM((D_SSD, SSD_N), F32), pltpu.VMEM((D_SSD, SSD_N), F32)],
        compiler_params=_cparams(1),
        name=f"ssd{seq}",
    )(*args)


def _place_halves(tile, kv_in_high):
    lo = lax.broadcasted_iota(jnp.int32, tile.shape, 1) < ATT_HD
    swapped = pltpu.roll(tile, ATT_HD, 1)
    if kv_in_high:
        return jnp.where(lo, swapped, 0.0), jnp.where(lo, 0.0, tile)
    return jnp.where(lo, tile, 0.0), jnp.where(lo, 0.0, swapped)


def _sink_attend(score_parts, value_parts, sink):
    m = sink
    for s in score_parts:
        m = jnp.maximum(m, jnp.max(s, axis=-1, keepdims=True))
    denom = jnp.exp(sink - m)
    out = None
    for s, v in zip(score_parts, value_parts):
        p = jnp.exp(s - m)
        denom = denom + jnp.sum(p, axis=-1, keepdims=True)
        o = _mm(p, v)
        out = o if out is None else out + o
    return out * (1.0 / denom)


def _ctx_attn_kernel(q_ref, k_ref, v_ref, sink_ref, o_ref):
    sink = sink_ref[...]
    for j in range(ATT_KV):
        t, high = j // 2, (j % 2 == 1)
        k_lo, k_hi = _place_halves(k_ref[:, t * LANES:(t + 1) * LANES], high)
        v_lo, v_hi = _place_halves(v_ref[:, t * LANES:(t + 1) * LANES], high)
        for qt in range(2 * j, 2 * j + 2):
            q = q_ref[:, qt * LANES:(qt + 1) * LANES]
            acc = None
            for s, (kk, vv) in enumerate(((k_lo, v_lo), (k_hi, v_hi))):
                h = 2 * qt + s
                sc = _mm_nt(q, kk) * ATT_SCALE
                o = _sink_attend([sc], [vv], sink[:, h:h + 1])
                acc = o if acc is None else acc + o
            o_ref[:, qt * LANES:(qt + 1) * LANES] = acc.astype(o_ref.dtype)


def _ctx_attn(p0, sink):
    def cols(width, start):
        return pl.BlockSpec((SEQ, width), lambda b: (b, start // width))

    return pl.pallas_call(
        _ctx_attn_kernel,
        grid=(BATCH,),
        in_specs=[cols(D, P0_Q), cols(ATT_KV_DIM, P0_K), cols(ATT_KV_DIM, P0_V),
                  pl.BlockSpec((1, LANES), lambda b: (0, 0))],
        out_specs=pl.BlockSpec((SEQ, D), lambda b: (b, 0)),
        out_shape=jax.ShapeDtypeStruct((T_CTX, D), BF16),
        compiler_params=_cparams(1),
        name="ctx_attn",
    )(p0, p0, p0, sink)


def _rope_tables():
    quarter = ATT_HD // 4
    t = np.arange(DEC_SEQ)
    lane = np.arange(LANES)
    inv = ROPE_BASE ** (-(lane % quarter).astype(np.float64) / quarter)
    pos = np.where(((lane % ATT_HD) < ATT_HD // 2)[None, :], (t // GRID_W)[:, None], (t % GRID_W)[:, None])
    ang = pos * inv[None, :]
    first = ((lane % (2 * quarter)) < quarter)[None, :]
    cos, sin = np.cos(ang), np.sin(ang)
    return (jnp.asarray(cos, F32), jnp.asarray(np.where(first, -sin, 0.0), F32),
            jnp.asarray(np.where(first, 0.0, sin), F32))


def _rope(x, cos, sa, sb):
    quarter = ATT_HD // 4
    return x * cos + pltpu.roll(x, LANES - quarter, 1) * sa + pltpu.roll(x, quarter, 1) * sb


def _lat_attn_kernel(q_ref, kp_ref, kc_ref, kn_ref, vp_ref, vc_ref, vn_ref, ck_ref, cv_ref,
                     cos_ref, sa_ref, sb_ref, sink_ref, o_ref):
    blk = pl.program_id(1)
    nb = pl.num_programs(1)
    B = ATT_BLOCK
    sink = sink_ref[...]

    def tables(b):
        r0 = pl.multiple_of(b * B, B)
        return cos_ref[pl.ds(r0, B), :], sa_ref[pl.ds(r0, B), :], sb_ref[pl.ds(r0, B), :]

    tq = tables(blk)
    tk = [tables(jnp.maximum(blk - 1, 0)), tq, tables(jnp.minimum(blk + 1, nb - 1))]
    qpos = blk * B + lax.broadcasted_iota(jnp.int32, (B, 3 * B), 0)
    kabs = (blk - 1) * B + lax.broadcasted_iota(jnp.int32, (B, 3 * B), 1)
    ok = (jnp.abs(qpos - kabs) <= WINDOW) & (kabs >= 0) & (kabs < nb * B)

    for j in range(ATT_KV):
        t, high = j // 2, (j % 2 == 1)
        sl = slice(t * LANES, (t + 1) * LANES)
        kw = jnp.concatenate([_rope(r[:, sl], *tb) for r, tb in zip((kp_ref, kc_ref, kn_ref), tk)], axis=0)
        vw = jnp.concatenate([r[:, sl] for r in (vp_ref, vc_ref, vn_ref)], axis=0)
        k_lo, k_hi = _place_halves(kw, high)
        v_lo, v_hi = _place_halves(vw, high)
        ck_lo, ck_hi = _place_halves(ck_ref[:, sl], high)
        cv_lo, cv_hi = _place_halves(cv_ref[:, sl], high)
        for qt in range(2 * j, 2 * j + 2):
            q = q_ref[:, qt * LANES:(qt + 1) * LANES]
            qr = _rope(q, *tq)
            acc = None
            for s, (kk, vv, ckk, cvv) in enumerate(((k_lo, v_lo, ck_lo, cv_lo), (k_hi, v_hi, ck_hi, cv_hi))):
                h = 2 * qt + s
                s_ctx = _mm_nt(q, ckk) * ATT_SCALE
                s_win = jnp.where(ok, _mm_nt(qr, kk) * ATT_SCALE, -jnp.inf)
                o = _sink_attend([s_ctx, s_win], [cvv, vv], sink[:, h:h + 1])
                acc = o if acc is None else acc + o
            o_ref[:, qt * LANES:(qt + 1) * LANES] = acc.astype(o_ref.dtype)


def _lat_attn(p0, ck, cv, sink):
    nb = DEC_SEQ // ATT_BLOCK
    base = T_CTX // ATT_BLOCK

    def kv(start, shift):
        return pl.BlockSpec((ATT_BLOCK, ATT_KV_DIM),
                            lambda b, i: (base + b * nb + jnp.clip(i + shift, 0, nb - 1), start // ATT_KV_DIM))

    def full(shape):
        return pl.BlockSpec(shape, lambda b, i: (0,) * len(shape))

    cache = pl.BlockSpec((None, PAST_LEN, ATT_KV_DIM), lambda b, i: (b, 0, 0))
    cos, sa, sb = _rope_tables()
    return pl.pallas_call(
        _lat_attn_kernel,
        grid=(DEC_BATCH, nb),
        in_specs=[pl.BlockSpec((ATT_BLOCK, D), lambda b, i: (base + b * nb + i, P0_Q // D)),
                  kv(P0_K, -1), kv(P0_K, 0), kv(P0_K, 1), kv(P0_V, -1), kv(P0_V, 0), kv(P0_V, 1),
                  cache, cache, full((DEC_SEQ, LANES)), full((DEC_SEQ, LANES)), full((DEC_SEQ, LANES)),
                  full((1, LANES))],
        out_specs=pl.BlockSpec((ATT_BLOCK, D), lambda b, i: (b * nb + i, 0)),
        out_shape=jax.ShapeDtypeStruct((T_LAT, D), BF16),
        compiler_params=_cparams(2),
        name="lat_attn",
    )(p0, p0, p0, p0, p0, p0, p0, ck, cv, cos, sa, sb, sink)


def _log_sigmoid(x):
    return jnp.minimum(x, 0.0) - jnp.log(1.0 + jnp.exp(-jnp.abs(x)))


def _gla_kernel(*refs, seq, has_s0):
    if has_s0:
        (q_ref, k_ref, v_ref, r_ref, lr_ref, w2f_ref, w2b_ref, bf_ref, bb_ref, nw_ref, s0f_ref, s0b_ref,
         y_ref, sf_ref, sb_ref, gf, gb, yf, yb, stf, stb) = refs
    else:
        (q_ref, k_ref, v_ref, r_ref, lr_ref, w2f_ref, w2b_ref, bf_ref, bb_ref, nw_ref,
         y_ref, sf_ref, sb_ref, gf, gb, yf, yb, stf, stb) = refs
    C = GLA_C
    nc = seq // C
    lr = lr_ref[...]
    gf[...] = _log_sigmoid(_mm(lr, w2f_ref[...]) + bf_ref[...]) / GLA_GATE_NORM
    gb[...] = _log_sigmoid(_mm(lr, w2b_ref[...]) + bb_ref[...]) / GLA_GATE_NORM
    for h in range(GLA_HEADS):
        rows = slice(h * GLA_DV, (h + 1) * GLA_DV)
        if has_s0:
            stf[rows, :] = jnp.transpose(s0f_ref[h * GLA_DK:(h + 1) * GLA_DK, :])
            stb[rows, :] = jnp.transpose(s0b_ref[h * GLA_DK:(h + 1) * GLA_DK, :])
        else:
            stf[rows, :] = jnp.zeros((GLA_DV, GLA_DK), F32)
            stb[rows, :] = jnp.zeros((GLA_DV, GLA_DK), F32)

    row = lax.broadcasted_iota(jnp.int32, (C, C), 0)
    col = lax.broadcasted_iota(jnp.int32, (C, C), 1)
    qscale = GLA_DK ** -0.5

    def chunk(c, fwd):
        g_scr, y_scr, st = (gf, yf, stf) if fwd else (gb, yb, stb)
        r0 = pl.multiple_of(c * C, C)
        g = g_scr[pl.ds(r0, C), :]
        cs = _cumsum_rows(g, C)
        total = cs[C - 1:C, :]
        q = q_ref[pl.ds(r0, C), :] * qscale
        k = k_ref[pl.ds(r0, C), :]
        v = v_ref[pl.ds(r0, C), :]
        if fwd:
            qs, ks, ke = q * jnp.exp(cs), k * jnp.exp(-cs), k * jnp.exp(total - cs)
            keep = col <= row
        else:
            ex = cs - g
            qs, ks, ke = q * jnp.exp(total - ex), k * jnp.exp(ex - total), k * jnp.exp(ex)
            keep = col >= row
        dec = jnp.exp(total)
        for h in range(GLA_HEADS):
            kc = slice(h * GLA_DK, (h + 1) * GLA_DK)
            vc = slice(h * GLA_DV, (h + 1) * GLA_DV)
            s_t = st[vc, :]
            att = jnp.where(keep, _mm_nt(qs[:, kc], ks[:, kc]), 0.0)
            y_scr[pl.ds(r0, C), vc] = _mm(att, v[:, vc]) + _mm_nt(qs[:, kc], s_t)
            st[vc, :] = dec[:, kc] * s_t + _mm_tn(v[:, vc], ke[:, kc])

    def body(i, carry):
        chunk(i, True)
        chunk(nc - 1 - i, False)
        return carry

    lax.fori_loop(0, nc, body, 0)

    nw = nw_ref[...]
    for blk in range(seq // LANES):
        rs = slice(blk * LANES, (blk + 1) * LANES)
        y = yf[rs, :] + yb[rs, :]
        gate = _silu(r_ref[rs, :])
        for h in range(GLA_HEADS):
            vc = slice(h * GLA_DV, (h + 1) * GLA_DV)
            y_ref[rs, vc] = (_rms(y[:, vc], nw) * gate[:, vc]).astype(y_ref.dtype)
    for h in range(GLA_HEADS):
        rows = slice(h * GLA_DV, (h + 1) * GLA_DV)
        sf_ref[h * GLA_DK:(h + 1) * GLA_DK, :] = jnp.transpose(stf[rows, :])
        sb_ref[h * GLA_DK:(h + 1) * GLA_DK, :] = jnp.transpose(stb[rows, :])


def _gla(p1, p1lr, tok0, nseq, seq, w2f, w2b, bgf, bgb, nw, s0f=None, s0b=None):
    has_s0 = s0f is not None
    b0 = tok0 // seq
    dk_all, dv_all = GLA_HEADS * GLA_DK, GLA_HEADS * GLA_DV

    def cols(width, start):
        return pl.BlockSpec((seq, width), lambda s: (b0 + s, start // width))

    def full(shape):
        return pl.BlockSpec(shape, lambda s: (0,) * len(shape))

    in_specs = [cols(dk_all, P1_Q), cols(dk_all, P1_K), cols(dv_all, P1_V), cols(dv_all, P1_R), cols(LANES, 0),
                full((LANES, dk_all)), full((LANES, dk_all)), full((1, dk_all)), full((1, dk_all)), full((1, GLA_DV))]
    args = [p1, p1, p1, p1, p1lr, w2f, w2b, bgf.reshape(1, -1), bgb.reshape(1, -1), nw.reshape(1, -1)]
    st_spec = pl.BlockSpec((None, dk_all, GLA_DV), lambda s: (s, 0, 0))
    if has_s0:
        in_specs += [st_spec, st_spec]
        args += [s0f, s0b]
    st_shape = jax.ShapeDtypeStruct((nseq, dk_all, GLA_DV), F32)
    return pl.pallas_call(
        functools.partial(_gla_kernel, seq=seq, has_s0=has_s0),
        grid=(nseq,), in_specs=in_specs,
        out_specs=[pl.BlockSpec((seq, dv_all), lambda s: (s, 0)), st_spec, st_spec],
        out_shape=[jax.ShapeDtypeStruct((nseq * seq, dv_all), BF16), st_shape, st_shape],
        scratch_shapes=[pltpu.VMEM((seq, dk_all), F32), pltpu.VMEM((seq, dk_all), F32),
                        pltpu.VMEM((seq, dv_all), F32), pltpu.VMEM((seq, dv_all), F32),
                        pltpu.VMEM((dv_all, GLA_DK), F32), pltpu.VMEM((dv_all, GLA_DK), F32)],
        compiler_params=_cparams(1),
        name=f"gla{seq}",
    )(*args)


ROUTE_TM = 256


def _split_bf16(x):
    hi = x.astype(BF16)
    return hi, (x - hi.astype(F32)).astype(BF16)


def _outproj_kernel(*refs, n_in, dual_x):
    y_refs = refs[:2 * n_in]
    n_x = 2 if dual_x else 1
    x_refs = refs[2 * n_in + 1:2 * n_in + 1 + n_x]
    w_ref = refs[2 * n_in]
    (g1_ref, sh_ref, sc_ref, nw_ref, wr_ref, br_ref,
     xo_ref, tr_ref, route_ref, cnt_ref, w_scr, wr_hi, wr_lo, carry) = refs[2 * n_in + 1 + n_x:]
    tm = ROUTE_TM
    is_ctx = pl.program_id(0) < T_CTX // tm

    @pl.when(pl.program_id(0) == 0)
    def _():
        w_scr[...] = w_ref[...].astype(BF16)
        hi, lo = _split_bf16(wr_ref[...])
        wr_hi[...] = hi
        wr_lo[...] = lo
        carry[...] = jnp.zeros(carry.shape, F32)

    o = None
    for i in range(n_in):
        y = jnp.where(is_ctx, y_refs[2 * i][...], y_refs[2 * i + 1][...])
        d = jnp.dot(y, w_scr[i * D:(i + 1) * D, :], preferred_element_type=F32)
        o = d if o is None else o + d
    x_in = jnp.where(is_ctx, x_refs[0][...], x_refs[1][...]) if dual_x else x_refs[0][...]
    x = x_in + g1_ref[0] * o
    xo_ref[...] = x
    t = _rms(x, nw_ref[...]) * (1.0 + sc_ref[0]) + sh_ref[0]
    for k in range(D // LANES):
        tr_ref[pl.ds(k, tm, stride=SUBLANES), :] = t[:, k * LANES:(k + 1) * LANES]

    t_hi, t_lo = _split_bf16(t)
    logit = (jnp.dot(t_hi, wr_hi[...], preferred_element_type=F32)
             + jnp.dot(t_lo, wr_hi[...], preferred_element_type=F32)
             + jnp.dot(t_hi, wr_lo[...], preferred_element_type=F32)) + br_ref[...]
    lane = lax.broadcasted_iota(jnp.int32, (tm, LANES), 1)
    lanef = lane.astype(F32)
    neg = -jnp.inf
    gl = jnp.where(lane < N_GROUPS, logit, neg)
    gmax = jnp.max(gl, axis=-1, keepdims=True)
    gsel = jnp.min(jnp.where(gl == gmax, lanef, float(LANES)), axis=-1, keepdims=True)
    gprob = 1.0 / jnp.sum(jnp.exp(gl - gmax), axis=-1, keepdims=True)
    first = N_GROUPS + EXP_PER_GROUP * gsel
    el = jnp.where((lanef >= first) & (lanef < first + EXP_PER_GROUP), logit, neg)
    m1 = jnp.max(el, axis=-1, keepdims=True)
    i1 = jnp.min(jnp.where(el == m1, lanef, float(LANES)), axis=-1, keepdims=True)
    el2 = jnp.where(lanef == i1, neg, el)
    m2 = jnp.max(el2, axis=-1, keepdims=True)
    i2 = jnp.min(jnp.where(el2 == m2, lanef, float(LANES)), axis=-1, keepdims=True)
    e2 = jnp.exp(m2 - m1)
    c1 = gprob / (1.0 + e2)
    c2 = gprob * e2 / (1.0 + e2)
    x1 = i1 - N_GROUPS
    x2 = i2 - N_GROUPS

    hot = ((lanef == x1) | (lanef == x2)).astype(F32)
    tri = (lax.broadcasted_iota(jnp.int32, (tm, tm), 1) < lax.broadcasted_iota(jnp.int32, (tm, tm), 0))
    before = _mm(tri.astype(F32), hot) + carry[0:1, :]
    r1 = jnp.sum(jnp.where(lanef == x1, before, 0.0), axis=-1, keepdims=True)
    r2 = jnp.sum(jnp.where(lanef == x2, before, 0.0), axis=-1, keepdims=True)
    total = carry[...] + jnp.sum(hot, axis=0, keepdims=True)
    carry[...] = total
    cnt_ref[...] = total
    vals = (x1, x2, c1, c2, r1, r2)
    out = jnp.zeros((tm, LANES), F32)
    for k, v in enumerate(vals):
        out = jnp.where(lane == k, v, out)
    route_ref[...] = out


def _outproj_route(ys, w_out, xs, mods, layer, norm_w, w_router, b_router):
    tm = ROUTE_TM
    n_in = len(ys) // 2
    dual_x = len(xs) == 2
    kdim = w_out.shape[0]

    def full(shape):
        return pl.BlockSpec(shape, lambda i: (0,) * len(shape))

    tile = pl.BlockSpec((tm, D), lambda i: (i, 0))
    pair = list(_ctx_lat_specs(tm))
    in_specs = (pair * n_in + [full((kdim, D))] + (pair if dual_x else [tile])
                + [_mod_spec(layer, 2, tm), _mod_spec(layer, 3, tm), _mod_spec(layer, 4, tm),
                   full((1, D)), full((D, LANES)), full((1, LANES))])
    return pl.pallas_call(
        functools.partial(_outproj_kernel, n_in=n_in, dual_x=dual_x),
        grid=(T // tm,), in_specs=in_specs,
        out_specs=[tile, pl.BlockSpec((tm * SUBLANES, LANES), lambda i: (i, 0)),
                   pl.BlockSpec((tm, LANES), lambda i: (i, 0)), full((SUBLANES, LANES))],
        out_shape=[jax.ShapeDtypeStruct((T, D), F32), jax.ShapeDtypeStruct((T * SUBLANES, LANES), F32),
                   jax.ShapeDtypeStruct((T, LANES), F32), jax.ShapeDtypeStruct((SUBLANES, LANES), F32)],
        scratch_shapes=[pltpu.VMEM((kdim, D), BF16), pltpu.VMEM((D, LANES), BF16), pltpu.VMEM((D, LANES), BF16),
                        pltpu.VMEM((SUBLANES, LANES), F32)],
        compiler_params=_cparams(1),
        name=f"outproj{layer}",
    )(*ys, w_out, *xs, mods, mods, mods, norm_w.reshape(1, D), w_router, b_router)


def _moe_meta(counts):
    tm = MOE_TM
    experts = jnp.arange(N_EXPERTS, dtype=jnp.int32)
    counts = counts[0, :N_EXPERTS].astype(jnp.int32)
    padded = ((counts + tm - 1) // tm) * tm
    ends = jnp.cumsum(padded)
    tile_start = jnp.arange(MOE_TILES, dtype=jnp.int32) * tm
    te = jnp.sum((tile_start[:, None] >= ends[None, :]).astype(jnp.int32), axis=1)
    last = jnp.max(jnp.where(counts > 0, experts, 0))
    meta = jnp.concatenate([jnp.minimum(te, last), ends[-1:] // tm]).astype(jnp.int32)
    return (ends - padded).astype(jnp.int32), meta


def _expert_changed(meta_ref, j):
    return (j == 0) | (meta_ref[j] != meta_ref[jnp.maximum(j - 1, 0)])


def _moe_up_kernel(e1_ref, e2_ref, r1_ref, r2_ref, c1_ref, c2_ref, starts_ref, meta_ref,
                   tr_hbm, wg_ref, wu_ref, a_ref, rowmap_ref, crow_ref, tr_scr, g_scr, wg_bf, wu_bf, sem):
    j = pl.program_id(0)
    tm = MOE_TM

    @pl.when(j == 0)
    def _():
        load = pltpu.make_async_copy(tr_hbm, tr_scr, sem)
        load.start()

        def clear(r, carry):
            rowmap_ref[r] = T
            crow_ref[r] = 0.0
            return carry
        lax.fori_loop(0, MOE_ROWS, clear, 0, unroll=8)

        def place(t, carry):
            p1 = starts_ref[e1_ref[t]] + r1_ref[t]
            rowmap_ref[p1] = t
            crow_ref[p1] = c1_ref[t]
            p2 = starts_ref[e2_ref[t]] + r2_ref[t]
            rowmap_ref[p2] = t
            crow_ref[p2] = c2_ref[t]
            return carry
        lax.fori_loop(0, T, place, 0, unroll=4)
        load.wait()

    @pl.when(j < meta_ref[MOE_TILES])
    def _():
        @pl.when(_expert_changed(meta_ref, j))
        def _():
            wg_bf[...] = wg_ref[...].astype(BF16)
            wu_bf[...] = wu_ref[...].astype(BF16)

        for mi in range(tm):
            tok = jnp.minimum(rowmap_ref[j * tm + mi], T - 1)
            g_scr[mi * SUBLANES:(mi + 1) * SUBLANES, :] = tr_scr[pl.ds(pl.multiple_of(tok * SUBLANES, SUBLANES), SUBLANES), :]
        x = _tokmajor_to_std(g_scr, tm).astype(BF16)
        g = jnp.dot(x, wg_bf[...], preferred_element_type=F32)
        u = jnp.dot(x, wu_bf[...], preferred_element_type=F32)
        a_ref[...] = (_silu(g) * u).astype(a_ref.dtype)

    @pl.when(j >= meta_ref[MOE_TILES])
    def _():
        a_ref[...] = jnp.zeros(a_ref.shape, a_ref.dtype)


def _moe_down_kernel(rowmap_ref, crow_ref, meta_ref, a_ref, wd_ref, out_hbm, acc, yr, wd_bf):
    j = pl.program_id(0)
    tm = MOE_TM
    zrows = 512

    @pl.when(j == 0)
    def _():
        def zero(i, carry):
            acc[pl.ds(pl.multiple_of(i * zrows, zrows), zrows), :] = jnp.zeros((zrows, LANES), F32)
            return carry
        lax.fori_loop(0, ACC_ROWS // zrows, zero, 0)

    @pl.when(j < meta_ref[MOE_TILES])
    def _():
        @pl.when(_expert_changed(meta_ref, j))
        def _():
            wd_bf[...] = wd_ref[...].astype(BF16)

        y = jnp.dot(a_ref[...], wd_bf[...], preferred_element_type=F32)
        for k in range(D // LANES):
            yr[pl.ds(k, tm, stride=SUBLANES), :] = y[:, k * LANES:(k + 1) * LANES]
        for b in range(tm // SUBLANES):
            rows = [j * tm + b * SUBLANES + i for i in range(SUBLANES)]
            offs = [pl.multiple_of(rowmap_ref[r] * SUBLANES, SUBLANES) for r in rows]
            olds = [acc[pl.ds(o, SUBLANES), :] for o in offs]
            for i, o in enumerate(offs):
                r = (b * SUBLANES + i) * SUBLANES
                acc[pl.ds(o, SUBLANES), :] = olds[i] + crow_ref[rows[i]] * yr[r:r + SUBLANES, :]

    @pl.when(j == pl.num_programs(0) - 1)
    def _():
        pltpu.sync_copy(acc.at[0:T * SUBLANES, :], out_hbm)


def _moe(tr, route, counts, layer, w_gate, w_up, w_down):
    tm = MOE_TM
    starts, meta = _moe_meta(counts)
    ri = route[:, :LANES // 2].astype(jnp.int32)
    e1, e2, r1, r2 = ri[:, 0], ri[:, 1], ri[:, 4], ri[:, 5]
    c1, c2 = route[:, 2], route[:, 3]
    n_pre = 8

    def wspec(shape, n):
        return pl.BlockSpec((None, None) + shape, lambda j, *pre: (layer, pre[n - 1][j], 0, 0))

    smem = pl.BlockSpec(memory_space=pltpu.SMEM)
    act, rowmap, crow = pl.pallas_call(
        _moe_up_kernel,
        grid_spec=pltpu.PrefetchScalarGridSpec(
            num_scalar_prefetch=n_pre, grid=(MOE_TILES,),
            in_specs=[pl.BlockSpec(memory_space=pl.ANY), wspec((D, D_EXPERT), n_pre), wspec((D, D_EXPERT), n_pre)],
            out_specs=[pl.BlockSpec((tm, D_EXPERT), lambda j, *pre: (j, 0)), smem, smem],
            scratch_shapes=[pltpu.VMEM((T * SUBLANES, LANES), F32), pltpu.VMEM((tm * SUBLANES, LANES), F32),
                            pltpu.VMEM((D, D_EXPERT), BF16), pltpu.VMEM((D, D_EXPERT), BF16),
                            pltpu.SemaphoreType.DMA(())]),
        out_shape=[jax.ShapeDtypeStruct((MOE_ROWS, D_EXPERT), BF16), jax.ShapeDtypeStruct((MOE_ROWS,), jnp.int32),
                   jax.ShapeDtypeStruct((MOE_ROWS,), F32)],
        compiler_params=_cparams(1),
        name=f"moe_up{layer}",
    )(e1, e2, r1, r2, c1, c2, starts, meta, tr, w_gate, w_up)

    return pl.pallas_call(
        _moe_down_kernel,
        grid_spec=pltpu.PrefetchScalarGridSpec(
            num_scalar_prefetch=3, grid=(MOE_TILES,),
            in_specs=[pl.BlockSpec((tm, D_EXPERT), lambda j, *pre: (j, 0)), wspec((D_EXPERT, D), 3)],
            out_specs=pl.BlockSpec(memory_space=pl.ANY),
            scratch_shapes=[pltpu.VMEM((ACC_ROWS, LANES), F32), pltpu.VMEM((tm * SUBLANES, LANES), F32),
                            pltpu.VMEM((D_EXPERT, D), BF16)]),
        out_shape=jax.ShapeDtypeStruct((T * SUBLANES, LANES), F32),
        compiler_params=_cparams(1),
        name=f"moe_down{layer}",
    )(rowmap, crow, meta, act, w_down)


FINAL_TM = 512


def _final_kernel(x_ref, moe_ref, g2_ref, nw_ref, oc_ref, ol_ref):
    x = x_ref[...] + g2_ref[0] * _tokmajor_to_std(moe_ref, FINAL_TM)
    y = _rms(x, nw_ref[...])
    is_ctx = pl.program_id(0) < T_CTX // FINAL_TM

    @pl.when(is_ctx)
    def _():
        oc_ref[...] = y

    @pl.when(jnp.logical_not(is_ctx))
    def _():
        ol_ref[...] = y


def _final(x, moe, mods, layer, norm_w):
    tm = FINAL_TM
    tile = pl.BlockSpec((tm, D), lambda i: (i, 0))
    return pl.pallas_call(
        _final_kernel,
        grid=(T // tm,),
        in_specs=[tile, pl.BlockSpec((tm * SUBLANES, LANES), lambda i: (i, 0)), _mod_spec(layer, 5, tm),
                  pl.BlockSpec((1, D), lambda i: (0, 0))],
        out_specs=list(_ctx_lat_specs(tm)),
        out_shape=[jax.ShapeDtypeStruct((T_CTX, D), F32), jax.ShapeDtypeStruct((T_LAT, D), F32)],
        compiler_params=_cparams(1),
        name="final_norm",
    )(x, moe, mods, norm_w.reshape(1, D))


def _pad_lanes(v):
    return jnp.pad(v.astype(F32), (0, LANES - v.shape[0])).reshape(1, LANES)


def kernel(x_prompt, x_sample, cache_k_attn, cache_v_attn, state_ssd_fwd, state_ssd_bwd, state_gla_fwd, state_gla_bwd, c, c_ctx, w_ada, b_ada, norm_mix_w, norm_ffn_w, w_in_even, conv_w, conv_b, dt_bias_fwd, dt_bias_bwd, a_log_fwd, a_log_bwd, d_skip, ssd_norm_w, attn_sink, w_out_even, w_in_odd, w_gk2_fwd, b_gk_fwd, w_gk2_bwd, b_gk_bwd, gla_norm_w, w_out_odd, w_router_group, b_router_group, w_router_expert, b_router_expert, w_gate_exp, w_up_exp, w_down_exp, final_norm_w):
    depth = w_ada.shape[0]
    assert depth == 2 and x_prompt.shape == (BATCH, SEQ, D) and x_sample.shape == (DEC_BATCH, DEC_SEQ, D)

    cond8 = jnp.concatenate([c_ctx[None, :], c, jnp.zeros((SUBLANES - 1 - DEC_BATCH, D), F32)], axis=0)
    mods = _adaln(cond8, w_ada, b_ada).reshape(depth * SUBLANES * 6, 1, D)
    xs0 = (x_prompt.reshape(T_CTX, D), x_sample.reshape(T_LAT, D))

    def router_params(i):
        wr = jnp.concatenate([w_router_group[i], w_router_expert[i],
                              jnp.zeros((D, LANES - N_GROUPS - N_EXPERTS), F32)], axis=1)
        return wr, _pad_lanes(jnp.concatenate([b_router_group[i], b_router_expert[i]]))

    def pad_cols(w):
        return jnp.concatenate([w, jnp.zeros((D, LANES - w.shape[1]), F32)], axis=1)

    w = w_in_even[0]
    n_zxbc = 2 * D_SSD + 2 * SSD_GROUPS * SSD_N
    n_dt = 2 * SSD_HEADS
    n_main = n_zxbc // PROJ_TN
    bc_tile, q_tiles = n_main - 1, D // PROJ_TN

    def out_block0(j):
        return jnp.where(j == bc_tile, bc_tile + q_tiles, jnp.where((j > bc_tile) & (j <= bc_tile + q_tiles), j - 1, j))

    p0, p0dt = _modproj(xs0, mods, 0, norm_mix_w[0], w_in_even, n_main, pad_cols(w[:, n_zxbc:n_zxbc + n_dt]),
                        out_block0, w_tail=w[:, n_zxbc + n_dt:])

    dtb = _pad_lanes(jnp.concatenate([dt_bias_fwd[0], dt_bias_bwd[0]]))
    alog = _pad_lanes(jnp.concatenate([a_log_fwd[0], a_log_bwd[0]]))
    dskip = jnp.repeat(d_skip[0], SSD_P).reshape(1, D_SSD)
    ssd_args = (conv_w[0], conv_b[0], dtb, alog, dskip, ssd_norm_w[0])
    y_ssd_c, ssd_f, ssd_b = _ssd(p0, p0dt, 0, BATCH, SEQ, *ssd_args)
    y_ssd_l, _, _ = _ssd(p0, p0dt, T_CTX, DEC_BATCH, DEC_SEQ, *ssd_args,
                         h0f=state_ssd_fwd[:, 0].reshape(DEC_BATCH, D_SSD, SSD_N),
                         h0b=state_ssd_bwd[:, 0].reshape(DEC_BATCH, D_SSD, SSD_N))
    sink = _pad_lanes(attn_sink[0])
    y_att_c = _ctx_attn(p0, sink)
    y_att_l = _lat_attn(p0, cache_k_attn[:, 0].reshape(DEC_BATCH, PAST_LEN, ATT_KV_DIM),
                        cache_v_attn[:, 0].reshape(DEC_BATCH, PAST_LEN, ATT_KV_DIM), sink)
    xmid0, tr0, route0, cnt0 = _outproj_route([y_ssd_c, y_ssd_l, y_att_c, y_att_l], w_out_even[0], xs0, mods, 0,
                                              norm_ffn_w[0], *router_params(0))
    moe0 = _moe(tr0, route0, cnt0, 0, w_gate_exp, w_up_exp, w_down_exp)

    dk_all = GLA_HEADS * GLA_DK
    n_qkvr = 2 * dk_all + 2 * GLA_HEADS * GLA_DV
    p1, p1lr, x1 = _modproj(xmid0, mods, 1, norm_mix_w[1], w_in_odd, n_qkvr // PROJ_TN,
                            pad_cols(w_in_odd[0][:, n_qkvr:]), lambda j: j, moe=moe0)
    w2f = jnp.zeros((LANES, dk_all), F32).at[:GLA_LOWRANK].set(w_gk2_fwd[0])
    w2b = jnp.zeros((LANES, dk_all), F32).at[GLA_LOWRANK:2 * GLA_LOWRANK].set(w_gk2_bwd[0])
    gla_args = (w2f, w2b, b_gk_fwd[0], b_gk_bwd[0], gla_norm_w[0])
    y_gla_c, gla_f, gla_b = _gla(p1, p1lr, 0, BATCH, SEQ, *gla_args)
    y_gla_l, _, _ = _gla(p1, p1lr, T_CTX, DEC_BATCH, DEC_SEQ, *gla_args,
                         s0f=state_gla_fwd[:, 0].reshape(DEC_BATCH, dk_all, GLA_DV),
                         s0b=state_gla_bwd[:, 0].reshape(DEC_BATCH, dk_all, GLA_DV))
    xmid1, tr1, route1, cnt1 = _outproj_route([y_gla_c, y_gla_l], w_out_odd[0], (x1,), mods, 1,
                                              norm_ffn_w[1], *router_params(1))
    moe1 = _moe(tr1, route1, cnt1, 1, w_gate_exp, w_up_exp, w_down_exp)
    y_c, y_l = _final(xmid1, moe1, mods, 1, final_norm_w)

    y_prompt = y_c.reshape(BATCH, SEQ, D)
    y_sample = y_l.reshape(DEC_BATCH, DEC_SEQ, D)
    new_k = p0[:T_CTX, P0_K:P0_K + ATT_KV_DIM].reshape(BATCH, 1, SEQ, ATT_KV, ATT_HD)
    new_v = p0[:T_CTX, P0_V:P0_V + ATT_KV_DIM].reshape(BATCH, 1, SEQ, ATT_KV, ATT_HD)
    return (y_prompt, y_sample, new_k, new_v,
            ssd_f.reshape(BATCH, 1, SSD_HEADS, SSD_P, SSD_N), ssd_b.reshape(BATCH, 1, SSD_HEADS, SSD_P, SSD_N),
            gla_f.reshape(BATCH, 1, GLA_HEADS, GLA_DK, GLA_DV), gla_b.reshape(BATCH, 1, GLA_HEADS, GLA_DK, GLA_DV))
```

```python
import functools
import math

import numpy as np
import jax
import jax.numpy as jnp
from jax import lax
from jax.experimental import pallas as pl
from jax.experimental.pallas import tpu as pltpu

F32 = jnp.float32
BF16 = jnp.bfloat16

D = 1024
BATCH, SEQ = 16, 256
DEC_BATCH, DEC_SEQ = 2, 1024
PAST_LEN = 512
GRID_W = 64
EPS = 1e-6
T_CTX = BATCH * SEQ
T_LAT = DEC_BATCH * DEC_SEQ
T = T_CTX + T_LAT

SSD_HEADS, SSD_P, SSD_N, SSD_GROUPS = 16, 64, 128, 2
SSD_CONV = 5
SSD_L = 128
D_SSD = SSD_HEADS * SSD_P
HEADS_PER_GROUP = SSD_HEADS // SSD_GROUPS
GROUP_W = HEADS_PER_GROUP * SSD_P

ATT_HEADS, ATT_KV, ATT_HD = 16, 4, 64
ATT_KV_DIM = ATT_KV * ATT_HD
WINDOW = 128
ATT_BLOCK = 128
ATT_SCALE = ATT_HD ** -0.5
ROPE_BASE = 10000.0

GLA_HEADS, GLA_DK, GLA_DV = 4, 128, 256
GLA_C = 64
GLA_GATE_NORM = 16.0
GLA_LOWRANK = 16

N_GROUPS, EXP_PER_GROUP = 4, 4
N_EXPERTS = 16
D_EXPERT = 512

LANES = 128
SUBLANES = 8
VMEM_LIMIT = 56 * 1024 * 1024

P0_Z, P0_X, P0_Q, P0_BC, P0_K, P0_V = 0, 1024, 2048, 3072, 3584, 3840
P0_W = 4096
P1_Q, P1_K, P1_V, P1_R = 0, 512, 1024, 2048
P1_W = 3072

MOE_TM = 256
MOE_TILES = (2 * T) // MOE_TM + N_EXPERTS
MOE_ROWS = MOE_TILES * MOE_TM
ACC_ROWS = (T + SUBLANES) * SUBLANES


def _cparams(n_axes, vmem=VMEM_LIMIT):
    return pltpu.CompilerParams(dimension_semantics=("arbitrary",) * n_axes, vmem_limit_bytes=vmem)


def _silu(x):
    return x / (1.0 + jnp.exp(-x))


def _softplus(x):
    return jnp.maximum(x, 0.0) + jnp.log(1.0 + jnp.exp(-jnp.abs(x)))


def _mm(a, b):
    return jnp.dot(a.astype(BF16), b.astype(BF16), preferred_element_type=F32)


def _mm_nt(a, b):
    return lax.dot_general(a.astype(BF16), b.astype(BF16), (((1,), (1,)), ((), ())),
                           preferred_element_type=F32)


def _mm_tn(a, b):
    return lax.dot_general(a.astype(BF16), b.astype(BF16), (((0,), (0,)), ((), ())),
                           preferred_element_type=F32)


def _rms(x, w):
    return x * lax.rsqrt(jnp.mean(x * x, axis=-1, keepdims=True) + EPS) * w


def _cumsum_rows(x, n):
    row = lax.broadcasted_iota(jnp.int32, x.shape, 0)
    s = 1
    while s < n:
        x = x + jnp.where(row >= s, pltpu.roll(x, s, 0), 0.0)
        s *= 2
    return x


def _mod_row(tok0):
    return jnp.where(tok0 < T_CTX, 0, 1 + (tok0 - T_CTX) // DEC_SEQ)


ADA_TN = 1536


def _adaln_kernel(c_ref, w_ref, b_ref, o_ref):
    s = _silu(c_ref[...])
    o_ref[0] = _mm(s, w_ref[0]) + b_ref[0]


def _adaln(cond8, w_ada, b_ada):
    depth = w_ada.shape[0]
    return pl.pallas_call(
        _adaln_kernel,
        grid=(depth, 6 * D // ADA_TN),
        in_specs=[
            pl.BlockSpec((SUBLANES, D), lambda l, j: (0, 0)),
            pl.BlockSpec((1, D, ADA_TN), lambda l, j: (l, 0, j)),
            pl.BlockSpec((1, 1, ADA_TN), lambda l, j: (l, 0, j)),
        ],
        out_specs=pl.BlockSpec((1, SUBLANES, ADA_TN), lambda l, j: (l, 0, j)),
        out_shape=jax.ShapeDtypeStruct((depth, SUBLANES, 6 * D), F32),
        compiler_params=_cparams(2),
        name="adaln",
    )(cond8, w_ada, b_ada.reshape(depth, 1, 6 * D))


def _mod_spec(layer, chunk, tm, tile_of=lambda i, *_: i):
    return pl.BlockSpec((1, 1, D), lambda *g: ((layer * SUBLANES + _mod_row(tile_of(*g) * tm)) * 6 + chunk, 0, 0))


def _tokmajor_to_std(ref, tm):
    return jnp.concatenate([ref[pl.ds(k, tm, stride=SUBLANES), :] for k in range(D // LANES)], axis=1)


PROJ_TM = 1024
PROJ_TN = 512


def _ctx_lat_specs(tm, width=D):
    n_ctx = T_CTX // tm
    return (pl.BlockSpec((tm, width), lambda i, *_: (jnp.minimum(i, n_ctx - 1), 0)),
            pl.BlockSpec((tm, width), lambda i, *_: (jnp.maximum(i - n_ctx, 0), 0)))


def _modproj_kernel(*refs, dual_x, n_main, has_tail):
    it = iter(refs)
    if dual_x:
        xc_ref, xl_ref = next(it), next(it)
    else:
        x_ref, moe_ref, g2_ref = next(it), next(it), next(it)
    sh_ref, sc_ref, nw_ref, wm_ref = next(it), next(it), next(it), next(it)
    wt_ref = next(it) if has_tail else None
    ws_ref, o_ref, os_ref = next(it), next(it), next(it)
    xo_ref = None if dual_x else next(it)
    h_all, w_bf = next(it), next(it)
    j, i = pl.program_id(0), pl.program_id(1)
    tm = PROJ_TM
    rows = pl.ds(pl.multiple_of(i * tm, tm), tm)

    @pl.when(j == 0)
    def _():
        if dual_x:
            x = jnp.where(i < T_CTX // tm, xc_ref[...], xl_ref[...])
        else:
            x = x_ref[...] + g2_ref[0] * _tokmajor_to_std(moe_ref, tm)
            xo_ref[...] = x
        h = (_rms(x, nw_ref[...]) * (1.0 + sc_ref[0]) + sh_ref[0]).astype(BF16)
        h_all[rows, :] = h
        os_ref[...] = jnp.dot(h, ws_ref[...].astype(BF16), preferred_element_type=F32)

    @pl.when(i == 0)
    def _():
        def load(w_ref):
            w_bf[...] = w_ref[...].astype(BF16)

        if has_tail:
            pl.when(j < n_main)(functools.partial(load, wm_ref))
            pl.when(j >= n_main)(functools.partial(load, wt_ref))
        else:
            load(wm_ref)

    o_ref[...] = jnp.dot(h_all[rows, :], w_bf[...], preferred_element_type=F32).astype(o_ref.dtype)


def _modproj(xs, mods, layer, norm_w, w_main, n_main, w_small, out_block, w_tail=None, moe=None):
    tm, tn = PROJ_TM, PROJ_TN
    dual_x = moe is None
    has_tail = w_tail is not None
    n_tiles = n_main + (w_tail.shape[1] // tn if has_tail else 0)
    n_i, n_ctx = T // tm, T_CTX // tm

    def tok(j, i):
        return jnp.where(j == 0, i, n_i - 1)

    tile = pl.BlockSpec((tm, D), lambda j, i: (tok(j, i), 0))
    if dual_x:
        in_specs = [pl.BlockSpec((tm, D), lambda j, i: (jnp.minimum(tok(j, i), n_ctx - 1), 0)),
                    pl.BlockSpec((tm, D), lambda j, i: (jnp.maximum(tok(j, i) - n_ctx, 0), 0))]
        args = list(xs)
    else:
        in_specs = [tile, pl.BlockSpec((tm * SUBLANES, LANES), lambda j, i: (tok(j, i), 0)),
                    _mod_spec(layer - 1, 5, tm, tok)]
        args = [xs, moe, mods]
    in_specs += [_mod_spec(layer, 0, tm, tok), _mod_spec(layer, 1, tm, tok), pl.BlockSpec((1, D), lambda j, i: (0, 0)),
                 pl.BlockSpec((None, D, tn), lambda j, i: (0, 0, jnp.minimum(j, n_main - 1)))]
    args += [mods, mods, norm_w.reshape(1, D), w_main]
    if has_tail:
        in_specs.append(pl.BlockSpec((D, tn), lambda j, i: (0, jnp.maximum(j - n_main, 0))))
        args.append(w_tail)
    in_specs.append(pl.BlockSpec((D, LANES), lambda j, i: (0, 0)))
    args.append(w_small)
    out_specs = [pl.BlockSpec((tm, tn), lambda j, i: (i, out_block(j))),
                 pl.BlockSpec((tm, LANES), lambda j, i: (tok(j, i), 0))]
    out_shape = [jax.ShapeDtypeStruct((T, n_tiles * tn), BF16), jax.ShapeDtypeStruct((T, LANES), F32)]
    if not dual_x:
        out_specs.append(tile)
        out_shape.append(jax.ShapeDtypeStruct((T, D), F32))
    return pl.pallas_call(
        functools.partial(_modproj_kernel, dual_x=dual_x, n_main=n_main, has_tail=has_tail),
        grid=(n_tiles, n_i), in_specs=in_specs, out_specs=out_specs, out_shape=out_shape,
        scratch_shapes=[pltpu.VMEM((T, D), BF16), pltpu.VMEM((D, tn), BF16)],
        compiler_params=_cparams(2),
        name=f"modproj{layer}",
    )(*args)


def _expand_heads(v, off):
    lo = lax.broadcasted_iota(jnp.int32, (v.shape[0], LANES), 1) < SSD_P
    tiles = []
    for q in range(SSD_HEADS // 2):
        a = jnp.broadcast_to(v[:, off + 2 * q:off + 2 * q + 1], (v.shape[0], LANES))
        b = jnp.broadcast_to(v[:, off + 2 * q + 1:off + 2 * q + 2], (v.shape[0], LANES))
        tiles.append(jnp.where(lo, a, b))
    return jnp.concatenate(tiles, axis=1)


def _ssd_kernel(*refs, seq, has_h0):
    if has_h0:
        (z_ref, x_ref, bc_ref, dt_ref, cwx_ref, cwbc_ref, cbx_ref, cbbc_ref, dtb_ref, alog_ref, dsk_ref,
         nw_ref, h0f_ref, h0b_ref, y_ref, sf_ref, sb_ref,
         xpad, bcpad, xc, bcc, a_scr, dt_scr, yacc, hf, hb) = refs
    else:
        (z_ref, x_ref, bc_ref, dt_ref, cwx_ref, cwbc_ref, cbx_ref, cbbc_ref, dtb_ref, alog_ref, dsk_ref,
         nw_ref, y_ref, sf_ref, sb_ref,
         xpad, bcpad, xc, bcc, a_scr, dt_scr, yacc, hf, hb) = refs
    L = SSD_L
    nc = seq // L
    pad = SUBLANES
    half = SSD_CONV // 2

    for buf, src, cw, cb, dst in ((xpad, x_ref, cwx_ref, cbx_ref, xc), (bcpad, bc_ref, cwbc_ref, cbbc_ref, bcc)):
        width = buf.shape[1]
        buf[0:pad, :] = jnp.zeros((pad, width), F32)
        buf[pad + seq:2 * pad + seq, :] = jnp.zeros((pad, width), F32)
        buf[pad:pad + seq, :] = src[...].astype(F32)
        for blk in range(nc):
            acc = jnp.broadcast_to(cb[...], (L, width))
            for j in range(SSD_CONV):
                r0 = pad - half + j + blk * L
                acc = acc + cw[j:j + 1, :] * buf[r0:r0 + L, :]
            dst[blk * L:(blk + 1) * L, :] = _silu(acc)

    lane = lax.broadcasted_iota(jnp.int32, (seq, LANES), 1)
    dts = jnp.where(lane < 2 * SSD_HEADS, _softplus(dt_ref[...] + dtb_ref[...]), 0.0)
    dt_scr[...] = dts
    a_scr[...] = dts * (-jnp.exp(alog_ref[...]))

    if has_h0:
        hf[...] = h0f_ref[...]
        hb[...] = h0b_ref[...]
    else:
        hf[...] = jnp.zeros(hf.shape, F32)
        hb[...] = jnp.zeros(hb.shape, F32)

    row = lax.broadcasted_iota(jnp.int32, (L, L), 0)
    col = lax.broadcasted_iota(jnp.int32, (L, L), 1)
    lane_l = lax.broadcasted_iota(jnp.int32, (L, LANES), 1)
    lo_half = lane_l < SSD_P

    def chunk(c, fwd, h_scr):
        off = 0 if fwd else SSD_HEADS
        r0 = pl.multiple_of(c * L, L)
        a = a_scr[pl.ds(r0, L), :]
        dt = dt_scr[pl.ds(r0, L), :]
        cs = _cumsum_rows(a, L)
        total = cs[L - 1:L, :]
        if fwd:
            u = cs
            rvec = jnp.exp(cs)
            ed = jnp.exp(total - cs) * dt
            keep = col <= row
        else:
            ex = cs - a
            u = -ex
            rvec = jnp.exp(total - ex)
            ed = jnp.exp(ex) * dt
            keep = col >= row
        ut = jnp.transpose(u)
        dtt = jnp.transpose(dt)
        tcol = jnp.transpose(jnp.broadcast_to(total, (L, LANES)))[:, 0:1]
        rexp = _expand_heads(rvec, off)
        edexp = _expand_heads(ed, off)
        x = xc[pl.ds(r0, L), :]
        bc = bcc[pl.ds(r0, L), :]
        outs = []
        for g in range(SSD_GROUPS):
            bg = bc[:, g * SSD_N:(g + 1) * SSD_N]
            cg = bc[:, SSD_GROUPS * SSD_N + g * SSD_N:SSD_GROUPS * SSD_N + (g + 1) * SSD_N]
            cbm = _mm_nt(cg, bg)
            hg = h_scr[g * GROUP_W:(g + 1) * GROUP_W, :]
            xg = x[:, g * GROUP_W:(g + 1) * GROUP_W]
            y_off = _mm_nt(cg, hg) * rexp[:, g * GROUP_W:(g + 1) * GROUP_W]
            tiles = []
            for p in range(HEADS_PER_GROUP // 2):
                xt = xg[:, p * LANES:(p + 1) * LANES]
                acc = None
                for s in range(2):
                    h = off + g * HEADS_PER_GROUP + 2 * p + s
                    seg = u[:, h:h + 1] - ut[h:h + 1, :]
                    m = cbm * jnp.exp(jnp.where(keep, seg, -jnp.inf)) * dtt[h:h + 1, :]
                    xm = jnp.where(lo_half if s == 0 else jnp.logical_not(lo_half), xt, 0.0)
                    d = _mm(m, xm)
                    acc = d if acc is None else acc + d
                tiles.append(acc)
            outs.append(y_off + jnp.concatenate(tiles, axis=1))
            decs = []
            for hh in range(HEADS_PER_GROUP):
                h = off + g * HEADS_PER_GROUP + hh
                decs.append(jnp.broadcast_to(jnp.exp(tcol[h:h + 1, :]), (SSD_P, SSD_N)))
            dec = jnp.concatenate(decs, axis=0)
            h_scr[g * GROUP_W:(g + 1) * GROUP_W, :] = dec * hg + _mm_tn(xg * edexp[:, g * GROUP_W:(g + 1) * GROUP_W], bg)
        return r0, x, jnp.concatenate(outs, axis=1)

    def fwd_body(c, carry):
        r0, _, y = chunk(c, True, hf)
        yacc[pl.ds(r0, L), :] = y
        return carry

    lax.fori_loop(0, nc, fwd_body, 0)

    def bwd_body(i, carry):
        c = nc - 1 - i
        r0, x, y = chunk(c, False, hb)
        y = yacc[pl.ds(r0, L), :] + y + dsk_ref[...] * x
        y = y * _silu(z_ref[pl.ds(r0, L), :].astype(F32))
        y_ref[pl.ds(r0, L), :] = _rms(y, nw_ref[...]).astype(y_ref.dtype)
        return carry

    lax.fori_loop(0, nc, bwd_body, 0)
    sf_ref[...] = hf[...]
    sb_ref[...] = hb[...]


def _ssd(p0, p0dt, tok0, nseq, seq, cw, cb, dtb, alog, dskip, nw, h0f=None, h0b=None):
    has_h0 = h0f is not None
    b0 = tok0 // seq

    def cols(width, start):
        return pl.BlockSpec((seq, width), lambda s: (b0 + s, start // width))

    def full(shape):
        return pl.BlockSpec(shape, lambda s: (0,) * len(shape))

    in_specs = [cols(D_SSD, P0_Z), cols(D_SSD, P0_X), cols(512, P0_BC), cols(LANES, 0),
                full((SSD_CONV, D_SSD)), full((SSD_CONV, 512)), full((1, D_SSD)), full((1, 512)),
                full((1, LANES)), full((1, LANES)), full((1, D_SSD)), full((1, D_SSD))]
    args = [p0, p0, p0, p0dt, cw[:, :D_SSD], cw[:, D_SSD:], cb[:D_SSD].reshape(1, -1), cb[D_SSD:].reshape(1, -1),
            dtb, alog, dskip, nw.reshape(1, -1)]
    st_spec = pl.BlockSpec((None, D_SSD, SSD_N), lambda s: (s, 0, 0))
    if has_h0:
        in_specs += [st_spec, st_spec]
        args += [h0f, h0b]
    st_shape = jax.ShapeDtypeStruct((nseq, D_SSD, SSD_N), F32)
    return pl.pallas_call(
        functools.partial(_ssd_kernel, seq=seq, has_h0=has_h0),
        grid=(nseq,), in_specs=in_specs,
        out_specs=[pl.BlockSpec((seq, D_SSD), lambda s: (s, 0)), st_spec, st_spec],
        out_shape=[jax.ShapeDtypeStruct((nseq * seq, D_SSD), BF16), st_shape, st_shape],
        scratch_shapes=[pltpu.VMEM((seq + 2 * SUBLANES, D_SSD), F32), pltpu.VMEM((seq + 2 * SUBLANES, 512), F32),
                        pltpu.VMEM((seq, D_SSD), F32), pltpu.VMEM((seq, 512), F32),
                        pltpu.VMEM((seq, LANES), F32), pltpu.VMEM((seq, LANES), F32),
                        pltpu.VMEM((seq, D_SSD), F32),
                        pltpu.VMEM((D_SSD, SSD_N), F32), pltpu.VMEM((D_SSD, SSD_N), F32)],
        compiler_params=_cparams(1),
        name=f"ssd{seq}",
    )(*args)


def _place_halves(tile, kv_in_high):
    lo = lax.broadcasted_iota(jnp.int32, tile.shape, 1) < ATT_HD
    swapped = pltpu.roll(tile, ATT_HD, 1)
    if kv_in_high:
        return jnp.where(lo, swapped, 0.0), jnp.where(lo, 0.0, tile)
    return jnp.where(lo, tile, 0.0), jnp.where(lo, 0.0, swapped)


def _sink_attend(score_parts, value_parts, sink):
    m = sink
    for s in score_parts:
        m = jnp.maximum(m, jnp.max(s, axis=-1, keepdims=True))
    denom = jnp.exp(sink - m)
    out = None
    for s, v in zip(score_parts, value_parts):
        p = jnp.exp(s - m)
        denom = denom + jnp.sum(p, axis=-1, keepdims=True)
        o = _mm(p, v)
        out = o if out is None else out + o
    return out * (1.0 / denom)


def _ctx_attn_kernel(q_ref, k_ref, v_ref, sink_ref, o_ref):
    sink = sink_ref[...]
    for j in range(ATT_KV):
        t, high = j // 2, (j % 2 == 1)
        k_lo, k_hi = _place_halves(k_ref[:, t * LANES:(t + 1) * LANES].astype(F32), high)
        v_lo, v_hi = _place_halves(v_ref[:, t * LANES:(t + 1) * LANES].astype(F32), high)
        for qt in range(2 * j, 2 * j + 2):
            q = q_ref[:, qt * LANES:(qt + 1) * LANES]
            acc = None
            for s, (kk, vv) in enumerate(((k_lo, v_lo), (k_hi, v_hi))):
                h = 2 * qt + s
                sc = _mm_nt(q, kk) * ATT_SCALE
                o = _sink_attend([sc], [vv], sink[:, h:h + 1])
                acc = o if acc is None else acc + o
            o_ref[:, qt * LANES:(qt + 1) * LANES] = acc.astype(o_ref.dtype)


def _ctx_attn(p0, sink):
    def cols(width, start):
        return pl.BlockSpec((SEQ, width), lambda b: (b, start // width))

    return pl.pallas_call(
        _ctx_attn_kernel,
        grid=(BATCH,),
        in_specs=[cols(D, P0_Q), cols(ATT_KV_DIM, P0_K), cols(ATT_KV_DIM, P0_V),
                  pl.BlockSpec((1, LANES), lambda b: (0, 0))],
        out_specs=pl.BlockSpec((SEQ, D), lambda b: (b, 0)),
        out_shape=jax.ShapeDtypeStruct((T_CTX, D), BF16),
        compiler_params=_cparams(1),
        name="ctx_attn",
    )(p0, p0, p0, sink)


def _rope_tables():
    quarter = ATT_HD // 4
    t = np.arange(DEC_SEQ)
    lane = np.arange(LANES)
    inv = ROPE_BASE ** (-(lane % quarter).astype(np.float64) / quarter)
    pos = np.where(((lane % ATT_HD) < ATT_HD // 2)[None, :], (t // GRID_W)[:, None], (t % GRID_W)[:, None])
    ang = pos * inv[None, :]
    first = ((lane % (2 * quarter)) < quarter)[None, :]
    cos, sin = np.cos(ang), np.sin(ang)
    return (jnp.asarray(cos, F32), jnp.asarray(np.where(first, -sin, 0.0), F32),
            jnp.asarray(np.where(first, 0.0, sin), F32))


def _rope(x, cos, sa, sb):
    quarter = ATT_HD // 4
    return x * cos + pltpu.roll(x, LANES - quarter, 1) * sa + pltpu.roll(x, quarter, 1) * sb


def _lat_attn_kernel(q_ref, kp_ref, kc_ref, kn_ref, vp_ref, vc_ref, vn_ref, ck_ref, cv_ref,
                     cos_ref, sa_ref, sb_ref, sink_ref, o_ref):
    blk = pl.program_id(1)
    nb = pl.num_programs(1)
    B = ATT_BLOCK
    sink = sink_ref[...]

    def tables(b):
        r0 = pl.multiple_of(b * B, B)
        return cos_ref[pl.ds(r0, B), :], sa_ref[pl.ds(r0, B), :], sb_ref[pl.ds(r0, B), :]

    tq = tables(blk)
    tk = [tables(jnp.maximum(blk - 1, 0)), tq, tables(jnp.minimum(blk + 1, nb - 1))]
    qpos = blk * B + lax.broadcasted_iota(jnp.int32, (B, 3 * B), 0)
    kabs = (blk - 1) * B + lax.broadcasted_iota(jnp.int32, (B, 3 * B), 1)
    ok = (jnp.abs(qpos - kabs) <= WINDOW) & (kabs >= 0) & (kabs < nb * B)

    for j in range(ATT_KV):
        t, high = j // 2, (j % 2 == 1)
        sl = slice(t * LANES, (t + 1) * LANES)
        kw = jnp.concatenate([_rope(r[:, sl].astype(F32), *tb) for r, tb in zip((kp_ref, kc_ref, kn_ref), tk)], axis=0)
        vw = jnp.concatenate([r[:, sl].astype(F32) for r in (vp_ref, vc_ref, vn_ref)], axis=0)
        k_lo, k_hi = _place_halves(kw, high)
        v_lo, v_hi = _place_halves(vw, high)
        ck_lo, ck_hi = _place_halves(ck_ref[:, sl], high)
        cv_lo, cv_hi = _place_halves(cv_ref[:, sl], high)
        for qt in range(2 * j, 2 * j + 2):
            q = q_ref[:, qt * LANES:(qt + 1) * LANES].astype(F32)
            qr = _rope(q, *tq)
            acc = None
            for s, (kk, vv, ckk, cvv) in enumerate(((k_lo, v_lo, ck_lo, cv_lo), (k_hi, v_hi, ck_hi, cv_hi))):
                h = 2 * qt + s
                s_ctx = _mm_nt(q, ckk) * ATT_SCALE
                s_win = jnp.where(ok, _mm_nt(qr, kk) * ATT_SCALE, -jnp.inf)
                o = _sink_attend([s_ctx, s_win], [cvv, vv], sink[:, h:h + 1])
                acc = o if acc is None else acc + o
            o_ref[:, qt * LANES:(qt + 1) * LANES] = acc.astype(o_ref.dtype)


def _lat_attn(p0, ck, cv, sink):
    nb = DEC_SEQ // ATT_BLOCK
    base = T_CTX // ATT_BLOCK

    def kv(start, shift):
        return pl.BlockSpec((ATT_BLOCK, ATT_KV_DIM),
                            lambda b, i: (base + b * nb + jnp.clip(i + shift, 0, nb - 1), start // ATT_KV_DIM))

    def full(shape):
        return pl.BlockSpec(shape, lambda b, i: (0,) * len(shape))

    cache = pl.BlockSpec((None, PAST_LEN, ATT_KV_DIM), lambda b, i: (b, 0, 0))
    cos, sa, sb = _rope_tables()
    return pl.pallas_call(
        _lat_attn_kernel,
        grid=(DEC_BATCH, nb),
        in_specs=[pl.BlockSpec((ATT_BLOCK, D), lambda b, i: (base + b * nb + i, P0_Q // D)),
                  kv(P0_K, -1), kv(P0_K, 0), kv(P0_K, 1), kv(P0_V, -1), kv(P0_V, 0), kv(P0_V, 1),
                  cache, cache, full((DEC_SEQ, LANES)), full((DEC_SEQ, LANES)), full((DEC_SEQ, LANES)),
                  full((1, LANES))],
        out_specs=pl.BlockSpec((ATT_BLOCK, D), lambda b, i: (b * nb + i, 0)),
        out_shape=jax.ShapeDtypeStruct((T_LAT, D), BF16),
        compiler_params=_cparams(2),
        name="lat_attn",
    )(p0, p0, p0, p0, p0, p0, p0, ck, cv, cos, sa, sb, sink)


def _log_sigmoid(x):
    return jnp.minimum(x, 0.0) - jnp.log(1.0 + jnp.exp(-jnp.abs(x)))


def _gla_kernel(*refs, seq, has_s0):
    if has_s0:
        (q_ref, k_ref, v_ref, r_ref, lr_ref, w2f_ref, w2b_ref, bf_ref, bb_ref, nw_ref, s0f_ref, s0b_ref,
         y_ref, sf_ref, sb_ref, gf, gb, yf, yb, stf, stb) = refs
    else:
        (q_ref, k_ref, v_ref, r_ref, lr_ref, w2f_ref, w2b_ref, bf_ref, bb_ref, nw_ref,
         y_ref, sf_ref, sb_ref, gf, gb, yf, yb, stf, stb) = refs
    C = GLA_C
    nc = seq // C
    lr = lr_ref[...]
    gf[...] = _log_sigmoid(_mm(lr, w2f_ref[...]) + bf_ref[...]) / GLA_GATE_NORM
    gb[...] = _log_sigmoid(_mm(lr, w2b_ref[...]) + bb_ref[...]) / GLA_GATE_NORM
    for h in range(GLA_HEADS):
        rows = slice(h * GLA_DV, (h + 1) * GLA_DV)
        if has_s0:
            stf[rows, :] = jnp.transpose(s0f_ref[h * GLA_DK:(h + 1) * GLA_DK, :])
            stb[rows, :] = jnp.transpose(s0b_ref[h * GLA_DK:(h + 1) * GLA_DK, :])
        else:
            stf[rows, :] = jnp.zeros((GLA_DV, GLA_DK), F32)
            stb[rows, :] = jnp.zeros((GLA_DV, GLA_DK), F32)

    row = lax.broadcasted_iota(jnp.int32, (C, C), 0)
    col = lax.broadcasted_iota(jnp.int32, (C, C), 1)
    qscale = GLA_DK ** -0.5

    def chunk(c, fwd):
        g_scr, y_scr, st = (gf, yf, stf) if fwd else (gb, yb, stb)
        r0 = pl.multiple_of(c * C, C)
        g = g_scr[pl.ds(r0, C), :]
        cs = _cumsum_rows(g, C)
        total = cs[C - 1:C, :]
        q = q_ref[pl.ds(r0, C), :].astype(F32) * qscale
        k = k_ref[pl.ds(r0, C), :].astype(F32)
        v = v_ref[pl.ds(r0, C), :]
        if fwd:
            qs, ks, ke = q * jnp.exp(cs), k * jnp.exp(-cs), k * jnp.exp(total - cs)
            keep = col <= row
        else:
            ex = cs - g
            qs, ks, ke = q * jnp.exp(total - ex), k * jnp.exp(ex - total), k * jnp.exp(ex)
            keep = col >= row
        dec = jnp.exp(total)
        for h in range(GLA_HEADS):
            kc = slice(h * GLA_DK, (h + 1) * GLA_DK)
            vc = slice(h * GLA_DV, (h + 1) * GLA_DV)
            s_t = st[vc, :]
            att = jnp.where(keep, _mm_nt(qs[:, kc], ks[:, kc]), 0.0)
            y_scr[pl.ds(r0, C), vc] = _mm(att, v[:, vc]) + _mm_nt(qs[:, kc], s_t)
            st[vc, :] = dec[:, kc] * s_t + _mm_tn(v[:, vc], ke[:, kc])

    def body(i, carry):
        chunk(i, True)
        chunk(nc - 1 - i, False)
        return carry

    lax.fori_loop(0, nc, body, 0)

    nw = nw_ref[...]
    for blk in range(seq // LANES):
        rs = slice(blk * LANES, (blk + 1) * LANES)
        y = yf[rs, :] + yb[rs, :]
        gate = _silu(r_ref[rs, :].astype(F32))
        for h in range(GLA_HEADS):
            vc = slice(h * GLA_DV, (h + 1) * GLA_DV)
            y_ref[rs, vc] = (_rms(y[:, vc], nw) * gate[:, vc]).astype(y_ref.dtype)
    for h in range(GLA_HEADS):
        rows = slice(h * GLA_DV, (h + 1) * GLA_DV)
        sf_ref[h * GLA_DK:(h + 1) * GLA_DK, :] = jnp.transpose(stf[rows, :])
        sb_ref[h * GLA_DK:(h + 1) * GLA_DK, :] = jnp.transpose(stb[rows, :])


def _gla(p1, p1lr, tok0, nseq, seq, w2f, w2b, bgf, bgb, nw, s0f=None, s0b=None):
    has_s0 = s0f is not None
    b0 = tok0 // seq
    dk_all, dv_all = GLA_HEADS * GLA_DK, GLA_HEADS * GLA_DV

    def cols(width, start):
        return pl.BlockSpec((seq, width), lambda s: (b0 + s, start // width))

    def full(shape):
        return pl.BlockSpec(shape, lambda s: (0,) * len(shape))

    in_specs = [cols(dk_all, P1_Q), cols(dk_all, P1_K), cols(dv_all, P1_V), cols(dv_all, P1_R), cols(LANES, 0),
                full((LANES, dk_all)), full((LANES, dk_all)), full((1, dk_all)), full((1, dk_all)), full((1, GLA_DV))]
    args = [p1, p1, p1, p1, p1lr, w2f, w2b, bgf.reshape(1, -1), bgb.reshape(1, -1), nw.reshape(1, -1)]
    st_spec = pl.BlockSpec((None, dk_all, GLA_DV), lambda s: (s, 0, 0))
    if has_s0:
        in_specs += [st_spec, st_spec]
        args += [s0f, s0b]
    st_shape = jax.ShapeDtypeStruct((nseq, dk_all, GLA_DV), F32)
    return pl.pallas_call(
        functools.partial(_gla_kernel, seq=seq, has_s0=has_s0),
        grid=(nseq,), in_specs=in_specs,
        out_specs=[pl.BlockSpec((seq, dv_all), lambda s: (s, 0)), st_spec, st_spec],
        out_shape=[jax.ShapeDtypeStruct((nseq * seq, dv_all), BF16), st_shape, st_shape],
        scratch_shapes=[pltpu.VMEM((seq, dk_all), F32), pltpu.VMEM((seq, dk_all), F32),
                        pltpu.VMEM((seq, dv_all), F32), pltpu.VMEM((seq, dv_all), F32),
                        pltpu.VMEM((dv_all, GLA_DK), F32), pltpu.VMEM((dv_all, GLA_DK), F32)],
        compiler_params=_cparams(1),
        name=f"gla{seq}",
    )(*args)


ROUTE_TM = 256


def _split_bf16(x):
    hi = x.astype(BF16)
    return hi, (x - hi.astype(F32)).astype(BF16)


def _outproj_kernel(*refs, n_in, dual_x):
    y_refs = refs[:2 * n_in]
    n_x = 2 if dual_x else 1
    x_refs = refs[2 * n_in + 1:2 * n_in + 1 + n_x]
    w_ref = refs[2 * n_in]
    (g1_ref, sh_ref, sc_ref, nw_ref, wr_ref, br_ref,
     xo_ref, tr_ref, route_ref, cnt_ref, w_scr, wr_hi, wr_lo, carry) = refs[2 * n_in + 1 + n_x:]
    tm = ROUTE_TM
    is_ctx = pl.program_id(0) < T_CTX // tm

    @pl.when(pl.program_id(0) == 0)
    def _():
        w_scr[...] = w_ref[...].astype(BF16)
        hi, lo = _split_bf16(wr_ref[...])
        wr_hi[...] = hi
        wr_lo[...] = lo
        carry[...] = jnp.zeros(carry.shape, F32)

    o = None
    for i in range(n_in):
        y = jnp.where(is_ctx, y_refs[2 * i][...], y_refs[2 * i + 1][...])
        d = jnp.dot(y, w_scr[i * D:(i + 1) * D, :], preferred_element_type=F32)
        o = d if o is None else o + d
    x_in = jnp.where(is_ctx, x_refs[0][...], x_refs[1][...]) if dual_x else x_refs[0][...]
    x = x_in + g1_ref[0] * o
    xo_ref[...] = x
    t = _rms(x, nw_ref[...]) * (1.0 + sc_ref[0]) + sh_ref[0]
    for k in range(D // LANES):
        tr_ref[pl.ds(k, tm, stride=SUBLANES), :] = t[:, k * LANES:(k + 1) * LANES]

    t_hi, t_lo = _split_bf16(t)
    logit = (jnp.dot(t_hi, wr_hi[...], preferred_element_type=F32)
             + jnp.dot(t_lo, wr_hi[...], preferred_element_type=F32)
             + jnp.dot(t_hi, wr_lo[...], preferred_element_type=F32)) + br_ref[...]
    lane = lax.broadcasted_iota(jnp.int32, (tm, LANES), 1)
    lanef = lane.astype(F32)
    neg = -jnp.inf
    gl = jnp.where(lane < N_GROUPS, logit, neg)
    gmax = jnp.max(gl, axis=-1, keepdims=True)
    gsel = jnp.min(jnp.where(gl == gmax, lanef, float(LANES)), axis=-1, keepdims=True)
    gprob = 1.0 / jnp.sum(jnp.exp(gl - gmax), axis=-1, keepdims=True)
    first = N_GROUPS + EXP_PER_GROUP * gsel
    el = jnp.where((lanef >= first) & (lanef < first + EXP_PER_GROUP), logit, neg)
    m1 = jnp.max(el, axis=-1, keepdims=True)
    i1 = jnp.min(jnp.where(el == m1, lanef, float(LANES)), axis=-1, keepdims=True)
    el2 = jnp.where(lanef == i1, neg, el)
    m2 = jnp.max(el2, axis=-1, keepdims=True)
    i2 = jnp.min(jnp.where(el2 == m2, lanef, float(LANES)), axis=-1, keepdims=True)
    e2 = jnp.exp(m2 - m1)
    c1 = gprob / (1.0 + e2)
    c2 = gprob * e2 / (1.0 + e2)
    x1 = i1 - N_GROUPS
    x2 = i2 - N_GROUPS

    hot = ((lanef == x1) | (lanef == x2)).astype(F32)
    tri = (lax.broadcasted_iota(jnp.int32, (tm, tm), 1) < lax.broadcasted_iota(jnp.int32, (tm, tm), 0))
    before = _mm(tri.astype(F32), hot) + carry[0:1, :]
    r1 = jnp.sum(jnp.where(lanef == x1, before, 0.0), axis=-1, keepdims=True)
    r2 = jnp.sum(jnp.where(lanef == x2, before, 0.0), axis=-1, keepdims=True)
    total = carry[...] + jnp.sum(hot, axis=0, keepdims=True)
    carry[...] = total
    cnt_ref[...] = total
    vals = (x1, x2, c1, c2, r1, r2)
    out = jnp.zeros((tm, LANES), F32)
    for k, v in enumerate(vals):
        out = jnp.where(lane == k, v, out)
    route_ref[...] = jnp.transpose(out)[0:SUBLANES, :]


def _outproj_route(ys, w_out, xs, mods, layer, norm_w, w_router, b_router):
    tm = ROUTE_TM
    n_in = len(ys) // 2
    dual_x = len(xs) == 2
    kdim = w_out.shape[0]

    def full(shape):
        return pl.BlockSpec(shape, lambda i: (0,) * len(shape))

    tile = pl.BlockSpec((tm, D), lambda i: (i, 0))
    pair = list(_ctx_lat_specs(tm))
    in_specs = (pair * n_in + [full((kdim, D))] + (pair if dual_x else [tile])
                + [_mod_spec(layer, 2, tm), _mod_spec(layer, 3, tm), _mod_spec(layer, 4, tm),
                   full((1, D)), full((D, LANES)), full((1, LANES))])
    return pl.pallas_call(
        functools.partial(_outproj_kernel, n_in=n_in, dual_x=dual_x),
        grid=(T // tm,), in_specs=in_specs,
        out_specs=[tile, pl.BlockSpec((tm * SUBLANES, LANES), lambda i: (i, 0)),
                   pl.BlockSpec((SUBLANES, tm), lambda i: (0, i)), full((SUBLANES, LANES))],
        out_shape=[jax.ShapeDtypeStruct((T, D), F32), jax.ShapeDtypeStruct((T * SUBLANES, LANES), F32),
                   jax.ShapeDtypeStruct((SUBLANES, T), F32), jax.ShapeDtypeStruct((SUBLANES, LANES), F32)],
        scratch_shapes=[pltpu.VMEM((kdim, D), BF16), pltpu.VMEM((D, LANES), BF16), pltpu.VMEM((D, LANES), BF16),
                        pltpu.VMEM((SUBLANES, LANES), F32)],
        compiler_params=_cparams(1),
        name=f"outproj{layer}",
    )(*ys, w_out, *xs, mods, mods, mods, norm_w.reshape(1, D), w_router, b_router)


def _moe_meta(counts):
    tm = MOE_TM
    experts = jnp.arange(N_EXPERTS, dtype=jnp.int32)
    counts = counts[0, :N_EXPERTS].astype(jnp.int32)
    padded = ((counts + tm - 1) // tm) * tm
    ends = jnp.cumsum(padded)
    tile_start = jnp.arange(MOE_TILES, dtype=jnp.int32) * tm
    te = jnp.sum((tile_start[:, None] >= ends[None, :]).astype(jnp.int32), axis=1)
    last = jnp.max(jnp.where(counts > 0, experts, 0))
    meta = jnp.concatenate([jnp.minimum(te, last), ends[-1:] // tm]).astype(jnp.int32)
    starts = ends - padded
    pads = jnp.concatenate([starts + counts, ends[-1:], ends, jnp.full((1,), MOE_ROWS)]).astype(jnp.int32)
    return starts.astype(jnp.int32), pads, meta


def _expert_changed(meta_ref, j):
    return (j == 0) | (meta_ref[j] != meta_ref[jnp.maximum(j - 1, 0)])


def _moe_up_kernel(pos1_ref, pos2_ref, pads_ref, meta_ref, tr_hbm, wg_ref, wu_ref, a_ref, rowmap_ref,
                   tr_scr, g0, g1, wg_bf, wu_bf, sem):
    j = pl.program_id(0)
    tm = MOE_TM
    ntiles = meta_ref[MOE_TILES]

    def gather(tile, dst):
        for mi in range(tm):
            tok = jnp.minimum(rowmap_ref[tile * tm + mi] >> 1, T - 1)
            dst[mi * SUBLANES:(mi + 1) * SUBLANES, :] = tr_scr[pl.ds(pl.multiple_of(tok * SUBLANES, SUBLANES), SUBLANES), :]

    @pl.when(j == 0)
    def _():
        load = pltpu.make_async_copy(tr_hbm, tr_scr, sem)
        load.start()

        def clear(r, carry):
            rowmap_ref[r] = 2 * T
            return carry
        for k in range(N_EXPERTS + 1):
            lax.fori_loop(pads_ref[k], pads_ref[N_EXPERTS + 1 + k], clear, 0)

        def place(t, carry):
            rowmap_ref[pos1_ref[t]] = 2 * t
            rowmap_ref[pos2_ref[t]] = 2 * t + 1
            return carry
        lax.fori_loop(0, T, place, 0, unroll=8)
        load.wait()
        gather(0, g0)

    def compute(cur, nxt):
        gather(jnp.minimum(j + 1, ntiles - 1), nxt)
        x = _tokmajor_to_std(cur, tm).astype(BF16)
        g = jnp.dot(x, wg_bf[...], preferred_element_type=F32)
        u = jnp.dot(x, wu_bf[...], preferred_element_type=F32)
        a_ref[...] = (_silu(g) * u).astype(a_ref.dtype)

    @pl.when(j < ntiles)
    def _():
        @pl.when(_expert_changed(meta_ref, j))
        def _():
            wg_bf[...] = wg_ref[...].astype(BF16)
            wu_bf[...] = wu_ref[...].astype(BF16)

        pl.when(j % 2 == 0)(functools.partial(compute, g0, g1))
        pl.when(j % 2 == 1)(functools.partial(compute, g1, g0))

    @pl.when(j >= ntiles)
    def _():
        a_ref[...] = jnp.zeros(a_ref.shape, a_ref.dtype)


def _moe_down_kernel(rowmap_ref, cpair_ref, meta_ref, a_ref, wd_ref, out_hbm, acc, y0, y1, wd_bf):
    j = pl.program_id(0)
    tm = MOE_TM
    zrows = 512
    ntiles = meta_ref[MOE_TILES]

    @pl.when(j == 0)
    def _():
        def zero(i, carry):
            acc[pl.ds(pl.multiple_of(i * zrows, zrows), zrows), :] = jnp.zeros((zrows, LANES), F32)
            return carry
        lax.fori_loop(0, ACC_ROWS // zrows, zero, 0)

    def matmul(dst):
        y = jnp.dot(a_ref[...], wd_bf[...], preferred_element_type=F32)
        for k in range(D // LANES):
            dst[pl.ds(k, tm, stride=SUBLANES), :] = y[:, k * LANES:(k + 1) * LANES]

    def scatter(tile, src):
        for b in range(tm // SUBLANES):
            ents = [rowmap_ref[tile * tm + b * SUBLANES + i] for i in range(SUBLANES)]
            offs = [pl.multiple_of((e >> 1) * SUBLANES, SUBLANES) for e in ents]
            olds = [acc[pl.ds(o, SUBLANES), :] for o in offs]
            for i, o in enumerate(offs):
                r = (b * SUBLANES + i) * SUBLANES
                acc[pl.ds(o, SUBLANES), :] = olds[i] + cpair_ref[ents[i]] * src[r:r + SUBLANES, :]

    has_mm = j < ntiles
    has_sc = (j >= 1) & (j <= ntiles)

    @pl.when(has_mm)
    def _():
        @pl.when(_expert_changed(meta_ref, j))
        def _():
            wd_bf[...] = wd_ref[...].astype(BF16)

    for par, (cur, prev) in enumerate(((y0, y1), (y1, y0))):
        mine = (j % 2) == par

        @pl.when(mine & has_mm & has_sc)
        def _():
            matmul(cur)
            scatter(j - 1, prev)

        @pl.when(mine & has_mm & jnp.logical_not(has_sc))
        def _():
            matmul(cur)

        @pl.when(mine & jnp.logical_not(has_mm) & has_sc)
        def _():
            scatter(j - 1, prev)

    @pl.when(j == pl.num_programs(0) - 1)
    def _():
        pltpu.sync_copy(acc.at[0:T * SUBLANES, :], out_hbm)


def _moe(tr, route_t, counts, layer, w_gate, w_up, w_down):
    tm = MOE_TM
    starts, pads, meta = _moe_meta(counts)
    experts = jnp.arange(N_EXPERTS, dtype=jnp.int32)

    def position(e, r):
        sel = e.astype(jnp.int32)[:, None] == experts[None, :]
        return jnp.sum(jnp.where(sel, starts[None, :], 0), axis=1) + r.astype(jnp.int32)

    pos1 = position(route_t[0], route_t[4])
    pos2 = position(route_t[1], route_t[5])
    cpair = jnp.concatenate([jnp.stack([route_t[2], route_t[3]], axis=1).reshape(2 * T), jnp.zeros((2,), F32)])

    def wspec(shape, n):
        return pl.BlockSpec((None, None) + shape,
                            lambda j, *pre: (layer, pre[n - 1][jnp.minimum(j, MOE_TILES - 1)], 0, 0))

    gscr = pltpu.VMEM((tm * SUBLANES, LANES), F32)
    act, rowmap = pl.pallas_call(
        _moe_up_kernel,
        grid_spec=pltpu.PrefetchScalarGridSpec(
            num_scalar_prefetch=4, grid=(MOE_TILES,),
            in_specs=[pl.BlockSpec(memory_space=pl.ANY), wspec((D, D_EXPERT), 4), wspec((D, D_EXPERT), 4)],
            out_specs=[pl.BlockSpec((tm, D_EXPERT), lambda j, *pre: (j, 0)), pl.BlockSpec(memory_space=pltpu.SMEM)],
            scratch_shapes=[pltpu.VMEM((T * SUBLANES, LANES), F32), gscr, gscr,
                            pltpu.VMEM((D, D_EXPERT), BF16), pltpu.VMEM((D, D_EXPERT), BF16),
                            pltpu.SemaphoreType.DMA(())]),
        out_shape=[jax.ShapeDtypeStruct((MOE_ROWS, D_EXPERT), BF16), jax.ShapeDtypeStruct((MOE_ROWS,), jnp.int32)],
        compiler_params=_cparams(1),
        name=f"moe_up{layer}",
    )(pos1, pos2, pads, meta, tr, w_gate, w_up)

    return pl.pallas_call(
        _moe_down_kernel,
        grid_spec=pltpu.PrefetchScalarGridSpec(
            num_scalar_prefetch=3, grid=(MOE_TILES + 1,),
            in_specs=[pl.BlockSpec((tm, D_EXPERT), lambda j, *pre: (jnp.minimum(j, MOE_TILES - 1), 0)),
                      wspec((D_EXPERT, D), 3)],
            out_specs=pl.BlockSpec(memory_space=pl.ANY),
            scratch_shapes=[pltpu.VMEM((ACC_ROWS, LANES), F32), gscr, gscr, pltpu.VMEM((D_EXPERT, D), BF16)]),
        out_shape=jax.ShapeDtypeStruct((T * SUBLANES, LANES), F32),
        compiler_params=_cparams(1),
        name=f"moe_down{layer}",
    )(rowmap, cpair, meta, act, w_down)


FINAL_TM = 512


def _final_kernel(x_ref, moe_ref, g2_ref, nw_ref, oc_ref, ol_ref):
    x = x_ref[...] + g2_ref[0] * _tokmajor_to_std(moe_ref, FINAL_TM)
    y = _rms(x, nw_ref[...])
    is_ctx = pl.program_id(0) < T_CTX // FINAL_TM

    @pl.when(is_ctx)
    def _():
        oc_ref[...] = y

    @pl.when(jnp.logical_not(is_ctx))
    def _():
        ol_ref[...] = y


def _final(x, moe, mods, layer, norm_w):
    tm = FINAL_TM
    tile = pl.BlockSpec((tm, D), lambda i: (i, 0))
    return pl.pallas_call(
        _final_kernel,
        grid=(T // tm,),
        in_specs=[tile, pl.BlockSpec((tm * SUBLANES, LANES), lambda i: (i, 0)), _mod_spec(layer, 5, tm),
                  pl.BlockSpec((1, D), lambda i: (0, 0))],
        out_specs=list(_ctx_lat_specs(tm)),
        out_shape=[jax.ShapeDtypeStruct((T_CTX, D), F32), jax.ShapeDtypeStruct((T_LAT, D), F32)],
        compiler_params=_cparams(1),
        name="final_norm",
    )(x, moe, mods, norm_w.reshape(1, D))


def _pad_lanes(v):
    return jnp.pad(v.astype(F32), (0, LANES - v.shape[0])).reshape(1, LANES)


def kernel(x_prompt, x_sample, cache_k_attn, cache_v_attn, state_ssd_fwd, state_ssd_bwd, state_gla_fwd, state_gla_bwd, c, c_ctx, w_ada, b_ada, norm_mix_w, norm_ffn_w, w_in_even, conv_w, conv_b, dt_bias_fwd, dt_bias_bwd, a_log_fwd, a_log_bwd, d_skip, ssd_norm_w, attn_sink, w_out_even, w_in_odd, w_gk2_fwd, b_gk_fwd, w_gk2_bwd, b_gk_bwd, gla_norm_w, w_out_odd, w_router_group, b_router_group, w_router_expert, b_router_expert, w_gate_exp, w_up_exp, w_down_exp, final_norm_w):
    depth = w_ada.shape[0]
    assert depth == 2 and x_prompt.shape == (BATCH, SEQ, D) and x_sample.shape == (DEC_BATCH, DEC_SEQ, D)

    cond8 = jnp.concatenate([c_ctx[None, :], c, jnp.zeros((SUBLANES - 1 - DEC_BATCH, D), F32)], axis=0)
    mods = _adaln(cond8, w_ada, b_ada).reshape(depth * SUBLANES * 6, 1, D)
    xs0 = (x_prompt.reshape(T_CTX, D), x_sample.reshape(T_LAT, D))

    def router_params(i):
        wr = jnp.concatenate([w_router_group[i], w_router_expert[i],
                              jnp.zeros((D, LANES - N_GROUPS - N_EXPERTS), F32)], axis=1)
        return wr, _pad_lanes(jnp.concatenate([b_router_group[i], b_router_expert[i]]))

    def pad_cols(w):
        return jnp.concatenate([w, jnp.zeros((D, LANES - w.shape[1]), F32)], axis=1)

    w = w_in_even[0]
    n_zxbc = 2 * D_SSD + 2 * SSD_GROUPS * SSD_N
    n_dt = 2 * SSD_HEADS
    n_main = n_zxbc // PROJ_TN
    bc_tile, q_tiles = n_main - 1, D // PROJ_TN

    def out_block0(j):
        return jnp.where(j == bc_tile, bc_tile + q_tiles, jnp.where((j > bc_tile) & (j <= bc_tile + q_tiles), j - 1, j))

    p0, p0dt = _modproj(xs0, mods, 0, norm_mix_w[0], w_in_even, n_main, pad_cols(w[:, n_zxbc:n_zxbc + n_dt]),
                        out_block0, w_tail=w[:, n_zxbc + n_dt:])

    dtb = _pad_lanes(jnp.concatenate([dt_bias_fwd[0], dt_bias_bwd[0]]))
    alog = _pad_lanes(jnp.concatenate([a_log_fwd[0], a_log_bwd[0]]))
    dskip = jnp.repeat(d_skip[0], SSD_P).reshape(1, D_SSD)
    ssd_args = (conv_w[0], conv_b[0], dtb, alog, dskip, ssd_norm_w[0])
    y_ssd_c, ssd_f, ssd_b = _ssd(p0, p0dt, 0, BATCH, SEQ, *ssd_args)
    y_ssd_l, _, _ = _ssd(p0, p0dt, T_CTX, DEC_BATCH, DEC_SEQ, *ssd_args,
                         h0f=state_ssd_fwd[:, 0].reshape(DEC_BATCH, D_SSD, SSD_N),
                         h0b=state_ssd_bwd[:, 0].reshape(DEC_BATCH, D_SSD, SSD_N))
    sink = _pad_lanes(attn_sink[0])
    y_att_c = _ctx_attn(p0, sink)
    y_att_l = _lat_attn(p0, cache_k_attn[:, 0].reshape(DEC_BATCH, PAST_LEN, ATT_KV_DIM),
                        cache_v_attn[:, 0].reshape(DEC_BATCH, PAST_LEN, ATT_KV_DIM), sink)
    xmid0, tr0, route0, cnt0 = _outproj_route([y_ssd_c, y_ssd_l, y_att_c, y_att_l], w_out_even[0], xs0, mods, 0,
                                              norm_ffn_w[0], *router_params(0))
    moe0 = _moe(tr0, route0, cnt0, 0, w_gate_exp, w_up_exp, w_down_exp)

    dk_all = GLA_HEADS * GLA_DK
    n_qkvr = 2 * dk_all + 2 * GLA_HEADS * GLA_DV
    p1, p1lr, x1 = _modproj(xmid0, mods, 1, norm_mix_w[1], w_in_odd, n_qkvr // PROJ_TN,
                            pad_cols(w_in_odd[0][:, n_qkvr:]), lambda j: j, moe=moe0)
    w2f = jnp.zeros((LANES, dk_all), F32).at[:GLA_LOWRANK].set(w_gk2_fwd[0])
    w2b = jnp.zeros((LANES, dk_all), F32).at[GLA_LOWRANK:2 * GLA_LOWRANK].set(w_gk2_bwd[0])
    gla_args = (w2f, w2b, b_gk_fwd[0], b_gk_bwd[0], gla_norm_w[0])
    y_gla_c, gla_f, gla_b = _gla(p1, p1lr, 0, BATCH, SEQ, *gla_args)
    y_gla_l, _, _ = _gla(p1, p1lr, T_CTX, DEC_BATCH, DEC_SEQ, *gla_args,
                         s0f=state_gla_fwd[:, 0].reshape(DEC_BATCH, dk_all, GLA_DV),
                         s0b=state_gla_bwd[:, 0].reshape(DEC_BATCH, dk_all, GLA_DV))
    xmid1, tr1, route1, cnt1 = _outproj_route([y_gla_c, y_gla_l], w_out_odd[0], (x1,), mods, 1,
                                              norm_ffn_w[1], *router_params(1))
    moe1 = _moe(tr1, route1, cnt1, 1, w_gate_exp, w_up_exp, w_down_exp)
    y_c, y_l = _final(xmid1, moe1, mods, 1, final_norm_w)

    y_prompt = y_c.reshape(BATCH, SEQ, D)
    y_sample = y_l.reshape(DEC_BATCH, DEC_SEQ, D)
    new_k = p0[:T_CTX, P0_K:P0_K + ATT_KV_DIM].astype(F32).reshape(BATCH, 1, SEQ, ATT_KV, ATT_HD)
    new_v = p0[:T_CTX, P0_V:P0_V + ATT_KV_DIM].astype(F32).reshape(BATCH, 1, SEQ, ATT_KV, ATT_HD)
    return (y_prompt, y_sample, new_k, new_v,
            ssd_f.reshape(BATCH, 1, SSD_HEADS, SSD_P, SSD_N), ssd_b.reshape(BATCH, 1, SSD_HEADS, SSD_P, SSD_N),
            gla_f.reshape(BATCH, 1, GLA_HEADS, GLA_DK, GLA_DV), gla_b.reshape(BATCH, 1, GLA_HEADS, GLA_DK, GLA_DV))
```

```python
import functools
import math

import numpy as np
import jax
import jax.numpy as jnp
from jax import lax
from jax.experimental import pallas as pl
from jax.experimental.pallas import tpu as pltpu

F32 = jnp.float32
BF16 = jnp.bfloat16

D = 1024
BATCH, SEQ = 16, 256
DEC_BATCH, DEC_SEQ = 2, 1024
PAST_LEN = 512
GRID_W = 64
EPS = 1e-6
T_CTX = BATCH * SEQ
T_LAT = DEC_BATCH * DEC_SEQ
T = T_CTX + T_LAT

SSD_HEADS, SSD_P, SSD_N, SSD_GROUPS = 16, 64, 128, 2
SSD_CONV = 5
SSD_L = 128
D_SSD = SSD_HEADS * SSD_P
HEADS_PER_GROUP = SSD_HEADS // SSD_GROUPS
GROUP_W = HEADS_PER_GROUP * SSD_P

ATT_HEADS, ATT_KV, ATT_HD = 16, 4, 64
ATT_KV_DIM = ATT_KV * ATT_HD
WINDOW = 128
ATT_BLOCK = 128
ATT_SCALE = ATT_HD ** -0.5
ROPE_BASE = 10000.0

GLA_HEADS, GLA_DK, GLA_DV = 4, 128, 256
GLA_C = 64
GLA_GATE_NORM = 16.0
GLA_LOWRANK = 16

N_GROUPS, EXP_PER_GROUP = 4, 4
N_EXPERTS = 16
D_EXPERT = 512

LANES = 128
SUBLANES = 8
VMEM_LIMIT = 56 * 1024 * 1024

P0_Z, P0_X, P0_Q, P0_BC, P0_K, P0_V = 0, 1024, 2048, 3072, 3584, 3840
P0_W = 4096
P1_Q, P1_K, P1_V, P1_R = 0, 512, 1024, 2048
P1_W = 3072

MOE_TM = 256
MOE_TILES = (2 * T) // MOE_TM + N_EXPERTS
MOE_ROWS = MOE_TILES * MOE_TM
ACC_ROWS = (T + SUBLANES) * SUBLANES


def _cparams(n_axes, vmem=VMEM_LIMIT):
    return pltpu.CompilerParams(dimension_semantics=("arbitrary",) * n_axes, vmem_limit_bytes=vmem)


def _silu(x):
    return x / (1.0 + jnp.exp(-x))


def _softplus(x):
    return jnp.maximum(x, 0.0) + jnp.log(1.0 + jnp.exp(-jnp.abs(x)))


def _mm(a, b):
    return jnp.dot(a.astype(BF16), b.astype(BF16), preferred_element_type=F32)


def _mm_nt(a, b):
    return lax.dot_general(a.astype(BF16), b.astype(BF16), (((1,), (1,)), ((), ())),
                           preferred_element_type=F32)


def _mm_tn(a, b):
    return lax.dot_general(a.astype(BF16), b.astype(BF16), (((0,), (0,)), ((), ())),
                           preferred_element_type=F32)


def _rms(x, w):
    return x * lax.rsqrt(jnp.mean(x * x, axis=-1, keepdims=True) + EPS) * w


def _cumsum_rows(x, n):
    row = lax.broadcasted_iota(jnp.int32, x.shape, 0)
    s = 1
    while s < n:
        x = x + jnp.where(row >= s, pltpu.roll(x, s, 0), 0.0)
        s *= 2
    return x


def _mod_row(tok0):
    return jnp.where(tok0 < T_CTX, 0, 1 + (tok0 - T_CTX) // DEC_SEQ)


ADA_TN = 1536


def _adaln_kernel(c_ref, w_ref, b_ref, o_ref):
    s = _silu(c_ref[...])
    o_ref[0] = _mm(s, w_ref[0]) + b_ref[0]


def _adaln(cond8, w_ada, b_ada):
    depth = w_ada.shape[0]
    return pl.pallas_call(
        _adaln_kernel,
        grid=(depth, 6 * D // ADA_TN),
        in_specs=[
            pl.BlockSpec((SUBLANES, D), lambda l, j: (0, 0)),
            pl.BlockSpec((1, D, ADA_TN), lambda l, j: (l, 0, j)),
            pl.BlockSpec((1, 1, ADA_TN), lambda l, j: (l, 0, j)),
        ],
        out_specs=pl.BlockSpec((1, SUBLANES, ADA_TN), lambda l, j: (l, 0, j)),
        out_shape=jax.ShapeDtypeStruct((depth, SUBLANES, 6 * D), F32),
        compiler_params=_cparams(2),
        name="adaln",
    )(cond8, w_ada, b_ada.reshape(depth, 1, 6 * D))


def _mod_spec(layer, chunk, tm, tile_of=lambda i, *_: i):
    return pl.BlockSpec((1, 1, D), lambda *g: ((layer * SUBLANES + _mod_row(tile_of(*g) * tm)) * 6 + chunk, 0, 0))


def _tokmajor_to_std(ref, tm):
    return jnp.concatenate([ref[pl.ds(k, tm, stride=SUBLANES), :] for k in range(D // LANES)], axis=1)


PROJ_TM = 1024
PROJ_TN = 512


def _ctx_lat_specs(tm, width=D):
    n_ctx = T_CTX // tm
    return (pl.BlockSpec((tm, width), lambda i, *_: (jnp.minimum(i, n_ctx - 1), 0)),
            pl.BlockSpec((tm, width), lambda i, *_: (jnp.maximum(i - n_ctx, 0), 0)))


def _modproj_kernel(*refs, dual_x, n_main, has_tail):
    it = iter(refs)
    if dual_x:
        xc_ref, xl_ref = next(it), next(it)
    else:
        x_ref, moe_ref, g2_ref = next(it), next(it), next(it)
    sh_ref, sc_ref, nw_ref, wm_ref = next(it), next(it), next(it), next(it)
    wt_ref = next(it) if has_tail else None
    ws_ref, o_ref, os_ref = next(it), next(it), next(it)
    xo_ref = None if dual_x else next(it)
    h_all, w_bf = next(it), next(it)
    j, i = pl.program_id(0), pl.program_id(1)
    tm = PROJ_TM
    rows = pl.ds(pl.multiple_of(i * tm, tm), tm)

    @pl.when(j == 0)
    def _():
        if dual_x:
            x = jnp.where(i < T_CTX // tm, xc_ref[...], xl_ref[...])
        else:
            x = x_ref[...] + g2_ref[0] * _tokmajor_to_std(moe_ref, tm)
            xo_ref[...] = x
        h = (_rms(x, nw_ref[...]) * (1.0 + sc_ref[0]) + sh_ref[0]).astype(BF16)
        h_all[rows, :] = h
        os_ref[...] = jnp.dot(h, ws_ref[...].astype(BF16), preferred_element_type=F32)

    @pl.when(i == 0)
    def _():
        def load(w_ref):
            w_bf[...] = w_ref[...].astype(BF16)

        if has_tail:
            pl.when(j < n_main)(functools.partial(load, wm_ref))
            pl.when(j >= n_main)(functools.partial(load, wt_ref))
        else:
            load(wm_ref)

    o_ref[...] = jnp.dot(h_all[rows, :], w_bf[...], preferred_element_type=F32).astype(o_ref.dtype)


def _modproj(xs, mods, layer, norm_w, w_main, n_main, w_small, out_block, w_tail=None, moe=None):
    tm, tn = PROJ_TM, PROJ_TN
    dual_x = moe is None
    has_tail = w_tail is not None
    n_tiles = n_main + (w_tail.shape[1] // tn if has_tail else 0)
    n_i, n_ctx = T // tm, T_CTX // tm

    def tok(j, i):
        return jnp.where(j == 0, i, n_i - 1)

    tile = pl.BlockSpec((tm, D), lambda j, i: (tok(j, i), 0))
    if dual_x:
        in_specs = [pl.BlockSpec((tm, D), lambda j, i: (jnp.minimum(tok(j, i), n_ctx - 1), 0)),
                    pl.BlockSpec((tm, D), lambda j, i: (jnp.maximum(tok(j, i) - n_ctx, 0), 0))]
        args = list(xs)
    else:
        in_specs = [tile, pl.BlockSpec((tm * SUBLANES, LANES), lambda j, i: (tok(j, i), 0)),
                    _mod_spec(layer - 1, 5, tm, tok)]
        args = [xs, moe, mods]
    in_specs += [_mod_spec(layer, 0, tm, tok), _mod_spec(layer, 1, tm, tok), pl.BlockSpec((1, D), lambda j, i: (0, 0)),
                 pl.BlockSpec((None, D, tn), lambda j, i: (0, 0, jnp.minimum(j, n_main - 1)))]
    args += [mods, mods, norm_w.reshape(1, D), w_main]
    if has_tail:
        in_specs.append(pl.BlockSpec((D, tn), lambda j, i: (0, jnp.maximum(j - n_main, 0))))
        args.append(w_tail)
    in_specs.append(pl.BlockSpec((D, LANES), lambda j, i: (0, 0)))
    args.append(w_small)
    out_specs = [pl.BlockSpec((tm, tn), lambda j, i: (i, out_block(j))),
                 pl.BlockSpec((tm, LANES), lambda j, i: (tok(j, i), 0))]
    out_shape = [jax.ShapeDtypeStruct((T, n_tiles * tn), BF16), jax.ShapeDtypeStruct((T, LANES), F32)]
    if not dual_x:
        out_specs.append(tile)
        out_shape.append(jax.ShapeDtypeStruct((T, D), F32))
    return pl.pallas_call(
        functools.partial(_modproj_kernel, dual_x=dual_x, n_main=n_main, has_tail=has_tail),
        grid=(n_tiles, n_i), in_specs=in_specs, out_specs=out_specs, out_shape=out_shape,
        scratch_shapes=[pltpu.VMEM((T, D), BF16), pltpu.VMEM((D, tn), BF16)],
        compiler_params=_cparams(2),
        name=f"modproj{layer}",
    )(*args)


def _expand_heads(v, off):
    lo = lax.broadcasted_iota(jnp.int32, (v.shape[0], LANES), 1) < SSD_P
    tiles = []
    for q in range(SSD_HEADS // 2):
        a = jnp.broadcast_to(v[:, off + 2 * q:off + 2 * q + 1], (v.shape[0], LANES))
        b = jnp.broadcast_to(v[:, off + 2 * q + 1:off + 2 * q + 2], (v.shape[0], LANES))
        tiles.append(jnp.where(lo, a, b))
    return jnp.concatenate(tiles, axis=1)


def _ssd_kernel(*refs, seq, has_h0):
    if has_h0:
        (z_ref, x_ref, bc_ref, dt_ref, cwx_ref, cwbc_ref, cbx_ref, cbbc_ref, dtb_ref, alog_ref, dsk_ref,
         nw_ref, h0f_ref, h0b_ref, y_ref, sf_ref, sb_ref,
         xpad, bcpad, xc, bcc, a_scr, dt_scr, yacc, hf, hb) = refs
    else:
        (z_ref, x_ref, bc_ref, dt_ref, cwx_ref, cwbc_ref, cbx_ref, cbbc_ref, dtb_ref, alog_ref, dsk_ref,
         nw_ref, y_ref, sf_ref, sb_ref,
         xpad, bcpad, xc, bcc, a_scr, dt_scr, yacc, hf, hb) = refs
    L = SSD_L
    nc = seq // L
    pad = SUBLANES
    half = SSD_CONV // 2

    for buf, src, cw, cb, dst in ((xpad, x_ref, cwx_ref, cbx_ref, xc), (bcpad, bc_ref, cwbc_ref, cbbc_ref, bcc)):
        width = buf.shape[1]
        buf[0:pad, :] = jnp.zeros((pad, width), F32)
        buf[pad + seq:2 * pad + seq, :] = jnp.zeros((pad, width), F32)
        buf[pad:pad + seq, :] = src[...].astype(F32)
        for blk in range(nc):
            acc = jnp.broadcast_to(cb[...], (L, width))
            for j in range(SSD_CONV):
                r0 = pad - half + j + blk * L
                acc = acc + cw[j:j + 1, :] * buf[r0:r0 + L, :]
            dst[blk * L:(blk + 1) * L, :] = _silu(acc)

    lane = lax.broadcasted_iota(jnp.int32, (seq, LANES), 1)
    dts = jnp.where(lane < 2 * SSD_HEADS, _softplus(dt_ref[...] + dtb_ref[...]), 0.0)
    dt_scr[...] = dts
    a_scr[...] = dts * (-jnp.exp(alog_ref[...]))

    if has_h0:
        hf[...] = h0f_ref[...]
        hb[...] = h0b_ref[...]
    else:
        hf[...] = jnp.zeros(hf.shape, F32)
        hb[...] = jnp.zeros(hb.shape, F32)

    row = lax.broadcasted_iota(jnp.int32, (L, L), 0)
    col = lax.broadcasted_iota(jnp.int32, (L, L), 1)
    lane_l = lax.broadcasted_iota(jnp.int32, (L, LANES), 1)
    lo_half = lane_l < SSD_P

    def chunk(c, fwd, h_scr):
        off = 0 if fwd else SSD_HEADS
        r0 = pl.multiple_of(c * L, L)
        a = a_scr[pl.ds(r0, L), :]
        dt = dt_scr[pl.ds(r0, L), :]
        cs = _cumsum_rows(a, L)
        total = cs[L - 1:L, :]
        if fwd:
            u = cs
            rvec = jnp.exp(cs)
            ed = jnp.exp(total - cs) * dt
            keep = col <= row
        else:
            ex = cs - a
            u = -ex
            rvec = jnp.exp(total - ex)
            ed = jnp.exp(ex) * dt
            keep = col >= row
        ut = jnp.transpose(u)
        dtt = jnp.transpose(dt)
        tcol = jnp.transpose(jnp.broadcast_to(total, (L, LANES)))[:, 0:1]
        rexp = _expand_heads(rvec, off)
        edexp = _expand_heads(ed, off)
        x = xc[pl.ds(r0, L), :]
        bc = bcc[pl.ds(r0, L), :]
        outs = []
        for g in range(SSD_GROUPS):
            bg = bc[:, g * SSD_N:(g + 1) * SSD_N]
            cg = bc[:, SSD_GROUPS * SSD_N + g * SSD_N:SSD_GROUPS * SSD_N + (g + 1) * SSD_N]
            cbm = _mm_nt(cg, bg)
            hg = h_scr[g * GROUP_W:(g + 1) * GROUP_W, :]
            xg = x[:, g * GROUP_W:(g + 1) * GROUP_W]
            y_off = _mm_nt(cg, hg) * rexp[:, g * GROUP_W:(g + 1) * GROUP_W]
            tiles = []
            for p in range(HEADS_PER_GROUP // 2):
                xt = xg[:, p * LANES:(p + 1) * LANES]
                acc = None
                for s in range(2):
                    h = off + g * HEADS_PER_GROUP + 2 * p + s
                    seg = u[:, h:h + 1] - ut[h:h + 1, :]
                    m = cbm * jnp.exp(jnp.where(keep, seg, -jnp.inf)) * dtt[h:h + 1, :]
                    xm = jnp.where(lo_half if s == 0 else jnp.logical_not(lo_half), xt, 0.0)
                    d = _mm(m, xm)
                    acc = d if acc is None else acc + d
                tiles.append(acc)
            outs.append(y_off + jnp.concatenate(tiles, axis=1))
            decs = []
            for hh in range(HEADS_PER_GROUP):
                h = off + g * HEADS_PER_GROUP + hh
                decs.append(jnp.broadcast_to(jnp.exp(tcol[h:h + 1, :]), (SSD_P, SSD_N)))
            dec = jnp.concatenate(decs, axis=0)
            h_scr[g * GROUP_W:(g + 1) * GROUP_W, :] = dec * hg + _mm_tn(xg * edexp[:, g * GROUP_W:(g + 1) * GROUP_W], bg)
        return r0, x, jnp.concatenate(outs, axis=1)

    def fwd_body(c, carry):
        r0, _, y = chunk(c, True, hf)
        yacc[pl.ds(r0, L), :] = y
        return carry

    lax.fori_loop(0, nc, fwd_body, 0)

    def bwd_body(i, carry):
        c = nc - 1 - i
        r0, x, y = chunk(c, False, hb)
        y = yacc[pl.ds(r0, L), :] + y + dsk_ref[...] * x
        y = y * _silu(z_ref[pl.ds(r0, L), :].astype(F32))
        y_ref[pl.ds(r0, L), :] = _rms(y, nw_ref[...]).astype(y_ref.dtype)
        return carry

    lax.fori_loop(0, nc, bwd_body, 0)
    sf_ref[...] = hf[...]
    sb_ref[...] = hb[...]


def _ssd(p0, p0dt, tok0, nseq, seq, cw, cb, dtb, alog, dskip, nw, h0f=None, h0b=None):
    has_h0 = h0f is not None
    b0 = tok0 // seq

    def cols(width, start):
        return pl.BlockSpec((seq, width), lambda s: (b0 + s, start // width))

    def full(shape):
        return pl.BlockSpec(shape, lambda s: (0,) * len(shape))

    in_specs = [cols(D_SSD, P0_Z), cols(D_SSD, P0_X), cols(512, P0_BC), cols(LANES, 0),
                full((SSD_CONV, D_SSD)), full((SSD_CONV, 512)), full((1, D_SSD)), full((1, 512)),
                full((1, LANES)), full((1, LANES)), full((1, D_SSD)), full((1, D_SSD))]
    args = [p0, p0, p0, p0dt, cw[:, :D_SSD], cw[:, D_SSD:], cb[:D_SSD].reshape(1, -1), cb[D_SSD:].reshape(1, -1),
            dtb, alog, dskip, nw.reshape(1, -1)]
    st_spec = pl.BlockSpec((None, D_SSD, SSD_N), lambda s: (s, 0, 0))
    if has_h0:
        in_specs += [st_spec, st_spec]
        args += [h0f, h0b]
    st_shape = jax.ShapeDtypeStruct((nseq, D_SSD, SSD_N), F32)
    return pl.pallas_call(
        functools.partial(_ssd_kernel, seq=seq, has_h0=has_h0),
        grid=(nseq,), in_specs=in_specs,
        out_specs=[pl.BlockSpec((seq, D_SSD), lambda s: (s, 0)), st_spec, st_spec],
        out_shape=[jax.ShapeDtypeStruct((nseq * seq, D_SSD), BF16), st_shape, st_shape],
        scratch_shapes=[pltpu.VMEM((seq + 2 * SUBLANES, D_SSD), F32), pltpu.VMEM((seq + 2 * SUBLANES, 512), F32),
                        pltpu.VMEM((seq, D_SSD), F32), pltpu.VMEM((seq, 512), F32),
                        pltpu.VMEM((seq, LANES), F32), pltpu.VMEM((seq, LANES), F32),
                        pltpu.VMEM((seq, D_SSD), F32),
                        pltpu.VMEM((D_SSD, SSD_N), F32), pltpu.VMEM((D_SSD, SSD_N), F32)],
        compiler_params=_cparams(1),
        name=f"ssd{seq}",
    )(*args)


def _place_halves(tile, kv_in_high):
    lo = lax.broadcasted_iota(jnp.int32, tile.shape, 1) < ATT_HD
    swapped = pltpu.roll(tile, ATT_HD, 1)
    if kv_in_high:
        return jnp.where(lo, swapped, 0.0), jnp.where(lo, 0.0, tile)
    return jnp.where(lo, tile, 0.0), jnp.where(lo, 0.0, swapped)


def _place_rows(vt, kv_in_high):
    head = vt[ATT_HD:, :] if kv_in_high else vt[:ATT_HD, :]
    z = jnp.zeros_like(head)
    return jnp.concatenate([head, z], axis=0), jnp.concatenate([z, head], axis=0)


LOG2E = 1.4426950408889634
SCORE_SCALE = ATT_SCALE * LOG2E


def _sink_attend_t(score_parts, value_parts, sink2):
    m = sink2
    for s in score_parts:
        m = jnp.maximum(m, jnp.max(s, axis=0, keepdims=True))
    denom = jnp.exp2(sink2 - m)
    out = None
    for s, v in zip(score_parts, value_parts):
        p = jnp.exp2(s - m)
        denom = denom + jnp.sum(p, axis=0, keepdims=True)
        o = _mm(v, p)
        out = o if out is None else out + o
    return out * (1.0 / denom)


def _attn_schedule(n, scores, attend):
    scores(0)
    for j in range(n):
        if j + 1 < n:
            scores(j + 1)
        attend(j)


def _ctx_attn_kernel(q_ref, k_ref, v_ref, sink_ref, o_ref, s_a, s_b):
    sink2 = sink_ref[...] * LOG2E
    bufs = (s_a, s_b)
    half = SEQ // 2

    def kv_tile(ref, j):
        return ref[:, (j // 2) * LANES:(j // 2 + 1) * LANES].astype(F32), (j % 2 == 1)

    def scores(j):
        k_lo, k_hi = _place_halves(*kv_tile(k_ref, j))
        qst = jnp.concatenate([q_ref[:, qt * LANES:(qt + 1) * LANES] for qt in (2 * j, 2 * j + 1)], axis=0)
        bufs[j % 2][...] = _mm_nt(jnp.concatenate([k_lo, k_hi], axis=0), qst) * SCORE_SCALE

    def attend(j):
        src = bufs[j % 2]
        v, high = kv_tile(v_ref, j)
        vts = _place_rows(jnp.transpose(v), high)
        for ql in range(2):
            qt = 2 * j + ql
            for qh in range(2):
                cols = slice(ql * SEQ + qh * half, ql * SEQ + (qh + 1) * half)
                acc = None
                for s, vv in enumerate(vts):
                    o = _sink_attend_t([src[s * SEQ:(s + 1) * SEQ, cols]], [vv], sink2[:, 2 * qt + s:2 * qt + s + 1])
                    acc = o if acc is None else acc + o
                o_ref[qh * half:(qh + 1) * half, qt * LANES:(qt + 1) * LANES] = jnp.transpose(acc).astype(o_ref.dtype)

    _attn_schedule(ATT_KV, scores, attend)


def _ctx_attn(p0, sink):
    def cols(width, start):
        return pl.BlockSpec((SEQ, width), lambda b: (b, start // width))

    sbuf = pltpu.VMEM((2 * SEQ, 2 * SEQ), F32)
    return pl.pallas_call(
        _ctx_attn_kernel,
        grid=(BATCH,),
        in_specs=[cols(D, P0_Q), cols(ATT_KV_DIM, P0_K), cols(ATT_KV_DIM, P0_V),
                  pl.BlockSpec((1, LANES), lambda b: (0, 0))],
        out_specs=pl.BlockSpec((SEQ, D), lambda b: (b, 0)),
        out_shape=jax.ShapeDtypeStruct((T_CTX, D), BF16),
        scratch_shapes=[sbuf, sbuf],
        compiler_params=_cparams(1),
        name="ctx_attn",
    )(p0, p0, p0, sink)


def _rope_tables():
    quarter = ATT_HD // 4
    t = np.arange(DEC_SEQ)
    lane = np.arange(LANES)
    inv = ROPE_BASE ** (-(lane % quarter).astype(np.float64) / quarter)
    pos = np.where(((lane % ATT_HD) < ATT_HD // 2)[None, :], (t // GRID_W)[:, None], (t % GRID_W)[:, None])
    ang = pos * inv[None, :]
    first = ((lane % (2 * quarter)) < quarter)[None, :]
    cos, sin = np.cos(ang), np.sin(ang)
    return (jnp.asarray(cos, F32), jnp.asarray(np.where(first, -sin, 0.0), F32),
            jnp.asarray(np.where(first, 0.0, sin), F32))


def _rope(x, cos, sa, sb):
    quarter = ATT_HD // 4
    return x * cos + pltpu.roll(x, LANES - quarter, 1) * sa + pltpu.roll(x, quarter, 1) * sb


def _lat_attn_kernel(q_ref, kp_ref, kc_ref, kn_ref, vp_ref, vc_ref, vn_ref, ck_ref, cv_ref,
                     cos_ref, sa_ref, sb_ref, sink_ref, o_ref, c_a, c_b, w_a, w_b):
    blk = pl.program_id(1)
    nb = pl.num_programs(1)
    B = ATT_BLOCK
    sink2 = sink_ref[...] * LOG2E
    cbufs, wbufs = (c_a, c_b), (w_a, w_b)

    def tables(b):
        r0 = pl.multiple_of(b * B, B)
        return cos_ref[pl.ds(r0, B), :], sa_ref[pl.ds(r0, B), :], sb_ref[pl.ds(r0, B), :]

    tq = tables(blk)
    tk = [tables(jnp.maximum(blk - 1, 0)), tq, tables(jnp.minimum(blk + 1, nb - 1))]
    kabs = (blk - 1) * B + lax.broadcasted_iota(jnp.int32, (3 * B, B), 0)
    qpos = blk * B + lax.broadcasted_iota(jnp.int32, (3 * B, B), 1)
    ok = (jnp.abs(qpos - kabs) <= WINDOW) & (kabs >= 0) & (kabs < nb * B)
    ok = jnp.concatenate([ok, ok], axis=1)
    ok = jnp.concatenate([ok, ok], axis=0)

    def scores(j):
        high = (j % 2 == 1)
        sl = slice((j // 2) * LANES, (j // 2 + 1) * LANES)
        kw = jnp.concatenate([_rope(r[:, sl].astype(F32), *tb) for r, tb in zip((kp_ref, kc_ref, kn_ref), tk)], axis=0)
        qs = [q_ref[:, qt * LANES:(qt + 1) * LANES].astype(F32) for qt in (2 * j, 2 * j + 1)]
        q_plain = jnp.concatenate(qs, axis=0)
        q_rope = jnp.concatenate([_rope(q, *tq) for q in qs], axis=0)
        cbufs[j % 2][...] = _mm_nt(jnp.concatenate(_place_halves(ck_ref[:, sl], high), axis=0), q_plain) * SCORE_SCALE
        win = _mm_nt(jnp.concatenate(_place_halves(kw, high), axis=0), q_rope) * SCORE_SCALE
        wbufs[j % 2][...] = jnp.where(ok, win, -jnp.inf)

    def attend(j):
        high = (j % 2 == 1)
        sl = slice((j // 2) * LANES, (j // 2 + 1) * LANES)
        vw = jnp.concatenate([jnp.transpose(r[:, sl].astype(F32)) for r in (vp_ref, vc_ref, vn_ref)], axis=1)
        vts = _place_rows(vw, high)
        cvts = _place_rows(jnp.transpose(cv_ref[:, sl]), high)
        csrc, wsrc = cbufs[j % 2], wbufs[j % 2]
        for ql in range(2):
            qt = 2 * j + ql
            cols = slice(ql * B, (ql + 1) * B)
            acc = None
            for s in range(2):
                parts = [csrc[s * PAST_LEN:(s + 1) * PAST_LEN, cols], wsrc[s * 3 * B:(s + 1) * 3 * B, cols]]
                o = _sink_attend_t(parts, [cvts[s], vts[s]], sink2[:, 2 * qt + s:2 * qt + s + 1])
                acc = o if acc is None else acc + o
            o_ref[:, qt * LANES:(qt + 1) * LANES] = jnp.transpose(acc).astype(o_ref.dtype)

    _attn_schedule(ATT_KV, scores, attend)


def _lat_attn(p0, ck, cv, sink):
    nb = DEC_SEQ // ATT_BLOCK
    base = T_CTX // ATT_BLOCK

    def kv(start, shift):
        return pl.BlockSpec((ATT_BLOCK, ATT_KV_DIM),
                            lambda b, i: (base + b * nb + jnp.clip(i + shift, 0, nb - 1), start // ATT_KV_DIM))

    def full(shape):
        return pl.BlockSpec(shape, lambda b, i: (0,) * len(shape))

    cache = pl.BlockSpec((None, PAST_LEN, ATT_KV_DIM), lambda b, i: (b, 0, 0))
    cos, sa, sb = _rope_tables()
    return pl.pallas_call(
        _lat_attn_kernel,
        grid=(DEC_BATCH, nb),
        in_specs=[pl.BlockSpec((ATT_BLOCK, D), lambda b, i: (base + b * nb + i, P0_Q // D)),
                  kv(P0_K, -1), kv(P0_K, 0), kv(P0_K, 1), kv(P0_V, -1), kv(P0_V, 0), kv(P0_V, 1),
                  cache, cache, full((DEC_SEQ, LANES)), full((DEC_SEQ, LANES)), full((DEC_SEQ, LANES)),
                  full((1, LANES))],
        out_specs=pl.BlockSpec((ATT_BLOCK, D), lambda b, i: (b * nb + i, 0)),
        out_shape=jax.ShapeDtypeStruct((T_LAT, D), BF16),
        scratch_shapes=[pltpu.VMEM((2 * PAST_LEN, 2 * ATT_BLOCK), F32)] * 2
        + [pltpu.VMEM((2 * 3 * ATT_BLOCK, 2 * ATT_BLOCK), F32)] * 2,
        compiler_params=_cparams(2),
        name="lat_attn",
    )(p0, p0, p0, p0, p0, p0, p0, ck, cv, cos, sa, sb, sink)


def _log_sigmoid(x):
    return jnp.minimum(x, 0.0) - jnp.log(1.0 + jnp.exp(-jnp.abs(x)))


def _gla_kernel(*refs, seq, has_s0):
    if has_s0:
        (q_ref, k_ref, v_ref, r_ref, lr_ref, w2f_ref, w2b_ref, bf_ref, bb_ref, nw_ref, s0f_ref, s0b_ref,
         y_ref, sf_ref, sb_ref, gf, gb, yf, yb, stf, stb) = refs
    else:
        (q_ref, k_ref, v_ref, r_ref, lr_ref, w2f_ref, w2b_ref, bf_ref, bb_ref, nw_ref,
         y_ref, sf_ref, sb_ref, gf, gb, yf, yb, stf, stb) = refs
    C = GLA_C
    nc = seq // C
    lr = lr_ref[...]
    gf[...] = _log_sigmoid(_mm(lr, w2f_ref[...]) + bf_ref[...]) / GLA_GATE_NORM
    gb[...] = _log_sigmoid(_mm(lr, w2b_ref[...]) + bb_ref[...]) / GLA_GATE_NORM
    for h in range(GLA_HEADS):
        rows = slice(h * GLA_DV, (h + 1) * GLA_DV)
        if has_s0:
            stf[rows, :] = jnp.transpose(s0f_ref[h * GLA_DK:(h + 1) * GLA_DK, :])
            stb[rows, :] = jnp.transpose(s0b_ref[h * GLA_DK:(h + 1) * GLA_DK, :])
        else:
            stf[rows, :] = jnp.zeros((GLA_DV, GLA_DK), F32)
            stb[rows, :] = jnp.zeros((GLA_DV, GLA_DK), F32)

    row = lax.broadcasted_iota(jnp.int32, (C, C), 0)
    col = lax.broadcasted_iota(jnp.int32, (C, C), 1)
    qscale = GLA_DK ** -0.5

    def chunk(c, fwd):
        g_scr, y_scr, st = (gf, yf, stf) if fwd else (gb, yb, stb)
        r0 = pl.multiple_of(c * C, C)
        g = g_scr[pl.ds(r0, C), :]
        cs = _cumsum_rows(g, C)
        total = cs[C - 1:C, :]
        q = q_ref[pl.ds(r0, C), :].astype(F32) * qscale
        k = k_ref[pl.ds(r0, C), :].astype(F32)
        v = v_ref[pl.ds(r0, C), :]
        if fwd:
            qs, ks, ke = q * jnp.exp(cs), k * jnp.exp(-cs), k * jnp.exp(total - cs)
            keep = col <= row
        else:
            ex = cs - g
            qs, ks, ke = q * jnp.exp(total - ex), k * jnp.exp(ex - total), k * jnp.exp(ex)
            keep = col >= row
        dec = jnp.exp(total)
        for h in range(GLA_HEADS):
            kc = slice(h * GLA_DK, (h + 1) * GLA_DK)
            vc = slice(h * GLA_DV, (h + 1) * GLA_DV)
            s_t = st[vc, :]
            att = jnp.where(keep, _mm_nt(qs[:, kc], ks[:, kc]), 0.0)
            y_scr[pl.ds(r0, C), vc] = _mm(att, v[:, vc]) + _mm_nt(qs[:, kc], s_t)
            st[vc, :] = dec[:, kc] * s_t + _mm_tn(v[:, vc], ke[:, kc])

    def body(i, carry):
        chunk(i, True)
        chunk(nc - 1 - i, False)
        return carry

    lax.fori_loop(0, nc, body, 0)

    nw = nw_ref[...]
    for blk in range(seq // LANES):
        rs = slice(blk * LANES, (blk + 1) * LANES)
        y = yf[rs, :] + yb[rs, :]
        gate = _silu(r_ref[rs, :].astype(F32))
        for h in range(GLA_HEADS):
            vc = slice(h * GLA_DV, (h + 1) * GLA_DV)
            y_ref[rs, vc] = (_rms(y[:, vc], nw) * gate[:, vc]).astype(y_ref.dtype)
    for h in range(GLA_HEADS):
        rows = slice(h * GLA_DV, (h + 1) * GLA_DV)
        sf_ref[h * GLA_DK:(h + 1) * GLA_DK, :] = jnp.transpose(stf[rows, :])
        sb_ref[h * GLA_DK:(h + 1) * GLA_DK, :] = jnp.transpose(stb[rows, :])


def _gla(p1, p1lr, tok0, nseq, seq, w2f, w2b, bgf, bgb, nw, s0f=None, s0b=None):
    has_s0 = s0f is not None
    b0 = tok0 // seq
    dk_all, dv_all = GLA_HEADS * GLA_DK, GLA_HEADS * GLA_DV

    def cols(width, start):
        return pl.BlockSpec((seq, width), lambda s: (b0 + s, start // width))

    def full(shape):
        return pl.BlockSpec(shape, lambda s: (0,) * len(shape))

    in_specs = [cols(dk_all, P1_Q), cols(dk_all, P1_K), cols(dv_all, P1_V), cols(dv_all, P1_R), cols(LANES, 0),
                full((LANES, dk_all)), full((LANES, dk_all)), full((1, dk_all)), full((1, dk_all)), full((1, GLA_DV))]
    args = [p1, p1, p1, p1, p1lr, w2f, w2b, bgf.reshape(1, -1), bgb.reshape(1, -1), nw.reshape(1, -1)]
    st_spec = pl.BlockSpec((None, dk_all, GLA_DV), lambda s: (s, 0, 0))
    if has_s0:
        in_specs += [st_spec, st_spec]
        args += [s0f, s0b]
    st_shape = jax.ShapeDtypeStruct((nseq, dk_all, GLA_DV), F32)
    return pl.pallas_call(
        functools.partial(_gla_kernel, seq=seq, has_s0=has_s0),
        grid=(nseq,), in_specs=in_specs,
        out_specs=[pl.BlockSpec((seq, dv_all), lambda s: (s, 0)), st_spec, st_spec],
        out_shape=[jax.ShapeDtypeStruct((nseq * seq, dv_all), BF16), st_shape, st_shape],
        scratch_shapes=[pltpu.VMEM((seq, dk_all), F32), pltpu.VMEM((seq, dk_all), F32),
                        pltpu.VMEM((seq, dv_all), F32), pltpu.VMEM((seq, dv_all), F32),
                        pltpu.VMEM((dv_all, GLA_DK), F32), pltpu.VMEM((dv_all, GLA_DK), F32)],
        compiler_params=_cparams(1),
        name=f"gla{seq}",
    )(*args)


ROUTE_TM = 256


def _split_bf16(x):
    hi = x.astype(BF16)
    return hi, (x - hi.astype(F32)).astype(BF16)


def _outproj_kernel(*refs, n_in, dual_x):
    y_refs = refs[:2 * n_in]
    n_x = 2 if dual_x else 1
    x_refs = refs[2 * n_in + 1:2 * n_in + 1 + n_x]
    w_ref = refs[2 * n_in]
    (g1_ref, sh_ref, sc_ref, nw_ref, wr_ref, br_ref,
     xo_ref, tr_ref, route_ref, cnt_ref, w_scr, wr_hi, wr_lo, carry) = refs[2 * n_in + 1 + n_x:]
    tm = ROUTE_TM
    is_ctx = pl.program_id(0) < T_CTX // tm

    @pl.when(pl.program_id(0) == 0)
    def _():
        w_scr[...] = w_ref[...].astype(BF16)
        hi, lo = _split_bf16(wr_ref[...])
        wr_hi[...] = hi
        wr_lo[...] = lo
        carry[...] = jnp.zeros(carry.shape, F32)

    o = None
    for i in range(n_in):
        y = jnp.where(is_ctx, y_refs[2 * i][...], y_refs[2 * i + 1][...])
        d = jnp.dot(y, w_scr[i * D:(i + 1) * D, :], preferred_element_type=F32)
        o = d if o is None else o + d
    x_in = jnp.where(is_ctx, x_refs[0][...], x_refs[1][...]) if dual_x else x_refs[0][...]
    x = x_in + g1_ref[0] * o
    xo_ref[...] = x
    t = _rms(x, nw_ref[...]) * (1.0 + sc_ref[0]) + sh_ref[0]
    for k in range(D // LANES):
        tr_ref[pl.ds(k, tm, stride=SUBLANES), :] = t[:, k * LANES:(k + 1) * LANES]

    t_hi, t_lo = _split_bf16(t)
    logit = (jnp.dot(t_hi, wr_hi[...], preferred_element_type=F32)
             + jnp.dot(t_lo, wr_hi[...], preferred_element_type=F32)
             + jnp.dot(t_hi, wr_lo[...], preferred_element_type=F32)) + br_ref[...]
    lane = lax.broadcasted_iota(jnp.int32, (tm, LANES), 1)
    lanef = lane.astype(F32)
    neg = -jnp.inf
    gl = jnp.where(lane < N_GROUPS, logit, neg)
    gmax = jnp.max(gl, axis=-1, keepdims=True)
    gsel = jnp.min(jnp.where(gl == gmax, lanef, float(LANES)), axis=-1, keepdims=True)
    gprob = 1.0 / jnp.sum(jnp.exp(gl - gmax), axis=-1, keepdims=True)
    first = N_GROUPS + EXP_PER_GROUP * gsel
    el = jnp.where((lanef >= first) & (lanef < first + EXP_PER_GROUP), logit, neg)
    m1 = jnp.max(el, axis=-1, keepdims=True)
    i1 = jnp.min(jnp.where(el == m1, lanef, float(LANES)), axis=-1, keepdims=True)
    el2 = jnp.where(lanef == i1, neg, el)
    m2 = jnp.max(el2, axis=-1, keepdims=True)
    i2 = jnp.min(jnp.where(el2 == m2, lanef, float(LANES)), axis=-1, keepdims=True)
    e2 = jnp.exp(m2 - m1)
    c1 = gprob / (1.0 + e2)
    c2 = gprob * e2 / (1.0 + e2)
    x1 = i1 - N_GROUPS
    x2 = i2 - N_GROUPS

    hot = ((lanef == x1) | (lanef == x2)).astype(F32)
    tri = (lax.broadcasted_iota(jnp.int32, (tm, tm), 1) < lax.broadcasted_iota(jnp.int32, (tm, tm), 0))
    before = _mm(tri.astype(F32), hot) + carry[0:1, :]
    r1 = jnp.sum(jnp.where(lanef == x1, before, 0.0), axis=-1, keepdims=True)
    r2 = jnp.sum(jnp.where(lanef == x2, before, 0.0), axis=-1, keepdims=True)
    total = carry[...] + jnp.sum(hot, axis=0, keepdims=True)
    carry[...] = total
    cnt_ref[...] = total
    vals = (x1, x2, c1, c2, r1, r2)
    out = jnp.zeros((tm, LANES), F32)
    for k, v in enumerate(vals):
        out = jnp.where(lane == k, v, out)
    route_ref[...] = jnp.transpose(out)[0:SUBLANES, :]


def _outproj_route(ys, w_out, xs, mods, layer, norm_w, w_router, b_router):
    tm = ROUTE_TM
    n_in = len(ys) // 2
    dual_x = len(xs) == 2
    kdim = w_out.shape[0]

    def full(shape):
        return pl.BlockSpec(shape, lambda i: (0,) * len(shape))

    tile = pl.BlockSpec((tm, D), lambda i: (i, 0))
    pair = list(_ctx_lat_specs(tm))
    in_specs = (pair * n_in + [full((kdim, D))] + (pair if dual_x else [tile])
                + [_mod_spec(layer, 2, tm), _mod_spec(layer, 3, tm), _mod_spec(layer, 4, tm),
                   full((1, D)), full((D, LANES)), full((1, LANES))])
    return pl.pallas_call(
        functools.partial(_outproj_kernel, n_in=n_in, dual_x=dual_x),
        grid=(T // tm,), in_specs=in_specs,
        out_specs=[tile, pl.BlockSpec((tm * SUBLANES, LANES), lambda i: (i, 0)),
                   pl.BlockSpec((SUBLANES, tm), lambda i: (0, i)), full((SUBLANES, LANES))],
        out_shape=[jax.ShapeDtypeStruct((T, D), F32), jax.ShapeDtypeStruct((T * SUBLANES, LANES), F32),
                   jax.ShapeDtypeStruct((SUBLANES, T), F32), jax.ShapeDtypeStruct((SUBLANES, LANES), F32)],
        scratch_shapes=[pltpu.VMEM((kdim, D), BF16), pltpu.VMEM((D, LANES), BF16), pltpu.VMEM((D, LANES), BF16),
                        pltpu.VMEM((SUBLANES, LANES), F32)],
        compiler_params=_cparams(1),
        name=f"outproj{layer}",
    )(*ys, w_out, *xs, mods, mods, mods, norm_w.reshape(1, D), w_router, b_router)


def _moe_meta(counts):
    tm = MOE_TM
    experts = jnp.arange(N_EXPERTS, dtype=jnp.int32)
    counts = counts[0, :N_EXPERTS].astype(jnp.int32)
    padded = ((counts + tm - 1) // tm) * tm
    ends = jnp.cumsum(padded)
    tile_start = jnp.arange(MOE_TILES, dtype=jnp.int32) * tm
    te = jnp.sum((tile_start[:, None] >= ends[None, :]).astype(jnp.int32), axis=1)
    last = jnp.max(jnp.where(counts > 0, experts, 0))
    meta = jnp.concatenate([jnp.minimum(te, last), ends[-1:] // tm]).astype(jnp.int32)
    starts = ends - padded
    pads = jnp.concatenate([starts + counts, ends[-1:], ends, jnp.full((1,), MOE_ROWS)]).astype(jnp.int32)
    return starts.astype(jnp.int32), pads, meta


def _expert_changed(meta_ref, j):
    return (j == 0) | (meta_ref[j] != meta_ref[jnp.maximum(j - 1, 0)])


def _moe_up_kernel(pos1_ref, pos2_ref, pads_ref, meta_ref, tr_hbm, wg_ref, wu_ref, a_ref, rowmap_ref,
                   tr_scr, g0, g1, wg_bf, wu_bf, sem):
    j = pl.program_id(0)
    tm = MOE_TM
    ntiles = meta_ref[MOE_TILES]

    def gather(tile, dst):
        for mi in range(tm):
            tok = jnp.minimum(rowmap_ref[tile * tm + mi] >> 1, T - 1)
            dst[mi * SUBLANES:(mi + 1) * SUBLANES, :] = tr_scr[pl.ds(pl.multiple_of(tok * SUBLANES, SUBLANES), SUBLANES), :]

    @pl.when(j == 0)
    def _():
        load = pltpu.make_async_copy(tr_hbm, tr_scr, sem)
        load.start()

        def clear(r, carry):
            rowmap_ref[r] = 2 * T
            return carry
        for k in range(N_EXPERTS + 1):
            lax.fori_loop(pads_ref[k], pads_ref[N_EXPERTS + 1 + k], clear, 0)

        def place(t, carry):
            rowmap_ref[pos1_ref[t]] = 2 * t
            rowmap_ref[pos2_ref[t]] = 2 * t + 1
            return carry
        lax.fori_loop(0, T, place, 0, unroll=8)
        load.wait()
        gather(0, g0)

    def compute(cur, nxt):
        gather(jnp.minimum(j + 1, ntiles - 1), nxt)
        x = _tokmajor_to_std(cur, tm).astype(BF16)
        g = jnp.dot(x, wg_bf[...], preferred_element_type=F32)
        u = jnp.dot(x, wu_bf[...], preferred_element_type=F32)
        a_ref[...] = (_silu(g) * u).astype(a_ref.dtype)

    @pl.when(j < ntiles)
    def _():
        @pl.when(_expert_changed(meta_ref, j))
        def _():
            wg_bf[...] = wg_ref[...].astype(BF16)
            wu_bf[...] = wu_ref[...].astype(BF16)

        pl.when(j % 2 == 0)(functools.partial(compute, g0, g1))
        pl.when(j % 2 == 1)(functools.partial(compute, g1, g0))

    @pl.when(j >= ntiles)
    def _():
        a_ref[...] = jnp.zeros(a_ref.shape, a_ref.dtype)


def _moe_down_kernel(rowmap_ref, cpair_ref, meta_ref, a_ref, wd_ref, out_hbm, acc, y0, y1, wd_bf):
    j = pl.program_id(0)
    tm = MOE_TM
    zrows = 512
    ntiles = meta_ref[MOE_TILES]

    @pl.when(j == 0)
    def _():
        def zero(i, carry):
            acc[pl.ds(pl.multiple_of(i * zrows, zrows), zrows), :] = jnp.zeros((zrows, LANES), F32)
            return carry
        lax.fori_loop(0, ACC_ROWS // zrows, zero, 0)

    def matmul(dst):
        y = jnp.dot(a_ref[...], wd_bf[...], preferred_element_type=F32)
        for k in range(D // LANES):
            dst[pl.ds(k, tm, stride=SUBLANES), :] = y[:, k * LANES:(k + 1) * LANES]

    def scatter(tile, src):
        for b in range(tm // SUBLANES):
            ents = [rowmap_ref[tile * tm + b * SUBLANES + i] for i in range(SUBLANES)]
            offs = [pl.multiple_of((e >> 1) * SUBLANES, SUBLANES) for e in ents]
            olds = [acc[pl.ds(o, SUBLANES), :] for o in offs]
            for i, o in enumerate(offs):
                r = (b * SUBLANES + i) * SUBLANES
                acc[pl.ds(o, SUBLANES), :] = olds[i] + cpair_ref[ents[i]] * src[r:r + SUBLANES, :]

    has_mm = j < ntiles
    has_sc = (j >= 1) & (j <= ntiles)

    @pl.when(has_mm)
    def _():
        @pl.when(_expert_changed(meta_ref, j))
        def _():
            wd_bf[...] = wd_ref[...].astype(BF16)

    for par, (cur, prev) in enumerate(((y0, y1), (y1, y0))):
        mine = (j % 2) == par

        @pl.when(mine & has_mm & has_sc)
        def _():
            matmul(cur)
            scatter(j - 1, prev)

        @pl.when(mine & has_mm & jnp.logical_not(has_sc))
        def _():
            matmul(cur)

        @pl.when(mine & jnp.logical_not(has_mm) & has_sc)
        def _():
            scatter(j - 1, prev)

    @pl.when(j == pl.num_programs(0) - 1)
    def _():
        pltpu.sync_copy(acc.at[0:T * SUBLANES, :], out_hbm)


def _moe(tr, route_t, counts, layer, w_gate, w_up, w_down):
    tm = MOE_TM
    starts, pads, meta = _moe_meta(counts)
    experts = jnp.arange(N_EXPERTS, dtype=jnp.int32)

    def position(e, r):
        sel = e.astype(jnp.int32)[:, None] == experts[None, :]
        return jnp.sum(jnp.where(sel, starts[None, :], 0), axis=1) + r.astype(jnp.int32)

    pos1 = position(route_t[0], route_t[4])
    pos2 = position(route_t[1], route_t[5])
    cpair = jnp.concatenate([jnp.stack([route_t[2], route_t[3]], axis=1).reshape(2 * T), jnp.zeros((2,), F32)])

    def wspec(shape, n):
        return pl.BlockSpec((None, None) + shape,
                            lambda j, *pre: (layer, pre[n - 1][jnp.minimum(j, MOE_TILES - 1)], 0, 0))

    gscr = pltpu.VMEM((tm * SUBLANES, LANES), F32)
    act, rowmap = pl.pallas_call(
        _moe_up_kernel,
        grid_spec=pltpu.PrefetchScalarGridSpec(
            num_scalar_prefetch=4, grid=(MOE_TILES,),
            in_specs=[pl.BlockSpec(memory_space=pl.ANY), wspec((D, D_EXPERT), 4), wspec((D, D_EXPERT), 4)],
            out_specs=[pl.BlockSpec((tm, D_EXPERT), lambda j, *pre: (j, 0)), pl.BlockSpec(memory_space=pltpu.SMEM)],
            scratch_shapes=[pltpu.VMEM((T * SUBLANES, LANES), F32), gscr, gscr,
                            pltpu.VMEM((D, D_EXPERT), BF16), pltpu.VMEM((D, D_EXPERT), BF16),
                            pltpu.SemaphoreType.DMA(())]),
        out_shape=[jax.ShapeDtypeStruct((MOE_ROWS, D_EXPERT), BF16), jax.ShapeDtypeStruct((MOE_ROWS,), jnp.int32)],
        compiler_params=_cparams(1),
        name=f"moe_up{layer}",
    )(pos1, pos2, pads, meta, tr, w_gate, w_up)

    return pl.pallas_call(
        _moe_down_kernel,
        grid_spec=pltpu.PrefetchScalarGridSpec(
            num_scalar_prefetch=3, grid=(MOE_TILES + 1,),
            in_specs=[pl.BlockSpec((tm, D_EXPERT), lambda j, *pre: (jnp.minimum(j, MOE_TILES - 1), 0)),
                      wspec((D_EXPERT, D), 3)],
            out_specs=pl.BlockSpec(memory_space=pl.ANY),
            scratch_shapes=[pltpu.VMEM((ACC_ROWS, LANES), F32), gscr, gscr, pltpu.VMEM((D_EXPERT, D), BF16)]),
        out_shape=jax.ShapeDtypeStruct((T * SUBLANES, LANES), F32),
        compiler_params=_cparams(1),
        name=f"moe_down{layer}",
    )(rowmap, cpair, meta, act, w_down)


FINAL_TM = 512


def _final_kernel(x_ref, moe_ref, g2_ref, nw_ref, oc_ref, ol_ref):
    x = x_ref[...] + g2_ref[0] * _tokmajor_to_std(moe_ref, FINAL_TM)
    y = _rms(x, nw_ref[...])
    is_ctx = pl.program_id(0) < T_CTX // FINAL_TM

    @pl.when(is_ctx)
    def _():
        oc_ref[...] = y

    @pl.when(jnp.logical_not(is_ctx))
    def _():
        ol_ref[...] = y


def _final(x, moe, mods, layer, norm_w):
    tm = FINAL_TM
    tile = pl.BlockSpec((tm, D), lambda i: (i, 0))
    return pl.pallas_call(
        _final_kernel,
        grid=(T // tm,),
        in_specs=[tile, pl.BlockSpec((tm * SUBLANES, LANES), lambda i: (i, 0)), _mod_spec(layer, 5, tm),
                  pl.BlockSpec((1, D), lambda i: (0, 0))],
        out_specs=list(_ctx_lat_specs(tm)),
        out_shape=[jax.ShapeDtypeStruct((T_CTX, D), F32), jax.ShapeDtypeStruct((T_LAT, D), F32)],
        compiler_params=_cparams(1),
        name="final_norm",
    )(x, moe, mods, norm_w.reshape(1, D))


def _pad_lanes(v):
    return jnp.pad(v.astype(F32), (0, LANES - v.shape[0])).reshape(1, LANES)


def kernel(x_prompt, x_sample, cache_k_attn, cache_v_attn, state_ssd_fwd, state_ssd_bwd, state_gla_fwd, state_gla_bwd, c, c_ctx, w_ada, b_ada, norm_mix_w, norm_ffn_w, w_in_even, conv_w, conv_b, dt_bias_fwd, dt_bias_bwd, a_log_fwd, a_log_bwd, d_skip, ssd_norm_w, attn_sink, w_out_even, w_in_odd, w_gk2_fwd, b_gk_fwd, w_gk2_bwd, b_gk_bwd, gla_norm_w, w_out_odd, w_router_group, b_router_group, w_router_expert, b_router_expert, w_gate_exp, w_up_exp, w_down_exp, final_norm_w):
    depth = w_ada.shape[0]
    assert depth == 2 and x_prompt.shape == (BATCH, SEQ, D) and x_sample.shape == (DEC_BATCH, DEC_SEQ, D)

    cond8 = jnp.concatenate([c_ctx[None, :], c, jnp.zeros((SUBLANES - 1 - DEC_BATCH, D), F32)], axis=0)
    mods = _adaln(cond8, w_ada, b_ada).reshape(depth * SUBLANES * 6, 1, D)
    xs0 = (x_prompt.reshape(T_CTX, D), x_sample.reshape(T_LAT, D))

    def router_params(i):
        wr = jnp.concatenate([w_router_group[i], w_router_expert[i],
                              jnp.zeros((D, LANES - N_GROUPS - N_EXPERTS), F32)], axis=1)
        return wr, _pad_lanes(jnp.concatenate([b_router_group[i], b_router_expert[i]]))

    def pad_cols(w):
        return jnp.concatenate([w, jnp.zeros((D, LANES - w.shape[1]), F32)], axis=1)

    w = w_in_even[0]
    n_zxbc = 2 * D_SSD + 2 * SSD_GROUPS * SSD_N
    n_dt = 2 * SSD_HEADS
    n_main = n_zxbc // PROJ_TN
    bc_tile, q_tiles = n_main - 1, D // PROJ_TN

    def out_block0(j):
        return jnp.where(j == bc_tile, bc_tile + q_tiles, jnp.where((j > bc_tile) & (j <= bc_tile + q_tiles), j - 1, j))

    p0, p0dt = _modproj(xs0, mods, 0, norm_mix_w[0], w_in_even, n_main, pad_cols(w[:, n_zxbc:n_zxbc + n_dt]),
                        out_block0, w_tail=w[:, n_zxbc + n_dt:])

    dtb = _pad_lanes(jnp.concatenate([dt_bias_fwd[0], dt_bias_bwd[0]]))
    alog = _pad_lanes(jnp.concatenate([a_log_fwd[0], a_log_bwd[0]]))
    dskip = jnp.repeat(d_skip[0], SSD_P).reshape(1, D_SSD)
    ssd_args = (conv_w[0], conv_b[0], dtb, alog, dskip, ssd_norm_w[0])
    y_ssd_c, ssd_f, ssd_b = _ssd(p0, p0dt, 0, BATCH, SEQ, *ssd_args)
    y_ssd_l, _, _ = _ssd(p0, p0dt, T_CTX, DEC_BATCH, DEC_SEQ, *ssd_args,
                         h0f=state_ssd_fwd[:, 0].reshape(DEC_BATCH, D_SSD, SSD_N),
                         h0b=state_ssd_bwd[:, 0].reshape(DEC_BATCH, D_SSD, SSD_N))
    sink = _pad_lanes(attn_sink[0])
    y_att_c = _ctx_attn(p0, sink)
    y_att_l = _lat_attn(p0, cache_k_attn[:, 0].reshape(DEC_BATCH, PAST_LEN, ATT_KV_DIM),
                        cache_v_attn[:, 0].reshape(DEC_BATCH, PAST_LEN, ATT_KV_DIM), sink)
    xmid0, tr0, route0, cnt0 = _outproj_route([y_ssd_c, y_ssd_l, y_att_c, y_att_l], w_out_even[0], xs0, mods, 0,
                                              norm_ffn_w[0], *router_params(0))
    moe0 = _moe(tr0, route0, cnt0, 0, w_gate_exp, w_up_exp, w_down_exp)

    dk_all = GLA_HEADS * GLA_DK
    n_qkvr = 2 * dk_all + 2 * GLA_HEADS * GLA_DV
    p1, p1lr, x1 = _modproj(xmid0, mods, 1, norm_mix_w[1], w_in_odd, n_qkvr // PROJ_TN,
                            pad_cols(w_in_odd[0][:, n_qkvr:]), lambda j: j, moe=moe0)
    w2f = jnp.zeros((LANES, dk_all), F32).at[:GLA_LOWRANK].set(w_gk2_fwd[0])
    w2b = jnp.zeros((LANES, dk_all), F32).at[GLA_LOWRANK:2 * GLA_LOWRANK].set(w_gk2_bwd[0])
    gla_args = (w2f, w2b, b_gk_fwd[0], b_gk_bwd[0], gla_norm_w[0])
    y_gla_c, gla_f, gla_b = _gla(p1, p1lr, 0, BATCH, SEQ, *gla_args)
    y_gla_l, _, _ = _gla(p1, p1lr, T_CTX, DEC_BATCH, DEC_SEQ, *gla_args,
                         s0f=state_gla_fwd[:, 0].reshape(DEC_BATCH, dk_all, GLA_DV),
                         s0b=state_gla_bwd[:, 0].reshape(DEC_BATCH, dk_all, GLA_DV))
    xmid1, tr1, route1, cnt1 = _outproj_route([y_gla_c, y_gla_l], w_out_odd[0], (x1,), mods, 1,
                                              norm_ffn_w[1], *router_params(1))
    moe1 = _moe(tr1, route1, cnt1, 1, w_gate_exp, w_up_exp, w_down_exp)
    y_c, y_l = _final(xmid1, moe1, mods, 1, final_norm_w)

    y_prompt = y_c.reshape(BATCH, SEQ, D)
    y_sample = y_l.reshape(DEC_BATCH, DEC_SEQ, D)
    new_k = p0[:T_CTX, P0_K:P0_K + ATT_KV_DIM].astype(F32).reshape(BATCH, 1, SEQ, ATT_KV, ATT_HD)
    new_v = p0[:T_CTX, P0_V:P0_V + ATT_KV_DIM].astype(F32).reshape(BATCH, 1, SEQ, ATT_KV, ATT_HD)
    return (y_prompt, y_sample, new_k, new_v,
            ssd_f.reshape(BATCH, 1, SSD_HEADS, SSD_P, SSD_N), ssd_b.reshape(BATCH, 1, SSD_HEADS, SSD_P, SSD_N),
            gla_f.reshape(BATCH, 1, GLA_HEADS, GLA_DK, GLA_DV), gla_b.reshape(BATCH, 1, GLA_HEADS, GLA_DK, GLA_DV))
```

```python
import functools
import math

import numpy as np
import jax
import jax.numpy as jnp
from jax import lax
from jax.experimental import pallas as pl
from jax.experimental.pallas import tpu as pltpu

F32 = jnp.float32
BF16 = jnp.bfloat16

D = 1024
BATCH, SEQ = 16, 256
DEC_BATCH, DEC_SEQ = 2, 1024
PAST_LEN = 512
GRID_W = 64
EPS = 1e-6
T_CTX = BATCH * SEQ
T_LAT = DEC_BATCH * DEC_SEQ
T = T_CTX + T_LAT

SSD_HEADS, SSD_P, SSD_N, SSD_GROUPS = 16, 64, 128, 2
SSD_CONV = 5
SSD_L = 128
D_SSD = SSD_HEADS * SSD_P
HEADS_PER_GROUP = SSD_HEADS // SSD_GROUPS
GROUP_W = HEADS_PER_GROUP * SSD_P

ATT_HEADS, ATT_KV, ATT_HD = 16, 4, 64
ATT_KV_DIM = ATT_KV * ATT_HD
WINDOW = 128
ATT_BLOCK = 128
ATT_SCALE = ATT_HD ** -0.5
ROPE_BASE = 10000.0

GLA_HEADS, GLA_DK, GLA_DV = 4, 128, 256
GLA_C = 64
GLA_GATE_NORM = 16.0
GLA_LOWRANK = 16

N_GROUPS, EXP_PER_GROUP = 4, 4
N_EXPERTS = 16
D_EXPERT = 512

LANES = 128
SUBLANES = 8
VMEM_LIMIT = 56 * 1024 * 1024

P0_Z, P0_X, P0_Q, P0_BC, P0_K, P0_V = 0, 1024, 2048, 3072, 3584, 3840
P0_W = 4096
P1_Q, P1_K, P1_V, P1_R = 0, 512, 1024, 2048
P1_W = 3072

MOE_TM = 256
MOE_TILES = (2 * T) // MOE_TM + N_EXPERTS
MOE_ROWS = MOE_TILES * MOE_TM
ACC_ROWS = (T + SUBLANES) * SUBLANES


def _cparams(n_axes, vmem=VMEM_LIMIT):
    return pltpu.CompilerParams(dimension_semantics=("arbitrary",) * n_axes, vmem_limit_bytes=vmem)


def _silu(x):
    return x / (1.0 + jnp.exp(-x))


def _softplus(x):
    return jnp.maximum(x, 0.0) + jnp.log(1.0 + jnp.exp(-jnp.abs(x)))


def _mm(a, b):
    return jnp.dot(a.astype(BF16), b.astype(BF16), preferred_element_type=F32)


def _mm_nt(a, b):
    return lax.dot_general(a.astype(BF16), b.astype(BF16), (((1,), (1,)), ((), ())),
                           preferred_element_type=F32)


def _mm_tn(a, b):
    return lax.dot_general(a.astype(BF16), b.astype(BF16), (((0,), (0,)), ((), ())),
                           preferred_element_type=F32)


def _rms(x, w):
    return x * lax.rsqrt(jnp.mean(x * x, axis=-1, keepdims=True) + EPS) * w


def _cumsum_rows(x, n):
    row = lax.broadcasted_iota(jnp.int32, x.shape, 0)
    s = 1
    while s < n:
        x = x + jnp.where(row >= s, pltpu.roll(x, s, 0), 0.0)
        s *= 2
    return x


def _mod_row(tok0):
    return jnp.where(tok0 < T_CTX, 0, 1 + (tok0 - T_CTX) // DEC_SEQ)


ADA_TN = 1536


def _adaln_kernel(c_ref, w_ref, b_ref, o_ref):
    s = _silu(c_ref[...])
    o_ref[0] = _mm(s, w_ref[0]) + b_ref[0]


def _adaln(cond8, w_ada, b_ada):
    depth = w_ada.shape[0]
    return pl.pallas_call(
        _adaln_kernel,
        grid=(depth, 6 * D // ADA_TN),
        in_specs=[
            pl.BlockSpec((SUBLANES, D), lambda l, j: (0, 0)),
            pl.BlockSpec((1, D, ADA_TN), lambda l, j: (l, 0, j)),
            pl.BlockSpec((1, 1, ADA_TN), lambda l, j: (l, 0, j)),
        ],
        out_specs=pl.BlockSpec((1, SUBLANES, ADA_TN), lambda l, j: (l, 0, j)),
        out_shape=jax.ShapeDtypeStruct((depth, SUBLANES, 6 * D), F32),
        compiler_params=_cparams(2),
        name="adaln",
    )(cond8, w_ada, b_ada.reshape(depth, 1, 6 * D))


def _mod_spec(layer, chunk, tm, tile_of=lambda i, *_: i):
    return pl.BlockSpec((1, 1, D), lambda *g: ((layer * SUBLANES + _mod_row(tile_of(*g) * tm)) * 6 + chunk, 0, 0))


def _tokmajor_to_std(ref, tm):
    return jnp.concatenate([ref[pl.ds(k, tm, stride=SUBLANES), :] for k in range(D // LANES)], axis=1)


PROJ_TM = 1024
PROJ_TN = 512


def _ctx_lat_specs(tm, width=D):
    n_ctx = T_CTX // tm
    return (pl.BlockSpec((tm, width), lambda i, *_: (jnp.minimum(i, n_ctx - 1), 0)),
            pl.BlockSpec((tm, width), lambda i, *_: (jnp.maximum(i - n_ctx, 0), 0)))


def _modproj_kernel(*refs, dual_x, n_main, has_tail):
    it = iter(refs)
    if dual_x:
        xc_ref, xl_ref = next(it), next(it)
    else:
        x_ref, moe_ref, g2_ref = next(it), next(it), next(it)
    sh_ref, sc_ref, nw_ref, wm_ref = next(it), next(it), next(it), next(it)
    wt_ref = next(it) if has_tail else None
    ws_ref, o_ref, os_ref = next(it), next(it), next(it)
    xo_ref = None if dual_x else next(it)
    h_all, w_bf = next(it), next(it)
    j, i = pl.program_id(0), pl.program_id(1)
    tm = PROJ_TM
    rows = pl.ds(pl.multiple_of(i * tm, tm), tm)

    @pl.when(j == 0)
    def _():
        if dual_x:
            x = jnp.where(i < T_CTX // tm, xc_ref[...], xl_ref[...])
        else:
            x = x_ref[...] + g2_ref[0] * _tokmajor_to_std(moe_ref, tm)
            xo_ref[...] = x
        h = (_rms(x, nw_ref[...]) * (1.0 + sc_ref[0]) + sh_ref[0]).astype(BF16)
        h_all[rows, :] = h
        os_ref[...] = jnp.dot(h, ws_ref[...].astype(BF16), preferred_element_type=F32)

    @pl.when(i == 0)
    def _():
        def load(w_ref):
            w_bf[...] = w_ref[...].astype(BF16)

        if has_tail:
            pl.when(j < n_main)(functools.partial(load, wm_ref))
            pl.when(j >= n_main)(functools.partial(load, wt_ref))
        else:
            load(wm_ref)

    o_ref[...] = jnp.dot(h_all[rows, :], w_bf[...], preferred_element_type=F32).astype(o_ref.dtype)


def _modproj(xs, mods, layer, norm_w, w_main, n_main, w_small, out_block, w_tail=None, moe=None):
    tm, tn = PROJ_TM, PROJ_TN
    dual_x = moe is None
    has_tail = w_tail is not None
    n_tiles = n_main + (w_tail.shape[1] // tn if has_tail else 0)
    n_i, n_ctx = T // tm, T_CTX // tm

    def tok(j, i):
        return jnp.where(j == 0, i, n_i - 1)

    tile = pl.BlockSpec((tm, D), lambda j, i: (tok(j, i), 0))
    if dual_x:
        in_specs = [pl.BlockSpec((tm, D), lambda j, i: (jnp.minimum(tok(j, i), n_ctx - 1), 0)),
                    pl.BlockSpec((tm, D), lambda j, i: (jnp.maximum(tok(j, i) - n_ctx, 0), 0))]
        args = list(xs)
    else:
        in_specs = [tile, pl.BlockSpec((tm * SUBLANES, LANES), lambda j, i: (tok(j, i), 0)),
                    _mod_spec(layer - 1, 5, tm, tok)]
        args = [xs, moe, mods]
    in_specs += [_mod_spec(layer, 0, tm, tok), _mod_spec(layer, 1, tm, tok), pl.BlockSpec((1, D), lambda j, i: (0, 0)),
                 pl.BlockSpec((None, D, tn), lambda j, i: (0, 0, jnp.minimum(j, n_main - 1)))]
    args += [mods, mods, norm_w.reshape(1, D), w_main]
    if has_tail:
        in_specs.append(pl.BlockSpec((D, tn), lambda j, i: (0, jnp.maximum(j - n_main, 0))))
        args.append(w_tail)
    in_specs.append(pl.BlockSpec((D, LANES), lambda j, i: (0, 0)))
    args.append(w_small)
    out_specs = [pl.BlockSpec((tm, tn), lambda j, i: (i, out_block(j))),
                 pl.BlockSpec((tm, LANES), lambda j, i: (tok(j, i), 0))]
    out_shape = [jax.ShapeDtypeStruct((T, n_tiles * tn), BF16), jax.ShapeDtypeStruct((T, LANES), F32)]
    if not dual_x:
        out_specs.append(tile)
        out_shape.append(jax.ShapeDtypeStruct((T, D), F32))
    return pl.pallas_call(
        functools.partial(_modproj_kernel, dual_x=dual_x, n_main=n_main, has_tail=has_tail),
        grid=(n_tiles, n_i), in_specs=in_specs, out_specs=out_specs, out_shape=out_shape,
        scratch_shapes=[pltpu.VMEM((T, D), BF16), pltpu.VMEM((D, tn), BF16)],
        compiler_params=_cparams(2),
        name=f"modproj{layer}",
    )(*args)


def _expand_heads(v, off):
    lo = lax.broadcasted_iota(jnp.int32, (v.shape[0], LANES), 1) < SSD_P
    tiles = []
    for q in range(SSD_HEADS // 2):
        a = jnp.broadcast_to(v[:, off + 2 * q:off + 2 * q + 1], (v.shape[0], LANES))
        b = jnp.broadcast_to(v[:, off + 2 * q + 1:off + 2 * q + 2], (v.shape[0], LANES))
        tiles.append(jnp.where(lo, a, b))
    return jnp.concatenate(tiles, axis=1)


def _ssd_kernel(*refs, seq, has_h0):
    if has_h0:
        (z_ref, x_ref, bc_ref, dt_ref, cwx_ref, cwbc_ref, cbx_ref, cbbc_ref, dtb_ref, alog_ref, dsk_ref,
         nw_ref, h0f_ref, h0b_ref, y_ref, sf_ref, sb_ref,
         xpad, bcpad, xc, bcc, a_scr, dt_scr, yacc, hf, hb) = refs
    else:
        (z_ref, x_ref, bc_ref, dt_ref, cwx_ref, cwbc_ref, cbx_ref, cbbc_ref, dtb_ref, alog_ref, dsk_ref,
         nw_ref, y_ref, sf_ref, sb_ref,
         xpad, bcpad, xc, bcc, a_scr, dt_scr, yacc, hf, hb) = refs
    L = SSD_L
    nc = seq // L
    pad = SUBLANES
    half = SSD_CONV // 2

    for buf, src, cw, cb, dst in ((xpad, x_ref, cwx_ref, cbx_ref, xc), (bcpad, bc_ref, cwbc_ref, cbbc_ref, bcc)):
        width = buf.shape[1]
        buf[0:pad, :] = jnp.zeros((pad, width), F32)
        buf[pad + seq:2 * pad + seq, :] = jnp.zeros((pad, width), F32)
        buf[pad:pad + seq, :] = src[...].astype(F32)
        for blk in range(nc):
            acc = jnp.broadcast_to(cb[...], (L, width))
            for j in range(SSD_CONV):
                r0 = pad - half + j + blk * L
                acc = acc + cw[j:j + 1, :] * buf[r0:r0 + L, :]
            dst[blk * L:(blk + 1) * L, :] = _silu(acc)

    lane = lax.broadcasted_iota(jnp.int32, (seq, LANES), 1)
    dts = jnp.where(lane < 2 * SSD_HEADS, _softplus(dt_ref[...] + dtb_ref[...]), 0.0)
    dt_scr[...] = dts
    a_scr[...] = dts * (-jnp.exp(alog_ref[...]))

    if has_h0:
        hf[...] = h0f_ref[...]
        hb[...] = h0b_ref[...]
    else:
        hf[...] = jnp.zeros(hf.shape, F32)
        hb[...] = jnp.zeros(hb.shape, F32)

    row = lax.broadcasted_iota(jnp.int32, (L, L), 0)
    col = lax.broadcasted_iota(jnp.int32, (L, L), 1)
    lane_l = lax.broadcasted_iota(jnp.int32, (L, LANES), 1)
    lo_half = lane_l < SSD_P

    def chunk(c, fwd, h_scr):
        off = 0 if fwd else SSD_HEADS
        r0 = pl.multiple_of(c * L, L)
        a = a_scr[pl.ds(r0, L), :]
        dt = dt_scr[pl.ds(r0, L), :]
        cs = _cumsum_rows(a, L)
        total = cs[L - 1:L, :]
        if fwd:
            u = cs
            rvec = jnp.exp(cs)
            ed = jnp.exp(total - cs) * dt
            keep = col <= row
        else:
            ex = cs - a
            u = -ex
            rvec = jnp.exp(total - ex)
            ed = jnp.exp(ex) * dt
            keep = col >= row
        ut = jnp.transpose(u)
        dtt = jnp.transpose(dt)
        tcol = jnp.transpose(jnp.broadcast_to(total, (L, LANES)))[:, 0:1]
        rexp = _expand_heads(rvec, off)
        edexp = _expand_heads(ed, off)
        x = xc[pl.ds(r0, L), :]
        bc = bcc[pl.ds(r0, L), :]
        outs = []
        for g in range(SSD_GROUPS):
            bg = bc[:, g * SSD_N:(g + 1) * SSD_N]
            cg = bc[:, SSD_GROUPS * SSD_N + g * SSD_N:SSD_GROUPS * SSD_N + (g + 1) * SSD_N]
            cbm = _mm_nt(cg, bg)
            hg = h_scr[g * GROUP_W:(g + 1) * GROUP_W, :]
            xg = x[:, g * GROUP_W:(g + 1) * GROUP_W]
            y_off = _mm_nt(cg, hg) * rexp[:, g * GROUP_W:(g + 1) * GROUP_W]
            tiles = []
            for p in range(HEADS_PER_GROUP // 2):
                xt = xg[:, p * LANES:(p + 1) * LANES]
                acc = None
                for s in range(2):
                    h = off + g * HEADS_PER_GROUP + 2 * p + s
                    seg = u[:, h:h + 1] - ut[h:h + 1, :]
                    m = cbm * jnp.exp(jnp.where(keep, seg, -jnp.inf)) * dtt[h:h + 1, :]
                    xm = jnp.where(lo_half if s == 0 else jnp.logical_not(lo_half), xt, 0.0)
                    d = _mm(m, xm)
                    acc = d if acc is None else acc + d
                tiles.append(acc)
            outs.append(y_off + jnp.concatenate(tiles, axis=1))
            decs = []
            for hh in range(HEADS_PER_GROUP):
                h = off + g * HEADS_PER_GROUP + hh
                decs.append(jnp.broadcast_to(jnp.exp(tcol[h:h + 1, :]), (SSD_P, SSD_N)))
            dec = jnp.concatenate(decs, axis=0)
            h_scr[g * GROUP_W:(g + 1) * GROUP_W, :] = dec * hg + _mm_tn(xg * edexp[:, g * GROUP_W:(g + 1) * GROUP_W], bg)
        return r0, x, jnp.concatenate(outs, axis=1)

    def fwd_body(c, carry):
        r0, _, y = chunk(c, True, hf)
        yacc[pl.ds(r0, L), :] = y
        return carry

    lax.fori_loop(0, nc, fwd_body, 0)

    def bwd_body(i, carry):
        c = nc - 1 - i
        r0, x, y = chunk(c, False, hb)
        y = yacc[pl.ds(r0, L), :] + y + dsk_ref[...] * x
        y = y * _silu(z_ref[pl.ds(r0, L), :].astype(F32))
        y_ref[pl.ds(r0, L), :] = _rms(y, nw_ref[...]).astype(y_ref.dtype)
        return carry

    lax.fori_loop(0, nc, bwd_body, 0)
    sf_ref[...] = hf[...]
    sb_ref[...] = hb[...]


def _ssd(p0, p0dt, tok0, nseq, seq, cw, cb, dtb, alog, dskip, nw, h0f=None, h0b=None):
    has_h0 = h0f is not None
    b0 = tok0 // seq

    def cols(width, start):
        return pl.BlockSpec((seq, width), lambda s: (b0 + s, start // width))

    def full(shape):
        return pl.BlockSpec(shape, lambda s: (0,) * len(shape))

    in_specs = [cols(D_SSD, P0_Z), cols(D_SSD, P0_X), cols(512, P0_BC), cols(LANES, 0),
                full((SSD_CONV, D_SSD)), full((SSD_CONV, 512)), full((1, D_SSD)), full((1, 512)),
                full((1, LANES)), full((1, LANES)), full((1, D_SSD)), full((1, D_SSD))]
    args = [p0, p0, p0, p0dt, cw[:, :D_SSD], cw[:, D_SSD:], cb[:D_SSD].reshape(1, -1), cb[D_SSD:].reshape(1, -1),
            dtb, alog, dskip, nw.reshape(1, -1)]
    st_spec = pl.BlockSpec((None, D_SSD, SSD_N), lambda s: (s, 0, 0))
    if has_h0:
        in_specs += [st_spec, st_spec]
        args += [h0f, h0b]
    st_shape = jax.ShapeDtypeStruct((nseq, D_SSD, SSD_N), F32)
    return pl.pallas_call(
        functools.partial(_ssd_kernel, seq=seq, has_h0=has_h0),
        grid=(nseq,), in_specs=in_specs,
        out_specs=[pl.BlockSpec((seq, D_SSD), lambda s: (s, 0)), st_spec, st_spec],
        out_shape=[jax.ShapeDtypeStruct((nseq * seq, D_SSD), BF16), st_shape, st_shape],
        scratch_shapes=[pltpu.VMEM((seq + 2 * SUBLANES, D_SSD), F32), pltpu.VMEM((seq + 2 * SUBLANES, 512), F32),
                        pltpu.VMEM((seq, D_SSD), F32), pltpu.VMEM((seq, 512), F32),
                        pltpu.VMEM((seq, LANES), F32), pltpu.VMEM((seq, LANES), F32),
                        pltpu.VMEM((seq, D_SSD), F32),
                        pltpu.VMEM((D_SSD, SSD_N), F32), pltpu.VMEM((D_SSD, SSD_N), F32)],
        compiler_params=_cparams(1),
        name=f"ssd{seq}",
    )(*args)


def _place_halves(tile, kv_in_high):
    lo = lax.broadcasted_iota(jnp.int32, tile.shape, 1) < ATT_HD
    swapped = pltpu.roll(tile, ATT_HD, 1)
    if kv_in_high:
        return jnp.where(lo, swapped, 0.0), jnp.where(lo, 0.0, tile)
    return jnp.where(lo, tile, 0.0), jnp.where(lo, 0.0, swapped)


def _place_rows(vt, kv_in_high):
    head = vt[ATT_HD:, :] if kv_in_high else vt[:ATT_HD, :]
    z = jnp.zeros_like(head)
    return jnp.concatenate([head, z], axis=0), jnp.concatenate([z, head], axis=0)


LOG2E = 1.4426950408889634
SCORE_SCALE = ATT_SCALE * LOG2E


def _sink_attend_t(score_parts, value_parts, sink2):
    m = sink2
    for s in score_parts:
        m = jnp.maximum(m, jnp.max(s, axis=0, keepdims=True))
    denom = jnp.exp2(sink2 - m)
    out = None
    for s, v in zip(score_parts, value_parts):
        p = jnp.exp2(s - m)
        denom = denom + jnp.sum(p, axis=0, keepdims=True)
        o = _mm(v, p)
        out = o if out is None else out + o
    return out * (1.0 / denom)


def _attn_schedule(n, scores, attend):
    scores(0)
    for j in range(n):
        if j + 1 < n:
            scores(j + 1)
        attend(j)


def _ctx_attn_kernel(q_ref, k_ref, v_ref, sink_ref, o_ref, s_a, s_b):
    sink2 = sink_ref[...] * LOG2E
    bufs = (s_a, s_b)
    half = SEQ // 2

    def kv_tile(ref, j):
        return ref[:, (j // 2) * LANES:(j // 2 + 1) * LANES].astype(F32), (j % 2 == 1)

    def scores(j):
        k_lo, k_hi = _place_halves(*kv_tile(k_ref, j))
        qst = jnp.concatenate([q_ref[:, qt * LANES:(qt + 1) * LANES] for qt in (2 * j, 2 * j + 1)], axis=0)
        bufs[j % 2][...] = _mm_nt(jnp.concatenate([k_lo, k_hi], axis=0), qst) * SCORE_SCALE

    def attend(j):
        src = bufs[j % 2]
        v, high = kv_tile(v_ref, j)
        vts = _place_rows(jnp.transpose(v), high)
        for ql in range(2):
            qt = 2 * j + ql
            for qh in range(2):
                cols = slice(ql * SEQ + qh * half, ql * SEQ + (qh + 1) * half)
                acc = None
                for s, vv in enumerate(vts):
                    o = _sink_attend_t([src[s * SEQ:(s + 1) * SEQ, cols]], [vv], sink2[:, 2 * qt + s:2 * qt + s + 1])
                    acc = o if acc is None else acc + o
                o_ref[qh * half:(qh + 1) * half, qt * LANES:(qt + 1) * LANES] = jnp.transpose(acc).astype(o_ref.dtype)

    _attn_schedule(ATT_KV, scores, attend)


def _ctx_attn(p0, sink):
    def cols(width, start):
        return pl.BlockSpec((SEQ, width), lambda b: (b, start // width))

    sbuf = pltpu.VMEM((2 * SEQ, 2 * SEQ), F32)
    return pl.pallas_call(
        _ctx_attn_kernel,
        grid=(BATCH,),
        in_specs=[cols(D, P0_Q), cols(ATT_KV_DIM, P0_K), cols(ATT_KV_DIM, P0_V),
                  pl.BlockSpec((1, LANES), lambda b: (0, 0))],
        out_specs=pl.BlockSpec((SEQ, D), lambda b: (b, 0)),
        out_shape=jax.ShapeDtypeStruct((T_CTX, D), BF16),
        scratch_shapes=[sbuf, sbuf],
        compiler_params=_cparams(1),
        name="ctx_attn",
    )(p0, p0, p0, sink)


def _rope_tables():
    quarter = ATT_HD // 4
    t = np.arange(DEC_SEQ)
    lane = np.arange(LANES)
    inv = ROPE_BASE ** (-(lane % quarter).astype(np.float64) / quarter)
    pos = np.where(((lane % ATT_HD) < ATT_HD // 2)[None, :], (t // GRID_W)[:, None], (t % GRID_W)[:, None])
    ang = pos * inv[None, :]
    first = ((lane % (2 * quarter)) < quarter)[None, :]
    cos, sin = np.cos(ang), np.sin(ang)
    return (jnp.asarray(cos, F32), jnp.asarray(np.where(first, -sin, 0.0), F32),
            jnp.asarray(np.where(first, 0.0, sin), F32))


def _rope(x, cos, sa, sb):
    quarter = ATT_HD // 4
    return x * cos + pltpu.roll(x, LANES - quarter, 1) * sa + pltpu.roll(x, quarter, 1) * sb


def _lat_attn_kernel(q_ref, kp_ref, kc_ref, kn_ref, vp_ref, vc_ref, vn_ref, ck_ref, cv_ref,
                     cos_ref, sa_ref, sb_ref, sink_ref, o_ref, c_a, c_b, w_a, w_b):
    blk = pl.program_id(1)
    nb = pl.num_programs(1)
    B = ATT_BLOCK
    sink2 = sink_ref[...] * LOG2E
    cbufs, wbufs = (c_a, c_b), (w_a, w_b)

    def tables(b):
        r0 = pl.multiple_of(b * B, B)
        return cos_ref[pl.ds(r0, B), :], sa_ref[pl.ds(r0, B), :], sb_ref[pl.ds(r0, B), :]

    tq = tables(blk)
    tk = [tables(jnp.maximum(blk - 1, 0)), tq, tables(jnp.minimum(blk + 1, nb - 1))]
    kabs = (blk - 1) * B + lax.broadcasted_iota(jnp.int32, (3 * B, B), 0)
    qpos = blk * B + lax.broadcasted_iota(jnp.int32, (3 * B, B), 1)
    ok = (jnp.abs(qpos - kabs) <= WINDOW) & (kabs >= 0) & (kabs < nb * B)
    ok = jnp.concatenate([ok, ok], axis=1)
    ok = jnp.concatenate([ok, ok], axis=0)

    def scores(j):
        high = (j % 2 == 1)
        sl = slice((j // 2) * LANES, (j // 2 + 1) * LANES)
        kw = jnp.concatenate([_rope(r[:, sl].astype(F32), *tb) for r, tb in zip((kp_ref, kc_ref, kn_ref), tk)], axis=0)
        qs = [q_ref[:, qt * LANES:(qt + 1) * LANES].astype(F32) for qt in (2 * j, 2 * j + 1)]
        q_plain = jnp.concatenate(qs, axis=0)
        q_rope = jnp.concatenate([_rope(q, *tq) for q in qs], axis=0)
        cbufs[j % 2][...] = _mm_nt(jnp.concatenate(_place_halves(ck_ref[:, sl], high), axis=0), q_plain) * SCORE_SCALE
        win = _mm_nt(jnp.concatenate(_place_halves(kw, high), axis=0), q_rope) * SCORE_SCALE
        wbufs[j % 2][...] = jnp.where(ok, win, -jnp.inf)

    def attend(j):
        high = (j % 2 == 1)
        sl = slice((j // 2) * LANES, (j // 2 + 1) * LANES)
        vw = jnp.concatenate([jnp.transpose(r[:, sl].astype(F32)) for r in (vp_ref, vc_ref, vn_ref)], axis=1)
        vts = _place_rows(vw, high)
        cvts = _place_rows(jnp.transpose(cv_ref[:, sl]), high)
        csrc, wsrc = cbufs[j % 2], wbufs[j % 2]
        for ql in range(2):
            qt = 2 * j + ql
            cols = slice(ql * B, (ql + 1) * B)
            acc = None
            for s in range(2):
                parts = [csrc[s * PAST_LEN:(s + 1) * PAST_LEN, cols], wsrc[s * 3 * B:(s + 1) * 3 * B, cols]]
                o = _sink_attend_t(parts, [cvts[s], vts[s]], sink2[:, 2 * qt + s:2 * qt + s + 1])
                acc = o if acc is None else acc + o
            o_ref[:, qt * LANES:(qt + 1) * LANES] = jnp.transpose(acc).astype(o_ref.dtype)

    _attn_schedule(ATT_KV, scores, attend)


def _lat_attn(p0, ck, cv, sink):
    nb = DEC_SEQ // ATT_BLOCK
    base = T_CTX // ATT_BLOCK

    def kv(start, shift):
        return pl.BlockSpec((ATT_BLOCK, ATT_KV_DIM),
                            lambda b, i: (base + b * nb + jnp.clip(i + shift, 0, nb - 1), start // ATT_KV_DIM))

    def full(shape):
        return pl.BlockSpec(shape, lambda b, i: (0,) * len(shape))

    cache = pl.BlockSpec((None, PAST_LEN, ATT_KV_DIM), lambda b, i: (b, 0, 0))
    cos, sa, sb = _rope_tables()
    return pl.pallas_call(
        _lat_attn_kernel,
        grid=(DEC_BATCH, nb),
        in_specs=[pl.BlockSpec((ATT_BLOCK, D), lambda b, i: (base + b * nb + i, P0_Q // D)),
                  kv(P0_K, -1), kv(P0_K, 0), kv(P0_K, 1), kv(P0_V, -1), kv(P0_V, 0), kv(P0_V, 1),
                  cache, cache, full((DEC_SEQ, LANES)), full((DEC_SEQ, LANES)), full((DEC_SEQ, LANES)),
                  full((1, LANES))],
        out_specs=pl.BlockSpec((ATT_BLOCK, D), lambda b, i: (b * nb + i, 0)),
        out_shape=jax.ShapeDtypeStruct((T_LAT, D), BF16),
        scratch_shapes=[pltpu.VMEM((2 * PAST_LEN, 2 * ATT_BLOCK), F32)] * 2
        + [pltpu.VMEM((2 * 3 * ATT_BLOCK, 2 * ATT_BLOCK), F32)] * 2,
        compiler_params=_cparams(2),
        name="lat_attn",
    )(p0, p0, p0, p0, p0, p0, p0, ck, cv, cos, sa, sb, sink)


def _log_sigmoid(x):
    return jnp.minimum(x, 0.0) - jnp.log(1.0 + jnp.exp(-jnp.abs(x)))


def _gla_kernel(*refs, seq, has_s0):
    if has_s0:
        (q_ref, k_ref, v_ref, r_ref, lr_ref, w2f_ref, w2b_ref, bf_ref, bb_ref, nw_ref, s0f_ref, s0b_ref,
         y_ref, sf_ref, sb_ref, gf, gb, yf, yb, stf, stb) = refs
    else:
        (q_ref, k_ref, v_ref, r_ref, lr_ref, w2f_ref, w2b_ref, bf_ref, bb_ref, nw_ref,
         y_ref, sf_ref, sb_ref, gf, gb, yf, yb, stf, stb) = refs
    C = GLA_C
    nc = seq // C
    lr = lr_ref[...]
    gf[...] = _log_sigmoid(_mm(lr, w2f_ref[...]) + bf_ref[...]) / GLA_GATE_NORM
    gb[...] = _log_sigmoid(_mm(lr, w2b_ref[...]) + bb_ref[...]) / GLA_GATE_NORM
    for h in range(GLA_HEADS):
        rows = slice(h * GLA_DV, (h + 1) * GLA_DV)
        if has_s0:
            stf[rows, :] = jnp.transpose(s0f_ref[h * GLA_DK:(h + 1) * GLA_DK, :])
            stb[rows, :] = jnp.transpose(s0b_ref[h * GLA_DK:(h + 1) * GLA_DK, :])
        else:
            stf[rows, :] = jnp.zeros((GLA_DV, GLA_DK), F32)
            stb[rows, :] = jnp.zeros((GLA_DV, GLA_DK), F32)

    row = lax.broadcasted_iota(jnp.int32, (C, C), 0)
    col = lax.broadcasted_iota(jnp.int32, (C, C), 1)
    qscale = GLA_DK ** -0.5

    def chunk(c, fwd):
        g_scr, y_scr, st = (gf, yf, stf) if fwd else (gb, yb, stb)
        r0 = pl.multiple_of(c * C, C)
        g = g_scr[pl.ds(r0, C), :]
        cs = _cumsum_rows(g, C)
        total = cs[C - 1:C, :]
        q = q_ref[pl.ds(r0, C), :].astype(F32) * qscale
        k = k_ref[pl.ds(r0, C), :].astype(F32)
        v = v_ref[pl.ds(r0, C), :]
        if fwd:
            qs, ks, ke = q * jnp.exp(cs), k * jnp.exp(-cs), k * jnp.exp(total - cs)
            keep = col <= row
        else:
            ex = cs - g
            qs, ks, ke = q * jnp.exp(total - ex), k * jnp.exp(ex - total), k * jnp.exp(ex)
            keep = col >= row
        dec = jnp.exp(total)
        for h in range(GLA_HEADS):
            kc = slice(h * GLA_DK, (h + 1) * GLA_DK)
            vc = slice(h * GLA_DV, (h + 1) * GLA_DV)
            s_t = st[vc, :]
            att = jnp.where(keep, _mm_nt(qs[:, kc], ks[:, kc]), 0.0)
            y_scr[pl.ds(r0, C), vc] = _mm(att, v[:, vc]) + _mm_nt(qs[:, kc], s_t)
            st[vc, :] = dec[:, kc] * s_t + _mm_tn(v[:, vc], ke[:, kc])

    def body(i, carry):
        chunk(i, True)
        chunk(nc - 1 - i, False)
        return carry

    lax.fori_loop(0, nc, body, 0)

    nw = nw_ref[...]
    for blk in range(seq // LANES):
        rs = slice(blk * LANES, (blk + 1) * LANES)
        y = yf[rs, :] + yb[rs, :]
        gate = _silu(r_ref[rs, :].astype(F32))
        for h in range(GLA_HEADS):
            vc = slice(h * GLA_DV, (h + 1) * GLA_DV)
            y_ref[rs, vc] = (_rms(y[:, vc], nw) * gate[:, vc]).astype(y_ref.dtype)
    for h in range(GLA_HEADS):
        rows = slice(h * GLA_DV, (h + 1) * GLA_DV)
        sf_ref[h * GLA_DK:(h + 1) * GLA_DK, :] = jnp.transpose(stf[rows, :])
        sb_ref[h * GLA_DK:(h + 1) * GLA_DK, :] = jnp.transpose(stb[rows, :])


def _gla(p1, p1lr, tok0, nseq, seq, w2f, w2b, bgf, bgb, nw, s0f=None, s0b=None):
    has_s0 = s0f is not None
    b0 = tok0 // seq
    dk_all, dv_all = GLA_HEADS * GLA_DK, GLA_HEADS * GLA_DV

    def cols(width, start):
        return pl.BlockSpec((seq, width), lambda s: (b0 + s, start // width))

    def full(shape):
        return pl.BlockSpec(shape, lambda s: (0,) * len(shape))

    in_specs = [cols(dk_all, P1_Q), cols(dk_all, P1_K), cols(dv_all, P1_V), cols(dv_all, P1_R), cols(LANES, 0),
                full((LANES, dk_all)), full((LANES, dk_all)), full((1, dk_all)), full((1, dk_all)), full((1, GLA_DV))]
    args = [p1, p1, p1, p1, p1lr, w2f, w2b, bgf.reshape(1, -1), bgb.reshape(1, -1), nw.reshape(1, -1)]
    st_spec = pl.BlockSpec((None, dk_all, GLA_DV), lambda s: (s, 0, 0))
    if has_s0:
        in_specs += [st_spec, st_spec]
        args += [s0f, s0b]
    st_shape = jax.ShapeDtypeStruct((nseq, dk_all, GLA_DV), F32)
    return pl.pallas_call(
        functools.partial(_gla_kernel, seq=seq, has_s0=has_s0),
        grid=(nseq,), in_specs=in_specs,
        out_specs=[pl.BlockSpec((seq, dv_all), lambda s: (s, 0)), st_spec, st_spec],
        out_shape=[jax.ShapeDtypeStruct((nseq * seq, dv_all), BF16), st_shape, st_shape],
        scratch_shapes=[pltpu.VMEM((seq, dk_all), F32), pltpu.VMEM((seq, dk_all), F32),
                        pltpu.VMEM((seq, dv_all), F32), pltpu.VMEM((seq, dv_all), F32),
                        pltpu.VMEM((dv_all, GLA_DK), F32), pltpu.VMEM((dv_all, GLA_DK), F32)],
        compiler_params=_cparams(1),
        name=f"gla{seq}",
    )(*args)


ROUTE_TM = 512
ROUTE_SUB = 256
ROUTE_ROWS = 32


def _split_bf16(x):
    hi = x.astype(BF16)
    return hi, (x - hi.astype(F32)).astype(BF16)


def _outproj_kernel(*refs, n_in, dual_x):
    y_refs = refs[:2 * n_in]
    n_x = 2 if dual_x else 1
    x_refs = refs[2 * n_in + 1:2 * n_in + 1 + n_x]
    w_ref = refs[2 * n_in]
    (g1_ref, sh_ref, sc_ref, nw_ref, wr_ref, br_ref,
     xo_ref, tr_ref, route_ref, cnt_ref, w_scr, wr_hl, carry) = refs[2 * n_in + 1 + n_x:]
    is_ctx = pl.program_id(0) < T_CTX // ROUTE_TM

    @pl.when(pl.program_id(0) == 0)
    def _():
        w_scr[...] = w_ref[...].astype(BF16)
        hi, lo = _split_bf16(jnp.transpose(wr_ref[...]))
        wr_hl[0:LANES, :] = hi
        wr_hl[LANES:2 * LANES, :] = lo
        carry[...] = jnp.zeros(carry.shape, F32)

    for sub in range(ROUTE_TM // ROUTE_SUB):
        _outproj_subtile(sub, is_ctx, y_refs, x_refs, dual_x, n_in, g1_ref, sh_ref, sc_ref, nw_ref, br_ref,
                         xo_ref, tr_ref, route_ref, cnt_ref, w_scr, wr_hl, carry)


def _outproj_subtile(sub, is_ctx, y_refs, x_refs, dual_x, n_in, g1_ref, sh_ref, sc_ref, nw_ref, br_ref,
                     xo_ref, tr_ref, route_ref, cnt_ref, w_scr, wr_hl, carry):
    tm = ROUTE_SUB
    rows = slice(sub * tm, (sub + 1) * tm)
    o = None
    for i in range(n_in):
        y = jnp.where(is_ctx, y_refs[2 * i][rows, :], y_refs[2 * i + 1][rows, :])
        d = jnp.dot(y, w_scr[i * D:(i + 1) * D, :], preferred_element_type=F32)
        o = d if o is None else o + d
    x_in = jnp.where(is_ctx, x_refs[0][rows, :], x_refs[1][rows, :]) if dual_x else x_refs[0][rows, :]
    x = x_in + g1_ref[0] * o
    xo_ref[rows, :] = x
    t = _rms(x, nw_ref[...]) * (1.0 + sc_ref[0]) + sh_ref[0]
    for k in range(D // LANES):
        tr_ref[pl.ds(sub * tm * SUBLANES + k, tm, stride=SUBLANES), :] = t[:, k * LANES:(k + 1) * LANES]

    t_hi, t_lo = _split_bf16(t)
    lg = _mm_nt(wr_hl[...], t_hi)
    nr = ROUTE_ROWS
    logit = lg[0:nr, :] + lg[LANES:LANES + nr, :] + _mm_nt(wr_hl[0:LANES, :], t_lo)[0:nr, :] + br_ref[0:nr, :]
    rowf = lax.broadcasted_iota(jnp.int32, (nr, tm), 0).astype(F32)
    neg = -jnp.inf

    def first_argmax(v, vmax):
        return jnp.min(jnp.where(v == vmax, rowf, float(LANES)), axis=0, keepdims=True)

    gl = jnp.where(rowf < N_GROUPS, logit, neg)
    gmax = jnp.max(gl, axis=0, keepdims=True)
    gsel = first_argmax(gl, gmax)
    gprob = 1.0 / jnp.sum(jnp.exp(gl - gmax), axis=0, keepdims=True)
    first = N_GROUPS + EXP_PER_GROUP * gsel
    el = jnp.where((rowf >= first) & (rowf < first + EXP_PER_GROUP), logit, neg)
    m1 = jnp.max(el, axis=0, keepdims=True)
    i1 = first_argmax(el, m1)
    el2 = jnp.where(rowf == i1, neg, el)
    m2 = jnp.max(el2, axis=0, keepdims=True)
    i2 = first_argmax(el2, m2)
    e2 = jnp.exp(m2 - m1)
    c1 = gprob / (1.0 + e2)
    c2 = gprob * e2 / (1.0 + e2)
    x1 = i1 - N_GROUPS
    x2 = i2 - N_GROUPS

    erow = rowf
    hot = ((erow == x1) | (erow == x2)).astype(F32)
    tri = (lax.broadcasted_iota(jnp.int32, (tm, tm), 0) < lax.broadcasted_iota(jnp.int32, (tm, tm), 1))
    before = _mm(hot, tri.astype(F32)) + carry[...]
    r1 = jnp.sum(jnp.where(erow == x1, before, 0.0), axis=0, keepdims=True)
    r2 = jnp.sum(jnp.where(erow == x2, before, 0.0), axis=0, keepdims=True)
    total = carry[...] + _mm(hot, jnp.ones((tm, tm), F32))
    carry[...] = total
    cnt_ref[...] = total[0:N_EXPERTS, 0:LANES]
    row8 = lax.broadcasted_iota(jnp.int32, (SUBLANES, tm), 0)
    out = jnp.zeros((SUBLANES, tm), F32)
    for k, v in enumerate((x1, x2, c1, c2, r1, r2)):
        out = jnp.where(row8 == k, jnp.broadcast_to(v, (SUBLANES, tm)), out)
    route_ref[:, rows] = out


def _outproj_route(ys, w_out, xs, mods, layer, norm_w, w_router, b_router):
    tm = ROUTE_TM
    n_in = len(ys) // 2
    dual_x = len(xs) == 2
    kdim = w_out.shape[0]

    def full(shape):
        return pl.BlockSpec(shape, lambda i: (0,) * len(shape))

    tile = pl.BlockSpec((tm, D), lambda i: (i, 0))
    pair = list(_ctx_lat_specs(tm))
    in_specs = (pair * n_in + [full((kdim, D))] + (pair if dual_x else [tile])
                + [_mod_spec(layer, 2, tm), _mod_spec(layer, 3, tm), _mod_spec(layer, 4, tm),
                   full((1, D)), full((D, LANES)), full((LANES, ROUTE_SUB))])
    cnt = jax.ShapeDtypeStruct((N_EXPERTS, LANES), F32)
    return pl.pallas_call(
        functools.partial(_outproj_kernel, n_in=n_in, dual_x=dual_x),
        grid=(T // tm,), in_specs=in_specs,
        out_specs=[tile, pl.BlockSpec((tm * SUBLANES, LANES), lambda i: (i, 0)),
                   pl.BlockSpec((SUBLANES, tm), lambda i: (0, i)), full(cnt.shape)],
        out_shape=[jax.ShapeDtypeStruct((T, D), F32), jax.ShapeDtypeStruct((T * SUBLANES, LANES), F32),
                   jax.ShapeDtypeStruct((SUBLANES, T), F32), cnt],
        scratch_shapes=[pltpu.VMEM((kdim, D), BF16), pltpu.VMEM((2 * LANES, D), BF16),
                        pltpu.VMEM((ROUTE_ROWS, ROUTE_SUB), F32)],
        compiler_params=_cparams(1),
        name=f"outproj{layer}",
    )(*ys, w_out, *xs, mods, mods, mods, norm_w.reshape(1, D), w_router,
      jnp.broadcast_to(b_router.reshape(LANES, 1), (LANES, ROUTE_SUB)))


def _moe_meta(counts):
    tm = MOE_TM
    experts = jnp.arange(N_EXPERTS, dtype=jnp.int32)
    counts = jnp.max(counts, axis=1).astype(jnp.int32)
    padded = ((counts + tm - 1) // tm) * tm
    ends = jnp.cumsum(padded)
    tile_start = jnp.arange(MOE_TILES, dtype=jnp.int32) * tm
    te = jnp.sum((tile_start[:, None] >= ends[None, :]).astype(jnp.int32), axis=1)
    last = jnp.max(jnp.where(counts > 0, experts, 0))
    meta = jnp.concatenate([jnp.minimum(te, last), ends[-1:] // tm]).astype(jnp.int32)
    starts = ends - padded
    pads = jnp.concatenate([starts + counts, ends[-1:], ends, jnp.full((1,), MOE_ROWS)]).astype(jnp.int32)
    return starts.astype(jnp.int32), pads, meta


def _expert_changed(meta_ref, j):
    return (j == 0) | (meta_ref[j] != meta_ref[jnp.maximum(j - 1, 0)])


def _moe_up_kernel(pos1_ref, pos2_ref, pads_ref, meta_ref, tr_hbm, wg_ref, wu_ref, a_ref, rowmap_ref,
                   tr_scr, g0, g1, wg_bf, wu_bf, sem):
    j = pl.program_id(0)
    tm = MOE_TM
    ntiles = meta_ref[MOE_TILES]

    def gather(tile, dst):
        for mi in range(tm):
            tok = jnp.minimum(rowmap_ref[tile * tm + mi] >> 1, T - 1)
            dst[mi * SUBLANES:(mi + 1) * SUBLANES, :] = tr_scr[pl.ds(pl.multiple_of(tok * SUBLANES, SUBLANES), SUBLANES), :]

    @pl.when(j == 0)
    def _():
        load = pltpu.make_async_copy(tr_hbm, tr_scr, sem)
        load.start()

        def clear(c, carry):
            for i in range(SUBLANES):
                rowmap_ref[c * SUBLANES + i] = 2 * T
            return carry
        for k in range(N_EXPERTS + 1):
            lax.fori_loop(pads_ref[k] // SUBLANES, pads_ref[N_EXPERTS + 1 + k] // SUBLANES, clear, 0)

        def place(t, carry):
            rowmap_ref[pos1_ref[t]] = 2 * t
            rowmap_ref[pos2_ref[t]] = 2 * t + 1
            return carry
        lax.fori_loop(0, T, place, 0, unroll=8)
        load.wait()
        gather(0, g0)

    def compute(cur, nxt):
        gather(jnp.minimum(j + 1, ntiles - 1), nxt)
        x = _tokmajor_to_std(cur, tm).astype(BF16)
        g = jnp.dot(x, wg_bf[...], preferred_element_type=F32)
        u = jnp.dot(x, wu_bf[...], preferred_element_type=F32)
        a_ref[...] = (_silu(g) * u).astype(a_ref.dtype)

    @pl.when(j < ntiles)
    def _():
        @pl.when(_expert_changed(meta_ref, j))
        def _():
            wg_bf[...] = wg_ref[...].astype(BF16)
            wu_bf[...] = wu_ref[...].astype(BF16)

        pl.when(j % 2 == 0)(functools.partial(compute, g0, g1))
        pl.when(j % 2 == 1)(functools.partial(compute, g1, g0))

    @pl.when(j >= ntiles)
    def _():
        a_ref[...] = jnp.zeros(a_ref.shape, a_ref.dtype)


def _moe_down_kernel(rowmap_ref, cpair_ref, meta_ref, a_ref, wd_ref, out_hbm, acc, y0, y1, wd_bf):
    j = pl.program_id(0)
    tm = MOE_TM
    zrows = 512
    ntiles = meta_ref[MOE_TILES]

    @pl.when(j == 0)
    def _():
        def zero(i, carry):
            acc[pl.ds(pl.multiple_of(i * zrows, zrows), zrows), :] = jnp.zeros((zrows, LANES), F32)
            return carry
        lax.fori_loop(0, ACC_ROWS // zrows, zero, 0)

    def matmul(dst):
        y = jnp.dot(a_ref[...], wd_bf[...], preferred_element_type=F32)
        for k in range(D // LANES):
            dst[pl.ds(k, tm, stride=SUBLANES), :] = y[:, k * LANES:(k + 1) * LANES]

    def scatter(tile, src):
        for b in range(tm // SUBLANES):
            ents = [rowmap_ref[tile * tm + b * SUBLANES + i] for i in range(SUBLANES)]
            offs = [pl.multiple_of((e >> 1) * SUBLANES, SUBLANES) for e in ents]
            olds = [acc[pl.ds(o, SUBLANES), :] for o in offs]
            for i, o in enumerate(offs):
                r = (b * SUBLANES + i) * SUBLANES
                acc[pl.ds(o, SUBLANES), :] = olds[i] + cpair_ref[ents[i]] * src[r:r + SUBLANES, :]

    has_mm = j < ntiles
    has_sc = (j >= 1) & (j <= ntiles)

    @pl.when(has_mm)
    def _():
        @pl.when(_expert_changed(meta_ref, j))
        def _():
            wd_bf[...] = wd_ref[...].astype(BF16)

    for par, (cur, prev) in enumerate(((y0, y1), (y1, y0))):
        mine = (j % 2) == par

        @pl.when(mine & has_mm & has_sc)
        def _():
            matmul(cur)
            scatter(j - 1, prev)

        @pl.when(mine & has_mm & jnp.logical_not(has_sc))
        def _():
            matmul(cur)

        @pl.when(mine & jnp.logical_not(has_mm) & has_sc)
        def _():
            scatter(j - 1, prev)

    @pl.when(j == pl.num_programs(0) - 1)
    def _():
        pltpu.sync_copy(acc.at[0:T * SUBLANES, :], out_hbm)


def _moe(tr, route_t, counts, layer, w_gate, w_up, w_down):
    tm = MOE_TM
    starts, pads, meta = _moe_meta(counts)
    experts = jnp.arange(N_EXPERTS, dtype=jnp.int32)

    def position(e, r):
        sel = e.astype(jnp.int32)[:, None] == experts[None, :]
        return jnp.sum(jnp.where(sel, starts[None, :], 0), axis=1) + r.astype(jnp.int32)

    pos1 = position(route_t[0], route_t[4])
    pos2 = position(route_t[1], route_t[5])
    cpair = jnp.concatenate([jnp.stack([route_t[2], route_t[3]], axis=1).reshape(2 * T), jnp.zeros((2,), F32)])

    def wspec(shape, n):
        return pl.BlockSpec((None, None) + shape,
                            lambda j, *pre: (layer, pre[n - 1][jnp.minimum(j, MOE_TILES - 1)], 0, 0))

    gscr = pltpu.VMEM((tm * SUBLANES, LANES), F32)
    act, rowmap = pl.pallas_call(
        _moe_up_kernel,
        grid_spec=pltpu.PrefetchScalarGridSpec(
            num_scalar_prefetch=4, grid=(MOE_TILES,),
            in_specs=[pl.BlockSpec(memory_space=pl.ANY), wspec((D, D_EXPERT), 4), wspec((D, D_EXPERT), 4)],
            out_specs=[pl.BlockSpec((tm, D_EXPERT), lambda j, *pre: (j, 0)), pl.BlockSpec(memory_space=pltpu.SMEM)],
            scratch_shapes=[pltpu.VMEM((T * SUBLANES, LANES), F32), gscr, gscr,
                            pltpu.VMEM((D, D_EXPERT), BF16), pltpu.VMEM((D, D_EXPERT), BF16),
                            pltpu.SemaphoreType.DMA(())]),
        out_shape=[jax.ShapeDtypeStruct((MOE_ROWS, D_EXPERT), BF16), jax.ShapeDtypeStruct((MOE_ROWS,), jnp.int32)],
        compiler_params=_cparams(1),
        name=f"moe_up{layer}",
    )(pos1, pos2, pads, meta, tr, w_gate, w_up)

    return pl.pallas_call(
        _moe_down_kernel,
        grid_spec=pltpu.PrefetchScalarGridSpec(
            num_scalar_prefetch=3, grid=(MOE_TILES + 1,),
            in_specs=[pl.BlockSpec((tm, D_EXPERT), lambda j, *pre: (jnp.minimum(j, MOE_TILES - 1), 0)),
                      wspec((D_EXPERT, D), 3)],
            out_specs=pl.BlockSpec(memory_space=pl.ANY),
            scratch_shapes=[pltpu.VMEM((ACC_ROWS, LANES), F32), gscr, gscr, pltpu.VMEM((D_EXPERT, D), BF16)]),
        out_shape=jax.ShapeDtypeStruct((T * SUBLANES, LANES), F32),
        compiler_params=_cparams(1),
        name=f"moe_down{layer}",
    )(rowmap, cpair, meta, act, w_down)


FINAL_TM = 512


def _final_kernel(x_ref, moe_ref, g2_ref, nw_ref, oc_ref, ol_ref):
    x = x_ref[...] + g2_ref[0] * _tokmajor_to_std(moe_ref, FINAL_TM)
    y = _rms(x, nw_ref[...])
    is_ctx = pl.program_id(0) < T_CTX // FINAL_TM

    @pl.when(is_ctx)
    def _():
        oc_ref[...] = y

    @pl.when(jnp.logical_not(is_ctx))
    def _():
        ol_ref[...] = y


def _final(x, moe, mods, layer, norm_w):
    tm = FINAL_TM
    tile = pl.BlockSpec((tm, D), lambda i: (i, 0))
    return pl.pallas_call(
        _final_kernel,
        grid=(T // tm,),
        in_specs=[tile, pl.BlockSpec((tm * SUBLANES, LANES), lambda i: (i, 0)), _mod_spec(layer, 5, tm),
                  pl.BlockSpec((1, D), lambda i: (0, 0))],
        out_specs=list(_ctx_lat_specs(tm)),
        out_shape=[jax.ShapeDtypeStruct((T_CTX, D), F32), jax.ShapeDtypeStruct((T_LAT, D), F32)],
        compiler_params=_cparams(1),
        name="final_norm",
    )(x, moe, mods, norm_w.reshape(1, D))


def _pad_lanes(v):
    return jnp.pad(v.astype(F32), (0, LANES - v.shape[0])).reshape(1, LANES)


def kernel(x_prompt, x_sample, cache_k_attn, cache_v_attn, state_ssd_fwd, state_ssd_bwd, state_gla_fwd, state_gla_bwd, c, c_ctx, w_ada, b_ada, norm_mix_w, norm_ffn_w, w_in_even, conv_w, conv_b, dt_bias_fwd, dt_bias_bwd, a_log_fwd, a_log_bwd, d_skip, ssd_norm_w, attn_sink, w_out_even, w_in_odd, w_gk2_fwd, b_gk_fwd, w_gk2_bwd, b_gk_bwd, gla_norm_w, w_out_odd, w_router_group, b_router_group, w_router_expert, b_router_expert, w_gate_exp, w_up_exp, w_down_exp, final_norm_w):
    depth = w_ada.shape[0]
    assert depth == 2 and x_prompt.shape == (BATCH, SEQ, D) and x_sample.shape == (DEC_BATCH, DEC_SEQ, D)

    cond8 = jnp.concatenate([c_ctx[None, :], c, jnp.zeros((SUBLANES - 1 - DEC_BATCH, D), F32)], axis=0)
    mods = _adaln(cond8, w_ada, b_ada).reshape(depth * SUBLANES * 6, 1, D)
    xs0 = (x_prompt.reshape(T_CTX, D), x_sample.reshape(T_LAT, D))

    def router_params(i):
        wr = jnp.concatenate([w_router_group[i], w_router_expert[i],
                              jnp.zeros((D, LANES - N_GROUPS - N_EXPERTS), F32)], axis=1)
        return wr, _pad_lanes(jnp.concatenate([b_router_group[i], b_router_expert[i]]))

    def pad_cols(w):
        return jnp.concatenate([w, jnp.zeros((D, LANES - w.shape[1]), F32)], axis=1)

    w = w_in_even[0]
    n_zxbc = 2 * D_SSD + 2 * SSD_GROUPS * SSD_N
    n_dt = 2 * SSD_HEADS
    n_main = n_zxbc // PROJ_TN
    bc_tile, q_tiles = n_main - 1, D // PROJ_TN

    def out_block0(j):
        return jnp.where(j == bc_tile, bc_tile + q_tiles, jnp.where((j > bc_tile) & (j <= bc_tile + q_tiles), j - 1, j))

    p0, p0dt = _modproj(xs0, mods, 0, norm_mix_w[0], w_in_even, n_main, pad_cols(w[:, n_zxbc:n_zxbc + n_dt]),
                        out_block0, w_tail=w[:, n_zxbc + n_dt:])

    dtb = _pad_lanes(jnp.concatenate([dt_bias_fwd[0], dt_bias_bwd[0]]))
    alog = _pad_lanes(jnp.concatenate([a_log_fwd[0], a_log_bwd[0]]))
    dskip = jnp.repeat(d_skip[0], SSD_P).reshape(1, D_SSD)
    ssd_args = (conv_w[0], conv_b[0], dtb, alog, dskip, ssd_norm_w[0])
    y_ssd_c, ssd_f, ssd_b = _ssd(p0, p0dt, 0, BATCH, SEQ, *ssd_args)
    y_ssd_l, _, _ = _ssd(p0, p0dt, T_CTX, DEC_BATCH, DEC_SEQ, *ssd_args,
                         h0f=state_ssd_fwd[:, 0].reshape(DEC_BATCH, D_SSD, SSD_N),
                         h0b=state_ssd_bwd[:, 0].reshape(DEC_BATCH, D_SSD, SSD_N))
    sink = _pad_lanes(attn_sink[0])
    y_att_c = _ctx_attn(p0, sink)
    y_att_l = _lat_attn(p0, cache_k_attn[:, 0].reshape(DEC_BATCH, PAST_LEN, ATT_KV_DIM),
                        cache_v_attn[:, 0].reshape(DEC_BATCH, PAST_LEN, ATT_KV_DIM), sink)
    xmid0, tr0, route0, cnt0 = _outproj_route([y_ssd_c, y_ssd_l, y_att_c, y_att_l], w_out_even[0], xs0, mods, 0,
                                              norm_ffn_w[0], *router_params(0))
    moe0 = _moe(tr0, route0, cnt0, 0, w_gate_exp, w_up_exp, w_down_exp)

    dk_all = GLA_HEADS * GLA_DK
    n_qkvr = 2 * dk_all + 2 * GLA_HEADS * GLA_DV
    p1, p1lr, x1 = _modproj(xmid0, mods, 1, norm_mix_w[1], w_in_odd, n_qkvr // PROJ_TN,
                            pad_cols(w_in_odd[0][:, n_qkvr:]), lambda j: j, moe=moe0)
    w2f = jnp.zeros((LANES, dk_all), F32).at[:GLA_LOWRANK].set(w_gk2_fwd[0])
    w2b = jnp.zeros((LANES, dk_all), F32).at[GLA_LOWRANK:2 * GLA_LOWRANK].set(w_gk2_bwd[0])
    gla_args = (w2f, w2b, b_gk_fwd[0], b_gk_bwd[0], gla_norm_w[0])
    y_gla_c, gla_f, gla_b = _gla(p1, p1lr, 0, BATCH, SEQ, *gla_args)
    y_gla_l, _, _ = _gla(p1, p1lr, T_CTX, DEC_BATCH, DEC_SEQ, *gla_args,
                         s0f=state_gla_fwd[:, 0].reshape(DEC_BATCH, dk_all, GLA_DV),
                         s0b=state_gla_bwd[:, 0].reshape(DEC_BATCH, dk_all, GLA_DV))
    xmid1, tr1, route1, cnt1 = _outproj_route([y_gla_c, y_gla_l], w_out_odd[0], (x1,), mods, 1,
                                              norm_ffn_w[1], *router_params(1))
    moe1 = _moe(tr1, route1, cnt1, 1, w_gate_exp, w_up_exp, w_down_exp)
    y_c, y_l = _final(xmid1, moe1, mods, 1, final_norm_w)

    y_prompt = y_c.reshape(BATCH, SEQ, D)
    y_sample = y_l.reshape(DEC_BATCH, DEC_SEQ, D)
    new_k = p0[:T_CTX, P0_K:P0_K + ATT_KV_DIM].astype(F32).reshape(BATCH, 1, SEQ, ATT_KV, ATT_HD)
    new_v = p0[:T_CTX, P0_V:P0_V + ATT_KV_DIM].astype(F32).reshape(BATCH, 1, SEQ, ATT_KV, ATT_HD)
    return (y_prompt, y_sample, new_k, new_v,
            ssd_f.reshape(BATCH, 1, SSD_HEADS, SSD_P, SSD_N), ssd_b.reshape(BATCH, 1, SSD_HEADS, SSD_P, SSD_N),
            gla_f.reshape(BATCH, 1, GLA_HEADS, GLA_DK, GLA_DV), gla_b.reshape(BATCH, 1, GLA_HEADS, GLA_DK, GLA_DV))
```

```python
import functools
import math

import numpy as np
import jax
import jax.numpy as jnp
from jax import lax
from jax.experimental import pallas as pl
from jax.experimental.pallas import tpu as pltpu

F32 = jnp.float32
BF16 = jnp.bfloat16

D = 1024
BATCH, SEQ = 16, 256
DEC_BATCH, DEC_SEQ = 2, 1024
PAST_LEN = 512
GRID_W = 64
EPS = 1e-6
T_CTX = BATCH * SEQ
T_LAT = DEC_BATCH * DEC_SEQ
T = T_CTX + T_LAT

SSD_HEADS, SSD_P, SSD_N, SSD_GROUPS = 16, 64, 128, 2
SSD_CONV = 5
SSD_L = 128
D_SSD = SSD_HEADS * SSD_P
HEADS_PER_GROUP = SSD_HEADS // SSD_GROUPS
GROUP_W = HEADS_PER_GROUP * SSD_P

ATT_HEADS, ATT_KV, ATT_HD = 16, 4, 64
ATT_KV_DIM = ATT_KV * ATT_HD
WINDOW = 128
ATT_BLOCK = 128
ATT_SCALE = ATT_HD ** -0.5
ROPE_BASE = 10000.0

GLA_HEADS, GLA_DK, GLA_DV = 4, 128, 256
GLA_C = 64
GLA_GATE_NORM = 16.0
GLA_LOWRANK = 16

N_GROUPS, EXP_PER_GROUP = 4, 4
N_EXPERTS = 16
D_EXPERT = 512

LANES = 128
SUBLANES = 8
VMEM_LIMIT = 56 * 1024 * 1024

P0_Z, P0_X, P0_Q, P0_BC, P0_K, P0_V = 0, 1024, 2048, 3072, 3584, 3840
P0_W = 4096
P1_Q, P1_K, P1_V, P1_R = 0, 512, 1024, 2048
P1_W = 3072

MOE_TM = 256
MOE_TILES = (2 * T) // MOE_TM + N_EXPERTS
MOE_ROWS = MOE_TILES * MOE_TM
ACC_ROWS = (T + SUBLANES) * SUBLANES


def _cparams(n_axes, vmem=VMEM_LIMIT):
    return pltpu.CompilerParams(dimension_semantics=("arbitrary",) * n_axes, vmem_limit_bytes=vmem)


def _silu(x):
    return x / (1.0 + jnp.exp(-x))


def _softplus(x):
    return jnp.maximum(x, 0.0) + jnp.log(1.0 + jnp.exp(-jnp.abs(x)))


def _mm(a, b):
    return jnp.dot(a.astype(BF16), b.astype(BF16), preferred_element_type=F32)


def _mm_nt(a, b):
    return lax.dot_general(a.astype(BF16), b.astype(BF16), (((1,), (1,)), ((), ())),
                           preferred_element_type=F32)


def _mm_tn(a, b):
    return lax.dot_general(a.astype(BF16), b.astype(BF16), (((0,), (0,)), ((), ())),
                           preferred_element_type=F32)


def _rms(x, w):
    return x * lax.rsqrt(jnp.mean(x * x, axis=-1, keepdims=True) + EPS) * w


def _cumsum_rows(x, n):
    row = lax.broadcasted_iota(jnp.int32, x.shape, 0)
    s = 1
    while s < n:
        x = x + jnp.where(row >= s, pltpu.roll(x, s, 0), 0.0)
        s *= 2
    return x


def _mod_row(tok0):
    return jnp.where(tok0 < T_CTX, 0, 1 + (tok0 - T_CTX) // DEC_SEQ)


ADA_TN = 1536


def _adaln_kernel(c_ref, w_ref, b_ref, o_ref):
    s = _silu(c_ref[...])
    o_ref[0] = _mm(s, w_ref[0]) + b_ref[0]


def _adaln(cond8, w_ada, b_ada):
    depth = w_ada.shape[0]
    return pl.pallas_call(
        _adaln_kernel,
        grid=(depth, 6 * D // ADA_TN),
        in_specs=[
            pl.BlockSpec((SUBLANES, D), lambda l, j: (0, 0)),
            pl.BlockSpec((1, D, ADA_TN), lambda l, j: (l, 0, j)),
            pl.BlockSpec((1, 1, ADA_TN), lambda l, j: (l, 0, j)),
        ],
        out_specs=pl.BlockSpec((1, SUBLANES, ADA_TN), lambda l, j: (l, 0, j)),
        out_shape=jax.ShapeDtypeStruct((depth, SUBLANES, 6 * D), F32),
        compiler_params=_cparams(2),
        name="adaln",
    )(cond8, w_ada, b_ada.reshape(depth, 1, 6 * D))


def _mod_spec(layer, chunk, tm, tile_of=lambda i, *_: i):
    return pl.BlockSpec((1, 1, D), lambda *g: ((layer * SUBLANES + _mod_row(tile_of(*g) * tm)) * 6 + chunk, 0, 0))


def _tokmajor_to_std(ref, tm):
    return jnp.concatenate([ref[pl.ds(k, tm, stride=SUBLANES), :] for k in range(D // LANES)], axis=1)


PROJ_TM = 1024
PROJ_TN = 512


def _ctx_lat_specs(tm, width=D):
    n_ctx = T_CTX // tm
    return (pl.BlockSpec((tm, width), lambda i, *_: (jnp.minimum(i, n_ctx - 1), 0)),
            pl.BlockSpec((tm, width), lambda i, *_: (jnp.maximum(i - n_ctx, 0), 0)))


def _modproj_kernel(*refs, dual_x):
    it = iter(refs)
    if dual_x:
        xc_ref, xl_ref = next(it), next(it)
    else:
        x_ref, moe_ref, g2_ref = next(it), next(it), next(it)
    sh_ref, sc_ref, nw_ref, w_ref, ws_ref, o_ref, os_ref = (next(it) for _ in range(7))
    xo_ref = None if dual_x else next(it)
    h_all, w_bf = next(it), next(it)
    j, i = pl.program_id(0), pl.program_id(1)
    tm = PROJ_TM
    rows = pl.ds(pl.multiple_of(i * tm, tm), tm)

    @pl.when(j == 0)
    def _():
        if dual_x:
            x = jnp.where(i < T_CTX // tm, xc_ref[...], xl_ref[...])
        else:
            x = x_ref[...] + g2_ref[0] * _tokmajor_to_std(moe_ref, tm)
            xo_ref[...] = x
        h = (_rms(x, nw_ref[...]) * (1.0 + sc_ref[0]) + sh_ref[0]).astype(BF16)
        h_all[rows, :] = h
        os_ref[...] = _mm_nt(h, ws_ref[...])

    @pl.when(i == 0)
    def _():
        w_bf[...] = w_ref[...].astype(BF16)

    o_ref[...] = _mm_nt(h_all[rows, :], w_bf[...]).astype(o_ref.dtype)


def _modproj(xs, mods, layer, norm_w, wt, tile_rows, small_row, out_block, moe=None):
    tm, tn = PROJ_TM, PROJ_TN
    dual_x = moe is None
    n_tiles = len(tile_rows)
    n_i, n_ctx = T // tm, T_CTX // tm

    def w_row(j, i):
        r = jnp.int32(tile_rows[0])
        for k in range(1, n_tiles):
            r = jnp.where(j == k, tile_rows[k], r)
        return pl.multiple_of(r, SUBLANES), 0

    def tok(j, i):
        return jnp.where(j == 0, i, n_i - 1)

    tile = pl.BlockSpec((tm, D), lambda j, i: (tok(j, i), 0))
    if dual_x:
        in_specs = [pl.BlockSpec((tm, D), lambda j, i: (jnp.minimum(tok(j, i), n_ctx - 1), 0)),
                    pl.BlockSpec((tm, D), lambda j, i: (jnp.maximum(tok(j, i) - n_ctx, 0), 0))]
        args = list(xs)
    else:
        in_specs = [tile, pl.BlockSpec((tm * SUBLANES, LANES), lambda j, i: (tok(j, i), 0)),
                    _mod_spec(layer - 1, 5, tm, tok)]
        args = [xs, moe, mods]
    in_specs += [_mod_spec(layer, 0, tm, tok), _mod_spec(layer, 1, tm, tok), pl.BlockSpec((1, D), lambda j, i: (0, 0)),
                 pl.BlockSpec((pl.Element(tn), pl.Element(D)), w_row),
                 pl.BlockSpec((pl.Element(LANES), pl.Element(D)), lambda j, i: (small_row, 0))]
    args += [mods, mods, norm_w.reshape(1, D), wt, wt]
    out_specs = [pl.BlockSpec((tm, tn), lambda j, i: (i, out_block(j))),
                 pl.BlockSpec((tm, LANES), lambda j, i: (tok(j, i), 0))]
    out_shape = [jax.ShapeDtypeStruct((T, n_tiles * tn), BF16), jax.ShapeDtypeStruct((T, LANES), F32)]
    if not dual_x:
        out_specs.append(tile)
        out_shape.append(jax.ShapeDtypeStruct((T, D), F32))
    return pl.pallas_call(
        functools.partial(_modproj_kernel, dual_x=dual_x),
        grid=(n_tiles, n_i), in_specs=in_specs, out_specs=out_specs, out_shape=out_shape,
        scratch_shapes=[pltpu.VMEM((T, D), BF16), pltpu.VMEM((tn, D), BF16)],
        compiler_params=_cparams(2),
        name=f"modproj{layer}",
    )(*args)


def _expand_heads(v, off):
    lo = lax.broadcasted_iota(jnp.int32, (v.shape[0], LANES), 1) < SSD_P
    tiles = []
    for q in range(SSD_HEADS // 2):
        a = jnp.broadcast_to(v[:, off + 2 * q:off + 2 * q + 1], (v.shape[0], LANES))
        b = jnp.broadcast_to(v[:, off + 2 * q + 1:off + 2 * q + 2], (v.shape[0], LANES))
        tiles.append(jnp.where(lo, a, b))
    return jnp.concatenate(tiles, axis=1)


def _ssd_kernel(*refs, seq, has_h0):
    if has_h0:
        (z_ref, x_ref, bc_ref, dt_ref, cwx_ref, cwbc_ref, cbx_ref, cbbc_ref, dtb_ref, alog_ref, dsk_ref,
         nw_ref, h0f_ref, h0b_ref, y_ref, sf_ref, sb_ref,
         xpad, bcpad, xc, bcc, a_scr, dt_scr, yacc, hf, hb) = refs
    else:
        (z_ref, x_ref, bc_ref, dt_ref, cwx_ref, cwbc_ref, cbx_ref, cbbc_ref, dtb_ref, alog_ref, dsk_ref,
         nw_ref, y_ref, sf_ref, sb_ref,
         xpad, bcpad, xc, bcc, a_scr, dt_scr, yacc, hf, hb) = refs
    L = SSD_L
    nc = seq // L
    pad = SUBLANES
    half = SSD_CONV // 2

    for buf, src, cw, cb, dst in ((xpad, x_ref, cwx_ref, cbx_ref, xc), (bcpad, bc_ref, cwbc_ref, cbbc_ref, bcc)):
        width = buf.shape[1]
        buf[0:pad, :] = jnp.zeros((pad, width), F32)
        buf[pad + seq:2 * pad + seq, :] = jnp.zeros((pad, width), F32)
        buf[pad:pad + seq, :] = src[...].astype(F32)
        for blk in range(nc):
            acc = jnp.broadcast_to(cb[...], (L, width))
            for j in range(SSD_CONV):
                r0 = pad - half + j + blk * L
                acc = acc + cw[j:j + 1, :] * buf[r0:r0 + L, :]
            dst[blk * L:(blk + 1) * L, :] = _silu(acc)

    lane = lax.broadcasted_iota(jnp.int32, (seq, LANES), 1)
    dts = jnp.where(lane < 2 * SSD_HEADS, _softplus(dt_ref[...] + dtb_ref[...]), 0.0)
    dt_scr[...] = dts
    a_scr[...] = dts * (-jnp.exp(alog_ref[...]))

    if has_h0:
        hf[...] = h0f_ref[...]
        hb[...] = h0b_ref[...]
    else:
        hf[...] = jnp.zeros(hf.shape, F32)
        hb[...] = jnp.zeros(hb.shape, F32)

    row = lax.broadcasted_iota(jnp.int32, (L, L), 0)
    col = lax.broadcasted_iota(jnp.int32, (L, L), 1)
    lane_l = lax.broadcasted_iota(jnp.int32, (L, LANES), 1)
    lo_half = lane_l < SSD_P

    def chunk(c, fwd, h_scr):
        off = 0 if fwd else SSD_HEADS
        r0 = pl.multiple_of(c * L, L)
        a = a_scr[pl.ds(r0, L), :]
        dt = dt_scr[pl.ds(r0, L), :]
        cs = _cumsum_rows(a, L)
        total = cs[L - 1:L, :]
        if fwd:
            u = cs
            rvec = jnp.exp(cs)
            ed = jnp.exp(total - cs) * dt
            keep = col <= row
        else:
            ex = cs - a
            u = -ex
            rvec = jnp.exp(total - ex)
            ed = jnp.exp(ex) * dt
            keep = col >= row
        ut = jnp.transpose(u)
        dtt = jnp.transpose(dt)
        tcol = jnp.transpose(jnp.broadcast_to(total, (L, LANES)))[:, 0:1]
        rexp = _expand_heads(rvec, off)
        edexp = _expand_heads(ed, off)
        x = xc[pl.ds(r0, L), :]
        bc = bcc[pl.ds(r0, L), :]
        outs = []
        for g in range(SSD_GROUPS):
            bg = bc[:, g * SSD_N:(g + 1) * SSD_N]
            cg = bc[:, SSD_GROUPS * SSD_N + g * SSD_N:SSD_GROUPS * SSD_N + (g + 1) * SSD_N]
            cbm = _mm_nt(cg, bg)
            hg = h_scr[g * GROUP_W:(g + 1) * GROUP_W, :]
            xg = x[:, g * GROUP_W:(g + 1) * GROUP_W]
            y_off = _mm_nt(cg, hg) * rexp[:, g * GROUP_W:(g + 1) * GROUP_W]
            tiles = []
            for p in range(HEADS_PER_GROUP // 2):
                xt = xg[:, p * LANES:(p + 1) * LANES]
                acc = None
                for s in range(2):
                    h = off + g * HEADS_PER_GROUP + 2 * p + s
                    seg = u[:, h:h + 1] - ut[h:h + 1, :]
                    m = cbm * jnp.exp(jnp.where(keep, seg, -jnp.inf)) * dtt[h:h + 1, :]
                    xm = jnp.where(lo_half if s == 0 else jnp.logical_not(lo_half), xt, 0.0)
                    d = _mm(m, xm)
                    acc = d if acc is None else acc + d
                tiles.append(acc)
            outs.append(y_off + jnp.concatenate(tiles, axis=1))
            decs = []
            for hh in range(HEADS_PER_GROUP):
                h = off + g * HEADS_PER_GROUP + hh
                decs.append(jnp.broadcast_to(jnp.exp(tcol[h:h + 1, :]), (SSD_P, SSD_N)))
            dec = jnp.concatenate(decs, axis=0)
            h_scr[g * GROUP_W:(g + 1) * GROUP_W, :] = dec * hg + _mm_tn(xg * edexp[:, g * GROUP_W:(g + 1) * GROUP_W], bg)
        return r0, x, jnp.concatenate(outs, axis=1)

    def fwd_body(c, carry):
        r0, _, y = chunk(c, True, hf)
        yacc[pl.ds(r0, L), :] = y
        return carry

    lax.fori_loop(0, nc, fwd_body, 0)

    def bwd_body(i, carry):
        c = nc - 1 - i
        r0, x, y = chunk(c, False, hb)
        y = yacc[pl.ds(r0, L), :] + y + dsk_ref[...] * x
        y = y * _silu(z_ref[pl.ds(r0, L), :].astype(F32))
        y_ref[pl.ds(r0, L), :] = _rms(y, nw_ref[...]).astype(y_ref.dtype)
        return carry

    lax.fori_loop(0, nc, bwd_body, 0)
    sf_ref[...] = hf[...]
    sb_ref[...] = hb[...]


def _ssd(p0, p0dt, tok0, nseq, seq, cw, cb, dtb, alog, dskip, nw, h0f=None, h0b=None):
    has_h0 = h0f is not None
    b0 = tok0 // seq

    def cols(width, start):
        return pl.BlockSpec((seq, width), lambda s: (b0 + s, start // width))

    def full(shape):
        return pl.BlockSpec(shape, lambda s: (0,) * len(shape))

    in_specs = [cols(D_SSD, P0_Z), cols(D_SSD, P0_X), cols(512, P0_BC), cols(LANES, 0),
                full((SSD_CONV, D_SSD)), full((SSD_CONV, 512)), full((1, D_SSD)), full((1, 512)),
                full((1, LANES)), full((1, LANES)), full((1, D_SSD)), full((1, D_SSD))]
    args = [p0, p0, p0, p0dt, cw[:, :D_SSD], cw[:, D_SSD:], cb[:D_SSD].reshape(1, -1), cb[D_SSD:].reshape(1, -1),
            dtb, alog, dskip, nw.reshape(1, -1)]
    st_spec = pl.BlockSpec((None, D_SSD, SSD_N), lambda s: (s, 0, 0))
    if has_h0:
        in_specs += [st_spec, st_spec]
        args += [h0f, h0b]
    st_shape = jax.ShapeDtypeStruct((nseq, D_SSD, SSD_N), F32)
    return pl.pallas_call(
        functools.partial(_ssd_kernel, seq=seq, has_h0=has_h0),
        grid=(nseq,), in_specs=in_specs,
        out_specs=[pl.BlockSpec((seq, D_SSD), lambda s: (s, 0)), st_spec, st_spec],
        out_shape=[jax.ShapeDtypeStruct((nseq * seq, D_SSD), BF16), st_shape, st_shape],
        scratch_shapes=[pltpu.VMEM((seq + 2 * SUBLANES, D_SSD), F32), pltpu.VMEM((seq + 2 * SUBLANES, 512), F32),
                        pltpu.VMEM((seq, D_SSD), F32), pltpu.VMEM((seq, 512), F32),
                        pltpu.VMEM((seq, LANES), F32), pltpu.VMEM((seq, LANES), F32),
                        pltpu.VMEM((seq, D_SSD), F32),
                        pltpu.VMEM((D_SSD, SSD_N), F32), pltpu.VMEM((D_SSD, SSD_N), F32)],
        compiler_params=_cparams(1),
        name=f"ssd{seq}",
    )(*args)


def _place_halves(tile, kv_in_high):
    lo = lax.broadcasted_iota(jnp.int32, tile.shape, 1) < ATT_HD
    swapped = pltpu.roll(tile, ATT_HD, 1)
    if kv_in_high:
        return jnp.where(lo, swapped, 0.0), jnp.where(lo, 0.0, tile)
    return jnp.where(lo, tile, 0.0), jnp.where(lo, 0.0, swapped)


def _place_rows(vt, kv_in_high):
    head = vt[ATT_HD:, :] if kv_in_high else vt[:ATT_HD, :]
    z = jnp.zeros_like(head)
    return jnp.concatenate([head, z], axis=0), jnp.concatenate([z, head], axis=0)


LOG2E = 1.4426950408889634
SCORE_SCALE = ATT_SCALE * LOG2E


def _sink_attend_t(score_parts, value_parts, sink2):
    m = sink2
    for s in score_parts:
        m = jnp.maximum(m, jnp.max(s, axis=0, keepdims=True))
    denom = jnp.exp2(sink2 - m)
    out = None
    for s, v in zip(score_parts, value_parts):
        p = jnp.exp2(s - m)
        denom = denom + jnp.sum(p, axis=0, keepdims=True)
        o = _mm(v, p)
        out = o if out is None else out + o
    return out * (1.0 / denom)


def _attn_schedule(n, scores, attend):
    scores(0)
    for j in range(n):
        if j + 1 < n:
            scores(j + 1)
        attend(j)


def _ctx_attn_kernel(q_ref, k_ref, v_ref, sink_ref, o_ref, s_a, s_b):
    sink2 = sink_ref[...] * LOG2E
    bufs = (s_a, s_b)
    half = SEQ // 2

    def kv_tile(ref, j):
        return ref[:, (j // 2) * LANES:(j // 2 + 1) * LANES].astype(F32), (j % 2 == 1)

    def scores(j):
        k_lo, k_hi = _place_halves(*kv_tile(k_ref, j))
        qst = jnp.concatenate([q_ref[:, qt * LANES:(qt + 1) * LANES] for qt in (2 * j, 2 * j + 1)], axis=0)
        bufs[j % 2][...] = _mm_nt(jnp.concatenate([k_lo, k_hi], axis=0), qst) * SCORE_SCALE

    def attend(j):
        src = bufs[j % 2]
        v, high = kv_tile(v_ref, j)
        vts = _place_rows(jnp.transpose(v), high)
        for ql in range(2):
            qt = 2 * j + ql
            for qh in range(2):
                cols = slice(ql * SEQ + qh * half, ql * SEQ + (qh + 1) * half)
                acc = None
                for s, vv in enumerate(vts):
                    o = _sink_attend_t([src[s * SEQ:(s + 1) * SEQ, cols]], [vv], sink2[:, 2 * qt + s:2 * qt + s + 1])
                    acc = o if acc is None else acc + o
                o_ref[qh * half:(qh + 1) * half, qt * LANES:(qt + 1) * LANES] = jnp.transpose(acc).astype(o_ref.dtype)

    _attn_schedule(ATT_KV, scores, attend)


def _ctx_attn(p0, sink):
    def cols(width, start):
        return pl.BlockSpec((SEQ, width), lambda b: (b, start // width))

    sbuf = pltpu.VMEM((2 * SEQ, 2 * SEQ), F32)
    return pl.pallas_call(
        _ctx_attn_kernel,
        grid=(BATCH,),
        in_specs=[cols(D, P0_Q), cols(ATT_KV_DIM, P0_K), cols(ATT_KV_DIM, P0_V),
                  pl.BlockSpec((1, LANES), lambda b: (0, 0))],
        out_specs=pl.BlockSpec((SEQ, D), lambda b: (b, 0)),
        out_shape=jax.ShapeDtypeStruct((T_CTX, D), BF16),
        scratch_shapes=[sbuf, sbuf],
        compiler_params=_cparams(1),
        name="ctx_attn",
    )(p0, p0, p0, sink)


def _rope_tables():
    quarter = ATT_HD // 4
    t = np.arange(DEC_SEQ)
    lane = np.arange(LANES)
    inv = ROPE_BASE ** (-(lane % quarter).astype(np.float64) / quarter)
    pos = np.where(((lane % ATT_HD) < ATT_HD // 2)[None, :], (t // GRID_W)[:, None], (t % GRID_W)[:, None])
    ang = pos * inv[None, :]
    first = ((lane % (2 * quarter)) < quarter)[None, :]
    cos, sin = np.cos(ang), np.sin(ang)
    return (jnp.asarray(cos, F32), jnp.asarray(np.where(first, -sin, 0.0), F32),
            jnp.asarray(np.where(first, 0.0, sin), F32))


def _rope(x, cos, sa, sb):
    quarter = ATT_HD // 4
    return x * cos + pltpu.roll(x, LANES - quarter, 1) * sa + pltpu.roll(x, quarter, 1) * sb


def _lat_attn_kernel(q_ref, kp_ref, kc_ref, kn_ref, vp_ref, vc_ref, vn_ref, ck_ref, cv_ref,
                     cos_ref, sa_ref, sb_ref, sink_ref, o_ref, c_a, c_b, w_a, w_b):
    blk = pl.program_id(1)
    nb = pl.num_programs(1)
    B = ATT_BLOCK
    sink2 = sink_ref[...] * LOG2E
    cbufs, wbufs = (c_a, c_b), (w_a, w_b)

    def tables(b):
        r0 = pl.multiple_of(b * B, B)
        return cos_ref[pl.ds(r0, B), :], sa_ref[pl.ds(r0, B), :], sb_ref[pl.ds(r0, B), :]

    tq = tables(blk)
    tk = [tables(jnp.maximum(blk - 1, 0)), tq, tables(jnp.minimum(blk + 1, nb - 1))]
    kabs = (blk - 1) * B + lax.broadcasted_iota(jnp.int32, (3 * B, B), 0)
    qpos = blk * B + lax.broadcasted_iota(jnp.int32, (3 * B, B), 1)
    ok = (jnp.abs(qpos - kabs) <= WINDOW) & (kabs >= 0) & (kabs < nb * B)
    ok = jnp.concatenate([ok, ok], axis=1)
    ok = jnp.concatenate([ok, ok], axis=0)

    def scores(j):
        high = (j % 2 == 1)
        sl = slice((j // 2) * LANES, (j // 2 + 1) * LANES)
        kw = jnp.concatenate([_rope(r[:, sl].astype(F32), *tb) for r, tb in zip((kp_ref, kc_ref, kn_ref), tk)], axis=0)
        qs = [q_ref[:, qt * LANES:(qt + 1) * LANES].astype(F32) for qt in (2 * j, 2 * j + 1)]
        q_plain = jnp.concatenate(qs, axis=0)
        q_rope = jnp.concatenate([_rope(q, *tq) for q in qs], axis=0)
        cbufs[j % 2][...] = _mm_nt(jnp.concatenate(_place_halves(ck_ref[:, sl], high), axis=0), q_plain) * SCORE_SCALE
        win = _mm_nt(jnp.concatenate(_place_halves(kw, high), axis=0), q_rope) * SCORE_SCALE
        wbufs[j % 2][...] = jnp.where(ok, win, -jnp.inf)

    def attend(j):
        high = (j % 2 == 1)
        sl = slice((j // 2) * LANES, (j // 2 + 1) * LANES)
        vw = jnp.concatenate([jnp.transpose(r[:, sl].astype(F32)) for r in (vp_ref, vc_ref, vn_ref)], axis=1)
        vts = _place_rows(vw, high)
        cvts = _place_rows(jnp.transpose(cv_ref[:, sl]), high)
        csrc, wsrc = cbufs[j % 2], wbufs[j % 2]
        for ql in range(2):
            qt = 2 * j + ql
            cols = slice(ql * B, (ql + 1) * B)
            acc = None
            for s in range(2):
                parts = [csrc[s * PAST_LEN:(s + 1) * PAST_LEN, cols], wsrc[s * 3 * B:(s + 1) * 3 * B, cols]]
                o = _sink_attend_t(parts, [cvts[s], vts[s]], sink2[:, 2 * qt + s:2 * qt + s + 1])
                acc = o if acc is None else acc + o
            o_ref[:, qt * LANES:(qt + 1) * LANES] = jnp.transpose(acc).astype(o_ref.dtype)

    _attn_schedule(ATT_KV, scores, attend)


def _lat_attn(p0, ck, cv, sink):
    nb = DEC_SEQ // ATT_BLOCK
    base = T_CTX // ATT_BLOCK

    def kv(start, shift):
        return pl.BlockSpec((ATT_BLOCK, ATT_KV_DIM),
                            lambda b, i: (base + b * nb + jnp.clip(i + shift, 0, nb - 1), start // ATT_KV_DIM))

    def full(shape):
        return pl.BlockSpec(shape, lambda b, i: (0,) * len(shape))

    cache = pl.BlockSpec((None, PAST_LEN, ATT_KV_DIM), lambda b, i: (b, 0, 0))
    cos, sa, sb = _rope_tables()
    return pl.pallas_call(
        _lat_attn_kernel,
        grid=(DEC_BATCH, nb),
        in_specs=[pl.BlockSpec((ATT_BLOCK, D), lambda b, i: (base + b * nb + i, P0_Q // D)),
                  kv(P0_K, -1), kv(P0_K, 0), kv(P0_K, 1), kv(P0_V, -1), kv(P0_V, 0), kv(P0_V, 1),
                  cache, cache, full((DEC_SEQ, LANES)), full((DEC_SEQ, LANES)), full((DEC_SEQ, LANES)),
                  full((1, LANES))],
        out_specs=pl.BlockSpec((ATT_BLOCK, D), lambda b, i: (b * nb + i, 0)),
        out_shape=jax.ShapeDtypeStruct((T_LAT, D), BF16),
        scratch_shapes=[pltpu.VMEM((2 * PAST_LEN, 2 * ATT_BLOCK), F32)] * 2
        + [pltpu.VMEM((2 * 3 * ATT_BLOCK, 2 * ATT_BLOCK), F32)] * 2,
        compiler_params=_cparams(2),
        name="lat_attn",
    )(p0, p0, p0, p0, p0, p0, p0, ck, cv, cos, sa, sb, sink)


def _log_sigmoid(x):
    return jnp.minimum(x, 0.0) - jnp.log(1.0 + jnp.exp(-jnp.abs(x)))


def _gla_kernel(*refs, seq, has_s0):
    if has_s0:
        (q_ref, k_ref, v_ref, r_ref, lr_ref, w2f_ref, w2b_ref, bf_ref, bb_ref, nw_ref, s0f_ref, s0b_ref,
         y_ref, sf_ref, sb_ref, gf, gb, yf, yb, stf, stb) = refs
    else:
        (q_ref, k_ref, v_ref, r_ref, lr_ref, w2f_ref, w2b_ref, bf_ref, bb_ref, nw_ref,
         y_ref, sf_ref, sb_ref, gf, gb, yf, yb, stf, stb) = refs
    C = GLA_C
    nc = seq // C
    lr = lr_ref[...]
    gf[...] = _log_sigmoid(_mm(lr, w2f_ref[...]) + bf_ref[...]) / GLA_GATE_NORM
    gb[...] = _log_sigmoid(_mm(lr, w2b_ref[...]) + bb_ref[...]) / GLA_GATE_NORM
    for h in range(GLA_HEADS):
        rows = slice(h * GLA_DV, (h + 1) * GLA_DV)
        if has_s0:
            stf[rows, :] = jnp.transpose(s0f_ref[h * GLA_DK:(h + 1) * GLA_DK, :])
            stb[rows, :] = jnp.transpose(s0b_ref[h * GLA_DK:(h + 1) * GLA_DK, :])
        else:
            stf[rows, :] = jnp.zeros((GLA_DV, GLA_DK), F32)
            stb[rows, :] = jnp.zeros((GLA_DV, GLA_DK), F32)

    row = lax.broadcasted_iota(jnp.int32, (C, C), 0)
    col = lax.broadcasted_iota(jnp.int32, (C, C), 1)
    qscale = GLA_DK ** -0.5

    def chunk(c, fwd):
        g_scr, y_scr, st = (gf, yf, stf) if fwd else (gb, yb, stb)
        r0 = pl.multiple_of(c * C, C)
        g = g_scr[pl.ds(r0, C), :]
        cs = _cumsum_rows(g, C)
        total = cs[C - 1:C, :]
        q = q_ref[pl.ds(r0, C), :].astype(F32) * qscale
        k = k_ref[pl.ds(r0, C), :].astype(F32)
        v = v_ref[pl.ds(r0, C), :]
        if fwd:
            qs, ks, ke = q * jnp.exp(cs), k * jnp.exp(-cs), k * jnp.exp(total - cs)
            keep = col <= row
        else:
            ex = cs - g
            qs, ks, ke = q * jnp.exp(total - ex), k * jnp.exp(ex - total), k * jnp.exp(ex)
            keep = col >= row
        dec = jnp.exp(total)
        for h in range(GLA_HEADS):
            kc = slice(h * GLA_DK, (h + 1) * GLA_DK)
            vc = slice(h * GLA_DV, (h + 1) * GLA_DV)
            s_t = st[vc, :]
            att = jnp.where(keep, _mm_nt(qs[:, kc], ks[:, kc]), 0.0)
            y_scr[pl.ds(r0, C), vc] = _mm(att, v[:, vc]) + _mm_nt(qs[:, kc], s_t)
            st[vc, :] = dec[:, kc] * s_t + _mm_tn(v[:, vc], ke[:, kc])

    def body(i, carry):
        chunk(i, True)
        chunk(nc - 1 - i, False)
        return carry

    lax.fori_loop(0, nc, body, 0)

    nw = nw_ref[...]
    for blk in range(seq // LANES):
        rs = slice(blk * LANES, (blk + 1) * LANES)
        y = yf[rs, :] + yb[rs, :]
        gate = _silu(r_ref[rs, :].astype(F32))
        for h in range(GLA_HEADS):
            vc = slice(h * GLA_DV, (h + 1) * GLA_DV)
            y_ref[rs, vc] = (_rms(y[:, vc], nw) * gate[:, vc]).astype(y_ref.dtype)
    for h in range(GLA_HEADS):
        rows = slice(h * GLA_DV, (h + 1) * GLA_DV)
        sf_ref[h * GLA_DK:(h + 1) * GLA_DK, :] = jnp.transpose(stf[rows, :])
        sb_ref[h * GLA_DK:(h + 1) * GLA_DK, :] = jnp.transpose(stb[rows, :])


def _gla(p1, p1lr, tok0, nseq, seq, w2f, w2b, bgf, bgb, nw, s0f=None, s0b=None):
    has_s0 = s0f is not None
    b0 = tok0 // seq
    dk_all, dv_all = GLA_HEADS * GLA_DK, GLA_HEADS * GLA_DV

    def cols(width, start):
        return pl.BlockSpec((seq, width), lambda s: (b0 + s, start // width))

    def full(shape):
        return pl.BlockSpec(shape, lambda s: (0,) * len(shape))

    in_specs = [cols(dk_all, P1_Q), cols(dk_all, P1_K), cols(dv_all, P1_V), cols(dv_all, P1_R), cols(LANES, 0),
                full((LANES, dk_all)), full((LANES, dk_all)), full((1, dk_all)), full((1, dk_all)), full((1, GLA_DV))]
    args = [p1, p1, p1, p1, p1lr, w2f, w2b, bgf.reshape(1, -1), bgb.reshape(1, -1), nw.reshape(1, -1)]
    st_spec = pl.BlockSpec((None, dk_all, GLA_DV), lambda s: (s, 0, 0))
    if has_s0:
        in_specs += [st_spec, st_spec]
        args += [s0f, s0b]
    st_shape = jax.ShapeDtypeStruct((nseq, dk_all, GLA_DV), F32)
    return pl.pallas_call(
        functools.partial(_gla_kernel, seq=seq, has_s0=has_s0),
        grid=(nseq,), in_specs=in_specs,
        out_specs=[pl.BlockSpec((seq, dv_all), lambda s: (s, 0)), st_spec, st_spec],
        out_shape=[jax.ShapeDtypeStruct((nseq * seq, dv_all), BF16), st_shape, st_shape],
        scratch_shapes=[pltpu.VMEM((seq, dk_all), F32), pltpu.VMEM((seq, dk_all), F32),
                        pltpu.VMEM((seq, dv_all), F32), pltpu.VMEM((seq, dv_all), F32),
                        pltpu.VMEM((dv_all, GLA_DK), F32), pltpu.VMEM((dv_all, GLA_DK), F32)],
        compiler_params=_cparams(1),
        name=f"gla{seq}",
    )(*args)


ROUTE_TM = 512
ROUTE_SUB = 256
ROUTE_ROWS = 32


def _split_bf16(x):
    hi = x.astype(BF16)
    return hi, (x - hi.astype(F32)).astype(BF16)


def _outproj_kernel(*refs, n_in, dual_x):
    y_refs = refs[:2 * n_in]
    n_x = 2 if dual_x else 1
    x_refs = refs[2 * n_in + 1:2 * n_in + 1 + n_x]
    w_ref = refs[2 * n_in]
    (g1_ref, sh_ref, sc_ref, nw_ref, wr_ref, br_ref,
     xo_ref, tr_ref, route_ref, cnt_ref, w_scr, wr_hl, carry) = refs[2 * n_in + 1 + n_x:]
    is_ctx = pl.program_id(0) < T_CTX // ROUTE_TM

    @pl.when(pl.program_id(0) == 0)
    def _():
        w_scr[...] = w_ref[...].astype(BF16)
        hi, lo = _split_bf16(jnp.transpose(wr_ref[...]))
        wr_hl[0:LANES, :] = hi
        wr_hl[LANES:2 * LANES, :] = lo
        carry[...] = jnp.zeros(carry.shape, F32)

    for sub in range(ROUTE_TM // ROUTE_SUB):
        _outproj_subtile(sub, is_ctx, y_refs, x_refs, dual_x, n_in, g1_ref, sh_ref, sc_ref, nw_ref, br_ref,
                         xo_ref, tr_ref, route_ref, cnt_ref, w_scr, wr_hl, carry)


def _outproj_subtile(sub, is_ctx, y_refs, x_refs, dual_x, n_in, g1_ref, sh_ref, sc_ref, nw_ref, br_ref,
                     xo_ref, tr_ref, route_ref, cnt_ref, w_scr, wr_hl, carry):
    tm = ROUTE_SUB
    rows = slice(sub * tm, (sub + 1) * tm)
    o = None
    for i in range(n_in):
        y = jnp.where(is_ctx, y_refs[2 * i][rows, :], y_refs[2 * i + 1][rows, :])
        d = jnp.dot(y, w_scr[i * D:(i + 1) * D, :], preferred_element_type=F32)
        o = d if o is None else o + d
    x_in = jnp.where(is_ctx, x_refs[0][rows, :], x_refs[1][rows, :]) if dual_x else x_refs[0][rows, :]
    x = x_in + g1_ref[0] * o
    xo_ref[rows, :] = x
    t = _rms(x, nw_ref[...]) * (1.0 + sc_ref[0]) + sh_ref[0]
    for k in range(D // LANES):
        tr_ref[pl.ds(sub * tm * SUBLANES + k, tm, stride=SUBLANES), :] = t[:, k * LANES:(k + 1) * LANES]

    t_hi, t_lo = _split_bf16(t)
    lg = _mm_nt(wr_hl[...], t_hi)
    nr = ROUTE_ROWS
    logit = lg[0:nr, :] + lg[LANES:LANES + nr, :] + _mm_nt(wr_hl[0:LANES, :], t_lo)[0:nr, :] + br_ref[0:nr, :]
    rowf = lax.broadcasted_iota(jnp.int32, (nr, tm), 0).astype(F32)
    neg = -jnp.inf

    def first_argmax(v, vmax):
        return jnp.min(jnp.where(v == vmax, rowf, float(LANES)), axis=0, keepdims=True)

    gl = jnp.where(rowf < N_GROUPS, logit, neg)
    gmax = jnp.max(gl, axis=0, keepdims=True)
    gsel = first_argmax(gl, gmax)
    gprob = 1.0 / jnp.sum(jnp.exp(gl - gmax), axis=0, keepdims=True)
    first = N_GROUPS + EXP_PER_GROUP * gsel
    el = jnp.where((rowf >= first) & (rowf < first + EXP_PER_GROUP), logit, neg)
    m1 = jnp.max(el, axis=0, keepdims=True)
    i1 = first_argmax(el, m1)
    el2 = jnp.where(rowf == i1, neg, el)
    m2 = jnp.max(el2, axis=0, keepdims=True)
    i2 = first_argmax(el2, m2)
    e2 = jnp.exp(m2 - m1)
    c1 = gprob / (1.0 + e2)
    c2 = gprob * e2 / (1.0 + e2)
    x1 = i1 - N_GROUPS
    x2 = i2 - N_GROUPS

    erow = rowf
    hot = ((erow == x1) | (erow == x2)).astype(F32)
    tri = (lax.broadcasted_iota(jnp.int32, (tm, tm), 0) < lax.broadcasted_iota(jnp.int32, (tm, tm), 1))
    before = _mm(hot, tri.astype(F32)) + carry[...]
    r1 = jnp.sum(jnp.where(erow == x1, before, 0.0), axis=0, keepdims=True)
    r2 = jnp.sum(jnp.where(erow == x2, before, 0.0), axis=0, keepdims=True)
    total = carry[...] + _mm(hot, jnp.ones((tm, tm), F32))
    carry[...] = total
    cnt_ref[...] = total[0:N_EXPERTS, 0:LANES]
    row8 = lax.broadcasted_iota(jnp.int32, (SUBLANES, tm), 0)
    out = jnp.zeros((SUBLANES, tm), F32)
    for k, v in enumerate((x1, x2, c1, c2, r1, r2)):
        out = jnp.where(row8 == k, jnp.broadcast_to(v, (SUBLANES, tm)), out)
    route_ref[:, rows] = out


def _outproj_route(ys, w_out, xs, mods, layer, norm_w, w_router, b_router):
    tm = ROUTE_TM
    n_in = len(ys) // 2
    dual_x = len(xs) == 2
    kdim = w_out.shape[0]

    def full(shape):
        return pl.BlockSpec(shape, lambda i: (0,) * len(shape))

    tile = pl.BlockSpec((tm, D), lambda i: (i, 0))
    pair = list(_ctx_lat_specs(tm))
    in_specs = (pair * n_in + [full((kdim, D))] + (pair if dual_x else [tile])
                + [_mod_spec(layer, 2, tm), _mod_spec(layer, 3, tm), _mod_spec(layer, 4, tm),
                   full((1, D)), full((D, LANES)), full((LANES, ROUTE_SUB))])
    cnt = jax.ShapeDtypeStruct((N_EXPERTS, LANES), F32)
    return pl.pallas_call(
        functools.partial(_outproj_kernel, n_in=n_in, dual_x=dual_x),
        grid=(T // tm,), in_specs=in_specs,
        out_specs=[tile, pl.BlockSpec((tm * SUBLANES, LANES), lambda i: (i, 0)),
                   pl.BlockSpec((SUBLANES, tm), lambda i: (0, i)), full(cnt.shape)],
        out_shape=[jax.ShapeDtypeStruct((T, D), F32), jax.ShapeDtypeStruct((T * SUBLANES, LANES), F32),
                   jax.ShapeDtypeStruct((SUBLANES, T), F32), cnt],
        scratch_shapes=[pltpu.VMEM((kdim, D), BF16), pltpu.VMEM((2 * LANES, D), BF16),
                        pltpu.VMEM((ROUTE_ROWS, ROUTE_SUB), F32)],
        compiler_params=_cparams(1),
        name=f"outproj{layer}",
    )(*ys, w_out, *xs, mods, mods, mods, norm_w.reshape(1, D), w_router,
      jnp.broadcast_to(b_router.reshape(LANES, 1), (LANES, ROUTE_SUB)))


def _moe_meta(counts):
    tm = MOE_TM
    experts = jnp.arange(N_EXPERTS, dtype=jnp.int32)
    counts = jnp.max(counts, axis=1).astype(jnp.int32)
    padded = ((counts + tm - 1) // tm) * tm
    ends = jnp.cumsum(padded)
    tile_start = jnp.arange(MOE_TILES, dtype=jnp.int32) * tm
    te = jnp.sum((tile_start[:, None] >= ends[None, :]).astype(jnp.int32), axis=1)
    last = jnp.max(jnp.where(counts > 0, experts, 0))
    meta = jnp.concatenate([jnp.minimum(te, last), ends[-1:] // tm]).astype(jnp.int32)
    starts = ends - padded
    pads = jnp.concatenate([starts + counts, ends[-1:], ends, jnp.full((1,), MOE_ROWS)]).astype(jnp.int32)
    return starts.astype(jnp.int32), pads, meta


def _expert_changed(meta_ref, j):
    return (j == 0) | (meta_ref[j] != meta_ref[jnp.maximum(j - 1, 0)])


def _moe_up_kernel(pos1_ref, pos2_ref, pads_ref, meta_ref, tr_hbm, wg_ref, wu_ref, a_ref, rowmap_ref,
                   tr_scr, g0, g1, wg_bf, wu_bf, sem):
    j = pl.program_id(0)
    tm = MOE_TM
    ntiles = meta_ref[MOE_TILES]

    def gather(tile, dst):
        for mi in range(tm):
            tok = jnp.minimum(rowmap_ref[tile * tm + mi] >> 1, T - 1)
            dst[mi * SUBLANES:(mi + 1) * SUBLANES, :] = tr_scr[pl.ds(pl.multiple_of(tok * SUBLANES, SUBLANES), SUBLANES), :]

    @pl.when(j == 0)
    def _():
        load = pltpu.make_async_copy(tr_hbm, tr_scr, sem)
        load.start()

        def clear(c, carry):
            for i in range(SUBLANES):
                rowmap_ref[c * SUBLANES + i] = 2 * T
            return carry
        for k in range(N_EXPERTS + 1):
            lax.fori_loop(pads_ref[k] // SUBLANES, pads_ref[N_EXPERTS + 1 + k] // SUBLANES, clear, 0)

        def place(t, carry):
            rowmap_ref[pos1_ref[t]] = 2 * t
            rowmap_ref[pos2_ref[t]] = 2 * t + 1
            return carry
        lax.fori_loop(0, T, place, 0, unroll=8)
        load.wait()
        gather(0, g0)

    def compute(cur, nxt):
        gather(jnp.minimum(j + 1, ntiles - 1), nxt)
        x = _tokmajor_to_std(cur, tm).astype(BF16)
        g = jnp.dot(x, wg_bf[...], preferred_element_type=F32)
        u = jnp.dot(x, wu_bf[...], preferred_element_type=F32)
        a_ref[...] = (_silu(g) * u).astype(a_ref.dtype)

    @pl.when(j < ntiles)
    def _():
        @pl.when(_expert_changed(meta_ref, j))
        def _():
            wg_bf[...] = wg_ref[...].astype(BF16)
            wu_bf[...] = wu_ref[...].astype(BF16)

        pl.when(j % 2 == 0)(functools.partial(compute, g0, g1))
        pl.when(j % 2 == 1)(functools.partial(compute, g1, g0))

    @pl.when(j >= ntiles)
    def _():
        a_ref[...] = jnp.zeros(a_ref.shape, a_ref.dtype)


def _moe_down_kernel(rowmap_ref, cpair_ref, meta_ref, a_ref, wd_ref, out_hbm, acc, y0, y1, wd_bf):
    j = pl.program_id(0)
    tm = MOE_TM
    zrows = 512
    ntiles = meta_ref[MOE_TILES]

    @pl.when(j == 0)
    def _():
        def zero(i, carry):
            acc[pl.ds(pl.multiple_of(i * zrows, zrows), zrows), :] = jnp.zeros((zrows, LANES), F32)
            return carry
        lax.fori_loop(0, ACC_ROWS // zrows, zero, 0)

    def matmul(dst):
        y = jnp.dot(a_ref[...], wd_bf[...], preferred_element_type=F32)
        for k in range(D // LANES):
            dst[pl.ds(k, tm, stride=SUBLANES), :] = y[:, k * LANES:(k + 1) * LANES]

    def scatter(tile, src):
        for b in range(tm // SUBLANES):
            ents = [rowmap_ref[tile * tm + b * SUBLANES + i] for i in range(SUBLANES)]
            offs = [pl.multiple_of((e >> 1) * SUBLANES, SUBLANES) for e in ents]
            olds = [acc[pl.ds(o, SUBLANES), :] for o in offs]
            for i, o in enumerate(offs):
                r = (b * SUBLANES + i) * SUBLANES
                acc[pl.ds(o, SUBLANES), :] = olds[i] + cpair_ref[ents[i]] * src[r:r + SUBLANES, :]

    has_mm = j < ntiles
    has_sc = (j >= 1) & (j <= ntiles)

    @pl.when(has_mm)
    def _():
        @pl.when(_expert_changed(meta_ref, j))
        def _():
            wd_bf[...] = wd_ref[...].astype(BF16)

    for par, (cur, prev) in enumerate(((y0, y1), (y1, y0))):
        mine = (j % 2) == par

        @pl.when(mine & has_mm & has_sc)
        def _():
            matmul(cur)
            scatter(j - 1, prev)

        @pl.when(mine & has_mm & jnp.logical_not(has_sc))
        def _():
            matmul(cur)

        @pl.when(mine & jnp.logical_not(has_mm) & has_sc)
        def _():
            scatter(j - 1, prev)

    @pl.when(j == pl.num_programs(0) - 1)
    def _():
        pltpu.sync_copy(acc.at[0:T * SUBLANES, :], out_hbm)


def _moe(tr, route_t, counts, layer, w_gate, w_up, w_down):
    tm = MOE_TM
    starts, pads, meta = _moe_meta(counts)
    experts = jnp.arange(N_EXPERTS, dtype=jnp.int32)

    def position(e, r):
        sel = e.astype(jnp.int32)[:, None] == experts[None, :]
        return jnp.sum(jnp.where(sel, starts[None, :], 0), axis=1) + r.astype(jnp.int32)

    pos1 = position(route_t[0], route_t[4])
    pos2 = position(route_t[1], route_t[5])
    cpair = jnp.concatenate([jnp.stack([route_t[2], route_t[3]], axis=1).reshape(2 * T), jnp.zeros((2,), F32)])

    def wspec(shape, n):
        return pl.BlockSpec((None, None) + shape,
                            lambda j, *pre: (layer, pre[n - 1][jnp.minimum(j, MOE_TILES - 1)], 0, 0))

    gscr = pltpu.VMEM((tm * SUBLANES, LANES), F32)
    act, rowmap = pl.pallas_call(
        _moe_up_kernel,
        grid_spec=pltpu.PrefetchScalarGridSpec(
            num_scalar_prefetch=4, grid=(MOE_TILES,),
            in_specs=[pl.BlockSpec(memory_space=pl.ANY), wspec((D, D_EXPERT), 4), wspec((D, D_EXPERT), 4)],
            out_specs=[pl.BlockSpec((tm, D_EXPERT), lambda j, *pre: (j, 0)), pl.BlockSpec(memory_space=pltpu.SMEM)],
            scratch_shapes=[pltpu.VMEM((T * SUBLANES, LANES), F32), gscr, gscr,
                            pltpu.VMEM((D, D_EXPERT), BF16), pltpu.VMEM((D, D_EXPERT), BF16),
                            pltpu.SemaphoreType.DMA(())]),
        out_shape=[jax.ShapeDtypeStruct((MOE_ROWS, D_EXPERT), BF16), jax.ShapeDtypeStruct((MOE_ROWS,), jnp.int32)],
        compiler_params=_cparams(1),
        name=f"moe_up{layer}",
    )(pos1, pos2, pads, meta, tr, w_gate, w_up)

    return pl.pallas_call(
        _moe_down_kernel,
        grid_spec=pltpu.PrefetchScalarGridSpec(
            num_scalar_prefetch=3, grid=(MOE_TILES + 1,),
            in_specs=[pl.BlockSpec((tm, D_EXPERT), lambda j, *pre: (jnp.minimum(j, MOE_TILES - 1), 0)),
                      wspec((D_EXPERT, D), 3)],
            out_specs=pl.BlockSpec(memory_space=pl.ANY),
            scratch_shapes=[pltpu.VMEM((ACC_ROWS, LANES), F32), gscr, gscr, pltpu.VMEM((D_EXPERT, D), BF16)]),
        out_shape=jax.ShapeDtypeStruct((T * SUBLANES, LANES), F32),
        compiler_params=_cparams(1),
        name=f"moe_down{layer}",
    )(rowmap, cpair, meta, act, w_down)


FINAL_TM = 512


def _final_kernel(x_ref, moe_ref, g2_ref, nw_ref, oc_ref, ol_ref):
    x = x_ref[...] + g2_ref[0] * _tokmajor_to_std(moe_ref, FINAL_TM)
    y = _rms(x, nw_ref[...])
    is_ctx = pl.program_id(0) < T_CTX // FINAL_TM

    @pl.when(is_ctx)
    def _():
        oc_ref[...] = y

    @pl.when(jnp.logical_not(is_ctx))
    def _():
        ol_ref[...] = y


def _final(x, moe, mods, layer, norm_w):
    tm = FINAL_TM
    tile = pl.BlockSpec((tm, D), lambda i: (i, 0))
    return pl.pallas_call(
        _final_kernel,
        grid=(T // tm,),
        in_specs=[tile, pl.BlockSpec((tm * SUBLANES, LANES), lambda i: (i, 0)), _mod_spec(layer, 5, tm),
                  pl.BlockSpec((1, D), lambda i: (0, 0))],
        out_specs=list(_ctx_lat_specs(tm)),
        out_shape=[jax.ShapeDtypeStruct((T_CTX, D), F32), jax.ShapeDtypeStruct((T_LAT, D), F32)],
        compiler_params=_cparams(1),
        name="final_norm",
    )(x, moe, mods, norm_w.reshape(1, D))


def _pad_lanes(v):
    return jnp.pad(v.astype(F32), (0, LANES - v.shape[0])).reshape(1, LANES)


def kernel(x_prompt, x_sample, cache_k_attn, cache_v_attn, state_ssd_fwd, state_ssd_bwd, state_gla_fwd, state_gla_bwd, c, c_ctx, w_ada, b_ada, norm_mix_w, norm_ffn_w, w_in_even, conv_w, conv_b, dt_bias_fwd, dt_bias_bwd, a_log_fwd, a_log_bwd, d_skip, ssd_norm_w, attn_sink, w_out_even, w_in_odd, w_gk2_fwd, b_gk_fwd, w_gk2_bwd, b_gk_bwd, gla_norm_w, w_out_odd, w_router_group, b_router_group, w_router_expert, b_router_expert, w_gate_exp, w_up_exp, w_down_exp, final_norm_w):
    depth = w_ada.shape[0]
    assert depth == 2 and x_prompt.shape == (BATCH, SEQ, D) and x_sample.shape == (DEC_BATCH, DEC_SEQ, D)

    cond8 = jnp.concatenate([c_ctx[None, :], c, jnp.zeros((SUBLANES - 1 - DEC_BATCH, D), F32)], axis=0)
    mods = _adaln(cond8, w_ada, b_ada).reshape(depth * SUBLANES * 6, 1, D)
    xs0 = (x_prompt.reshape(T_CTX, D), x_sample.reshape(T_LAT, D))

    def router_params(i):
        wr = jnp.concatenate([w_router_group[i], w_router_expert[i],
                              jnp.zeros((D, LANES - N_GROUPS - N_EXPERTS), F32)], axis=1)
        return wr, _pad_lanes(jnp.concatenate([b_router_group[i], b_router_expert[i]]))

    n_zxbc = 2 * D_SSD + 2 * SSD_GROUPS * SSD_N
    n_dt = 2 * SSD_HEADS
    bc_tile, q_tiles = n_zxbc // PROJ_TN - 1, D // PROJ_TN
    rows0 = [k * PROJ_TN for k in range(bc_tile + 1)] + [n_zxbc + n_dt + k * PROJ_TN for k in range(q_tiles + 1)]

    def out_block0(j):
        return jnp.where(j == bc_tile, bc_tile + q_tiles, jnp.where((j > bc_tile) & (j <= bc_tile + q_tiles), j - 1, j))

    p0, p0dt = _modproj(xs0, mods, 0, norm_mix_w[0], jnp.transpose(w_in_even[0]), rows0, n_zxbc, out_block0)

    dtb = _pad_lanes(jnp.concatenate([dt_bias_fwd[0], dt_bias_bwd[0]]))
    alog = _pad_lanes(jnp.concatenate([a_log_fwd[0], a_log_bwd[0]]))
    dskip = jnp.repeat(d_skip[0], SSD_P).reshape(1, D_SSD)
    ssd_args = (conv_w[0], conv_b[0], dtb, alog, dskip, ssd_norm_w[0])
    y_ssd_c, ssd_f, ssd_b = _ssd(p0, p0dt, 0, BATCH, SEQ, *ssd_args)
    y_ssd_l, _, _ = _ssd(p0, p0dt, T_CTX, DEC_BATCH, DEC_SEQ, *ssd_args,
                         h0f=state_ssd_fwd[:, 0].reshape(DEC_BATCH, D_SSD, SSD_N),
                         h0b=state_ssd_bwd[:, 0].reshape(DEC_BATCH, D_SSD, SSD_N))
    sink = _pad_lanes(attn_sink[0])
    y_att_c = _ctx_attn(p0, sink)
    y_att_l = _lat_attn(p0, cache_k_attn[:, 0].reshape(DEC_BATCH, PAST_LEN, ATT_KV_DIM),
                        cache_v_attn[:, 0].reshape(DEC_BATCH, PAST_LEN, ATT_KV_DIM), sink)
    xmid0, tr0, route0, cnt0 = _outproj_route([y_ssd_c, y_ssd_l, y_att_c, y_att_l], w_out_even[0], xs0, mods, 0,
                                              norm_ffn_w[0], *router_params(0))
    moe0 = _moe(tr0, route0, cnt0, 0, w_gate_exp, w_up_exp, w_down_exp)

    dk_all = GLA_HEADS * GLA_DK
    n_qkvr = 2 * dk_all + 2 * GLA_HEADS * GLA_DV
    n_odd = w_in_odd.shape[2]
    p1, p1lr, x1 = _modproj(xmid0, mods, 1, norm_mix_w[1], jnp.transpose(w_in_odd[0]),
                            [k * PROJ_TN for k in range(n_qkvr // PROJ_TN)], n_odd - LANES, lambda j: j, moe=moe0)
    lr0 = LANES - 2 * GLA_LOWRANK
    w2f = jnp.zeros((LANES, dk_all), F32).at[lr0:lr0 + GLA_LOWRANK].set(w_gk2_fwd[0])
    w2b = jnp.zeros((LANES, dk_all), F32).at[lr0 + GLA_LOWRANK:].set(w_gk2_bwd[0])
    gla_args = (w2f, w2b, b_gk_fwd[0], b_gk_bwd[0], gla_norm_w[0])
    y_gla_c, gla_f, gla_b = _gla(p1, p1lr, 0, BATCH, SEQ, *gla_args)
    y_gla_l, _, _ = _gla(p1, p1lr, T_CTX, DEC_BATCH, DEC_SEQ, *gla_args,
                         s0f=state_gla_fwd[:, 0].reshape(DEC_BATCH, dk_all, GLA_DV),
                         s0b=state_gla_bwd[:, 0].reshape(DEC_BATCH, dk_all, GLA_DV))
    xmid1, tr1, route1, cnt1 = _outproj_route([y_gla_c, y_gla_l], w_out_odd[0], (x1,), mods, 1,
                                              norm_ffn_w[1], *router_params(1))
    moe1 = _moe(tr1, route1, cnt1, 1, w_gate_exp, w_up_exp, w_down_exp)
    y_c, y_l = _final(xmid1, moe1, mods, 1, final_norm_w)

    y_prompt = y_c.reshape(BATCH, SEQ, D)
    y_sample = y_l.reshape(DEC_BATCH, DEC_SEQ, D)
    new_k = p0[:T_CTX, P0_K:P0_K + ATT_KV_DIM].astype(F32).reshape(BATCH, 1, SEQ, ATT_KV, ATT_HD)
    new_v = p0[:T_CTX, P0_V:P0_V + ATT_KV_DIM].astype(F32).reshape(BATCH, 1, SEQ, ATT_KV, ATT_HD)
    return (y_prompt, y_sample, new_k, new_v,
            ssd_f.reshape(BATCH, 1, SSD_HEADS, SSD_P, SSD_N), ssd_b.reshape(BATCH, 1, SSD_HEADS, SSD_P, SSD_N),
            gla_f.reshape(BATCH, 1, GLA_HEADS, GLA_DK, GLA_DV), gla_b.reshape(BATCH, 1, GLA_HEADS, GLA_DK, GLA_DV))
```

```python
import functools
import math

import numpy as np
import jax
import jax.numpy as jnp
from jax import lax
from jax.experimental import pallas as pl
from jax.experimental.pallas import tpu as pltpu

F32 = jnp.float32
BF16 = jnp.bfloat16

D = 1024
BATCH, SEQ = 16, 256
DEC_BATCH, DEC_SEQ = 2, 1024
PAST_LEN = 512
GRID_W = 64
EPS = 1e-6
T_CTX = BATCH * SEQ
T_LAT = DEC_BATCH * DEC_SEQ
T = T_CTX + T_LAT

SSD_HEADS, SSD_P, SSD_N, SSD_GROUPS = 16, 64, 128, 2
SSD_CONV = 5
SSD_L = 128
D_SSD = SSD_HEADS * SSD_P
HEADS_PER_GROUP = SSD_HEADS // SSD_GROUPS
GROUP_W = HEADS_PER_GROUP * SSD_P

ATT_HEADS, ATT_KV, ATT_HD = 16, 4, 64
ATT_KV_DIM = ATT_KV * ATT_HD
WINDOW = 128
ATT_BLOCK = 128
ATT_SCALE = ATT_HD ** -0.5
ROPE_BASE = 10000.0

GLA_HEADS, GLA_DK, GLA_DV = 4, 128, 256
GLA_C = 64
GLA_GATE_NORM = 16.0
GLA_LOWRANK = 16

N_GROUPS, EXP_PER_GROUP = 4, 4
N_EXPERTS = 16
D_EXPERT = 512

LANES = 128
SUBLANES = 8
VMEM_LIMIT = 56 * 1024 * 1024

P0_Z, P0_X, P0_Q, P0_BC, P0_K, P0_V = 0, 1024, 2048, 3072, 3584, 3840
P0_W = 4096
P1_Q, P1_K, P1_V, P1_R = 0, 512, 1024, 2048
P1_W = 3072

MOE_TM = 256
MOE_TILES = (2 * T) // MOE_TM + N_EXPERTS
MOE_ROWS = MOE_TILES * MOE_TM
ACC_ROWS = (T + SUBLANES) * SUBLANES


def _cparams(n_axes, vmem=VMEM_LIMIT):
    return pltpu.CompilerParams(dimension_semantics=("arbitrary",) * n_axes, vmem_limit_bytes=vmem)


def _silu(x):
    return x / (1.0 + jnp.exp(-x))


def _softplus(x):
    return jnp.maximum(x, 0.0) + jnp.log(1.0 + jnp.exp(-jnp.abs(x)))


def _mm(a, b):
    return jnp.dot(a.astype(BF16), b.astype(BF16), preferred_element_type=F32)


def _mm_nt(a, b):
    return lax.dot_general(a.astype(BF16), b.astype(BF16), (((1,), (1,)), ((), ())),
                           preferred_element_type=F32)


def _mm_tn(a, b):
    return lax.dot_general(a.astype(BF16), b.astype(BF16), (((0,), (0,)), ((), ())),
                           preferred_element_type=F32)


def _rms(x, w):
    return x * lax.rsqrt(jnp.mean(x * x, axis=-1, keepdims=True) + EPS) * w


def _cumsum_rows(x, n):
    row = lax.broadcasted_iota(jnp.int32, x.shape, 0)
    s = 1
    while s < n:
        x = x + jnp.where(row >= s, pltpu.roll(x, s, 0), 0.0)
        s *= 2
    return x


def _mod_row(tok0):
    return jnp.where(tok0 < T_CTX, 0, 1 + (tok0 - T_CTX) // DEC_SEQ)


ADA_TN = 1536


def _adaln_kernel(c_ref, w_ref, b_ref, o_ref):
    s = _silu(c_ref[...])
    o_ref[0] = _mm(s, w_ref[0]) + b_ref[0]


def _adaln(cond8, w_ada, b_ada):
    depth = w_ada.shape[0]
    return pl.pallas_call(
        _adaln_kernel,
        grid=(depth, 6 * D // ADA_TN),
        in_specs=[
            pl.BlockSpec((SUBLANES, D), lambda l, j: (0, 0)),
            pl.BlockSpec((1, D, ADA_TN), lambda l, j: (l, 0, j)),
            pl.BlockSpec((1, 1, ADA_TN), lambda l, j: (l, 0, j)),
        ],
        out_specs=pl.BlockSpec((1, SUBLANES, ADA_TN), lambda l, j: (l, 0, j)),
        out_shape=jax.ShapeDtypeStruct((depth, SUBLANES, 6 * D), F32),
        compiler_params=_cparams(2),
        name="adaln",
    )(cond8, w_ada, b_ada.reshape(depth, 1, 6 * D))


def _mod_spec(layer, chunk, tm, tile_of=lambda i, *_: i):
    return pl.BlockSpec((1, 1, D), lambda *g: ((layer * SUBLANES + _mod_row(tile_of(*g) * tm)) * 6 + chunk, 0, 0))


def _tokmajor_to_std(ref, tm):
    return jnp.concatenate([ref[pl.ds(k, tm, stride=SUBLANES), :] for k in range(D // LANES)], axis=1)


PROJ_TM = 1024
PROJ_TN = 512


def _ctx_lat_specs(tm, width=D):
    n_ctx = T_CTX // tm
    return (pl.BlockSpec((tm, width), lambda i, *_: (jnp.minimum(i, n_ctx - 1), 0)),
            pl.BlockSpec((tm, width), lambda i, *_: (jnp.maximum(i - n_ctx, 0), 0)))


def _modproj_kernel(*refs, dual_x):
    it = iter(refs)
    if dual_x:
        xc_ref, xl_ref = next(it), next(it)
    else:
        x_ref, moe_ref, g2_ref = next(it), next(it), next(it)
    sh_ref, sc_ref, nw_ref, w_ref, ws_ref, o_ref, os_ref = (next(it) for _ in range(7))
    xo_ref = None if dual_x else next(it)
    h_all, w_bf = next(it), next(it)
    j, i = pl.program_id(0), pl.program_id(1)
    tm = PROJ_TM
    rows = pl.ds(pl.multiple_of(i * tm, tm), tm)

    @pl.when(j == 0)
    def _():
        if dual_x:
            x = jnp.where(i < T_CTX // tm, xc_ref[...], xl_ref[...])
        else:
            x = x_ref[...] + g2_ref[0] * _tokmajor_to_std(moe_ref, tm)
            xo_ref[...] = x
        h = (_rms(x, nw_ref[...]) * (1.0 + sc_ref[0]) + sh_ref[0]).astype(BF16)
        h_all[rows, :] = h
        os_ref[...] = _mm_nt(h, ws_ref[...])

    @pl.when(i == 0)
    def _():
        w_bf[...] = w_ref[...].astype(BF16)

    o_ref[...] = _mm_nt(h_all[rows, :], w_bf[...]).astype(o_ref.dtype)


def _modproj(xs, mods, layer, norm_w, wt, tile_rows, small_row, out_block, moe=None):
    tm, tn = PROJ_TM, PROJ_TN
    dual_x = moe is None
    n_tiles = len(tile_rows)
    n_i, n_ctx = T // tm, T_CTX // tm

    def w_row(j, i):
        r = jnp.int32(tile_rows[0])
        for k in range(1, n_tiles):
            r = jnp.where(j == k, tile_rows[k], r)
        return pl.multiple_of(r, SUBLANES), 0

    def tok(j, i):
        return jnp.where(j == 0, i, n_i - 1)

    tile = pl.BlockSpec((tm, D), lambda j, i: (tok(j, i), 0))
    if dual_x:
        in_specs = [pl.BlockSpec((tm, D), lambda j, i: (jnp.minimum(tok(j, i), n_ctx - 1), 0)),
                    pl.BlockSpec((tm, D), lambda j, i: (jnp.maximum(tok(j, i) - n_ctx, 0), 0))]
        args = list(xs)
    else:
        in_specs = [tile, pl.BlockSpec((tm * SUBLANES, LANES), lambda j, i: (tok(j, i), 0)),
                    _mod_spec(layer - 1, 5, tm, tok)]
        args = [xs, moe, mods]
    in_specs += [_mod_spec(layer, 0, tm, tok), _mod_spec(layer, 1, tm, tok), pl.BlockSpec((1, D), lambda j, i: (0, 0)),
                 pl.BlockSpec((pl.Element(tn), pl.Element(D)), w_row),
                 pl.BlockSpec((pl.Element(LANES), pl.Element(D)), lambda j, i: (small_row, 0))]
    args += [mods, mods, norm_w.reshape(1, D), wt, wt]
    out_specs = [pl.BlockSpec((tm, tn), lambda j, i: (i, out_block(j))),
                 pl.BlockSpec((tm, LANES), lambda j, i: (tok(j, i), 0))]
    out_shape = [jax.ShapeDtypeStruct((T, n_tiles * tn), BF16), jax.ShapeDtypeStruct((T, LANES), F32)]
    if not dual_x:
        out_specs.append(tile)
        out_shape.append(jax.ShapeDtypeStruct((T, D), F32))
    return pl.pallas_call(
        functools.partial(_modproj_kernel, dual_x=dual_x),
        grid=(n_tiles, n_i), in_specs=in_specs, out_specs=out_specs, out_shape=out_shape,
        scratch_shapes=[pltpu.VMEM((T, D), BF16), pltpu.VMEM((tn, D), BF16)],
        compiler_params=_cparams(2),
        name=f"modproj{layer}",
    )(*args)


def _expand_heads(v, off):
    lo = lax.broadcasted_iota(jnp.int32, (v.shape[0], LANES), 1) < SSD_P
    tiles = []
    for q in range(SSD_HEADS // 2):
        a = jnp.broadcast_to(v[:, off + 2 * q:off + 2 * q + 1], (v.shape[0], LANES))
        b = jnp.broadcast_to(v[:, off + 2 * q + 1:off + 2 * q + 2], (v.shape[0], LANES))
        tiles.append(jnp.where(lo, a, b))
    return jnp.concatenate(tiles, axis=1)


def _ssd_kernel(*refs, seq, has_h0):
    if has_h0:
        (z_ref, x_ref, bc_ref, dt_ref, cwx_ref, cwbc_ref, cbx_ref, cbbc_ref, dtb_ref, alog_ref, dsk_ref,
         nw_ref, h0f_ref, h0b_ref, y_ref, sf_ref, sb_ref,
         xpad, bcpad, xc, bcc, a_scr, dt_scr, yacc, hf, hb) = refs
    else:
        (z_ref, x_ref, bc_ref, dt_ref, cwx_ref, cwbc_ref, cbx_ref, cbbc_ref, dtb_ref, alog_ref, dsk_ref,
         nw_ref, y_ref, sf_ref, sb_ref,
         xpad, bcpad, xc, bcc, a_scr, dt_scr, yacc, hf, hb) = refs
    L = SSD_L
    nc = seq // L
    pad = SUBLANES
    half = SSD_CONV // 2

    for buf, src, cw, cb, dst in ((xpad, x_ref, cwx_ref, cbx_ref, xc), (bcpad, bc_ref, cwbc_ref, cbbc_ref, bcc)):
        width = buf.shape[1]
        buf[0:pad, :] = jnp.zeros((pad, width), F32)
        buf[pad + seq:2 * pad + seq, :] = jnp.zeros((pad, width), F32)
        buf[pad:pad + seq, :] = src[...].astype(F32)
        for blk in range(nc):
            acc = jnp.broadcast_to(cb[...], (L, width))
            for j in range(SSD_CONV):
                r0 = pad - half + j + blk * L
                acc = acc + cw[j:j + 1, :] * buf[r0:r0 + L, :]
            dst[blk * L:(blk + 1) * L, :] = _silu(acc)

    lane = lax.broadcasted_iota(jnp.int32, (seq, LANES), 1)
    dts = jnp.where(lane < 2 * SSD_HEADS, _softplus(dt_ref[...] + dtb_ref[...]), 0.0)
    dt_scr[...] = dts
    a_scr[...] = dts * (-jnp.exp(alog_ref[...]))

    if has_h0:
        hf[...] = h0f_ref[...]
        hb[...] = h0b_ref[...]
    else:
        hf[...] = jnp.zeros(hf.shape, F32)
        hb[...] = jnp.zeros(hb.shape, F32)

    row = lax.broadcasted_iota(jnp.int32, (L, L), 0)
    col = lax.broadcasted_iota(jnp.int32, (L, L), 1)
    lane_l = lax.broadcasted_iota(jnp.int32, (L, LANES), 1)
    lo_half = lane_l < SSD_P

    def chunk(c, fwd, h_scr):
        off = 0 if fwd else SSD_HEADS
        r0 = pl.multiple_of(c * L, L)
        a = a_scr[pl.ds(r0, L), :]
        dt = dt_scr[pl.ds(r0, L), :]
        cs = _cumsum_rows(a, L)
        total = cs[L - 1:L, :]
        if fwd:
            u = cs
            rvec = jnp.exp(cs)
            ed = jnp.exp(total - cs) * dt
            keep = col <= row
        else:
            ex = cs - a
            u = -ex
            rvec = jnp.exp(total - ex)
            ed = jnp.exp(ex) * dt
            keep = col >= row
        ut = jnp.transpose(u)
        dtt = jnp.transpose(dt)
        tcol = jnp.transpose(jnp.broadcast_to(total, (L, LANES)))[:, 0:1]
        rexp = _expand_heads(rvec, off)
        edexp = _expand_heads(ed, off)
        x = xc[pl.ds(r0, L), :]
        bc = bcc[pl.ds(r0, L), :]
        outs = []
        for g in range(SSD_GROUPS):
            bg = bc[:, g * SSD_N:(g + 1) * SSD_N]
            cg = bc[:, SSD_GROUPS * SSD_N + g * SSD_N:SSD_GROUPS * SSD_N + (g + 1) * SSD_N]
            cbm = _mm_nt(cg, bg)
            hg = h_scr[g * GROUP_W:(g + 1) * GROUP_W, :]
            xg = x[:, g * GROUP_W:(g + 1) * GROUP_W]
            y_off = _mm_nt(cg, hg) * rexp[:, g * GROUP_W:(g + 1) * GROUP_W]
            tiles = []
            for p in range(HEADS_PER_GROUP // 2):
                xt = xg[:, p * LANES:(p + 1) * LANES]
                acc = None
                for s in range(2):
                    h = off + g * HEADS_PER_GROUP + 2 * p + s
                    seg = u[:, h:h + 1] - ut[h:h + 1, :]
                    m = cbm * jnp.exp(jnp.where(keep, seg, -jnp.inf)) * dtt[h:h + 1, :]
                    xm = jnp.where(lo_half if s == 0 else jnp.logical_not(lo_half), xt, 0.0)
                    d = _mm(m, xm)
                    acc = d if acc is None else acc + d
                tiles.append(acc)
            outs.append(y_off + jnp.concatenate(tiles, axis=1))
            decs = []
            for hh in range(HEADS_PER_GROUP):
                h = off + g * HEADS_PER_GROUP + hh
                decs.append(jnp.broadcast_to(jnp.exp(tcol[h:h + 1, :]), (SSD_P, SSD_N)))
            dec = jnp.concatenate(decs, axis=0)
            h_scr[g * GROUP_W:(g + 1) * GROUP_W, :] = dec * hg + _mm_tn(xg * edexp[:, g * GROUP_W:(g + 1) * GROUP_W], bg)
        return r0, x, jnp.concatenate(outs, axis=1)

    def fwd_body(c, carry):
        r0, _, y = chunk(c, True, hf)
        yacc[pl.ds(r0, L), :] = y
        return carry

    lax.fori_loop(0, nc, fwd_body, 0)

    def bwd_body(i, carry):
        c = nc - 1 - i
        r0, x, y = chunk(c, False, hb)
        y = yacc[pl.ds(r0, L), :] + y + dsk_ref[...] * x
        y = y * _silu(z_ref[pl.ds(r0, L), :].astype(F32))
        y_ref[pl.ds(r0, L), :] = _rms(y, nw_ref[...]).astype(y_ref.dtype)
        return carry

    lax.fori_loop(0, nc, bwd_body, 0)
    sf_ref[...] = hf[...]
    sb_ref[...] = hb[...]


def _ssd(p0, p0dt, tok0, nseq, seq, cw, cb, dtb, alog, dskip, nw, h0f=None, h0b=None):
    has_h0 = h0f is not None
    b0 = tok0 // seq

    def cols(width, start):
        return pl.BlockSpec((seq, width), lambda s: (b0 + s, start // width))

    def full(shape):
        return pl.BlockSpec(shape, lambda s: (0,) * len(shape))

    in_specs = [cols(D_SSD, P0_Z), cols(D_SSD, P0_X), cols(512, P0_BC), cols(LANES, 0),
                full((SSD_CONV, D_SSD)), full((SSD_CONV, 512)), full((1, D_SSD)), full((1, 512)),
                full((1, LANES)), full((1, LANES)), full((1, D_SSD)), full((1, D_SSD))]
    args = [p0, p0, p0, p0dt, cw[:, :D_SSD], cw[:, D_SSD:], cb[:D_SSD].reshape(1, -1), cb[D_SSD:].reshape(1, -1),
            dtb, alog, dskip, nw.reshape(1, -1)]
    st_spec = pl.BlockSpec((None, D_SSD, SSD_N), lambda s: (s, 0, 0))
    if has_h0:
        in_specs += [st_spec, st_spec]
        args += [h0f, h0b]
    st_shape = jax.ShapeDtypeStruct((nseq, D_SSD, SSD_N), F32)
    return pl.pallas_call(
        functools.partial(_ssd_kernel, seq=seq, has_h0=has_h0),
        grid=(nseq,), in_specs=in_specs,
        out_specs=[pl.BlockSpec((seq, D_SSD), lambda s: (s, 0)), st_spec, st_spec],
        out_shape=[jax.ShapeDtypeStruct((nseq * seq, D_SSD), BF16), st_shape, st_shape],
        scratch_shapes=[pltpu.VMEM((seq + 2 * SUBLANES, D_SSD), F32), pltpu.VMEM((seq + 2 * SUBLANES, 512), F32),
                        pltpu.VMEM((seq, D_SSD), F32), pltpu.VMEM((seq, 512), F32),
                        pltpu.VMEM((seq, LANES), F32), pltpu.VMEM((seq, LANES), F32),
                        pltpu.VMEM((seq, D_SSD), F32),
                        pltpu.VMEM((D_SSD, SSD_N), F32), pltpu.VMEM((D_SSD, SSD_N), F32)],
        compiler_params=_cparams(1),
        name=f"ssd{seq}",
    )(*args)


def _place_halves(tile, kv_in_high):
    lo = lax.broadcasted_iota(jnp.int32, tile.shape, 1) < ATT_HD
    swapped = pltpu.roll(tile, ATT_HD, 1)
    if kv_in_high:
        return jnp.where(lo, swapped, 0.0), jnp.where(lo, 0.0, tile)
    return jnp.where(lo, tile, 0.0), jnp.where(lo, 0.0, swapped)


def _place_rows(vt, kv_in_high):
    head = vt[ATT_HD:, :] if kv_in_high else vt[:ATT_HD, :]
    z = jnp.zeros_like(head)
    return jnp.concatenate([head, z], axis=0), jnp.concatenate([z, head], axis=0)


LOG2E = 1.4426950408889634
SCORE_SCALE = ATT_SCALE * LOG2E


def _sink_attend_t(score_parts, value_parts, sink2):
    m = sink2
    for s in score_parts:
        m = jnp.maximum(m, jnp.max(s, axis=0, keepdims=True))
    denom = jnp.exp2(sink2 - m)
    out = None
    for s, v in zip(score_parts, value_parts):
        p = jnp.exp2(s - m)
        denom = denom + jnp.sum(p, axis=0, keepdims=True)
        o = _mm(v, p)
        out = o if out is None else out + o
    return out * (1.0 / denom)


def _attn_schedule(n, scores, attend):
    scores(0)
    for j in range(n):
        if j + 1 < n:
            scores(j + 1)
        attend(j)


def _ctx_attn_kernel(q_ref, k_ref, v_ref, sink_ref, o_ref, kt_ref, vt_ref, s_a, s_b):
    sink2 = sink_ref[...] * LOG2E
    bufs = (s_a, s_b)
    half = SEQ // 2
    for t in range(ATT_KV_DIM // LANES):
        cols = slice(t * LANES, (t + 1) * LANES)
        kt_ref[cols, :] = jnp.transpose(k_ref[:, cols].astype(F32))
        vt_ref[cols, :] = jnp.transpose(v_ref[:, cols].astype(F32))

    def kv_tile(ref, j):
        return ref[:, (j // 2) * LANES:(j // 2 + 1) * LANES].astype(F32), (j % 2 == 1)

    def scores(j):
        k_lo, k_hi = _place_halves(*kv_tile(k_ref, j))
        qst = jnp.concatenate([q_ref[:, qt * LANES:(qt + 1) * LANES] for qt in (2 * j, 2 * j + 1)], axis=0)
        bufs[j % 2][...] = _mm_nt(jnp.concatenate([k_lo, k_hi], axis=0), qst) * SCORE_SCALE

    def attend(j):
        src = bufs[j % 2]
        v, high = kv_tile(v_ref, j)
        vts = _place_rows(jnp.transpose(v), high)
        for ql in range(2):
            qt = 2 * j + ql
            for qh in range(2):
                cols = slice(ql * SEQ + qh * half, ql * SEQ + (qh + 1) * half)
                acc = None
                for s, vv in enumerate(vts):
                    o = _sink_attend_t([src[s * SEQ:(s + 1) * SEQ, cols]], [vv], sink2[:, 2 * qt + s:2 * qt + s + 1])
                    acc = o if acc is None else acc + o
                o_ref[qh * half:(qh + 1) * half, qt * LANES:(qt + 1) * LANES] = jnp.transpose(acc).astype(o_ref.dtype)

    _attn_schedule(ATT_KV, scores, attend)


def _ctx_attn(p0, sink):
    def cols(width, start):
        return pl.BlockSpec((SEQ, width), lambda b: (b, start // width))

    sbuf = pltpu.VMEM((2 * SEQ, 2 * SEQ), F32)
    return pl.pallas_call(
        _ctx_attn_kernel,
        grid=(BATCH,),
        in_specs=[cols(D, P0_Q), cols(ATT_KV_DIM, P0_K), cols(ATT_KV_DIM, P0_V),
                  pl.BlockSpec((1, LANES), lambda b: (0, 0))],
        out_specs=[pl.BlockSpec((SEQ, D), lambda b: (b, 0))]
        + [pl.BlockSpec((None, ATT_KV_DIM, SEQ), lambda b: (b, 0, 0))] * 2,
        out_shape=[jax.ShapeDtypeStruct((T_CTX, D), BF16)] + [jax.ShapeDtypeStruct((BATCH, ATT_KV_DIM, SEQ), F32)] * 2,
        scratch_shapes=[sbuf, sbuf],
        compiler_params=_cparams(1),
        name="ctx_attn",
    )(p0, p0, p0, sink)


def _rope_tables():
    quarter = ATT_HD // 4
    t = np.arange(DEC_SEQ)
    lane = np.arange(LANES)
    inv = ROPE_BASE ** (-(lane % quarter).astype(np.float64) / quarter)
    pos = np.where(((lane % ATT_HD) < ATT_HD // 2)[None, :], (t // GRID_W)[:, None], (t % GRID_W)[:, None])
    ang = pos * inv[None, :]
    first = ((lane % (2 * quarter)) < quarter)[None, :]
    cos, sin = np.cos(ang), np.sin(ang)
    return (jnp.asarray(cos, F32), jnp.asarray(np.where(first, -sin, 0.0), F32),
            jnp.asarray(np.where(first, 0.0, sin), F32))


def _rope(x, cos, sa, sb):
    quarter = ATT_HD // 4
    return x * cos + pltpu.roll(x, LANES - quarter, 1) * sa + pltpu.roll(x, quarter, 1) * sb


def _lat_attn_kernel(q_ref, kp_ref, kc_ref, kn_ref, vp_ref, vc_ref, vn_ref, ck_ref, cv_ref,
                     cos_ref, sa_ref, sb_ref, sink_ref, o_ref, c_a, c_b, w_a, w_b):
    blk = pl.program_id(1)
    nb = pl.num_programs(1)
    B = ATT_BLOCK
    sink2 = sink_ref[...] * LOG2E
    cbufs, wbufs = (c_a, c_b), (w_a, w_b)

    def tables(b):
        r0 = pl.multiple_of(b * B, B)
        return cos_ref[pl.ds(r0, B), :], sa_ref[pl.ds(r0, B), :], sb_ref[pl.ds(r0, B), :]

    tq = tables(blk)
    tk = [tables(jnp.maximum(blk - 1, 0)), tq, tables(jnp.minimum(blk + 1, nb - 1))]
    kabs = (blk - 1) * B + lax.broadcasted_iota(jnp.int32, (3 * B, B), 0)
    qpos = blk * B + lax.broadcasted_iota(jnp.int32, (3 * B, B), 1)
    ok = (jnp.abs(qpos - kabs) <= WINDOW) & (kabs >= 0) & (kabs < nb * B)
    ok = jnp.concatenate([ok, ok], axis=1)
    ok = jnp.concatenate([ok, ok], axis=0)

    def scores(j):
        high = (j % 2 == 1)
        sl = slice((j // 2) * LANES, (j // 2 + 1) * LANES)
        kw = jnp.concatenate([_rope(r[:, sl].astype(F32), *tb) for r, tb in zip((kp_ref, kc_ref, kn_ref), tk)], axis=0)
        qs = [q_ref[:, qt * LANES:(qt + 1) * LANES].astype(F32) for qt in (2 * j, 2 * j + 1)]
        q_plain = jnp.concatenate(qs, axis=0)
        q_rope = jnp.concatenate([_rope(q, *tq) for q in qs], axis=0)
        ck = jnp.transpose(ck_ref[sl, :])
        cbufs[j % 2][...] = _mm_nt(jnp.concatenate(_place_halves(ck, high), axis=0), q_plain) * SCORE_SCALE
        win = _mm_nt(jnp.concatenate(_place_halves(kw, high), axis=0), q_rope) * SCORE_SCALE
        wbufs[j % 2][...] = jnp.where(ok, win, -jnp.inf)

    def attend(j):
        high = (j % 2 == 1)
        sl = slice((j // 2) * LANES, (j // 2 + 1) * LANES)
        vw = jnp.concatenate([jnp.transpose(r[:, sl].astype(F32)) for r in (vp_ref, vc_ref, vn_ref)], axis=1)
        vts = _place_rows(vw, high)
        cvts = _place_rows(cv_ref[sl, :], high)
        csrc, wsrc = cbufs[j % 2], wbufs[j % 2]
        for ql in range(2):
            qt = 2 * j + ql
            cols = slice(ql * B, (ql + 1) * B)
            acc = None
            for s in range(2):
                parts = [csrc[s * PAST_LEN:(s + 1) * PAST_LEN, cols], wsrc[s * 3 * B:(s + 1) * 3 * B, cols]]
                o = _sink_attend_t(parts, [cvts[s], vts[s]], sink2[:, 2 * qt + s:2 * qt + s + 1])
                acc = o if acc is None else acc + o
            o_ref[:, qt * LANES:(qt + 1) * LANES] = jnp.transpose(acc).astype(o_ref.dtype)

    _attn_schedule(ATT_KV, scores, attend)


def _lat_attn(p0, ck, cv, sink):
    nb = DEC_SEQ // ATT_BLOCK
    base = T_CTX // ATT_BLOCK

    def kv(start, shift):
        return pl.BlockSpec((ATT_BLOCK, ATT_KV_DIM),
                            lambda b, i: (base + b * nb + jnp.clip(i + shift, 0, nb - 1), start // ATT_KV_DIM))

    def full(shape):
        return pl.BlockSpec(shape, lambda b, i: (0,) * len(shape))

    cache = pl.BlockSpec((None, ATT_KV_DIM, PAST_LEN), lambda b, i: (b, 0, 0))
    cos, sa, sb = _rope_tables()
    return pl.pallas_call(
        _lat_attn_kernel,
        grid=(DEC_BATCH, nb),
        in_specs=[pl.BlockSpec((ATT_BLOCK, D), lambda b, i: (base + b * nb + i, P0_Q // D)),
                  kv(P0_K, -1), kv(P0_K, 0), kv(P0_K, 1), kv(P0_V, -1), kv(P0_V, 0), kv(P0_V, 1),
                  cache, cache, full((DEC_SEQ, LANES)), full((DEC_SEQ, LANES)), full((DEC_SEQ, LANES)),
                  full((1, LANES))],
        out_specs=pl.BlockSpec((ATT_BLOCK, D), lambda b, i: (b * nb + i, 0)),
        out_shape=jax.ShapeDtypeStruct((T_LAT, D), BF16),
        scratch_shapes=[pltpu.VMEM((2 * PAST_LEN, 2 * ATT_BLOCK), F32)] * 2
        + [pltpu.VMEM((2 * 3 * ATT_BLOCK, 2 * ATT_BLOCK), F32)] * 2,
        compiler_params=_cparams(2),
        name="lat_attn",
    )(p0, p0, p0, p0, p0, p0, p0, ck, cv, cos, sa, sb, sink)


def _log_sigmoid(x):
    return jnp.minimum(x, 0.0) - jnp.log(1.0 + jnp.exp(-jnp.abs(x)))


def _gla_kernel(*refs, seq, has_s0):
    if has_s0:
        (q_ref, k_ref, v_ref, r_ref, lr_ref, w2f_ref, w2b_ref, bf_ref, bb_ref, nw_ref, s0f_ref, s0b_ref,
         y_ref, sf_ref, sb_ref, gf, gb, yf, yb, stf, stb) = refs
    else:
        (q_ref, k_ref, v_ref, r_ref, lr_ref, w2f_ref, w2b_ref, bf_ref, bb_ref, nw_ref,
         y_ref, sf_ref, sb_ref, gf, gb, yf, yb, stf, stb) = refs
    C = GLA_C
    nc = seq // C
    lr = lr_ref[...]
    gf[...] = _log_sigmoid(_mm(lr, w2f_ref[...]) + bf_ref[...]) / GLA_GATE_NORM
    gb[...] = _log_sigmoid(_mm(lr, w2b_ref[...]) + bb_ref[...]) / GLA_GATE_NORM
    for h in range(GLA_HEADS):
        rows = slice(h * GLA_DV, (h + 1) * GLA_DV)
        if has_s0:
            stf[rows, :] = jnp.transpose(s0f_ref[h * GLA_DK:(h + 1) * GLA_DK, :])
            stb[rows, :] = jnp.transpose(s0b_ref[h * GLA_DK:(h + 1) * GLA_DK, :])
        else:
            stf[rows, :] = jnp.zeros((GLA_DV, GLA_DK), F32)
            stb[rows, :] = jnp.zeros((GLA_DV, GLA_DK), F32)

    row = lax.broadcasted_iota(jnp.int32, (C, C), 0)
    col = lax.broadcasted_iota(jnp.int32, (C, C), 1)
    qscale = GLA_DK ** -0.5

    def chunk(c, fwd):
        g_scr, y_scr, st = (gf, yf, stf) if fwd else (gb, yb, stb)
        r0 = pl.multiple_of(c * C, C)
        g = g_scr[pl.ds(r0, C), :]
        cs = _cumsum_rows(g, C)
        total = cs[C - 1:C, :]
        q = q_ref[pl.ds(r0, C), :].astype(F32) * qscale
        k = k_ref[pl.ds(r0, C), :].astype(F32)
        v = v_ref[pl.ds(r0, C), :]
        if fwd:
            qs, ks, ke = q * jnp.exp(cs), k * jnp.exp(-cs), k * jnp.exp(total - cs)
            keep = col <= row
        else:
            ex = cs - g
            qs, ks, ke = q * jnp.exp(total - ex), k * jnp.exp(ex - total), k * jnp.exp(ex)
            keep = col >= row
        dec = jnp.exp(total)
        for h in range(GLA_HEADS):
            kc = slice(h * GLA_DK, (h + 1) * GLA_DK)
            vc = slice(h * GLA_DV, (h + 1) * GLA_DV)
            s_t = st[vc, :]
            att = jnp.where(keep, _mm_nt(qs[:, kc], ks[:, kc]), 0.0)
            y_scr[pl.ds(r0, C), vc] = _mm(att, v[:, vc]) + _mm_nt(qs[:, kc], s_t)
            st[vc, :] = dec[:, kc] * s_t + _mm_tn(v[:, vc], ke[:, kc])

    def body(i, carry):
        chunk(i, True)
        chunk(nc - 1 - i, False)
        return carry

    lax.fori_loop(0, nc, body, 0)

    nw = nw_ref[...]
    for blk in range(seq // LANES):
        rs = slice(blk * LANES, (blk + 1) * LANES)
        y = yf[rs, :] + yb[rs, :]
        gate = _silu(r_ref[rs, :].astype(F32))
        for h in range(GLA_HEADS):
            vc = slice(h * GLA_DV, (h + 1) * GLA_DV)
            y_ref[rs, vc] = (_rms(y[:, vc], nw) * gate[:, vc]).astype(y_ref.dtype)
    for h in range(GLA_HEADS):
        rows = slice(h * GLA_DV, (h + 1) * GLA_DV)
        sf_ref[h * GLA_DK:(h + 1) * GLA_DK, :] = jnp.transpose(stf[rows, :])
        sb_ref[h * GLA_DK:(h + 1) * GLA_DK, :] = jnp.transpose(stb[rows, :])


def _gla(p1, p1lr, tok0, nseq, seq, w2f, w2b, bgf, bgb, nw, s0f=None, s0b=None):
    has_s0 = s0f is not None
    b0 = tok0 // seq
    dk_all, dv_all = GLA_HEADS * GLA_DK, GLA_HEADS * GLA_DV

    def cols(width, start):
        return pl.BlockSpec((seq, width), lambda s: (b0 + s, start // width))

    def full(shape):
        return pl.BlockSpec(shape, lambda s: (0,) * len(shape))

    in_specs = [cols(dk_all, P1_Q), cols(dk_all, P1_K), cols(dv_all, P1_V), cols(dv_all, P1_R), cols(LANES, 0),
                full((LANES, dk_all)), full((LANES, dk_all)), full((1, dk_all)), full((1, dk_all)), full((1, GLA_DV))]
    args = [p1, p1, p1, p1, p1lr, w2f, w2b, bgf.reshape(1, -1), bgb.reshape(1, -1), nw.reshape(1, -1)]
    st_spec = pl.BlockSpec((None, dk_all, GLA_DV), lambda s: (s, 0, 0))
    if has_s0:
        in_specs += [st_spec, st_spec]
        args += [s0f, s0b]
    st_shape = jax.ShapeDtypeStruct((nseq, dk_all, GLA_DV), F32)
    return pl.pallas_call(
        functools.partial(_gla_kernel, seq=seq, has_s0=has_s0),
        grid=(nseq,), in_specs=in_specs,
        out_specs=[pl.BlockSpec((seq, dv_all), lambda s: (s, 0)), st_spec, st_spec],
        out_shape=[jax.ShapeDtypeStruct((nseq * seq, dv_all), BF16), st_shape, st_shape],
        scratch_shapes=[pltpu.VMEM((seq, dk_all), F32), pltpu.VMEM((seq, dk_all), F32),
                        pltpu.VMEM((seq, dv_all), F32), pltpu.VMEM((seq, dv_all), F32),
                        pltpu.VMEM((dv_all, GLA_DK), F32), pltpu.VMEM((dv_all, GLA_DK), F32)],
        compiler_params=_cparams(1),
        name=f"gla{seq}",
    )(*args)


ROUTE_TM = 512
ROUTE_SUB = 256
ROUTE_ROWS = 32


def _split_bf16(x):
    hi = x.astype(BF16)
    return hi, (x - hi.astype(F32)).astype(BF16)


def _outproj_kernel(*refs, n_in, dual_x):
    y_refs = refs[:2 * n_in]
    n_x = 2 if dual_x else 1
    x_refs = refs[2 * n_in + 1:2 * n_in + 1 + n_x]
    w_ref = refs[2 * n_in]
    (g1_ref, sh_ref, sc_ref, nw_ref, wr_ref, br_ref,
     xo_ref, tr_ref, route_ref, cnt_ref, w_scr, wr_hl, carry) = refs[2 * n_in + 1 + n_x:]
    is_ctx = pl.program_id(0) < T_CTX // ROUTE_TM

    @pl.when(pl.program_id(0) == 0)
    def _():
        w_scr[...] = w_ref[...].astype(BF16)
        hi, lo = _split_bf16(jnp.transpose(wr_ref[...]))
        wr_hl[0:LANES, :] = hi
        wr_hl[LANES:2 * LANES, :] = lo
        carry[...] = jnp.zeros(carry.shape, F32)

    for sub in range(ROUTE_TM // ROUTE_SUB):
        _outproj_subtile(sub, is_ctx, y_refs, x_refs, dual_x, n_in, g1_ref, sh_ref, sc_ref, nw_ref, br_ref,
                         xo_ref, tr_ref, route_ref, cnt_ref, w_scr, wr_hl, carry)


def _outproj_subtile(sub, is_ctx, y_refs, x_refs, dual_x, n_in, g1_ref, sh_ref, sc_ref, nw_ref, br_ref,
                     xo_ref, tr_ref, route_ref, cnt_ref, w_scr, wr_hl, carry):
    tm = ROUTE_SUB
    rows = slice(sub * tm, (sub + 1) * tm)
    o = None
    for i in range(n_in):
        y = jnp.where(is_ctx, y_refs[2 * i][rows, :], y_refs[2 * i + 1][rows, :])
        d = jnp.dot(y, w_scr[i * D:(i + 1) * D, :], preferred_element_type=F32)
        o = d if o is None else o + d
    x_in = jnp.where(is_ctx, x_refs[0][rows, :], x_refs[1][rows, :]) if dual_x else x_refs[0][rows, :]
    x = x_in + g1_ref[0] * o
    xo_ref[rows, :] = x
    t = _rms(x, nw_ref[...]) * (1.0 + sc_ref[0]) + sh_ref[0]
    for k in range(D // LANES):
        tr_ref[pl.ds(sub * tm * SUBLANES + k, tm, stride=SUBLANES), :] = t[:, k * LANES:(k + 1) * LANES]

    t_hi, t_lo = _split_bf16(t)
    lg = _mm_nt(wr_hl[...], t_hi)
    nr = ROUTE_ROWS
    logit = lg[0:nr, :] + lg[LANES:LANES + nr, :] + _mm_nt(wr_hl[0:LANES, :], t_lo)[0:nr, :] + br_ref[0:nr, :]
    rowf = lax.broadcasted_iota(jnp.int32, (nr, tm), 0).astype(F32)
    neg = -jnp.inf

    def first_argmax(v, vmax):
        return jnp.min(jnp.where(v == vmax, rowf, float(LANES)), axis=0, keepdims=True)

    gl = jnp.where(rowf < N_GROUPS, logit, neg)
    gmax = jnp.max(gl, axis=0, keepdims=True)
    gsel = first_argmax(gl, gmax)
    gprob = 1.0 / jnp.sum(jnp.exp(gl - gmax), axis=0, keepdims=True)
    first = N_GROUPS + EXP_PER_GROUP * gsel
    el = jnp.where((rowf >= first) & (rowf < first + EXP_PER_GROUP), logit, neg)
    m1 = jnp.max(el, axis=0, keepdims=True)
    i1 = first_argmax(el, m1)
    el2 = jnp.where(rowf == i1, neg, el)
    m2 = jnp.max(el2, axis=0, keepdims=True)
    i2 = first_argmax(el2, m2)
    e2 = jnp.exp(m2 - m1)
    c1 = gprob / (1.0 + e2)
    c2 = gprob * e2 / (1.0 + e2)
    x1 = i1 - N_GROUPS
    x2 = i2 - N_GROUPS

    erow = rowf
    hot = ((erow == x1) | (erow == x2)).astype(F32)
    tri = (lax.broadcasted_iota(jnp.int32, (tm, tm), 0) < lax.broadcasted_iota(jnp.int32, (tm, tm), 1))
    before = _mm(hot, tri.astype(F32)) + carry[...]
    r1 = jnp.sum(jnp.where(erow == x1, before, 0.0), axis=0, keepdims=True)
    r2 = jnp.sum(jnp.where(erow == x2, before, 0.0), axis=0, keepdims=True)
    total = carry[...] + _mm(hot, jnp.ones((tm, tm), F32))
    carry[...] = total
    cnt_ref[...] = total[0:N_EXPERTS, 0:LANES]
    row8 = lax.broadcasted_iota(jnp.int32, (SUBLANES, tm), 0)
    out = jnp.zeros((SUBLANES, tm), F32)
    for k, v in enumerate((x1, x2, c1, c2, r1, r2)):
        out = jnp.where(row8 == k, jnp.broadcast_to(v, (SUBLANES, tm)), out)
    route_ref[:, rows] = out


def _outproj_route(ys, w_out, xs, mods, layer, norm_w, w_router, b_router):
    tm = ROUTE_TM
    n_in = len(ys) // 2
    dual_x = len(xs) == 2
    kdim = w_out.shape[0]

    def full(shape):
        return pl.BlockSpec(shape, lambda i: (0,) * len(shape))

    tile = pl.BlockSpec((tm, D), lambda i: (i, 0))
    pair = list(_ctx_lat_specs(tm))
    in_specs = (pair * n_in + [full((kdim, D))] + (pair if dual_x else [tile])
                + [_mod_spec(layer, 2, tm), _mod_spec(layer, 3, tm), _mod_spec(layer, 4, tm),
                   full((1, D)), full((D, LANES)), full((LANES, ROUTE_SUB))])
    cnt = jax.ShapeDtypeStruct((N_EXPERTS, LANES), F32)
    return pl.pallas_call(
        functools.partial(_outproj_kernel, n_in=n_in, dual_x=dual_x),
        grid=(T // tm,), in_specs=in_specs,
        out_specs=[tile, pl.BlockSpec((tm * SUBLANES, LANES), lambda i: (i, 0)),
                   pl.BlockSpec((SUBLANES, tm), lambda i: (0, i)), full(cnt.shape)],
        out_shape=[jax.ShapeDtypeStruct((T, D), F32), jax.ShapeDtypeStruct((T * SUBLANES, LANES), F32),
                   jax.ShapeDtypeStruct((SUBLANES, T), F32), cnt],
        scratch_shapes=[pltpu.VMEM((kdim, D), BF16), pltpu.VMEM((2 * LANES, D), BF16),
                        pltpu.VMEM((ROUTE_ROWS, ROUTE_SUB), F32)],
        compiler_params=_cparams(1),
        name=f"outproj{layer}",
    )(*ys, w_out, *xs, mods, mods, mods, norm_w.reshape(1, D), w_router,
      jnp.broadcast_to(b_router.reshape(LANES, 1), (LANES, ROUTE_SUB)))


def _moe_meta(counts):
    tm = MOE_TM
    experts = jnp.arange(N_EXPERTS, dtype=jnp.int32)
    counts = jnp.max(counts, axis=1).astype(jnp.int32)
    padded = ((counts + tm - 1) // tm) * tm
    ends = jnp.cumsum(padded)
    tile_start = jnp.arange(MOE_TILES, dtype=jnp.int32) * tm
    te = jnp.sum((tile_start[:, None] >= ends[None, :]).astype(jnp.int32), axis=1)
    last = jnp.max(jnp.where(counts > 0, experts, 0))
    meta = jnp.concatenate([jnp.minimum(te, last), ends[-1:] // tm]).astype(jnp.int32)
    starts = ends - padded
    pads = jnp.concatenate([starts + counts, ends[-1:], ends, jnp.full((1,), MOE_ROWS)]).astype(jnp.int32)
    return starts.astype(jnp.int32), pads, meta


def _expert_changed(meta_ref, j):
    return (j == 0) | (meta_ref[j] != meta_ref[jnp.maximum(j - 1, 0)])


def _moe_up_kernel(pos1_ref, pos2_ref, pads_ref, meta_ref, tr_hbm, wg_ref, wu_ref, a_ref, rowmap_ref,
                   tr_scr, g0, g1, wg_bf, wu_bf, sem):
    j = pl.program_id(0)
    tm = MOE_TM
    ntiles = meta_ref[MOE_TILES]

    def gather(tile, dst):
        for mi in range(tm):
            tok = jnp.minimum(rowmap_ref[tile * tm + mi] >> 1, T - 1)
            dst[mi * SUBLANES:(mi + 1) * SUBLANES, :] = tr_scr[pl.ds(pl.multiple_of(tok * SUBLANES, SUBLANES), SUBLANES), :]

    @pl.when(j == 0)
    def _():
        load = pltpu.make_async_copy(tr_hbm, tr_scr, sem)
        load.start()

        def clear(c, carry):
            for i in range(SUBLANES):
                rowmap_ref[c * SUBLANES + i] = 2 * T
            return carry
        for k in range(N_EXPERTS + 1):
            lax.fori_loop(pads_ref[k] // SUBLANES, pads_ref[N_EXPERTS + 1 + k] // SUBLANES, clear, 0)

        def place(t, carry):
            rowmap_ref[pos1_ref[t]] = 2 * t
            rowmap_ref[pos2_ref[t]] = 2 * t + 1
            return carry
        lax.fori_loop(0, T, place, 0, unroll=8)
        load.wait()
        gather(0, g0)

    def compute(cur, nxt):
        gather(jnp.minimum(j + 1, ntiles - 1), nxt)
        x = _tokmajor_to_std(cur, tm).astype(BF16)
        g = jnp.dot(x, wg_bf[...], preferred_element_type=F32)
        u = jnp.dot(x, wu_bf[...], preferred_element_type=F32)
        a_ref[...] = (_silu(g) * u).astype(a_ref.dtype)

    @pl.when(j < ntiles)
    def _():
        @pl.when(_expert_changed(meta_ref, j))
        def _():
            wg_bf[...] = wg_ref[...].astype(BF16)
            wu_bf[...] = wu_ref[...].astype(BF16)

        pl.when(j % 2 == 0)(functools.partial(compute, g0, g1))
        pl.when(j % 2 == 1)(functools.partial(compute, g1, g0))

    @pl.when(j >= ntiles)
    def _():
        a_ref[...] = jnp.zeros(a_ref.shape, a_ref.dtype)


def _moe_down_kernel(rowmap_ref, cpair_ref, meta_ref, a_ref, wd_ref, out_hbm, acc, y0, y1, wd_bf):
    j = pl.program_id(0)
    tm = MOE_TM
    zrows = 512
    ntiles = meta_ref[MOE_TILES]

    @pl.when(j == 0)
    def _():
        def zero(i, carry):
            acc[pl.ds(pl.multiple_of(i * zrows, zrows), zrows), :] = jnp.zeros((zrows, LANES), F32)
            return carry
        lax.fori_loop(0, ACC_ROWS // zrows, zero, 0)

    def matmul(dst):
        y = jnp.dot(a_ref[...], wd_bf[...], preferred_element_type=F32)
        for k in range(D // LANES):
            dst[pl.ds(k, tm, stride=SUBLANES), :] = y[:, k * LANES:(k + 1) * LANES]

    def scatter(tile, src):
        for b in range(tm // SUBLANES):
            ents = [rowmap_ref[tile * tm + b * SUBLANES + i] for i in range(SUBLANES)]
            offs = [pl.multiple_of((e >> 1) * SUBLANES, SUBLANES) for e in ents]
            olds = [acc[pl.ds(o, SUBLANES), :] for o in offs]
            for i, o in enumerate(offs):
                r = (b * SUBLANES + i) * SUBLANES
                acc[pl.ds(o, SUBLANES), :] = olds[i] + cpair_ref[ents[i]] * src[r:r + SUBLANES, :]

    has_mm = j < ntiles
    has_sc = (j >= 1) & (j <= ntiles)

    @pl.when(has_mm)
    def _():
        @pl.when(_expert_changed(meta_ref, j))
        def _():
            wd_bf[...] = wd_ref[...].astype(BF16)

    for par, (cur, prev) in enumerate(((y0, y1), (y1, y0))):
        mine = (j % 2) == par

        @pl.when(mine & has_mm & has_sc)
        def _():
            matmul(cur)
            scatter(j - 1, prev)

        @pl.when(mine & has_mm & jnp.logical_not(has_sc))
        def _():
            matmul(cur)

        @pl.when(mine & jnp.logical_not(has_mm) & has_sc)
        def _():
            scatter(j - 1, prev)

    @pl.when(j == pl.num_programs(0) - 1)
    def _():
        pltpu.sync_copy(acc.at[0:T * SUBLANES, :], out_hbm)


def _moe(tr, route_t, counts, layer, w_gate, w_up, w_down):
    tm = MOE_TM
    starts, pads, meta = _moe_meta(counts)
    experts = jnp.arange(N_EXPERTS, dtype=jnp.int32)

    def position(e, r):
        sel = e.astype(jnp.int32)[:, None] == experts[None, :]
        return jnp.sum(jnp.where(sel, starts[None, :], 0), axis=1) + r.astype(jnp.int32)

    pos1 = position(route_t[0], route_t[4])
    pos2 = position(route_t[1], route_t[5])
    cpair = jnp.concatenate([jnp.stack([route_t[2], route_t[3]], axis=1).reshape(2 * T), jnp.zeros((2,), F32)])

    def wspec(shape, n):
        return pl.BlockSpec((None, None) + shape,
                            lambda j, *pre: (layer, pre[n - 1][jnp.minimum(j, MOE_TILES - 1)], 0, 0))

    gscr = pltpu.VMEM((tm * SUBLANES, LANES), F32)
    act, rowmap = pl.pallas_call(
        _moe_up_kernel,
        grid_spec=pltpu.PrefetchScalarGridSpec(
            num_scalar_prefetch=4, grid=(MOE_TILES,),
            in_specs=[pl.BlockSpec(memory_space=pl.ANY), wspec((D, D_EXPERT), 4), wspec((D, D_EXPERT), 4)],
            out_specs=[pl.BlockSpec((tm, D_EXPERT), lambda j, *pre: (j, 0)), pl.BlockSpec(memory_space=pltpu.SMEM)],
            scratch_shapes=[pltpu.VMEM((T * SUBLANES, LANES), F32), gscr, gscr,
                            pltpu.VMEM((D, D_EXPERT), BF16), pltpu.VMEM((D, D_EXPERT), BF16),
                            pltpu.SemaphoreType.DMA(())]),
        out_shape=[jax.ShapeDtypeStruct((MOE_ROWS, D_EXPERT), BF16), jax.ShapeDtypeStruct((MOE_ROWS,), jnp.int32)],
        compiler_params=_cparams(1),
        name=f"moe_up{layer}",
    )(pos1, pos2, pads, meta, tr, w_gate, w_up)

    return pl.pallas_call(
        _moe_down_kernel,
        grid_spec=pltpu.PrefetchScalarGridSpec(
            num_scalar_prefetch=3, grid=(MOE_TILES + 1,),
            in_specs=[pl.BlockSpec((tm, D_EXPERT), lambda j, *pre: (jnp.minimum(j, MOE_TILES - 1), 0)),
                      wspec((D_EXPERT, D), 3)],
            out_specs=pl.BlockSpec(memory_space=pl.ANY),
            scratch_shapes=[pltpu.VMEM((ACC_ROWS, LANES), F32), gscr, gscr, pltpu.VMEM((D_EXPERT, D), BF16)]),
        out_shape=jax.ShapeDtypeStruct((T * SUBLANES, LANES), F32),
        compiler_params=_cparams(1),
        name=f"moe_down{layer}",
    )(rowmap, cpair, meta, act, w_down)


FINAL_TM = 512


def _final_kernel(x_ref, moe_ref, g2_ref, nw_ref, oc_ref, ol_ref):
    x = x_ref[...] + g2_ref[0] * _tokmajor_to_std(moe_ref, FINAL_TM)
    y = _rms(x, nw_ref[...])
    is_ctx = pl.program_id(0) < T_CTX // FINAL_TM

    @pl.when(is_ctx)
    def _():
        oc_ref[...] = y

    @pl.when(jnp.logical_not(is_ctx))
    def _():
        ol_ref[...] = y


def _final(x, moe, mods, layer, norm_w):
    tm = FINAL_TM
    tile = pl.BlockSpec((tm, D), lambda i: (i, 0))
    return pl.pallas_call(
        _final_kernel,
        grid=(T // tm,),
        in_specs=[tile, pl.BlockSpec((tm * SUBLANES, LANES), lambda i: (i, 0)), _mod_spec(layer, 5, tm),
                  pl.BlockSpec((1, D), lambda i: (0, 0))],
        out_specs=list(_ctx_lat_specs(tm)),
        out_shape=[jax.ShapeDtypeStruct((T_CTX, D), F32), jax.ShapeDtypeStruct((T_LAT, D), F32)],
        compiler_params=_cparams(1),
        name="final_norm",
    )(x, moe, mods, norm_w.reshape(1, D))


def _pad_lanes(v):
    return jnp.pad(v.astype(F32), (0, LANES - v.shape[0])).reshape(1, LANES)


def kernel(x_prompt, x_sample, cache_k_attn, cache_v_attn, state_ssd_fwd, state_ssd_bwd, state_gla_fwd, state_gla_bwd, c, c_ctx, w_ada, b_ada, norm_mix_w, norm_ffn_w, w_in_even, conv_w, conv_b, dt_bias_fwd, dt_bias_bwd, a_log_fwd, a_log_bwd, d_skip, ssd_norm_w, attn_sink, w_out_even, w_in_odd, w_gk2_fwd, b_gk_fwd, w_gk2_bwd, b_gk_bwd, gla_norm_w, w_out_odd, w_router_group, b_router_group, w_router_expert, b_router_expert, w_gate_exp, w_up_exp, w_down_exp, final_norm_w):
    depth = w_ada.shape[0]
    assert depth == 2 and x_prompt.shape == (BATCH, SEQ, D) and x_sample.shape == (DEC_BATCH, DEC_SEQ, D)

    cond8 = jnp.concatenate([c_ctx[None, :], c, jnp.zeros((SUBLANES - 1 - DEC_BATCH, D), F32)], axis=0)
    mods = _adaln(cond8, w_ada, b_ada).reshape(depth * SUBLANES * 6, 1, D)
    xs0 = (x_prompt.reshape(T_CTX, D), x_sample.reshape(T_LAT, D))

    def router_params(i):
        wr = jnp.concatenate([w_router_group[i], w_router_expert[i],
                              jnp.zeros((D, LANES - N_GROUPS - N_EXPERTS), F32)], axis=1)
        return wr, _pad_lanes(jnp.concatenate([b_router_group[i], b_router_expert[i]]))

    n_zxbc = 2 * D_SSD + 2 * SSD_GROUPS * SSD_N
    n_dt = 2 * SSD_HEADS
    bc_tile, q_tiles = n_zxbc // PROJ_TN - 1, D // PROJ_TN
    rows0 = [k * PROJ_TN for k in range(bc_tile + 1)] + [n_zxbc + n_dt + k * PROJ_TN for k in range(q_tiles + 1)]

    def out_block0(j):
        return jnp.where(j == bc_tile, bc_tile + q_tiles, jnp.where((j > bc_tile) & (j <= bc_tile + q_tiles), j - 1, j))

    p0, p0dt = _modproj(xs0, mods, 0, norm_mix_w[0], jnp.transpose(w_in_even[0]), rows0, n_zxbc, out_block0)

    dtb = _pad_lanes(jnp.concatenate([dt_bias_fwd[0], dt_bias_bwd[0]]))
    alog = _pad_lanes(jnp.concatenate([a_log_fwd[0], a_log_bwd[0]]))
    dskip = jnp.repeat(d_skip[0], SSD_P).reshape(1, D_SSD)
    ssd_args = (conv_w[0], conv_b[0], dtb, alog, dskip, ssd_norm_w[0])
    y_ssd_c, ssd_f, ssd_b = _ssd(p0, p0dt, 0, BATCH, SEQ, *ssd_args)
    y_ssd_l, _, _ = _ssd(p0, p0dt, T_CTX, DEC_BATCH, DEC_SEQ, *ssd_args,
                         h0f=state_ssd_fwd[:, 0].reshape(DEC_BATCH, D_SSD, SSD_N),
                         h0b=state_ssd_bwd[:, 0].reshape(DEC_BATCH, D_SSD, SSD_N))
    sink = _pad_lanes(attn_sink[0])
    y_att_c, new_kt, new_vt = _ctx_attn(p0, sink)
    def cache_in(t):
        return jnp.transpose(t[:, 0], (0, 2, 3, 1)).reshape(DEC_BATCH, ATT_KV_DIM, PAST_LEN)

    y_att_l = _lat_attn(p0, cache_in(cache_k_attn), cache_in(cache_v_attn), sink)
    xmid0, tr0, route0, cnt0 = _outproj_route([y_ssd_c, y_ssd_l, y_att_c, y_att_l], w_out_even[0], xs0, mods, 0,
                                              norm_ffn_w[0], *router_params(0))
    moe0 = _moe(tr0, route0, cnt0, 0, w_gate_exp, w_up_exp, w_down_exp)

    dk_all = GLA_HEADS * GLA_DK
    n_qkvr = 2 * dk_all + 2 * GLA_HEADS * GLA_DV
    n_odd = w_in_odd.shape[2]
    p1, p1lr, x1 = _modproj(xmid0, mods, 1, norm_mix_w[1], jnp.transpose(w_in_odd[0]),
                            [k * PROJ_TN for k in range(n_qkvr // PROJ_TN)], n_odd - LANES, lambda j: j, moe=moe0)
    lr0 = LANES - 2 * GLA_LOWRANK
    w2f = jnp.zeros((LANES, dk_all), F32).at[lr0:lr0 + GLA_LOWRANK].set(w_gk2_fwd[0])
    w2b = jnp.zeros((LANES, dk_all), F32).at[lr0 + GLA_LOWRANK:].set(w_gk2_bwd[0])
    gla_args = (w2f, w2b, b_gk_fwd[0], b_gk_bwd[0], gla_norm_w[0])
    y_gla_c, gla_f, gla_b = _gla(p1, p1lr, 0, BATCH, SEQ, *gla_args)
    y_gla_l, _, _ = _gla(p1, p1lr, T_CTX, DEC_BATCH, DEC_SEQ, *gla_args,
                         s0f=state_gla_fwd[:, 0].reshape(DEC_BATCH, dk_all, GLA_DV),
                         s0b=state_gla_bwd[:, 0].reshape(DEC_BATCH, dk_all, GLA_DV))
    xmid1, tr1, route1, cnt1 = _outproj_route([y_gla_c, y_gla_l], w_out_odd[0], (x1,), mods, 1,
                                              norm_ffn_w[1], *router_params(1))
    moe1 = _moe(tr1, route1, cnt1, 1, w_gate_exp, w_up_exp, w_down_exp)
    y_c, y_l = _final(xmid1, moe1, mods, 1, final_norm_w)

    y_prompt = y_c.reshape(BATCH, SEQ, D)
    y_sample = y_l.reshape(DEC_BATCH, DEC_SEQ, D)
    def cache_out(t):
        return jnp.transpose(t.reshape(BATCH, 1, ATT_KV, ATT_HD, SEQ), (0, 1, 4, 2, 3))

    new_k, new_v = cache_out(new_kt), cache_out(new_vt)
    return (y_prompt, y_sample, new_k, new_v,
            ssd_f.reshape(BATCH, 1, SSD_HEADS, SSD_P, SSD_N), ssd_b.reshape(BATCH, 1, SSD_HEADS, SSD_P, SSD_N),
            gla_f.reshape(BATCH, 1, GLA_HEADS, GLA_DK, GLA_DV), gla_b.reshape(BATCH, 1, GLA_HEADS, GLA_DK, GLA_DV))
```

```python
import functools
import math

import numpy as np
import jax
import jax.numpy as jnp
from jax import lax
from jax.experimental import pallas as pl
from jax.experimental.pallas import tpu as pltpu

F32 = jnp.float32
BF16 = jnp.bfloat16

D = 1024
BATCH, SEQ = 16, 256
DEC_BATCH, DEC_SEQ = 2, 1024
PAST_LEN = 512
GRID_W = 64
EPS = 1e-6
T_CTX = BATCH * SEQ
T_LAT = DEC_BATCH * DEC_SEQ
T = T_CTX + T_LAT

SSD_HEADS, SSD_P, SSD_N, SSD_GROUPS = 16, 64, 128, 2
SSD_CONV = 5
SSD_L = 128
D_SSD = SSD_HEADS * SSD_P
HEADS_PER_GROUP = SSD_HEADS // SSD_GROUPS
GROUP_W = HEADS_PER_GROUP * SSD_P

ATT_HEADS, ATT_KV, ATT_HD = 16, 4, 64
ATT_KV_DIM = ATT_KV * ATT_HD
WINDOW = 128
ATT_BLOCK = 128
ATT_SCALE = ATT_HD ** -0.5
ROPE_BASE = 10000.0

GLA_HEADS, GLA_DK, GLA_DV = 4, 128, 256
GLA_C = 64
GLA_GATE_NORM = 16.0
GLA_LOWRANK = 16

N_GROUPS, EXP_PER_GROUP = 4, 4
N_EXPERTS = 16
D_EXPERT = 512

LANES = 128
SUBLANES = 8
VMEM_LIMIT = 56 * 1024 * 1024

P0_Z, P0_X, P0_Q, P0_BC, P0_K, P0_V = 0, 1024, 2048, 3072, 3584, 3840
P0_W = 4096
P1_Q, P1_K, P1_V, P1_R = 0, 512, 1024, 2048
P1_W = 3072

MOE_TM = 256
MOE_TILES = (2 * T) // MOE_TM + N_EXPERTS
MOE_ROWS = MOE_TILES * MOE_TM
ACC_ROWS = (T + SUBLANES) * SUBLANES


def _cparams(n_axes, vmem=VMEM_LIMIT):
    return pltpu.CompilerParams(dimension_semantics=("arbitrary",) * n_axes, vmem_limit_bytes=vmem)


def _silu(x):
    return x / (1.0 + jnp.exp(-x))


def _softplus(x):
    return jnp.maximum(x, 0.0) + jnp.log(1.0 + jnp.exp(-jnp.abs(x)))


def _mm(a, b):
    return jnp.dot(a.astype(BF16), b.astype(BF16), preferred_element_type=F32)


def _mm_nt(a, b):
    return lax.dot_general(a.astype(BF16), b.astype(BF16), (((1,), (1,)), ((), ())),
                           preferred_element_type=F32)


def _mm_tn(a, b):
    return lax.dot_general(a.astype(BF16), b.astype(BF16), (((0,), (0,)), ((), ())),
                           preferred_element_type=F32)


def _rms(x, w):
    return x * lax.rsqrt(jnp.mean(x * x, axis=-1, keepdims=True) + EPS) * w


def _cumsum_rows(x, n):
    row = lax.broadcasted_iota(jnp.int32, x.shape, 0)
    s = 1
    while s < n:
        x = x + jnp.where(row >= s, pltpu.roll(x, s, 0), 0.0)
        s *= 2
    return x


def _mod_row(tok0):
    return jnp.where(tok0 < T_CTX, 0, 1 + (tok0 - T_CTX) // DEC_SEQ)


ADA_TN = 1536


def _adaln_kernel(c_ref, w_ref, b_ref, o_ref):
    s = _silu(c_ref[...])
    o_ref[0] = _mm(s, w_ref[0]) + b_ref[0]


def _adaln(cond8, w_ada, b_ada):
    depth = w_ada.shape[0]
    return pl.pallas_call(
        _adaln_kernel,
        grid=(depth, 6 * D // ADA_TN),
        in_specs=[
            pl.BlockSpec((SUBLANES, D), lambda l, j: (0, 0)),
            pl.BlockSpec((1, D, ADA_TN), lambda l, j: (l, 0, j)),
            pl.BlockSpec((1, 1, ADA_TN), lambda l, j: (l, 0, j)),
        ],
        out_specs=pl.BlockSpec((1, SUBLANES, ADA_TN), lambda l, j: (l, 0, j)),
        out_shape=jax.ShapeDtypeStruct((depth, SUBLANES, 6 * D), F32),
        compiler_params=_cparams(2),
        name="adaln",
    )(cond8, w_ada, b_ada.reshape(depth, 1, 6 * D))


def _mod_spec(layer, chunk, tm, tile_of=lambda i, *_: i):
    return pl.BlockSpec((1, 1, D), lambda *g: ((layer * SUBLANES + _mod_row(tile_of(*g) * tm)) * 6 + chunk, 0, 0))


def _tokmajor_to_std(ref, tm):
    return jnp.concatenate([ref[pl.ds(k, tm, stride=SUBLANES), :] for k in range(D // LANES)], axis=1)


PROJ_TM = 1024
PROJ_TN = 512


def _ctx_lat_specs(tm, width=D):
    n_ctx = T_CTX // tm
    return (pl.BlockSpec((tm, width), lambda i, *_: (jnp.minimum(i, n_ctx - 1), 0)),
            pl.BlockSpec((tm, width), lambda i, *_: (jnp.maximum(i - n_ctx, 0), 0)))


def _modproj_kernel(*refs, dual_x):
    it = iter(refs)
    if dual_x:
        xc_ref, xl_ref = next(it), next(it)
    else:
        x_ref, moe_ref, g2_ref = next(it), next(it), next(it)
    sh_ref, sc_ref, nw_ref, w_ref, ws_ref, o_ref, os_ref = (next(it) for _ in range(7))
    xo_ref = None if dual_x else next(it)
    h_all, w_bf = next(it), next(it)
    j, i = pl.program_id(0), pl.program_id(1)
    tm = PROJ_TM
    rows = pl.ds(pl.multiple_of(i * tm, tm), tm)

    @pl.when(j == 0)
    def _():
        if dual_x:
            x = jnp.where(i < T_CTX // tm, xc_ref[...], xl_ref[...])
        else:
            x = x_ref[...] + g2_ref[0] * _tokmajor_to_std(moe_ref, tm)
            xo_ref[...] = x
        h = (_rms(x, nw_ref[...]) * (1.0 + sc_ref[0]) + sh_ref[0]).astype(BF16)
        h_all[rows, :] = h
        os_ref[...] = _mm_nt(h, ws_ref[...])

    @pl.when(i == 0)
    def _():
        w_bf[...] = w_ref[...].astype(BF16)

    o_ref[...] = _mm_nt(h_all[rows, :], w_bf[...]).astype(o_ref.dtype)


def _modproj(xs, mods, layer, norm_w, wt, tile_rows, small_row, out_block, moe=None):
    tm, tn = PROJ_TM, PROJ_TN
    dual_x = moe is None
    n_tiles = len(tile_rows)
    n_i, n_ctx = T // tm, T_CTX // tm

    def w_row(j, i):
        r = jnp.int32(tile_rows[0])
        for k in range(1, n_tiles):
            r = jnp.where(j == k, tile_rows[k], r)
        return pl.multiple_of(r, SUBLANES), 0

    def tok(j, i):
        return jnp.where(j == 0, i, n_i - 1)

    tile = pl.BlockSpec((tm, D), lambda j, i: (tok(j, i), 0))
    if dual_x:
        in_specs = [pl.BlockSpec((tm, D), lambda j, i: (jnp.minimum(tok(j, i), n_ctx - 1), 0)),
                    pl.BlockSpec((tm, D), lambda j, i: (jnp.maximum(tok(j, i) - n_ctx, 0), 0))]
        args = list(xs)
    else:
        in_specs = [tile, pl.BlockSpec((tm * SUBLANES, LANES), lambda j, i: (tok(j, i), 0)),
                    _mod_spec(layer - 1, 5, tm, tok)]
        args = [xs, moe, mods]
    in_specs += [_mod_spec(layer, 0, tm, tok), _mod_spec(layer, 1, tm, tok), pl.BlockSpec((1, D), lambda j, i: (0, 0)),
                 pl.BlockSpec((pl.Element(tn), pl.Element(D)), w_row),
                 pl.BlockSpec((pl.Element(LANES), pl.Element(D)), lambda j, i: (small_row, 0))]
    args += [mods, mods, norm_w.reshape(1, D), wt, wt]
    out_specs = [pl.BlockSpec((tm, tn), lambda j, i: (i, out_block(j))),
                 pl.BlockSpec((tm, LANES), lambda j, i: (tok(j, i), 0))]
    out_shape = [jax.ShapeDtypeStruct((T, n_tiles * tn), BF16), jax.ShapeDtypeStruct((T, LANES), F32)]
    if not dual_x:
        out_specs.append(tile)
        out_shape.append(jax.ShapeDtypeStruct((T, D), F32))
    return pl.pallas_call(
        functools.partial(_modproj_kernel, dual_x=dual_x),
        grid=(n_tiles, n_i), in_specs=in_specs, out_specs=out_specs, out_shape=out_shape,
        scratch_shapes=[pltpu.VMEM((T, D), BF16), pltpu.VMEM((tn, D), BF16)],
        compiler_params=_cparams(2),
        name=f"modproj{layer}",
    )(*args)


def _expand_heads(v, off):
    lo = lax.broadcasted_iota(jnp.int32, (v.shape[0], LANES), 1) < SSD_P
    tiles = []
    for q in range(SSD_HEADS // 2):
        a = jnp.broadcast_to(v[:, off + 2 * q:off + 2 * q + 1], (v.shape[0], LANES))
        b = jnp.broadcast_to(v[:, off + 2 * q + 1:off + 2 * q + 2], (v.shape[0], LANES))
        tiles.append(jnp.where(lo, a, b))
    return jnp.concatenate(tiles, axis=1)


def _ssd_kernel(*refs, seq, has_h0):
    if has_h0:
        (z_ref, x_ref, bc_ref, dt_ref, cwx_ref, cwbc_ref, cbx_ref, cbbc_ref, dtb_ref, alog_ref, dsk_ref,
         nw_ref, h0f_ref, h0b_ref, y_ref, sf_ref, sb_ref,
         xpad, bcpad, xc, bcc, a_scr, dt_scr, yacc, hf, hb) = refs
    else:
        (z_ref, x_ref, bc_ref, dt_ref, cwx_ref, cwbc_ref, cbx_ref, cbbc_ref, dtb_ref, alog_ref, dsk_ref,
         nw_ref, y_ref, sf_ref, sb_ref,
         xpad, bcpad, xc, bcc, a_scr, dt_scr, yacc, hf, hb) = refs
    L = SSD_L
    nc = seq // L
    pad = SUBLANES
    half = SSD_CONV // 2

    for buf, src, cw, cb, dst in ((xpad, x_ref, cwx_ref, cbx_ref, xc), (bcpad, bc_ref, cwbc_ref, cbbc_ref, bcc)):
        width = buf.shape[1]
        buf[0:pad, :] = jnp.zeros((pad, width), F32)
        buf[pad + seq:2 * pad + seq, :] = jnp.zeros((pad, width), F32)
        buf[pad:pad + seq, :] = src[...].astype(F32)
        for blk in range(nc):
            acc = jnp.broadcast_to(cb[...], (L, width))
            for j in range(SSD_CONV):
                r0 = pad - half + j + blk * L
                acc = acc + cw[j:j + 1, :] * buf[r0:r0 + L, :]
            dst[blk * L:(blk + 1) * L, :] = _silu(acc)

    lane = lax.broadcasted_iota(jnp.int32, (seq, LANES), 1)
    dts = jnp.where(lane < 2 * SSD_HEADS, _softplus(dt_ref[...] + dtb_ref[...]), 0.0)
    dt_scr[...] = dts
    a_scr[...] = dts * (-jnp.exp(alog_ref[...]))

    if has_h0:
        hf[...] = h0f_ref[...]
        hb[...] = h0b_ref[...]
    else:
        hf[...] = jnp.zeros(hf.shape, F32)
        hb[...] = jnp.zeros(hb.shape, F32)

    row = lax.broadcasted_iota(jnp.int32, (L, L), 0)
    col = lax.broadcasted_iota(jnp.int32, (L, L), 1)
    lane_l = lax.broadcasted_iota(jnp.int32, (L, LANES), 1)
    lo_half = lane_l < SSD_P

    def chunk(c, fwd, h_scr):
        off = 0 if fwd else SSD_HEADS
        r0 = pl.multiple_of(c * L, L)
        a = a_scr[pl.ds(r0, L), :]
        dt = dt_scr[pl.ds(r0, L), :]
        cs = _cumsum_rows(a, L)
        total = cs[L - 1:L, :]
        if fwd:
            u = cs
            rvec = jnp.exp(cs)
            ed = jnp.exp(total - cs) * dt
            keep = col <= row
        else:
            ex = cs - a
            u = -ex
            rvec = jnp.exp(total - ex)
            ed = jnp.exp(ex) * dt
            keep = col >= row
        ut = jnp.transpose(u)
        dtt = jnp.transpose(dt)
        tcol = jnp.transpose(jnp.broadcast_to(total, (L, LANES)))[:, 0:1]
        rexp = _expand_heads(rvec, off)
        edexp = _expand_heads(ed, off)
        x = xc[pl.ds(r0, L), :]
        bc = bcc[pl.ds(r0, L), :]
        outs = []
        for g in range(SSD_GROUPS):
            bg = bc[:, g * SSD_N:(g + 1) * SSD_N]
            cg = bc[:, SSD_GROUPS * SSD_N + g * SSD_N:SSD_GROUPS * SSD_N + (g + 1) * SSD_N]
            cbm = _mm_nt(cg, bg)
            hg = h_scr[g * GROUP_W:(g + 1) * GROUP_W, :]
            xg = x[:, g * GROUP_W:(g + 1) * GROUP_W]
            y_off = _mm_nt(cg, hg) * rexp[:, g * GROUP_W:(g + 1) * GROUP_W]
            tiles = []
            for p in range(HEADS_PER_GROUP // 2):
                xt = xg[:, p * LANES:(p + 1) * LANES]
                acc = None
                for s in range(2):
                    h = off + g * HEADS_PER_GROUP + 2 * p + s
                    seg = u[:, h:h + 1] - ut[h:h + 1, :]
                    m = cbm * jnp.exp(jnp.where(keep, seg, -jnp.inf)) * dtt[h:h + 1, :]
                    xm = jnp.where(lo_half if s == 0 else jnp.logical_not(lo_half), xt, 0.0)
                    d = _mm(m, xm)
                    acc = d if acc is None else acc + d
                tiles.append(acc)
            outs.append(y_off + jnp.concatenate(tiles, axis=1))
            decs = []
            for hh in range(HEADS_PER_GROUP):
                h = off + g * HEADS_PER_GROUP + hh
                decs.append(jnp.broadcast_to(jnp.exp(tcol[h:h + 1, :]), (SSD_P, SSD_N)))
            dec = jnp.concatenate(decs, axis=0)
            h_scr[g * GROUP_W:(g + 1) * GROUP_W, :] = dec * hg + _mm_tn(xg * edexp[:, g * GROUP_W:(g + 1) * GROUP_W], bg)
        return r0, x, jnp.concatenate(outs, axis=1)

    def fwd_body(c, carry):
        r0, _, y = chunk(c, True, hf)
        yacc[pl.ds(r0, L), :] = y
        return carry

    lax.fori_loop(0, nc, fwd_body, 0)

    def bwd_body(i, carry):
        c = nc - 1 - i
        r0, x, y = chunk(c, False, hb)
        y = yacc[pl.ds(r0, L), :] + y + dsk_ref[...] * x
        y = y * _silu(z_ref[pl.ds(r0, L), :].astype(F32))
        y_ref[pl.ds(r0, L), :] = _rms(y, nw_ref[...]).astype(y_ref.dtype)
        return carry

    lax.fori_loop(0, nc, bwd_body, 0)
    sf_ref[...] = hf[...]
    sb_ref[...] = hb[...]


def _ssd(p0, p0dt, tok0, nseq, seq, cw, cb, dtb, alog, dskip, nw, h0f=None, h0b=None):
    has_h0 = h0f is not None
    b0 = tok0 // seq

    def cols(width, start):
        return pl.BlockSpec((seq, width), lambda s: (b0 + s, start // width))

    def full(shape):
        return pl.BlockSpec(shape, lambda s: (0,) * len(shape))

    in_specs = [cols(D_SSD, P0_Z), cols(D_SSD, P0_X), cols(512, P0_BC), cols(LANES, 0),
                full((SSD_CONV, D_SSD)), full((SSD_CONV, 512)), full((1, D_SSD)), full((1, 512)),
                full((1, LANES)), full((1, LANES)), full((1, D_SSD)), full((1, D_SSD))]
    args = [p0, p0, p0, p0dt, cw[:, :D_SSD], cw[:, D_SSD:], cb[:D_SSD].reshape(1, -1), cb[D_SSD:].reshape(1, -1),
            dtb, alog, dskip, nw.reshape(1, -1)]
    st_spec = pl.BlockSpec((None, D_SSD, SSD_N), lambda s: (s, 0, 0))
    if has_h0:
        in_specs += [st_spec, st_spec]
        args += [h0f, h0b]
    st_shape = jax.ShapeDtypeStruct((nseq, D_SSD, SSD_N), F32)
    return pl.pallas_call(
        functools.partial(_ssd_kernel, seq=seq, has_h0=has_h0),
        grid=(nseq,), in_specs=in_specs,
        out_specs=[pl.BlockSpec((seq, D_SSD), lambda s: (s, 0)), st_spec, st_spec],
        out_shape=[jax.ShapeDtypeStruct((nseq * seq, D_SSD), BF16), st_shape, st_shape],
        scratch_shapes=[pltpu.VMEM((seq + 2 * SUBLANES, D_SSD), F32), pltpu.VMEM((seq + 2 * SUBLANES, 512), F32),
                        pltpu.VMEM((seq, D_SSD), F32), pltpu.VMEM((seq, 512), F32),
                        pltpu.VMEM((seq, LANES), F32), pltpu.VMEM((seq, LANES), F32),
                        pltpu.VMEM((seq, D_SSD), F32),
                        pltpu.VMEM((D_SSD, SSD_N), F32), pltpu.VMEM((D_SSD, SSD_N), F32)],
        compiler_params=_cparams(1),
        name=f"ssd{seq}",
    )(*args)


def _place_halves(tile, kv_in_high):
    lo = lax.broadcasted_iota(jnp.int32, tile.shape, 1) < ATT_HD
    swapped = pltpu.roll(tile, ATT_HD, 1)
    if kv_in_high:
        return jnp.where(lo, swapped, 0.0), jnp.where(lo, 0.0, tile)
    return jnp.where(lo, tile, 0.0), jnp.where(lo, 0.0, swapped)


def _place_rows(vt, kv_in_high):
    head = vt[ATT_HD:, :] if kv_in_high else vt[:ATT_HD, :]
    z = jnp.zeros_like(head)
    return jnp.concatenate([head, z], axis=0), jnp.concatenate([z, head], axis=0)


LOG2E = 1.4426950408889634
SCORE_SCALE = ATT_SCALE * LOG2E


def _sink_attend_t(score_parts, value_parts, sink2):
    m = sink2
    for s in score_parts:
        m = jnp.maximum(m, jnp.max(s, axis=0, keepdims=True))
    denom = jnp.exp2(sink2 - m)
    out = None
    for s, v in zip(score_parts, value_parts):
        p = jnp.exp2(s - m)
        denom = denom + jnp.sum(p, axis=0, keepdims=True)
        o = _mm(v, p)
        out = o if out is None else out + o
    return out * (1.0 / denom)


def _attn_schedule(n, scores, attend):
    scores(0)
    for j in range(n):
        if j + 1 < n:
            scores(j + 1)
        attend(j)


def _ctx_attn_kernel(q_ref, k_ref, v_ref, sink_ref, o_ref, kt_ref, vt_ref, s_a, s_b):
    sink2 = sink_ref[...] * LOG2E
    bufs = (s_a, s_b)
    half = SEQ // 2
    for t in range(ATT_KV_DIM // LANES):
        cols = slice(t * LANES, (t + 1) * LANES)
        kt_ref[cols, :] = jnp.transpose(k_ref[:, cols].astype(F32))
        vt_ref[cols, :] = jnp.transpose(v_ref[:, cols].astype(F32))

    def kv_tile(ref, j):
        return ref[:, (j // 2) * LANES:(j // 2 + 1) * LANES].astype(F32), (j % 2 == 1)

    def scores(j):
        k_lo, k_hi = _place_halves(*kv_tile(k_ref, j))
        qst = jnp.concatenate([q_ref[:, qt * LANES:(qt + 1) * LANES] for qt in (2 * j, 2 * j + 1)], axis=0)
        bufs[j % 2][...] = _mm_nt(jnp.concatenate([k_lo, k_hi], axis=0), qst) * SCORE_SCALE

    def attend(j):
        src = bufs[j % 2]
        v, high = kv_tile(v_ref, j)
        vts = _place_rows(jnp.transpose(v), high)
        for ql in range(2):
            qt = 2 * j + ql
            for qh in range(2):
                cols = slice(ql * SEQ + qh * half, ql * SEQ + (qh + 1) * half)
                acc = None
                for s, vv in enumerate(vts):
                    o = _sink_attend_t([src[s * SEQ:(s + 1) * SEQ, cols]], [vv], sink2[:, 2 * qt + s:2 * qt + s + 1])
                    acc = o if acc is None else acc + o
                o_ref[qh * half:(qh + 1) * half, qt * LANES:(qt + 1) * LANES] = jnp.transpose(acc).astype(o_ref.dtype)

    _attn_schedule(ATT_KV, scores, attend)


def _ctx_attn(p0, sink):
    def cols(width, start):
        return pl.BlockSpec((SEQ, width), lambda b: (b, start // width))

    sbuf = pltpu.VMEM((2 * SEQ, 2 * SEQ), F32)
    return pl.pallas_call(
        _ctx_attn_kernel,
        grid=(BATCH,),
        in_specs=[cols(D, P0_Q), cols(ATT_KV_DIM, P0_K), cols(ATT_KV_DIM, P0_V),
                  pl.BlockSpec((1, LANES), lambda b: (0, 0))],
        out_specs=[pl.BlockSpec((SEQ, D), lambda b: (b, 0))]
        + [pl.BlockSpec((None, ATT_KV_DIM, SEQ), lambda b: (b, 0, 0))] * 2,
        out_shape=[jax.ShapeDtypeStruct((T_CTX, D), BF16)] + [jax.ShapeDtypeStruct((BATCH, ATT_KV_DIM, SEQ), F32)] * 2,
        scratch_shapes=[sbuf, sbuf],
        compiler_params=_cparams(1),
        name="ctx_attn",
    )(p0, p0, p0, sink)


def _rope_tables():
    quarter = ATT_HD // 4
    t = np.arange(DEC_SEQ)
    lane = np.arange(LANES)
    inv = ROPE_BASE ** (-(lane % quarter).astype(np.float64) / quarter)
    pos = np.where(((lane % ATT_HD) < ATT_HD // 2)[None, :], (t // GRID_W)[:, None], (t % GRID_W)[:, None])
    ang = pos * inv[None, :]
    first = ((lane % (2 * quarter)) < quarter)[None, :]
    cos, sin = np.cos(ang), np.sin(ang)
    return (jnp.asarray(cos, F32), jnp.asarray(np.where(first, -sin, 0.0), F32),
            jnp.asarray(np.where(first, 0.0, sin), F32))


def _rope(x, cos, sa, sb):
    quarter = ATT_HD // 4
    return x * cos + pltpu.roll(x, LANES - quarter, 1) * sa + pltpu.roll(x, quarter, 1) * sb


def _lat_attn_kernel(q_ref, kp_ref, kc_ref, kn_ref, vp_ref, vc_ref, vn_ref, ck_ref, cv_ref,
                     cos_ref, sa_ref, sb_ref, sink_ref, o_ref, c_a, c_b, w_a, w_b):
    blk = pl.program_id(1)
    nb = pl.num_programs(1)
    B = ATT_BLOCK
    sink2 = sink_ref[...] * LOG2E
    cbufs, wbufs = (c_a, c_b), (w_a, w_b)

    def tables(b):
        r0 = pl.multiple_of(b * B, B)
        return cos_ref[pl.ds(r0, B), :], sa_ref[pl.ds(r0, B), :], sb_ref[pl.ds(r0, B), :]

    tq = tables(blk)
    tk = [tables(jnp.maximum(blk - 1, 0)), tq, tables(jnp.minimum(blk + 1, nb - 1))]
    kabs = (blk - 1) * B + lax.broadcasted_iota(jnp.int32, (3 * B, B), 0)
    qpos = blk * B + lax.broadcasted_iota(jnp.int32, (3 * B, B), 1)
    ok = (jnp.abs(qpos - kabs) <= WINDOW) & (kabs >= 0) & (kabs < nb * B)
    ok = jnp.concatenate([ok, ok], axis=1)
    ok = jnp.concatenate([ok, ok], axis=0)

    def scores(j):
        high = (j % 2 == 1)
        sl = slice((j // 2) * LANES, (j // 2 + 1) * LANES)
        kw = jnp.concatenate([_rope(r[:, sl].astype(F32), *tb) for r, tb in zip((kp_ref, kc_ref, kn_ref), tk)], axis=0)
        qs = [q_ref[:, qt * LANES:(qt + 1) * LANES].astype(F32) for qt in (2 * j, 2 * j + 1)]
        q_plain = jnp.concatenate(qs, axis=0)
        q_rope = jnp.concatenate([_rope(q, *tq) for q in qs], axis=0)
        ck = jnp.transpose(ck_ref[sl, :])
        cbufs[j % 2][...] = _mm_nt(jnp.concatenate(_place_halves(ck, high), axis=0), q_plain) * SCORE_SCALE
        win = _mm_nt(jnp.concatenate(_place_halves(kw, high), axis=0), q_rope) * SCORE_SCALE
        wbufs[j % 2][...] = jnp.where(ok, win, -jnp.inf)

    def attend(j):
        high = (j % 2 == 1)
        sl = slice((j // 2) * LANES, (j // 2 + 1) * LANES)
        vw = jnp.concatenate([jnp.transpose(r[:, sl].astype(F32)) for r in (vp_ref, vc_ref, vn_ref)], axis=1)
        vts = _place_rows(vw, high)
        cvts = _place_rows(cv_ref[sl, :], high)
        csrc, wsrc = cbufs[j % 2], wbufs[j % 2]
        for ql in range(2):
            qt = 2 * j + ql
            cols = slice(ql * B, (ql + 1) * B)
            acc = None
            for s in range(2):
                parts = [csrc[s * PAST_LEN:(s + 1) * PAST_LEN, cols], wsrc[s * 3 * B:(s + 1) * 3 * B, cols]]
                o = _sink_attend_t(parts, [cvts[s], vts[s]], sink2[:, 2 * qt + s:2 * qt + s + 1])
                acc = o if acc is None else acc + o
            o_ref[:, qt * LANES:(qt + 1) * LANES] = jnp.transpose(acc).astype(o_ref.dtype)

    _attn_schedule(ATT_KV, scores, attend)


def _lat_attn(p0, ck, cv, sink):
    nb = DEC_SEQ // ATT_BLOCK
    base = T_CTX // ATT_BLOCK

    def kv(start, shift):
        return pl.BlockSpec((ATT_BLOCK, ATT_KV_DIM),
                            lambda b, i: (base + b * nb + jnp.clip(i + shift, 0, nb - 1), start // ATT_KV_DIM))

    def full(shape):
        return pl.BlockSpec(shape, lambda b, i: (0,) * len(shape))

    cache = pl.BlockSpec((None, ATT_KV_DIM, PAST_LEN), lambda b, i: (b, 0, 0))
    cos, sa, sb = _rope_tables()
    return pl.pallas_call(
        _lat_attn_kernel,
        grid=(DEC_BATCH, nb),
        in_specs=[pl.BlockSpec((ATT_BLOCK, D), lambda b, i: (base + b * nb + i, P0_Q // D)),
                  kv(P0_K, -1), kv(P0_K, 0), kv(P0_K, 1), kv(P0_V, -1), kv(P0_V, 0), kv(P0_V, 1),
                  cache, cache, full((DEC_SEQ, LANES)), full((DEC_SEQ, LANES)), full((DEC_SEQ, LANES)),
                  full((1, LANES))],
        out_specs=pl.BlockSpec((ATT_BLOCK, D), lambda b, i: (b * nb + i, 0)),
        out_shape=jax.ShapeDtypeStruct((T_LAT, D), BF16),
        scratch_shapes=[pltpu.VMEM((2 * PAST_LEN, 2 * ATT_BLOCK), F32)] * 2
        + [pltpu.VMEM((2 * 3 * ATT_BLOCK, 2 * ATT_BLOCK), F32)] * 2,
        compiler_params=_cparams(2),
        name="lat_attn",
    )(p0, p0, p0, p0, p0, p0, p0, ck, cv, cos, sa, sb, sink)


def _log_sigmoid(x):
    return jnp.minimum(x, 0.0) - jnp.log(1.0 + jnp.exp(-jnp.abs(x)))


def _gla_kernel(*refs, seq, has_s0):
    if has_s0:
        (q_ref, k_ref, v_ref, r_ref, lr_ref, w2f_ref, w2b_ref, bf_ref, bb_ref, nw_ref, s0f_ref, s0b_ref,
         y_ref, sf_ref, sb_ref, gf, gb, yf, yb, stf, stb) = refs
    else:
        (q_ref, k_ref, v_ref, r_ref, lr_ref, w2f_ref, w2b_ref, bf_ref, bb_ref, nw_ref,
         y_ref, sf_ref, sb_ref, gf, gb, yf, yb, stf, stb) = refs
    C = GLA_C
    nc = seq // C
    lr = lr_ref[...]
    gf[...] = _log_sigmoid(_mm(lr, w2f_ref[...]) + bf_ref[...]) / GLA_GATE_NORM
    gb[...] = _log_sigmoid(_mm(lr, w2b_ref[...]) + bb_ref[...]) / GLA_GATE_NORM
    for h in range(GLA_HEADS):
        rows = slice(h * GLA_DV, (h + 1) * GLA_DV)
        if has_s0:
            stf[rows, :] = jnp.transpose(s0f_ref[h * GLA_DK:(h + 1) * GLA_DK, :])
            stb[rows, :] = jnp.transpose(s0b_ref[h * GLA_DK:(h + 1) * GLA_DK, :])
        else:
            stf[rows, :] = jnp.zeros((GLA_DV, GLA_DK), F32)
            stb[rows, :] = jnp.zeros((GLA_DV, GLA_DK), F32)

    row = lax.broadcasted_iota(jnp.int32, (C, C), 0)
    col = lax.broadcasted_iota(jnp.int32, (C, C), 1)
    qscale = GLA_DK ** -0.5

    def chunk(c, fwd):
        g_scr, y_scr, st = (gf, yf, stf) if fwd else (gb, yb, stb)
        r0 = pl.multiple_of(c * C, C)
        g = g_scr[pl.ds(r0, C), :]
        cs = _cumsum_rows(g, C)
        total = cs[C - 1:C, :]
        q = q_ref[pl.ds(r0, C), :].astype(F32) * qscale
        k = k_ref[pl.ds(r0, C), :].astype(F32)
        v = v_ref[pl.ds(r0, C), :]
        if fwd:
            qs, ks, ke = q * jnp.exp(cs), k * jnp.exp(-cs), k * jnp.exp(total - cs)
            keep = col <= row
        else:
            ex = cs - g
            qs, ks, ke = q * jnp.exp(total - ex), k * jnp.exp(ex - total), k * jnp.exp(ex)
            keep = col >= row
        dec = jnp.exp(total)
        for h in range(GLA_HEADS):
            kc = slice(h * GLA_DK, (h + 1) * GLA_DK)
            vc = slice(h * GLA_DV, (h + 1) * GLA_DV)
            s_t = st[vc, :]
            att = jnp.where(keep, _mm_nt(qs[:, kc], ks[:, kc]), 0.0)
            y_scr[pl.ds(r0, C), vc] = _mm(att, v[:, vc]) + _mm_nt(qs[:, kc], s_t)
            st[vc, :] = dec[:, kc] * s_t + _mm_tn(v[:, vc], ke[:, kc])

    def body(i, carry):
        chunk(i, True)
        chunk(nc - 1 - i, False)
        return carry

    lax.fori_loop(0, nc, body, 0)

    nw = nw_ref[...]
    for blk in range(seq // LANES):
        rs = slice(blk * LANES, (blk + 1) * LANES)
        y = yf[rs, :] + yb[rs, :]
        gate = _silu(r_ref[rs, :].astype(F32))
        for h in range(GLA_HEADS):
            vc = slice(h * GLA_DV, (h + 1) * GLA_DV)
            y_ref[rs, vc] = (_rms(y[:, vc], nw) * gate[:, vc]).astype(y_ref.dtype)
    for h in range(GLA_HEADS):
        rows = slice(h * GLA_DV, (h + 1) * GLA_DV)
        sf_ref[h * GLA_DK:(h + 1) * GLA_DK, :] = jnp.transpose(stf[rows, :])
        sb_ref[h * GLA_DK:(h + 1) * GLA_DK, :] = jnp.transpose(stb[rows, :])


def _gla(p1, p1lr, tok0, nseq, seq, w2f, w2b, bgf, bgb, nw, s0f=None, s0b=None):
    has_s0 = s0f is not None
    b0 = tok0 // seq
    dk_all, dv_all = GLA_HEADS * GLA_DK, GLA_HEADS * GLA_DV

    def cols(width, start):
        return pl.BlockSpec((seq, width), lambda s: (b0 + s, start // width))

    def full(shape):
        return pl.BlockSpec(shape, lambda s: (0,) * len(shape))

    in_specs = [cols(dk_all, P1_Q), cols(dk_all, P1_K), cols(dv_all, P1_V), cols(dv_all, P1_R), cols(LANES, 0),
                full((LANES, dk_all)), full((LANES, dk_all)), full((1, dk_all)), full((1, dk_all)), full((1, GLA_DV))]
    args = [p1, p1, p1, p1, p1lr, w2f, w2b, bgf.reshape(1, -1), bgb.reshape(1, -1), nw.reshape(1, -1)]
    st_spec = pl.BlockSpec((None, dk_all, GLA_DV), lambda s: (s, 0, 0))
    if has_s0:
        in_specs += [st_spec, st_spec]
        args += [s0f, s0b]
    st_shape = jax.ShapeDtypeStruct((nseq, dk_all, GLA_DV), F32)
    return pl.pallas_call(
        functools.partial(_gla_kernel, seq=seq, has_s0=has_s0),
        grid=(nseq,), in_specs=in_specs,
        out_specs=[pl.BlockSpec((seq, dv_all), lambda s: (s, 0)), st_spec, st_spec],
        out_shape=[jax.ShapeDtypeStruct((nseq * seq, dv_all), BF16), st_shape, st_shape],
        scratch_shapes=[pltpu.VMEM((seq, dk_all), F32), pltpu.VMEM((seq, dk_all), F32),
                        pltpu.VMEM((seq, dv_all), F32), pltpu.VMEM((seq, dv_all), F32),
                        pltpu.VMEM((dv_all, GLA_DK), F32), pltpu.VMEM((dv_all, GLA_DK), F32)],
        compiler_params=_cparams(1),
        name=f"gla{seq}",
    )(*args)


ROUTE_TM = 512
ROUTE_SUB = 256
ROUTE_ROWS = 32


def _split_bf16(x):
    hi = x.astype(BF16)
    return hi, (x - hi.astype(F32)).astype(BF16)


def _outproj_kernel(*refs, n_in, dual_x):
    y_refs = refs[:2 * n_in]
    n_x = 2 if dual_x else 1
    x_refs = refs[2 * n_in + 1:2 * n_in + 1 + n_x]
    w_ref = refs[2 * n_in]
    (g1_ref, sh_ref, sc_ref, nw_ref, wr_ref, br_ref,
     xo_ref, tr_ref, route_ref, cnt_ref, w_scr, wr_hl, carry) = refs[2 * n_in + 1 + n_x:]
    is_ctx = pl.program_id(0) < T_CTX // ROUTE_TM

    @pl.when(pl.program_id(0) == 0)
    def _():
        w_scr[...] = w_ref[...].astype(BF16)
        hi, lo = _split_bf16(jnp.transpose(wr_ref[...]))
        wr_hl[0:LANES, :] = hi
        wr_hl[LANES:2 * LANES, :] = lo
        carry[...] = jnp.zeros(carry.shape, F32)

    for sub in range(ROUTE_TM // ROUTE_SUB):
        _outproj_subtile(sub, is_ctx, y_refs, x_refs, dual_x, n_in, g1_ref, sh_ref, sc_ref, nw_ref, br_ref,
                         xo_ref, tr_ref, route_ref, cnt_ref, w_scr, wr_hl, carry)


def _outproj_subtile(sub, is_ctx, y_refs, x_refs, dual_x, n_in, g1_ref, sh_ref, sc_ref, nw_ref, br_ref,
                     xo_ref, tr_ref, route_ref, cnt_ref, w_scr, wr_hl, carry):
    tm = ROUTE_SUB
    rows = slice(sub * tm, (sub + 1) * tm)
    o = None
    for i in range(n_in):
        y = jnp.where(is_ctx, y_refs[2 * i][rows, :], y_refs[2 * i + 1][rows, :])
        d = jnp.dot(y, w_scr[i * D:(i + 1) * D, :], preferred_element_type=F32)
        o = d if o is None else o + d
    x_in = jnp.where(is_ctx, x_refs[0][rows, :], x_refs[1][rows, :]) if dual_x else x_refs[0][rows, :]
    x = x_in + g1_ref[0] * o
    xo_ref[rows, :] = x
    t = _rms(x, nw_ref[...]) * (1.0 + sc_ref[0]) + sh_ref[0]
    for k in range(D // LANES):
        tr_ref[pl.ds(sub * tm * SUBLANES + k, tm, stride=SUBLANES), :] = t[:, k * LANES:(k + 1) * LANES]

    t_hi, t_lo = _split_bf16(t)
    lg = _mm_nt(wr_hl[...], t_hi)
    nr = ROUTE_ROWS
    logit = lg[0:nr, :] + lg[LANES:LANES + nr, :] + _mm_nt(wr_hl[0:LANES, :], t_lo)[0:nr, :] + br_ref[0:nr, :]
    rowf = lax.broadcasted_iota(jnp.int32, (nr, tm), 0).astype(F32)
    neg = -jnp.inf

    def first_argmax(v, vmax):
        return jnp.min(jnp.where(v == vmax, rowf, float(LANES)), axis=0, keepdims=True)

    gl = jnp.where(rowf < N_GROUPS, logit, neg)
    gmax = jnp.max(gl, axis=0, keepdims=True)
    gsel = first_argmax(gl, gmax)
    gprob = 1.0 / jnp.sum(jnp.exp(gl - gmax), axis=0, keepdims=True)
    first = N_GROUPS + EXP_PER_GROUP * gsel
    el = jnp.where((rowf >= first) & (rowf < first + EXP_PER_GROUP), logit, neg)
    m1 = jnp.max(el, axis=0, keepdims=True)
    i1 = first_argmax(el, m1)
    el2 = jnp.where(rowf == i1, neg, el)
    m2 = jnp.max(el2, axis=0, keepdims=True)
    i2 = first_argmax(el2, m2)
    e2 = jnp.exp(m2 - m1)
    c1 = gprob / (1.0 + e2)
    c2 = gprob * e2 / (1.0 + e2)
    x1 = i1 - N_GROUPS
    x2 = i2 - N_GROUPS

    erow = rowf
    hot = ((erow == x1) | (erow == x2)).astype(F32)
    tri = (lax.broadcasted_iota(jnp.int32, (tm, tm), 0) < lax.broadcasted_iota(jnp.int32, (tm, tm), 1))
    before = _mm(hot, tri.astype(F32)) + carry[...]
    r1 = jnp.sum(jnp.where(erow == x1, before, 0.0), axis=0, keepdims=True)
    r2 = jnp.sum(jnp.where(erow == x2, before, 0.0), axis=0, keepdims=True)
    total = carry[...] + _mm(hot, jnp.ones((tm, tm), F32))
    carry[...] = total
    cnt_ref[...] = total[0:N_EXPERTS, 0:LANES]
    row8 = lax.broadcasted_iota(jnp.int32, (SUBLANES, tm), 0)
    out = jnp.zeros((SUBLANES, tm), F32)
    for k, v in enumerate((x1, x2, c1, c2, r1, r2)):
        out = jnp.where(row8 == k, jnp.broadcast_to(v, (SUBLANES, tm)), out)
    route_ref[:, rows] = out


def _outproj_route(ys, w_out, xs, mods, layer, norm_w, w_router, b_router):
    tm = ROUTE_TM
    n_in = len(ys) // 2
    dual_x = len(xs) == 2
    kdim = w_out.shape[0]

    def full(shape):
        return pl.BlockSpec(shape, lambda i: (0,) * len(shape))

    tile = pl.BlockSpec((tm, D), lambda i: (i, 0))
    pair = list(_ctx_lat_specs(tm))
    in_specs = (pair * n_in + [full((kdim, D))] + (pair if dual_x else [tile])
                + [_mod_spec(layer, 2, tm), _mod_spec(layer, 3, tm), _mod_spec(layer, 4, tm),
                   full((1, D)), full((D, LANES)), full((LANES, ROUTE_SUB))])
    cnt = jax.ShapeDtypeStruct((N_EXPERTS, LANES), F32)
    return pl.pallas_call(
        functools.partial(_outproj_kernel, n_in=n_in, dual_x=dual_x),
        grid=(T // tm,), in_specs=in_specs,
        out_specs=[tile, pl.BlockSpec((tm * SUBLANES, LANES), lambda i: (i, 0)),
                   pl.BlockSpec((SUBLANES, tm), lambda i: (0, i)), full(cnt.shape)],
        out_shape=[jax.ShapeDtypeStruct((T, D), F32), jax.ShapeDtypeStruct((T * SUBLANES, LANES), F32),
                   jax.ShapeDtypeStruct((SUBLANES, T), F32), cnt],
        scratch_shapes=[pltpu.VMEM((kdim, D), BF16), pltpu.VMEM((2 * LANES, D), BF16),
                        pltpu.VMEM((ROUTE_ROWS, ROUTE_SUB), F32)],
        compiler_params=_cparams(1),
        name=f"outproj{layer}",
    )(*ys, w_out, *xs, mods, mods, mods, norm_w.reshape(1, D), w_router,
      jnp.broadcast_to(b_router.reshape(LANES, 1), (LANES, ROUTE_SUB)))


def _moe_meta(counts):
    tm = MOE_TM
    counts = jnp.max(counts, axis=1).astype(jnp.int32)
    padded = ((counts + tm - 1) // tm) * tm
    ends = jnp.cumsum(padded)
    starts = ends - padded
    pads = jnp.concatenate([starts + counts, ends[-1:], ends, jnp.full((1,), MOE_ROWS + tm)]).astype(jnp.int32)
    einfo = jnp.concatenate([starts // tm, padded // tm, ends[-1:] // tm]).astype(jnp.int32)
    return starts.astype(jnp.int32), pads, einfo


def _moe_up_kernel(pos1_ref, pos2_ref, pads_ref, einfo_ref, tr_hbm, wg_ref, wu_ref, a_hbm, rowmap_ref,
                   tr_scr, g0, g1, a0, a1, wg_bf, wu_bf, sem, a_sems):
    e = pl.program_id(0)
    tm = MOE_TM
    first, n = einfo_ref[e], einfo_ref[N_EXPERTS + e]

    def gather(tile, dst):
        for mi in range(tm):
            tok = jnp.minimum(rowmap_ref[tile * tm + mi] >> 1, T - 1)
            dst[mi * SUBLANES:(mi + 1) * SUBLANES, :] = tr_scr[pl.ds(pl.multiple_of(tok * SUBLANES, SUBLANES), SUBLANES), :]

    @pl.when(e == 0)
    def _():
        load = pltpu.make_async_copy(tr_hbm, tr_scr, sem)
        load.start()

        def clear(c, carry):
            for i in range(SUBLANES):
                rowmap_ref[c * SUBLANES + i] = 2 * T
            return carry
        for k in range(N_EXPERTS + 1):
            lax.fori_loop(pads_ref[k] // SUBLANES, pads_ref[N_EXPERTS + 1 + k] // SUBLANES, clear, 0)

        def place(t, carry):
            rowmap_ref[pos1_ref[t]] = 2 * t
            rowmap_ref[pos2_ref[t]] = 2 * t + 1
            return carry
        lax.fori_loop(0, T, place, 0, unroll=8)
        load.wait()

    def store(buf, tile, k):
        return pltpu.make_async_copy(buf, a_hbm.at[pl.ds(pl.multiple_of(tile * tm, tm), tm), :], a_sems.at[k])

    def compute(cur, out):
        x = _tokmajor_to_std(cur, tm).astype(BF16)
        g = jnp.dot(x, wg_bf[...], preferred_element_type=F32)
        u = jnp.dot(x, wu_bf[...], preferred_element_type=F32)
        out[...] = (_silu(g) * u).astype(out.dtype)

    @pl.when(n > 0)
    def _():
        wg_bf[...] = wg_ref[...].astype(BF16)
        wu_bf[...] = wu_ref[...].astype(BF16)
        last = first + n - 1
        gather(first, g0)

        def pair(p, carry):
            ta = first + 2 * p
            tb = jnp.minimum(ta + 1, last)
            tb_dst = jnp.where(ta + 1 > last, MOE_TILES, tb)

            @pl.when(p > 0)
            def _():
                store(a0, 0, 0).wait()
                store(a1, 0, 1).wait()

            gather(tb, g1)
            compute(g0, a0)
            store(a0, ta, 0).start()
            gather(jnp.minimum(ta + 2, last), g0)
            compute(g1, a1)
            store(a1, tb_dst, 1).start()
            return carry

        lax.fori_loop(0, (n + 1) // 2, pair, 0)
        store(a0, 0, 0).wait()
        store(a1, 0, 1).wait()

    @pl.when(e == pl.num_programs(0) - 1)
    def _():
        total = einfo_ref[2 * N_EXPERTS]
        a0[...] = jnp.zeros(a0.shape, a0.dtype)

        def fill(t, carry):
            store(a0, t, 0).start()
            return carry

        def drain(t, carry):
            store(a0, t, 0).wait()
            return carry
        lax.fori_loop(total, MOE_TILES + 1, fill, 0)
        lax.fori_loop(total, MOE_TILES + 1, drain, 0)


def _moe_down_kernel(rowmap_ref, cpair_ref, einfo_ref, a_hbm, wd_ref, out_hbm, acc, y0, y1, ab0, ab1, wd_bf, a_sems):
    e = pl.program_id(0)
    tm = MOE_TM
    zrows = 512
    first, n = einfo_ref[e], einfo_ref[N_EXPERTS + e]

    @pl.when(e == 0)
    def _():
        def zero(i, carry):
            acc[pl.ds(pl.multiple_of(i * zrows, zrows), zrows), :] = jnp.zeros((zrows, LANES), F32)
            return carry
        lax.fori_loop(0, ACC_ROWS // zrows, zero, 0)

    def load(buf, tile, k):
        return pltpu.make_async_copy(a_hbm.at[pl.ds(pl.multiple_of(tile * tm, tm), tm), :], buf, a_sems.at[k])

    def matmul(src, dst):
        y = jnp.dot(src[...], wd_bf[...], preferred_element_type=F32)
        for k in range(D // LANES):
            dst[pl.ds(k, tm, stride=SUBLANES), :] = y[:, k * LANES:(k + 1) * LANES]

    def scatter(tile, src):
        for b in range(tm // SUBLANES):
            ents = [rowmap_ref[tile * tm + b * SUBLANES + i] for i in range(SUBLANES)]
            offs = [pl.multiple_of((e >> 1) * SUBLANES, SUBLANES) for e in ents]
            olds = [acc[pl.ds(o, SUBLANES), :] for o in offs]
            for i, o in enumerate(offs):
                r = (b * SUBLANES + i) * SUBLANES
                acc[pl.ds(o, SUBLANES), :] = olds[i] + cpair_ref[ents[i]] * src[r:r + SUBLANES, :]

    @pl.when(n > 0)
    def _():
        wd_bf[...] = wd_ref[...].astype(BF16)
        last = first + n - 1
        head = load(ab0, first, 0)
        head.start()
        load(ab1, jnp.minimum(first + 1, last), 1).start()
        head.wait()
        matmul(ab0, y0)
        load(ab0, jnp.minimum(first + 2, last), 0).start()

        def pair(p, carry):
            ta = first + 2 * p
            tb_rows = jnp.where(ta + 1 > last, MOE_TILES, ta + 1)
            load(ab1, 0, 1).wait()
            matmul(ab1, y1)
            scatter(ta, y0)
            load(ab1, jnp.minimum(ta + 3, last), 1).start()
            load(ab0, 0, 0).wait()
            matmul(ab0, y0)
            scatter(tb_rows, y1)
            load(ab0, jnp.minimum(ta + 4, last), 0).start()
            return carry

        lax.fori_loop(0, (n + 1) // 2, pair, 0)
        load(ab0, 0, 0).wait()
        load(ab1, 0, 1).wait()

    @pl.when(e == pl.num_programs(0) - 1)
    def _():
        pltpu.sync_copy(acc.at[0:T * SUBLANES, :], out_hbm)


def _moe(tr, route_t, counts, layer, w_gate, w_up, w_down):
    tm = MOE_TM
    starts, pads, einfo = _moe_meta(counts)
    experts = jnp.arange(N_EXPERTS, dtype=jnp.int32)

    def position(e, r):
        sel = e.astype(jnp.int32)[:, None] == experts[None, :]
        return jnp.sum(jnp.where(sel, starts[None, :], 0), axis=1) + r.astype(jnp.int32)

    pos1 = position(route_t[0], route_t[4])
    pos2 = position(route_t[1], route_t[5])
    cpair = jnp.concatenate([jnp.stack([route_t[2], route_t[3]], axis=1).reshape(2 * T), jnp.zeros((2,), F32)])

    def wspec(shape):
        return pl.BlockSpec((None, None) + shape, lambda e, *pre: (layer, e, 0, 0))

    gscr = pltpu.VMEM((tm * SUBLANES, LANES), F32)
    abuf = pltpu.VMEM((tm, D_EXPERT), BF16)
    any_spec = pl.BlockSpec(memory_space=pl.ANY)
    rows_all = MOE_ROWS + tm
    act, rowmap = pl.pallas_call(
        _moe_up_kernel,
        grid_spec=pltpu.PrefetchScalarGridSpec(
            num_scalar_prefetch=4, grid=(N_EXPERTS,),
            in_specs=[any_spec, wspec((D, D_EXPERT)), wspec((D, D_EXPERT))],
            out_specs=[any_spec, pl.BlockSpec(memory_space=pltpu.SMEM)],
            scratch_shapes=[pltpu.VMEM((T * SUBLANES, LANES), F32), gscr, gscr, abuf, abuf,
                            pltpu.VMEM((D, D_EXPERT), BF16), pltpu.VMEM((D, D_EXPERT), BF16),
                            pltpu.SemaphoreType.DMA(()), pltpu.SemaphoreType.DMA((2,))]),
        out_shape=[jax.ShapeDtypeStruct((rows_all, D_EXPERT), BF16), jax.ShapeDtypeStruct((rows_all,), jnp.int32)],
        compiler_params=_cparams(1),
        name=f"moe_up{layer}",
    )(pos1, pos2, pads, einfo, tr, w_gate, w_up)

    return pl.pallas_call(
        _moe_down_kernel,
        grid_spec=pltpu.PrefetchScalarGridSpec(
            num_scalar_prefetch=3, grid=(N_EXPERTS,),
            in_specs=[any_spec, wspec((D_EXPERT, D))],
            out_specs=any_spec,
            scratch_shapes=[pltpu.VMEM((ACC_ROWS, LANES), F32), gscr, gscr, abuf, abuf,
                            pltpu.VMEM((D_EXPERT, D), BF16), pltpu.SemaphoreType.DMA((2,))]),
        out_shape=jax.ShapeDtypeStruct((T * SUBLANES, LANES), F32),
        compiler_params=_cparams(1),
        name=f"moe_down{layer}",
    )(rowmap, cpair, einfo, act, w_down)


FINAL_TM = 512


def _final_kernel(x_ref, moe_ref, g2_ref, nw_ref, oc_ref, ol_ref):
    x = x_ref[...] + g2_ref[0] * _tokmajor_to_std(moe_ref, FINAL_TM)
    y = _rms(x, nw_ref[...])
    is_ctx = pl.program_id(0) < T_CTX // FINAL_TM

    @pl.when(is_ctx)
    def _():
        oc_ref[...] = y

    @pl.when(jnp.logical_not(is_ctx))
    def _():
        ol_ref[...] = y


def _final(x, moe, mods, layer, norm_w):
    tm = FINAL_TM
    tile = pl.BlockSpec((tm, D), lambda i: (i, 0))
    return pl.pallas_call(
        _final_kernel,
        grid=(T // tm,),
        in_specs=[tile, pl.BlockSpec((tm * SUBLANES, LANES), lambda i: (i, 0)), _mod_spec(layer, 5, tm),
                  pl.BlockSpec((1, D), lambda i: (0, 0))],
        out_specs=list(_ctx_lat_specs(tm)),
        out_shape=[jax.ShapeDtypeStruct((T_CTX, D), F32), jax.ShapeDtypeStruct((T_LAT, D), F32)],
        compiler_params=_cparams(1),
        name="final_norm",
    )(x, moe, mods, norm_w.reshape(1, D))


def _pad_lanes(v):
    return jnp.pad(v.astype(F32), (0, LANES - v.shape[0])).reshape(1, LANES)


def kernel(x_prompt, x_sample, cache_k_attn, cache_v_attn, state_ssd_fwd, state_ssd_bwd, state_gla_fwd, state_gla_bwd, c, c_ctx, w_ada, b_ada, norm_mix_w, norm_ffn_w, w_in_even, conv_w, conv_b, dt_bias_fwd, dt_bias_bwd, a_log_fwd, a_log_bwd, d_skip, ssd_norm_w, attn_sink, w_out_even, w_in_odd, w_gk2_fwd, b_gk_fwd, w_gk2_bwd, b_gk_bwd, gla_norm_w, w_out_odd, w_router_group, b_router_group, w_router_expert, b_router_expert, w_gate_exp, w_up_exp, w_down_exp, final_norm_w):
    depth = w_ada.shape[0]
    assert depth == 2 and x_prompt.shape == (BATCH, SEQ, D) and x_sample.shape == (DEC_BATCH, DEC_SEQ, D)

    cond8 = jnp.concatenate([c_ctx[None, :], c, jnp.zeros((SUBLANES - 1 - DEC_BATCH, D), F32)], axis=0)
    mods = _adaln(cond8, w_ada, b_ada).reshape(depth * SUBLANES * 6, 1, D)
    xs0 = (x_prompt.reshape(T_CTX, D), x_sample.reshape(T_LAT, D))

    def router_params(i):
        wr = jnp.concatenate([w_router_group[i], w_router_expert[i],
                              jnp.zeros((D, LANES - N_GROUPS - N_EXPERTS), F32)], axis=1)
        return wr, _pad_lanes(jnp.concatenate([b_router_group[i], b_router_expert[i]]))

    n_zxbc = 2 * D_SSD + 2 * SSD_GROUPS * SSD_N
    n_dt = 2 * SSD_HEADS
    bc_tile, q_tiles = n_zxbc // PROJ_TN - 1, D // PROJ_TN
    rows0 = [k * PROJ_TN for k in range(bc_tile + 1)] + [n_zxbc + n_dt + k * PROJ_TN for k in range(q_tiles + 1)]

    def out_block0(j):
        return jnp.where(j == bc_tile, bc_tile + q_tiles, jnp.where((j > bc_tile) & (j <= bc_tile + q_tiles), j - 1, j))

    p0, p0dt = _modproj(xs0, mods, 0, norm_mix_w[0], jnp.transpose(w_in_even[0]), rows0, n_zxbc, out_block0)

    dtb = _pad_lanes(jnp.concatenate([dt_bias_fwd[0], dt_bias_bwd[0]]))
    alog = _pad_lanes(jnp.concatenate([a_log_fwd[0], a_log_bwd[0]]))
    dskip = jnp.repeat(d_skip[0], SSD_P).reshape(1, D_SSD)
    ssd_args = (conv_w[0], conv_b[0], dtb, alog, dskip, ssd_norm_w[0])
    y_ssd_c, ssd_f, ssd_b = _ssd(p0, p0dt, 0, BATCH, SEQ, *ssd_args)
    y_ssd_l, _, _ = _ssd(p0, p0dt, T_CTX, DEC_BATCH, DEC_SEQ, *ssd_args,
                         h0f=state_ssd_fwd[:, 0].reshape(DEC_BATCH, D_SSD, SSD_N),
                         h0b=state_ssd_bwd[:, 0].reshape(DEC_BATCH, D_SSD, SSD_N))
    sink = _pad_lanes(attn_sink[0])
    y_att_c, new_kt, new_vt = _ctx_attn(p0, sink)
    def cache_in(t):
        return jnp.transpose(t[:, 0], (0, 2, 3, 1)).reshape(DEC_BATCH, ATT_KV_DIM, PAST_LEN)

    y_att_l = _lat_attn(p0, cache_in(cache_k_attn), cache_in(cache_v_attn), sink)
    xmid0, tr0, route0, cnt0 = _outproj_route([y_ssd_c, y_ssd_l, y_att_c, y_att_l], w_out_even[0], xs0, mods, 0,
                                              norm_ffn_w[0], *router_params(0))
    moe0 = _moe(tr0, route0, cnt0, 0, w_gate_exp, w_up_exp, w_down_exp)

    dk_all = GLA_HEADS * GLA_DK
    n_qkvr = 2 * dk_all + 2 * GLA_HEADS * GLA_DV
    n_odd = w_in_odd.shape[2]
    p1, p1lr, x1 = _modproj(xmid0, mods, 1, norm_mix_w[1], jnp.transpose(w_in_odd[0]),
                            [k * PROJ_TN for k in range(n_qkvr // PROJ_TN)], n_odd - LANES, lambda j: j, moe=moe0)
    lr0 = LANES - 2 * GLA_LOWRANK
    w2f = jnp.zeros((LANES, dk_all), F32).at[lr0:lr0 + GLA_LOWRANK].set(w_gk2_fwd[0])
    w2b = jnp.zeros((LANES, dk_all), F32).at[lr0 + GLA_LOWRANK:].set(w_gk2_bwd[0])
    gla_args = (w2f, w2b, b_gk_fwd[0], b_gk_bwd[0], gla_norm_w[0])
    y_gla_c, gla_f, gla_b = _gla(p1, p1lr, 0, BATCH, SEQ, *gla_args)
    y_gla_l, _, _ = _gla(p1, p1lr, T_CTX, DEC_BATCH, DEC_SEQ, *gla_args,
                         s0f=state_gla_fwd[:, 0].reshape(DEC_BATCH, dk_all, GLA_DV),
                         s0b=state_gla_bwd[:, 0].reshape(DEC_BATCH, dk_all, GLA_DV))
    xmid1, tr1, route1, cnt1 = _outproj_route([y_gla_c, y_gla_l], w_out_odd[0], (x1,), mods, 1,
                                              norm_ffn_w[1], *router_params(1))
    moe1 = _moe(tr1, route1, cnt1, 1, w_gate_exp, w_up_exp, w_down_exp)
    y_c, y_l = _final(xmid1, moe1, mods, 1, final_norm_w)

    y_prompt = y_c.reshape(BATCH, SEQ, D)
    y_sample = y_l.reshape(DEC_BATCH, DEC_SEQ, D)
    def cache_out(t):
        return jnp.transpose(t.reshape(BATCH, 1, ATT_KV, ATT_HD, SEQ), (0, 1, 4, 2, 3))

    new_k, new_v = cache_out(new_kt), cache_out(new_vt)
    return (y_prompt, y_sample, new_k, new_v,
            ssd_f.reshape(BATCH, 1, SSD_HEADS, SSD_P, SSD_N), ssd_b.reshape(BATCH, 1, SSD_HEADS, SSD_P, SSD_N),
            gla_f.reshape(BATCH, 1, GLA_HEADS, GLA_DK, GLA_DV), gla_b.reshape(BATCH, 1, GLA_HEADS, GLA_DK, GLA_DV))
```

```python
import functools
import math

import numpy as np
import jax
import jax.numpy as jnp
from jax import lax
from jax.experimental import pallas as pl
from jax.experimental.pallas import tpu as pltpu

F32 = jnp.float32
BF16 = jnp.bfloat16

D = 1024
BATCH, SEQ = 16, 256
DEC_BATCH, DEC_SEQ = 2, 1024
PAST_LEN = 512
GRID_W = 64
EPS = 1e-6
T_CTX = BATCH * SEQ
T_LAT = DEC_BATCH * DEC_SEQ
T = T_CTX + T_LAT

SSD_HEADS, SSD_P, SSD_N, SSD_GROUPS = 16, 64, 128, 2
SSD_CONV = 5
SSD_L = 128
D_SSD = SSD_HEADS * SSD_P
HEADS_PER_GROUP = SSD_HEADS // SSD_GROUPS
GROUP_W = HEADS_PER_GROUP * SSD_P

ATT_HEADS, ATT_KV, ATT_HD = 16, 4, 64
ATT_KV_DIM = ATT_KV * ATT_HD
WINDOW = 128
ATT_BLOCK = 128
ATT_SCALE = ATT_HD ** -0.5
ROPE_BASE = 10000.0

GLA_HEADS, GLA_DK, GLA_DV = 4, 128, 256
GLA_C = 64
GLA_GATE_NORM = 16.0
GLA_LOWRANK = 16

N_GROUPS, EXP_PER_GROUP = 4, 4
N_EXPERTS = 16
D_EXPERT = 512

LANES = 128
SUBLANES = 8
VMEM_LIMIT = 56 * 1024 * 1024

P0_Z, P0_X, P0_Q, P0_BC, P0_K, P0_V = 0, 1024, 2048, 3072, 3584, 3840
P0_W = 4096
P1_Q, P1_K, P1_V, P1_R = 0, 512, 1024, 2048
P1_W = 3072

MOE_TM = 256
MOE_TILES = (2 * T) // MOE_TM + N_EXPERTS
MOE_ROWS = MOE_TILES * MOE_TM
MOE_RING = 4
ACC_ROWS = (T + SUBLANES) * SUBLANES


def _cparams(n_axes, vmem=VMEM_LIMIT):
    return pltpu.CompilerParams(dimension_semantics=("arbitrary",) * n_axes, vmem_limit_bytes=vmem)


def _silu(x):
    return x / (1.0 + jnp.exp(-x))


def _softplus(x):
    return jnp.maximum(x, 0.0) + jnp.log(1.0 + jnp.exp(-jnp.abs(x)))


def _mm(a, b):
    return jnp.dot(a.astype(BF16), b.astype(BF16), preferred_element_type=F32)


def _mm_nt(a, b):
    return lax.dot_general(a.astype(BF16), b.astype(BF16), (((1,), (1,)), ((), ())),
                           preferred_element_type=F32)


def _mm_tn(a, b):
    return lax.dot_general(a.astype(BF16), b.astype(BF16), (((0,), (0,)), ((), ())),
                           preferred_element_type=F32)


def _rms(x, w):
    return x * lax.rsqrt(jnp.mean(x * x, axis=-1, keepdims=True) + EPS) * w


def _cumsum_rows(x, n):
    row = lax.broadcasted_iota(jnp.int32, x.shape, 0)
    s = 1
    while s < n:
        x = x + jnp.where(row >= s, pltpu.roll(x, s, 0), 0.0)
        s *= 2
    return x


def _mod_row(tok0):
    return jnp.where(tok0 < T_CTX, 0, 1 + (tok0 - T_CTX) // DEC_SEQ)


ADA_TN = 1536


def _adaln_kernel(c_ref, w_ref, b_ref, o_ref):
    s = _silu(c_ref[...])
    o_ref[0] = _mm(s, w_ref[0]) + b_ref[0]


def _adaln(cond8, w_ada, b_ada):
    depth = w_ada.shape[0]
    return pl.pallas_call(
        _adaln_kernel,
        grid=(depth, 6 * D // ADA_TN),
        in_specs=[
            pl.BlockSpec((SUBLANES, D), lambda l, j: (0, 0)),
            pl.BlockSpec((1, D, ADA_TN), lambda l, j: (l, 0, j)),
            pl.BlockSpec((1, 1, ADA_TN), lambda l, j: (l, 0, j)),
        ],
        out_specs=pl.BlockSpec((1, SUBLANES, ADA_TN), lambda l, j: (l, 0, j)),
        out_shape=jax.ShapeDtypeStruct((depth, SUBLANES, 6 * D), F32),
        compiler_params=_cparams(2),
        name="adaln",
    )(cond8, w_ada, b_ada.reshape(depth, 1, 6 * D))


def _mod_spec(layer, chunk, tm, tile_of=lambda i, *_: i):
    return pl.BlockSpec((1, 1, D), lambda *g: ((layer * SUBLANES + _mod_row(tile_of(*g) * tm)) * 6 + chunk, 0, 0))


def _tokmajor_to_std(ref, tm):
    return jnp.concatenate([ref[pl.ds(k, tm, stride=SUBLANES), :] for k in range(D // LANES)], axis=1)


PROJ_TM = 1024
PROJ_TN = 512


def _ctx_lat_specs(tm, width=D):
    n_ctx = T_CTX // tm
    return (pl.BlockSpec((tm, width), lambda i, *_: (jnp.minimum(i, n_ctx - 1), 0)),
            pl.BlockSpec((tm, width), lambda i, *_: (jnp.maximum(i - n_ctx, 0), 0)))


def _modproj_kernel(*refs, dual_x):
    it = iter(refs)
    if dual_x:
        xc_ref, xl_ref = next(it), next(it)
    else:
        x_ref, moe_ref, g2_ref = next(it), next(it), next(it)
    sh_ref, sc_ref, nw_ref, w_ref, ws_ref, o_ref, os_ref = (next(it) for _ in range(7))
    xo_ref = None if dual_x else next(it)
    h_all, w_bf = next(it), next(it)
    j, i = pl.program_id(0), pl.program_id(1)
    tm = PROJ_TM
    rows = pl.ds(pl.multiple_of(i * tm, tm), tm)

    @pl.when(j == 0)
    def _():
        if dual_x:
            x = jnp.where(i < T_CTX // tm, xc_ref[...], xl_ref[...])
        else:
            x = x_ref[...] + g2_ref[0] * _tokmajor_to_std(moe_ref, tm)
            xo_ref[...] = x
        h = (_rms(x, nw_ref[...]) * (1.0 + sc_ref[0]) + sh_ref[0]).astype(BF16)
        h_all[rows, :] = h
        os_ref[...] = _mm_nt(h, ws_ref[...])

    @pl.when(i == 0)
    def _():
        w_bf[...] = w_ref[...].astype(BF16)

    o_ref[...] = _mm_nt(h_all[rows, :], w_bf[...]).astype(o_ref.dtype)


def _modproj(xs, mods, layer, norm_w, wt, tile_rows, small_row, out_block, moe=None):
    tm, tn = PROJ_TM, PROJ_TN
    dual_x = moe is None
    n_tiles = len(tile_rows)
    n_i, n_ctx = T // tm, T_CTX // tm

    def w_row(j, i):
        r = jnp.int32(tile_rows[0])
        for k in range(1, n_tiles):
            r = jnp.where(j == k, tile_rows[k], r)
        return pl.multiple_of(r, SUBLANES), 0

    def tok(j, i):
        return jnp.where(j == 0, i, n_i - 1)

    tile = pl.BlockSpec((tm, D), lambda j, i: (tok(j, i), 0))
    if dual_x:
        in_specs = [pl.BlockSpec((tm, D), lambda j, i: (jnp.minimum(tok(j, i), n_ctx - 1), 0)),
                    pl.BlockSpec((tm, D), lambda j, i: (jnp.maximum(tok(j, i) - n_ctx, 0), 0))]
        args = list(xs)
    else:
        in_specs = [tile, pl.BlockSpec((tm * SUBLANES, LANES), lambda j, i: (tok(j, i), 0)),
                    _mod_spec(layer - 1, 5, tm, tok)]
        args = [xs, moe, mods]
    in_specs += [_mod_spec(layer, 0, tm, tok), _mod_spec(layer, 1, tm, tok), pl.BlockSpec((1, D), lambda j, i: (0, 0)),
                 pl.BlockSpec((pl.Element(tn), pl.Element(D)), w_row),
                 pl.BlockSpec((pl.Element(LANES), pl.Element(D)), lambda j, i: (small_row, 0))]
    args += [mods, mods, norm_w.reshape(1, D), wt, wt]
    out_specs = [pl.BlockSpec((tm, tn), lambda j, i: (i, out_block(j))),
                 pl.BlockSpec((tm, LANES), lambda j, i: (tok(j, i), 0))]
    out_shape = [jax.ShapeDtypeStruct((T, n_tiles * tn), BF16), jax.ShapeDtypeStruct((T, LANES), F32)]
    if not dual_x:
        out_specs.append(tile)
        out_shape.append(jax.ShapeDtypeStruct((T, D), F32))
    return pl.pallas_call(
        functools.partial(_modproj_kernel, dual_x=dual_x),
        grid=(n_tiles, n_i), in_specs=in_specs, out_specs=out_specs, out_shape=out_shape,
        scratch_shapes=[pltpu.VMEM((T, D), BF16), pltpu.VMEM((tn, D), BF16)],
        compiler_params=_cparams(2),
        name=f"modproj{layer}",
    )(*args)


def _expand_heads(v, off):
    lo = lax.broadcasted_iota(jnp.int32, (v.shape[0], LANES), 1) < SSD_P
    tiles = []
    for q in range(SSD_HEADS // 2):
        a = jnp.broadcast_to(v[:, off + 2 * q:off + 2 * q + 1], (v.shape[0], LANES))
        b = jnp.broadcast_to(v[:, off + 2 * q + 1:off + 2 * q + 2], (v.shape[0], LANES))
        tiles.append(jnp.where(lo, a, b))
    return jnp.concatenate(tiles, axis=1)


def _ssd_kernel(*refs, seq, has_h0):
    if has_h0:
        (z_ref, x_ref, bc_ref, dt_ref, cwx_ref, cwbc_ref, cbx_ref, cbbc_ref, dtb_ref, alog_ref, dsk_ref,
         nw_ref, h0f_ref, h0b_ref, y_ref, sf_ref, sb_ref,
         xpad, bcpad, xc, bcc, a_scr, dt_scr, yacc, hf, hb) = refs
    else:
        (z_ref, x_ref, bc_ref, dt_ref, cwx_ref, cwbc_ref, cbx_ref, cbbc_ref, dtb_ref, alog_ref, dsk_ref,
         nw_ref, y_ref, sf_ref, sb_ref,
         xpad, bcpad, xc, bcc, a_scr, dt_scr, yacc, hf, hb) = refs
    L = SSD_L
    nc = seq // L
    pad = SUBLANES
    half = SSD_CONV // 2

    for buf, src, cw, cb, dst in ((xpad, x_ref, cwx_ref, cbx_ref, xc), (bcpad, bc_ref, cwbc_ref, cbbc_ref, bcc)):
        width = buf.shape[1]
        buf[0:pad, :] = jnp.zeros((pad, width), F32)
        buf[pad + seq:2 * pad + seq, :] = jnp.zeros((pad, width), F32)
        buf[pad:pad + seq, :] = src[...].astype(F32)
        for blk in range(nc):
            acc = jnp.broadcast_to(cb[...], (L, width))
            for j in range(SSD_CONV):
                r0 = pad - half + j + blk * L
                acc = acc + cw[j:j + 1, :] * buf[r0:r0 + L, :]
            dst[blk * L:(blk + 1) * L, :] = _silu(acc)

    lane = lax.broadcasted_iota(jnp.int32, (seq, LANES), 1)
    dts = jnp.where(lane < 2 * SSD_HEADS, _softplus(dt_ref[...] + dtb_ref[...]), 0.0)
    dt_scr[...] = dts
    a_scr[...] = dts * (-jnp.exp(alog_ref[...]))

    if has_h0:
        hf[...] = h0f_ref[...]
        hb[...] = h0b_ref[...]
    else:
        hf[...] = jnp.zeros(hf.shape, F32)
        hb[...] = jnp.zeros(hb.shape, F32)

    row = lax.broadcasted_iota(jnp.int32, (L, L), 0)
    col = lax.broadcasted_iota(jnp.int32, (L, L), 1)
    lane_l = lax.broadcasted_iota(jnp.int32, (L, LANES), 1)
    lo_half = lane_l < SSD_P

    def chunk(c, fwd, h_scr):
        off = 0 if fwd else SSD_HEADS
        r0 = pl.multiple_of(c * L, L)
        a = a_scr[pl.ds(r0, L), :]
        dt = dt_scr[pl.ds(r0, L), :]
        cs = _cumsum_rows(a, L)
        total = cs[L - 1:L, :]
        if fwd:
            u = cs
            rvec = jnp.exp(cs)
            ed = jnp.exp(total - cs) * dt
            keep = col <= row
        else:
            ex = cs - a
            u = -ex
            rvec = jnp.exp(total - ex)
            ed = jnp.exp(ex) * dt
            keep = col >= row
        ut = jnp.transpose(u)
        dtt = jnp.transpose(dt)
        tcol = jnp.transpose(jnp.broadcast_to(total, (L, LANES)))[:, 0:1]
        rexp = _expand_heads(rvec, off)
        edexp = _expand_heads(ed, off)
        x = xc[pl.ds(r0, L), :]
        bc = bcc[pl.ds(r0, L), :]
        outs = []
        for g in range(SSD_GROUPS):
            bg = bc[:, g * SSD_N:(g + 1) * SSD_N]
            cg = bc[:, SSD_GROUPS * SSD_N + g * SSD_N:SSD_GROUPS * SSD_N + (g + 1) * SSD_N]
            cbm = _mm_nt(cg, bg)
            hg = h_scr[g * GROUP_W:(g + 1) * GROUP_W, :]
            xg = x[:, g * GROUP_W:(g + 1) * GROUP_W]
            y_off = _mm_nt(cg, hg) * rexp[:, g * GROUP_W:(g + 1) * GROUP_W]
            tiles = []
            for p in range(HEADS_PER_GROUP // 2):
                xt = xg[:, p * LANES:(p + 1) * LANES]
                acc = None
                for s in range(2):
                    h = off + g * HEADS_PER_GROUP + 2 * p + s
                    seg = u[:, h:h + 1] - ut[h:h + 1, :]
                    m = cbm * jnp.exp(jnp.where(keep, seg, -jnp.inf)) * dtt[h:h + 1, :]
                    xm = jnp.where(lo_half if s == 0 else jnp.logical_not(lo_half), xt, 0.0)
                    d = _mm(m, xm)
                    acc = d if acc is None else acc + d
                tiles.append(acc)
            outs.append(y_off + jnp.concatenate(tiles, axis=1))
            decs = []
            for hh in range(HEADS_PER_GROUP):
                h = off + g * HEADS_PER_GROUP + hh
                decs.append(jnp.broadcast_to(jnp.exp(tcol[h:h + 1, :]), (SSD_P, SSD_N)))
            dec = jnp.concatenate(decs, axis=0)
            h_scr[g * GROUP_W:(g + 1) * GROUP_W, :] = dec * hg + _mm_tn(xg * edexp[:, g * GROUP_W:(g + 1) * GROUP_W], bg)
        return r0, x, jnp.concatenate(outs, axis=1)

    def fwd_body(c, carry):
        r0, _, y = chunk(c, True, hf)
        yacc[pl.ds(r0, L), :] = y
        return carry

    lax.fori_loop(0, nc, fwd_body, 0)

    def bwd_body(i, carry):
        c = nc - 1 - i
        r0, x, y = chunk(c, False, hb)
        y = yacc[pl.ds(r0, L), :] + y + dsk_ref[...] * x
        y = y * _silu(z_ref[pl.ds(r0, L), :].astype(F32))
        y_ref[pl.ds(r0, L), :] = _rms(y, nw_ref[...]).astype(y_ref.dtype)
        return carry

    lax.fori_loop(0, nc, bwd_body, 0)
    sf_ref[...] = hf[...]
    sb_ref[...] = hb[...]


def _ssd(p0, p0dt, tok0, nseq, seq, cw, cb, dtb, alog, dskip, nw, h0f=None, h0b=None):
    has_h0 = h0f is not None
    b0 = tok0 // seq

    def cols(width, start):
        return pl.BlockSpec((seq, width), lambda s: (b0 + s, start // width))

    def full(shape):
        return pl.BlockSpec(shape, lambda s: (0,) * len(shape))

    in_specs = [cols(D_SSD, P0_Z), cols(D_SSD, P0_X), cols(512, P0_BC), cols(LANES, 0),
                full((SSD_CONV, D_SSD)), full((SSD_CONV, 512)), full((1, D_SSD)), full((1, 512)),
                full((1, LANES)), full((1, LANES)), full((1, D_SSD)), full((1, D_SSD))]
    args = [p0, p0, p0, p0dt, cw[:, :D_SSD], cw[:, D_SSD:], cb[:D_SSD].reshape(1, -1), cb[D_SSD:].reshape(1, -1),
            dtb, alog, dskip, nw.reshape(1, -1)]
    st_spec = pl.BlockSpec((None, D_SSD, SSD_N), lambda s: (s, 0, 0))
    if has_h0:
        in_specs += [st_spec, st_spec]
        args += [h0f, h0b]
    st_shape = jax.ShapeDtypeStruct((nseq, D_SSD, SSD_N), F32)
    return pl.pallas_call(
        functools.partial(_ssd_kernel, seq=seq, has_h0=has_h0),
        grid=(nseq,), in_specs=in_specs,
        out_specs=[pl.BlockSpec((seq, D_SSD), lambda s: (s, 0)), st_spec, st_spec],
        out_shape=[jax.ShapeDtypeStruct((nseq * seq, D_SSD), BF16), st_shape, st_shape],
        scratch_shapes=[pltpu.VMEM((seq + 2 * SUBLANES, D_SSD), F32), pltpu.VMEM((seq + 2 * SUBLANES, 512), F32),
                        pltpu.VMEM((seq, D_SSD), F32), pltpu.VMEM((seq, 512), F32),
                        pltpu.VMEM((seq, LANES), F32), pltpu.VMEM((seq, LANES), F32),
                        pltpu.VMEM((seq, D_SSD), F32),
                        pltpu.VMEM((D_SSD, SSD_N), F32), pltpu.VMEM((D_SSD, SSD_N), F32)],
        compiler_params=_cparams(1),
        name=f"ssd{seq}",
    )(*args)


def _place_halves(tile, kv_in_high):
    lo = lax.broadcasted_iota(jnp.int32, tile.shape, 1) < ATT_HD
    swapped = pltpu.roll(tile, ATT_HD, 1)
    if kv_in_high:
        return jnp.where(lo, swapped, 0.0), jnp.where(lo, 0.0, tile)
    return jnp.where(lo, tile, 0.0), jnp.where(lo, 0.0, swapped)


def _place_rows(vt, kv_in_high):
    head = vt[ATT_HD:, :] if kv_in_high else vt[:ATT_HD, :]
    z = jnp.zeros_like(head)
    return jnp.concatenate([head, z], axis=0), jnp.concatenate([z, head], axis=0)


LOG2E = 1.4426950408889634
SCORE_SCALE = ATT_SCALE * LOG2E


def _sink_attend_t(score_parts, value_parts, sink2):
    m = sink2
    for s in score_parts:
        m = jnp.maximum(m, jnp.max(s, axis=0, keepdims=True))
    denom = jnp.exp2(sink2 - m)
    out = None
    for s, v in zip(score_parts, value_parts):
        p = jnp.exp2(s - m)
        denom = denom + jnp.sum(p, axis=0, keepdims=True)
        o = _mm(v, p)
        out = o if out is None else out + o
    return out * (1.0 / denom)


def _attn_schedule(n, scores, attend):
    scores(0)
    for j in range(n):
        if j + 1 < n:
            scores(j + 1)
        attend(j)


def _ctx_attn_kernel(q_ref, k_ref, v_ref, sink_ref, o_ref, kt_ref, vt_ref, s_a, s_b):
    sink2 = sink_ref[...] * LOG2E
    bufs = (s_a, s_b)
    half = SEQ // 2
    for t in range(ATT_KV_DIM // LANES):
        cols = slice(t * LANES, (t + 1) * LANES)
        kt_ref[cols, :] = jnp.transpose(k_ref[:, cols].astype(F32))
        vt_ref[cols, :] = jnp.transpose(v_ref[:, cols].astype(F32))

    def kv_tile(ref, j):
        return ref[:, (j // 2) * LANES:(j // 2 + 1) * LANES].astype(F32), (j % 2 == 1)

    def scores(j):
        k_lo, k_hi = _place_halves(*kv_tile(k_ref, j))
        qst = jnp.concatenate([q_ref[:, qt * LANES:(qt + 1) * LANES] for qt in (2 * j, 2 * j + 1)], axis=0)
        bufs[j % 2][...] = _mm_nt(jnp.concatenate([k_lo, k_hi], axis=0), qst) * SCORE_SCALE

    def attend(j):
        src = bufs[j % 2]
        v, high = kv_tile(v_ref, j)
        vts = _place_rows(jnp.transpose(v), high)
        for ql in range(2):
            qt = 2 * j + ql
            for qh in range(2):
                cols = slice(ql * SEQ + qh * half, ql * SEQ + (qh + 1) * half)
                acc = None
                for s, vv in enumerate(vts):
                    o = _sink_attend_t([src[s * SEQ:(s + 1) * SEQ, cols]], [vv], sink2[:, 2 * qt + s:2 * qt + s + 1])
                    acc = o if acc is None else acc + o
                o_ref[qh * half:(qh + 1) * half, qt * LANES:(qt + 1) * LANES] = jnp.transpose(acc).astype(o_ref.dtype)

    _attn_schedule(ATT_KV, scores, attend)


def _ctx_attn(p0, sink):
    def cols(width, start):
        return pl.BlockSpec((SEQ, width), lambda b: (b, start // width))

    sbuf = pltpu.VMEM((2 * SEQ, 2 * SEQ), F32)
    return pl.pallas_call(
        _ctx_attn_kernel,
        grid=(BATCH,),
        in_specs=[cols(D, P0_Q), cols(ATT_KV_DIM, P0_K), cols(ATT_KV_DIM, P0_V),
                  pl.BlockSpec((1, LANES), lambda b: (0, 0))],
        out_specs=[pl.BlockSpec((SEQ, D), lambda b: (b, 0))]
        + [pl.BlockSpec((None, ATT_KV_DIM, SEQ), lambda b: (b, 0, 0))] * 2,
        out_shape=[jax.ShapeDtypeStruct((T_CTX, D), BF16)] + [jax.ShapeDtypeStruct((BATCH, ATT_KV_DIM, SEQ), F32)] * 2,
        scratch_shapes=[sbuf, sbuf],
        compiler_params=_cparams(1),
        name="ctx_attn",
    )(p0, p0, p0, sink)


def _rope_tables():
    quarter = ATT_HD // 4
    t = np.arange(DEC_SEQ)
    lane = np.arange(LANES)
    inv = ROPE_BASE ** (-(lane % quarter).astype(np.float64) / quarter)
    pos = np.where(((lane % ATT_HD) < ATT_HD // 2)[None, :], (t // GRID_W)[:, None], (t % GRID_W)[:, None])
    ang = pos * inv[None, :]
    first = ((lane % (2 * quarter)) < quarter)[None, :]
    cos, sin = np.cos(ang), np.sin(ang)
    return (jnp.asarray(cos, F32), jnp.asarray(np.where(first, -sin, 0.0), F32),
            jnp.asarray(np.where(first, 0.0, sin), F32))


def _rope(x, cos, sa, sb):
    quarter = ATT_HD // 4
    return x * cos + pltpu.roll(x, LANES - quarter, 1) * sa + pltpu.roll(x, quarter, 1) * sb


def _lat_attn_kernel(q_ref, kp_ref, kc_ref, kn_ref, vp_ref, vc_ref, vn_ref, ck_ref, cv_ref,
                     cos_ref, sa_ref, sb_ref, sink_ref, o_ref, c_a, c_b, w_a, w_b):
    blk = pl.program_id(1)
    nb = pl.num_programs(1)
    B = ATT_BLOCK
    sink2 = sink_ref[...] * LOG2E
    cbufs, wbufs = (c_a, c_b), (w_a, w_b)

    def tables(b):
        r0 = pl.multiple_of(b * B, B)
        return cos_ref[pl.ds(r0, B), :], sa_ref[pl.ds(r0, B), :], sb_ref[pl.ds(r0, B), :]

    tq = tables(blk)
    tk = [tables(jnp.maximum(blk - 1, 0)), tq, tables(jnp.minimum(blk + 1, nb - 1))]
    kabs = (blk - 1) * B + lax.broadcasted_iota(jnp.int32, (3 * B, B), 0)
    qpos = blk * B + lax.broadcasted_iota(jnp.int32, (3 * B, B), 1)
    ok = (jnp.abs(qpos - kabs) <= WINDOW) & (kabs >= 0) & (kabs < nb * B)
    ok = jnp.concatenate([ok, ok], axis=1)
    ok = jnp.concatenate([ok, ok], axis=0)

    def scores(j):
        high = (j % 2 == 1)
        sl = slice((j // 2) * LANES, (j // 2 + 1) * LANES)
        kw = jnp.concatenate([_rope(r[:, sl].astype(F32), *tb) for r, tb in zip((kp_ref, kc_ref, kn_ref), tk)], axis=0)
        qs = [q_ref[:, qt * LANES:(qt + 1) * LANES].astype(F32) for qt in (2 * j, 2 * j + 1)]
        q_plain = jnp.concatenate(qs, axis=0)
        q_rope = jnp.concatenate([_rope(q, *tq) for q in qs], axis=0)
        ck = jnp.transpose(ck_ref[sl, :])
        cbufs[j % 2][...] = _mm_nt(jnp.concatenate(_place_halves(ck, high), axis=0), q_plain) * SCORE_SCALE
        win = _mm_nt(jnp.concatenate(_place_halves(kw, high), axis=0), q_rope) * SCORE_SCALE
        wbufs[j % 2][...] = jnp.where(ok, win, -jnp.inf)

    def attend(j):
        high = (j % 2 == 1)
        sl = slice((j // 2) * LANES, (j // 2 + 1) * LANES)
        vw = jnp.concatenate([jnp.transpose(r[:, sl].astype(F32)) for r in (vp_ref, vc_ref, vn_ref)], axis=1)
        vts = _place_rows(vw, high)
        cvts = _place_rows(cv_ref[sl, :], high)
        csrc, wsrc = cbufs[j % 2], wbufs[j % 2]
        for ql in range(2):
            qt = 2 * j + ql
            cols = slice(ql * B, (ql + 1) * B)
            acc = None
            for s in range(2):
                parts = [csrc[s * PAST_LEN:(s + 1) * PAST_LEN, cols], wsrc[s * 3 * B:(s + 1) * 3 * B, cols]]
                o = _sink_attend_t(parts, [cvts[s], vts[s]], sink2[:, 2 * qt + s:2 * qt + s + 1])
                acc = o if acc is None else acc + o
            o_ref[:, qt * LANES:(qt + 1) * LANES] = jnp.transpose(acc).astype(o_ref.dtype)

    _attn_schedule(ATT_KV, scores, attend)


def _lat_attn(p0, ck, cv, sink):
    nb = DEC_SEQ // ATT_BLOCK
    base = T_CTX // ATT_BLOCK

    def kv(start, shift):
        return pl.BlockSpec((ATT_BLOCK, ATT_KV_DIM),
                            lambda b, i: (base + b * nb + jnp.clip(i + shift, 0, nb - 1), start // ATT_KV_DIM))

    def full(shape):
        return pl.BlockSpec(shape, lambda b, i: (0,) * len(shape))

    cache = pl.BlockSpec((None, ATT_KV_DIM, PAST_LEN), lambda b, i: (b, 0, 0))
    cos, sa, sb = _rope_tables()
    return pl.pallas_call(
        _lat_attn_kernel,
        grid=(DEC_BATCH, nb),
        in_specs=[pl.BlockSpec((ATT_BLOCK, D), lambda b, i: (base + b * nb + i, P0_Q // D)),
                  kv(P0_K, -1), kv(P0_K, 0), kv(P0_K, 1), kv(P0_V, -1), kv(P0_V, 0), kv(P0_V, 1),
                  cache, cache, full((DEC_SEQ, LANES)), full((DEC_SEQ, LANES)), full((DEC_SEQ, LANES)),
                  full((1, LANES))],
        out_specs=pl.BlockSpec((ATT_BLOCK, D), lambda b, i: (b * nb + i, 0)),
        out_shape=jax.ShapeDtypeStruct((T_LAT, D), BF16),
        scratch_shapes=[pltpu.VMEM((2 * PAST_LEN, 2 * ATT_BLOCK), F32)] * 2
        + [pltpu.VMEM((2 * 3 * ATT_BLOCK, 2 * ATT_BLOCK), F32)] * 2,
        compiler_params=_cparams(2),
        name="lat_attn",
    )(p0, p0, p0, p0, p0, p0, p0, ck, cv, cos, sa, sb, sink)


def _log_sigmoid(x):
    return jnp.minimum(x, 0.0) - jnp.log(1.0 + jnp.exp(-jnp.abs(x)))


def _gla_kernel(*refs, seq, has_s0):
    if has_s0:
        (q_ref, k_ref, v_ref, r_ref, lr_ref, w2f_ref, w2b_ref, bf_ref, bb_ref, nw_ref, s0f_ref, s0b_ref,
         y_ref, sf_ref, sb_ref, gf, gb, yf, yb, stf, stb) = refs
    else:
        (q_ref, k_ref, v_ref, r_ref, lr_ref, w2f_ref, w2b_ref, bf_ref, bb_ref, nw_ref,
         y_ref, sf_ref, sb_ref, gf, gb, yf, yb, stf, stb) = refs
    C = GLA_C
    nc = seq // C
    lr = lr_ref[...]
    gf[...] = _log_sigmoid(_mm(lr, w2f_ref[...]) + bf_ref[...]) / GLA_GATE_NORM
    gb[...] = _log_sigmoid(_mm(lr, w2b_ref[...]) + bb_ref[...]) / GLA_GATE_NORM
    for h in range(GLA_HEADS):
        rows = slice(h * GLA_DV, (h + 1) * GLA_DV)
        if has_s0:
            stf[rows, :] = jnp.transpose(s0f_ref[h * GLA_DK:(h + 1) * GLA_DK, :])
            stb[rows, :] = jnp.transpose(s0b_ref[h * GLA_DK:(h + 1) * GLA_DK, :])
        else:
            stf[rows, :] = jnp.zeros((GLA_DV, GLA_DK), F32)
            stb[rows, :] = jnp.zeros((GLA_DV, GLA_DK), F32)

    row = lax.broadcasted_iota(jnp.int32, (C, C), 0)
    col = lax.broadcasted_iota(jnp.int32, (C, C), 1)
    qscale = GLA_DK ** -0.5

    def chunk(c, fwd):
        g_scr, y_scr, st = (gf, yf, stf) if fwd else (gb, yb, stb)
        r0 = pl.multiple_of(c * C, C)
        g = g_scr[pl.ds(r0, C), :]
        cs = _cumsum_rows(g, C)
        total = cs[C - 1:C, :]
        q = q_ref[pl.ds(r0, C), :].astype(F32) * qscale
        k = k_ref[pl.ds(r0, C), :].astype(F32)
        v = v_ref[pl.ds(r0, C), :]
        if fwd:
            qs, ks, ke = q * jnp.exp(cs), k * jnp.exp(-cs), k * jnp.exp(total - cs)
            keep = col <= row
        else:
            ex = cs - g
            qs, ks, ke = q * jnp.exp(total - ex), k * jnp.exp(ex - total), k * jnp.exp(ex)
            keep = col >= row
        dec = jnp.exp(total)
        for h in range(GLA_HEADS):
            kc = slice(h * GLA_DK, (h + 1) * GLA_DK)
            vc = slice(h * GLA_DV, (h + 1) * GLA_DV)
            s_t = st[vc, :]
            att = jnp.where(keep, _mm_nt(qs[:, kc], ks[:, kc]), 0.0)
            y_scr[pl.ds(r0, C), vc] = _mm(att, v[:, vc]) + _mm_nt(qs[:, kc], s_t)
            st[vc, :] = dec[:, kc] * s_t + _mm_tn(v[:, vc], ke[:, kc])

    def body(i, carry):
        chunk(i, True)
        chunk(nc - 1 - i, False)
        return carry

    lax.fori_loop(0, nc, body, 0)

    nw = nw_ref[...]
    for blk in range(seq // LANES):
        rs = slice(blk * LANES, (blk + 1) * LANES)
        y = yf[rs, :] + yb[rs, :]
        gate = _silu(r_ref[rs, :].astype(F32))
        for h in range(GLA_HEADS):
            vc = slice(h * GLA_DV, (h + 1) * GLA_DV)
            y_ref[rs, vc] = (_rms(y[:, vc], nw) * gate[:, vc]).astype(y_ref.dtype)
    for h in range(GLA_HEADS):
        rows = slice(h * GLA_DV, (h + 1) * GLA_DV)
        sf_ref[h * GLA_DK:(h + 1) * GLA_DK, :] = jnp.transpose(stf[rows, :])
        sb_ref[h * GLA_DK:(h + 1) * GLA_DK, :] = jnp.transpose(stb[rows, :])


def _gla(p1, p1lr, tok0, nseq, seq, w2f, w2b, bgf, bgb, nw, s0f=None, s0b=None):
    has_s0 = s0f is not None
    b0 = tok0 // seq
    dk_all, dv_all = GLA_HEADS * GLA_DK, GLA_HEADS * GLA_DV

    def cols(width, start):
        return pl.BlockSpec((seq, width), lambda s: (b0 + s, start // width))

    def full(shape):
        return pl.BlockSpec(shape, lambda s: (0,) * len(shape))

    in_specs = [cols(dk_all, P1_Q), cols(dk_all, P1_K), cols(dv_all, P1_V), cols(dv_all, P1_R), cols(LANES, 0),
                full((LANES, dk_all)), full((LANES, dk_all)), full((1, dk_all)), full((1, dk_all)), full((1, GLA_DV))]
    args = [p1, p1, p1, p1, p1lr, w2f, w2b, bgf.reshape(1, -1), bgb.reshape(1, -1), nw.reshape(1, -1)]
    st_spec = pl.BlockSpec((None, dk_all, GLA_DV), lambda s: (s, 0, 0))
    if has_s0:
        in_specs += [st_spec, st_spec]
        args += [s0f, s0b]
    st_shape = jax.ShapeDtypeStruct((nseq, dk_all, GLA_DV), F32)
    return pl.pallas_call(
        functools.partial(_gla_kernel, seq=seq, has_s0=has_s0),
        grid=(nseq,), in_specs=in_specs,
        out_specs=[pl.BlockSpec((seq, dv_all), lambda s: (s, 0)), st_spec, st_spec],
        out_shape=[jax.ShapeDtypeStruct((nseq * seq, dv_all), BF16), st_shape, st_shape],
        scratch_shapes=[pltpu.VMEM((seq, dk_all), F32), pltpu.VMEM((seq, dk_all), F32),
                        pltpu.VMEM((seq, dv_all), F32), pltpu.VMEM((seq, dv_all), F32),
                        pltpu.VMEM((dv_all, GLA_DK), F32), pltpu.VMEM((dv_all, GLA_DK), F32)],
        compiler_params=_cparams(1),
        name=f"gla{seq}",
    )(*args)


ROUTE_TM = 512
ROUTE_SUB = 256
ROUTE_ROWS = 32


def _split_bf16(x):
    hi = x.astype(BF16)
    return hi, (x - hi.astype(F32)).astype(BF16)


def _outproj_kernel(*refs, n_in, dual_x):
    y_refs = refs[:2 * n_in]
    n_x = 2 if dual_x else 1
    x_refs = refs[2 * n_in + 1:2 * n_in + 1 + n_x]
    w_ref = refs[2 * n_in]
    (g1_ref, sh_ref, sc_ref, nw_ref, wr_ref, br_ref,
     xo_ref, tr_ref, route_ref, cnt_ref, w_scr, wr_hl, carry) = refs[2 * n_in + 1 + n_x:]
    is_ctx = pl.program_id(0) < T_CTX // ROUTE_TM

    @pl.when(pl.program_id(0) == 0)
    def _():
        w_scr[...] = w_ref[...].astype(BF16)
        hi, lo = _split_bf16(jnp.transpose(wr_ref[...]))
        wr_hl[0:LANES, :] = hi
        wr_hl[LANES:2 * LANES, :] = lo
        carry[...] = jnp.zeros(carry.shape, F32)

    for sub in range(ROUTE_TM // ROUTE_SUB):
        _outproj_subtile(sub, is_ctx, y_refs, x_refs, dual_x, n_in, g1_ref, sh_ref, sc_ref, nw_ref, br_ref,
                         xo_ref, tr_ref, route_ref, cnt_ref, w_scr, wr_hl, carry)


def _outproj_subtile(sub, is_ctx, y_refs, x_refs, dual_x, n_in, g1_ref, sh_ref, sc_ref, nw_ref, br_ref,
                     xo_ref, tr_ref, route_ref, cnt_ref, w_scr, wr_hl, carry):
    tm = ROUTE_SUB
    rows = slice(sub * tm, (sub + 1) * tm)
    o = None
    for i in range(n_in):
        y = jnp.where(is_ctx, y_refs[2 * i][rows, :], y_refs[2 * i + 1][rows, :])
        d = jnp.dot(y, w_scr[i * D:(i + 1) * D, :], preferred_element_type=F32)
        o = d if o is None else o + d
    x_in = jnp.where(is_ctx, x_refs[0][rows, :], x_refs[1][rows, :]) if dual_x else x_refs[0][rows, :]
    x = x_in + g1_ref[0] * o
    xo_ref[rows, :] = x
    t = _rms(x, nw_ref[...]) * (1.0 + sc_ref[0]) + sh_ref[0]
    for k in range(D // LANES):
        tr_ref[pl.ds(sub * tm * SUBLANES + k, tm, stride=SUBLANES), :] = t[:, k * LANES:(k + 1) * LANES]

    t_hi, t_lo = _split_bf16(t)
    lg = _mm_nt(wr_hl[...], t_hi)
    nr = ROUTE_ROWS
    logit = lg[0:nr, :] + lg[LANES:LANES + nr, :] + _mm_nt(wr_hl[0:LANES, :], t_lo)[0:nr, :] + br_ref[0:nr, :]
    rowf = lax.broadcasted_iota(jnp.int32, (nr, tm), 0).astype(F32)
    neg = -jnp.inf

    def first_argmax(v, vmax):
        return jnp.min(jnp.where(v == vmax, rowf, float(LANES)), axis=0, keepdims=True)

    gl = jnp.where(rowf < N_GROUPS, logit, neg)
    gmax = jnp.max(gl, axis=0, keepdims=True)
    gsel = first_argmax(gl, gmax)
    gprob = 1.0 / jnp.sum(jnp.exp(gl - gmax), axis=0, keepdims=True)
    first = N_GROUPS + EXP_PER_GROUP * gsel
    el = jnp.where((rowf >= first) & (rowf < first + EXP_PER_GROUP), logit, neg)
    m1 = jnp.max(el, axis=0, keepdims=True)
    i1 = first_argmax(el, m1)
    el2 = jnp.where(rowf == i1, neg, el)
    m2 = jnp.max(el2, axis=0, keepdims=True)
    i2 = first_argmax(el2, m2)
    e2 = jnp.exp(m2 - m1)
    c1 = gprob / (1.0 + e2)
    c2 = gprob * e2 / (1.0 + e2)
    x1 = i1 - N_GROUPS
    x2 = i2 - N_GROUPS

    erow = rowf
    hot = ((erow == x1) | (erow == x2)).astype(F32)
    tri = (lax.broadcasted_iota(jnp.int32, (tm, tm), 0) < lax.broadcasted_iota(jnp.int32, (tm, tm), 1))
    before = _mm(hot, tri.astype(F32)) + carry[...]
    r1 = jnp.sum(jnp.where(erow == x1, before, 0.0), axis=0, keepdims=True)
    r2 = jnp.sum(jnp.where(erow == x2, before, 0.0), axis=0, keepdims=True)
    total = carry[...] + _mm(hot, jnp.ones((tm, tm), F32))
    carry[...] = total
    cnt_ref[...] = total[0:N_EXPERTS, 0:LANES]
    row8 = lax.broadcasted_iota(jnp.int32, (SUBLANES, tm), 0)
    out = jnp.zeros((SUBLANES, tm), F32)
    for k, v in enumerate((x1, x2, c1, c2, r1, r2)):
        out = jnp.where(row8 == k, jnp.broadcast_to(v, (SUBLANES, tm)), out)
    route_ref[:, rows] = out


def _outproj_route(ys, w_out, xs, mods, layer, norm_w, w_router, b_router):
    tm = ROUTE_TM
    n_in = len(ys) // 2
    dual_x = len(xs) == 2
    kdim = w_out.shape[0]

    def full(shape):
        return pl.BlockSpec(shape, lambda i: (0,) * len(shape))

    tile = pl.BlockSpec((tm, D), lambda i: (i, 0))
    pair = list(_ctx_lat_specs(tm))
    in_specs = (pair * n_in + [full((kdim, D))] + (pair if dual_x else [tile])
                + [_mod_spec(layer, 2, tm), _mod_spec(layer, 3, tm), _mod_spec(layer, 4, tm),
                   full((1, D)), full((D, LANES)), full((LANES, ROUTE_SUB))])
    cnt = jax.ShapeDtypeStruct((N_EXPERTS, LANES), F32)
    return pl.pallas_call(
        functools.partial(_outproj_kernel, n_in=n_in, dual_x=dual_x),
        grid=(T // tm,), in_specs=in_specs,
        out_specs=[tile, pl.BlockSpec((tm * SUBLANES, LANES), lambda i: (i, 0)),
                   pl.BlockSpec((SUBLANES, tm), lambda i: (0, i)), full(cnt.shape)],
        out_shape=[jax.ShapeDtypeStruct((T, D), F32), jax.ShapeDtypeStruct((T * SUBLANES, LANES), F32),
                   jax.ShapeDtypeStruct((SUBLANES, T), F32), cnt],
        scratch_shapes=[pltpu.VMEM((kdim, D), BF16), pltpu.VMEM((2 * LANES, D), BF16),
                        pltpu.VMEM((ROUTE_ROWS, ROUTE_SUB), F32)],
        compiler_params=_cparams(1),
        name=f"outproj{layer}",
    )(*ys, w_out, *xs, mods, mods, mods, norm_w.reshape(1, D), w_router,
      jnp.broadcast_to(b_router.reshape(LANES, 1), (LANES, ROUTE_SUB)))


def _moe_meta(counts):
    tm = MOE_TM
    counts = jnp.max(counts, axis=1).astype(jnp.int32)
    padded = ((counts + tm - 1) // tm) * tm
    ends = jnp.cumsum(padded)
    starts = ends - padded
    pads = jnp.concatenate([starts + counts, ends[-1:], ends, jnp.full((1,), MOE_ROWS + tm)]).astype(jnp.int32)
    einfo = jnp.concatenate([starts // tm, padded // tm, ends[-1:] // tm]).astype(jnp.int32)
    return starts.astype(jnp.int32), pads, einfo


def _moe_up_kernel(pos1_ref, pos2_ref, pads_ref, einfo_ref, tr_hbm, wg_ref, wu_ref, a_hbm, rowmap_ref,
                   tr_scr, g0, g1, a_st, wg_bf, wu_bf, sem, a_sems):
    e = pl.program_id(0)
    tm = MOE_TM
    first, n = einfo_ref[e], einfo_ref[N_EXPERTS + e]

    def gather(tile, dst):
        for mi in range(tm):
            tok = jnp.minimum(rowmap_ref[tile * tm + mi] >> 1, T - 1)
            dst[mi * SUBLANES:(mi + 1) * SUBLANES, :] = tr_scr[pl.ds(pl.multiple_of(tok * SUBLANES, SUBLANES), SUBLANES), :]

    @pl.when(e == 0)
    def _():
        load = pltpu.make_async_copy(tr_hbm, tr_scr, sem)
        load.start()

        def clear(c, carry):
            for i in range(SUBLANES):
                rowmap_ref[c * SUBLANES + i] = 2 * T
            return carry
        for k in range(N_EXPERTS + 1):
            lax.fori_loop(pads_ref[k] // SUBLANES, pads_ref[N_EXPERTS + 1 + k] // SUBLANES, clear, 0)

        def place(t, carry):
            rowmap_ref[pos1_ref[t]] = 2 * t
            rowmap_ref[pos2_ref[t]] = 2 * t + 1
            return carry
        lax.fori_loop(0, T, place, 0, unroll=8)
        load.wait()

    def store(slot, tile):
        return pltpu.make_async_copy(a_st.at[slot], a_hbm.at[pl.ds(pl.multiple_of(tile * tm, tm), tm), :],
                                     a_sems.at[slot])

    def compute(cur, slot):
        x = _tokmajor_to_std(cur, tm).astype(BF16)
        g = jnp.dot(x, wg_bf[...], preferred_element_type=F32)
        u = jnp.dot(x, wu_bf[...], preferred_element_type=F32)
        a_st[slot] = (_silu(g) * u).astype(a_st.dtype)

    @pl.when(n > 0)
    def _():
        wg_bf[...] = wg_ref[...].astype(BF16)
        wu_bf[...] = wu_ref[...].astype(BF16)
        last = first + n - 1
        npairs = (n + 1) // 2
        gather(first, g0)

        def pair(p, carry):
            ta = first + 2 * p
            tb = jnp.minimum(ta + 1, last)
            tb_dst = jnp.where(ta + 1 > last, MOE_TILES, tb)
            sa = 2 * (p % 2)

            @pl.when(p >= 2)
            def _():
                store(sa, 0).wait()
                store(sa + 1, 0).wait()

            gather(tb, g1)
            compute(g0, sa)
            store(sa, ta).start()
            gather(jnp.minimum(ta + 2, last), g0)
            compute(g1, sa + 1)
            store(sa + 1, tb_dst).start()
            return carry

        lax.fori_loop(0, npairs, pair, 0)
        for back in (1, 2):
            @pl.when(npairs >= back)
            def _():
                sa = 2 * ((npairs - back) % 2)
                store(sa, 0).wait()
                store(sa + 1, 0).wait()

    @pl.when(e == pl.num_programs(0) - 1)
    def _():
        total = einfo_ref[2 * N_EXPERTS]
        a_st[0] = jnp.zeros(a_st.shape[1:], a_st.dtype)

        def fill(t, carry):
            store(0, t).start()
            return carry

        def drain(t, carry):
            store(0, t).wait()
            return carry
        lax.fori_loop(total, MOE_TILES + 1, fill, 0)
        lax.fori_loop(total, MOE_TILES + 1, drain, 0)


def _moe_down_kernel(rowmap_ref, cpair_ref, einfo_ref, a_hbm, wd_ref, out_hbm, acc, y0, y1, a_st, wd_bf, a_sems):
    e = pl.program_id(0)
    tm = MOE_TM
    zrows = 512
    first, n = einfo_ref[e], einfo_ref[N_EXPERTS + e]

    @pl.when(e == 0)
    def _():
        def zero(i, carry):
            acc[pl.ds(pl.multiple_of(i * zrows, zrows), zrows), :] = jnp.zeros((zrows, LANES), F32)
            return carry
        lax.fori_loop(0, ACC_ROWS // zrows, zero, 0)

    def load(k, last=None):
        tile = first + (0 if last is None else jnp.minimum(k, last))
        slot = k % MOE_RING
        return pltpu.make_async_copy(a_hbm.at[pl.ds(pl.multiple_of(tile * tm, tm), tm), :], a_st.at[slot],
                                     a_sems.at[slot])

    def matmul(k, dst):
        y = jnp.dot(a_st[k % MOE_RING], wd_bf[...], preferred_element_type=F32)
        for k in range(D // LANES):
            dst[pl.ds(k, tm, stride=SUBLANES), :] = y[:, k * LANES:(k + 1) * LANES]

    def scatter(tile, src):
        for b in range(tm // SUBLANES):
            ents = [rowmap_ref[tile * tm + b * SUBLANES + i] for i in range(SUBLANES)]
            offs = [pl.multiple_of((e >> 1) * SUBLANES, SUBLANES) for e in ents]
            olds = [acc[pl.ds(o, SUBLANES), :] for o in offs]
            for i, o in enumerate(offs):
                r = (b * SUBLANES + i) * SUBLANES
                acc[pl.ds(o, SUBLANES), :] = olds[i] + cpair_ref[ents[i]] * src[r:r + SUBLANES, :]

    @pl.when(n > 0)
    def _():
        wd_bf[...] = wd_ref[...].astype(BF16)
        last = n - 1
        for k in range(MOE_RING):
            load(k, last).start()
        load(0).wait()
        matmul(0, y0)
        load(MOE_RING, last).start()

        def pair(p, carry):
            ka = 2 * p
            tb_rows = jnp.where(ka + 1 > last, MOE_TILES, first + ka + 1)
            load(ka + 1).wait()
            matmul(ka + 1, y1)
            scatter(first + ka, y0)
            load(ka + 1 + MOE_RING, last).start()
            load(ka + 2).wait()
            matmul(ka + 2, y0)
            scatter(tb_rows, y1)
            load(ka + 2 + MOE_RING, last).start()
            return carry

        lax.fori_loop(0, (n + 1) // 2, pair, 0)
        for k in range(MOE_RING):
            load(k).wait()

    @pl.when(e == pl.num_programs(0) - 1)
    def _():
        pltpu.sync_copy(acc.at[0:T * SUBLANES, :], out_hbm)


def _moe(tr, route_t, counts, layer, w_gate, w_up, w_down):
    tm = MOE_TM
    starts, pads, einfo = _moe_meta(counts)
    experts = jnp.arange(N_EXPERTS, dtype=jnp.int32)

    def position(e, r):
        sel = e.astype(jnp.int32)[:, None] == experts[None, :]
        return jnp.sum(jnp.where(sel, starts[None, :], 0), axis=1) + r.astype(jnp.int32)

    pos1 = position(route_t[0], route_t[4])
    pos2 = position(route_t[1], route_t[5])
    cpair = jnp.concatenate([jnp.stack([route_t[2], route_t[3]], axis=1).reshape(2 * T), jnp.zeros((2,), F32)])

    def wspec(shape):
        return pl.BlockSpec((None, None) + shape, lambda e, *pre: (layer, e, 0, 0))

    gscr = pltpu.VMEM((tm * SUBLANES, LANES), F32)
    aring = pltpu.VMEM((MOE_RING, tm, D_EXPERT), BF16)
    any_spec = pl.BlockSpec(memory_space=pl.ANY)
    rows_all = MOE_ROWS + tm
    act, rowmap = pl.pallas_call(
        _moe_up_kernel,
        grid_spec=pltpu.PrefetchScalarGridSpec(
            num_scalar_prefetch=4, grid=(N_EXPERTS,),
            in_specs=[any_spec, wspec((D, D_EXPERT)), wspec((D, D_EXPERT))],
            out_specs=[any_spec, pl.BlockSpec(memory_space=pltpu.SMEM)],
            scratch_shapes=[pltpu.VMEM((T * SUBLANES, LANES), F32), gscr, gscr, aring,
                            pltpu.VMEM((D, D_EXPERT), BF16), pltpu.VMEM((D, D_EXPERT), BF16),
                            pltpu.SemaphoreType.DMA(()), pltpu.SemaphoreType.DMA((MOE_RING,))]),
        out_shape=[jax.ShapeDtypeStruct((rows_all, D_EXPERT), BF16), jax.ShapeDtypeStruct((rows_all,), jnp.int32)],
        compiler_params=_cparams(1),
        name=f"moe_up{layer}",
    )(pos1, pos2, pads, einfo, tr, w_gate, w_up)

    return pl.pallas_call(
        _moe_down_kernel,
        grid_spec=pltpu.PrefetchScalarGridSpec(
            num_scalar_prefetch=3, grid=(N_EXPERTS,),
            in_specs=[any_spec, wspec((D_EXPERT, D))],
            out_specs=any_spec,
            scratch_shapes=[pltpu.VMEM((ACC_ROWS, LANES), F32), gscr, gscr, aring,
                            pltpu.VMEM((D_EXPERT, D), BF16), pltpu.SemaphoreType.DMA((MOE_RING,))]),
        out_shape=jax.ShapeDtypeStruct((T * SUBLANES, LANES), F32),
        compiler_params=_cparams(1),
        name=f"moe_down{layer}",
    )(rowmap, cpair, einfo, act, w_down)


FINAL_TM = 512


def _final_kernel(x_ref, moe_ref, g2_ref, nw_ref, oc_ref, ol_ref):
    x = x_ref[...] + g2_ref[0] * _tokmajor_to_std(moe_ref, FINAL_TM)
    y = _rms(x, nw_ref[...])
    is_ctx = pl.program_id(0) < T_CTX // FINAL_TM

    @pl.when(is_ctx)
    def _():
        oc_ref[...] = y

    @pl.when(jnp.logical_not(is_ctx))
    def _():
        ol_ref[...] = y


def _final(x, moe, mods, layer, norm_w):
    tm = FINAL_TM
    tile = pl.BlockSpec((tm, D), lambda i: (i, 0))
    return pl.pallas_call(
        _final_kernel,
        grid=(T // tm,),
        in_specs=[tile, pl.BlockSpec((tm * SUBLANES, LANES), lambda i: (i, 0)), _mod_spec(layer, 5, tm),
                  pl.BlockSpec((1, D), lambda i: (0, 0))],
        out_specs=list(_ctx_lat_specs(tm)),
        out_shape=[jax.ShapeDtypeStruct((T_CTX, D), F32), jax.ShapeDtypeStruct((T_LAT, D), F32)],
        compiler_params=_cparams(1),
        name="final_norm",
    )(x, moe, mods, norm_w.reshape(1, D))


def _pad_lanes(v):
    return jnp.pad(v.astype(F32), (0, LANES - v.shape[0])).reshape(1, LANES)


def kernel(x_prompt, x_sample, cache_k_attn, cache_v_attn, state_ssd_fwd, state_ssd_bwd, state_gla_fwd, state_gla_bwd, c, c_ctx, w_ada, b_ada, norm_mix_w, norm_ffn_w, w_in_even, conv_w, conv_b, dt_bias_fwd, dt_bias_bwd, a_log_fwd, a_log_bwd, d_skip, ssd_norm_w, attn_sink, w_out_even, w_in_odd, w_gk2_fwd, b_gk_fwd, w_gk2_bwd, b_gk_bwd, gla_norm_w, w_out_odd, w_router_group, b_router_group, w_router_expert, b_router_expert, w_gate_exp, w_up_exp, w_down_exp, final_norm_w):
    depth = w_ada.shape[0]
    assert depth == 2 and x_prompt.shape == (BATCH, SEQ, D) and x_sample.shape == (DEC_BATCH, DEC_SEQ, D)

    cond8 = jnp.concatenate([c_ctx[None, :], c, jnp.zeros((SUBLANES - 1 - DEC_BATCH, D), F32)], axis=0)
    mods = _adaln(cond8, w_ada, b_ada).reshape(depth * SUBLANES * 6, 1, D)
    xs0 = (x_prompt.reshape(T_CTX, D), x_sample.reshape(T_LAT, D))

    def router_params(i):
        wr = jnp.concatenate([w_router_group[i], w_router_expert[i],
                              jnp.zeros((D, LANES - N_GROUPS - N_EXPERTS), F32)], axis=1)
        return wr, _pad_lanes(jnp.concatenate([b_router_group[i], b_router_expert[i]]))

    n_zxbc = 2 * D_SSD + 2 * SSD_GROUPS * SSD_N
    n_dt = 2 * SSD_HEADS
    bc_tile, q_tiles = n_zxbc // PROJ_TN - 1, D // PROJ_TN
    rows0 = [k * PROJ_TN for k in range(bc_tile + 1)] + [n_zxbc + n_dt + k * PROJ_TN for k in range(q_tiles + 1)]

    def out_block0(j):
        return jnp.where(j == bc_tile, bc_tile + q_tiles, jnp.where((j > bc_tile) & (j <= bc_tile + q_tiles), j - 1, j))

    p0, p0dt = _modproj(xs0, mods, 0, norm_mix_w[0], jnp.transpose(w_in_even[0]), rows0, n_zxbc, out_block0)

    dtb = _pad_lanes(jnp.concatenate([dt_bias_fwd[0], dt_bias_bwd[0]]))
    alog = _pad_lanes(jnp.concatenate([a_log_fwd[0], a_log_bwd[0]]))
    dskip = jnp.repeat(d_skip[0], SSD_P).reshape(1, D_SSD)
    ssd_args = (conv_w[0], conv_b[0], dtb, alog, dskip, ssd_norm_w[0])
    y_ssd_c, ssd_f, ssd_b = _ssd(p0, p0dt, 0, BATCH, SEQ, *ssd_args)
    y_ssd_l, _, _ = _ssd(p0, p0dt, T_CTX, DEC_BATCH, DEC_SEQ, *ssd_args,
                         h0f=state_ssd_fwd[:, 0].reshape(DEC_BATCH, D_SSD, SSD_N),
                         h0b=state_ssd_bwd[:, 0].reshape(DEC_BATCH, D_SSD, SSD_N))
    sink = _pad_lanes(attn_sink[0])
    y_att_c, new_kt, new_vt = _ctx_attn(p0, sink)
    def cache_in(t):
        return jnp.transpose(t[:, 0], (0, 2, 3, 1)).reshape(DEC_BATCH, ATT_KV_DIM, PAST_LEN)

    y_att_l = _lat_attn(p0, cache_in(cache_k_attn), cache_in(cache_v_attn), sink)
    xmid0, tr0, route0, cnt0 = _outproj_route([y_ssd_c, y_ssd_l, y_att_c, y_att_l], w_out_even[0], xs0, mods, 0,
                                              norm_ffn_w[0], *router_params(0))
    moe0 = _moe(tr0, route0, cnt0, 0, w_gate_exp, w_up_exp, w_down_exp)

    dk_all = GLA_HEADS * GLA_DK
    n_qkvr = 2 * dk_all + 2 * GLA_HEADS * GLA_DV
    n_odd = w_in_odd.shape[2]
    p1, p1lr, x1 = _modproj(xmid0, mods, 1, norm_mix_w[1], jnp.transpose(w_in_odd[0]),
                            [k * PROJ_TN for k in range(n_qkvr // PROJ_TN)], n_odd - LANES, lambda j: j, moe=moe0)
    lr0 = LANES - 2 * GLA_LOWRANK
    w2f = jnp.zeros((LANES, dk_all), F32).at[lr0:lr0 + GLA_LOWRANK].set(w_gk2_fwd[0])
    w2b = jnp.zeros((LANES, dk_all), F32).at[lr0 + GLA_LOWRANK:].set(w_gk2_bwd[0])
    gla_args = (w2f, w2b, b_gk_fwd[0], b_gk_bwd[0], gla_norm_w[0])
    y_gla_c, gla_f, gla_b = _gla(p1, p1lr, 0, BATCH, SEQ, *gla_args)
    y_gla_l, _, _ = _gla(p1, p1lr, T_CTX, DEC_BATCH, DEC_SEQ, *gla_args,
                         s0f=state_gla_fwd[:, 0].reshape(DEC_BATCH, dk_all, GLA_DV),
                         s0b=state_gla_bwd[:, 0].reshape(DEC_BATCH, dk_all, GLA_DV))
    xmid1, tr1, route1, cnt1 = _outproj_route([y_gla_c, y_gla_l], w_out_odd[0], (x1,), mods, 1,
                                              norm_ffn_w[1], *router_params(1))
    moe1 = _moe(tr1, route1, cnt1, 1, w_gate_exp, w_up_exp, w_down_exp)
    y_c, y_l = _final(xmid1, moe1, mods, 1, final_norm_w)

    y_prompt = y_c.reshape(BATCH, SEQ, D)
    y_sample = y_l.reshape(DEC_BATCH, DEC_SEQ, D)
    def cache_out(t):
        return jnp.transpose(t.reshape(BATCH, 1, ATT_KV, ATT_HD, SEQ), (0, 1, 4, 2, 3))

    new_k, new_v = cache_out(new_kt), cache_out(new_vt)
    return (y_prompt, y_sample, new_k, new_v,
            ssd_f.reshape(BATCH, 1, SSD_HEADS, SSD_P, SSD_N), ssd_b.reshape(BATCH, 1, SSD_HEADS, SSD_P, SSD_N),
            gla_f.reshape(BATCH, 1, GLA_HEADS, GLA_DK, GLA_DV), gla_b.reshape(BATCH, 1, GLA_HEADS, GLA_DK, GLA_DV))
```

```python
import functools
import math

import numpy as np
import jax
import jax.numpy as jnp
from jax import lax
from jax.experimental import pallas as pl
from jax.experimental.pallas import tpu as pltpu

F32 = jnp.float32
BF16 = jnp.bfloat16

D = 1024
BATCH, SEQ = 16, 256
DEC_BATCH, DEC_SEQ = 2, 1024
PAST_LEN = 512
GRID_W = 64
EPS = 1e-6
T_CTX = BATCH * SEQ
T_LAT = DEC_BATCH * DEC_SEQ
T = T_CTX + T_LAT

SSD_HEADS, SSD_P, SSD_N, SSD_GROUPS = 16, 64, 128, 2
SSD_CONV = 5
SSD_L = 128
D_SSD = SSD_HEADS * SSD_P
HEADS_PER_GROUP = SSD_HEADS // SSD_GROUPS
GROUP_W = HEADS_PER_GROUP * SSD_P

ATT_HEADS, ATT_KV, ATT_HD = 16, 4, 64
ATT_KV_DIM = ATT_KV * ATT_HD
WINDOW = 128
ATT_BLOCK = 128
ATT_SCALE = ATT_HD ** -0.5
ROPE_BASE = 10000.0

GLA_HEADS, GLA_DK, GLA_DV = 4, 128, 256
GLA_C = 64
GLA_GATE_NORM = 16.0
GLA_LOWRANK = 16

N_GROUPS, EXP_PER_GROUP = 4, 4
N_EXPERTS = 16
D_EXPERT = 512

LANES = 128
SUBLANES = 8
VMEM_LIMIT = 56 * 1024 * 1024

P0_Z, P0_X, P0_Q, P0_BC, P0_K, P0_V = 0, 1024, 2048, 3072, 3584, 3840
P0_W = 4096
P1_Q, P1_K, P1_V, P1_R = 0, 512, 1024, 2048
P1_W = 3072

MOE_TM = 256
MOE_TILES = (2 * T) // MOE_TM + N_EXPERTS
MOE_ROWS = MOE_TILES * MOE_TM
ACC_ROWS = (T + SUBLANES) * SUBLANES


def _cparams(n_axes, vmem=VMEM_LIMIT):
    return pltpu.CompilerParams(dimension_semantics=("arbitrary",) * n_axes, vmem_limit_bytes=vmem)


def _silu(x):
    return x / (1.0 + jnp.exp(-x))


def _softplus(x):
    return jnp.maximum(x, 0.0) + jnp.log(1.0 + jnp.exp(-jnp.abs(x)))


def _mm(a, b):
    return jnp.dot(a.astype(BF16), b.astype(BF16), preferred_element_type=F32)


def _mm_nt(a, b):
    return lax.dot_general(a.astype(BF16), b.astype(BF16), (((1,), (1,)), ((), ())),
                           preferred_element_type=F32)


def _mm_tn(a, b):
    return lax.dot_general(a.astype(BF16), b.astype(BF16), (((0,), (0,)), ((), ())),
                           preferred_element_type=F32)


def _rms(x, w):
    return x * lax.rsqrt(jnp.mean(x * x, axis=-1, keepdims=True) + EPS) * w


def _cumsum_rows(x, n):
    row = lax.broadcasted_iota(jnp.int32, x.shape, 0)
    s = 1
    while s < n:
        x = x + jnp.where(row >= s, pltpu.roll(x, s, 0), 0.0)
        s *= 2
    return x


def _mod_row(tok0):
    return jnp.where(tok0 < T_CTX, 0, 1 + (tok0 - T_CTX) // DEC_SEQ)


ADA_TN = 1536


def _adaln_kernel(c_ref, w_ref, b_ref, o_ref):
    s = _silu(c_ref[...])
    o_ref[0] = _mm(s, w_ref[0]) + b_ref[0]


def _adaln(cond8, w_ada, b_ada):
    depth = w_ada.shape[0]
    return pl.pallas_call(
        _adaln_kernel,
        grid=(depth, 6 * D // ADA_TN),
        in_specs=[
            pl.BlockSpec((SUBLANES, D), lambda l, j: (0, 0)),
            pl.BlockSpec((1, D, ADA_TN), lambda l, j: (l, 0, j)),
            pl.BlockSpec((1, 1, ADA_TN), lambda l, j: (l, 0, j)),
        ],
        out_specs=pl.BlockSpec((1, SUBLANES, ADA_TN), lambda l, j: (l, 0, j)),
        out_shape=jax.ShapeDtypeStruct((depth, SUBLANES, 6 * D), F32),
        compiler_params=_cparams(2),
        name="adaln",
    )(cond8, w_ada, b_ada.reshape(depth, 1, 6 * D))


def _mod_spec(layer, chunk, tm, tile_of=lambda i, *_: i):
    return pl.BlockSpec((1, 1, D), lambda *g: ((layer * SUBLANES + _mod_row(tile_of(*g) * tm)) * 6 + chunk, 0, 0))


def _tokmajor_to_std(ref, tm):
    return jnp.concatenate([ref[pl.ds(k, tm, stride=SUBLANES), :] for k in range(D // LANES)], axis=1)


PROJ_TM = 1024
PROJ_TN = 512


def _ctx_lat_specs(tm, width=D):
    n_ctx = T_CTX // tm
    return (pl.BlockSpec((tm, width), lambda i, *_: (jnp.minimum(i, n_ctx - 1), 0)),
            pl.BlockSpec((tm, width), lambda i, *_: (jnp.maximum(i - n_ctx, 0), 0)))


def _modproj_kernel(*refs, dual_x):
    it = iter(refs)
    if dual_x:
        xc_ref, xl_ref = next(it), next(it)
    else:
        x_ref, moe_ref, g2_ref = next(it), next(it), next(it)
    sh_ref, sc_ref, nw_ref, w_ref, ws_ref, o_ref, os_ref = (next(it) for _ in range(7))
    xo_ref = None if dual_x else next(it)
    h_all, w_bf = next(it), next(it)
    j, i = pl.program_id(0), pl.program_id(1)
    tm = PROJ_TM
    rows = pl.ds(pl.multiple_of(i * tm, tm), tm)

    @pl.when(j == 0)
    def _():
        if dual_x:
            x = jnp.where(i < T_CTX // tm, xc_ref[...], xl_ref[...])
        else:
            x = x_ref[...] + g2_ref[0] * _tokmajor_to_std(moe_ref, tm)
            xo_ref[...] = x
        h = (_rms(x, nw_ref[...]) * (1.0 + sc_ref[0]) + sh_ref[0]).astype(BF16)
        h_all[rows, :] = h
        os_ref[...] = _mm_nt(h, ws_ref[...])

    @pl.when(i == 0)
    def _():
        w_bf[...] = w_ref[...].astype(BF16)

    o_ref[...] = _mm_nt(h_all[rows, :], w_bf[...]).astype(o_ref.dtype)


def _modproj(xs, mods, layer, norm_w, wt, tile_rows, small_row, out_block, moe=None):
    tm, tn = PROJ_TM, PROJ_TN
    dual_x = moe is None
    n_tiles = len(tile_rows)
    n_i, n_ctx = T // tm, T_CTX // tm

    def w_row(j, i):
        r = jnp.int32(tile_rows[0])
        for k in range(1, n_tiles):
            r = jnp.where(j == k, tile_rows[k], r)
        return pl.multiple_of(r, SUBLANES), 0

    def tok(j, i):
        return jnp.where(j == 0, i, n_i - 1)

    tile = pl.BlockSpec((tm, D), lambda j, i: (tok(j, i), 0))
    if dual_x:
        in_specs = [pl.BlockSpec((tm, D), lambda j, i: (jnp.minimum(tok(j, i), n_ctx - 1), 0)),
                    pl.BlockSpec((tm, D), lambda j, i: (jnp.maximum(tok(j, i) - n_ctx, 0), 0))]
        args = list(xs)
    else:
        in_specs = [tile, pl.BlockSpec((tm * SUBLANES, LANES), lambda j, i: (tok(j, i), 0)),
                    _mod_spec(layer - 1, 5, tm, tok)]
        args = [xs, moe, mods]
    in_specs += [_mod_spec(layer, 0, tm, tok), _mod_spec(layer, 1, tm, tok), pl.BlockSpec((1, D), lambda j, i: (0, 0)),
                 pl.BlockSpec((pl.Element(tn), pl.Element(D)), w_row),
                 pl.BlockSpec((pl.Element(LANES), pl.Element(D)), lambda j, i: (small_row, 0))]
    args += [mods, mods, norm_w.reshape(1, D), wt, wt]
    out_specs = [pl.BlockSpec((tm, tn), lambda j, i: (i, out_block(j))),
                 pl.BlockSpec((tm, LANES), lambda j, i: (tok(j, i), 0))]
    out_shape = [jax.ShapeDtypeStruct((T, n_tiles * tn), BF16), jax.ShapeDtypeStruct((T, LANES), F32)]
    if not dual_x:
        out_specs.append(tile)
        out_shape.append(jax.ShapeDtypeStruct((T, D), F32))
    return pl.pallas_call(
        functools.partial(_modproj_kernel, dual_x=dual_x),
        grid=(n_tiles, n_i), in_specs=in_specs, out_specs=out_specs, out_shape=out_shape,
        scratch_shapes=[pltpu.VMEM((T, D), BF16), pltpu.VMEM((tn, D), BF16)],
        compiler_params=_cparams(2),
        name=f"modproj{layer}",
    )(*args)


def _expand_heads(v, off):
    hi = (lax.broadcasted_iota(jnp.int32, (v.shape[0], LANES), 1) >= SSD_P).astype(jnp.int32)
    tiles = [jnp.take_along_axis(v, hi + (off + 2 * q), axis=1) for q in range(SSD_HEADS // 2)]
    return jnp.concatenate(tiles, axis=1)


def _ssd_kernel(*refs, seq, has_h0):
    if has_h0:
        (z_ref, x_ref, bc_ref, dt_ref, cwx_ref, cwbc_ref, cbx_ref, cbbc_ref, dtb_ref, alog_ref, dsk_ref,
         nw_ref, h0f_ref, h0b_ref, y_ref, sf_ref, sb_ref,
         xpad, bcpad, xc, bcc, a_scr, dt_scr, yacc, hf, hb) = refs
    else:
        (z_ref, x_ref, bc_ref, dt_ref, cwx_ref, cwbc_ref, cbx_ref, cbbc_ref, dtb_ref, alog_ref, dsk_ref,
         nw_ref, y_ref, sf_ref, sb_ref,
         xpad, bcpad, xc, bcc, a_scr, dt_scr, yacc, hf, hb) = refs
    L = SSD_L
    nc = seq // L
    pad = SUBLANES
    half = SSD_CONV // 2

    for buf, src, cw, cb, dst in ((xpad, x_ref, cwx_ref, cbx_ref, xc), (bcpad, bc_ref, cwbc_ref, cbbc_ref, bcc)):
        width = buf.shape[1]
        buf[0:pad, :] = jnp.zeros((pad, width), F32)
        buf[pad + seq:2 * pad + seq, :] = jnp.zeros((pad, width), F32)
        buf[pad:pad + seq, :] = src[...].astype(F32)
        for blk in range(nc):
            acc = jnp.broadcast_to(cb[...], (L, width))
            for j in range(SSD_CONV):
                r0 = pad - half + j + blk * L
                acc = acc + cw[j:j + 1, :] * buf[r0:r0 + L, :]
            dst[blk * L:(blk + 1) * L, :] = _silu(acc)

    lane = lax.broadcasted_iota(jnp.int32, (seq, LANES), 1)
    dts = jnp.where(lane < 2 * SSD_HEADS, _softplus(dt_ref[...] + dtb_ref[...]), 0.0)
    dt_scr[...] = dts
    a_scr[...] = dts * (-jnp.exp(alog_ref[...]))

    if has_h0:
        hf[...] = h0f_ref[...]
        hb[...] = h0b_ref[...]
    else:
        hf[...] = jnp.zeros(hf.shape, F32)
        hb[...] = jnp.zeros(hb.shape, F32)

    row = lax.broadcasted_iota(jnp.int32, (L, L), 0)
    col = lax.broadcasted_iota(jnp.int32, (L, L), 1)
    lane_l = lax.broadcasted_iota(jnp.int32, (L, LANES), 1)
    lo_half = lane_l < SSD_P

    def chunk(c, fwd, h_scr):
        off = 0 if fwd else SSD_HEADS
        r0 = pl.multiple_of(c * L, L)
        a = a_scr[pl.ds(r0, L), :]
        dt = dt_scr[pl.ds(r0, L), :]
        cs = _cumsum_rows(a, L)
        total = cs[L - 1:L, :]
        if fwd:
            u = cs
            rvec = jnp.exp(cs)
            ed = jnp.exp(total - cs) * dt
            keep = col <= row
        else:
            ex = cs - a
            u = -ex
            rvec = jnp.exp(total - ex)
            ed = jnp.exp(ex) * dt
            keep = col >= row
        ut = jnp.transpose(u)
        dtt = jnp.transpose(dt)
        tcol = jnp.transpose(jnp.broadcast_to(total, (L, LANES)))[:, 0:1]
        rexp = _expand_heads(rvec, off)
        edexp = _expand_heads(ed, off)
        x = xc[pl.ds(r0, L), :]
        bc = bcc[pl.ds(r0, L), :]
        outs = []
        for g in range(SSD_GROUPS):
            bg = bc[:, g * SSD_N:(g + 1) * SSD_N]
            cg = bc[:, SSD_GROUPS * SSD_N + g * SSD_N:SSD_GROUPS * SSD_N + (g + 1) * SSD_N]
            cbm = _mm_nt(cg, bg)
            hg = h_scr[g * GROUP_W:(g + 1) * GROUP_W, :]
            xg = x[:, g * GROUP_W:(g + 1) * GROUP_W]
            y_off = _mm_nt(cg, hg) * rexp[:, g * GROUP_W:(g + 1) * GROUP_W]
            tiles = []
            for p in range(HEADS_PER_GROUP // 2):
                xt = xg[:, p * LANES:(p + 1) * LANES]
                acc = None
                for s in range(2):
                    h = off + g * HEADS_PER_GROUP + 2 * p + s
                    seg = u[:, h:h + 1] - ut[h:h + 1, :]
                    m = cbm * jnp.exp(jnp.where(keep, seg, -jnp.inf)) * dtt[h:h + 1, :]
                    xm = jnp.where(lo_half if s == 0 else jnp.logical_not(lo_half), xt, 0.0)
                    d = _mm(m, xm)
                    acc = d if acc is None else acc + d
                tiles.append(acc)
            outs.append(y_off + jnp.concatenate(tiles, axis=1))
            decs = []
            for hh in range(HEADS_PER_GROUP):
                h = off + g * HEADS_PER_GROUP + hh
                decs.append(jnp.broadcast_to(jnp.exp(tcol[h:h + 1, :]), (SSD_P, SSD_N)))
            dec = jnp.concatenate(decs, axis=0)
            h_scr[g * GROUP_W:(g + 1) * GROUP_W, :] = dec * hg + _mm_tn(xg * edexp[:, g * GROUP_W:(g + 1) * GROUP_W], bg)
        return r0, x, jnp.concatenate(outs, axis=1)

    def fwd_body(c, carry):
        r0, _, y = chunk(c, True, hf)
        yacc[pl.ds(r0, L), :] = y
        return carry

    lax.fori_loop(0, nc, fwd_body, 0)

    def bwd_body(i, carry):
        c = nc - 1 - i
        r0, x, y = chunk(c, False, hb)
        y = yacc[pl.ds(r0, L), :] + y + dsk_ref[...] * x
        y = y * _silu(z_ref[pl.ds(r0, L), :].astype(F32))
        y_ref[pl.ds(r0, L), :] = _rms(y, nw_ref[...]).astype(y_ref.dtype)
        return carry

    lax.fori_loop(0, nc, bwd_body, 0)
    sf_ref[...] = hf[...]
    sb_ref[...] = hb[...]


def _ssd(p0, p0dt, tok0, nseq, seq, cw, cb, dtb, alog, dskip, nw, h0f=None, h0b=None):
    has_h0 = h0f is not None
    b0 = tok0 // seq

    def cols(width, start):
        return pl.BlockSpec((seq, width), lambda s: (b0 + s, start // width))

    def full(shape):
        return pl.BlockSpec(shape, lambda s: (0,) * len(shape))

    in_specs = [cols(D_SSD, P0_Z), cols(D_SSD, P0_X), cols(512, P0_BC), cols(LANES, 0),
                full((SSD_CONV, D_SSD)), full((SSD_CONV, 512)), full((1, D_SSD)), full((1, 512)),
                full((1, LANES)), full((1, LANES)), full((1, D_SSD)), full((1, D_SSD))]
    args = [p0, p0, p0, p0dt, cw[:, :D_SSD], cw[:, D_SSD:], cb[:D_SSD].reshape(1, -1), cb[D_SSD:].reshape(1, -1),
            dtb, alog, dskip, nw.reshape(1, -1)]
    st_spec = pl.BlockSpec((None, D_SSD, SSD_N), lambda s: (s, 0, 0))
    if has_h0:
        in_specs += [st_spec, st_spec]
        args += [h0f, h0b]
    st_shape = jax.ShapeDtypeStruct((nseq, D_SSD, SSD_N), F32)
    return pl.pallas_call(
        functools.partial(_ssd_kernel, seq=seq, has_h0=has_h0),
        grid=(nseq,), in_specs=in_specs,
        out_specs=[pl.BlockSpec((seq, D_SSD), lambda s: (s, 0)), st_spec, st_spec],
        out_shape=[jax.ShapeDtypeStruct((nseq * seq, D_SSD), BF16), st_shape, st_shape],
        scratch_shapes=[pltpu.VMEM((seq + 2 * SUBLANES, D_SSD), F32), pltpu.VMEM((seq + 2 * SUBLANES, 512), F32),
                        pltpu.VMEM((seq, D_SSD), F32), pltpu.VMEM((seq, 512), F32),
                        pltpu.VMEM((seq, LANES), F32), pltpu.VMEM((seq, LANES), F32),
                        pltpu.VMEM((seq, D_SSD), F32),
                        pltpu.VMEM((D_SSD, SSD_N), F32), pltpu.VMEM((D_SSD, SSD_N), F32)],
        compiler_params=_cparams(1),
        name=f"ssd{seq}",
    )(*args)


def _place_halves(tile, kv_in_high):
    lo = lax.broadcasted_iota(jnp.int32, tile.shape, 1) < ATT_HD
    swapped = pltpu.roll(tile, ATT_HD, 1)
    if kv_in_high:
        return jnp.where(lo, swapped, 0.0), jnp.where(lo, 0.0, tile)
    return jnp.where(lo, tile, 0.0), jnp.where(lo, 0.0, swapped)


def _place_rows(vt, kv_in_high):
    head = vt[ATT_HD:, :] if kv_in_high else vt[:ATT_HD, :]
    z = jnp.zeros_like(head)
    return jnp.concatenate([head, z], axis=0), jnp.concatenate([z, head], axis=0)


LOG2E = 1.4426950408889634
SCORE_SCALE = ATT_SCALE * LOG2E


def _sink_attend_t(score_parts, value_parts, sink2):
    m = sink2
    for s in score_parts:
        m = jnp.maximum(m, jnp.max(s, axis=0, keepdims=True))
    denom = jnp.exp2(sink2 - m)
    out = None
    for s, v in zip(score_parts, value_parts):
        p = jnp.exp2(s - m)
        denom = denom + jnp.sum(p, axis=0, keepdims=True)
        o = _mm(v, p)
        out = o if out is None else out + o
    return out * (1.0 / denom)


def _attn_schedule(n, scores, attend):
    scores(0)
    for j in range(n):
        if j + 1 < n:
            scores(j + 1)
        attend(j)


def _ctx_attn_kernel(q_ref, k_ref, v_ref, sink_ref, o_ref, kt_ref, vt_ref, s_a, s_b):
    sink2 = sink_ref[...] * LOG2E
    bufs = (s_a, s_b)
    half = SEQ // 2
    for t in range(ATT_KV_DIM // LANES):
        cols = slice(t * LANES, (t + 1) * LANES)
        kt_ref[cols, :] = jnp.transpose(k_ref[:, cols].astype(F32))
        vt_ref[cols, :] = jnp.transpose(v_ref[:, cols].astype(F32))

    def kv_tile(ref, j):
        return ref[:, (j // 2) * LANES:(j // 2 + 1) * LANES].astype(F32), (j % 2 == 1)

    def scores(j):
        k_lo, k_hi = _place_halves(*kv_tile(k_ref, j))
        qst = jnp.concatenate([q_ref[:, qt * LANES:(qt + 1) * LANES] for qt in (2 * j, 2 * j + 1)], axis=0)
        bufs[j % 2][...] = _mm_nt(jnp.concatenate([k_lo, k_hi], axis=0), qst) * SCORE_SCALE

    def attend(j):
        src = bufs[j % 2]
        v, high = kv_tile(v_ref, j)
        vts = _place_rows(jnp.transpose(v), high)
        for ql in range(2):
            qt = 2 * j + ql
            for qh in range(2):
                cols = slice(ql * SEQ + qh * half, ql * SEQ + (qh + 1) * half)
                acc = None
                for s, vv in enumerate(vts):
                    o = _sink_attend_t([src[s * SEQ:(s + 1) * SEQ, cols]], [vv], sink2[:, 2 * qt + s:2 * qt + s + 1])
                    acc = o if acc is None else acc + o
                o_ref[qh * half:(qh + 1) * half, qt * LANES:(qt + 1) * LANES] = jnp.transpose(acc).astype(o_ref.dtype)

    _attn_schedule(ATT_KV, scores, attend)


def _ctx_attn(p0, sink):
    def cols(width, start):
        return pl.BlockSpec((SEQ, width), lambda b: (b, start // width))

    sbuf = pltpu.VMEM((2 * SEQ, 2 * SEQ), F32)
    return pl.pallas_call(
        _ctx_attn_kernel,
        grid=(BATCH,),
        in_specs=[cols(D, P0_Q), cols(ATT_KV_DIM, P0_K), cols(ATT_KV_DIM, P0_V),
                  pl.BlockSpec((1, LANES), lambda b: (0, 0))],
        out_specs=[pl.BlockSpec((SEQ, D), lambda b: (b, 0))]
        + [pl.BlockSpec((None, ATT_KV_DIM, SEQ), lambda b: (b, 0, 0))] * 2,
        out_shape=[jax.ShapeDtypeStruct((T_CTX, D), BF16)] + [jax.ShapeDtypeStruct((BATCH, ATT_KV_DIM, SEQ), F32)] * 2,
        scratch_shapes=[sbuf, sbuf],
        compiler_params=_cparams(1),
        name="ctx_attn",
    )(p0, p0, p0, sink)


def _rope_tables():
    quarter = ATT_HD // 4
    t = np.arange(DEC_SEQ)
    lane = np.arange(LANES)
    inv = ROPE_BASE ** (-(lane % quarter).astype(np.float64) / quarter)
    pos = np.where(((lane % ATT_HD) < ATT_HD // 2)[None, :], (t // GRID_W)[:, None], (t % GRID_W)[:, None])
    ang = pos * inv[None, :]
    first = ((lane % (2 * quarter)) < quarter)[None, :]
    cos, sin = np.cos(ang), np.sin(ang)
    return (jnp.asarray(cos, F32), jnp.asarray(np.where(first, -sin, 0.0), F32),
            jnp.asarray(np.where(first, 0.0, sin), F32))


def _rope(x, cos, sa, sb):
    quarter = ATT_HD // 4
    return x * cos + pltpu.roll(x, LANES - quarter, 1) * sa + pltpu.roll(x, quarter, 1) * sb


def _lat_attn_kernel(q_ref, kp_ref, kc_ref, kn_ref, vp_ref, vc_ref, vn_ref, ck_ref, cv_ref,
                     cos_ref, sa_ref, sb_ref, sink_ref, o_ref, c_a, c_b, w_a, w_b):
    blk = pl.program_id(1)
    nb = pl.num_programs(1)
    B = ATT_BLOCK
    sink2 = sink_ref[...] * LOG2E
    cbufs, wbufs = (c_a, c_b), (w_a, w_b)

    def tables(b):
        r0 = pl.multiple_of(b * B, B)
        return cos_ref[pl.ds(r0, B), :], sa_ref[pl.ds(r0, B), :], sb_ref[pl.ds(r0, B), :]

    tq = tables(blk)
    tk = [tables(jnp.maximum(blk - 1, 0)), tq, tables(jnp.minimum(blk + 1, nb - 1))]
    kabs = (blk - 1) * B + lax.broadcasted_iota(jnp.int32, (3 * B, B), 0)
    qpos = blk * B + lax.broadcasted_iota(jnp.int32, (3 * B, B), 1)
    ok = (jnp.abs(qpos - kabs) <= WINDOW) & (kabs >= 0) & (kabs < nb * B)
    ok = jnp.concatenate([ok, ok], axis=1)
    ok = jnp.concatenate([ok, ok], axis=0)

    def scores(j):
        high = (j % 2 == 1)
        sl = slice((j // 2) * LANES, (j // 2 + 1) * LANES)
        kw = jnp.concatenate([_rope(r[:, sl].astype(F32), *tb) for r, tb in zip((kp_ref, kc_ref, kn_ref), tk)], axis=0)
        qs = [q_ref[:, qt * LANES:(qt + 1) * LANES].astype(F32) for qt in (2 * j, 2 * j + 1)]
        q_plain = jnp.concatenate(qs, axis=0)
        q_rope = jnp.concatenate([_rope(q, *tq) for q in qs], axis=0)
        ck = jnp.transpose(ck_ref[sl, :])
        cbufs[j % 2][...] = _mm_nt(jnp.concatenate(_place_halves(ck, high), axis=0), q_plain) * SCORE_SCALE
        win = _mm_nt(jnp.concatenate(_place_halves(kw, high), axis=0), q_rope) * SCORE_SCALE
        wbufs[j % 2][...] = jnp.where(ok, win, -jnp.inf)

    def attend(j):
        high = (j % 2 == 1)
        sl = slice((j // 2) * LANES, (j // 2 + 1) * LANES)
        vw = jnp.concatenate([jnp.transpose(r[:, sl].astype(F32)) for r in (vp_ref, vc_ref, vn_ref)], axis=1)
        vts = _place_rows(vw, high)
        cvts = _place_rows(cv_ref[sl, :], high)
        csrc, wsrc = cbufs[j % 2], wbufs[j % 2]
        for ql in range(2):
            qt = 2 * j + ql
            cols = slice(ql * B, (ql + 1) * B)
            acc = None
            for s in range(2):
                parts = [csrc[s * PAST_LEN:(s + 1) * PAST_LEN, cols], wsrc[s * 3 * B:(s + 1) * 3 * B, cols]]
                o = _sink_attend_t(parts, [cvts[s], vts[s]], sink2[:, 2 * qt + s:2 * qt + s + 1])
                acc = o if acc is None else acc + o
            o_ref[:, qt * LANES:(qt + 1) * LANES] = jnp.transpose(acc).astype(o_ref.dtype)

    _attn_schedule(ATT_KV, scores, attend)


def _lat_attn(p0, ck, cv, sink):
    nb = DEC_SEQ // ATT_BLOCK
    base = T_CTX // ATT_BLOCK

    def kv(start, shift):
        return pl.BlockSpec((ATT_BLOCK, ATT_KV_DIM),
                            lambda b, i: (base + b * nb + jnp.clip(i + shift, 0, nb - 1), start // ATT_KV_DIM))

    def full(shape):
        return pl.BlockSpec(shape, lambda b, i: (0,) * len(shape))

    cache = pl.BlockSpec((None, ATT_KV_DIM, PAST_LEN), lambda b, i: (b, 0, 0))
    cos, sa, sb = _rope_tables()
    return pl.pallas_call(
        _lat_attn_kernel,
        grid=(DEC_BATCH, nb),
        in_specs=[pl.BlockSpec((ATT_BLOCK, D), lambda b, i: (base + b * nb + i, P0_Q // D)),
                  kv(P0_K, -1), kv(P0_K, 0), kv(P0_K, 1), kv(P0_V, -1), kv(P0_V, 0), kv(P0_V, 1),
                  cache, cache, full((DEC_SEQ, LANES)), full((DEC_SEQ, LANES)), full((DEC_SEQ, LANES)),
                  full((1, LANES))],
        out_specs=pl.BlockSpec((ATT_BLOCK, D), lambda b, i: (b * nb + i, 0)),
        out_shape=jax.ShapeDtypeStruct((T_LAT, D), BF16),
        scratch_shapes=[pltpu.VMEM((2 * PAST_LEN, 2 * ATT_BLOCK), F32)] * 2
        + [pltpu.VMEM((2 * 3 * ATT_BLOCK, 2 * ATT_BLOCK), F32)] * 2,
        compiler_params=_cparams(2),
        name="lat_attn",
    )(p0, p0, p0, p0, p0, p0, p0, ck, cv, cos, sa, sb, sink)


def _log_sigmoid(x):
    return jnp.minimum(x, 0.0) - jnp.log(1.0 + jnp.exp(-jnp.abs(x)))


def _gla_kernel(*refs, seq, has_s0):
    if has_s0:
        (q_ref, k_ref, v_ref, r_ref, lr_ref, w2f_ref, w2b_ref, bf_ref, bb_ref, nw_ref, s0f_ref, s0b_ref,
         y_ref, sf_ref, sb_ref, gf, gb, yf, yb, stf, stb) = refs
    else:
        (q_ref, k_ref, v_ref, r_ref, lr_ref, w2f_ref, w2b_ref, bf_ref, bb_ref, nw_ref,
         y_ref, sf_ref, sb_ref, gf, gb, yf, yb, stf, stb) = refs
    C = GLA_C
    nc = seq // C
    lr = lr_ref[...]
    gf[...] = _log_sigmoid(_mm(lr, w2f_ref[...]) + bf_ref[...]) / GLA_GATE_NORM
    gb[...] = _log_sigmoid(_mm(lr, w2b_ref[...]) + bb_ref[...]) / GLA_GATE_NORM
    for h in range(GLA_HEADS):
        rows = slice(h * GLA_DV, (h + 1) * GLA_DV)
        if has_s0:
            stf[rows, :] = jnp.transpose(s0f_ref[h * GLA_DK:(h + 1) * GLA_DK, :])
            stb[rows, :] = jnp.transpose(s0b_ref[h * GLA_DK:(h + 1) * GLA_DK, :])
        else:
            stf[rows, :] = jnp.zeros((GLA_DV, GLA_DK), F32)
            stb[rows, :] = jnp.zeros((GLA_DV, GLA_DK), F32)

    row = lax.broadcasted_iota(jnp.int32, (C, C), 0)
    col = lax.broadcasted_iota(jnp.int32, (C, C), 1)
    qscale = GLA_DK ** -0.5

    def chunk(c, fwd):
        g_scr, y_scr, st = (gf, yf, stf) if fwd else (gb, yb, stb)
        r0 = pl.multiple_of(c * C, C)
        g = g_scr[pl.ds(r0, C), :]
        cs = _cumsum_rows(g, C)
        total = cs[C - 1:C, :]
        q = q_ref[pl.ds(r0, C), :].astype(F32) * qscale
        k = k_ref[pl.ds(r0, C), :].astype(F32)
        v = v_ref[pl.ds(r0, C), :]
        if fwd:
            qs, ks, ke = q * jnp.exp(cs), k * jnp.exp(-cs), k * jnp.exp(total - cs)
            keep = col <= row
        else:
            ex = cs - g
            qs, ks, ke = q * jnp.exp(total - ex), k * jnp.exp(ex - total), k * jnp.exp(ex)
            keep = col >= row
        dec = jnp.exp(total)
        for h in range(GLA_HEADS):
            kc = slice(h * GLA_DK, (h + 1) * GLA_DK)
            vc = slice(h * GLA_DV, (h + 1) * GLA_DV)
            s_t = st[vc, :]
            att = jnp.where(keep, _mm_nt(qs[:, kc], ks[:, kc]), 0.0)
            y_scr[pl.ds(r0, C), vc] = _mm(att, v[:, vc]) + _mm_nt(qs[:, kc], s_t)
            st[vc, :] = dec[:, kc] * s_t + _mm_tn(v[:, vc], ke[:, kc])

    def body(i, carry):
        chunk(i, True)
        chunk(nc - 1 - i, False)
        return carry

    lax.fori_loop(0, nc, body, 0, unroll=4)

    nw = nw_ref[...]
    for blk in range(seq // LANES):
        rs = slice(blk * LANES, (blk + 1) * LANES)
        y = yf[rs, :] + yb[rs, :]
        gate = _silu(r_ref[rs, :].astype(F32))
        for h in range(GLA_HEADS):
            vc = slice(h * GLA_DV, (h + 1) * GLA_DV)
            y_ref[rs, vc] = (_rms(y[:, vc], nw) * gate[:, vc]).astype(y_ref.dtype)
    for h in range(GLA_HEADS):
        rows = slice(h * GLA_DV, (h + 1) * GLA_DV)
        sf_ref[h * GLA_DK:(h + 1) * GLA_DK, :] = jnp.transpose(stf[rows, :])
        sb_ref[h * GLA_DK:(h + 1) * GLA_DK, :] = jnp.transpose(stb[rows, :])


def _gla(p1, p1lr, tok0, nseq, seq, w2f, w2b, bgf, bgb, nw, s0f=None, s0b=None):
    has_s0 = s0f is not None
    b0 = tok0 // seq
    dk_all, dv_all = GLA_HEADS * GLA_DK, GLA_HEADS * GLA_DV

    def cols(width, start):
        return pl.BlockSpec((seq, width), lambda s: (b0 + s, start // width))

    def full(shape):
        return pl.BlockSpec(shape, lambda s: (0,) * len(shape))

    in_specs = [cols(dk_all, P1_Q), cols(dk_all, P1_K), cols(dv_all, P1_V), cols(dv_all, P1_R), cols(LANES, 0),
                full((LANES, dk_all)), full((LANES, dk_all)), full((1, dk_all)), full((1, dk_all)), full((1, GLA_DV))]
    args = [p1, p1, p1, p1, p1lr, w2f, w2b, bgf.reshape(1, -1), bgb.reshape(1, -1), nw.reshape(1, -1)]
    st_spec = pl.BlockSpec((None, dk_all, GLA_DV), lambda s: (s, 0, 0))
    if has_s0:
        in_specs += [st_spec, st_spec]
        args += [s0f, s0b]
    st_shape = jax.ShapeDtypeStruct((nseq, dk_all, GLA_DV), F32)
    return pl.pallas_call(
        functools.partial(_gla_kernel, seq=seq, has_s0=has_s0),
        grid=(nseq,), in_specs=in_specs,
        out_specs=[pl.BlockSpec((seq, dv_all), lambda s: (s, 0)), st_spec, st_spec],
        out_shape=[jax.ShapeDtypeStruct((nseq * seq, dv_all), BF16), st_shape, st_shape],
        scratch_shapes=[pltpu.VMEM((seq, dk_all), F32), pltpu.VMEM((seq, dk_all), F32),
                        pltpu.VMEM((seq, dv_all), F32), pltpu.VMEM((seq, dv_all), F32),
                        pltpu.VMEM((dv_all, GLA_DK), F32), pltpu.VMEM((dv_all, GLA_DK), F32)],
        compiler_params=_cparams(1),
        name=f"gla{seq}",
    )(*args)


ROUTE_TM = 512
ROUTE_SUB = 256
ROUTE_ROWS = 32


def _split_bf16(x):
    hi = x.astype(BF16)
    return hi, (x - hi.astype(F32)).astype(BF16)


def _outproj_kernel(*refs, n_in, dual_x):
    y_refs = refs[:2 * n_in]
    n_x = 2 if dual_x else 1
    x_refs = refs[2 * n_in + 1:2 * n_in + 1 + n_x]
    w_ref = refs[2 * n_in]
    (g1_ref, sh_ref, sc_ref, nw_ref, wr_ref, br_ref,
     xo_ref, tr_ref, route_ref, cnt_ref, w_scr, wr_hl, carry) = refs[2 * n_in + 1 + n_x:]
    is_ctx = pl.program_id(0) < T_CTX // ROUTE_TM

    @pl.when(pl.program_id(0) == 0)
    def _():
        w_scr[...] = w_ref[...].astype(BF16)
        hi, lo = _split_bf16(jnp.transpose(wr_ref[...]))
        wr_hl[0:LANES, :] = hi
        wr_hl[LANES:2 * LANES, :] = lo
        carry[...] = jnp.zeros(carry.shape, F32)

    for sub in range(ROUTE_TM // ROUTE_SUB):
        _outproj_subtile(sub, is_ctx, y_refs, x_refs, dual_x, n_in, g1_ref, sh_ref, sc_ref, nw_ref, br_ref,
                         xo_ref, tr_ref, route_ref, cnt_ref, w_scr, wr_hl, carry)


def _outproj_subtile(sub, is_ctx, y_refs, x_refs, dual_x, n_in, g1_ref, sh_ref, sc_ref, nw_ref, br_ref,
                     xo_ref, tr_ref, route_ref, cnt_ref, w_scr, wr_hl, carry):
    tm = ROUTE_SUB
    rows = slice(sub * tm, (sub + 1) * tm)
    o = None
    for i in range(n_in):
        y = jnp.where(is_ctx, y_refs[2 * i][rows, :], y_refs[2 * i + 1][rows, :])
        d = jnp.dot(y, w_scr[i * D:(i + 1) * D, :], preferred_element_type=F32)
        o = d if o is None else o + d
    x_in = jnp.where(is_ctx, x_refs[0][rows, :], x_refs[1][rows, :]) if dual_x else x_refs[0][rows, :]
    x = x_in + g1_ref[0] * o
    xo_ref[rows, :] = x
    t = _rms(x, nw_ref[...]) * (1.0 + sc_ref[0]) + sh_ref[0]
    for k in range(D // LANES):
        tr_ref[pl.ds(sub * tm * SUBLANES + k, tm, stride=SUBLANES), :] = t[:, k * LANES:(k + 1) * LANES]

    t_hi, t_lo = _split_bf16(t)
    lg = _mm_nt(wr_hl[...], t_hi)
    nr = ROUTE_ROWS
    logit = lg[0:nr, :] + lg[LANES:LANES + nr, :] + _mm_nt(wr_hl[0:LANES, :], t_lo)[0:nr, :] + br_ref[0:nr, :]
    rowf = lax.broadcasted_iota(jnp.int32, (nr, tm), 0).astype(F32)
    neg = -jnp.inf

    def first_argmax(v, vmax):
        return jnp.min(jnp.where(v == vmax, rowf, float(LANES)), axis=0, keepdims=True)

    gl = jnp.where(rowf < N_GROUPS, logit, neg)
    gmax = jnp.max(gl, axis=0, keepdims=True)
    gsel = first_argmax(gl, gmax)
    gprob = 1.0 / jnp.sum(jnp.exp(gl - gmax), axis=0, keepdims=True)
    first = N_GROUPS + EXP_PER_GROUP * gsel
    el = jnp.where((rowf >= first) & (rowf < first + EXP_PER_GROUP), logit, neg)
    m1 = jnp.max(el, axis=0, keepdims=True)
    i1 = first_argmax(el, m1)
    el2 = jnp.where(rowf == i1, neg, el)
    m2 = jnp.max(el2, axis=0, keepdims=True)
    i2 = first_argmax(el2, m2)
    e2 = jnp.exp(m2 - m1)
    c1 = gprob / (1.0 + e2)
    c2 = gprob * e2 / (1.0 + e2)
    x1 = i1 - N_GROUPS
    x2 = i2 - N_GROUPS

    erow = rowf
    hot = ((erow == x1) | (erow == x2)).astype(F32)
    tri = (lax.broadcasted_iota(jnp.int32, (tm, tm), 0) < lax.broadcasted_iota(jnp.int32, (tm, tm), 1))
    before = _mm(hot, tri.astype(F32)) + carry[...]
    r1 = jnp.sum(jnp.where(erow == x1, before, 0.0), axis=0, keepdims=True)
    r2 = jnp.sum(jnp.where(erow == x2, before, 0.0), axis=0, keepdims=True)
    total = carry[...] + _mm(hot, jnp.ones((tm, tm), F32))
    carry[...] = total
    cnt_ref[...] = total[0:N_EXPERTS, 0:LANES]
    row8 = lax.broadcasted_iota(jnp.int32, (SUBLANES, tm), 0)
    out = jnp.zeros((SUBLANES, tm), F32)
    for k, v in enumerate((x1, x2, c1, c2, r1, r2)):
        out = jnp.where(row8 == k, jnp.broadcast_to(v, (SUBLANES, tm)), out)
    route_ref[:, rows] = out


def _outproj_route(ys, w_out, xs, mods, layer, norm_w, w_router, b_router):
    tm = ROUTE_TM
    n_in = len(ys) // 2
    dual_x = len(xs) == 2
    kdim = w_out.shape[0]

    def full(shape):
        return pl.BlockSpec(shape, lambda i: (0,) * len(shape))

    tile = pl.BlockSpec((tm, D), lambda i: (i, 0))
    pair = list(_ctx_lat_specs(tm))
    in_specs = (pair * n_in + [full((kdim, D))] + (pair if dual_x else [tile])
                + [_mod_spec(layer, 2, tm), _mod_spec(layer, 3, tm), _mod_spec(layer, 4, tm),
                   full((1, D)), full((D, LANES)), full((LANES, ROUTE_SUB))])
    cnt = jax.ShapeDtypeStruct((N_EXPERTS, LANES), F32)
    return pl.pallas_call(
        functools.partial(_outproj_kernel, n_in=n_in, dual_x=dual_x),
        grid=(T // tm,), in_specs=in_specs,
        out_specs=[tile, pl.BlockSpec((tm * SUBLANES, LANES), lambda i: (i, 0)),
                   pl.BlockSpec((SUBLANES, tm), lambda i: (0, i)), full(cnt.shape)],
        out_shape=[jax.ShapeDtypeStruct((T, D), F32), jax.ShapeDtypeStruct((T * SUBLANES, LANES), F32),
                   jax.ShapeDtypeStruct((SUBLANES, T), F32), cnt],
        scratch_shapes=[pltpu.VMEM((kdim, D), BF16), pltpu.VMEM((2 * LANES, D), BF16),
                        pltpu.VMEM((ROUTE_ROWS, ROUTE_SUB), F32)],
        compiler_params=_cparams(1),
        name=f"outproj{layer}",
    )(*ys, w_out, *xs, mods, mods, mods, norm_w.reshape(1, D), w_router,
      jnp.broadcast_to(b_router.reshape(LANES, 1), (LANES, ROUTE_SUB)))


def _moe_meta(counts):
    tm = MOE_TM
    experts = jnp.arange(N_EXPERTS, dtype=jnp.int32)
    counts = jnp.max(counts, axis=1).astype(jnp.int32)
    padded = ((counts + tm - 1) // tm) * tm
    ends = jnp.cumsum(padded)
    tile_start = jnp.arange(MOE_TILES, dtype=jnp.int32) * tm
    te = jnp.sum((tile_start[:, None] >= ends[None, :]).astype(jnp.int32), axis=1)
    last = jnp.max(jnp.where(counts > 0, experts, 0))
    meta = jnp.concatenate([jnp.minimum(te, last), ends[-1:] // tm]).astype(jnp.int32)
    starts = ends - padded
    pads = jnp.concatenate([starts + counts, ends[-1:], ends, jnp.full((1,), MOE_ROWS)]).astype(jnp.int32)
    return starts.astype(jnp.int32), pads, meta


def _expert_changed(meta_ref, j):
    return (j == 0) | (meta_ref[j] != meta_ref[jnp.maximum(j - 1, 0)])


def _moe_up_kernel(pos1_ref, pos2_ref, pads_ref, meta_ref, tr_hbm, wg_ref, wu_ref, a_ref, rowmap_ref,
                   tr_scr, g0, g1, wg_bf, wu_bf, sem):
    j = pl.program_id(0)
    tm = MOE_TM
    ntiles = meta_ref[MOE_TILES]

    def gather(tile, dst):
        for mi in range(tm):
            tok = jnp.minimum(rowmap_ref[tile * tm + mi] >> 1, T - 1)
            dst[mi * SUBLANES:(mi + 1) * SUBLANES, :] = tr_scr[pl.ds(pl.multiple_of(tok * SUBLANES, SUBLANES), SUBLANES), :]

    @pl.when(j == 0)
    def _():
        load = pltpu.make_async_copy(tr_hbm, tr_scr, sem)
        load.start()

        def clear(c, carry):
            for i in range(SUBLANES):
                rowmap_ref[c * SUBLANES + i] = 2 * T
            return carry
        for k in range(N_EXPERTS + 1):
            lax.fori_loop(pads_ref[k] // SUBLANES, pads_ref[N_EXPERTS + 1 + k] // SUBLANES, clear, 0)

        def place(t, carry):
            rowmap_ref[pos1_ref[t]] = 2 * t
            rowmap_ref[pos2_ref[t]] = 2 * t + 1
            return carry
        lax.fori_loop(0, T, place, 0, unroll=8)
        load.wait()
        gather(0, g0)

    def compute(cur, nxt):
        gather(jnp.minimum(j + 1, ntiles - 1), nxt)
        x = _tokmajor_to_std(cur, tm).astype(BF16)
        g = jnp.dot(x, wg_bf[...], preferred_element_type=F32)
        u = jnp.dot(x, wu_bf[...], preferred_element_type=F32)
        a_ref[...] = (_silu(g) * u).astype(a_ref.dtype)

    @pl.when(j < ntiles)
    def _():
        @pl.when(_expert_changed(meta_ref, j))
        def _():
            wg_bf[...] = wg_ref[...].astype(BF16)
            wu_bf[...] = wu_ref[...].astype(BF16)

        pl.when(j % 2 == 0)(functools.partial(compute, g0, g1))
        pl.when(j % 2 == 1)(functools.partial(compute, g1, g0))

    @pl.when(j >= ntiles)
    def _():
        a_ref[...] = jnp.zeros(a_ref.shape, a_ref.dtype)


def _moe_down_kernel(rowmap_ref, cpair_ref, meta_ref, a_ref, wd_ref, out_hbm, acc, y0, y1, wd_bf):
    j = pl.program_id(0)
    tm = MOE_TM
    zrows = 512
    ntiles = meta_ref[MOE_TILES]

    @pl.when(j == 0)
    def _():
        def zero(i, carry):
            acc[pl.ds(pl.multiple_of(i * zrows, zrows), zrows), :] = jnp.zeros((zrows, LANES), F32)
            return carry
        lax.fori_loop(0, ACC_ROWS // zrows, zero, 0)

    def matmul(dst):
        y = jnp.dot(a_ref[...], wd_bf[...], preferred_element_type=F32)
        for k in range(D // LANES):
            dst[pl.ds(k, tm, stride=SUBLANES), :] = y[:, k * LANES:(k + 1) * LANES]

    def scatter(tile, src):
        for b in range(tm // SUBLANES):
            ents = [rowmap_ref[tile * tm + b * SUBLANES + i] for i in range(SUBLANES)]
            offs = [pl.multiple_of((e >> 1) * SUBLANES, SUBLANES) for e in ents]
            olds = [acc[pl.ds(o, SUBLANES), :] for o in offs]
            for i, o in enumerate(offs):
                r = (b * SUBLANES + i) * SUBLANES
                acc[pl.ds(o, SUBLANES), :] = olds[i] + cpair_ref[ents[i]] * src[r:r + SUBLANES, :]

    has_mm = j < ntiles
    has_sc = (j >= 1) & (j <= ntiles)

    @pl.when(has_mm)
    def _():
        @pl.when(_expert_changed(meta_ref, j))
        def _():
            wd_bf[...] = wd_ref[...].astype(BF16)

    for par, (cur, prev) in enumerate(((y0, y1), (y1, y0))):
        mine = (j % 2) == par

        @pl.when(mine & has_mm & has_sc)
        def _():
            matmul(cur)
            scatter(j - 1, prev)

        @pl.when(mine & has_mm & jnp.logical_not(has_sc))
        def _():
            matmul(cur)

        @pl.when(mine & jnp.logical_not(has_mm) & has_sc)
        def _():
            scatter(j - 1, prev)

    @pl.when(j == pl.num_programs(0) - 1)
    def _():
        pltpu.sync_copy(acc.at[0:T * SUBLANES, :], out_hbm)


def _moe(tr, route_t, counts, layer, w_gate, w_up, w_down):
    tm = MOE_TM
    starts, pads, meta = _moe_meta(counts)
    experts = jnp.arange(N_EXPERTS, dtype=jnp.int32)

    def position(e, r):
        sel = e.astype(jnp.int32)[:, None] == experts[None, :]
        return jnp.sum(jnp.where(sel, starts[None, :], 0), axis=1) + r.astype(jnp.int32)

    pos1 = position(route_t[0], route_t[4])
    pos2 = position(route_t[1], route_t[5])
    cpair = jnp.concatenate([jnp.stack([route_t[2], route_t[3]], axis=1).reshape(2 * T), jnp.zeros((2,), F32)])

    def wspec(shape, n):
        return pl.BlockSpec((None, None) + shape,
                            lambda j, *pre: (layer, pre[n - 1][jnp.minimum(j, MOE_TILES - 1)], 0, 0))

    gscr = pltpu.VMEM((tm * SUBLANES, LANES), F32)
    act, rowmap = pl.pallas_call(
        _moe_up_kernel,
        grid_spec=pltpu.PrefetchScalarGridSpec(
            num_scalar_prefetch=4, grid=(MOE_TILES,),
            in_specs=[pl.BlockSpec(memory_space=pl.ANY), wspec((D, D_EXPERT), 4), wspec((D, D_EXPERT), 4)],
            out_specs=[pl.BlockSpec((tm, D_EXPERT), lambda j, *pre: (j, 0)), pl.BlockSpec(memory_space=pltpu.SMEM)],
            scratch_shapes=[pltpu.VMEM((T * SUBLANES, LANES), F32), gscr, gscr,
                            pltpu.VMEM((D, D_EXPERT), BF16), pltpu.VMEM((D, D_EXPERT), BF16),
                            pltpu.SemaphoreType.DMA(())]),
        out_shape=[jax.ShapeDtypeStruct((MOE_ROWS, D_EXPERT), BF16), jax.ShapeDtypeStruct((MOE_ROWS,), jnp.int32)],
        compiler_params=_cparams(1),
        name=f"moe_up{layer}",
    )(pos1, pos2, pads, meta, tr, w_gate, w_up)

    return pl.pallas_call(
        _moe_down_kernel,
        grid_spec=pltpu.PrefetchScalarGridSpec(
            num_scalar_prefetch=3, grid=(MOE_TILES + 1,),
            in_specs=[pl.BlockSpec((tm, D_EXPERT), lambda j, *pre: (jnp.minimum(j, MOE_TILES - 1), 0)),
                      wspec((D_EXPERT, D), 3)],
            out_specs=pl.BlockSpec(memory_space=pl.ANY),
            scratch_shapes=[pltpu.VMEM((ACC_ROWS, LANES), F32), gscr, gscr, pltpu.VMEM((D_EXPERT, D), BF16)]),
        out_shape=jax.ShapeDtypeStruct((T * SUBLANES, LANES), F32),
        compiler_params=_cparams(1),
        name=f"moe_down{layer}",
    )(rowmap, cpair, meta, act, w_down)


FINAL_TM = 512


def _final_kernel(x_ref, moe_ref, g2_ref, nw_ref, oc_ref, ol_ref):
    x = x_ref[...] + g2_ref[0] * _tokmajor_to_std(moe_ref, FINAL_TM)
    y = _rms(x, nw_ref[...])
    is_ctx = pl.program_id(0) < T_CTX // FINAL_TM

    @pl.when(is_ctx)
    def _():
        oc_ref[...] = y

    @pl.when(jnp.logical_not(is_ctx))
    def _():
        ol_ref[...] = y


def _final(x, moe, mods, layer, norm_w):
    tm = FINAL_TM
    tile = pl.BlockSpec((tm, D), lambda i: (i, 0))
    return pl.pallas_call(
        _final_kernel,
        grid=(T // tm,),
        in_specs=[tile, pl.BlockSpec((tm * SUBLANES, LANES), lambda i: (i, 0)), _mod_spec(layer, 5, tm),
                  pl.BlockSpec((1, D), lambda i: (0, 0))],
        out_specs=list(_ctx_lat_specs(tm)),
        out_shape=[jax.ShapeDtypeStruct((T_CTX, D), F32), jax.ShapeDtypeStruct((T_LAT, D), F32)],
        compiler_params=_cparams(1),
        name="final_norm",
    )(x, moe, mods, norm_w.reshape(1, D))


def _pad_lanes(v):
    return jnp.pad(v.astype(F32), (0, LANES - v.shape[0])).reshape(1, LANES)


def kernel(x_prompt, x_sample, cache_k_attn, cache_v_attn, state_ssd_fwd, state_ssd_bwd, state_gla_fwd, state_gla_bwd, c, c_ctx, w_ada, b_ada, norm_mix_w, norm_ffn_w, w_in_even, conv_w, conv_b, dt_bias_fwd, dt_bias_bwd, a_log_fwd, a_log_bwd, d_skip, ssd_norm_w, attn_sink, w_out_even, w_in_odd, w_gk2_fwd, b_gk_fwd, w_gk2_bwd, b_gk_bwd, gla_norm_w, w_out_odd, w_router_group, b_router_group, w_router_expert, b_router_expert, w_gate_exp, w_up_exp, w_down_exp, final_norm_w):
    depth = w_ada.shape[0]
    assert depth == 2 and x_prompt.shape == (BATCH, SEQ, D) and x_sample.shape == (DEC_BATCH, DEC_SEQ, D)

    cond8 = jnp.concatenate([c_ctx[None, :], c, jnp.zeros((SUBLANES - 1 - DEC_BATCH, D), F32)], axis=0)
    mods = _adaln(cond8, w_ada, b_ada).reshape(depth * SUBLANES * 6, 1, D)
    xs0 = (x_prompt.reshape(T_CTX, D), x_sample.reshape(T_LAT, D))

    def router_params(i):
        wr = jnp.concatenate([w_router_group[i], w_router_expert[i],
                              jnp.zeros((D, LANES - N_GROUPS - N_EXPERTS), F32)], axis=1)
        return wr, _pad_lanes(jnp.concatenate([b_router_group[i], b_router_expert[i]]))

    n_zxbc = 2 * D_SSD + 2 * SSD_GROUPS * SSD_N
    n_dt = 2 * SSD_HEADS
    bc_tile, q_tiles = n_zxbc // PROJ_TN - 1, D // PROJ_TN
    rows0 = [k * PROJ_TN for k in range(bc_tile + 1)] + [n_zxbc + n_dt + k * PROJ_TN for k in range(q_tiles + 1)]

    def out_block0(j):
        return jnp.where(j == bc_tile, bc_tile + q_tiles, jnp.where((j > bc_tile) & (j <= bc_tile + q_tiles), j - 1, j))

    p0, p0dt = _modproj(xs0, mods, 0, norm_mix_w[0], jnp.transpose(w_in_even[0]), rows0, n_zxbc, out_block0)

    dtb = _pad_lanes(jnp.concatenate([dt_bias_fwd[0], dt_bias_bwd[0]]))
    alog = _pad_lanes(jnp.concatenate([a_log_fwd[0], a_log_bwd[0]]))
    dskip = jnp.repeat(d_skip[0], SSD_P).reshape(1, D_SSD)
    ssd_args = (conv_w[0], conv_b[0], dtb, alog, dskip, ssd_norm_w[0])
    y_ssd_c, ssd_f, ssd_b = _ssd(p0, p0dt, 0, BATCH, SEQ, *ssd_args)
    y_ssd_l, _, _ = _ssd(p0, p0dt, T_CTX, DEC_BATCH, DEC_SEQ, *ssd_args,
                         h0f=state_ssd_fwd[:, 0].reshape(DEC_BATCH, D_SSD, SSD_N),
                         h0b=state_ssd_bwd[:, 0].reshape(DEC_BATCH, D_SSD, SSD_N))
    sink = _pad_lanes(attn_sink[0])
    y_att_c, new_kt, new_vt = _ctx_attn(p0, sink)
    def cache_in(t):
        return jnp.transpose(t[:, 0], (0, 2, 3, 1)).reshape(DEC_BATCH, ATT_KV_DIM, PAST_LEN)

    y_att_l = _lat_attn(p0, cache_in(cache_k_attn), cache_in(cache_v_attn), sink)
    xmid0, tr0, route0, cnt0 = _outproj_route([y_ssd_c, y_ssd_l, y_att_c, y_att_l], w_out_even[0], xs0, mods, 0,
                                              norm_ffn_w[0], *router_params(0))
    moe0 = _moe(tr0, route0, cnt0, 0, w_gate_exp, w_up_exp, w_down_exp)

    dk_all = GLA_HEADS * GLA_DK
    n_qkvr = 2 * dk_all + 2 * GLA_HEADS * GLA_DV
    n_odd = w_in_odd.shape[2]
    p1, p1lr, x1 = _modproj(xmid0, mods, 1, norm_mix_w[1], jnp.transpose(w_in_odd[0]),
                            [k * PROJ_TN for k in range(n_qkvr // PROJ_TN)], n_odd - LANES, lambda j: j, moe=moe0)
    lr0 = LANES - 2 * GLA_LOWRANK
    w2f = jnp.zeros((LANES, dk_all), F32).at[lr0:lr0 + GLA_LOWRANK].set(w_gk2_fwd[0])
    w2b = jnp.zeros((LANES, dk_all), F32).at[lr0 + GLA_LOWRANK:].set(w_gk2_bwd[0])
    gla_args = (w2f, w2b, b_gk_fwd[0], b_gk_bwd[0], gla_norm_w[0])
    y_gla_c, gla_f, gla_b = _gla(p1, p1lr, 0, BATCH, SEQ, *gla_args)
    y_gla_l, _, _ = _gla(p1, p1lr, T_CTX, DEC_BATCH, DEC_SEQ, *gla_args,
                         s0f=state_gla_fwd[:, 0].reshape(DEC_BATCH, dk_all, GLA_DV),
                         s0b=state_gla_bwd[:, 0].reshape(DEC_BATCH, dk_all, GLA_DV))
    xmid1, tr1, route1, cnt1 = _outproj_route([y_gla_c, y_gla_l], w_out_odd[0], (x1,), mods, 1,
                                              norm_ffn_w[1], *router_params(1))
    moe1 = _moe(tr1, route1, cnt1, 1, w_gate_exp, w_up_exp, w_down_exp)
    y_c, y_l = _final(xmid1, moe1, mods, 1, final_norm_w)

    y_prompt = y_c.reshape(BATCH, SEQ, D)
    y_sample = y_l.reshape(DEC_BATCH, DEC_SEQ, D)
    def cache_out(t):
        return jnp.transpose(t.reshape(BATCH, 1, ATT_KV, ATT_HD, SEQ), (0, 1, 4, 2, 3))

    new_k, new_v = cache_out(new_kt), cache_out(new_vt)
    return (y_prompt, y_sample, new_k, new_v,
            ssd_f.reshape(BATCH, 1, SSD_HEADS, SSD_P, SSD_N), ssd_b.reshape(BATCH, 1, SSD_HEADS, SSD_P, SSD_N),
            gla_f.reshape(BATCH, 1, GLA_HEADS, GLA_DK, GLA_DV), gla_b.reshape(BATCH, 1, GLA_HEADS, GLA_DK, GLA_DV))
```

```python
import functools
import math

import numpy as np
import jax
import jax.numpy as jnp
from jax import lax
from jax.experimental import pallas as pl
from jax.experimental.pallas import tpu as pltpu

F32 = jnp.float32
BF16 = jnp.bfloat16

D = 1024
BATCH, SEQ = 16, 256
DEC_BATCH, DEC_SEQ = 2, 1024
PAST_LEN = 512
GRID_W = 64
EPS = 1e-6
T_CTX = BATCH * SEQ
T_LAT = DEC_BATCH * DEC_SEQ
T = T_CTX + T_LAT

SSD_HEADS, SSD_P, SSD_N, SSD_GROUPS = 16, 64, 128, 2
SSD_CONV = 5
SSD_L = 128
D_SSD = SSD_HEADS * SSD_P
HEADS_PER_GROUP = SSD_HEADS // SSD_GROUPS
GROUP_W = HEADS_PER_GROUP * SSD_P

ATT_HEADS, ATT_KV, ATT_HD = 16, 4, 64
ATT_KV_DIM = ATT_KV * ATT_HD
WINDOW = 128
ATT_BLOCK = 128
ATT_SCALE = ATT_HD ** -0.5
ROPE_BASE = 10000.0

GLA_HEADS, GLA_DK, GLA_DV = 4, 128, 256
GLA_C = 64
GLA_GATE_NORM = 16.0
GLA_LOWRANK = 16

N_GROUPS, EXP_PER_GROUP = 4, 4
N_EXPERTS = 16
D_EXPERT = 512

LANES = 128
SUBLANES = 8
VMEM_LIMIT = 56 * 1024 * 1024

P0_Z, P0_X, P0_Q, P0_BC, P0_K, P0_V = 0, 1024, 2048, 3072, 3584, 3840
P0_W = 4096
P1_Q, P1_K, P1_V, P1_R = 0, 512, 1024, 2048
P1_W = 3072

MOE_TM = 512
MOE_TILES = (2 * T) // MOE_TM + N_EXPERTS
MOE_ROWS = MOE_TILES * MOE_TM
ACC_ROWS = (T + SUBLANES) * SUBLANES


def _cparams(n_axes, vmem=VMEM_LIMIT):
    return pltpu.CompilerParams(dimension_semantics=("arbitrary",) * n_axes, vmem_limit_bytes=vmem)


def _silu(x):
    return x / (1.0 + jnp.exp(-x))


def _softplus(x):
    return jnp.maximum(x, 0.0) + jnp.log(1.0 + jnp.exp(-jnp.abs(x)))


def _mm(a, b):
    return jnp.dot(a.astype(BF16), b.astype(BF16), preferred_element_type=F32)


def _mm_nt(a, b):
    return lax.dot_general(a.astype(BF16), b.astype(BF16), (((1,), (1,)), ((), ())),
                           preferred_element_type=F32)


def _mm_tn(a, b):
    return lax.dot_general(a.astype(BF16), b.astype(BF16), (((0,), (0,)), ((), ())),
                           preferred_element_type=F32)


def _rms(x, w):
    return x * lax.rsqrt(jnp.mean(x * x, axis=-1, keepdims=True) + EPS) * w


def _cumsum_rows(x, n):
    row = lax.broadcasted_iota(jnp.int32, x.shape, 0)
    s = 1
    while s < n:
        x = x + jnp.where(row >= s, pltpu.roll(x, s, 0), 0.0)
        s *= 2
    return x


def _mod_row(tok0):
    return jnp.where(tok0 < T_CTX, 0, 1 + (tok0 - T_CTX) // DEC_SEQ)


ADA_TN = 1536


def _adaln_kernel(c_ref, w_ref, b_ref, o_ref):
    s = _silu(c_ref[...])
    o_ref[0] = _mm(s, w_ref[0]) + b_ref[0]


def _adaln(cond8, w_ada, b_ada):
    depth = w_ada.shape[0]
    return pl.pallas_call(
        _adaln_kernel,
        grid=(depth, 6 * D // ADA_TN),
        in_specs=[
            pl.BlockSpec((SUBLANES, D), lambda l, j: (0, 0)),
            pl.BlockSpec((1, D, ADA_TN), lambda l, j: (l, 0, j)),
            pl.BlockSpec((1, 1, ADA_TN), lambda l, j: (l, 0, j)),
        ],
        out_specs=pl.BlockSpec((1, SUBLANES, ADA_TN), lambda l, j: (l, 0, j)),
        out_shape=jax.ShapeDtypeStruct((depth, SUBLANES, 6 * D), F32),
        compiler_params=_cparams(2),
        name="adaln",
    )(cond8, w_ada, b_ada.reshape(depth, 1, 6 * D))


def _mod_spec(layer, chunk, tm, tile_of=lambda i, *_: i):
    return pl.BlockSpec((1, 1, D), lambda *g: ((layer * SUBLANES + _mod_row(tile_of(*g) * tm)) * 6 + chunk, 0, 0))


def _tokmajor_to_std(ref, tm):
    return jnp.concatenate([ref[pl.ds(k, tm, stride=SUBLANES), :] for k in range(D // LANES)], axis=1)


PROJ_TM = 1024
PROJ_TN = 512


def _ctx_lat_specs(tm, width=D):
    n_ctx = T_CTX // tm
    return (pl.BlockSpec((tm, width), lambda i, *_: (jnp.minimum(i, n_ctx - 1), 0)),
            pl.BlockSpec((tm, width), lambda i, *_: (jnp.maximum(i - n_ctx, 0), 0)))


def _modproj_kernel(*refs, dual_x):
    it = iter(refs)
    if dual_x:
        xc_ref, xl_ref = next(it), next(it)
    else:
        x_ref, moe_ref, g2_ref = next(it), next(it), next(it)
    sh_ref, sc_ref, nw_ref, w_ref, ws_ref, o_ref, os_ref = (next(it) for _ in range(7))
    xo_ref = None if dual_x else next(it)
    h_all, w_bf = next(it), next(it)
    j, i = pl.program_id(0), pl.program_id(1)
    tm = PROJ_TM
    rows = pl.ds(pl.multiple_of(i * tm, tm), tm)

    @pl.when(j == 0)
    def _():
        if dual_x:
            x = jnp.where(i < T_CTX // tm, xc_ref[...], xl_ref[...])
        else:
            x = x_ref[...] + g2_ref[0] * _tokmajor_to_std(moe_ref, tm)
            xo_ref[...] = x
        h = (_rms(x, nw_ref[...]) * (1.0 + sc_ref[0]) + sh_ref[0]).astype(BF16)
        h_all[rows, :] = h
        os_ref[...] = _mm_nt(h, ws_ref[...])

    @pl.when(i == 0)
    def _():
        w_bf[...] = w_ref[...].astype(BF16)

    o_ref[...] = _mm_nt(h_all[rows, :], w_bf[...]).astype(o_ref.dtype)


def _modproj(xs, mods, layer, norm_w, wt, tile_rows, small_row, out_block, moe=None):
    tm, tn = PROJ_TM, PROJ_TN
    dual_x = moe is None
    n_tiles = len(tile_rows)
    n_i, n_ctx = T // tm, T_CTX // tm

    def w_row(j, i):
        r = jnp.int32(tile_rows[0])
        for k in range(1, n_tiles):
            r = jnp.where(j == k, tile_rows[k], r)
        return pl.multiple_of(r, SUBLANES), 0

    def tok(j, i):
        return jnp.where(j == 0, i, n_i - 1)

    tile = pl.BlockSpec((tm, D), lambda j, i: (tok(j, i), 0))
    if dual_x:
        in_specs = [pl.BlockSpec((tm, D), lambda j, i: (jnp.minimum(tok(j, i), n_ctx - 1), 0)),
                    pl.BlockSpec((tm, D), lambda j, i: (jnp.maximum(tok(j, i) - n_ctx, 0), 0))]
        args = list(xs)
    else:
        in_specs = [tile, pl.BlockSpec((tm * SUBLANES, LANES), lambda j, i: (tok(j, i), 0)),
                    _mod_spec(layer - 1, 5, tm, tok)]
        args = [xs, moe, mods]
    in_specs += [_mod_spec(layer, 0, tm, tok), _mod_spec(layer, 1, tm, tok), pl.BlockSpec((1, D), lambda j, i: (0, 0)),
                 pl.BlockSpec((pl.Element(tn), pl.Element(D)), w_row),
                 pl.BlockSpec((pl.Element(LANES), pl.Element(D)), lambda j, i: (small_row, 0))]
    args += [mods, mods, norm_w.reshape(1, D), wt, wt]
    out_specs = [pl.BlockSpec((tm, tn), lambda j, i: (i, out_block(j))),
                 pl.BlockSpec((tm, LANES), lambda j, i: (tok(j, i), 0))]
    out_shape = [jax.ShapeDtypeStruct((T, n_tiles * tn), BF16), jax.ShapeDtypeStruct((T, LANES), F32)]
    if not dual_x:
        out_specs.append(tile)
        out_shape.append(jax.ShapeDtypeStruct((T, D), F32))
    return pl.pallas_call(
        functools.partial(_modproj_kernel, dual_x=dual_x),
        grid=(n_tiles, n_i), in_specs=in_specs, out_specs=out_specs, out_shape=out_shape,
        scratch_shapes=[pltpu.VMEM((T, D), BF16), pltpu.VMEM((tn, D), BF16)],
        compiler_params=_cparams(2),
        name=f"modproj{layer}",
    )(*args)


def _expand_heads(v, off):
    hi = (lax.broadcasted_iota(jnp.int32, (v.shape[0], LANES), 1) >= SSD_P).astype(jnp.int32)
    tiles = [jnp.take_along_axis(v, hi + (off + 2 * q), axis=1) for q in range(SSD_HEADS // 2)]
    return jnp.concatenate(tiles, axis=1)


def _ssd_kernel(*refs, seq, has_h0):
    if has_h0:
        (z_ref, x_ref, bc_ref, dt_ref, cwx_ref, cwbc_ref, cbx_ref, cbbc_ref, dtb_ref, alog_ref, dsk_ref,
         nw_ref, h0f_ref, h0b_ref, y_ref, sf_ref, sb_ref,
         xpad, bcpad, xc, bcc, a_scr, dt_scr, yacc, hf, hb) = refs
    else:
        (z_ref, x_ref, bc_ref, dt_ref, cwx_ref, cwbc_ref, cbx_ref, cbbc_ref, dtb_ref, alog_ref, dsk_ref,
         nw_ref, y_ref, sf_ref, sb_ref,
         xpad, bcpad, xc, bcc, a_scr, dt_scr, yacc, hf, hb) = refs
    L = SSD_L
    nc = seq // L
    pad = SUBLANES
    half = SSD_CONV // 2

    for buf, src, cw, cb, dst in ((xpad, x_ref, cwx_ref, cbx_ref, xc), (bcpad, bc_ref, cwbc_ref, cbbc_ref, bcc)):
        width = buf.shape[1]
        buf[0:pad, :] = jnp.zeros((pad, width), F32)
        buf[pad + seq:2 * pad + seq, :] = jnp.zeros((pad, width), F32)
        buf[pad:pad + seq, :] = src[...].astype(F32)
        for blk in range(nc):
            acc = jnp.broadcast_to(cb[...], (L, width))
            for j in range(SSD_CONV):
                r0 = pad - half + j + blk * L
                acc = acc + cw[j:j + 1, :] * buf[r0:r0 + L, :]
            dst[blk * L:(blk + 1) * L, :] = _silu(acc)

    lane = lax.broadcasted_iota(jnp.int32, (seq, LANES), 1)
    dts = jnp.where(lane < 2 * SSD_HEADS, _softplus(dt_ref[...] + dtb_ref[...]), 0.0)
    dt_scr[...] = dts
    a_scr[...] = dts * (-jnp.exp(alog_ref[...]))

    if has_h0:
        hf[...] = h0f_ref[...]
        hb[...] = h0b_ref[...]
    else:
        hf[...] = jnp.zeros(hf.shape, F32)
        hb[...] = jnp.zeros(hb.shape, F32)

    row = lax.broadcasted_iota(jnp.int32, (L, L), 0)
    col = lax.broadcasted_iota(jnp.int32, (L, L), 1)
    lane_l = lax.broadcasted_iota(jnp.int32, (L, LANES), 1)
    lo_half = lane_l < SSD_P

    def chunk(c, fwd, h_scr):
        off = 0 if fwd else SSD_HEADS
        r0 = pl.multiple_of(c * L, L)
        a = a_scr[pl.ds(r0, L), :]
        dt = dt_scr[pl.ds(r0, L), :]
        cs = _cumsum_rows(a, L)
        total = cs[L - 1:L, :]
        if fwd:
            u = cs
            rvec = jnp.exp(cs)
            ed = jnp.exp(total - cs) * dt
            keep = col <= row
        else:
            ex = cs - a
            u = -ex
            rvec = jnp.exp(total - ex)
            ed = jnp.exp(ex) * dt
            keep = col >= row
        ut = jnp.transpose(u)
        dtt = jnp.transpose(dt)
        tcol = jnp.transpose(jnp.broadcast_to(total, (L, LANES)))[:, 0:1]
        rexp = _expand_heads(rvec, off)
        edexp = _expand_heads(ed, off)
        x = xc[pl.ds(r0, L), :]
        bc = bcc[pl.ds(r0, L), :]
        outs = []
        for g in range(SSD_GROUPS):
            bg = bc[:, g * SSD_N:(g + 1) * SSD_N]
            cg = bc[:, SSD_GROUPS * SSD_N + g * SSD_N:SSD_GROUPS * SSD_N + (g + 1) * SSD_N]
            cbm = _mm_nt(cg, bg)
            hg = h_scr[g * GROUP_W:(g + 1) * GROUP_W, :]
            xg = x[:, g * GROUP_W:(g + 1) * GROUP_W]
            y_off = _mm_nt(cg, hg) * rexp[:, g * GROUP_W:(g + 1) * GROUP_W]
            tiles = []
            for p in range(HEADS_PER_GROUP // 2):
                xt = xg[:, p * LANES:(p + 1) * LANES]
                acc = None
                for s in range(2):
                    h = off + g * HEADS_PER_GROUP + 2 * p + s
                    seg = u[:, h:h + 1] - ut[h:h + 1, :]
                    m = cbm * jnp.exp(jnp.where(keep, seg, -jnp.inf)) * dtt[h:h + 1, :]
                    xm = jnp.where(lo_half if s == 0 else jnp.logical_not(lo_half), xt, 0.0)
                    d = _mm(m, xm)
                    acc = d if acc is None else acc + d
                tiles.append(acc)
            outs.append(y_off + jnp.concatenate(tiles, axis=1))
            decs = []
            for hh in range(HEADS_PER_GROUP):
                h = off + g * HEADS_PER_GROUP + hh
                decs.append(jnp.broadcast_to(jnp.exp(tcol[h:h + 1, :]), (SSD_P, SSD_N)))
            dec = jnp.concatenate(decs, axis=0)
            h_scr[g * GROUP_W:(g + 1) * GROUP_W, :] = dec * hg + _mm_tn(xg * edexp[:, g * GROUP_W:(g + 1) * GROUP_W], bg)
        return r0, x, jnp.concatenate(outs, axis=1)

    def fwd_body(c, carry):
        r0, _, y = chunk(c, True, hf)
        yacc[pl.ds(r0, L), :] = y
        return carry

    lax.fori_loop(0, nc, fwd_body, 0)

    def bwd_body(i, carry):
        c = nc - 1 - i
        r0, x, y = chunk(c, False, hb)
        y = yacc[pl.ds(r0, L), :] + y + dsk_ref[...] * x
        y = y * _silu(z_ref[pl.ds(r0, L), :].astype(F32))
        y_ref[pl.ds(r0, L), :] = _rms(y, nw_ref[...]).astype(y_ref.dtype)
        return carry

    lax.fori_loop(0, nc, bwd_body, 0)
    sf_ref[...] = hf[...]
    sb_ref[...] = hb[...]


def _ssd(p0, p0dt, tok0, nseq, seq, cw, cb, dtb, alog, dskip, nw, h0f=None, h0b=None):
    has_h0 = h0f is not None
    b0 = tok0 // seq

    def cols(width, start):
        return pl.BlockSpec((seq, width), lambda s: (b0 + s, start // width))

    def full(shape):
        return pl.BlockSpec(shape, lambda s: (0,) * len(shape))

    in_specs = [cols(D_SSD, P0_Z), cols(D_SSD, P0_X), cols(512, P0_BC), cols(LANES, 0),
                full((SSD_CONV, D_SSD)), full((SSD_CONV, 512)), full((1, D_SSD)), full((1, 512)),
                full((1, LANES)), full((1, LANES)), full((1, D_SSD)), full((1, D_SSD))]
    args = [p0, p0, p0, p0dt, cw[:, :D_SSD], cw[:, D_SSD:], cb[:D_SSD].reshape(1, -1), cb[D_SSD:].reshape(1, -1),
            dtb, alog, dskip, nw.reshape(1, -1)]
    st_spec = pl.BlockSpec((None, D_SSD, SSD_N), lambda s: (s, 0, 0))
    if has_h0:
        in_specs += [st_spec, st_spec]
        args += [h0f, h0b]
    st_shape = jax.ShapeDtypeStruct((nseq, D_SSD, SSD_N), F32)
    return pl.pallas_call(
        functools.partial(_ssd_kernel, seq=seq, has_h0=has_h0),
        grid=(nseq,), in_specs=in_specs,
        out_specs=[pl.BlockSpec((seq, D_SSD), lambda s: (s, 0)), st_spec, st_spec],
        out_shape=[jax.ShapeDtypeStruct((nseq * seq, D_SSD), BF16), st_shape, st_shape],
        scratch_shapes=[pltpu.VMEM((seq + 2 * SUBLANES, D_SSD), F32), pltpu.VMEM((seq + 2 * SUBLANES, 512), F32),
                        pltpu.VMEM((seq, D_SSD), F32), pltpu.VMEM((seq, 512), F32),
                        pltpu.VMEM((seq, LANES), F32), pltpu.VMEM((seq, LANES), F32),
                        pltpu.VMEM((seq, D_SSD), F32),
                        pltpu.VMEM((D_SSD, SSD_N), F32), pltpu.VMEM((D_SSD, SSD_N), F32)],
        compiler_params=_cparams(1),
        name=f"ssd{seq}",
    )(*args)


def _place_halves(tile, kv_in_high):
    lo = lax.broadcasted_iota(jnp.int32, tile.shape, 1) < ATT_HD
    swapped = pltpu.roll(tile, ATT_HD, 1)
    if kv_in_high:
        return jnp.where(lo, swapped, 0.0), jnp.where(lo, 0.0, tile)
    return jnp.where(lo, tile, 0.0), jnp.where(lo, 0.0, swapped)


def _place_rows(vt, kv_in_high):
    head = vt[ATT_HD:, :] if kv_in_high else vt[:ATT_HD, :]
    z = jnp.zeros_like(head)
    return jnp.concatenate([head, z], axis=0), jnp.concatenate([z, head], axis=0)


LOG2E = 1.4426950408889634
SCORE_SCALE = ATT_SCALE * LOG2E


def _sink_attend_t(score_parts, value_parts, sink2):
    m = sink2
    for s in score_parts:
        m = jnp.maximum(m, jnp.max(s, axis=0, keepdims=True))
    denom = jnp.exp2(sink2 - m)
    out = None
    for s, v in zip(score_parts, value_parts):
        p = jnp.exp2(s - m)
        denom = denom + jnp.sum(p, axis=0, keepdims=True)
        o = _mm(v, p)
        out = o if out is None else out + o
    return out * (1.0 / denom)


def _attn_schedule(n, scores, attend):
    scores(0)
    for j in range(n):
        if j + 1 < n:
            scores(j + 1)
        attend(j)


def _ctx_attn_kernel(q_ref, k_ref, v_ref, sink_ref, o_ref, kt_ref, vt_ref, s_a, s_b):
    sink2 = sink_ref[...] * LOG2E
    bufs = (s_a, s_b)
    half = SEQ // 2
    for t in range(ATT_KV_DIM // LANES):
        cols = slice(t * LANES, (t + 1) * LANES)
        kt_ref[cols, :] = jnp.transpose(k_ref[:, cols].astype(F32))
        vt_ref[cols, :] = jnp.transpose(v_ref[:, cols].astype(F32))

    def kv_tile(ref, j):
        return ref[:, (j // 2) * LANES:(j // 2 + 1) * LANES].astype(F32), (j % 2 == 1)

    def scores(j):
        k_lo, k_hi = _place_halves(*kv_tile(k_ref, j))
        qst = jnp.concatenate([q_ref[:, qt * LANES:(qt + 1) * LANES] for qt in (2 * j, 2 * j + 1)], axis=0)
        bufs[j % 2][...] = _mm_nt(jnp.concatenate([k_lo, k_hi], axis=0), qst) * SCORE_SCALE

    def attend(j):
        src = bufs[j % 2]
        v, high = kv_tile(v_ref, j)
        vts = _place_rows(jnp.transpose(v), high)
        for ql in range(2):
            qt = 2 * j + ql
            for qh in range(2):
                cols = slice(ql * SEQ + qh * half, ql * SEQ + (qh + 1) * half)
                acc = None
                for s, vv in enumerate(vts):
                    o = _sink_attend_t([src[s * SEQ:(s + 1) * SEQ, cols]], [vv], sink2[:, 2 * qt + s:2 * qt + s + 1])
                    acc = o if acc is None else acc + o
                o_ref[qh * half:(qh + 1) * half, qt * LANES:(qt + 1) * LANES] = jnp.transpose(acc).astype(o_ref.dtype)

    _attn_schedule(ATT_KV, scores, attend)


def _ctx_attn(p0, sink):
    def cols(width, start):
        return pl.BlockSpec((SEQ, width), lambda b: (b, start // width))

    sbuf = pltpu.VMEM((2 * SEQ, 2 * SEQ), F32)
    return pl.pallas_call(
        _ctx_attn_kernel,
        grid=(BATCH,),
        in_specs=[cols(D, P0_Q), cols(ATT_KV_DIM, P0_K), cols(ATT_KV_DIM, P0_V),
                  pl.BlockSpec((1, LANES), lambda b: (0, 0))],
        out_specs=[pl.BlockSpec((SEQ, D), lambda b: (b, 0))]
        + [pl.BlockSpec((None, ATT_KV_DIM, SEQ), lambda b: (b, 0, 0))] * 2,
        out_shape=[jax.ShapeDtypeStruct((T_CTX, D), BF16)] + [jax.ShapeDtypeStruct((BATCH, ATT_KV_DIM, SEQ), F32)] * 2,
        scratch_shapes=[sbuf, sbuf],
        compiler_params=_cparams(1),
        name="ctx_attn",
    )(p0, p0, p0, sink)


def _rope_tables():
    quarter = ATT_HD // 4
    t = np.arange(DEC_SEQ)
    lane = np.arange(LANES)
    inv = ROPE_BASE ** (-(lane % quarter).astype(np.float64) / quarter)
    pos = np.where(((lane % ATT_HD) < ATT_HD // 2)[None, :], (t // GRID_W)[:, None], (t % GRID_W)[:, None])
    ang = pos * inv[None, :]
    first = ((lane % (2 * quarter)) < quarter)[None, :]
    cos, sin = np.cos(ang), np.sin(ang)
    return (jnp.asarray(cos, F32), jnp.asarray(np.where(first, -sin, 0.0), F32),
            jnp.asarray(np.where(first, 0.0, sin), F32))


def _rope(x, cos, sa, sb):
    quarter = ATT_HD // 4
    return x * cos + pltpu.roll(x, LANES - quarter, 1) * sa + pltpu.roll(x, quarter, 1) * sb


def _lat_attn_kernel(q_ref, kp_ref, kc_ref, kn_ref, vp_ref, vc_ref, vn_ref, ck_ref, cv_ref,
                     cos_ref, sa_ref, sb_ref, sink_ref, o_ref, c_a, c_b, w_a, w_b):
    blk = pl.program_id(1)
    nb = pl.num_programs(1)
    B = ATT_BLOCK
    sink2 = sink_ref[...] * LOG2E
    cbufs, wbufs = (c_a, c_b), (w_a, w_b)

    def tables(b):
        r0 = pl.multiple_of(b * B, B)
        return cos_ref[pl.ds(r0, B), :], sa_ref[pl.ds(r0, B), :], sb_ref[pl.ds(r0, B), :]

    tq = tables(blk)
    tk = [tables(jnp.maximum(blk - 1, 0)), tq, tables(jnp.minimum(blk + 1, nb - 1))]
    kabs = (blk - 1) * B + lax.broadcasted_iota(jnp.int32, (3 * B, B), 0)
    qpos = blk * B + lax.broadcasted_iota(jnp.int32, (3 * B, B), 1)
    ok = (jnp.abs(qpos - kabs) <= WINDOW) & (kabs >= 0) & (kabs < nb * B)
    ok = jnp.concatenate([ok, ok], axis=1)
    ok = jnp.concatenate([ok, ok], axis=0)

    def scores(j):
        high = (j % 2 == 1)
        sl = slice((j // 2) * LANES, (j // 2 + 1) * LANES)
        kw = jnp.concatenate([_rope(r[:, sl].astype(F32), *tb) for r, tb in zip((kp_ref, kc_ref, kn_ref), tk)], axis=0)
        qs = [q_ref[:, qt * LANES:(qt + 1) * LANES].astype(F32) for qt in (2 * j, 2 * j + 1)]
        q_plain = jnp.concatenate(qs, axis=0)
        q_rope = jnp.concatenate([_rope(q, *tq) for q in qs], axis=0)
        ck = jnp.transpose(ck_ref[sl, :])
        cbufs[j % 2][...] = _mm_nt(jnp.concatenate(_place_halves(ck, high), axis=0), q_plain) * SCORE_SCALE
        win = _mm_nt(jnp.concatenate(_place_halves(kw, high), axis=0), q_rope) * SCORE_SCALE
        wbufs[j % 2][...] = jnp.where(ok, win, -jnp.inf)

    def attend(j):
        high = (j % 2 == 1)
        sl = slice((j // 2) * LANES, (j // 2 + 1) * LANES)
        vw = jnp.concatenate([jnp.transpose(r[:, sl].astype(F32)) for r in (vp_ref, vc_ref, vn_ref)], axis=1)
        vts = _place_rows(vw, high)
        cvts = _place_rows(cv_ref[sl, :], high)
        csrc, wsrc = cbufs[j % 2], wbufs[j % 2]
        for ql in range(2):
            qt = 2 * j + ql
            cols = slice(ql * B, (ql + 1) * B)
            acc = None
            for s in range(2):
                parts = [csrc[s * PAST_LEN:(s + 1) * PAST_LEN, cols], wsrc[s * 3 * B:(s + 1) * 3 * B, cols]]
                o = _sink_attend_t(parts, [cvts[s], vts[s]], sink2[:, 2 * qt + s:2 * qt + s + 1])
                acc = o if acc is None else acc + o
            o_ref[:, qt * LANES:(qt + 1) * LANES] = jnp.transpose(acc).astype(o_ref.dtype)

    _attn_schedule(ATT_KV, scores, attend)


def _lat_attn(p0, ck, cv, sink):
    nb = DEC_SEQ // ATT_BLOCK
    base = T_CTX // ATT_BLOCK

    def kv(start, shift):
        return pl.BlockSpec((ATT_BLOCK, ATT_KV_DIM),
                            lambda b, i: (base + b * nb + jnp.clip(i + shift, 0, nb - 1), start // ATT_KV_DIM))

    def full(shape):
        return pl.BlockSpec(shape, lambda b, i: (0,) * len(shape))

    cache = pl.BlockSpec((None, ATT_KV_DIM, PAST_LEN), lambda b, i: (b, 0, 0))
    cos, sa, sb = _rope_tables()
    return pl.pallas_call(
        _lat_attn_kernel,
        grid=(DEC_BATCH, nb),
        in_specs=[pl.BlockSpec((ATT_BLOCK, D), lambda b, i: (base + b * nb + i, P0_Q // D)),
                  kv(P0_K, -1), kv(P0_K, 0), kv(P0_K, 1), kv(P0_V, -1), kv(P0_V, 0), kv(P0_V, 1),
                  cache, cache, full((DEC_SEQ, LANES)), full((DEC_SEQ, LANES)), full((DEC_SEQ, LANES)),
                  full((1, LANES))],
        out_specs=pl.BlockSpec((ATT_BLOCK, D), lambda b, i: (b * nb + i, 0)),
        out_shape=jax.ShapeDtypeStruct((T_LAT, D), BF16),
        scratch_shapes=[pltpu.VMEM((2 * PAST_LEN, 2 * ATT_BLOCK), F32)] * 2
        + [pltpu.VMEM((2 * 3 * ATT_BLOCK, 2 * ATT_BLOCK), F32)] * 2,
        compiler_params=_cparams(2),
        name="lat_attn",
    )(p0, p0, p0, p0, p0, p0, p0, ck, cv, cos, sa, sb, sink)


def _log_sigmoid(x):
    return jnp.minimum(x, 0.0) - jnp.log(1.0 + jnp.exp(-jnp.abs(x)))


def _gla_kernel(*refs, seq, has_s0):
    if has_s0:
        (q_ref, k_ref, v_ref, r_ref, lr_ref, w2f_ref, w2b_ref, bf_ref, bb_ref, nw_ref, s0f_ref, s0b_ref,
         y_ref, sf_ref, sb_ref, gf, gb, yf, yb, stf, stb) = refs
    else:
        (q_ref, k_ref, v_ref, r_ref, lr_ref, w2f_ref, w2b_ref, bf_ref, bb_ref, nw_ref,
         y_ref, sf_ref, sb_ref, gf, gb, yf, yb, stf, stb) = refs
    C = GLA_C
    nc = seq // C
    lr = lr_ref[...]
    gf[...] = _log_sigmoid(_mm(lr, w2f_ref[...]) + bf_ref[...]) / GLA_GATE_NORM
    gb[...] = _log_sigmoid(_mm(lr, w2b_ref[...]) + bb_ref[...]) / GLA_GATE_NORM
    for h in range(GLA_HEADS):
        rows = slice(h * GLA_DV, (h + 1) * GLA_DV)
        if has_s0:
            stf[rows, :] = jnp.transpose(s0f_ref[h * GLA_DK:(h + 1) * GLA_DK, :])
            stb[rows, :] = jnp.transpose(s0b_ref[h * GLA_DK:(h + 1) * GLA_DK, :])
        else:
            stf[rows, :] = jnp.zeros((GLA_DV, GLA_DK), F32)
            stb[rows, :] = jnp.zeros((GLA_DV, GLA_DK), F32)

    row = lax.broadcasted_iota(jnp.int32, (C, C), 0)
    col = lax.broadcasted_iota(jnp.int32, (C, C), 1)
    qscale = GLA_DK ** -0.5

    def chunk(c, fwd):
        g_scr, y_scr, st = (gf, yf, stf) if fwd else (gb, yb, stb)
        r0 = pl.multiple_of(c * C, C)
        g = g_scr[pl.ds(r0, C), :]
        cs = _cumsum_rows(g, C)
        total = cs[C - 1:C, :]
        q = q_ref[pl.ds(r0, C), :].astype(F32) * qscale
        k = k_ref[pl.ds(r0, C), :].astype(F32)
        v = v_ref[pl.ds(r0, C), :]
        if fwd:
            qs, ks, ke = q * jnp.exp(cs), k * jnp.exp(-cs), k * jnp.exp(total - cs)
            keep = col <= row
        else:
            ex = cs - g
            qs, ks, ke = q * jnp.exp(total - ex), k * jnp.exp(ex - total), k * jnp.exp(ex)
            keep = col >= row
        dec = jnp.exp(total)
        for h in range(GLA_HEADS):
            kc = slice(h * GLA_DK, (h + 1) * GLA_DK)
            vc = slice(h * GLA_DV, (h + 1) * GLA_DV)
            s_t = st[vc, :]
            att = jnp.where(keep, _mm_nt(qs[:, kc], ks[:, kc]), 0.0)
            y_scr[pl.ds(r0, C), vc] = _mm(att, v[:, vc]) + _mm_nt(qs[:, kc], s_t)
            st[vc, :] = dec[:, kc] * s_t + _mm_tn(v[:, vc], ke[:, kc])

    def body(i, carry):
        chunk(i, True)
        chunk(nc - 1 - i, False)
        return carry

    lax.fori_loop(0, nc, body, 0, unroll=4)

    nw = nw_ref[...]
    for blk in range(seq // LANES):
        rs = slice(blk * LANES, (blk + 1) * LANES)
        y = yf[rs, :] + yb[rs, :]
        gate = _silu(r_ref[rs, :].astype(F32))
        for h in range(GLA_HEADS):
            vc = slice(h * GLA_DV, (h + 1) * GLA_DV)
            y_ref[rs, vc] = (_rms(y[:, vc], nw) * gate[:, vc]).astype(y_ref.dtype)
    for h in range(GLA_HEADS):
        rows = slice(h * GLA_DV, (h + 1) * GLA_DV)
        sf_ref[h * GLA_DK:(h + 1) * GLA_DK, :] = jnp.transpose(stf[rows, :])
        sb_ref[h * GLA_DK:(h + 1) * GLA_DK, :] = jnp.transpose(stb[rows, :])


def _gla(p1, p1lr, tok0, nseq, seq, w2f, w2b, bgf, bgb, nw, s0f=None, s0b=None):
    has_s0 = s0f is not None
    b0 = tok0 // seq
    dk_all, dv_all = GLA_HEADS * GLA_DK, GLA_HEADS * GLA_DV

    def cols(width, start):
        return pl.BlockSpec((seq, width), lambda s: (b0 + s, start // width))

    def full(shape):
        return pl.BlockSpec(shape, lambda s: (0,) * len(shape))

    in_specs = [cols(dk_all, P1_Q), cols(dk_all, P1_K), cols(dv_all, P1_V), cols(dv_all, P1_R), cols(LANES, 0),
                full((LANES, dk_all)), full((LANES, dk_all)), full((1, dk_all)), full((1, dk_all)), full((1, GLA_DV))]
    args = [p1, p1, p1, p1, p1lr, w2f, w2b, bgf.reshape(1, -1), bgb.reshape(1, -1), nw.reshape(1, -1)]
    st_spec = pl.BlockSpec((None, dk_all, GLA_DV), lambda s: (s, 0, 0))
    if has_s0:
        in_specs += [st_spec, st_spec]
        args += [s0f, s0b]
    st_shape = jax.ShapeDtypeStruct((nseq, dk_all, GLA_DV), F32)
    return pl.pallas_call(
        functools.partial(_gla_kernel, seq=seq, has_s0=has_s0),
        grid=(nseq,), in_specs=in_specs,
        out_specs=[pl.BlockSpec((seq, dv_all), lambda s: (s, 0)), st_spec, st_spec],
        out_shape=[jax.ShapeDtypeStruct((nseq * seq, dv_all), BF16), st_shape, st_shape],
        scratch_shapes=[pltpu.VMEM((seq, dk_all), F32), pltpu.VMEM((seq, dk_all), F32),
                        pltpu.VMEM((seq, dv_all), F32), pltpu.VMEM((seq, dv_all), F32),
                        pltpu.VMEM((dv_all, GLA_DK), F32), pltpu.VMEM((dv_all, GLA_DK), F32)],
        compiler_params=_cparams(1),
        name=f"gla{seq}",
    )(*args)


ROUTE_TM = 512
ROUTE_SUB = 256
ROUTE_ROWS = 32


def _split_bf16(x):
    hi = x.astype(BF16)
    return hi, (x - hi.astype(F32)).astype(BF16)


def _outproj_kernel(*refs, n_in, dual_x):
    y_refs = refs[:2 * n_in]
    n_x = 2 if dual_x else 1
    x_refs = refs[2 * n_in + 1:2 * n_in + 1 + n_x]
    w_ref = refs[2 * n_in]
    (g1_ref, sh_ref, sc_ref, nw_ref, wr_ref, br_ref,
     xo_ref, tr_ref, route_ref, cnt_ref, w_scr, wr_hl, carry) = refs[2 * n_in + 1 + n_x:]
    is_ctx = pl.program_id(0) < T_CTX // ROUTE_TM

    @pl.when(pl.program_id(0) == 0)
    def _():
        w_scr[...] = w_ref[...].astype(BF16)
        hi, lo = _split_bf16(jnp.transpose(wr_ref[...]))
        wr_hl[0:LANES, :] = hi
        wr_hl[LANES:2 * LANES, :] = lo
        carry[...] = jnp.zeros(carry.shape, F32)

    for sub in range(ROUTE_TM // ROUTE_SUB):
        _outproj_subtile(sub, is_ctx, y_refs, x_refs, dual_x, n_in, g1_ref, sh_ref, sc_ref, nw_ref, br_ref,
                         xo_ref, tr_ref, route_ref, cnt_ref, w_scr, wr_hl, carry)


def _outproj_subtile(sub, is_ctx, y_refs, x_refs, dual_x, n_in, g1_ref, sh_ref, sc_ref, nw_ref, br_ref,
                     xo_ref, tr_ref, route_ref, cnt_ref, w_scr, wr_hl, carry):
    tm = ROUTE_SUB
    rows = slice(sub * tm, (sub + 1) * tm)
    o = None
    for i in range(n_in):
        y = jnp.where(is_ctx, y_refs[2 * i][rows, :], y_refs[2 * i + 1][rows, :])
        d = jnp.dot(y, w_scr[i * D:(i + 1) * D, :], preferred_element_type=F32)
        o = d if o is None else o + d
    x_in = jnp.where(is_ctx, x_refs[0][rows, :], x_refs[1][rows, :]) if dual_x else x_refs[0][rows, :]
    x = x_in + g1_ref[0] * o
    xo_ref[rows, :] = x
    t = _rms(x, nw_ref[...]) * (1.0 + sc_ref[0]) + sh_ref[0]
    for k in range(D // LANES):
        tr_ref[pl.ds(sub * tm * SUBLANES + k, tm, stride=SUBLANES), :] = t[:, k * LANES:(k + 1) * LANES]

    t_hi, t_lo = _split_bf16(t)
    lg = _mm_nt(wr_hl[...], t_hi)
    nr = ROUTE_ROWS
    logit = lg[0:nr, :] + lg[LANES:LANES + nr, :] + _mm_nt(wr_hl[0:LANES, :], t_lo)[0:nr, :] + br_ref[0:nr, :]
    rowf = lax.broadcasted_iota(jnp.int32, (nr, tm), 0).astype(F32)
    neg = -jnp.inf

    def first_argmax(v, vmax):
        return jnp.min(jnp.where(v == vmax, rowf, float(LANES)), axis=0, keepdims=True)

    gl = jnp.where(rowf < N_GROUPS, logit, neg)
    gmax = jnp.max(gl, axis=0, keepdims=True)
    gsel = first_argmax(gl, gmax)
    gprob = 1.0 / jnp.sum(jnp.exp(gl - gmax), axis=0, keepdims=True)
    first = N_GROUPS + EXP_PER_GROUP * gsel
    el = jnp.where((rowf >= first) & (rowf < first + EXP_PER_GROUP), logit, neg)
    m1 = jnp.max(el, axis=0, keepdims=True)
    i1 = first_argmax(el, m1)
    el2 = jnp.where(rowf == i1, neg, el)
    m2 = jnp.max(el2, axis=0, keepdims=True)
    i2 = first_argmax(el2, m2)
    e2 = jnp.exp(m2 - m1)
    c1 = gprob / (1.0 + e2)
    c2 = gprob * e2 / (1.0 + e2)
    x1 = i1 - N_GROUPS
    x2 = i2 - N_GROUPS

    erow = rowf
    hot = ((erow == x1) | (erow == x2)).astype(F32)
    tri = (lax.broadcasted_iota(jnp.int32, (tm, tm), 0) < lax.broadcasted_iota(jnp.int32, (tm, tm), 1))
    before = _mm(hot, tri.astype(F32)) + carry[...]
    r1 = jnp.sum(jnp.where(erow == x1, before, 0.0), axis=0, keepdims=True)
    r2 = jnp.sum(jnp.where(erow == x2, before, 0.0), axis=0, keepdims=True)
    total = carry[...] + _mm(hot, jnp.ones((tm, tm), F32))
    carry[...] = total
    cnt_ref[...] = total[0:N_EXPERTS, 0:LANES]
    row8 = lax.broadcasted_iota(jnp.int32, (SUBLANES, tm), 0)
    out = jnp.zeros((SUBLANES, tm), F32)
    for k, v in enumerate((x1, x2, c1, c2, r1, r2)):
        out = jnp.where(row8 == k, jnp.broadcast_to(v, (SUBLANES, tm)), out)
    route_ref[:, rows] = out


def _outproj_route(ys, w_out, xs, mods, layer, norm_w, w_router, b_router):
    tm = ROUTE_TM
    n_in = len(ys) // 2
    dual_x = len(xs) == 2
    kdim = w_out.shape[0]

    def full(shape):
        return pl.BlockSpec(shape, lambda i: (0,) * len(shape))

    tile = pl.BlockSpec((tm, D), lambda i: (i, 0))
    pair = list(_ctx_lat_specs(tm))
    in_specs = (pair * n_in + [full((kdim, D))] + (pair if dual_x else [tile])
                + [_mod_spec(layer, 2, tm), _mod_spec(layer, 3, tm), _mod_spec(layer, 4, tm),
                   full((1, D)), full((D, LANES)), full((LANES, ROUTE_SUB))])
    cnt = jax.ShapeDtypeStruct((N_EXPERTS, LANES), F32)
    return pl.pallas_call(
        functools.partial(_outproj_kernel, n_in=n_in, dual_x=dual_x),
        grid=(T // tm,), in_specs=in_specs,
        out_specs=[tile, pl.BlockSpec((tm * SUBLANES, LANES), lambda i: (i, 0)),
                   pl.BlockSpec((SUBLANES, tm), lambda i: (0, i)), full(cnt.shape)],
        out_shape=[jax.ShapeDtypeStruct((T, D), F32), jax.ShapeDtypeStruct((T * SUBLANES, LANES), F32),
                   jax.ShapeDtypeStruct((SUBLANES, T), F32), cnt],
        scratch_shapes=[pltpu.VMEM((kdim, D), BF16), pltpu.VMEM((2 * LANES, D), BF16),
                        pltpu.VMEM((ROUTE_ROWS, ROUTE_SUB), F32)],
        compiler_params=_cparams(1),
        name=f"outproj{layer}",
    )(*ys, w_out, *xs, mods, mods, mods, norm_w.reshape(1, D), w_router,
      jnp.broadcast_to(b_router.reshape(LANES, 1), (LANES, ROUTE_SUB)))


def _moe_meta(counts):
    tm = MOE_TM
    experts = jnp.arange(N_EXPERTS, dtype=jnp.int32)
    counts = jnp.max(counts, axis=1).astype(jnp.int32)
    padded = ((counts + tm - 1) // tm) * tm
    ends = jnp.cumsum(padded)
    tile_start = jnp.arange(MOE_TILES, dtype=jnp.int32) * tm
    te = jnp.sum((tile_start[:, None] >= ends[None, :]).astype(jnp.int32), axis=1)
    last = jnp.max(jnp.where(counts > 0, experts, 0))
    meta = jnp.concatenate([jnp.minimum(te, last), ends[-1:] // tm]).astype(jnp.int32)
    starts = ends - padded
    pads = jnp.concatenate([starts + counts, ends[-1:], ends, jnp.full((1,), MOE_ROWS)]).astype(jnp.int32)
    return starts.astype(jnp.int32), pads, meta


def _expert_changed(meta_ref, j):
    return (j == 0) | (meta_ref[j] != meta_ref[jnp.maximum(j - 1, 0)])


def _moe_up_kernel(pos1_ref, pos2_ref, pads_ref, meta_ref, tr_hbm, wg_ref, wu_ref, a_ref, rowmap_ref,
                   tr_scr, g0, g1, wg_bf, wu_bf, sem):
    j = pl.program_id(0)
    tm = MOE_TM
    ntiles = meta_ref[MOE_TILES]

    def gather(tile, dst):
        for mi in range(tm):
            tok = jnp.minimum(rowmap_ref[tile * tm + mi] >> 1, T - 1)
            dst[mi * SUBLANES:(mi + 1) * SUBLANES, :] = tr_scr[pl.ds(pl.multiple_of(tok * SUBLANES, SUBLANES), SUBLANES), :]

    @pl.when(j == 0)
    def _():
        load = pltpu.make_async_copy(tr_hbm, tr_scr, sem)
        load.start()

        def clear(c, carry):
            for i in range(SUBLANES):
                rowmap_ref[c * SUBLANES + i] = 2 * T
            return carry
        for k in range(N_EXPERTS + 1):
            lax.fori_loop(pads_ref[k] // SUBLANES, pads_ref[N_EXPERTS + 1 + k] // SUBLANES, clear, 0)

        def place(t, carry):
            rowmap_ref[pos1_ref[t]] = 2 * t
            rowmap_ref[pos2_ref[t]] = 2 * t + 1
            return carry
        lax.fori_loop(0, T, place, 0, unroll=8)
        load.wait()
        gather(0, g0)

    def compute(cur, nxt):
        gather(jnp.minimum(j + 1, ntiles - 1), nxt)
        x = _tokmajor_to_std(cur, tm).astype(BF16)
        g = jnp.dot(x, wg_bf[...], preferred_element_type=F32)
        u = jnp.dot(x, wu_bf[...], preferred_element_type=F32)
        a_ref[...] = (_silu(g) * u).astype(a_ref.dtype)

    @pl.when(j < ntiles)
    def _():
        @pl.when(_expert_changed(meta_ref, j))
        def _():
            wg_bf[...] = wg_ref[...].astype(BF16)
            wu_bf[...] = wu_ref[...].astype(BF16)

        pl.when(j % 2 == 0)(functools.partial(compute, g0, g1))
        pl.when(j % 2 == 1)(functools.partial(compute, g1, g0))

    @pl.when(j >= ntiles)
    def _():
        a_ref[...] = jnp.zeros(a_ref.shape, a_ref.dtype)


def _moe_down_kernel(rowmap_ref, cpair_ref, meta_ref, a_ref, wd_ref, out_hbm, acc, y0, y1, wd_bf):
    j = pl.program_id(0)
    tm = MOE_TM
    zrows = 512
    ntiles = meta_ref[MOE_TILES]

    @pl.when(j == 0)
    def _():
        def zero(i, carry):
            acc[pl.ds(pl.multiple_of(i * zrows, zrows), zrows), :] = jnp.zeros((zrows, LANES), F32)
            return carry
        lax.fori_loop(0, ACC_ROWS // zrows, zero, 0)

    def matmul(dst):
        y = jnp.dot(a_ref[...], wd_bf[...], preferred_element_type=F32)
        for k in range(D // LANES):
            dst[pl.ds(k, tm, stride=SUBLANES), :] = y[:, k * LANES:(k + 1) * LANES]

    def scatter(tile, src):
        for b in range(tm // SUBLANES):
            ents = [rowmap_ref[tile * tm + b * SUBLANES + i] for i in range(SUBLANES)]
            offs = [pl.multiple_of((e >> 1) * SUBLANES, SUBLANES) for e in ents]
            olds = [acc[pl.ds(o, SUBLANES), :] for o in offs]
            for i, o in enumerate(offs):
                r = (b * SUBLANES + i) * SUBLANES
                acc[pl.ds(o, SUBLANES), :] = olds[i] + cpair_ref[ents[i]] * src[r:r + SUBLANES, :]

    has_mm = j < ntiles
    has_sc = (j >= 1) & (j <= ntiles)

    @pl.when(has_mm)
    def _():
        @pl.when(_expert_changed(meta_ref, j))
        def _():
            wd_bf[...] = wd_ref[...].astype(BF16)

    for par, (cur, prev) in enumerate(((y0, y1), (y1, y0))):
        mine = (j % 2) == par

        @pl.when(mine & has_mm & has_sc)
        def _():
            matmul(cur)
            scatter(j - 1, prev)

        @pl.when(mine & has_mm & jnp.logical_not(has_sc))
        def _():
            matmul(cur)

        @pl.when(mine & jnp.logical_not(has_mm) & has_sc)
        def _():
            scatter(j - 1, prev)

    @pl.when(j == pl.num_programs(0) - 1)
    def _():
        pltpu.sync_copy(acc.at[0:T * SUBLANES, :], out_hbm)


def _moe(tr, route_t, counts, layer, w_gate, w_up, w_down):
    tm = MOE_TM
    starts, pads, meta = _moe_meta(counts)
    experts = jnp.arange(N_EXPERTS, dtype=jnp.int32)

    def position(e, r):
        sel = e.astype(jnp.int32)[:, None] == experts[None, :]
        return jnp.sum(jnp.where(sel, starts[None, :], 0), axis=1) + r.astype(jnp.int32)

    pos1 = position(route_t[0], route_t[4])
    pos2 = position(route_t[1], route_t[5])
    cpair = jnp.concatenate([jnp.stack([route_t[2], route_t[3]], axis=1).reshape(2 * T), jnp.zeros((2,), F32)])

    def wspec(shape, n):
        return pl.BlockSpec((None, None) + shape,
                            lambda j, *pre: (layer, pre[n - 1][jnp.minimum(j, MOE_TILES - 1)], 0, 0))

    gscr = pltpu.VMEM((tm * SUBLANES, LANES), F32)
    act, rowmap = pl.pallas_call(
        _moe_up_kernel,
        grid_spec=pltpu.PrefetchScalarGridSpec(
            num_scalar_prefetch=4, grid=(MOE_TILES,),
            in_specs=[pl.BlockSpec(memory_space=pl.ANY), wspec((D, D_EXPERT), 4), wspec((D, D_EXPERT), 4)],
            out_specs=[pl.BlockSpec((tm, D_EXPERT), lambda j, *pre: (j, 0)), pl.BlockSpec(memory_space=pltpu.SMEM)],
            scratch_shapes=[pltpu.VMEM((T * SUBLANES, LANES), F32), gscr, gscr,
                            pltpu.VMEM((D, D_EXPERT), BF16), pltpu.VMEM((D, D_EXPERT), BF16),
                            pltpu.SemaphoreType.DMA(())]),
        out_shape=[jax.ShapeDtypeStruct((MOE_ROWS, D_EXPERT), BF16), jax.ShapeDtypeStruct((MOE_ROWS,), jnp.int32)],
        compiler_params=_cparams(1),
        name=f"moe_up{layer}",
    )(pos1, pos2, pads, meta, tr, w_gate, w_up)

    return pl.pallas_call(
        _moe_down_kernel,
        grid_spec=pltpu.PrefetchScalarGridSpec(
            num_scalar_prefetch=3, grid=(MOE_TILES + 1,),
            in_specs=[pl.BlockSpec((tm, D_EXPERT), lambda j, *pre: (jnp.minimum(j, MOE_TILES - 1), 0)),
                      wspec((D_EXPERT, D), 3)],
            out_specs=pl.BlockSpec(memory_space=pl.ANY),
            scratch_shapes=[pltpu.VMEM((ACC_ROWS, LANES), F32), gscr, gscr, pltpu.VMEM((D_EXPERT, D), BF16)]),
        out_shape=jax.ShapeDtypeStruct((T * SUBLANES, LANES), F32),
        compiler_params=_cparams(1),
        name=f"moe_down{layer}",
    )(rowmap, cpair, meta, act, w_down)


FINAL_TM = 512


def _final_kernel(x_ref, moe_ref, g2_ref, nw_ref, oc_ref, ol_ref):
    x = x_ref[...] + g2_ref[0] * _tokmajor_to_std(moe_ref, FINAL_TM)
    y = _rms(x, nw_ref[...])
    is_ctx = pl.program_id(0) < T_CTX // FINAL_TM

    @pl.when(is_ctx)
    def _():
        oc_ref[...] = y

    @pl.when(jnp.logical_not(is_ctx))
    def _():
        ol_ref[...] = y


def _final(x, moe, mods, layer, norm_w):
    tm = FINAL_TM
    tile = pl.BlockSpec((tm, D), lambda i: (i, 0))
    return pl.pallas_call(
        _final_kernel,
        grid=(T // tm,),
        in_specs=[tile, pl.BlockSpec((tm * SUBLANES, LANES), lambda i: (i, 0)), _mod_spec(layer, 5, tm),
                  pl.BlockSpec((1, D), lambda i: (0, 0))],
        out_specs=list(_ctx_lat_specs(tm)),
        out_shape=[jax.ShapeDtypeStruct((T_CTX, D), F32), jax.ShapeDtypeStruct((T_LAT, D), F32)],
        compiler_params=_cparams(1),
        name="final_norm",
    )(x, moe, mods, norm_w.reshape(1, D))


def _pad_lanes(v):
    return jnp.pad(v.astype(F32), (0, LANES - v.shape[0])).reshape(1, LANES)


def kernel(x_prompt, x_sample, cache_k_attn, cache_v_attn, state_ssd_fwd, state_ssd_bwd, state_gla_fwd, state_gla_bwd, c, c_ctx, w_ada, b_ada, norm_mix_w, norm_ffn_w, w_in_even, conv_w, conv_b, dt_bias_fwd, dt_bias_bwd, a_log_fwd, a_log_bwd, d_skip, ssd_norm_w, attn_sink, w_out_even, w_in_odd, w_gk2_fwd, b_gk_fwd, w_gk2_bwd, b_gk_bwd, gla_norm_w, w_out_odd, w_router_group, b_router_group, w_router_expert, b_router_expert, w_gate_exp, w_up_exp, w_down_exp, final_norm_w):
    depth = w_ada.shape[0]
    assert depth == 2 and x_prompt.shape == (BATCH, SEQ, D) and x_sample.shape == (DEC_BATCH, DEC_SEQ, D)

    cond8 = jnp.concatenate([c_ctx[None, :], c, jnp.zeros((SUBLANES - 1 - DEC_BATCH, D), F32)], axis=0)
    mods = _adaln(cond8, w_ada, b_ada).reshape(depth * SUBLANES * 6, 1, D)
    xs0 = (x_prompt.reshape(T_CTX, D), x_sample.reshape(T_LAT, D))

    def router_params(i):
        wr = jnp.concatenate([w_router_group[i], w_router_expert[i],
                              jnp.zeros((D, LANES - N_GROUPS - N_EXPERTS), F32)], axis=1)
        return wr, _pad_lanes(jnp.concatenate([b_router_group[i], b_router_expert[i]]))

    n_zxbc = 2 * D_SSD + 2 * SSD_GROUPS * SSD_N
    n_dt = 2 * SSD_HEADS
    bc_tile, q_tiles = n_zxbc // PROJ_TN - 1, D // PROJ_TN
    rows0 = [k * PROJ_TN for k in range(bc_tile + 1)] + [n_zxbc + n_dt + k * PROJ_TN for k in range(q_tiles + 1)]

    def out_block0(j):
        return jnp.where(j == bc_tile, bc_tile + q_tiles, jnp.where((j > bc_tile) & (j <= bc_tile + q_tiles), j - 1, j))

    p0, p0dt = _modproj(xs0, mods, 0, norm_mix_w[0], jnp.transpose(w_in_even[0]), rows0, n_zxbc, out_block0)

    dtb = _pad_lanes(jnp.concatenate([dt_bias_fwd[0], dt_bias_bwd[0]]))
    alog = _pad_lanes(jnp.concatenate([a_log_fwd[0], a_log_bwd[0]]))
    dskip = jnp.repeat(d_skip[0], SSD_P).reshape(1, D_SSD)
    ssd_args = (conv_w[0], conv_b[0], dtb, alog, dskip, ssd_norm_w[0])
    y_ssd_c, ssd_f, ssd_b = _ssd(p0, p0dt, 0, BATCH, SEQ, *ssd_args)
    y_ssd_l, _, _ = _ssd(p0, p0dt, T_CTX, DEC_BATCH, DEC_SEQ, *ssd_args,
                         h0f=state_ssd_fwd[:, 0].reshape(DEC_BATCH, D_SSD, SSD_N),
                         h0b=state_ssd_bwd[:, 0].reshape(DEC_BATCH, D_SSD, SSD_N))
    sink = _pad_lanes(attn_sink[0])
    y_att_c, new_kt, new_vt = _ctx_attn(p0, sink)
    def cache_in(t):
        return jnp.transpose(t[:, 0], (0, 2, 3, 1)).reshape(DEC_BATCH, ATT_KV_DIM, PAST_LEN)

    y_att_l = _lat_attn(p0, cache_in(cache_k_attn), cache_in(cache_v_attn), sink)
    xmid0, tr0, route0, cnt0 = _outproj_route([y_ssd_c, y_ssd_l, y_att_c, y_att_l], w_out_even[0], xs0, mods, 0,
                                              norm_ffn_w[0], *router_params(0))
    moe0 = _moe(tr0, route0, cnt0, 0, w_gate_exp, w_up_exp, w_down_exp)

    dk_all = GLA_HEADS * GLA_DK
    n_qkvr = 2 * dk_all + 2 * GLA_HEADS * GLA_DV
    n_odd = w_in_odd.shape[2]
    p1, p1lr, x1 = _modproj(xmid0, mods, 1, norm_mix_w[1], jnp.transpose(w_in_odd[0]),
                            [k * PROJ_TN for k in range(n_qkvr // PROJ_TN)], n_odd - LANES, lambda j: j, moe=moe0)
    lr0 = LANES - 2 * GLA_LOWRANK
    w2f = jnp.zeros((LANES, dk_all), F32).at[lr0:lr0 + GLA_LOWRANK].set(w_gk2_fwd[0])
    w2b = jnp.zeros((LANES, dk_all), F32).at[lr0 + GLA_LOWRANK:].set(w_gk2_bwd[0])
    gla_args = (w2f, w2b, b_gk_fwd[0], b_gk_bwd[0], gla_norm_w[0])
    y_gla_c, gla_f, gla_b = _gla(p1, p1lr, 0, BATCH, SEQ, *gla_args)
    y_gla_l, _, _ = _gla(p1, p1lr, T_CTX, DEC_BATCH, DEC_SEQ, *gla_args,
                         s0f=state_gla_fwd[:, 0].reshape(DEC_BATCH, dk_all, GLA_DV),
                         s0b=state_gla_bwd[:, 0].reshape(DEC_BATCH, dk_all, GLA_DV))
    xmid1, tr1, route1, cnt1 = _outproj_route([y_gla_c, y_gla_l], w_out_odd[0], (x1,), mods, 1,
                                              norm_ffn_w[1], *router_params(1))
    moe1 = _moe(tr1, route1, cnt1, 1, w_gate_exp, w_up_exp, w_down_exp)
    y_c, y_l = _final(xmid1, moe1, mods, 1, final_norm_w)

    y_prompt = y_c.reshape(BATCH, SEQ, D)
    y_sample = y_l.reshape(DEC_BATCH, DEC_SEQ, D)
    def cache_out(t):
        return jnp.transpose(t.reshape(BATCH, 1, ATT_KV, ATT_HD, SEQ), (0, 1, 4, 2, 3))

    new_k, new_v = cache_out(new_kt), cache_out(new_vt)
    return (y_prompt, y_sample, new_k, new_v,
            ssd_f.reshape(BATCH, 1, SSD_HEADS, SSD_P, SSD_N), ssd_b.reshape(BATCH, 1, SSD_HEADS, SSD_P, SSD_N),
            gla_f.reshape(BATCH, 1, GLA_HEADS, GLA_DK, GLA_DV), gla_b.reshape(BATCH, 1, GLA_HEADS, GLA_DK, GLA_DV))
```

```python
import functools
import math

import numpy as np
import jax
import jax.numpy as jnp
from jax import lax
from jax.experimental import pallas as pl
from jax.experimental.pallas import tpu as pltpu

F32 = jnp.float32
BF16 = jnp.bfloat16

D = 1024
BATCH, SEQ = 16, 256
DEC_BATCH, DEC_SEQ = 2, 1024
PAST_LEN = 512
GRID_W = 64
EPS = 1e-6
T_CTX = BATCH * SEQ
T_LAT = DEC_BATCH * DEC_SEQ
T = T_CTX + T_LAT

SSD_HEADS, SSD_P, SSD_N, SSD_GROUPS = 16, 64, 128, 2
SSD_CONV = 5
SSD_L = 128
D_SSD = SSD_HEADS * SSD_P
HEADS_PER_GROUP = SSD_HEADS // SSD_GROUPS
GROUP_W = HEADS_PER_GROUP * SSD_P

ATT_HEADS, ATT_KV, ATT_HD = 16, 4, 64
ATT_KV_DIM = ATT_KV * ATT_HD
WINDOW = 128
ATT_BLOCK = 128
ATT_SCALE = ATT_HD ** -0.5
ROPE_BASE = 10000.0

GLA_HEADS, GLA_DK, GLA_DV = 4, 128, 256
GLA_C = 64
GLA_GATE_NORM = 16.0
GLA_LOWRANK = 16

N_GROUPS, EXP_PER_GROUP = 4, 4
N_EXPERTS = 16
D_EXPERT = 512

LANES = 128
SUBLANES = 8
VMEM_LIMIT = 56 * 1024 * 1024

P0_Z, P0_X, P0_Q, P0_BC, P0_K, P0_V = 0, 1024, 2048, 3072, 3584, 3840
P0_W = 4096
P1_Q, P1_K, P1_V, P1_R = 0, 512, 1024, 2048
P1_W = 3072

MOE_TM = 512
MOE_TILES = (2 * T) // MOE_TM + N_EXPERTS
MOE_ROWS = MOE_TILES * MOE_TM
ACC_ROWS = (T + SUBLANES) * SUBLANES


def _cparams(n_axes, vmem=VMEM_LIMIT):
    return pltpu.CompilerParams(dimension_semantics=("arbitrary",) * n_axes, vmem_limit_bytes=vmem)


def _silu(x):
    return x / (1.0 + jnp.exp(-x))


def _softplus(x):
    return jnp.maximum(x, 0.0) + jnp.log(1.0 + jnp.exp(-jnp.abs(x)))


def _mm(a, b):
    return jnp.dot(a.astype(BF16), b.astype(BF16), preferred_element_type=F32)


def _mm_nt(a, b):
    return lax.dot_general(a.astype(BF16), b.astype(BF16), (((1,), (1,)), ((), ())),
                           preferred_element_type=F32)


def _mm_tn(a, b):
    return lax.dot_general(a.astype(BF16), b.astype(BF16), (((0,), (0,)), ((), ())),
                           preferred_element_type=F32)


def _rms(x, w):
    return x * lax.rsqrt(jnp.mean(x * x, axis=-1, keepdims=True) + EPS) * w


def _cumsum_rows(x, n):
    row = lax.broadcasted_iota(jnp.int32, x.shape, 0)
    s = 1
    while s < n:
        x = x + jnp.where(row >= s, pltpu.roll(x, s, 0), 0.0)
        s *= 2
    return x


def _mod_row(tok0):
    return jnp.where(tok0 < T_CTX, 0, 1 + (tok0 - T_CTX) // DEC_SEQ)


ADA_TN = 1536


def _adaln_kernel(c_ref, w_ref, b_ref, o_ref):
    s = _silu(c_ref[...])
    o_ref[0] = _mm(s, w_ref[0]) + b_ref[0]


def _adaln(cond8, w_ada, b_ada):
    depth = w_ada.shape[0]
    return pl.pallas_call(
        _adaln_kernel,
        grid=(depth, 6 * D // ADA_TN),
        in_specs=[
            pl.BlockSpec((SUBLANES, D), lambda l, j: (0, 0)),
            pl.BlockSpec((1, D, ADA_TN), lambda l, j: (l, 0, j)),
            pl.BlockSpec((1, 1, ADA_TN), lambda l, j: (l, 0, j)),
        ],
        out_specs=pl.BlockSpec((1, SUBLANES, ADA_TN), lambda l, j: (l, 0, j)),
        out_shape=jax.ShapeDtypeStruct((depth, SUBLANES, 6 * D), F32),
        compiler_params=_cparams(2),
        name="adaln",
    )(cond8, w_ada, b_ada.reshape(depth, 1, 6 * D))


def _mod_spec(layer, chunk, tm, tile_of=lambda i, *_: i):
    return pl.BlockSpec((1, 1, D), lambda *g: ((layer * SUBLANES + _mod_row(tile_of(*g) * tm)) * 6 + chunk, 0, 0))


def _tokmajor_to_std(ref, tm):
    return jnp.concatenate([ref[pl.ds(k, tm, stride=SUBLANES), :] for k in range(D // LANES)], axis=1)


PROJ_TM = 1024
PROJ_TN = 512
PROJ_TN1 = 1024


def _ctx_lat_specs(tm, width=D):
    n_ctx = T_CTX // tm
    return (pl.BlockSpec((tm, width), lambda i, *_: (jnp.minimum(i, n_ctx - 1), 0)),
            pl.BlockSpec((tm, width), lambda i, *_: (jnp.maximum(i - n_ctx, 0), 0)))


def _modproj_kernel(*refs, dual_x):
    it = iter(refs)
    if dual_x:
        xc_ref, xl_ref = next(it), next(it)
    else:
        x_ref, moe_ref, g2_ref = next(it), next(it), next(it)
    sh_ref, sc_ref, nw_ref, w_ref, ws_ref, o_ref, os_ref = (next(it) for _ in range(7))
    xo_ref = None if dual_x else next(it)
    h_all, w_bf = next(it), next(it)
    j, i = pl.program_id(0), pl.program_id(1)
    tm = PROJ_TM
    rows = pl.ds(pl.multiple_of(i * tm, tm), tm)

    @pl.when(j == 0)
    def _():
        if dual_x:
            x = jnp.where(i < T_CTX // tm, xc_ref[...], xl_ref[...])
        else:
            x = x_ref[...] + g2_ref[0] * _tokmajor_to_std(moe_ref, tm)
            xo_ref[...] = x
        h = (_rms(x, nw_ref[...]) * (1.0 + sc_ref[0]) + sh_ref[0]).astype(BF16)
        h_all[rows, :] = h
        os_ref[...] = _mm_nt(h, ws_ref[...])

    @pl.when(i == 0)
    def _():
        w_bf[...] = w_ref[...].astype(BF16)

    o_ref[...] = _mm_nt(h_all[rows, :], w_bf[...]).astype(o_ref.dtype)


def _modproj(xs, mods, layer, norm_w, wt, tile_rows, small_row, out_block, moe=None, tn=PROJ_TN):
    tm = PROJ_TM
    dual_x = moe is None
    n_tiles = len(tile_rows)
    n_i, n_ctx = T // tm, T_CTX // tm

    def w_row(j, i):
        r = jnp.int32(tile_rows[0])
        for k in range(1, n_tiles):
            r = jnp.where(j == k, tile_rows[k], r)
        return pl.multiple_of(r, SUBLANES), 0

    def tok(j, i):
        return jnp.where(j == 0, i, n_i - 1)

    tile = pl.BlockSpec((tm, D), lambda j, i: (tok(j, i), 0))
    if dual_x:
        in_specs = [pl.BlockSpec((tm, D), lambda j, i: (jnp.minimum(tok(j, i), n_ctx - 1), 0)),
                    pl.BlockSpec((tm, D), lambda j, i: (jnp.maximum(tok(j, i) - n_ctx, 0), 0))]
        args = list(xs)
    else:
        in_specs = [tile, pl.BlockSpec((tm * SUBLANES, LANES), lambda j, i: (tok(j, i), 0)),
                    _mod_spec(layer - 1, 5, tm, tok)]
        args = [xs, moe, mods]
    in_specs += [_mod_spec(layer, 0, tm, tok), _mod_spec(layer, 1, tm, tok), pl.BlockSpec((1, D), lambda j, i: (0, 0)),
                 pl.BlockSpec((pl.Element(tn), pl.Element(D)), w_row),
                 pl.BlockSpec((pl.Element(LANES), pl.Element(D)), lambda j, i: (small_row, 0))]
    args += [mods, mods, norm_w.reshape(1, D), wt, wt]
    out_specs = [pl.BlockSpec((tm, tn), lambda j, i: (i, out_block(j))),
                 pl.BlockSpec((tm, LANES), lambda j, i: (tok(j, i), 0))]
    out_shape = [jax.ShapeDtypeStruct((T, n_tiles * tn), BF16), jax.ShapeDtypeStruct((T, LANES), F32)]
    if not dual_x:
        out_specs.append(tile)
        out_shape.append(jax.ShapeDtypeStruct((T, D), F32))
    return pl.pallas_call(
        functools.partial(_modproj_kernel, dual_x=dual_x),
        grid=(n_tiles, n_i), in_specs=in_specs, out_specs=out_specs, out_shape=out_shape,
        scratch_shapes=[pltpu.VMEM((T, D), BF16), pltpu.VMEM((tn, D), BF16)],
        compiler_params=_cparams(2),
        name=f"modproj{layer}",
    )(*args)


def _expand_heads(v, off):
    hi = (lax.broadcasted_iota(jnp.int32, (v.shape[0], LANES), 1) >= SSD_P).astype(jnp.int32)
    tiles = [jnp.take_along_axis(v, hi + (off + 2 * q), axis=1) for q in range(SSD_HEADS // 2)]
    return jnp.concatenate(tiles, axis=1)


def _ssd_kernel(*refs, seq, has_h0):
    if has_h0:
        (z_ref, x_ref, bc_ref, dt_ref, cwx_ref, cwbc_ref, cbx_ref, cbbc_ref, dtb_ref, alog_ref, dsk_ref,
         nw_ref, h0f_ref, h0b_ref, y_ref, sf_ref, sb_ref,
         xpad, bcpad, xc, bcc, a_scr, dt_scr, yacc, hf, hb) = refs
    else:
        (z_ref, x_ref, bc_ref, dt_ref, cwx_ref, cwbc_ref, cbx_ref, cbbc_ref, dtb_ref, alog_ref, dsk_ref,
         nw_ref, y_ref, sf_ref, sb_ref,
         xpad, bcpad, xc, bcc, a_scr, dt_scr, yacc, hf, hb) = refs
    L = SSD_L
    nc = seq // L
    pad = SUBLANES
    half = SSD_CONV // 2

    for buf, src, cw, cb, dst in ((xpad, x_ref, cwx_ref, cbx_ref, xc), (bcpad, bc_ref, cwbc_ref, cbbc_ref, bcc)):
        width = buf.shape[1]
        buf[0:pad, :] = jnp.zeros((pad, width), F32)
        buf[pad + seq:2 * pad + seq, :] = jnp.zeros((pad, width), F32)
        buf[pad:pad + seq, :] = src[...].astype(F32)
        for blk in range(nc):
            acc = jnp.broadcast_to(cb[...], (L, width))
            for j in range(SSD_CONV):
                r0 = pad - half + j + blk * L
                acc = acc + cw[j:j + 1, :] * buf[r0:r0 + L, :]
            dst[blk * L:(blk + 1) * L, :] = _silu(acc)

    lane = lax.broadcasted_iota(jnp.int32, (seq, LANES), 1)
    dts = jnp.where(lane < 2 * SSD_HEADS, _softplus(dt_ref[...] + dtb_ref[...]), 0.0)
    dt_scr[...] = dts
    a_scr[...] = dts * (-jnp.exp(alog_ref[...]))

    if has_h0:
        hf[...] = h0f_ref[...]
        hb[...] = h0b_ref[...]
    else:
        hf[...] = jnp.zeros(hf.shape, F32)
        hb[...] = jnp.zeros(hb.shape, F32)

    row = lax.broadcasted_iota(jnp.int32, (L, L), 0)
    col = lax.broadcasted_iota(jnp.int32, (L, L), 1)
    lane_l = lax.broadcasted_iota(jnp.int32, (L, LANES), 1)
    lo_half = lane_l < SSD_P

    def chunk(c, fwd, h_scr):
        off = 0 if fwd else SSD_HEADS
        r0 = pl.multiple_of(c * L, L)
        a = a_scr[pl.ds(r0, L), :]
        dt = dt_scr[pl.ds(r0, L), :]
        cs = _cumsum_rows(a, L)
        total = cs[L - 1:L, :]
        if fwd:
            u = cs
            rvec = jnp.exp(cs)
            ed = jnp.exp(total - cs) * dt
            keep = col <= row
        else:
            ex = cs - a
            u = -ex
            rvec = jnp.exp(total - ex)
            ed = jnp.exp(ex) * dt
            keep = col >= row
        ut = jnp.transpose(u)
        dtt = jnp.transpose(dt)
        tcol = jnp.transpose(jnp.broadcast_to(total, (L, LANES)))[:, 0:1]
        rexp = _expand_heads(rvec, off)
        edexp = _expand_heads(ed, off)
        x = xc[pl.ds(r0, L), :]
        bc = bcc[pl.ds(r0, L), :]
        outs = []
        for g in range(SSD_GROUPS):
            bg = bc[:, g * SSD_N:(g + 1) * SSD_N]
            cg = bc[:, SSD_GROUPS * SSD_N + g * SSD_N:SSD_GROUPS * SSD_N + (g + 1) * SSD_N]
            cbm = _mm_nt(cg, bg)
            hg = h_scr[g * GROUP_W:(g + 1) * GROUP_W, :]
            xg = x[:, g * GROUP_W:(g + 1) * GROUP_W]
            y_off = _mm_nt(cg, hg) * rexp[:, g * GROUP_W:(g + 1) * GROUP_W]
            tiles = []
            for p in range(HEADS_PER_GROUP // 2):
                xt = xg[:, p * LANES:(p + 1) * LANES]
                acc = None
                for s in range(2):
                    h = off + g * HEADS_PER_GROUP + 2 * p + s
                    seg = u[:, h:h + 1] - ut[h:h + 1, :]
                    m = cbm * jnp.exp(jnp.where(keep, seg, -jnp.inf)) * dtt[h:h + 1, :]
                    xm = jnp.where(lo_half if s == 0 else jnp.logical_not(lo_half), xt, 0.0)
                    d = _mm(m, xm)
                    acc = d if acc is None else acc + d
                tiles.append(acc)
            outs.append(y_off + jnp.concatenate(tiles, axis=1))
            decs = []
            for hh in range(HEADS_PER_GROUP):
                h = off + g * HEADS_PER_GROUP + hh
                decs.append(jnp.broadcast_to(jnp.exp(tcol[h:h + 1, :]), (SSD_P, SSD_N)))
            dec = jnp.concatenate(decs, axis=0)
            h_scr[g * GROUP_W:(g + 1) * GROUP_W, :] = dec * hg + _mm_tn(xg * edexp[:, g * GROUP_W:(g + 1) * GROUP_W], bg)
        return r0, x, jnp.concatenate(outs, axis=1)

    def fwd_body(c, carry):
        r0, _, y = chunk(c, True, hf)
        yacc[pl.ds(r0, L), :] = y
        return carry

    lax.fori_loop(0, nc, fwd_body, 0)

    def bwd_body(i, carry):
        c = nc - 1 - i
        r0, x, y = chunk(c, False, hb)
        y = yacc[pl.ds(r0, L), :] + y + dsk_ref[...] * x
        y = y * _silu(z_ref[pl.ds(r0, L), :].astype(F32))
        y_ref[pl.ds(r0, L), :] = _rms(y, nw_ref[...]).astype(y_ref.dtype)
        return carry

    lax.fori_loop(0, nc, bwd_body, 0)
    sf_ref[...] = hf[...]
    sb_ref[...] = hb[...]


def _ssd(p0, p0dt, tok0, nseq, seq, cw, cb, dtb, alog, dskip, nw, h0f=None, h0b=None):
    has_h0 = h0f is not None
    b0 = tok0 // seq

    def cols(width, start):
        return pl.BlockSpec((seq, width), lambda s: (b0 + s, start // width))

    def full(shape):
        return pl.BlockSpec(shape, lambda s: (0,) * len(shape))

    in_specs = [cols(D_SSD, P0_Z), cols(D_SSD, P0_X), cols(512, P0_BC), cols(LANES, 0),
                full((SSD_CONV, D_SSD)), full((SSD_CONV, 512)), full((1, D_SSD)), full((1, 512)),
                full((1, LANES)), full((1, LANES)), full((1, D_SSD)), full((1, D_SSD))]
    args = [p0, p0, p0, p0dt, cw[:, :D_SSD], cw[:, D_SSD:], cb[:D_SSD].reshape(1, -1), cb[D_SSD:].reshape(1, -1),
            dtb, alog, dskip, nw.reshape(1, -1)]
    st_spec = pl.BlockSpec((None, D_SSD, SSD_N), lambda s: (s, 0, 0))
    if has_h0:
        in_specs += [st_spec, st_spec]
        args += [h0f, h0b]
    st_shape = jax.ShapeDtypeStruct((nseq, D_SSD, SSD_N), F32)
    return pl.pallas_call(
        functools.partial(_ssd_kernel, seq=seq, has_h0=has_h0),
        grid=(nseq,), in_specs=in_specs,
        out_specs=[pl.BlockSpec((seq, D_SSD), lambda s: (s, 0)), st_spec, st_spec],
        out_shape=[jax.ShapeDtypeStruct((nseq * seq, D_SSD), BF16), st_shape, st_shape],
        scratch_shapes=[pltpu.VMEM((seq + 2 * SUBLANES, D_SSD), F32), pltpu.VMEM((seq + 2 * SUBLANES, 512), F32),
                        pltpu.VMEM((seq, D_SSD), F32), pltpu.VMEM((seq, 512), F32),
                        pltpu.VMEM((seq, LANES), F32), pltpu.VMEM((seq, LANES), F32),
                        pltpu.VMEM((seq, D_SSD), F32),
                        pltpu.VMEM((D_SSD, SSD_N), F32), pltpu.VMEM((D_SSD, SSD_N), F32)],
        compiler_params=_cparams(1),
        name=f"ssd{seq}",
    )(*args)


def _place_halves(tile, kv_in_high):
    lo = lax.broadcasted_iota(jnp.int32, tile.shape, 1) < ATT_HD
    swapped = pltpu.roll(tile, ATT_HD, 1)
    if kv_in_high:
        return jnp.where(lo, swapped, 0.0), jnp.where(lo, 0.0, tile)
    return jnp.where(lo, tile, 0.0), jnp.where(lo, 0.0, swapped)


def _place_rows(vt, kv_in_high):
    head = vt[ATT_HD:, :] if kv_in_high else vt[:ATT_HD, :]
    z = jnp.zeros_like(head)
    return jnp.concatenate([head, z], axis=0), jnp.concatenate([z, head], axis=0)


LOG2E = 1.4426950408889634
SCORE_SCALE = ATT_SCALE * LOG2E


def _sink_attend_t(score_parts, value_parts, sink2):
    m = sink2
    for s in score_parts:
        m = jnp.maximum(m, jnp.max(s, axis=0, keepdims=True))
    denom = jnp.exp2(sink2 - m)
    out = None
    for s, v in zip(score_parts, value_parts):
        p = jnp.exp2(s - m)
        denom = denom + jnp.sum(p, axis=0, keepdims=True)
        o = _mm(v, p)
        out = o if out is None else out + o
    return out * (1.0 / denom)


def _attn_schedule(n, scores, attend):
    scores(0)
    for j in range(n):
        if j + 1 < n:
            scores(j + 1)
        attend(j)


def _ctx_attn_kernel(q_ref, k_ref, v_ref, sink_ref, o_ref, kt_ref, vt_ref, s_a, s_b):
    sink2 = sink_ref[...] * LOG2E
    bufs = (s_a, s_b)
    half = SEQ // 2
    for t in range(ATT_KV_DIM // LANES):
        cols = slice(t * LANES, (t + 1) * LANES)
        kt_ref[cols, :] = jnp.transpose(k_ref[:, cols].astype(F32))
        vt_ref[cols, :] = jnp.transpose(v_ref[:, cols].astype(F32))

    def kv_tile(ref, j):
        return ref[:, (j // 2) * LANES:(j // 2 + 1) * LANES].astype(F32), (j % 2 == 1)

    def scores(j):
        k_lo, k_hi = _place_halves(*kv_tile(k_ref, j))
        qst = jnp.concatenate([q_ref[:, qt * LANES:(qt + 1) * LANES] for qt in (2 * j, 2 * j + 1)], axis=0)
        bufs[j % 2][...] = _mm_nt(jnp.concatenate([k_lo, k_hi], axis=0), qst) * SCORE_SCALE

    def attend(j):
        src = bufs[j % 2]
        v, high = kv_tile(v_ref, j)
        vts = _place_rows(jnp.transpose(v), high)
        for ql in range(2):
            qt = 2 * j + ql
            for qh in range(2):
                cols = slice(ql * SEQ + qh * half, ql * SEQ + (qh + 1) * half)
                acc = None
                for s, vv in enumerate(vts):
                    o = _sink_attend_t([src[s * SEQ:(s + 1) * SEQ, cols]], [vv], sink2[:, 2 * qt + s:2 * qt + s + 1])
                    acc = o if acc is None else acc + o
                o_ref[qh * half:(qh + 1) * half, qt * LANES:(qt + 1) * LANES] = jnp.transpose(acc).astype(o_ref.dtype)

    _attn_schedule(ATT_KV, scores, attend)


def _ctx_attn(p0, sink):
    def cols(width, start):
        return pl.BlockSpec((SEQ, width), lambda b: (b, start // width))

    sbuf = pltpu.VMEM((2 * SEQ, 2 * SEQ), F32)
    return pl.pallas_call(
        _ctx_attn_kernel,
        grid=(BATCH,),
        in_specs=[cols(D, P0_Q), cols(ATT_KV_DIM, P0_K), cols(ATT_KV_DIM, P0_V),
                  pl.BlockSpec((1, LANES), lambda b: (0, 0))],
        out_specs=[pl.BlockSpec((SEQ, D), lambda b: (b, 0))]
        + [pl.BlockSpec((None, ATT_KV_DIM, SEQ), lambda b: (b, 0, 0))] * 2,
        out_shape=[jax.ShapeDtypeStruct((T_CTX, D), BF16)] + [jax.ShapeDtypeStruct((BATCH, ATT_KV_DIM, SEQ), F32)] * 2,
        scratch_shapes=[sbuf, sbuf],
        compiler_params=_cparams(1),
        name="ctx_attn",
    )(p0, p0, p0, sink)


def _rope_tables():
    quarter = ATT_HD // 4
    t = np.arange(DEC_SEQ)
    lane = np.arange(LANES)
    inv = ROPE_BASE ** (-(lane % quarter).astype(np.float64) / quarter)
    pos = np.where(((lane % ATT_HD) < ATT_HD // 2)[None, :], (t // GRID_W)[:, None], (t % GRID_W)[:, None])
    ang = pos * inv[None, :]
    first = ((lane % (2 * quarter)) < quarter)[None, :]
    cos, sin = np.cos(ang), np.sin(ang)
    return (jnp.asarray(cos, F32), jnp.asarray(np.where(first, -sin, 0.0), F32),
            jnp.asarray(np.where(first, 0.0, sin), F32))


def _rope(x, cos, sa, sb):
    quarter = ATT_HD // 4
    return x * cos + pltpu.roll(x, LANES - quarter, 1) * sa + pltpu.roll(x, quarter, 1) * sb


def _lat_attn_kernel(q_ref, kp_ref, kc_ref, kn_ref, vp_ref, vc_ref, vn_ref, ck_ref, cv_ref,
                     cos_ref, sa_ref, sb_ref, sink_ref, o_ref, c_a, c_b, w_a, w_b):
    blk = pl.program_id(1)
    nb = pl.num_programs(1)
    B = ATT_BLOCK
    sink2 = sink_ref[...] * LOG2E
    cbufs, wbufs = (c_a, c_b), (w_a, w_b)

    def tables(b):
        r0 = pl.multiple_of(b * B, B)
        return cos_ref[pl.ds(r0, B), :], sa_ref[pl.ds(r0, B), :], sb_ref[pl.ds(r0, B), :]

    tq = tables(blk)
    tk = [tables(jnp.maximum(blk - 1, 0)), tq, tables(jnp.minimum(blk + 1, nb - 1))]
    kabs = (blk - 1) * B + lax.broadcasted_iota(jnp.int32, (3 * B, B), 0)
    qpos = blk * B + lax.broadcasted_iota(jnp.int32, (3 * B, B), 1)
    ok = (jnp.abs(qpos - kabs) <= WINDOW) & (kabs >= 0) & (kabs < nb * B)
    ok = jnp.concatenate([ok, ok], axis=1)
    ok = jnp.concatenate([ok, ok], axis=0)

    def scores(j):
        high = (j % 2 == 1)
        sl = slice((j // 2) * LANES, (j // 2 + 1) * LANES)
        kw = jnp.concatenate([_rope(r[:, sl].astype(F32), *tb) for r, tb in zip((kp_ref, kc_ref, kn_ref), tk)], axis=0)
        qs = [q_ref[:, qt * LANES:(qt + 1) * LANES].astype(F32) for qt in (2 * j, 2 * j + 1)]
        q_plain = jnp.concatenate(qs, axis=0)
        q_rope = jnp.concatenate([_rope(q, *tq) for q in qs], axis=0)
        ck = jnp.transpose(ck_ref[sl, :])
        cbufs[j % 2][...] = _mm_nt(jnp.concatenate(_place_halves(ck, high), axis=0), q_plain) * SCORE_SCALE
        win = _mm_nt(jnp.concatenate(_place_halves(kw, high), axis=0), q_rope) * SCORE_SCALE
        wbufs[j % 2][...] = jnp.where(ok, win, -jnp.inf)

    def attend(j):
        high = (j % 2 == 1)
        sl = slice((j // 2) * LANES, (j // 2 + 1) * LANES)
        vw = jnp.concatenate([jnp.transpose(r[:, sl].astype(F32)) for r in (vp_ref, vc_ref, vn_ref)], axis=1)
        vts = _place_rows(vw, high)
        cvts = _place_rows(cv_ref[sl, :], high)
        csrc, wsrc = cbufs[j % 2], wbufs[j % 2]
        for ql in range(2):
            qt = 2 * j + ql
            cols = slice(ql * B, (ql + 1) * B)
            acc = None
            for s in range(2):
                parts = [csrc[s * PAST_LEN:(s + 1) * PAST_LEN, cols], wsrc[s * 3 * B:(s + 1) * 3 * B, cols]]
                o = _sink_attend_t(parts, [cvts[s], vts[s]], sink2[:, 2 * qt + s:2 * qt + s + 1])
                acc = o if acc is None else acc + o
            o_ref[:, qt * LANES:(qt + 1) * LANES] = jnp.transpose(acc).astype(o_ref.dtype)

    _attn_schedule(ATT_KV, scores, attend)


def _lat_attn(p0, ck, cv, sink):
    nb = DEC_SEQ // ATT_BLOCK
    base = T_CTX // ATT_BLOCK

    def kv(start, shift):
        return pl.BlockSpec((ATT_BLOCK, ATT_KV_DIM),
                            lambda b, i: (base + b * nb + jnp.clip(i + shift, 0, nb - 1), start // ATT_KV_DIM))

    def full(shape):
        return pl.BlockSpec(shape, lambda b, i: (0,) * len(shape))

    cache = pl.BlockSpec((None, ATT_KV_DIM, PAST_LEN), lambda b, i: (b, 0, 0))
    cos, sa, sb = _rope_tables()
    return pl.pallas_call(
        _lat_attn_kernel,
        grid=(DEC_BATCH, nb),
        in_specs=[pl.BlockSpec((ATT_BLOCK, D), lambda b, i: (base + b * nb + i, P0_Q // D)),
                  kv(P0_K, -1), kv(P0_K, 0), kv(P0_K, 1), kv(P0_V, -1), kv(P0_V, 0), kv(P0_V, 1),
                  cache, cache, full((DEC_SEQ, LANES)), full((DEC_SEQ, LANES)), full((DEC_SEQ, LANES)),
                  full((1, LANES))],
        out_specs=pl.BlockSpec((ATT_BLOCK, D), lambda b, i: (b * nb + i, 0)),
        out_shape=jax.ShapeDtypeStruct((T_LAT, D), BF16),
        scratch_shapes=[pltpu.VMEM((2 * PAST_LEN, 2 * ATT_BLOCK), F32)] * 2
        + [pltpu.VMEM((2 * 3 * ATT_BLOCK, 2 * ATT_BLOCK), F32)] * 2,
        compiler_params=_cparams(2),
        name="lat_attn",
    )(p0, p0, p0, p0, p0, p0, p0, ck, cv, cos, sa, sb, sink)


def _log_sigmoid(x):
    return jnp.minimum(x, 0.0) - jnp.log(1.0 + jnp.exp(-jnp.abs(x)))


def _gla_kernel(*refs, seq, has_s0):
    if has_s0:
        (q_ref, k_ref, v_ref, r_ref, lr_ref, w2f_ref, w2b_ref, bf_ref, bb_ref, nw_ref, s0f_ref, s0b_ref,
         y_ref, sf_ref, sb_ref, gf, gb, yf, yb, stf, stb) = refs
    else:
        (q_ref, k_ref, v_ref, r_ref, lr_ref, w2f_ref, w2b_ref, bf_ref, bb_ref, nw_ref,
         y_ref, sf_ref, sb_ref, gf, gb, yf, yb, stf, stb) = refs
    C = GLA_C
    nc = seq // C
    lr = lr_ref[...]
    gf[...] = _log_sigmoid(_mm(lr, w2f_ref[...]) + bf_ref[...]) / GLA_GATE_NORM
    gb[...] = _log_sigmoid(_mm(lr, w2b_ref[...]) + bb_ref[...]) / GLA_GATE_NORM
    for h in range(GLA_HEADS):
        rows = slice(h * GLA_DV, (h + 1) * GLA_DV)
        if has_s0:
            stf[rows, :] = jnp.transpose(s0f_ref[h * GLA_DK:(h + 1) * GLA_DK, :])
            stb[rows, :] = jnp.transpose(s0b_ref[h * GLA_DK:(h + 1) * GLA_DK, :])
        else:
            stf[rows, :] = jnp.zeros((GLA_DV, GLA_DK), F32)
            stb[rows, :] = jnp.zeros((GLA_DV, GLA_DK), F32)

    row = lax.broadcasted_iota(jnp.int32, (C, C), 0)
    col = lax.broadcasted_iota(jnp.int32, (C, C), 1)
    qscale = GLA_DK ** -0.5

    def chunk(c, fwd):
        g_scr, y_scr, st = (gf, yf, stf) if fwd else (gb, yb, stb)
        r0 = pl.multiple_of(c * C, C)
        g = g_scr[pl.ds(r0, C), :]
        cs = _cumsum_rows(g, C)
        total = cs[C - 1:C, :]
        q = q_ref[pl.ds(r0, C), :].astype(F32) * qscale
        k = k_ref[pl.ds(r0, C), :].astype(F32)
        v = v_ref[pl.ds(r0, C), :]
        if fwd:
            qs, ks, ke = q * jnp.exp(cs), k * jnp.exp(-cs), k * jnp.exp(total - cs)
            keep = col <= row
        else:
            ex = cs - g
            qs, ks, ke = q * jnp.exp(total - ex), k * jnp.exp(ex - total), k * jnp.exp(ex)
            keep = col >= row
        dec = jnp.exp(total)
        for h in range(GLA_HEADS):
            kc = slice(h * GLA_DK, (h + 1) * GLA_DK)
            vc = slice(h * GLA_DV, (h + 1) * GLA_DV)
            s_t = st[vc, :]
            att = jnp.where(keep, _mm_nt(qs[:, kc], ks[:, kc]), 0.0)
            y_scr[pl.ds(r0, C), vc] = _mm(att, v[:, vc]) + _mm_nt(qs[:, kc], s_t)
            st[vc, :] = dec[:, kc] * s_t + _mm_tn(v[:, vc], ke[:, kc])

    def body(i, carry):
        chunk(i, True)
        chunk(nc - 1 - i, False)
        return carry

    lax.fori_loop(0, nc, body, 0, unroll=4)

    nw = nw_ref[...]
    for blk in range(seq // LANES):
        rs = slice(blk * LANES, (blk + 1) * LANES)
        y = yf[rs, :] + yb[rs, :]
        gate = _silu(r_ref[rs, :].astype(F32))
        for h in range(GLA_HEADS):
            vc = slice(h * GLA_DV, (h + 1) * GLA_DV)
            y_ref[rs, vc] = (_rms(y[:, vc], nw) * gate[:, vc]).astype(y_ref.dtype)
    for h in range(GLA_HEADS):
        rows = slice(h * GLA_DV, (h + 1) * GLA_DV)
        sf_ref[h * GLA_DK:(h + 1) * GLA_DK, :] = jnp.transpose(stf[rows, :])
        sb_ref[h * GLA_DK:(h + 1) * GLA_DK, :] = jnp.transpose(stb[rows, :])


def _gla(p1, p1lr, tok0, nseq, seq, w2f, w2b, bgf, bgb, nw, s0f=None, s0b=None):
    has_s0 = s0f is not None
    b0 = tok0 // seq
    dk_all, dv_all = GLA_HEADS * GLA_DK, GLA_HEADS * GLA_DV

    def cols(width, start):
        return pl.BlockSpec((seq, width), lambda s: (b0 + s, start // width))

    def full(shape):
        return pl.BlockSpec(shape, lambda s: (0,) * len(shape))

    in_specs = [cols(dk_all, P1_Q), cols(dk_all, P1_K), cols(dv_all, P1_V), cols(dv_all, P1_R), cols(LANES, 0),
                full((LANES, dk_all)), full((LANES, dk_all)), full((1, dk_all)), full((1, dk_all)), full((1, GLA_DV))]
    args = [p1, p1, p1, p1, p1lr, w2f, w2b, bgf.reshape(1, -1), bgb.reshape(1, -1), nw.reshape(1, -1)]
    st_spec = pl.BlockSpec((None, dk_all, GLA_DV), lambda s: (s, 0, 0))
    if has_s0:
        in_specs += [st_spec, st_spec]
        args += [s0f, s0b]
    st_shape = jax.ShapeDtypeStruct((nseq, dk_all, GLA_DV), F32)
    return pl.pallas_call(
        functools.partial(_gla_kernel, seq=seq, has_s0=has_s0),
        grid=(nseq,), in_specs=in_specs,
        out_specs=[pl.BlockSpec((seq, dv_all), lambda s: (s, 0)), st_spec, st_spec],
        out_shape=[jax.ShapeDtypeStruct((nseq * seq, dv_all), BF16), st_shape, st_shape],
        scratch_shapes=[pltpu.VMEM((seq, dk_all), F32), pltpu.VMEM((seq, dk_all), F32),
                        pltpu.VMEM((seq, dv_all), F32), pltpu.VMEM((seq, dv_all), F32),
                        pltpu.VMEM((dv_all, GLA_DK), F32), pltpu.VMEM((dv_all, GLA_DK), F32)],
        compiler_params=_cparams(1),
        name=f"gla{seq}",
    )(*args)


ROUTE_TM = 512
ROUTE_SUB = 256
ROUTE_ROWS = 32


def _split_bf16(x):
    hi = x.astype(BF16)
    return hi, (x - hi.astype(F32)).astype(BF16)


def _outproj_kernel(*refs, n_in, dual_x):
    y_refs = refs[:2 * n_in]
    n_x = 2 if dual_x else 1
    x_refs = refs[2 * n_in + 1:2 * n_in + 1 + n_x]
    w_ref = refs[2 * n_in]
    (g1_ref, sh_ref, sc_ref, nw_ref, wr_ref, br_ref,
     xo_ref, tr_ref, route_ref, cnt_ref, w_scr, wr_hl, carry, o_scr) = refs[2 * n_in + 1 + n_x:]
    is_ctx = pl.program_id(0) < T_CTX // ROUTE_TM

    @pl.when(pl.program_id(0) == 0)
    def _():
        w_scr[...] = w_ref[...].astype(BF16)
        hi, lo = _split_bf16(jnp.transpose(wr_ref[...]))
        wr_hl[0:LANES, :] = hi
        wr_hl[LANES:2 * LANES, :] = lo
        carry[...] = jnp.zeros(carry.shape, F32)

    o = None
    for i in range(n_in):
        y = jnp.where(is_ctx, y_refs[2 * i][...], y_refs[2 * i + 1][...])
        d = jnp.dot(y, w_scr[i * D:(i + 1) * D, :], preferred_element_type=F32)
        o = d if o is None else o + d
    o_scr[...] = o
    for sub in range(ROUTE_TM // ROUTE_SUB):
        _outproj_subtile(sub, is_ctx, o_scr, x_refs, dual_x, g1_ref, sh_ref, sc_ref, nw_ref, br_ref,
                         xo_ref, tr_ref, route_ref, cnt_ref, wr_hl, carry)


def _outproj_subtile(sub, is_ctx, o_scr, x_refs, dual_x, g1_ref, sh_ref, sc_ref, nw_ref, br_ref,
                     xo_ref, tr_ref, route_ref, cnt_ref, wr_hl, carry):
    tm = ROUTE_SUB
    rows = slice(sub * tm, (sub + 1) * tm)
    x_in = jnp.where(is_ctx, x_refs[0][rows, :], x_refs[1][rows, :]) if dual_x else x_refs[0][rows, :]
    x = x_in + g1_ref[0] * o_scr[rows, :]
    xo_ref[rows, :] = x
    t = _rms(x, nw_ref[...]) * (1.0 + sc_ref[0]) + sh_ref[0]
    for k in range(D // LANES):
        tr_ref[pl.ds(sub * tm * SUBLANES + k, tm, stride=SUBLANES), :] = t[:, k * LANES:(k + 1) * LANES]

    t_hi, t_lo = _split_bf16(t)
    lg = _mm_nt(wr_hl[...], t_hi)
    nr = ROUTE_ROWS
    logit = lg[0:nr, :] + lg[LANES:LANES + nr, :] + _mm_nt(wr_hl[0:LANES, :], t_lo)[0:nr, :] + br_ref[0:nr, :]
    rowf = lax.broadcasted_iota(jnp.int32, (nr, tm), 0).astype(F32)
    neg = -jnp.inf

    def first_argmax(v, vmax):
        return jnp.min(jnp.where(v == vmax, rowf, float(LANES)), axis=0, keepdims=True)

    gl = jnp.where(rowf < N_GROUPS, logit, neg)
    gmax = jnp.max(gl, axis=0, keepdims=True)
    gsel = first_argmax(gl, gmax)
    gprob = 1.0 / jnp.sum(jnp.exp(gl - gmax), axis=0, keepdims=True)
    first = N_GROUPS + EXP_PER_GROUP * gsel
    el = jnp.where((rowf >= first) & (rowf < first + EXP_PER_GROUP), logit, neg)
    m1 = jnp.max(el, axis=0, keepdims=True)
    i1 = first_argmax(el, m1)
    el2 = jnp.where(rowf == i1, neg, el)
    m2 = jnp.max(el2, axis=0, keepdims=True)
    i2 = first_argmax(el2, m2)
    e2 = jnp.exp(m2 - m1)
    c1 = gprob / (1.0 + e2)
    c2 = gprob * e2 / (1.0 + e2)
    x1 = i1 - N_GROUPS
    x2 = i2 - N_GROUPS

    erow = rowf
    hot = ((erow == x1) | (erow == x2)).astype(F32)
    tri = (lax.broadcasted_iota(jnp.int32, (tm, tm), 0) < lax.broadcasted_iota(jnp.int32, (tm, tm), 1))
    before = _mm(hot, tri.astype(F32)) + carry[...]
    r1 = jnp.sum(jnp.where(erow == x1, before, 0.0), axis=0, keepdims=True)
    r2 = jnp.sum(jnp.where(erow == x2, before, 0.0), axis=0, keepdims=True)
    total = carry[...] + _mm(hot, jnp.ones((tm, tm), F32))
    carry[...] = total
    cnt_ref[...] = total[0:N_EXPERTS, 0:LANES]
    row8 = lax.broadcasted_iota(jnp.int32, (SUBLANES, tm), 0)
    out = jnp.zeros((SUBLANES, tm), F32)
    for k, v in enumerate((x1, x2, c1, c2, r1, r2)):
        out = jnp.where(row8 == k, jnp.broadcast_to(v, (SUBLANES, tm)), out)
    route_ref[:, rows] = out


def _outproj_route(ys, w_out, xs, mods, layer, norm_w, w_router, b_router):
    tm = ROUTE_TM
    n_in = len(ys) // 2
    dual_x = len(xs) == 2
    kdim = w_out.shape[0]

    def full(shape):
        return pl.BlockSpec(shape, lambda i: (0,) * len(shape))

    tile = pl.BlockSpec((tm, D), lambda i: (i, 0))
    pair = list(_ctx_lat_specs(tm))
    in_specs = (pair * n_in + [full((kdim, D))] + (pair if dual_x else [tile])
                + [_mod_spec(layer, 2, tm), _mod_spec(layer, 3, tm), _mod_spec(layer, 4, tm),
                   full((1, D)), full((D, LANES)), full((LANES, ROUTE_SUB))])
    cnt = jax.ShapeDtypeStruct((N_EXPERTS, LANES), F32)
    return pl.pallas_call(
        functools.partial(_outproj_kernel, n_in=n_in, dual_x=dual_x),
        grid=(T // tm,), in_specs=in_specs,
        out_specs=[tile, pl.BlockSpec((tm * SUBLANES, LANES), lambda i: (i, 0)),
                   pl.BlockSpec((SUBLANES, tm), lambda i: (0, i)), full(cnt.shape)],
        out_shape=[jax.ShapeDtypeStruct((T, D), F32), jax.ShapeDtypeStruct((T * SUBLANES, LANES), F32),
                   jax.ShapeDtypeStruct((SUBLANES, T), F32), cnt],
        scratch_shapes=[pltpu.VMEM((kdim, D), BF16), pltpu.VMEM((2 * LANES, D), BF16),
                        pltpu.VMEM((ROUTE_ROWS, ROUTE_SUB), F32), pltpu.VMEM((tm, D), F32)],
        compiler_params=_cparams(1),
        name=f"outproj{layer}",
    )(*ys, w_out, *xs, mods, mods, mods, norm_w.reshape(1, D), w_router,
      jnp.broadcast_to(b_router.reshape(LANES, 1), (LANES, ROUTE_SUB)))


def _moe_meta(counts):
    tm = MOE_TM
    experts = jnp.arange(N_EXPERTS, dtype=jnp.int32)
    counts = jnp.max(counts, axis=1).astype(jnp.int32)
    padded = ((counts + tm - 1) // tm) * tm
    ends = jnp.cumsum(padded)
    tile_start = jnp.arange(MOE_TILES, dtype=jnp.int32) * tm
    te = jnp.sum((tile_start[:, None] >= ends[None, :]).astype(jnp.int32), axis=1)
    last = jnp.max(jnp.where(counts > 0, experts, 0))
    meta = jnp.concatenate([jnp.minimum(te, last), ends[-1:] // tm]).astype(jnp.int32)
    starts = ends - padded
    pads = jnp.concatenate([starts + counts, ends[-1:], ends, jnp.full((1,), MOE_ROWS)]).astype(jnp.int32)
    return starts.astype(jnp.int32), pads, meta


def _expert_changed(meta_ref, j):
    return (j == 0) | (meta_ref[j] != meta_ref[jnp.maximum(j - 1, 0)])


def _moe_up_kernel(pos1_ref, pos2_ref, pads_ref, meta_ref, tr_hbm, wg_ref, wu_ref, a_ref, rowmap_ref,
                   tr_scr, g0, g1, wg_bf, wu_bf, sem):
    j = pl.program_id(0)
    tm = MOE_TM
    ntiles = meta_ref[MOE_TILES]

    def gather(tile, dst):
        for mi in range(tm):
            tok = jnp.minimum(rowmap_ref[tile * tm + mi] >> 1, T - 1)
            dst[mi * SUBLANES:(mi + 1) * SUBLANES, :] = tr_scr[pl.ds(pl.multiple_of(tok * SUBLANES, SUBLANES), SUBLANES), :]

    @pl.when(j == 0)
    def _():
        load = pltpu.make_async_copy(tr_hbm, tr_scr, sem)
        load.start()

        def clear(c, carry):
            for i in range(SUBLANES):
                rowmap_ref[c * SUBLANES + i] = 2 * T
            return carry
        for k in range(N_EXPERTS + 1):
            lax.fori_loop(pads_ref[k] // SUBLANES, pads_ref[N_EXPERTS + 1 + k] // SUBLANES, clear, 0)

        def place(t, carry):
            rowmap_ref[pos1_ref[t]] = 2 * t
            rowmap_ref[pos2_ref[t]] = 2 * t + 1
            return carry
        lax.fori_loop(0, T, place, 0, unroll=8)
        load.wait()
        gather(0, g0)

    def compute(cur, nxt):
        gather(jnp.minimum(j + 1, ntiles - 1), nxt)
        x = _tokmajor_to_std(cur, tm).astype(BF16)
        g = jnp.dot(x, wg_bf[...], preferred_element_type=F32)
        u = jnp.dot(x, wu_bf[...], preferred_element_type=F32)
        a_ref[...] = (_silu(g) * u).astype(a_ref.dtype)

    @pl.when(j < ntiles)
    def _():
        @pl.when(_expert_changed(meta_ref, j))
        def _():
            wg_bf[...] = wg_ref[...].astype(BF16)
            wu_bf[...] = wu_ref[...].astype(BF16)

        pl.when(j % 2 == 0)(functools.partial(compute, g0, g1))
        pl.when(j % 2 == 1)(functools.partial(compute, g1, g0))

    @pl.when(j >= ntiles)
    def _():
        a_ref[...] = jnp.zeros(a_ref.shape, a_ref.dtype)


def _moe_down_kernel(rowmap_ref, cpair_ref, meta_ref, a_ref, wd_ref, out_hbm, acc, y0, y1, wd_bf):
    j = pl.program_id(0)
    tm = MOE_TM
    zrows = 512
    ntiles = meta_ref[MOE_TILES]

    @pl.when(j == 0)
    def _():
        def zero(i, carry):
            acc[pl.ds(pl.multiple_of(i * zrows, zrows), zrows), :] = jnp.zeros((zrows, LANES), F32)
            return carry
        lax.fori_loop(0, ACC_ROWS // zrows, zero, 0)

    def matmul(dst):
        y = jnp.dot(a_ref[...], wd_bf[...], preferred_element_type=F32)
        for k in range(D // LANES):
            dst[pl.ds(k, tm, stride=SUBLANES), :] = y[:, k * LANES:(k + 1) * LANES]

    def scatter(tile, src):
        for b in range(tm // SUBLANES):
            ents = [rowmap_ref[tile * tm + b * SUBLANES + i] for i in range(SUBLANES)]
            offs = [pl.multiple_of((e >> 1) * SUBLANES, SUBLANES) for e in ents]
            olds = [acc[pl.ds(o, SUBLANES), :] for o in offs]
            for i, o in enumerate(offs):
                r = (b * SUBLANES + i) * SUBLANES
                acc[pl.ds(o, SUBLANES), :] = olds[i] + cpair_ref[ents[i]] * src[r:r + SUBLANES, :]

    has_mm = j < ntiles
    has_sc = (j >= 1) & (j <= ntiles)

    @pl.when(has_mm)
    def _():
        @pl.when(_expert_changed(meta_ref, j))
        def _():
            wd_bf[...] = wd_ref[...].astype(BF16)

    for par, (cur, prev) in enumerate(((y0, y1), (y1, y0))):
        mine = (j % 2) == par

        @pl.when(mine & has_mm & has_sc)
        def _():
            matmul(cur)
            scatter(j - 1, prev)

        @pl.when(mine & has_mm & jnp.logical_not(has_sc))
        def _():
            matmul(cur)

        @pl.when(mine & jnp.logical_not(has_mm) & has_sc)
        def _():
            scatter(j - 1, prev)

    @pl.when(j == pl.num_programs(0) - 1)
    def _():
        pltpu.sync_copy(acc.at[0:T * SUBLANES, :], out_hbm)


def _moe(tr, route_t, counts, layer, w_gate, w_up, w_down):
    tm = MOE_TM
    starts, pads, meta = _moe_meta(counts)
    experts = jnp.arange(N_EXPERTS, dtype=jnp.int32)

    def position(e, r):
        sel = e.astype(jnp.int32)[:, None] == experts[None, :]
        return jnp.sum(jnp.where(sel, starts[None, :], 0), axis=1) + r.astype(jnp.int32)

    pos1 = position(route_t[0], route_t[4])
    pos2 = position(route_t[1], route_t[5])
    cpair = jnp.concatenate([jnp.stack([route_t[2], route_t[3]], axis=1).reshape(2 * T), jnp.zeros((2,), F32)])

    def wspec(shape, n):
        return pl.BlockSpec((None, None) + shape,
                            lambda j, *pre: (layer, pre[n - 1][jnp.minimum(j, MOE_TILES - 1)], 0, 0))

    gscr = pltpu.VMEM((tm * SUBLANES, LANES), F32)
    act, rowmap = pl.pallas_call(
        _moe_up_kernel,
        grid_spec=pltpu.PrefetchScalarGridSpec(
            num_scalar_prefetch=4, grid=(MOE_TILES,),
            in_specs=[pl.BlockSpec(memory_space=pl.ANY), wspec((D, D_EXPERT), 4), wspec((D, D_EXPERT), 4)],
            out_specs=[pl.BlockSpec((tm, D_EXPERT), lambda j, *pre: (j, 0)), pl.BlockSpec(memory_space=pltpu.SMEM)],
            scratch_shapes=[pltpu.VMEM((T * SUBLANES, LANES), F32), gscr, gscr,
                            pltpu.VMEM((D, D_EXPERT), BF16), pltpu.VMEM((D, D_EXPERT), BF16),
                            pltpu.SemaphoreType.DMA(())]),
        out_shape=[jax.ShapeDtypeStruct((MOE_ROWS, D_EXPERT), BF16), jax.ShapeDtypeStruct((MOE_ROWS,), jnp.int32)],
        compiler_params=_cparams(1),
        name=f"moe_up{layer}",
    )(pos1, pos2, pads, meta, tr, w_gate, w_up)

    return pl.pallas_call(
        _moe_down_kernel,
        grid_spec=pltpu.PrefetchScalarGridSpec(
            num_scalar_prefetch=3, grid=(MOE_TILES + 1,),
            in_specs=[pl.BlockSpec((tm, D_EXPERT), lambda j, *pre: (jnp.minimum(j, MOE_TILES - 1), 0)),
                      wspec((D_EXPERT, D), 3)],
            out_specs=pl.BlockSpec(memory_space=pl.ANY),
            scratch_shapes=[pltpu.VMEM((ACC_ROWS, LANES), F32), gscr, gscr, pltpu.VMEM((D_EXPERT, D), BF16)]),
        out_shape=jax.ShapeDtypeStruct((T * SUBLANES, LANES), F32),
        compiler_params=_cparams(1),
        name=f"moe_down{layer}",
    )(rowmap, cpair, meta, act, w_down)


FINAL_TM = 512


def _final_kernel(x_ref, moe_ref, g2_ref, nw_ref, oc_ref, ol_ref):
    x = x_ref[...] + g2_ref[0] * _tokmajor_to_std(moe_ref, FINAL_TM)
    y = _rms(x, nw_ref[...])
    is_ctx = pl.program_id(0) < T_CTX // FINAL_TM

    @pl.when(is_ctx)
    def _():
        oc_ref[...] = y

    @pl.when(jnp.logical_not(is_ctx))
    def _():
        ol_ref[...] = y


def _final(x, moe, mods, layer, norm_w):
    tm = FINAL_TM
    tile = pl.BlockSpec((tm, D), lambda i: (i, 0))
    return pl.pallas_call(
        _final_kernel,
        grid=(T // tm,),
        in_specs=[tile, pl.BlockSpec((tm * SUBLANES, LANES), lambda i: (i, 0)), _mod_spec(layer, 5, tm),
                  pl.BlockSpec((1, D), lambda i: (0, 0))],
        out_specs=list(_ctx_lat_specs(tm)),
        out_shape=[jax.ShapeDtypeStruct((T_CTX, D), F32), jax.ShapeDtypeStruct((T_LAT, D), F32)],
        compiler_params=_cparams(1),
        name="final_norm",
    )(x, moe, mods, norm_w.reshape(1, D))


def _pad_lanes(v):
    return jnp.pad(v.astype(F32), (0, LANES - v.shape[0])).reshape(1, LANES)


def kernel(x_prompt, x_sample, cache_k_attn, cache_v_attn, state_ssd_fwd, state_ssd_bwd, state_gla_fwd, state_gla_bwd, c, c_ctx, w_ada, b_ada, norm_mix_w, norm_ffn_w, w_in_even, conv_w, conv_b, dt_bias_fwd, dt_bias_bwd, a_log_fwd, a_log_bwd, d_skip, ssd_norm_w, attn_sink, w_out_even, w_in_odd, w_gk2_fwd, b_gk_fwd, w_gk2_bwd, b_gk_bwd, gla_norm_w, w_out_odd, w_router_group, b_router_group, w_router_expert, b_router_expert, w_gate_exp, w_up_exp, w_down_exp, final_norm_w):
    depth = w_ada.shape[0]
    assert depth == 2 and x_prompt.shape == (BATCH, SEQ, D) and x_sample.shape == (DEC_BATCH, DEC_SEQ, D)

    cond8 = jnp.concatenate([c_ctx[None, :], c, jnp.zeros((SUBLANES - 1 - DEC_BATCH, D), F32)], axis=0)
    mods = _adaln(cond8, w_ada, b_ada).reshape(depth * SUBLANES * 6, 1, D)
    xs0 = (x_prompt.reshape(T_CTX, D), x_sample.reshape(T_LAT, D))

    def router_params(i):
        wr = jnp.concatenate([w_router_group[i], w_router_expert[i],
                              jnp.zeros((D, LANES - N_GROUPS - N_EXPERTS), F32)], axis=1)
        return wr, _pad_lanes(jnp.concatenate([b_router_group[i], b_router_expert[i]]))

    n_zxbc = 2 * D_SSD + 2 * SSD_GROUPS * SSD_N
    n_dt = 2 * SSD_HEADS
    bc_tile, q_tiles = n_zxbc // PROJ_TN - 1, D // PROJ_TN
    rows0 = [k * PROJ_TN for k in range(bc_tile + 1)] + [n_zxbc + n_dt + k * PROJ_TN for k in range(q_tiles + 1)]

    def out_block0(j):
        return jnp.where(j == bc_tile, bc_tile + q_tiles, jnp.where((j > bc_tile) & (j <= bc_tile + q_tiles), j - 1, j))

    p0, p0dt = _modproj(xs0, mods, 0, norm_mix_w[0], jnp.transpose(w_in_even[0]), rows0, n_zxbc, out_block0)

    dtb = _pad_lanes(jnp.concatenate([dt_bias_fwd[0], dt_bias_bwd[0]]))
    alog = _pad_lanes(jnp.concatenate([a_log_fwd[0], a_log_bwd[0]]))
    dskip = jnp.repeat(d_skip[0], SSD_P).reshape(1, D_SSD)
    ssd_args = (conv_w[0], conv_b[0], dtb, alog, dskip, ssd_norm_w[0])
    y_ssd_c, ssd_f, ssd_b = _ssd(p0, p0dt, 0, BATCH, SEQ, *ssd_args)
    y_ssd_l, _, _ = _ssd(p0, p0dt, T_CTX, DEC_BATCH, DEC_SEQ, *ssd_args,
                         h0f=state_ssd_fwd[:, 0].reshape(DEC_BATCH, D_SSD, SSD_N),
                         h0b=state_ssd_bwd[:, 0].reshape(DEC_BATCH, D_SSD, SSD_N))
    sink = _pad_lanes(attn_sink[0])
    y_att_c, new_kt, new_vt = _ctx_attn(p0, sink)
    def cache_in(t):
        return jnp.transpose(t[:, 0], (0, 2, 3, 1)).reshape(DEC_BATCH, ATT_KV_DIM, PAST_LEN)

    y_att_l = _lat_attn(p0, cache_in(cache_k_attn), cache_in(cache_v_attn), sink)
    xmid0, tr0, route0, cnt0 = _outproj_route([y_ssd_c, y_ssd_l, y_att_c, y_att_l], w_out_even[0], xs0, mods, 0,
                                              norm_ffn_w[0], *router_params(0))
    moe0 = _moe(tr0, route0, cnt0, 0, w_gate_exp, w_up_exp, w_down_exp)

    dk_all = GLA_HEADS * GLA_DK
    n_qkvr = 2 * dk_all + 2 * GLA_HEADS * GLA_DV
    n_odd = w_in_odd.shape[2]
    p1, p1lr, x1 = _modproj(xmid0, mods, 1, norm_mix_w[1], jnp.transpose(w_in_odd[0]),
                            [k * PROJ_TN1 for k in range(n_qkvr // PROJ_TN1)], n_odd - LANES, lambda j: j, moe=moe0,
                            tn=PROJ_TN1)
    lr0 = LANES - 2 * GLA_LOWRANK
    w2f = jnp.zeros((LANES, dk_all), F32).at[lr0:lr0 + GLA_LOWRANK].set(w_gk2_fwd[0])
    w2b = jnp.zeros((LANES, dk_all), F32).at[lr0 + GLA_LOWRANK:].set(w_gk2_bwd[0])
    gla_args = (w2f, w2b, b_gk_fwd[0], b_gk_bwd[0], gla_norm_w[0])
    y_gla_c, gla_f, gla_b = _gla(p1, p1lr, 0, BATCH, SEQ, *gla_args)
    y_gla_l, _, _ = _gla(p1, p1lr, T_CTX, DEC_BATCH, DEC_SEQ, *gla_args,
                         s0f=state_gla_fwd[:, 0].reshape(DEC_BATCH, dk_all, GLA_DV),
                         s0b=state_gla_bwd[:, 0].reshape(DEC_BATCH, dk_all, GLA_DV))
    xmid1, tr1, route1, cnt1 = _outproj_route([y_gla_c, y_gla_l], w_out_odd[0], (x1,), mods, 1,
                                              norm_ffn_w[1], *router_params(1))
    moe1 = _moe(tr1, route1, cnt1, 1, w_gate_exp, w_up_exp, w_down_exp)
    y_c, y_l = _final(xmid1, moe1, mods, 1, final_norm_w)

    y_prompt = y_c.reshape(BATCH, SEQ, D)
    y_sample = y_l.reshape(DEC_BATCH, DEC_SEQ, D)
    def cache_out(t):
        return jnp.transpose(t.reshape(BATCH, 1, ATT_KV, ATT_HD, SEQ), (0, 1, 4, 2, 3))

    new_k, new_v = cache_out(new_kt), cache_out(new_vt)
    return (y_prompt, y_sample, new_k, new_v,
            ssd_f.reshape(BATCH, 1, SSD_HEADS, SSD_P, SSD_N), ssd_b.reshape(BATCH, 1, SSD_HEADS, SSD_P, SSD_N),
            gla_f.reshape(BATCH, 1, GLA_HEADS, GLA_DK, GLA_DV), gla_b.reshape(BATCH, 1, GLA_HEADS, GLA_DK, GLA_DV))
```

```python
import functools
import math

import numpy as np
import jax
import jax.numpy as jnp
from jax import lax
from jax.experimental import pallas as pl
from jax.experimental.pallas import tpu as pltpu

F32 = jnp.float32
BF16 = jnp.bfloat16

D = 1024
BATCH, SEQ = 16, 256
DEC_BATCH, DEC_SEQ = 2, 1024
PAST_LEN = 512
GRID_W = 64
EPS = 1e-6
T_CTX = BATCH * SEQ
T_LAT = DEC_BATCH * DEC_SEQ
T = T_CTX + T_LAT

SSD_HEADS, SSD_P, SSD_N, SSD_GROUPS = 16, 64, 128, 2
SSD_CONV = 5
SSD_L = 128
D_SSD = SSD_HEADS * SSD_P
HEADS_PER_GROUP = SSD_HEADS // SSD_GROUPS
GROUP_W = HEADS_PER_GROUP * SSD_P

ATT_HEADS, ATT_KV, ATT_HD = 16, 4, 64
ATT_KV_DIM = ATT_KV * ATT_HD
WINDOW = 128
ATT_BLOCK = 128
ATT_SCALE = ATT_HD ** -0.5
ROPE_BASE = 10000.0

GLA_HEADS, GLA_DK, GLA_DV = 4, 128, 256
GLA_C = 64
GLA_GATE_NORM = 16.0
GLA_LOWRANK = 16

N_GROUPS, EXP_PER_GROUP = 4, 4
N_EXPERTS = 16
D_EXPERT = 512

LANES = 128
SUBLANES = 8
VMEM_LIMIT = 56 * 1024 * 1024

P0_Z, P0_X, P0_Q, P0_BC, P0_K, P0_V = 0, 1024, 2048, 3072, 3584, 3840
P0_W = 4096
P1_Q, P1_K, P1_V, P1_R = 0, 512, 1024, 2048
P1_W = 3072

MOE_TM = 512
MOE_TILES = (2 * T) // MOE_TM + N_EXPERTS
MOE_ROWS = MOE_TILES * MOE_TM
ACC_ROWS = (T + SUBLANES) * SUBLANES


def _cparams(n_axes, vmem=VMEM_LIMIT):
    return pltpu.CompilerParams(dimension_semantics=("arbitrary",) * n_axes, vmem_limit_bytes=vmem)


def _silu(x):
    return x / (1.0 + jnp.exp(-x))


def _softplus(x):
    return jnp.maximum(x, 0.0) + jnp.log(1.0 + jnp.exp(-jnp.abs(x)))


def _mm(a, b):
    return jnp.dot(a.astype(BF16), b.astype(BF16), preferred_element_type=F32)


def _mm_nt(a, b):
    return lax.dot_general(a.astype(BF16), b.astype(BF16), (((1,), (1,)), ((), ())),
                           preferred_element_type=F32)


def _mm_tn(a, b):
    return lax.dot_general(a.astype(BF16), b.astype(BF16), (((0,), (0,)), ((), ())),
                           preferred_element_type=F32)


def _rms(x, w):
    return x * lax.rsqrt(jnp.mean(x * x, axis=-1, keepdims=True) + EPS) * w


def _cumsum_rows(x, n):
    row = lax.broadcasted_iota(jnp.int32, x.shape, 0)
    s = 1
    while s < n:
        x = x + jnp.where(row >= s, pltpu.roll(x, s, 0), 0.0)
        s *= 2
    return x


def _mod_row(tok0):
    return jnp.where(tok0 < T_CTX, 0, 1 + (tok0 - T_CTX) // DEC_SEQ)


ADA_TN = 1536


def _adaln_kernel(c_ref, w_ref, b_ref, o_ref):
    s = _silu(c_ref[...])
    o_ref[0] = _mm(s, w_ref[0]) + b_ref[0]


def _adaln(cond8, w_ada, b_ada):
    depth = w_ada.shape[0]
    return pl.pallas_call(
        _adaln_kernel,
        grid=(depth, 6 * D // ADA_TN),
        in_specs=[
            pl.BlockSpec((SUBLANES, D), lambda l, j: (0, 0)),
            pl.BlockSpec((1, D, ADA_TN), lambda l, j: (l, 0, j)),
            pl.BlockSpec((1, 1, ADA_TN), lambda l, j: (l, 0, j)),
        ],
        out_specs=pl.BlockSpec((1, SUBLANES, ADA_TN), lambda l, j: (l, 0, j)),
        out_shape=jax.ShapeDtypeStruct((depth, SUBLANES, 6 * D), F32),
        compiler_params=_cparams(2),
        name="adaln",
    )(cond8, w_ada, b_ada.reshape(depth, 1, 6 * D))


def _mod_spec(layer, chunk, tm, tile_of=lambda i, *_: i):
    return pl.BlockSpec((1, 1, D), lambda *g: ((layer * SUBLANES + _mod_row(tile_of(*g) * tm)) * 6 + chunk, 0, 0))


def _tokmajor_to_std(ref, tm):
    return jnp.concatenate([ref[pl.ds(k, tm, stride=SUBLANES), :] for k in range(D // LANES)], axis=1)


PROJ_TM = 1024
PROJ_TN = 1024


def _ctx_lat_specs(tm, width=D):
    n_ctx = T_CTX // tm
    return (pl.BlockSpec((tm, width), lambda i, *_: (jnp.minimum(i, n_ctx - 1), 0)),
            pl.BlockSpec((tm, width), lambda i, *_: (jnp.maximum(i - n_ctx, 0), 0)))


def _modproj_kernel(*refs, dual_x):
    it = iter(refs)
    if dual_x:
        xc_ref, xl_ref = next(it), next(it)
    else:
        x_ref, moe_ref, g2_ref = next(it), next(it), next(it)
    sh_ref, sc_ref, nw_ref, wlo_ref, whi_ref, ws_ref, o_ref, os_ref = (next(it) for _ in range(8))
    xo_ref = None if dual_x else next(it)
    h_all, w_bf = next(it), next(it)
    j, i = pl.program_id(0), pl.program_id(1)
    tm = PROJ_TM
    rows = pl.ds(pl.multiple_of(i * tm, tm), tm)

    @pl.when(j == 0)
    def _():
        if dual_x:
            x = jnp.where(i < T_CTX // tm, xc_ref[...], xl_ref[...])
        else:
            x = x_ref[...] + g2_ref[0] * _tokmajor_to_std(moe_ref, tm)
            xo_ref[...] = x
        h = (_rms(x, nw_ref[...]) * (1.0 + sc_ref[0]) + sh_ref[0]).astype(BF16)
        h_all[rows, :] = h
        os_ref[...] = _mm_nt(h, ws_ref[...])

    @pl.when(i == 0)
    def _():
        half = wlo_ref.shape[0]
        w_bf[0:half, :] = wlo_ref[...].astype(BF16)
        w_bf[half:2 * half, :] = whi_ref[...].astype(BF16)

    o_ref[...] = _mm_nt(h_all[rows, :], w_bf[...]).astype(o_ref.dtype)


def _modproj(xs, mods, layer, norm_w, wt, tile_rows, small_row, moe=None):
    tm, tn = PROJ_TM, PROJ_TN
    dual_x = moe is None
    n_tiles = len(tile_rows)
    n_i, n_ctx = T // tm, T_CTX // tm

    def w_row(side):
        def index(j, i):
            r = jnp.int32(tile_rows[0][side])
            for k in range(1, n_tiles):
                r = jnp.where(j == k, tile_rows[k][side], r)
            return pl.multiple_of(r, SUBLANES), 0
        return index

    def tok(j, i):
        return jnp.where(j == 0, i, n_i - 1)

    tile = pl.BlockSpec((tm, D), lambda j, i: (tok(j, i), 0))
    if dual_x:
        in_specs = [pl.BlockSpec((tm, D), lambda j, i: (jnp.minimum(tok(j, i), n_ctx - 1), 0)),
                    pl.BlockSpec((tm, D), lambda j, i: (jnp.maximum(tok(j, i) - n_ctx, 0), 0))]
        args = list(xs)
    else:
        in_specs = [tile, pl.BlockSpec((tm * SUBLANES, LANES), lambda j, i: (tok(j, i), 0)),
                    _mod_spec(layer - 1, 5, tm, tok)]
        args = [xs, moe, mods]
    in_specs += [_mod_spec(layer, 0, tm, tok), _mod_spec(layer, 1, tm, tok), pl.BlockSpec((1, D), lambda j, i: (0, 0)),
                 pl.BlockSpec((pl.Element(tn // 2), pl.Element(D)), w_row(0)),
                 pl.BlockSpec((pl.Element(tn // 2), pl.Element(D)), w_row(1)),
                 pl.BlockSpec((pl.Element(LANES), pl.Element(D)), lambda j, i: (small_row, 0))]
    args += [mods, mods, norm_w.reshape(1, D), wt, wt, wt]
    out_specs = [pl.BlockSpec((tm, tn), lambda j, i: (i, j)),
                 pl.BlockSpec((tm, LANES), lambda j, i: (tok(j, i), 0))]
    out_shape = [jax.ShapeDtypeStruct((T, n_tiles * tn), BF16), jax.ShapeDtypeStruct((T, LANES), F32)]
    if not dual_x:
        out_specs.append(tile)
        out_shape.append(jax.ShapeDtypeStruct((T, D), F32))
    return pl.pallas_call(
        functools.partial(_modproj_kernel, dual_x=dual_x),
        grid=(n_tiles, n_i), in_specs=in_specs, out_specs=out_specs, out_shape=out_shape,
        scratch_shapes=[pltpu.VMEM((T, D), BF16), pltpu.VMEM((tn, D), BF16)],
        compiler_params=_cparams(2),
        name=f"modproj{layer}",
    )(*args)


def _expand_heads(v, off):
    hi = (lax.broadcasted_iota(jnp.int32, (v.shape[0], LANES), 1) >= SSD_P).astype(jnp.int32)
    tiles = [jnp.take_along_axis(v, hi + (off + 2 * q), axis=1) for q in range(SSD_HEADS // 2)]
    return jnp.concatenate(tiles, axis=1)


def _ssd_kernel(*refs, seq, has_h0):
    if has_h0:
        (z_ref, x_ref, bc_ref, dt_ref, cwx_ref, cwbc_ref, cbx_ref, cbbc_ref, dtb_ref, alog_ref, dsk_ref,
         nw_ref, h0f_ref, h0b_ref, y_ref, sf_ref, sb_ref,
         xpad, bcpad, xc, bcc, a_scr, dt_scr, yacc, hf, hb) = refs
    else:
        (z_ref, x_ref, bc_ref, dt_ref, cwx_ref, cwbc_ref, cbx_ref, cbbc_ref, dtb_ref, alog_ref, dsk_ref,
         nw_ref, y_ref, sf_ref, sb_ref,
         xpad, bcpad, xc, bcc, a_scr, dt_scr, yacc, hf, hb) = refs
    L = SSD_L
    nc = seq // L
    pad = SUBLANES
    half = SSD_CONV // 2

    for buf, src, cw, cb, dst in ((xpad, x_ref, cwx_ref, cbx_ref, xc), (bcpad, bc_ref, cwbc_ref, cbbc_ref, bcc)):
        width = buf.shape[1]
        buf[0:pad, :] = jnp.zeros((pad, width), F32)
        buf[pad + seq:2 * pad + seq, :] = jnp.zeros((pad, width), F32)
        buf[pad:pad + seq, :] = src[...].astype(F32)
        for blk in range(nc):
            acc = jnp.broadcast_to(cb[...], (L, width))
            for j in range(SSD_CONV):
                r0 = pad - half + j + blk * L
                acc = acc + cw[j:j + 1, :] * buf[r0:r0 + L, :]
            dst[blk * L:(blk + 1) * L, :] = _silu(acc)

    lane = lax.broadcasted_iota(jnp.int32, (seq, LANES), 1)
    dts = jnp.where(lane < 2 * SSD_HEADS, _softplus(dt_ref[...] + dtb_ref[...]), 0.0)
    dt_scr[...] = dts
    a_scr[...] = dts * (-jnp.exp(alog_ref[...]))

    if has_h0:
        hf[...] = h0f_ref[...]
        hb[...] = h0b_ref[...]
    else:
        hf[...] = jnp.zeros(hf.shape, F32)
        hb[...] = jnp.zeros(hb.shape, F32)

    row = lax.broadcasted_iota(jnp.int32, (L, L), 0)
    col = lax.broadcasted_iota(jnp.int32, (L, L), 1)
    lane_l = lax.broadcasted_iota(jnp.int32, (L, LANES), 1)
    lo_half = lane_l < SSD_P

    def chunk(c, fwd, h_scr):
        off = 0 if fwd else SSD_HEADS
        r0 = pl.multiple_of(c * L, L)
        a = a_scr[pl.ds(r0, L), :]
        dt = dt_scr[pl.ds(r0, L), :]
        cs = _cumsum_rows(a, L)
        total = cs[L - 1:L, :]
        if fwd:
            u = cs
            rvec = jnp.exp(cs)
            ed = jnp.exp(total - cs) * dt
            keep = col <= row
        else:
            ex = cs - a
            u = -ex
            rvec = jnp.exp(total - ex)
            ed = jnp.exp(ex) * dt
            keep = col >= row
        ut = jnp.transpose(u)
        dtt = jnp.transpose(dt)
        tcol = jnp.transpose(jnp.broadcast_to(total, (L, LANES)))[:, 0:1]
        rexp = _expand_heads(rvec, off)
        edexp = _expand_heads(ed, off)
        x = xc[pl.ds(r0, L), :]
        bc = bcc[pl.ds(r0, L), :]
        outs = []
        for g in range(SSD_GROUPS):
            bg = bc[:, g * SSD_N:(g + 1) * SSD_N]
            cg = bc[:, SSD_GROUPS * SSD_N + g * SSD_N:SSD_GROUPS * SSD_N + (g + 1) * SSD_N]
            cbm = _mm_nt(cg, bg)
            hg = h_scr[g * GROUP_W:(g + 1) * GROUP_W, :]
            xg = x[:, g * GROUP_W:(g + 1) * GROUP_W]
            y_off = _mm_nt(cg, hg) * rexp[:, g * GROUP_W:(g + 1) * GROUP_W]
            tiles = []
            for p in range(HEADS_PER_GROUP // 2):
                xt = xg[:, p * LANES:(p + 1) * LANES]
                acc = None
                for s in range(2):
                    h = off + g * HEADS_PER_GROUP + 2 * p + s
                    seg = u[:, h:h + 1] - ut[h:h + 1, :]
                    m = cbm * jnp.exp(jnp.where(keep, seg, -jnp.inf)) * dtt[h:h + 1, :]
                    xm = jnp.where(lo_half if s == 0 else jnp.logical_not(lo_half), xt, 0.0)
                    d = _mm(m, xm)
                    acc = d if acc is None else acc + d
                tiles.append(acc)
            outs.append(y_off + jnp.concatenate(tiles, axis=1))
            decs = []
            for hh in range(HEADS_PER_GROUP):
                h = off + g * HEADS_PER_GROUP + hh
                decs.append(jnp.broadcast_to(jnp.exp(tcol[h:h + 1, :]), (SSD_P, SSD_N)))
            dec = jnp.concatenate(decs, axis=0)
            h_scr[g * GROUP_W:(g + 1) * GROUP_W, :] = dec * hg + _mm_tn(xg * edexp[:, g * GROUP_W:(g + 1) * GROUP_W], bg)
        return r0, x, jnp.concatenate(outs, axis=1)

    def fwd_body(c, carry):
        r0, _, y = chunk(c, True, hf)
        yacc[pl.ds(r0, L), :] = y
        return carry

    lax.fori_loop(0, nc, fwd_body, 0)

    def bwd_body(i, carry):
        c = nc - 1 - i
        r0, x, y = chunk(c, False, hb)
        y = yacc[pl.ds(r0, L), :] + y + dsk_ref[...] * x
        y = y * _silu(z_ref[pl.ds(r0, L), :].astype(F32))
        y_ref[pl.ds(r0, L), :] = _rms(y, nw_ref[...]).astype(y_ref.dtype)
        return carry

    lax.fori_loop(0, nc, bwd_body, 0)
    sf_ref[...] = hf[...]
    sb_ref[...] = hb[...]


def _ssd(p0, p0dt, tok0, nseq, seq, cw, cb, dtb, alog, dskip, nw, h0f=None, h0b=None):
    has_h0 = h0f is not None
    b0 = tok0 // seq

    def cols(width, start):
        return pl.BlockSpec((seq, width), lambda s: (b0 + s, start // width))

    def full(shape):
        return pl.BlockSpec(shape, lambda s: (0,) * len(shape))

    in_specs = [cols(D_SSD, P0_Z), cols(D_SSD, P0_X), cols(512, P0_BC), cols(LANES, 0),
                full((SSD_CONV, D_SSD)), full((SSD_CONV, 512)), full((1, D_SSD)), full((1, 512)),
                full((1, LANES)), full((1, LANES)), full((1, D_SSD)), full((1, D_SSD))]
    args = [p0, p0, p0, p0dt, cw[:, :D_SSD], cw[:, D_SSD:], cb[:D_SSD].reshape(1, -1), cb[D_SSD:].reshape(1, -1),
            dtb, alog, dskip, nw.reshape(1, -1)]
    st_spec = pl.BlockSpec((None, D_SSD, SSD_N), lambda s: (s, 0, 0))
    if has_h0:
        in_specs += [st_spec, st_spec]
        args += [h0f, h0b]
    st_shape = jax.ShapeDtypeStruct((nseq, D_SSD, SSD_N), F32)
    return pl.pallas_call(
        functools.partial(_ssd_kernel, seq=seq, has_h0=has_h0),
        grid=(nseq,), in_specs=in_specs,
        out_specs=[pl.BlockSpec((seq, D_SSD), lambda s: (s, 0)), st_spec, st_spec],
        out_shape=[jax.ShapeDtypeStruct((nseq * seq, D_SSD), BF16), st_shape, st_shape],
        scratch_shapes=[pltpu.VMEM((seq + 2 * SUBLANES, D_SSD), F32), pltpu.VMEM((seq + 2 * SUBLANES, 512), F32),
                        pltpu.VMEM((seq, D_SSD), F32), pltpu.VMEM((seq, 512), F32),
                        pltpu.VMEM((seq, LANES), F32), pltpu.VMEM((seq, LANES), F32),
                        pltpu.VMEM((seq, D_SSD), F32),
                        pltpu.VMEM((D_SSD, SSD_N), F32), pltpu.VMEM((D_SSD, SSD_N), F32)],
        compiler_params=_cparams(1),
        name=f"ssd{seq}",
    )(*args)


def _place_halves(tile, kv_in_high):
    lo = lax.broadcasted_iota(jnp.int32, tile.shape, 1) < ATT_HD
    swapped = pltpu.roll(tile, ATT_HD, 1)
    if kv_in_high:
        return jnp.where(lo, swapped, 0.0), jnp.where(lo, 0.0, tile)
    return jnp.where(lo, tile, 0.0), jnp.where(lo, 0.0, swapped)


def _place_rows(vt, kv_in_high):
    head = vt[ATT_HD:, :] if kv_in_high else vt[:ATT_HD, :]
    z = jnp.zeros_like(head)
    return jnp.concatenate([head, z], axis=0), jnp.concatenate([z, head], axis=0)


LOG2E = 1.4426950408889634
SCORE_SCALE = ATT_SCALE * LOG2E


def _sink_attend_t(score_parts, value_parts, sink2):
    m = sink2
    for s in score_parts:
        m = jnp.maximum(m, jnp.max(s, axis=0, keepdims=True))
    denom = jnp.exp2(sink2 - m)
    out = None
    for s, v in zip(score_parts, value_parts):
        p = jnp.exp2(s - m)
        denom = denom + jnp.sum(p, axis=0, keepdims=True)
        o = _mm(v, p)
        out = o if out is None else out + o
    return out * (1.0 / denom)


def _attn_schedule(n, scores, attend):
    scores(0)
    for j in range(n):
        if j + 1 < n:
            scores(j + 1)
        attend(j)


def _ctx_attn_kernel(q_ref, k_ref, v_ref, sink_ref, o_ref, kt_ref, vt_ref, s_a, s_b):
    sink2 = sink_ref[...] * LOG2E
    bufs = (s_a, s_b)
    half = SEQ // 2
    for t in range(ATT_KV_DIM // LANES):
        cols = slice(t * LANES, (t + 1) * LANES)
        kt_ref[cols, :] = jnp.transpose(k_ref[:, cols].astype(F32))
        vt_ref[cols, :] = jnp.transpose(v_ref[:, cols].astype(F32))

    def kv_tile(ref, j):
        return ref[:, (j // 2) * LANES:(j // 2 + 1) * LANES].astype(F32), (j % 2 == 1)

    def scores(j):
        k_lo, k_hi = _place_halves(*kv_tile(k_ref, j))
        qst = jnp.concatenate([q_ref[:, qt * LANES:(qt + 1) * LANES] for qt in (2 * j, 2 * j + 1)], axis=0)
        bufs[j % 2][...] = _mm_nt(jnp.concatenate([k_lo, k_hi], axis=0), qst) * SCORE_SCALE

    def attend(j):
        src = bufs[j % 2]
        v, high = kv_tile(v_ref, j)
        vts = _place_rows(jnp.transpose(v), high)
        for ql in range(2):
            qt = 2 * j + ql
            for qh in range(2):
                cols = slice(ql * SEQ + qh * half, ql * SEQ + (qh + 1) * half)
                acc = None
                for s, vv in enumerate(vts):
                    o = _sink_attend_t([src[s * SEQ:(s + 1) * SEQ, cols]], [vv], sink2[:, 2 * qt + s:2 * qt + s + 1])
                    acc = o if acc is None else acc + o
                o_ref[qh * half:(qh + 1) * half, qt * LANES:(qt + 1) * LANES] = jnp.transpose(acc).astype(o_ref.dtype)

    _attn_schedule(ATT_KV, scores, attend)


def _ctx_attn(p0, sink):
    def cols(width, start):
        return pl.BlockSpec((SEQ, width), lambda b: (b, start // width))

    sbuf = pltpu.VMEM((2 * SEQ, 2 * SEQ), F32)
    return pl.pallas_call(
        _ctx_attn_kernel,
        grid=(BATCH,),
        in_specs=[cols(D, P0_Q), cols(ATT_KV_DIM, P0_K), cols(ATT_KV_DIM, P0_V),
                  pl.BlockSpec((1, LANES), lambda b: (0, 0))],
        out_specs=[pl.BlockSpec((SEQ, D), lambda b: (b, 0))]
        + [pl.BlockSpec((None, ATT_KV_DIM, SEQ), lambda b: (b, 0, 0))] * 2,
        out_shape=[jax.ShapeDtypeStruct((T_CTX, D), BF16)] + [jax.ShapeDtypeStruct((BATCH, ATT_KV_DIM, SEQ), F32)] * 2,
        scratch_shapes=[sbuf, sbuf],
        compiler_params=_cparams(1),
        name="ctx_attn",
    )(p0, p0, p0, sink)


def _rope_tables():
    quarter = ATT_HD // 4
    t = np.arange(DEC_SEQ)
    lane = np.arange(LANES)
    inv = ROPE_BASE ** (-(lane % quarter).astype(np.float64) / quarter)
    pos = np.where(((lane % ATT_HD) < ATT_HD // 2)[None, :], (t // GRID_W)[:, None], (t % GRID_W)[:, None])
    ang = pos * inv[None, :]
    first = ((lane % (2 * quarter)) < quarter)[None, :]
    cos, sin = np.cos(ang), np.sin(ang)
    return (jnp.asarray(cos, F32), jnp.asarray(np.where(first, -sin, 0.0), F32),
            jnp.asarray(np.where(first, 0.0, sin), F32))


def _rope(x, cos, sa, sb):
    quarter = ATT_HD // 4
    return x * cos + pltpu.roll(x, LANES - quarter, 1) * sa + pltpu.roll(x, quarter, 1) * sb


def _lat_attn_kernel(q_ref, kp_ref, kc_ref, kn_ref, vp_ref, vc_ref, vn_ref, ck_ref, cv_ref,
                     cos_ref, sa_ref, sb_ref, sink_ref, o_ref, c_a, c_b, w_a, w_b):
    blk = pl.program_id(1)
    nb = pl.num_programs(1)
    B = ATT_BLOCK
    sink2 = sink_ref[...] * LOG2E
    cbufs, wbufs = (c_a, c_b), (w_a, w_b)

    def tables(b):
        r0 = pl.multiple_of(b * B, B)
        return cos_ref[pl.ds(r0, B), :], sa_ref[pl.ds(r0, B), :], sb_ref[pl.ds(r0, B), :]

    tq = tables(blk)
    tk = [tables(jnp.maximum(blk - 1, 0)), tq, tables(jnp.minimum(blk + 1, nb - 1))]
    kabs = (blk - 1) * B + lax.broadcasted_iota(jnp.int32, (3 * B, B), 0)
    qpos = blk * B + lax.broadcasted_iota(jnp.int32, (3 * B, B), 1)
    ok = (jnp.abs(qpos - kabs) <= WINDOW) & (kabs >= 0) & (kabs < nb * B)
    ok = jnp.concatenate([ok, ok], axis=1)
    ok = jnp.concatenate([ok, ok], axis=0)

    def scores(j):
        high = (j % 2 == 1)
        sl = slice((j // 2) * LANES, (j // 2 + 1) * LANES)
        kw = jnp.concatenate([_rope(r[:, sl].astype(F32), *tb) for r, tb in zip((kp_ref, kc_ref, kn_ref), tk)], axis=0)
        qs = [q_ref[:, qt * LANES:(qt + 1) * LANES].astype(F32) for qt in (2 * j, 2 * j + 1)]
        q_plain = jnp.concatenate(qs, axis=0)
        q_rope = jnp.concatenate([_rope(q, *tq) for q in qs], axis=0)
        ck = jnp.transpose(ck_ref[sl, :])
        cbufs[j % 2][...] = _mm_nt(jnp.concatenate(_place_halves(ck, high), axis=0), q_plain) * SCORE_SCALE
        win = _mm_nt(jnp.concatenate(_place_halves(kw, high), axis=0), q_rope) * SCORE_SCALE
        wbufs[j % 2][...] = jnp.where(ok, win, -jnp.inf)

    def attend(j):
        high = (j % 2 == 1)
        sl = slice((j // 2) * LANES, (j // 2 + 1) * LANES)
        vw = jnp.concatenate([jnp.transpose(r[:, sl].astype(F32)) for r in (vp_ref, vc_ref, vn_ref)], axis=1)
        vts = _place_rows(vw, high)
        cvts = _place_rows(cv_ref[sl, :], high)
        csrc, wsrc = cbufs[j % 2], wbufs[j % 2]
        for ql in range(2):
            qt = 2 * j + ql
            cols = slice(ql * B, (ql + 1) * B)
            acc = None
            for s in range(2):
                parts = [csrc[s * PAST_LEN:(s + 1) * PAST_LEN, cols], wsrc[s * 3 * B:(s + 1) * 3 * B, cols]]
                o = _sink_attend_t(parts, [cvts[s], vts[s]], sink2[:, 2 * qt + s:2 * qt + s + 1])
                acc = o if acc is None else acc + o
            o_ref[:, qt * LANES:(qt + 1) * LANES] = jnp.transpose(acc).astype(o_ref.dtype)

    _attn_schedule(ATT_KV, scores, attend)


def _lat_attn(p0, ck, cv, sink):
    nb = DEC_SEQ // ATT_BLOCK
    base = T_CTX // ATT_BLOCK

    def kv(start, shift):
        return pl.BlockSpec((ATT_BLOCK, ATT_KV_DIM),
                            lambda b, i: (base + b * nb + jnp.clip(i + shift, 0, nb - 1), start // ATT_KV_DIM))

    def full(shape):
        return pl.BlockSpec(shape, lambda b, i: (0,) * len(shape))

    cache = pl.BlockSpec((None, ATT_KV_DIM, PAST_LEN), lambda b, i: (b, 0, 0))
    cos, sa, sb = _rope_tables()
    return pl.pallas_call(
        _lat_attn_kernel,
        grid=(DEC_BATCH, nb),
        in_specs=[pl.BlockSpec((ATT_BLOCK, D), lambda b, i: (base + b * nb + i, P0_Q // D)),
                  kv(P0_K, -1), kv(P0_K, 0), kv(P0_K, 1), kv(P0_V, -1), kv(P0_V, 0), kv(P0_V, 1),
                  cache, cache, full((DEC_SEQ, LANES)), full((DEC_SEQ, LANES)), full((DEC_SEQ, LANES)),
                  full((1, LANES))],
        out_specs=pl.BlockSpec((ATT_BLOCK, D), lambda b, i: (b * nb + i, 0)),
        out_shape=jax.ShapeDtypeStruct((T_LAT, D), BF16),
        scratch_shapes=[pltpu.VMEM((2 * PAST_LEN, 2 * ATT_BLOCK), F32)] * 2
        + [pltpu.VMEM((2 * 3 * ATT_BLOCK, 2 * ATT_BLOCK), F32)] * 2,
        compiler_params=_cparams(2),
        name="lat_attn",
    )(p0, p0, p0, p0, p0, p0, p0, ck, cv, cos, sa, sb, sink)


def _log_sigmoid(x):
    return jnp.minimum(x, 0.0) - jnp.log(1.0 + jnp.exp(-jnp.abs(x)))


def _gla_kernel(*refs, seq, has_s0):
    if has_s0:
        (q_ref, k_ref, v_ref, r_ref, lr_ref, w2f_ref, w2b_ref, bf_ref, bb_ref, nw_ref, s0f_ref, s0b_ref,
         y_ref, sf_ref, sb_ref, gf, gb, yf, yb, stf, stb) = refs
    else:
        (q_ref, k_ref, v_ref, r_ref, lr_ref, w2f_ref, w2b_ref, bf_ref, bb_ref, nw_ref,
         y_ref, sf_ref, sb_ref, gf, gb, yf, yb, stf, stb) = refs
    C = GLA_C
    nc = seq // C
    lr = lr_ref[...]
    gf[...] = _log_sigmoid(_mm(lr, w2f_ref[...]) + bf_ref[...]) / GLA_GATE_NORM
    gb[...] = _log_sigmoid(_mm(lr, w2b_ref[...]) + bb_ref[...]) / GLA_GATE_NORM
    for h in range(GLA_HEADS):
        rows = slice(h * GLA_DV, (h + 1) * GLA_DV)
        if has_s0:
            stf[rows, :] = jnp.transpose(s0f_ref[h * GLA_DK:(h + 1) * GLA_DK, :])
            stb[rows, :] = jnp.transpose(s0b_ref[h * GLA_DK:(h + 1) * GLA_DK, :])
        else:
            stf[rows, :] = jnp.zeros((GLA_DV, GLA_DK), F32)
            stb[rows, :] = jnp.zeros((GLA_DV, GLA_DK), F32)

    row = lax.broadcasted_iota(jnp.int32, (C, C), 0)
    col = lax.broadcasted_iota(jnp.int32, (C, C), 1)
    qscale = GLA_DK ** -0.5

    def chunk(c, fwd):
        g_scr, y_scr, st = (gf, yf, stf) if fwd else (gb, yb, stb)
        r0 = pl.multiple_of(c * C, C)
        g = g_scr[pl.ds(r0, C), :]
        cs = _cumsum_rows(g, C)
        total = cs[C - 1:C, :]
        q = q_ref[pl.ds(r0, C), :].astype(F32) * qscale
        k = k_ref[pl.ds(r0, C), :].astype(F32)
        v = v_ref[pl.ds(r0, C), :]
        if fwd:
            qs, ks, ke = q * jnp.exp(cs), k * jnp.exp(-cs), k * jnp.exp(total - cs)
            keep = col <= row
        else:
            ex = cs - g
            qs, ks, ke = q * jnp.exp(total - ex), k * jnp.exp(ex - total), k * jnp.exp(ex)
            keep = col >= row
        dec = jnp.exp(total)
        for h in range(GLA_HEADS):
            kc = slice(h * GLA_DK, (h + 1) * GLA_DK)
            vc = slice(h * GLA_DV, (h + 1) * GLA_DV)
            s_t = st[vc, :]
            att = jnp.where(keep, _mm_nt(qs[:, kc], ks[:, kc]), 0.0)
            y_scr[pl.ds(r0, C), vc] = _mm(att, v[:, vc]) + _mm_nt(qs[:, kc], s_t)
            st[vc, :] = dec[:, kc] * s_t + _mm_tn(v[:, vc], ke[:, kc])

    def body(i, carry):
        chunk(i, True)
        chunk(nc - 1 - i, False)
        return carry

    lax.fori_loop(0, nc, body, 0, unroll=4)

    nw = nw_ref[...]
    for blk in range(seq // LANES):
        rs = slice(blk * LANES, (blk + 1) * LANES)
        y = yf[rs, :] + yb[rs, :]
        gate = _silu(r_ref[rs, :].astype(F32))
        for h in range(GLA_HEADS):
            vc = slice(h * GLA_DV, (h + 1) * GLA_DV)
            y_ref[rs, vc] = (_rms(y[:, vc], nw) * gate[:, vc]).astype(y_ref.dtype)
    for h in range(GLA_HEADS):
        rows = slice(h * GLA_DV, (h + 1) * GLA_DV)
        sf_ref[h * GLA_DK:(h + 1) * GLA_DK, :] = jnp.transpose(stf[rows, :])
        sb_ref[h * GLA_DK:(h + 1) * GLA_DK, :] = jnp.transpose(stb[rows, :])


def _gla(p1, p1lr, tok0, nseq, seq, w2f, w2b, bgf, bgb, nw, s0f=None, s0b=None):
    has_s0 = s0f is not None
    b0 = tok0 // seq
    dk_all, dv_all = GLA_HEADS * GLA_DK, GLA_HEADS * GLA_DV

    def cols(width, start):
        return pl.BlockSpec((seq, width), lambda s: (b0 + s, start // width))

    def full(shape):
        return pl.BlockSpec(shape, lambda s: (0,) * len(shape))

    in_specs = [cols(dk_all, P1_Q), cols(dk_all, P1_K), cols(dv_all, P1_V), cols(dv_all, P1_R), cols(LANES, 0),
                full((LANES, dk_all)), full((LANES, dk_all)), full((1, dk_all)), full((1, dk_all)), full((1, GLA_DV))]
    args = [p1, p1, p1, p1, p1lr, w2f, w2b, bgf.reshape(1, -1), bgb.reshape(1, -1), nw.reshape(1, -1)]
    st_spec = pl.BlockSpec((None, dk_all, GLA_DV), lambda s: (s, 0, 0))
    if has_s0:
        in_specs += [st_spec, st_spec]
        args += [s0f, s0b]
    st_shape = jax.ShapeDtypeStruct((nseq, dk_all, GLA_DV), F32)
    return pl.pallas_call(
        functools.partial(_gla_kernel, seq=seq, has_s0=has_s0),
        grid=(nseq,), in_specs=in_specs,
        out_specs=[pl.BlockSpec((seq, dv_all), lambda s: (s, 0)), st_spec, st_spec],
        out_shape=[jax.ShapeDtypeStruct((nseq * seq, dv_all), BF16), st_shape, st_shape],
        scratch_shapes=[pltpu.VMEM((seq, dk_all), F32), pltpu.VMEM((seq, dk_all), F32),
                        pltpu.VMEM((seq, dv_all), F32), pltpu.VMEM((seq, dv_all), F32),
                        pltpu.VMEM((dv_all, GLA_DK), F32), pltpu.VMEM((dv_all, GLA_DK), F32)],
        compiler_params=_cparams(1),
        name=f"gla{seq}",
    )(*args)


ROUTE_TM = 512
ROUTE_SUB = 256
ROUTE_ROWS = 32


def _split_bf16(x):
    hi = x.astype(BF16)
    return hi, (x - hi.astype(F32)).astype(BF16)


def _outproj_kernel(*refs, n_in, dual_x):
    y_refs = refs[:2 * n_in]
    n_x = 2 if dual_x else 1
    x_refs = refs[2 * n_in + 1:2 * n_in + 1 + n_x]
    w_ref = refs[2 * n_in]
    (g1_ref, sh_ref, sc_ref, nw_ref, wr_ref, br_ref,
     xo_ref, tr_ref, route_ref, cnt_ref, w_scr, wr_hl, carry, o_scr) = refs[2 * n_in + 1 + n_x:]
    is_ctx = pl.program_id(0) < T_CTX // ROUTE_TM

    @pl.when(pl.program_id(0) == 0)
    def _():
        w_scr[...] = w_ref[...].astype(BF16)
        hi, lo = _split_bf16(jnp.transpose(wr_ref[...]))
        wr_hl[0:LANES, :] = hi
        wr_hl[LANES:2 * LANES, :] = lo
        carry[...] = jnp.zeros(carry.shape, F32)

    o = None
    for i in range(n_in):
        y = jnp.where(is_ctx, y_refs[2 * i][...], y_refs[2 * i + 1][...])
        d = jnp.dot(y, w_scr[i * D:(i + 1) * D, :], preferred_element_type=F32)
        o = d if o is None else o + d
    o_scr[...] = o
    for sub in range(ROUTE_TM // ROUTE_SUB):
        _outproj_subtile(sub, is_ctx, o_scr, x_refs, dual_x, g1_ref, sh_ref, sc_ref, nw_ref, br_ref,
                         xo_ref, tr_ref, route_ref, cnt_ref, wr_hl, carry)


def _outproj_subtile(sub, is_ctx, o_scr, x_refs, dual_x, g1_ref, sh_ref, sc_ref, nw_ref, br_ref,
                     xo_ref, tr_ref, route_ref, cnt_ref, wr_hl, carry):
    tm = ROUTE_SUB
    rows = slice(sub * tm, (sub + 1) * tm)
    x_in = jnp.where(is_ctx, x_refs[0][rows, :], x_refs[1][rows, :]) if dual_x else x_refs[0][rows, :]
    x = x_in + g1_ref[0] * o_scr[rows, :]
    xo_ref[rows, :] = x
    t = _rms(x, nw_ref[...]) * (1.0 + sc_ref[0]) + sh_ref[0]
    for k in range(D // LANES):
        tr_ref[pl.ds(sub * tm * SUBLANES + k, tm, stride=SUBLANES), :] = t[:, k * LANES:(k + 1) * LANES]

    t_hi, t_lo = _split_bf16(t)
    lg = _mm_nt(wr_hl[...], t_hi)
    nr = ROUTE_ROWS
    logit = lg[0:nr, :] + lg[LANES:LANES + nr, :] + _mm_nt(wr_hl[0:LANES, :], t_lo)[0:nr, :] + br_ref[0:nr, :]
    rowf = lax.broadcasted_iota(jnp.int32, (nr, tm), 0).astype(F32)
    neg = -jnp.inf

    def first_argmax(v, vmax):
        return jnp.min(jnp.where(v == vmax, rowf, float(LANES)), axis=0, keepdims=True)

    gl = jnp.where(rowf < N_GROUPS, logit, neg)
    gmax = jnp.max(gl, axis=0, keepdims=True)
    gsel = first_argmax(gl, gmax)
    gprob = 1.0 / jnp.sum(jnp.exp(gl - gmax), axis=0, keepdims=True)
    first = N_GROUPS + EXP_PER_GROUP * gsel
    el = jnp.where((rowf >= first) & (rowf < first + EXP_PER_GROUP), logit, neg)
    m1 = jnp.max(el, axis=0, keepdims=True)
    i1 = first_argmax(el, m1)
    el2 = jnp.where(rowf == i1, neg, el)
    m2 = jnp.max(el2, axis=0, keepdims=True)
    i2 = first_argmax(el2, m2)
    e2 = jnp.exp(m2 - m1)
    c1 = gprob / (1.0 + e2)
    c2 = gprob * e2 / (1.0 + e2)
    x1 = i1 - N_GROUPS
    x2 = i2 - N_GROUPS

    erow = rowf
    hot = ((erow == x1) | (erow == x2)).astype(F32)
    tri = (lax.broadcasted_iota(jnp.int32, (tm, tm), 0) < lax.broadcasted_iota(jnp.int32, (tm, tm), 1))
    before = _mm(hot, tri.astype(F32)) + carry[...]
    r1 = jnp.sum(jnp.where(erow == x1, before, 0.0), axis=0, keepdims=True)
    r2 = jnp.sum(jnp.where(erow == x2, before, 0.0), axis=0, keepdims=True)
    total = carry[...] + _mm(hot, jnp.ones((tm, tm), F32))
    carry[...] = total
    cnt_ref[...] = total[0:N_EXPERTS, 0:LANES]
    row8 = lax.broadcasted_iota(jnp.int32, (SUBLANES, tm), 0)
    out = jnp.zeros((SUBLANES, tm), F32)
    for k, v in enumerate((x1, x2, c1, c2, r1, r2)):
        out = jnp.where(row8 == k, jnp.broadcast_to(v, (SUBLANES, tm)), out)
    route_ref[:, rows] = out


def _outproj_route(ys, w_out, xs, mods, layer, norm_w, w_router, b_router):
    tm = ROUTE_TM
    n_in = len(ys) // 2
    dual_x = len(xs) == 2
    kdim = w_out.shape[0]

    def full(shape):
        return pl.BlockSpec(shape, lambda i: (0,) * len(shape))

    tile = pl.BlockSpec((tm, D), lambda i: (i, 0))
    pair = list(_ctx_lat_specs(tm))
    in_specs = (pair * n_in + [full((kdim, D))] + (pair if dual_x else [tile])
                + [_mod_spec(layer, 2, tm), _mod_spec(layer, 3, tm), _mod_spec(layer, 4, tm),
                   full((1, D)), full((D, LANES)), full((LANES, ROUTE_SUB))])
    cnt = jax.ShapeDtypeStruct((N_EXPERTS, LANES), F32)
    return pl.pallas_call(
        functools.partial(_outproj_kernel, n_in=n_in, dual_x=dual_x),
        grid=(T // tm,), in_specs=in_specs,
        out_specs=[tile, pl.BlockSpec((tm * SUBLANES, LANES), lambda i: (i, 0)),
                   pl.BlockSpec((SUBLANES, tm), lambda i: (0, i)), full(cnt.shape)],
        out_shape=[jax.ShapeDtypeStruct((T, D), F32), jax.ShapeDtypeStruct((T * SUBLANES, LANES), F32),
                   jax.ShapeDtypeStruct((SUBLANES, T), F32), cnt],
        scratch_shapes=[pltpu.VMEM((kdim, D), BF16), pltpu.VMEM((2 * LANES, D), BF16),
                        pltpu.VMEM((ROUTE_ROWS, ROUTE_SUB), F32), pltpu.VMEM((tm, D), F32)],
        compiler_params=_cparams(1),
        name=f"outproj{layer}",
    )(*ys, w_out, *xs, mods, mods, mods, norm_w.reshape(1, D), w_router,
      jnp.broadcast_to(b_router.reshape(LANES, 1), (LANES, ROUTE_SUB)))


def _moe_meta(counts):
    tm = MOE_TM
    experts = jnp.arange(N_EXPERTS, dtype=jnp.int32)
    counts = jnp.max(counts, axis=1).astype(jnp.int32)
    padded = ((counts + tm - 1) // tm) * tm
    ends = jnp.cumsum(padded)
    tile_start = jnp.arange(MOE_TILES, dtype=jnp.int32) * tm
    te = jnp.sum((tile_start[:, None] >= ends[None, :]).astype(jnp.int32), axis=1)
    last = jnp.max(jnp.where(counts > 0, experts, 0))
    meta = jnp.concatenate([jnp.minimum(te, last), ends[-1:] // tm]).astype(jnp.int32)
    starts = ends - padded
    pads = jnp.concatenate([starts + counts, ends[-1:], ends, jnp.full((1,), MOE_ROWS)]).astype(jnp.int32)
    return starts.astype(jnp.int32), pads, meta


def _expert_changed(meta_ref, j):
    return (j == 0) | (meta_ref[j] != meta_ref[jnp.maximum(j - 1, 0)])


def _moe_up_kernel(pos1_ref, pos2_ref, pads_ref, meta_ref, tr_hbm, wg_ref, wu_ref, a_ref, rowmap_ref,
                   tr_scr, g0, g1, wg_bf, wu_bf, sem):
    j = pl.program_id(0)
    tm = MOE_TM
    ntiles = meta_ref[MOE_TILES]

    def gather(tile, dst):
        for mi in range(tm):
            tok = jnp.minimum(rowmap_ref[tile * tm + mi] >> 1, T - 1)
            dst[mi * SUBLANES:(mi + 1) * SUBLANES, :] = tr_scr[pl.ds(pl.multiple_of(tok * SUBLANES, SUBLANES), SUBLANES), :]

    @pl.when(j == 0)
    def _():
        load = pltpu.make_async_copy(tr_hbm, tr_scr, sem)
        load.start()

        def clear(c, carry):
            for i in range(SUBLANES):
                rowmap_ref[c * SUBLANES + i] = 2 * T
            return carry
        for k in range(N_EXPERTS + 1):
            lax.fori_loop(pads_ref[k] // SUBLANES, pads_ref[N_EXPERTS + 1 + k] // SUBLANES, clear, 0)

        def place(t, carry):
            rowmap_ref[pos1_ref[t]] = 2 * t
            rowmap_ref[pos2_ref[t]] = 2 * t + 1
            return carry
        lax.fori_loop(0, T, place, 0, unroll=8)
        load.wait()
        gather(0, g0)

    def compute(cur, nxt):
        gather(jnp.minimum(j + 1, ntiles - 1), nxt)
        x = _tokmajor_to_std(cur, tm).astype(BF16)
        g = jnp.dot(x, wg_bf[...], preferred_element_type=F32)
        u = jnp.dot(x, wu_bf[...], preferred_element_type=F32)
        a_ref[...] = (_silu(g) * u).astype(a_ref.dtype)

    @pl.when(j < ntiles)
    def _():
        @pl.when(_expert_changed(meta_ref, j))
        def _():
            wg_bf[...] = wg_ref[...].astype(BF16)
            wu_bf[...] = wu_ref[...].astype(BF16)

        pl.when(j % 2 == 0)(functools.partial(compute, g0, g1))
        pl.when(j % 2 == 1)(functools.partial(compute, g1, g0))

    @pl.when(j >= ntiles)
    def _():
        a_ref[...] = jnp.zeros(a_ref.shape, a_ref.dtype)


def _moe_down_kernel(rowmap_ref, cpair_ref, meta_ref, a_ref, wd_ref, out_hbm, acc, y0, y1, wd_bf):
    j = pl.program_id(0)
    tm = MOE_TM
    zrows = 512
    ntiles = meta_ref[MOE_TILES]

    @pl.when(j == 0)
    def _():
        def zero(i, carry):
            acc[pl.ds(pl.multiple_of(i * zrows, zrows), zrows), :] = jnp.zeros((zrows, LANES), F32)
            return carry
        lax.fori_loop(0, ACC_ROWS // zrows, zero, 0)

    def matmul(dst):
        y = jnp.dot(a_ref[...], wd_bf[...], preferred_element_type=F32)
        for k in range(D // LANES):
            dst[pl.ds(k, tm, stride=SUBLANES), :] = y[:, k * LANES:(k + 1) * LANES]

    def scatter(tile, src):
        for b in range(tm // SUBLANES):
            ents = [rowmap_ref[tile * tm + b * SUBLANES + i] for i in range(SUBLANES)]
            offs = [pl.multiple_of((e >> 1) * SUBLANES, SUBLANES) for e in ents]
            olds = [acc[pl.ds(o, SUBLANES), :] for o in offs]
            for i, o in enumerate(offs):
                r = (b * SUBLANES + i) * SUBLANES
                acc[pl.ds(o, SUBLANES), :] = olds[i] + cpair_ref[ents[i]] * src[r:r + SUBLANES, :]

    has_mm = j < ntiles
    has_sc = (j >= 1) & (j <= ntiles)

    @pl.when(has_mm)
    def _():
        @pl.when(_expert_changed(meta_ref, j))
        def _():
            wd_bf[...] = wd_ref[...].astype(BF16)

    for par, (cur, prev) in enumerate(((y0, y1), (y1, y0))):
        mine = (j % 2) == par

        @pl.when(mine & has_mm & has_sc)
        def _():
            matmul(cur)
            scatter(j - 1, prev)

        @pl.when(mine & has_mm & jnp.logical_not(has_sc))
        def _():
            matmul(cur)

        @pl.when(mine & jnp.logical_not(has_mm) & has_sc)
        def _():
            scatter(j - 1, prev)

    @pl.when(j == pl.num_programs(0) - 1)
    def _():
        pltpu.sync_copy(acc.at[0:T * SUBLANES, :], out_hbm)


def _moe(tr, route_t, counts, layer, w_gate, w_up, w_down):
    tm = MOE_TM
    starts, pads, meta = _moe_meta(counts)
    experts = jnp.arange(N_EXPERTS, dtype=jnp.int32)

    def position(e, r):
        sel = e.astype(jnp.int32)[:, None] == experts[None, :]
        return jnp.sum(jnp.where(sel, starts[None, :], 0), axis=1) + r.astype(jnp.int32)

    pos1 = position(route_t[0], route_t[4])
    pos2 = position(route_t[1], route_t[5])
    cpair = jnp.concatenate([jnp.stack([route_t[2], route_t[3]], axis=1).reshape(2 * T), jnp.zeros((2,), F32)])

    def wspec(shape, n):
        return pl.BlockSpec((None, None) + shape,
                            lambda j, *pre: (layer, pre[n - 1][jnp.minimum(j, MOE_TILES - 1)], 0, 0))

    gscr = pltpu.VMEM((tm * SUBLANES, LANES), F32)
    act, rowmap = pl.pallas_call(
        _moe_up_kernel,
        grid_spec=pltpu.PrefetchScalarGridSpec(
            num_scalar_prefetch=4, grid=(MOE_TILES,),
            in_specs=[pl.BlockSpec(memory_space=pl.ANY), wspec((D, D_EXPERT), 4), wspec((D, D_EXPERT), 4)],
            out_specs=[pl.BlockSpec((tm, D_EXPERT), lambda j, *pre: (j, 0)), pl.BlockSpec(memory_space=pltpu.SMEM)],
            scratch_shapes=[pltpu.VMEM((T * SUBLANES, LANES), F32), gscr, gscr,
                            pltpu.VMEM((D, D_EXPERT), BF16), pltpu.VMEM((D, D_EXPERT), BF16),
                            pltpu.SemaphoreType.DMA(())]),
        out_shape=[jax.ShapeDtypeStruct((MOE_ROWS, D_EXPERT), BF16), jax.ShapeDtypeStruct((MOE_ROWS,), jnp.int32)],
        compiler_params=_cparams(1),
        name=f"moe_up{layer}",
    )(pos1, pos2, pads, meta, tr, w_gate, w_up)

    return pl.pallas_call(
        _moe_down_kernel,
        grid_spec=pltpu.PrefetchScalarGridSpec(
            num_scalar_prefetch=3, grid=(MOE_TILES + 1,),
            in_specs=[pl.BlockSpec((tm, D_EXPERT), lambda j, *pre: (jnp.minimum(j, MOE_TILES - 1), 0)),
                      wspec((D_EXPERT, D), 3)],
            out_specs=pl.BlockSpec(memory_space=pl.ANY),
            scratch_shapes=[pltpu.VMEM((ACC_ROWS, LANES), F32), gscr, gscr, pltpu.VMEM((D_EXPERT, D), BF16)]),
        out_shape=jax.ShapeDtypeStruct((T * SUBLANES, LANES), F32),
        compiler_params=_cparams(1),
        name=f"moe_down{layer}",
    )(rowmap, cpair, meta, act, w_down)


FINAL_TM = 512


def _final_kernel(x_ref, moe_ref, g2_ref, nw_ref, oc_ref, ol_ref):
    x = x_ref[...] + g2_ref[0] * _tokmajor_to_std(moe_ref, FINAL_TM)
    y = _rms(x, nw_ref[...])
    is_ctx = pl.program_id(0) < T_CTX // FINAL_TM

    @pl.when(is_ctx)
    def _():
        oc_ref[...] = y

    @pl.when(jnp.logical_not(is_ctx))
    def _():
        ol_ref[...] = y


def _final(x, moe, mods, layer, norm_w):
    tm = FINAL_TM
    tile = pl.BlockSpec((tm, D), lambda i: (i, 0))
    return pl.pallas_call(
        _final_kernel,
        grid=(T // tm,),
        in_specs=[tile, pl.BlockSpec((tm * SUBLANES, LANES), lambda i: (i, 0)), _mod_spec(layer, 5, tm),
                  pl.BlockSpec((1, D), lambda i: (0, 0))],
        out_specs=list(_ctx_lat_specs(tm)),
        out_shape=[jax.ShapeDtypeStruct((T_CTX, D), F32), jax.ShapeDtypeStruct((T_LAT, D), F32)],
        compiler_params=_cparams(1),
        name="final_norm",
    )(x, moe, mods, norm_w.reshape(1, D))


def _pad_lanes(v):
    return jnp.pad(v.astype(F32), (0, LANES - v.shape[0])).reshape(1, LANES)


def kernel(x_prompt, x_sample, cache_k_attn, cache_v_attn, state_ssd_fwd, state_ssd_bwd, state_gla_fwd, state_gla_bwd, c, c_ctx, w_ada, b_ada, norm_mix_w, norm_ffn_w, w_in_even, conv_w, conv_b, dt_bias_fwd, dt_bias_bwd, a_log_fwd, a_log_bwd, d_skip, ssd_norm_w, attn_sink, w_out_even, w_in_odd, w_gk2_fwd, b_gk_fwd, w_gk2_bwd, b_gk_bwd, gla_norm_w, w_out_odd, w_router_group, b_router_group, w_router_expert, b_router_expert, w_gate_exp, w_up_exp, w_down_exp, final_norm_w):
    depth = w_ada.shape[0]
    assert depth == 2 and x_prompt.shape == (BATCH, SEQ, D) and x_sample.shape == (DEC_BATCH, DEC_SEQ, D)

    cond8 = jnp.concatenate([c_ctx[None, :], c, jnp.zeros((SUBLANES - 1 - DEC_BATCH, D), F32)], axis=0)
    mods = _adaln(cond8, w_ada, b_ada).reshape(depth * SUBLANES * 6, 1, D)
    xs0 = (x_prompt.reshape(T_CTX, D), x_sample.reshape(T_LAT, D))

    def router_params(i):
        wr = jnp.concatenate([w_router_group[i], w_router_expert[i],
                              jnp.zeros((D, LANES - N_GROUPS - N_EXPERTS), F32)], axis=1)
        return wr, _pad_lanes(jnp.concatenate([b_router_group[i], b_router_expert[i]]))

    n_zxbc = 2 * D_SSD + 2 * SSD_GROUPS * SSD_N
    n_dt = 2 * SSD_HEADS
    half = PROJ_TN // 2
    r_bc, r_q = 2 * D_SSD, n_zxbc + n_dt
    r_kv = r_q + D
    rows0 = [(0, half), (D_SSD, D_SSD + half), (r_q, r_q + half), (r_bc, r_kv)]
    p0, p0dt = _modproj(xs0, mods, 0, norm_mix_w[0], jnp.transpose(w_in_even[0]), rows0, n_zxbc)

    dtb = _pad_lanes(jnp.concatenate([dt_bias_fwd[0], dt_bias_bwd[0]]))
    alog = _pad_lanes(jnp.concatenate([a_log_fwd[0], a_log_bwd[0]]))
    dskip = jnp.repeat(d_skip[0], SSD_P).reshape(1, D_SSD)
    ssd_args = (conv_w[0], conv_b[0], dtb, alog, dskip, ssd_norm_w[0])
    y_ssd_c, ssd_f, ssd_b = _ssd(p0, p0dt, 0, BATCH, SEQ, *ssd_args)
    y_ssd_l, _, _ = _ssd(p0, p0dt, T_CTX, DEC_BATCH, DEC_SEQ, *ssd_args,
                         h0f=state_ssd_fwd[:, 0].reshape(DEC_BATCH, D_SSD, SSD_N),
                         h0b=state_ssd_bwd[:, 0].reshape(DEC_BATCH, D_SSD, SSD_N))
    sink = _pad_lanes(attn_sink[0])
    y_att_c, new_kt, new_vt = _ctx_attn(p0, sink)
    def cache_in(t):
        return jnp.transpose(t[:, 0], (0, 2, 3, 1)).reshape(DEC_BATCH, ATT_KV_DIM, PAST_LEN)

    y_att_l = _lat_attn(p0, cache_in(cache_k_attn), cache_in(cache_v_attn), sink)
    xmid0, tr0, route0, cnt0 = _outproj_route([y_ssd_c, y_ssd_l, y_att_c, y_att_l], w_out_even[0], xs0, mods, 0,
                                              norm_ffn_w[0], *router_params(0))
    moe0 = _moe(tr0, route0, cnt0, 0, w_gate_exp, w_up_exp, w_down_exp)

    dk_all = GLA_HEADS * GLA_DK
    n_qkvr = 2 * dk_all + 2 * GLA_HEADS * GLA_DV
    n_odd = w_in_odd.shape[2]
    p1, p1lr, x1 = _modproj(xmid0, mods, 1, norm_mix_w[1], jnp.transpose(w_in_odd[0]),
                            [(k * PROJ_TN, k * PROJ_TN + half) for k in range(n_qkvr // PROJ_TN)], n_odd - LANES,
                            moe=moe0)
    lr0 = LANES - 2 * GLA_LOWRANK
    w2f = jnp.zeros((LANES, dk_all), F32).at[lr0:lr0 + GLA_LOWRANK].set(w_gk2_fwd[0])
    w2b = jnp.zeros((LANES, dk_all), F32).at[lr0 + GLA_LOWRANK:].set(w_gk2_bwd[0])
    gla_args = (w2f, w2b, b_gk_fwd[0], b_gk_bwd[0], gla_norm_w[0])
    y_gla_c, gla_f, gla_b = _gla(p1, p1lr, 0, BATCH, SEQ, *gla_args)
    y_gla_l, _, _ = _gla(p1, p1lr, T_CTX, DEC_BATCH, DEC_SEQ, *gla_args,
                         s0f=state_gla_fwd[:, 0].reshape(DEC_BATCH, dk_all, GLA_DV),
                         s0b=state_gla_bwd[:, 0].reshape(DEC_BATCH, dk_all, GLA_DV))
    xmid1, tr1, route1, cnt1 = _outproj_route([y_gla_c, y_gla_l], w_out_odd[0], (x1,), mods, 1,
                                              norm_ffn_w[1], *router_params(1))
    moe1 = _moe(tr1, route1, cnt1, 1, w_gate_exp, w_up_exp, w_down_exp)
    y_c, y_l = _final(xmid1, moe1, mods, 1, final_norm_w)

    y_prompt = y_c.reshape(BATCH, SEQ, D)
    y_sample = y_l.reshape(DEC_BATCH, DEC_SEQ, D)
    def cache_out(t):
        return jnp.transpose(t.reshape(BATCH, 1, ATT_KV, ATT_HD, SEQ), (0, 1, 4, 2, 3))

    new_k, new_v = cache_out(new_kt), cache_out(new_vt)
    return (y_prompt, y_sample, new_k, new_v,
            ssd_f.reshape(BATCH, 1, SSD_HEADS, SSD_P, SSD_N), ssd_b.reshape(BATCH, 1, SSD_HEADS, SSD_P, SSD_N),
            gla_f.reshape(BATCH, 1, GLA_HEADS, GLA_DK, GLA_DV), gla_b.reshape(BATCH, 1, GLA_HEADS, GLA_DK, GLA_DV))
```

```python
import functools
import math

import numpy as np
import jax
import jax.numpy as jnp
from jax import lax
from jax.experimental import pallas as pl
from jax.experimental.pallas import tpu as pltpu

F32 = jnp.float32
BF16 = jnp.bfloat16

D = 1024
BATCH, SEQ = 16, 256
DEC_BATCH, DEC_SEQ = 2, 1024
PAST_LEN = 512
GRID_W = 64
EPS = 1e-6
T_CTX = BATCH * SEQ
T_LAT = DEC_BATCH * DEC_SEQ
T = T_CTX + T_LAT

SSD_HEADS, SSD_P, SSD_N, SSD_GROUPS = 16, 64, 128, 2
SSD_CONV = 5
SSD_L = 128
D_SSD = SSD_HEADS * SSD_P
HEADS_PER_GROUP = SSD_HEADS // SSD_GROUPS
GROUP_W = HEADS_PER_GROUP * SSD_P

ATT_HEADS, ATT_KV, ATT_HD = 16, 4, 64
ATT_KV_DIM = ATT_KV * ATT_HD
WINDOW = 128
ATT_BLOCK = 128
ATT_SCALE = ATT_HD ** -0.5
ROPE_BASE = 10000.0

GLA_HEADS, GLA_DK, GLA_DV = 4, 128, 256
GLA_C = 64
GLA_GATE_NORM = 16.0
GLA_LOWRANK = 16

N_GROUPS, EXP_PER_GROUP = 4, 4
N_EXPERTS = 16
D_EXPERT = 512

LANES = 128
SUBLANES = 8
VMEM_LIMIT = 56 * 1024 * 1024

P0_Z, P0_X, P0_Q, P0_BC, P0_K, P0_V = 0, 1024, 2048, 3072, 3584, 3840
P0_W = 4096
P1_Q, P1_K, P1_V, P1_R = 0, 512, 1024, 2048
P1_W = 3072

MOE_TM = 512
MOE_TILES = (2 * T) // MOE_TM + N_EXPERTS
MOE_ROWS = MOE_TILES * MOE_TM
ACC_ROWS = (T + SUBLANES) * SUBLANES
MERGE_TM = 512


def _cparams(n_axes, vmem=VMEM_LIMIT):
    return pltpu.CompilerParams(dimension_semantics=("arbitrary",) * n_axes, vmem_limit_bytes=vmem)


def _silu(x):
    return x / (1.0 + jnp.exp(-x))


def _softplus(x):
    return jnp.maximum(x, 0.0) + jnp.log(1.0 + jnp.exp(-jnp.abs(x)))


def _mm(a, b):
    return jnp.dot(a.astype(BF16), b.astype(BF16), preferred_element_type=F32)


def _mm_nt(a, b):
    return lax.dot_general(a.astype(BF16), b.astype(BF16), (((1,), (1,)), ((), ())),
                           preferred_element_type=F32)


def _mm_tn(a, b):
    return lax.dot_general(a.astype(BF16), b.astype(BF16), (((0,), (0,)), ((), ())),
                           preferred_element_type=F32)


def _rms(x, w):
    return x * lax.rsqrt(jnp.mean(x * x, axis=-1, keepdims=True) + EPS) * w


def _cumsum_rows(x, n):
    row = lax.broadcasted_iota(jnp.int32, x.shape, 0)
    s = 1
    while s < n:
        x = x + jnp.where(row >= s, pltpu.roll(x, s, 0), 0.0)
        s *= 2
    return x


def _mod_row(tok0):
    return jnp.where(tok0 < T_CTX, 0, 1 + (tok0 - T_CTX) // DEC_SEQ)


ADA_TN = 1536


def _adaln_kernel(c_ref, w_ref, b_ref, o_ref):
    s = _silu(c_ref[...])
    o_ref[0] = _mm(s, w_ref[0]) + b_ref[0]


def _adaln(cond8, w_ada, b_ada):
    depth = w_ada.shape[0]
    return pl.pallas_call(
        _adaln_kernel,
        grid=(depth, 6 * D // ADA_TN),
        in_specs=[
            pl.BlockSpec((SUBLANES, D), lambda l, j: (0, 0)),
            pl.BlockSpec((1, D, ADA_TN), lambda l, j: (l, 0, j)),
            pl.BlockSpec((1, 1, ADA_TN), lambda l, j: (l, 0, j)),
        ],
        out_specs=pl.BlockSpec((1, SUBLANES, ADA_TN), lambda l, j: (l, 0, j)),
        out_shape=jax.ShapeDtypeStruct((depth, SUBLANES, 6 * D), F32),
        compiler_params=_cparams(2),
        name="adaln",
    )(cond8, w_ada, b_ada.reshape(depth, 1, 6 * D))


def _mod_spec(layer, chunk, tm, tile_of=lambda i, *_: i):
    return pl.BlockSpec((1, 1, D), lambda *g: ((layer * SUBLANES + _mod_row(tile_of(*g) * tm)) * 6 + chunk, 0, 0))


def _tokmajor_to_std(ref, tm):
    return jnp.concatenate([ref[pl.ds(k, tm, stride=SUBLANES), :] for k in range(D // LANES)], axis=1)


PROJ_TM = 1024
PROJ_TN = 1024


def _ctx_lat_specs(tm, width=D):
    n_ctx = T_CTX // tm
    return (pl.BlockSpec((tm, width), lambda i, *_: (jnp.minimum(i, n_ctx - 1), 0)),
            pl.BlockSpec((tm, width), lambda i, *_: (jnp.maximum(i - n_ctx, 0), 0)))


def _modproj_kernel(*refs, dual_x):
    it = iter(refs)
    if dual_x:
        xc_ref, xl_ref = next(it), next(it)
    else:
        x_ref = next(it)
    sh_ref, sc_ref, nw_ref, wlo_ref, whi_ref, ws_ref, o_ref, os_ref, h_all, w_bf = (next(it) for _ in range(10))
    j, i = pl.program_id(0), pl.program_id(1)
    tm = PROJ_TM
    rows = pl.ds(pl.multiple_of(i * tm, tm), tm)

    @pl.when(j == 0)
    def _():
        x = jnp.where(i < T_CTX // tm, xc_ref[...], xl_ref[...]) if dual_x else x_ref[...]
        h = (_rms(x, nw_ref[...]) * (1.0 + sc_ref[0]) + sh_ref[0]).astype(BF16)
        h_all[rows, :] = h
        os_ref[...] = _mm_nt(h, ws_ref[...])

    @pl.when(i == 0)
    def _():
        half = wlo_ref.shape[0]
        w_bf[0:half, :] = wlo_ref[...].astype(BF16)
        w_bf[half:2 * half, :] = whi_ref[...].astype(BF16)

    o_ref[...] = _mm_nt(h_all[rows, :], w_bf[...]).astype(o_ref.dtype)


def _modproj(xs, mods, layer, norm_w, wt, tile_rows, small_row):
    tm, tn = PROJ_TM, PROJ_TN
    dual_x = isinstance(xs, tuple)
    n_tiles = len(tile_rows)
    n_i, n_ctx = T // tm, T_CTX // tm

    def w_row(side):
        def index(j, i):
            r = jnp.int32(tile_rows[0][side])
            for k in range(1, n_tiles):
                r = jnp.where(j == k, tile_rows[k][side], r)
            return pl.multiple_of(r, SUBLANES), 0
        return index

    def tok(j, i):
        return jnp.where(j == 0, i, n_i - 1)

    tile = pl.BlockSpec((tm, D), lambda j, i: (tok(j, i), 0))
    if dual_x:
        in_specs = [pl.BlockSpec((tm, D), lambda j, i: (jnp.minimum(tok(j, i), n_ctx - 1), 0)),
                    pl.BlockSpec((tm, D), lambda j, i: (jnp.maximum(tok(j, i) - n_ctx, 0), 0))]
        args = list(xs)
    else:
        in_specs = [tile]
        args = [xs]
    in_specs += [_mod_spec(layer, 0, tm, tok), _mod_spec(layer, 1, tm, tok), pl.BlockSpec((1, D), lambda j, i: (0, 0)),
                 pl.BlockSpec((pl.Element(tn // 2), pl.Element(D)), w_row(0)),
                 pl.BlockSpec((pl.Element(tn // 2), pl.Element(D)), w_row(1)),
                 pl.BlockSpec((pl.Element(LANES), pl.Element(D)), lambda j, i: (small_row, 0))]
    args += [mods, mods, norm_w.reshape(1, D), wt, wt, wt]
    out_specs = [pl.BlockSpec((tm, tn), lambda j, i: (i, j)),
                 pl.BlockSpec((tm, LANES), lambda j, i: (tok(j, i), 0))]
    out_shape = [jax.ShapeDtypeStruct((T, n_tiles * tn), BF16), jax.ShapeDtypeStruct((T, LANES), F32)]
    return pl.pallas_call(
        functools.partial(_modproj_kernel, dual_x=dual_x),
        grid=(n_tiles, n_i), in_specs=in_specs, out_specs=out_specs, out_shape=out_shape,
        scratch_shapes=[pltpu.VMEM((T, D), BF16), pltpu.VMEM((tn, D), BF16)],
        compiler_params=_cparams(2),
        name=f"modproj{layer}",
    )(*args)


def _expand_heads(v, off):
    hi = (lax.broadcasted_iota(jnp.int32, (v.shape[0], LANES), 1) >= SSD_P).astype(jnp.int32)
    tiles = [jnp.take_along_axis(v, hi + (off + 2 * q), axis=1) for q in range(SSD_HEADS // 2)]
    return jnp.concatenate(tiles, axis=1)


def _ssd_kernel(*refs, seq, has_h0):
    if has_h0:
        (z_ref, x_ref, bc_ref, dt_ref, cwx_ref, cwbc_ref, cbx_ref, cbbc_ref, dtb_ref, alog_ref, dsk_ref,
         nw_ref, h0f_ref, h0b_ref, y_ref, sf_ref, sb_ref,
         xpad, bcpad, xc, bcc, a_scr, dt_scr, yacc, hf, hb) = refs
    else:
        (z_ref, x_ref, bc_ref, dt_ref, cwx_ref, cwbc_ref, cbx_ref, cbbc_ref, dtb_ref, alog_ref, dsk_ref,
         nw_ref, y_ref, sf_ref, sb_ref,
         xpad, bcpad, xc, bcc, a_scr, dt_scr, yacc, hf, hb) = refs
    L = SSD_L
    nc = seq // L
    pad = SUBLANES
    half = SSD_CONV // 2

    for buf, src, cw, cb, dst in ((xpad, x_ref, cwx_ref, cbx_ref, xc), (bcpad, bc_ref, cwbc_ref, cbbc_ref, bcc)):
        width = buf.shape[1]
        buf[0:pad, :] = jnp.zeros((pad, width), F32)
        buf[pad + seq:2 * pad + seq, :] = jnp.zeros((pad, width), F32)
        buf[pad:pad + seq, :] = src[...].astype(F32)
        for blk in range(nc):
            acc = jnp.broadcast_to(cb[...], (L, width))
            for j in range(SSD_CONV):
                r0 = pad - half + j + blk * L
                acc = acc + cw[j:j + 1, :] * buf[r0:r0 + L, :]
            dst[blk * L:(blk + 1) * L, :] = _silu(acc)

    lane = lax.broadcasted_iota(jnp.int32, (seq, LANES), 1)
    dts = jnp.where(lane < 2 * SSD_HEADS, _softplus(dt_ref[...] + dtb_ref[...]), 0.0)
    dt_scr[...] = dts
    a_scr[...] = dts * (-jnp.exp(alog_ref[...]))

    if has_h0:
        hf[...] = h0f_ref[...]
        hb[...] = h0b_ref[...]
    else:
        hf[...] = jnp.zeros(hf.shape, F32)
        hb[...] = jnp.zeros(hb.shape, F32)

    row = lax.broadcasted_iota(jnp.int32, (L, L), 0)
    col = lax.broadcasted_iota(jnp.int32, (L, L), 1)
    lane_l = lax.broadcasted_iota(jnp.int32, (L, LANES), 1)
    lo_half = lane_l < SSD_P

    def chunk(c, fwd, h_scr):
        off = 0 if fwd else SSD_HEADS
        r0 = pl.multiple_of(c * L, L)
        a = a_scr[pl.ds(r0, L), :]
        dt = dt_scr[pl.ds(r0, L), :]
        cs = _cumsum_rows(a, L)
        total = cs[L - 1:L, :]
        if fwd:
            u = cs
            rvec = jnp.exp(cs)
            ed = jnp.exp(total - cs) * dt
            keep = col <= row
        else:
            ex = cs - a
            u = -ex
            rvec = jnp.exp(total - ex)
            ed = jnp.exp(ex) * dt
            keep = col >= row
        ut = jnp.transpose(u)
        dtt = jnp.transpose(dt)
        tcol = jnp.transpose(jnp.broadcast_to(total, (L, LANES)))[:, 0:1]
        rexp = _expand_heads(rvec, off)
        edexp = _expand_heads(ed, off)
        x = xc[pl.ds(r0, L), :]
        bc = bcc[pl.ds(r0, L), :]
        outs = []
        for g in range(SSD_GROUPS):
            bg = bc[:, g * SSD_N:(g + 1) * SSD_N]
            cg = bc[:, SSD_GROUPS * SSD_N + g * SSD_N:SSD_GROUPS * SSD_N + (g + 1) * SSD_N]
            cbm = _mm_nt(cg, bg)
            hg = h_scr[g * GROUP_W:(g + 1) * GROUP_W, :]
            xg = x[:, g * GROUP_W:(g + 1) * GROUP_W]
            y_off = _mm_nt(cg, hg) * rexp[:, g * GROUP_W:(g + 1) * GROUP_W]
            tiles = []
            for p in range(HEADS_PER_GROUP // 2):
                xt = xg[:, p * LANES:(p + 1) * LANES]
                acc = None
                for s in range(2):
                    h = off + g * HEADS_PER_GROUP + 2 * p + s
                    seg = u[:, h:h + 1] - ut[h:h + 1, :]
                    m = cbm * jnp.exp(jnp.where(keep, seg, -jnp.inf)) * dtt[h:h + 1, :]
                    xm = jnp.where(lo_half if s == 0 else jnp.logical_not(lo_half), xt, 0.0)
                    d = _mm(m, xm)
                    acc = d if acc is None else acc + d
                tiles.append(acc)
            outs.append(y_off + jnp.concatenate(tiles, axis=1))
            decs = []
            for hh in range(HEADS_PER_GROUP):
                h = off + g * HEADS_PER_GROUP + hh
                decs.append(jnp.broadcast_to(jnp.exp(tcol[h:h + 1, :]), (SSD_P, SSD_N)))
            dec = jnp.concatenate(decs, axis=0)
            h_scr[g * GROUP_W:(g + 1) * GROUP_W, :] = dec * hg + _mm_tn(xg * edexp[:, g * GROUP_W:(g + 1) * GROUP_W], bg)
        return r0, x, jnp.concatenate(outs, axis=1)

    def fwd_body(c, carry):
        r0, _, y = chunk(c, True, hf)
        yacc[pl.ds(r0, L), :] = y
        return carry

    lax.fori_loop(0, nc, fwd_body, 0)

    def bwd_body(i, carry):
        c = nc - 1 - i
        r0, x, y = chunk(c, False, hb)
        y = yacc[pl.ds(r0, L), :] + y + dsk_ref[...] * x
        y = y * _silu(z_ref[pl.ds(r0, L), :].astype(F32))
        y_ref[pl.ds(r0, L), :] = _rms(y, nw_ref[...]).astype(y_ref.dtype)
        return carry

    lax.fori_loop(0, nc, bwd_body, 0)
    sf_ref[...] = hf[...]
    sb_ref[...] = hb[...]


def _ssd(p0, p0dt, tok0, nseq, seq, cw, cb, dtb, alog, dskip, nw, h0f=None, h0b=None):
    has_h0 = h0f is not None
    b0 = tok0 // seq

    def cols(width, start):
        return pl.BlockSpec((seq, width), lambda s: (b0 + s, start // width))

    def full(shape):
        return pl.BlockSpec(shape, lambda s: (0,) * len(shape))

    in_specs = [cols(D_SSD, P0_Z), cols(D_SSD, P0_X), cols(512, P0_BC), cols(LANES, 0),
                full((SSD_CONV, D_SSD)), full((SSD_CONV, 512)), full((1, D_SSD)), full((1, 512)),
                full((1, LANES)), full((1, LANES)), full((1, D_SSD)), full((1, D_SSD))]
    args = [p0, p0, p0, p0dt, cw[:, :D_SSD], cw[:, D_SSD:], cb[:D_SSD].reshape(1, -1), cb[D_SSD:].reshape(1, -1),
            dtb, alog, dskip, nw.reshape(1, -1)]
    st_spec = pl.BlockSpec((None, D_SSD, SSD_N), lambda s: (s, 0, 0))
    if has_h0:
        in_specs += [st_spec, st_spec]
        args += [h0f, h0b]
    st_shape = jax.ShapeDtypeStruct((nseq, D_SSD, SSD_N), F32)
    return pl.pallas_call(
        functools.partial(_ssd_kernel, seq=seq, has_h0=has_h0),
        grid=(nseq,), in_specs=in_specs,
        out_specs=[pl.BlockSpec((seq, D_SSD), lambda s: (s, 0)), st_spec, st_spec],
        out_shape=[jax.ShapeDtypeStruct((nseq * seq, D_SSD), BF16), st_shape, st_shape],
        scratch_shapes=[pltpu.VMEM((seq + 2 * SUBLANES, D_SSD), F32), pltpu.VMEM((seq + 2 * SUBLANES, 512), F32),
                        pltpu.VMEM((seq, D_SSD), F32), pltpu.VMEM((seq, 512), F32),
                        pltpu.VMEM((seq, LANES), F32), pltpu.VMEM((seq, LANES), F32),
                        pltpu.VMEM((seq, D_SSD), F32),
                        pltpu.VMEM((D_SSD, SSD_N), F32), pltpu.VMEM((D_SSD, SSD_N), F32)],
        compiler_params=_cparams(1),
        name=f"ssd{seq}",
    )(*args)


def _place_halves(tile, kv_in_high):
    lo = lax.broadcasted_iota(jnp.int32, tile.shape, 1) < ATT_HD
    swapped = pltpu.roll(tile, ATT_HD, 1)
    if kv_in_high:
        return jnp.where(lo, swapped, 0.0), jnp.where(lo, 0.0, tile)
    return jnp.where(lo, tile, 0.0), jnp.where(lo, 0.0, swapped)


def _place_rows(vt, kv_in_high):
    head = vt[ATT_HD:, :] if kv_in_high else vt[:ATT_HD, :]
    z = jnp.zeros_like(head)
    return jnp.concatenate([head, z], axis=0), jnp.concatenate([z, head], axis=0)


LOG2E = 1.4426950408889634
SCORE_SCALE = ATT_SCALE * LOG2E


def _sink_attend_t(score_parts, value_parts, sink2):
    m = sink2
    for s in score_parts:
        m = jnp.maximum(m, jnp.max(s, axis=0, keepdims=True))
    denom = jnp.exp2(sink2 - m)
    out = None
    for s, v in zip(score_parts, value_parts):
        p = jnp.exp2(s - m)
        denom = denom + jnp.sum(p, axis=0, keepdims=True)
        o = _mm(v, p)
        out = o if out is None else out + o
    return out * (1.0 / denom)


def _attn_schedule(n, scores, attend):
    scores(0)
    for j in range(n):
        if j + 1 < n:
            scores(j + 1)
        attend(j)


def _ctx_attn_kernel(q_ref, k_ref, v_ref, sink_ref, o_ref, kt_ref, vt_ref, s_a, s_b):
    sink2 = sink_ref[...] * LOG2E
    bufs = (s_a, s_b)
    half = SEQ // 2
    for t in range(ATT_KV_DIM // LANES):
        cols = slice(t * LANES, (t + 1) * LANES)
        kt_ref[cols, :] = jnp.transpose(k_ref[:, cols].astype(F32))
        vt_ref[cols, :] = jnp.transpose(v_ref[:, cols].astype(F32))

    def kv_tile(ref, j):
        return ref[:, (j // 2) * LANES:(j // 2 + 1) * LANES].astype(F32), (j % 2 == 1)

    def scores(j):
        k_lo, k_hi = _place_halves(*kv_tile(k_ref, j))
        qst = jnp.concatenate([q_ref[:, qt * LANES:(qt + 1) * LANES] for qt in (2 * j, 2 * j + 1)], axis=0)
        bufs[j % 2][...] = _mm_nt(jnp.concatenate([k_lo, k_hi], axis=0), qst) * SCORE_SCALE

    def attend(j):
        src = bufs[j % 2]
        v, high = kv_tile(v_ref, j)
        vts = _place_rows(jnp.transpose(v), high)
        for ql in range(2):
            qt = 2 * j + ql
            for qh in range(2):
                cols = slice(ql * SEQ + qh * half, ql * SEQ + (qh + 1) * half)
                acc = None
                for s, vv in enumerate(vts):
                    o = _sink_attend_t([src[s * SEQ:(s + 1) * SEQ, cols]], [vv], sink2[:, 2 * qt + s:2 * qt + s + 1])
                    acc = o if acc is None else acc + o
                o_ref[qh * half:(qh + 1) * half, qt * LANES:(qt + 1) * LANES] = jnp.transpose(acc).astype(o_ref.dtype)

    _attn_schedule(ATT_KV, scores, attend)


def _ctx_attn(p0, sink):
    def cols(width, start):
        return pl.BlockSpec((SEQ, width), lambda b: (b, start // width))

    sbuf = pltpu.VMEM((2 * SEQ, 2 * SEQ), F32)
    return pl.pallas_call(
        _ctx_attn_kernel,
        grid=(BATCH,),
        in_specs=[cols(D, P0_Q), cols(ATT_KV_DIM, P0_K), cols(ATT_KV_DIM, P0_V),
                  pl.BlockSpec((1, LANES), lambda b: (0, 0))],
        out_specs=[pl.BlockSpec((SEQ, D), lambda b: (b, 0))]
        + [pl.BlockSpec((None, ATT_KV_DIM, SEQ), lambda b: (b, 0, 0))] * 2,
        out_shape=[jax.ShapeDtypeStruct((T_CTX, D), BF16)] + [jax.ShapeDtypeStruct((BATCH, ATT_KV_DIM, SEQ), F32)] * 2,
        scratch_shapes=[sbuf, sbuf],
        compiler_params=_cparams(1),
        name="ctx_attn",
    )(p0, p0, p0, sink)


def _rope_tables():
    quarter = ATT_HD // 4
    t = np.arange(DEC_SEQ)
    lane = np.arange(LANES)
    inv = ROPE_BASE ** (-(lane % quarter).astype(np.float64) / quarter)
    pos = np.where(((lane % ATT_HD) < ATT_HD // 2)[None, :], (t // GRID_W)[:, None], (t % GRID_W)[:, None])
    ang = pos * inv[None, :]
    first = ((lane % (2 * quarter)) < quarter)[None, :]
    cos, sin = np.cos(ang), np.sin(ang)
    return (jnp.asarray(cos, F32), jnp.asarray(np.where(first, -sin, 0.0), F32),
            jnp.asarray(np.where(first, 0.0, sin), F32))


def _rope(x, cos, sa, sb):
    quarter = ATT_HD // 4
    return x * cos + pltpu.roll(x, LANES - quarter, 1) * sa + pltpu.roll(x, quarter, 1) * sb


def _lat_attn_kernel(q_ref, kp_ref, kc_ref, kn_ref, vp_ref, vc_ref, vn_ref, ck_ref, cv_ref,
                     cos_ref, sa_ref, sb_ref, sink_ref, o_ref, c_a, c_b, w_a, w_b):
    blk = pl.program_id(1)
    nb = pl.num_programs(1)
    B = ATT_BLOCK
    sink2 = sink_ref[...] * LOG2E
    cbufs, wbufs = (c_a, c_b), (w_a, w_b)

    def tables(b):
        r0 = pl.multiple_of(b * B, B)
        return cos_ref[pl.ds(r0, B), :], sa_ref[pl.ds(r0, B), :], sb_ref[pl.ds(r0, B), :]

    tq = tables(blk)
    tk = [tables(jnp.maximum(blk - 1, 0)), tq, tables(jnp.minimum(blk + 1, nb - 1))]
    kabs = (blk - 1) * B + lax.broadcasted_iota(jnp.int32, (3 * B, B), 0)
    qpos = blk * B + lax.broadcasted_iota(jnp.int32, (3 * B, B), 1)
    ok = (jnp.abs(qpos - kabs) <= WINDOW) & (kabs >= 0) & (kabs < nb * B)
    ok = jnp.concatenate([ok, ok], axis=1)
    ok = jnp.concatenate([ok, ok], axis=0)

    def scores(j):
        high = (j % 2 == 1)
        sl = slice((j // 2) * LANES, (j // 2 + 1) * LANES)
        kw = jnp.concatenate([_rope(r[:, sl].astype(F32), *tb) for r, tb in zip((kp_ref, kc_ref, kn_ref), tk)], axis=0)
        qs = [q_ref[:, qt * LANES:(qt + 1) * LANES].astype(F32) for qt in (2 * j, 2 * j + 1)]
        q_plain = jnp.concatenate(qs, axis=0)
        q_rope = jnp.concatenate([_rope(q, *tq) for q in qs], axis=0)
        ck = jnp.transpose(ck_ref[sl, :])
        cbufs[j % 2][...] = _mm_nt(jnp.concatenate(_place_halves(ck, high), axis=0), q_plain) * SCORE_SCALE
        win = _mm_nt(jnp.concatenate(_place_halves(kw, high), axis=0), q_rope) * SCORE_SCALE
        wbufs[j % 2][...] = jnp.where(ok, win, -jnp.inf)

    def attend(j):
        high = (j % 2 == 1)
        sl = slice((j // 2) * LANES, (j // 2 + 1) * LANES)
        vw = jnp.concatenate([jnp.transpose(r[:, sl].astype(F32)) for r in (vp_ref, vc_ref, vn_ref)], axis=1)
        vts = _place_rows(vw, high)
        cvts = _place_rows(cv_ref[sl, :], high)
        csrc, wsrc = cbufs[j % 2], wbufs[j % 2]
        for ql in range(2):
            qt = 2 * j + ql
            cols = slice(ql * B, (ql + 1) * B)
            acc = None
            for s in range(2):
                parts = [csrc[s * PAST_LEN:(s + 1) * PAST_LEN, cols], wsrc[s * 3 * B:(s + 1) * 3 * B, cols]]
                o = _sink_attend_t(parts, [cvts[s], vts[s]], sink2[:, 2 * qt + s:2 * qt + s + 1])
                acc = o if acc is None else acc + o
            o_ref[:, qt * LANES:(qt + 1) * LANES] = jnp.transpose(acc).astype(o_ref.dtype)

    _attn_schedule(ATT_KV, scores, attend)


def _lat_attn(p0, ck, cv, sink):
    nb = DEC_SEQ // ATT_BLOCK
    base = T_CTX // ATT_BLOCK

    def kv(start, shift):
        return pl.BlockSpec((ATT_BLOCK, ATT_KV_DIM),
                            lambda b, i: (base + b * nb + jnp.clip(i + shift, 0, nb - 1), start // ATT_KV_DIM))

    def full(shape):
        return pl.BlockSpec(shape, lambda b, i: (0,) * len(shape))

    cache = pl.BlockSpec((None, ATT_KV_DIM, PAST_LEN), lambda b, i: (b, 0, 0))
    cos, sa, sb = _rope_tables()
    return pl.pallas_call(
        _lat_attn_kernel,
        grid=(DEC_BATCH, nb),
        in_specs=[pl.BlockSpec((ATT_BLOCK, D), lambda b, i: (base + b * nb + i, P0_Q // D)),
                  kv(P0_K, -1), kv(P0_K, 0), kv(P0_K, 1), kv(P0_V, -1), kv(P0_V, 0), kv(P0_V, 1),
                  cache, cache, full((DEC_SEQ, LANES)), full((DEC_SEQ, LANES)), full((DEC_SEQ, LANES)),
                  full((1, LANES))],
        out_specs=pl.BlockSpec((ATT_BLOCK, D), lambda b, i: (b * nb + i, 0)),
        out_shape=jax.ShapeDtypeStruct((T_LAT, D), BF16),
        scratch_shapes=[pltpu.VMEM((2 * PAST_LEN, 2 * ATT_BLOCK), F32)] * 2
        + [pltpu.VMEM((2 * 3 * ATT_BLOCK, 2 * ATT_BLOCK), F32)] * 2,
        compiler_params=_cparams(2),
        name="lat_attn",
    )(p0, p0, p0, p0, p0, p0, p0, ck, cv, cos, sa, sb, sink)


def _log_sigmoid(x):
    return jnp.minimum(x, 0.0) - jnp.log(1.0 + jnp.exp(-jnp.abs(x)))


def _gla_kernel(*refs, seq, has_s0):
    if has_s0:
        (q_ref, k_ref, v_ref, r_ref, lr_ref, w2f_ref, w2b_ref, bf_ref, bb_ref, nw_ref, s0f_ref, s0b_ref,
         y_ref, sf_ref, sb_ref, gf, gb, yf, yb, stf, stb) = refs
    else:
        (q_ref, k_ref, v_ref, r_ref, lr_ref, w2f_ref, w2b_ref, bf_ref, bb_ref, nw_ref,
         y_ref, sf_ref, sb_ref, gf, gb, yf, yb, stf, stb) = refs
    C = GLA_C
    nc = seq // C
    lr = lr_ref[...]
    gf[...] = _log_sigmoid(_mm(lr, w2f_ref[...]) + bf_ref[...]) / GLA_GATE_NORM
    gb[...] = _log_sigmoid(_mm(lr, w2b_ref[...]) + bb_ref[...]) / GLA_GATE_NORM
    for h in range(GLA_HEADS):
        rows = slice(h * GLA_DV, (h + 1) * GLA_DV)
        if has_s0:
            stf[rows, :] = jnp.transpose(s0f_ref[h * GLA_DK:(h + 1) * GLA_DK, :])
            stb[rows, :] = jnp.transpose(s0b_ref[h * GLA_DK:(h + 1) * GLA_DK, :])
        else:
            stf[rows, :] = jnp.zeros((GLA_DV, GLA_DK), F32)
            stb[rows, :] = jnp.zeros((GLA_DV, GLA_DK), F32)

    row = lax.broadcasted_iota(jnp.int32, (C, C), 0)
    col = lax.broadcasted_iota(jnp.int32, (C, C), 1)
    qscale = GLA_DK ** -0.5

    def chunk(c, fwd):
        g_scr, y_scr, st = (gf, yf, stf) if fwd else (gb, yb, stb)
        r0 = pl.multiple_of(c * C, C)
        g = g_scr[pl.ds(r0, C), :]
        cs = _cumsum_rows(g, C)
        total = cs[C - 1:C, :]
        q = q_ref[pl.ds(r0, C), :].astype(F32) * qscale
        k = k_ref[pl.ds(r0, C), :].astype(F32)
        v = v_ref[pl.ds(r0, C), :]
        if fwd:
            qs, ks, ke = q * jnp.exp(cs), k * jnp.exp(-cs), k * jnp.exp(total - cs)
            keep = col <= row
        else:
            ex = cs - g
            qs, ks, ke = q * jnp.exp(total - ex), k * jnp.exp(ex - total), k * jnp.exp(ex)
            keep = col >= row
        dec = jnp.exp(total)
        for h in range(GLA_HEADS):
            kc = slice(h * GLA_DK, (h + 1) * GLA_DK)
            vc = slice(h * GLA_DV, (h + 1) * GLA_DV)
            s_t = st[vc, :]
            att = jnp.where(keep, _mm_nt(qs[:, kc], ks[:, kc]), 0.0)
            y_scr[pl.ds(r0, C), vc] = _mm(att, v[:, vc]) + _mm_nt(qs[:, kc], s_t)
            st[vc, :] = dec[:, kc] * s_t + _mm_tn(v[:, vc], ke[:, kc])

    def body(i, carry):
        chunk(i, True)
        chunk(nc - 1 - i, False)
        return carry

    lax.fori_loop(0, nc, body, 0, unroll=4)

    nw = nw_ref[...]
    for blk in range(seq // LANES):
        rs = slice(blk * LANES, (blk + 1) * LANES)
        y = yf[rs, :] + yb[rs, :]
        gate = _silu(r_ref[rs, :].astype(F32))
        for h in range(GLA_HEADS):
            vc = slice(h * GLA_DV, (h + 1) * GLA_DV)
            y_ref[rs, vc] = (_rms(y[:, vc], nw) * gate[:, vc]).astype(y_ref.dtype)
    for h in range(GLA_HEADS):
        rows = slice(h * GLA_DV, (h + 1) * GLA_DV)
        sf_ref[h * GLA_DK:(h + 1) * GLA_DK, :] = jnp.transpose(stf[rows, :])
        sb_ref[h * GLA_DK:(h + 1) * GLA_DK, :] = jnp.transpose(stb[rows, :])


def _gla(p1, p1lr, tok0, nseq, seq, w2f, w2b, bgf, bgb, nw, s0f=None, s0b=None):
    has_s0 = s0f is not None
    b0 = tok0 // seq
    dk_all, dv_all = GLA_HEADS * GLA_DK, GLA_HEADS * GLA_DV

    def cols(width, start):
        return pl.BlockSpec((seq, width), lambda s: (b0 + s, start // width))

    def full(shape):
        return pl.BlockSpec(shape, lambda s: (0,) * len(shape))

    in_specs = [cols(dk_all, P1_Q), cols(dk_all, P1_K), cols(dv_all, P1_V), cols(dv_all, P1_R), cols(LANES, 0),
                full((LANES, dk_all)), full((LANES, dk_all)), full((1, dk_all)), full((1, dk_all)), full((1, GLA_DV))]
    args = [p1, p1, p1, p1, p1lr, w2f, w2b, bgf.reshape(1, -1), bgb.reshape(1, -1), nw.reshape(1, -1)]
    st_spec = pl.BlockSpec((None, dk_all, GLA_DV), lambda s: (s, 0, 0))
    if has_s0:
        in_specs += [st_spec, st_spec]
        args += [s0f, s0b]
    st_shape = jax.ShapeDtypeStruct((nseq, dk_all, GLA_DV), F32)
    return pl.pallas_call(
        functools.partial(_gla_kernel, seq=seq, has_s0=has_s0),
        grid=(nseq,), in_specs=in_specs,
        out_specs=[pl.BlockSpec((seq, dv_all), lambda s: (s, 0)), st_spec, st_spec],
        out_shape=[jax.ShapeDtypeStruct((nseq * seq, dv_all), BF16), st_shape, st_shape],
        scratch_shapes=[pltpu.VMEM((seq, dk_all), F32), pltpu.VMEM((seq, dk_all), F32),
                        pltpu.VMEM((seq, dv_all), F32), pltpu.VMEM((seq, dv_all), F32),
                        pltpu.VMEM((dv_all, GLA_DK), F32), pltpu.VMEM((dv_all, GLA_DK), F32)],
        compiler_params=_cparams(1),
        name=f"gla{seq}",
    )(*args)


ROUTE_TM = 512
ROUTE_SUB = 256
ROUTE_ROWS = 32


def _split_bf16(x):
    hi = x.astype(BF16)
    return hi, (x - hi.astype(F32)).astype(BF16)


def _outproj_kernel(*refs, n_in, dual_x):
    y_refs = refs[:2 * n_in]
    n_x = 2 if dual_x else 1
    x_refs = refs[2 * n_in + 1:2 * n_in + 1 + n_x]
    w_ref = refs[2 * n_in]
    (g1_ref, sh_ref, sc_ref, nw_ref, wr_ref, br_ref,
     xo_ref, tr_ref, route_ref, cnt_ref, w_scr, wr_hl, carry, o_scr) = refs[2 * n_in + 1 + n_x:]
    is_ctx = pl.program_id(0) < T_CTX // ROUTE_TM

    @pl.when(pl.program_id(0) == 0)
    def _():
        w_scr[...] = w_ref[...].astype(BF16)
        hi, lo = _split_bf16(jnp.transpose(wr_ref[...]))
        wr_hl[0:LANES, :] = hi
        wr_hl[LANES:2 * LANES, :] = lo
        carry[...] = jnp.zeros(carry.shape, F32)

    o = None
    for i in range(n_in):
        y = jnp.where(is_ctx, y_refs[2 * i][...], y_refs[2 * i + 1][...])
        d = jnp.dot(y, w_scr[i * D:(i + 1) * D, :], preferred_element_type=F32)
        o = d if o is None else o + d
    o_scr[...] = o
    for sub in range(ROUTE_TM // ROUTE_SUB):
        _outproj_subtile(sub, is_ctx, o_scr, x_refs, dual_x, g1_ref, sh_ref, sc_ref, nw_ref, br_ref,
                         xo_ref, tr_ref, route_ref, cnt_ref, wr_hl, carry)


def _outproj_subtile(sub, is_ctx, o_scr, x_refs, dual_x, g1_ref, sh_ref, sc_ref, nw_ref, br_ref,
                     xo_ref, tr_ref, route_ref, cnt_ref, wr_hl, carry):
    tm = ROUTE_SUB
    rows = slice(sub * tm, (sub + 1) * tm)
    x_in = jnp.where(is_ctx, x_refs[0][rows, :], x_refs[1][rows, :]) if dual_x else x_refs[0][rows, :]
    x = x_in + g1_ref[0] * o_scr[rows, :]
    xo_ref[rows, :] = x
    t = _rms(x, nw_ref[...]) * (1.0 + sc_ref[0]) + sh_ref[0]
    for k in range(D // LANES):
        tr_ref[pl.ds(sub * tm * SUBLANES + k, tm, stride=SUBLANES), :] = t[:, k * LANES:(k + 1) * LANES]

    t_hi, t_lo = _split_bf16(t)
    lg = _mm_nt(wr_hl[...], t_hi)
    nr = ROUTE_ROWS
    logit = lg[0:nr, :] + lg[LANES:LANES + nr, :] + _mm_nt(wr_hl[0:LANES, :], t_lo)[0:nr, :] + br_ref[0:nr, :]
    rowf = lax.broadcasted_iota(jnp.int32, (nr, tm), 0).astype(F32)
    neg = -jnp.inf

    def first_argmax(v, vmax):
        return jnp.min(jnp.where(v == vmax, rowf, float(LANES)), axis=0, keepdims=True)

    gl = jnp.where(rowf < N_GROUPS, logit, neg)
    gmax = jnp.max(gl, axis=0, keepdims=True)
    gsel = first_argmax(gl, gmax)
    gprob = 1.0 / jnp.sum(jnp.exp(gl - gmax), axis=0, keepdims=True)
    first = N_GROUPS + EXP_PER_GROUP * gsel
    el = jnp.where((rowf >= first) & (rowf < first + EXP_PER_GROUP), logit, neg)
    m1 = jnp.max(el, axis=0, keepdims=True)
    i1 = first_argmax(el, m1)
    el2 = jnp.where(rowf == i1, neg, el)
    m2 = jnp.max(el2, axis=0, keepdims=True)
    i2 = first_argmax(el2, m2)
    e2 = jnp.exp(m2 - m1)
    c1 = gprob / (1.0 + e2)
    c2 = gprob * e2 / (1.0 + e2)
    x1 = i1 - N_GROUPS
    x2 = i2 - N_GROUPS

    erow = rowf
    hot = ((erow == x1) | (erow == x2)).astype(F32)
    tri = (lax.broadcasted_iota(jnp.int32, (tm, tm), 0) < lax.broadcasted_iota(jnp.int32, (tm, tm), 1))
    before = _mm(hot, tri.astype(F32)) + carry[...]
    r1 = jnp.sum(jnp.where(erow == x1, before, 0.0), axis=0, keepdims=True)
    r2 = jnp.sum(jnp.where(erow == x2, before, 0.0), axis=0, keepdims=True)
    total = carry[...] + _mm(hot, jnp.ones((tm, tm), F32))
    carry[...] = total
    cnt_ref[...] = total[0:N_EXPERTS, 0:LANES]
    row8 = lax.broadcasted_iota(jnp.int32, (SUBLANES, tm), 0)
    out = jnp.zeros((SUBLANES, tm), F32)
    for k, v in enumerate((x1, x2, c1, c2, r1, r2)):
        out = jnp.where(row8 == k, jnp.broadcast_to(v, (SUBLANES, tm)), out)
    route_ref[:, rows] = out


def _outproj_route(ys, w_out, xs, mods, layer, norm_w, w_router, b_router):
    tm = ROUTE_TM
    n_in = len(ys) // 2
    dual_x = len(xs) == 2
    kdim = w_out.shape[0]

    def full(shape):
        return pl.BlockSpec(shape, lambda i: (0,) * len(shape))

    tile = pl.BlockSpec((tm, D), lambda i: (i, 0))
    pair = list(_ctx_lat_specs(tm))
    in_specs = (pair * n_in + [full((kdim, D))] + (pair if dual_x else [tile])
                + [_mod_spec(layer, 2, tm), _mod_spec(layer, 3, tm), _mod_spec(layer, 4, tm),
                   full((1, D)), full((D, LANES)), full((LANES, ROUTE_SUB))])
    cnt = jax.ShapeDtypeStruct((N_EXPERTS, LANES), F32)
    return pl.pallas_call(
        functools.partial(_outproj_kernel, n_in=n_in, dual_x=dual_x),
        grid=(T // tm,), in_specs=in_specs,
        out_specs=[tile, pl.BlockSpec((tm * SUBLANES, LANES), lambda i: (i, 0)),
                   pl.BlockSpec((SUBLANES, tm), lambda i: (0, i)), full(cnt.shape)],
        out_shape=[jax.ShapeDtypeStruct((T, D), F32), jax.ShapeDtypeStruct((T * SUBLANES, LANES), F32),
                   jax.ShapeDtypeStruct((SUBLANES, T), F32), cnt],
        scratch_shapes=[pltpu.VMEM((kdim, D), BF16), pltpu.VMEM((2 * LANES, D), BF16),
                        pltpu.VMEM((ROUTE_ROWS, ROUTE_SUB), F32), pltpu.VMEM((tm, D), F32)],
        compiler_params=_cparams(1),
        name=f"outproj{layer}",
    )(*ys, w_out, *xs, mods, mods, mods, norm_w.reshape(1, D), w_router,
      jnp.broadcast_to(b_router.reshape(LANES, 1), (LANES, ROUTE_SUB)))


def _moe_meta(counts):
    tm = MOE_TM
    experts = jnp.arange(N_EXPERTS, dtype=jnp.int32)
    counts = jnp.max(counts, axis=1).astype(jnp.int32)
    padded = ((counts + tm - 1) // tm) * tm
    ends = jnp.cumsum(padded)
    tile_start = jnp.arange(MOE_TILES, dtype=jnp.int32) * tm
    te = jnp.sum((tile_start[:, None] >= ends[None, :]).astype(jnp.int32), axis=1)
    last = jnp.max(jnp.where(counts > 0, experts, 0))
    meta = jnp.concatenate([jnp.minimum(te, last), ends[-1:] // tm]).astype(jnp.int32)
    starts = ends - padded
    pads = jnp.concatenate([starts + counts, ends[-1:], ends, jnp.full((1,), MOE_ROWS)]).astype(jnp.int32)
    return starts.astype(jnp.int32), pads, meta


def _expert_changed(meta_ref, j):
    return (j == 0) | (meta_ref[j] != meta_ref[jnp.maximum(j - 1, 0)])


def _moe_up_kernel(pos1_ref, pos2_ref, pads_ref, meta_ref, tr_hbm, wg_ref, wu_ref, a_ref, rowmap_ref,
                   tr_scr, g0, g1, wg_bf, wu_bf, sem):
    j = pl.program_id(0)
    tm = MOE_TM
    ntiles = meta_ref[MOE_TILES]

    def gather(tile, dst):
        for mi in range(tm):
            tok = jnp.minimum(rowmap_ref[tile * tm + mi] >> 1, T - 1)
            dst[mi * SUBLANES:(mi + 1) * SUBLANES, :] = tr_scr[pl.ds(pl.multiple_of(tok * SUBLANES, SUBLANES), SUBLANES), :]

    @pl.when(j == 0)
    def _():
        load = pltpu.make_async_copy(tr_hbm, tr_scr, sem)
        load.start()

        def clear(c, carry):
            for i in range(SUBLANES):
                rowmap_ref[c * SUBLANES + i] = 2 * T
            return carry
        for k in range(N_EXPERTS + 1):
            lax.fori_loop(pads_ref[k] // SUBLANES, pads_ref[N_EXPERTS + 1 + k] // SUBLANES, clear, 0)

        def place(t, carry):
            rowmap_ref[pos1_ref[t]] = 2 * t
            rowmap_ref[pos2_ref[t]] = 2 * t + 1
            return carry
        lax.fori_loop(0, T, place, 0, unroll=8)
        load.wait()
        gather(0, g0)

    def compute(cur, nxt):
        gather(jnp.minimum(j + 1, ntiles - 1), nxt)
        x = _tokmajor_to_std(cur, tm).astype(BF16)
        g = jnp.dot(x, wg_bf[...], preferred_element_type=F32)
        u = jnp.dot(x, wu_bf[...], preferred_element_type=F32)
        a_ref[...] = (_silu(g) * u).astype(a_ref.dtype)

    @pl.when(j < ntiles)
    def _():
        @pl.when(_expert_changed(meta_ref, j))
        def _():
            wg_bf[...] = wg_ref[...].astype(BF16)
            wu_bf[...] = wu_ref[...].astype(BF16)

        pl.when(j % 2 == 0)(functools.partial(compute, g0, g1))
        pl.when(j % 2 == 1)(functools.partial(compute, g1, g0))

    @pl.when(j >= ntiles)
    def _():
        a_ref[...] = jnp.zeros(a_ref.shape, a_ref.dtype)


def _moe_down_kernel(rowmap_ref, cpair_ref, meta_ref, a_ref, wd_ref, x_hbm, mods_ref, fw_ref, *rest, layer, final):
    n_out = 2 if final else 1
    outs = rest[:n_out]
    acc, y0, y1, wd_bf, xin, xout, io_sems = rest[n_out:]
    j = pl.program_id(0)
    tm = MOE_TM
    zrows = 512
    ntiles = meta_ref[MOE_TILES]

    @pl.when(j == 0)
    def _():
        def zero(i, carry):
            acc[pl.ds(pl.multiple_of(i * zrows, zrows), zrows), :] = jnp.zeros((zrows, LANES), F32)
            return carry
        lax.fori_loop(0, ACC_ROWS // zrows, zero, 0)

    def matmul(dst):
        y = jnp.dot(a_ref[...], wd_bf[...], preferred_element_type=F32)
        for k in range(D // LANES):
            dst[pl.ds(k, tm, stride=SUBLANES), :] = y[:, k * LANES:(k + 1) * LANES]

    def scatter(tile, src):
        for b in range(tm // SUBLANES):
            ents = [rowmap_ref[tile * tm + b * SUBLANES + i] for i in range(SUBLANES)]
            offs = [pl.multiple_of((e >> 1) * SUBLANES, SUBLANES) for e in ents]
            olds = [acc[pl.ds(o, SUBLANES), :] for o in offs]
            for i, o in enumerate(offs):
                r = (b * SUBLANES + i) * SUBLANES
                acc[pl.ds(o, SUBLANES), :] = olds[i] + cpair_ref[ents[i]] * src[r:r + SUBLANES, :]

    has_mm = j < ntiles
    has_sc = (j >= 1) & (j <= ntiles)

    @pl.when(has_mm)
    def _():
        @pl.when(_expert_changed(meta_ref, j))
        def _():
            wd_bf[...] = wd_ref[...].astype(BF16)

    for par, (cur, prev) in enumerate(((y0, y1), (y1, y0))):
        mine = (j % 2) == par

        @pl.when(mine & has_mm & has_sc)
        def _():
            matmul(cur)
            scatter(j - 1, prev)

        @pl.when(mine & has_mm & jnp.logical_not(has_sc))
        def _():
            matmul(cur)

        @pl.when(mine & jnp.logical_not(has_mm) & has_sc)
        def _():
            scatter(j - 1, prev)

    @pl.when(j == pl.num_programs(0) - 1)
    def _():
        ft = MERGE_TM
        n_t = T // ft

        def fetch(i):
            return pltpu.make_async_copy(x_hbm.at[i * ft:(i + 1) * ft, :], xin.at[i % 2], io_sems.at[i % 2])

        def flush(i):
            if final:
                n_c = T_CTX // ft
                dst = outs[0].at[i * ft:(i + 1) * ft, :] if i < n_c else outs[1].at[(i - n_c) * ft:(i - n_c + 1) * ft, :]
            else:
                dst = outs[0].at[i * ft:(i + 1) * ft, :]
            return pltpu.make_async_copy(xout.at[i % 2], dst, io_sems.at[2 + i % 2])

        fetch(0).start()
        for i in range(n_t):
            if i + 1 < n_t:
                fetch(i + 1).start()
            fetch(i).wait()
            moe = jnp.concatenate([acc[pl.ds(i * ft * SUBLANES + k, ft, stride=SUBLANES), :]
                                   for k in range(D // LANES)], axis=1)
            row = 0 if i * ft < T_CTX else 1 + (i * ft - T_CTX) // DEC_SEQ
            x = xin[i % 2] + mods_ref[(layer * SUBLANES + row) * 6 + 5] * moe
            if i >= 2:
                flush(i - 2).wait()
            xout[i % 2] = _rms(x, fw_ref[...]) if final else x
            flush(i).start()
        for i in range(max(n_t - 2, 0), n_t):
            flush(i).wait()


def _moe(tr, route_t, counts, layer, w_gate, w_up, w_down, xmid, mods, final_w=None):
    tm = MOE_TM
    starts, pads, meta = _moe_meta(counts)
    experts = jnp.arange(N_EXPERTS, dtype=jnp.int32)

    def position(e, r):
        sel = e.astype(jnp.int32)[:, None] == experts[None, :]
        return jnp.sum(jnp.where(sel, starts[None, :], 0), axis=1) + r.astype(jnp.int32)

    pos1 = position(route_t[0], route_t[4])
    pos2 = position(route_t[1], route_t[5])
    cpair = jnp.concatenate([jnp.stack([route_t[2], route_t[3]], axis=1).reshape(2 * T), jnp.zeros((2,), F32)])

    def wspec(shape, n):
        return pl.BlockSpec((None, None) + shape,
                            lambda j, *pre: (layer, pre[n - 1][jnp.minimum(j, MOE_TILES - 1)], 0, 0))

    gscr = pltpu.VMEM((tm * SUBLANES, LANES), F32)
    act, rowmap = pl.pallas_call(
        _moe_up_kernel,
        grid_spec=pltpu.PrefetchScalarGridSpec(
            num_scalar_prefetch=4, grid=(MOE_TILES,),
            in_specs=[pl.BlockSpec(memory_space=pl.ANY), wspec((D, D_EXPERT), 4), wspec((D, D_EXPERT), 4)],
            out_specs=[pl.BlockSpec((tm, D_EXPERT), lambda j, *pre: (j, 0)), pl.BlockSpec(memory_space=pltpu.SMEM)],
            scratch_shapes=[pltpu.VMEM((T * SUBLANES, LANES), F32), gscr, gscr,
                            pltpu.VMEM((D, D_EXPERT), BF16), pltpu.VMEM((D, D_EXPERT), BF16),
                            pltpu.SemaphoreType.DMA(())]),
        out_shape=[jax.ShapeDtypeStruct((MOE_ROWS, D_EXPERT), BF16), jax.ShapeDtypeStruct((MOE_ROWS,), jnp.int32)],
        compiler_params=_cparams(1),
        name=f"moe_up{layer}",
    )(pos1, pos2, pads, meta, tr, w_gate, w_up)

    final = final_w is not None
    any_spec = pl.BlockSpec(memory_space=pl.ANY)
    fw = (final_w if final else jnp.ones((D,), F32)).reshape(1, D)
    stage = pltpu.VMEM((2, MERGE_TM, D), F32)
    if final:
        out_shape = [jax.ShapeDtypeStruct((T_CTX, D), F32), jax.ShapeDtypeStruct((T_LAT, D), F32)]
    else:
        out_shape = [jax.ShapeDtypeStruct((T, D), F32)]
    return pl.pallas_call(
        functools.partial(_moe_down_kernel, layer=layer, final=final),
        grid_spec=pltpu.PrefetchScalarGridSpec(
            num_scalar_prefetch=3, grid=(MOE_TILES + 1,),
            in_specs=[pl.BlockSpec((tm, D_EXPERT), lambda j, *pre: (jnp.minimum(j, MOE_TILES - 1), 0)),
                      wspec((D_EXPERT, D), 3), any_spec,
                      pl.BlockSpec(mods.shape, lambda j, *pre: (0, 0, 0)), pl.BlockSpec((1, D), lambda j, *pre: (0, 0))],
            out_specs=[any_spec] * len(out_shape),
            scratch_shapes=[pltpu.VMEM((ACC_ROWS, LANES), F32), gscr, gscr, pltpu.VMEM((D_EXPERT, D), BF16),
                            stage, stage, pltpu.SemaphoreType.DMA((4,))]),
        out_shape=out_shape,
        compiler_params=_cparams(1),
        name=f"moe_down{layer}",
    )(rowmap, cpair, meta, act, w_down, xmid, mods, fw)


def _pad_lanes(v):
    return jnp.pad(v.astype(F32), (0, LANES - v.shape[0])).reshape(1, LANES)


def kernel(x_prompt, x_sample, cache_k_attn, cache_v_attn, state_ssd_fwd, state_ssd_bwd, state_gla_fwd, state_gla_bwd, c, c_ctx, w_ada, b_ada, norm_mix_w, norm_ffn_w, w_in_even, conv_w, conv_b, dt_bias_fwd, dt_bias_bwd, a_log_fwd, a_log_bwd, d_skip, ssd_norm_w, attn_sink, w_out_even, w_in_odd, w_gk2_fwd, b_gk_fwd, w_gk2_bwd, b_gk_bwd, gla_norm_w, w_out_odd, w_router_group, b_router_group, w_router_expert, b_router_expert, w_gate_exp, w_up_exp, w_down_exp, final_norm_w):
    depth = w_ada.shape[0]
    assert depth == 2 and x_prompt.shape == (BATCH, SEQ, D) and x_sample.shape == (DEC_BATCH, DEC_SEQ, D)

    cond8 = jnp.concatenate([c_ctx[None, :], c, jnp.zeros((SUBLANES - 1 - DEC_BATCH, D), F32)], axis=0)
    mods = _adaln(cond8, w_ada, b_ada).reshape(depth * SUBLANES * 6, 1, D)
    xs0 = (x_prompt.reshape(T_CTX, D), x_sample.reshape(T_LAT, D))

    def router_params(i):
        wr = jnp.concatenate([w_router_group[i], w_router_expert[i],
                              jnp.zeros((D, LANES - N_GROUPS - N_EXPERTS), F32)], axis=1)
        return wr, _pad_lanes(jnp.concatenate([b_router_group[i], b_router_expert[i]]))

    n_zxbc = 2 * D_SSD + 2 * SSD_GROUPS * SSD_N
    n_dt = 2 * SSD_HEADS
    half = PROJ_TN // 2
    r_bc, r_q = 2 * D_SSD, n_zxbc + n_dt
    r_kv = r_q + D
    rows0 = [(0, half), (D_SSD, D_SSD + half), (r_q, r_q + half), (r_bc, r_kv)]
    p0, p0dt = _modproj(xs0, mods, 0, norm_mix_w[0], jnp.transpose(w_in_even[0]), rows0, n_zxbc)

    dtb = _pad_lanes(jnp.concatenate([dt_bias_fwd[0], dt_bias_bwd[0]]))
    alog = _pad_lanes(jnp.concatenate([a_log_fwd[0], a_log_bwd[0]]))
    dskip = jnp.repeat(d_skip[0], SSD_P).reshape(1, D_SSD)
    ssd_args = (conv_w[0], conv_b[0], dtb, alog, dskip, ssd_norm_w[0])
    y_ssd_c, ssd_f, ssd_b = _ssd(p0, p0dt, 0, BATCH, SEQ, *ssd_args)
    y_ssd_l, _, _ = _ssd(p0, p0dt, T_CTX, DEC_BATCH, DEC_SEQ, *ssd_args,
                         h0f=state_ssd_fwd[:, 0].reshape(DEC_BATCH, D_SSD, SSD_N),
                         h0b=state_ssd_bwd[:, 0].reshape(DEC_BATCH, D_SSD, SSD_N))
    sink = _pad_lanes(attn_sink[0])
    y_att_c, new_kt, new_vt = _ctx_attn(p0, sink)
    def cache_in(t):
        return jnp.transpose(t[:, 0], (0, 2, 3, 1)).reshape(DEC_BATCH, ATT_KV_DIM, PAST_LEN)

    y_att_l = _lat_attn(p0, cache_in(cache_k_attn), cache_in(cache_v_attn), sink)
    xmid0, tr0, route0, cnt0 = _outproj_route([y_ssd_c, y_ssd_l, y_att_c, y_att_l], w_out_even[0], xs0, mods, 0,
                                              norm_ffn_w[0], *router_params(0))
    (x1,) = _moe(tr0, route0, cnt0, 0, w_gate_exp, w_up_exp, w_down_exp, xmid0, mods)

    dk_all = GLA_HEADS * GLA_DK
    n_qkvr = 2 * dk_all + 2 * GLA_HEADS * GLA_DV
    n_odd = w_in_odd.shape[2]
    p1, p1lr = _modproj(x1, mods, 1, norm_mix_w[1], jnp.transpose(w_in_odd[0]),
                        [(k * PROJ_TN, k * PROJ_TN + half) for k in range(n_qkvr // PROJ_TN)], n_odd - LANES)
    lr0 = LANES - 2 * GLA_LOWRANK
    w2f = jnp.zeros((LANES, dk_all), F32).at[lr0:lr0 + GLA_LOWRANK].set(w_gk2_fwd[0])
    w2b = jnp.zeros((LANES, dk_all), F32).at[lr0 + GLA_LOWRANK:].set(w_gk2_bwd[0])
    gla_args = (w2f, w2b, b_gk_fwd[0], b_gk_bwd[0], gla_norm_w[0])
    y_gla_c, gla_f, gla_b = _gla(p1, p1lr, 0, BATCH, SEQ, *gla_args)
    y_gla_l, _, _ = _gla(p1, p1lr, T_CTX, DEC_BATCH, DEC_SEQ, *gla_args,
                         s0f=state_gla_fwd[:, 0].reshape(DEC_BATCH, dk_all, GLA_DV),
                         s0b=state_gla_bwd[:, 0].reshape(DEC_BATCH, dk_all, GLA_DV))
    xmid1, tr1, route1, cnt1 = _outproj_route([y_gla_c, y_gla_l], w_out_odd[0], (x1,), mods, 1,
                                              norm_ffn_w[1], *router_params(1))
    y_c, y_l = _moe(tr1, route1, cnt1, 1, w_gate_exp, w_up_exp, w_down_exp, xmid1, mods, final_w=final_norm_w)

    y_prompt = y_c.reshape(BATCH, SEQ, D)
    y_sample = y_l.reshape(DEC_BATCH, DEC_SEQ, D)
    def cache_out(t):
        return jnp.transpose(t.reshape(BATCH, 1, ATT_KV, ATT_HD, SEQ), (0, 1, 4, 2, 3))

    new_k, new_v = cache_out(new_kt), cache_out(new_vt)
    return (y_prompt, y_sample, new_k, new_v,
            ssd_f.reshape(BATCH, 1, SSD_HEADS, SSD_P, SSD_N), ssd_b.reshape(BATCH, 1, SSD_HEADS, SSD_P, SSD_N),
            gla_f.reshape(BATCH, 1, GLA_HEADS, GLA_DK, GLA_DV), gla_b.reshape(BATCH, 1, GLA_HEADS, GLA_DK, GLA_DV))
```

```python
import functools
import math

import numpy as np
import jax
import jax.numpy as jnp
from jax import lax
from jax.experimental import pallas as pl
from jax.experimental.pallas import tpu as pltpu

F32 = jnp.float32
BF16 = jnp.bfloat16

D = 1024
BATCH, SEQ = 16, 256
DEC_BATCH, DEC_SEQ = 2, 1024
PAST_LEN = 512
GRID_W = 64
EPS = 1e-6
T_CTX = BATCH * SEQ
T_LAT = DEC_BATCH * DEC_SEQ
T = T_CTX + T_LAT

SSD_HEADS, SSD_P, SSD_N, SSD_GROUPS = 16, 64, 128, 2
SSD_CONV = 5
SSD_L = 128
D_SSD = SSD_HEADS * SSD_P
HEADS_PER_GROUP = SSD_HEADS // SSD_GROUPS
GROUP_W = HEADS_PER_GROUP * SSD_P

ATT_HEADS, ATT_KV, ATT_HD = 16, 4, 64
ATT_KV_DIM = ATT_KV * ATT_HD
WINDOW = 128
ATT_BLOCK = 128
ATT_SCALE = ATT_HD ** -0.5
ROPE_BASE = 10000.0

GLA_HEADS, GLA_DK, GLA_DV = 4, 128, 256
GLA_C = 64
GLA_GATE_NORM = 16.0
GLA_LOWRANK = 16

N_GROUPS, EXP_PER_GROUP = 4, 4
N_EXPERTS = 16
D_EXPERT = 512

LANES = 128
SUBLANES = 8
VMEM_LIMIT = 56 * 1024 * 1024

P0_Z, P0_X, P0_Q, P0_BC, P0_K, P0_V = 0, 1024, 2048, 3072, 3584, 3840
P0_W = 4096
P1_Q, P1_K, P1_V, P1_R = 0, 512, 1024, 2048
P1_W = 3072

MOE_TM = 512
MOE_TILES = (2 * T) // MOE_TM + N_EXPERTS
MOE_ROWS = MOE_TILES * MOE_TM
ACC_ROWS = (T + SUBLANES) * SUBLANES
MERGE_TM = 512


def _cparams(n_axes, vmem=VMEM_LIMIT):
    return pltpu.CompilerParams(dimension_semantics=("arbitrary",) * n_axes, vmem_limit_bytes=vmem)


def _silu(x):
    return x / (1.0 + jnp.exp(-x))


def _softplus(x):
    return jnp.maximum(x, 0.0) + jnp.log(1.0 + jnp.exp(-jnp.abs(x)))


def _mm(a, b):
    return jnp.dot(a.astype(BF16), b.astype(BF16), preferred_element_type=F32)


def _mm_nt(a, b):
    return lax.dot_general(a.astype(BF16), b.astype(BF16), (((1,), (1,)), ((), ())),
                           preferred_element_type=F32)


def _mm_tn(a, b):
    return lax.dot_general(a.astype(BF16), b.astype(BF16), (((0,), (0,)), ((), ())),
                           preferred_element_type=F32)


def _rms(x, w):
    return x * lax.rsqrt(jnp.mean(x * x, axis=-1, keepdims=True) + EPS) * w


def _cumsum_rows(x, n):
    row = lax.broadcasted_iota(jnp.int32, x.shape, 0)
    s = 1
    while s < n:
        x = x + jnp.where(row >= s, pltpu.roll(x, s, 0), 0.0)
        s *= 2
    return x


def _mod_row(tok0):
    return jnp.where(tok0 < T_CTX, 0, 1 + (tok0 - T_CTX) // DEC_SEQ)


ADA_TN = 1536


def _adaln_kernel(c_ref, w_ref, b_ref, o_ref):
    s = _silu(c_ref[...])
    o_ref[0] = _mm(s, w_ref[0]) + b_ref[0]


def _adaln(cond8, w_ada, b_ada):
    depth = w_ada.shape[0]
    return pl.pallas_call(
        _adaln_kernel,
        grid=(depth, 6 * D // ADA_TN),
        in_specs=[
            pl.BlockSpec((SUBLANES, D), lambda l, j: (0, 0)),
            pl.BlockSpec((1, D, ADA_TN), lambda l, j: (l, 0, j)),
            pl.BlockSpec((1, 1, ADA_TN), lambda l, j: (l, 0, j)),
        ],
        out_specs=pl.BlockSpec((1, SUBLANES, ADA_TN), lambda l, j: (l, 0, j)),
        out_shape=jax.ShapeDtypeStruct((depth, SUBLANES, 6 * D), F32),
        compiler_params=_cparams(2),
        name="adaln",
    )(cond8, w_ada, b_ada.reshape(depth, 1, 6 * D))


def _mod_spec(layer, chunk, tm, tile_of=lambda i, *_: i):
    return pl.BlockSpec((1, 1, D), lambda *g: ((layer * SUBLANES + _mod_row(tile_of(*g) * tm)) * 6 + chunk, 0, 0))


def _tokmajor_to_std(ref, tm):
    return jnp.concatenate([ref[pl.ds(k, tm, stride=SUBLANES), :] for k in range(D // LANES)], axis=1)


PROJ_TM = 1024
PROJ_TN = 1024


def _ctx_lat_specs(tm, width=D):
    n_ctx = T_CTX // tm
    return (pl.BlockSpec((tm, width), lambda i, *_: (jnp.minimum(i, n_ctx - 1), 0)),
            pl.BlockSpec((tm, width), lambda i, *_: (jnp.maximum(i - n_ctx, 0), 0)))


def _modproj_kernel(*refs, dual_x):
    it = iter(refs)
    if dual_x:
        xc_ref, xl_ref = next(it), next(it)
    else:
        x_ref, moe_ref, g2_ref = next(it), next(it), next(it)
    sh_ref, sc_ref, nw_ref, wlo_ref, whi_ref, ws_ref, o_ref, os_ref = (next(it) for _ in range(8))
    xo_ref = None if dual_x else next(it)
    h_all, w_bf = next(it), next(it)
    j, i = pl.program_id(0), pl.program_id(1)
    tm = PROJ_TM
    rows = pl.ds(pl.multiple_of(i * tm, tm), tm)

    @pl.when(j == 0)
    def _():
        if dual_x:
            x = jnp.where(i < T_CTX // tm, xc_ref[...], xl_ref[...])
        else:
            x = x_ref[...] + g2_ref[0] * _tokmajor_to_std(moe_ref, tm)
            xo_ref[...] = x
        h = (_rms(x, nw_ref[...]) * (1.0 + sc_ref[0]) + sh_ref[0]).astype(BF16)
        h_all[rows, :] = h
        os_ref[...] = _mm_nt(h, ws_ref[...])

    @pl.when(i == 0)
    def _():
        half = wlo_ref.shape[0]
        w_bf[0:half, :] = wlo_ref[...].astype(BF16)
        w_bf[half:2 * half, :] = whi_ref[...].astype(BF16)

    o_ref[...] = _mm_nt(h_all[rows, :], w_bf[...]).astype(o_ref.dtype)


def _modproj(xs, mods, layer, norm_w, wt, tile_rows, small_row, moe=None):
    tm, tn = PROJ_TM, PROJ_TN
    dual_x = moe is None
    n_tiles = len(tile_rows)
    n_i, n_ctx = T // tm, T_CTX // tm

    def w_row(side):
        def index(j, i):
            r = jnp.int32(tile_rows[0][side])
            for k in range(1, n_tiles):
                r = jnp.where(j == k, tile_rows[k][side], r)
            return pl.multiple_of(r, SUBLANES), 0
        return index

    def tok(j, i):
        return jnp.where(j == 0, i, n_i - 1)

    tile = pl.BlockSpec((tm, D), lambda j, i: (tok(j, i), 0))
    if dual_x:
        in_specs = [pl.BlockSpec((tm, D), lambda j, i: (jnp.minimum(tok(j, i), n_ctx - 1), 0)),
                    pl.BlockSpec((tm, D), lambda j, i: (jnp.maximum(tok(j, i) - n_ctx, 0), 0))]
        args = list(xs)
    else:
        in_specs = [tile, pl.BlockSpec((tm * SUBLANES, LANES), lambda j, i: (tok(j, i), 0)),
                    _mod_spec(layer - 1, 5, tm, tok)]
        args = [xs, moe, mods]
    in_specs += [_mod_spec(layer, 0, tm, tok), _mod_spec(layer, 1, tm, tok), pl.BlockSpec((1, D), lambda j, i: (0, 0)),
                 pl.BlockSpec((pl.Element(tn // 2), pl.Element(D)), w_row(0)),
                 pl.BlockSpec((pl.Element(tn // 2), pl.Element(D)), w_row(1)),
                 pl.BlockSpec((pl.Element(LANES), pl.Element(D)), lambda j, i: (small_row, 0))]
    args += [mods, mods, norm_w.reshape(1, D), wt, wt, wt]
    out_specs = [pl.BlockSpec((tm, tn), lambda j, i: (i, j)),
                 pl.BlockSpec((tm, LANES), lambda j, i: (tok(j, i), 0))]
    out_shape = [jax.ShapeDtypeStruct((T, n_tiles * tn), BF16), jax.ShapeDtypeStruct((T, LANES), F32)]
    if not dual_x:
        out_specs.append(tile)
        out_shape.append(jax.ShapeDtypeStruct((T, D), F32))
    return pl.pallas_call(
        functools.partial(_modproj_kernel, dual_x=dual_x),
        grid=(n_tiles, n_i), in_specs=in_specs, out_specs=out_specs, out_shape=out_shape,
        scratch_shapes=[pltpu.VMEM((T, D), BF16), pltpu.VMEM((tn, D), BF16)],
        compiler_params=_cparams(2),
        name=f"modproj{layer}",
    )(*args)


def _expand_heads(v, off):
    hi = (lax.broadcasted_iota(jnp.int32, (v.shape[0], LANES), 1) >= SSD_P).astype(jnp.int32)
    tiles = [jnp.take_along_axis(v, hi + (off + 2 * q), axis=1) for q in range(SSD_HEADS // 2)]
    return jnp.concatenate(tiles, axis=1)


def _ssd_kernel(*refs, seq, has_h0):
    if has_h0:
        (z_ref, x_ref, bc_ref, dt_ref, cwx_ref, cwbc_ref, cbx_ref, cbbc_ref, dtb_ref, alog_ref, dsk_ref,
         nw_ref, h0f_ref, h0b_ref, y_ref, sf_ref, sb_ref,
         xpad, bcpad, xc, bcc, a_scr, dt_scr, yf, yb, hf, hb) = refs
    else:
        (z_ref, x_ref, bc_ref, dt_ref, cwx_ref, cwbc_ref, cbx_ref, cbbc_ref, dtb_ref, alog_ref, dsk_ref,
         nw_ref, y_ref, sf_ref, sb_ref,
         xpad, bcpad, xc, bcc, a_scr, dt_scr, yf, yb, hf, hb) = refs
    L = SSD_L
    nc = seq // L
    pad = SUBLANES
    half = SSD_CONV // 2

    for buf, src, cw, cb, dst in ((xpad, x_ref, cwx_ref, cbx_ref, xc), (bcpad, bc_ref, cwbc_ref, cbbc_ref, bcc)):
        width = buf.shape[1]
        buf[0:pad, :] = jnp.zeros((pad, width), F32)
        buf[pad + seq:2 * pad + seq, :] = jnp.zeros((pad, width), F32)
        buf[pad:pad + seq, :] = src[...].astype(F32)
        for blk in range(nc):
            acc = jnp.broadcast_to(cb[...], (L, width))
            for j in range(SSD_CONV):
                r0 = pad - half + j + blk * L
                acc = acc + cw[j:j + 1, :] * buf[r0:r0 + L, :]
            dst[blk * L:(blk + 1) * L, :] = _silu(acc)

    lane = lax.broadcasted_iota(jnp.int32, (seq, LANES), 1)
    dts = jnp.where(lane < 2 * SSD_HEADS, _softplus(dt_ref[...] + dtb_ref[...]), 0.0)
    dt_scr[...] = dts
    a_scr[...] = dts * (-jnp.exp(alog_ref[...]))

    if has_h0:
        hf[...] = h0f_ref[...]
        hb[...] = h0b_ref[...]
    else:
        hf[...] = jnp.zeros(hf.shape, F32)
        hb[...] = jnp.zeros(hb.shape, F32)

    row = lax.broadcasted_iota(jnp.int32, (L, L), 0)
    col = lax.broadcasted_iota(jnp.int32, (L, L), 1)
    lane_l = lax.broadcasted_iota(jnp.int32, (L, LANES), 1)
    lo_half = lane_l < SSD_P

    def chunk(c, fwd, h_scr):
        off = 0 if fwd else SSD_HEADS
        r0 = pl.multiple_of(c * L, L)
        a = a_scr[pl.ds(r0, L), :]
        dt = dt_scr[pl.ds(r0, L), :]
        cs = _cumsum_rows(a, L)
        total = cs[L - 1:L, :]
        if fwd:
            u = cs
            rvec = jnp.exp(cs)
            ed = jnp.exp(total - cs) * dt
            keep = col <= row
        else:
            ex = cs - a
            u = -ex
            rvec = jnp.exp(total - ex)
            ed = jnp.exp(ex) * dt
            keep = col >= row
        ut = jnp.transpose(u)
        dtt = jnp.transpose(dt)
        tcol = jnp.transpose(jnp.broadcast_to(total, (L, LANES)))[:, 0:1]
        rexp = _expand_heads(rvec, off)
        edexp = _expand_heads(ed, off)
        x = xc[pl.ds(r0, L), :]
        bc = bcc[pl.ds(r0, L), :]
        outs = []
        for g in range(SSD_GROUPS):
            bg = bc[:, g * SSD_N:(g + 1) * SSD_N]
            cg = bc[:, SSD_GROUPS * SSD_N + g * SSD_N:SSD_GROUPS * SSD_N + (g + 1) * SSD_N]
            cbm = _mm_nt(cg, bg)
            hg = h_scr[g * GROUP_W:(g + 1) * GROUP_W, :]
            xg = x[:, g * GROUP_W:(g + 1) * GROUP_W]
            y_off = _mm_nt(cg, hg) * rexp[:, g * GROUP_W:(g + 1) * GROUP_W]
            tiles = []
            for p in range(HEADS_PER_GROUP // 2):
                xt = xg[:, p * LANES:(p + 1) * LANES]
                acc = None
                for s in range(2):
                    h = off + g * HEADS_PER_GROUP + 2 * p + s
                    seg = u[:, h:h + 1] - ut[h:h + 1, :]
                    m = cbm * jnp.exp(jnp.where(keep, seg, -jnp.inf)) * dtt[h:h + 1, :]
                    xm = jnp.where(lo_half if s == 0 else jnp.logical_not(lo_half), xt, 0.0)
                    d = _mm(m, xm)
                    acc = d if acc is None else acc + d
                tiles.append(acc)
            outs.append(y_off + jnp.concatenate(tiles, axis=1))
            decs = []
            for hh in range(HEADS_PER_GROUP):
                h = off + g * HEADS_PER_GROUP + hh
                decs.append(jnp.broadcast_to(jnp.exp(tcol[h:h + 1, :]), (SSD_P, SSD_N)))
            dec = jnp.concatenate(decs, axis=0)
            h_scr[g * GROUP_W:(g + 1) * GROUP_W, :] = dec * hg + _mm_tn(xg * edexp[:, g * GROUP_W:(g + 1) * GROUP_W], bg)
        return r0, jnp.concatenate(outs, axis=1)

    def body(i, carry):
        r0, y = chunk(i, True, hf)
        yf[pl.ds(r0, L), :] = y
        r0, y = chunk(nc - 1 - i, False, hb)
        yb[pl.ds(r0, L), :] = y
        return carry

    lax.fori_loop(0, nc, body, 0)
    sf_ref[...] = hf[...]
    sb_ref[...] = hb[...]

    for blk in range(nc):
        rs = slice(blk * L, (blk + 1) * L)
        y = yf[rs, :] + yb[rs, :] + dsk_ref[...] * xc[rs, :]
        y = y * _silu(z_ref[rs, :].astype(F32))
        y_ref[rs, :] = _rms(y, nw_ref[...]).astype(y_ref.dtype)


def _ssd(p0, p0dt, tok0, nseq, seq, cw, cb, dtb, alog, dskip, nw, h0f=None, h0b=None):
    has_h0 = h0f is not None
    b0 = tok0 // seq

    def cols(width, start):
        return pl.BlockSpec((seq, width), lambda s: (b0 + s, start // width))

    def full(shape):
        return pl.BlockSpec(shape, lambda s: (0,) * len(shape))

    in_specs = [cols(D_SSD, P0_Z), cols(D_SSD, P0_X), cols(512, P0_BC), cols(LANES, 0),
                full((SSD_CONV, D_SSD)), full((SSD_CONV, 512)), full((1, D_SSD)), full((1, 512)),
                full((1, LANES)), full((1, LANES)), full((1, D_SSD)), full((1, D_SSD))]
    args = [p0, p0, p0, p0dt, cw[:, :D_SSD], cw[:, D_SSD:], cb[:D_SSD].reshape(1, -1), cb[D_SSD:].reshape(1, -1),
            dtb, alog, dskip, nw.reshape(1, -1)]
    st_spec = pl.BlockSpec((None, D_SSD, SSD_N), lambda s: (s, 0, 0))
    if has_h0:
        in_specs += [st_spec, st_spec]
        args += [h0f, h0b]
    st_shape = jax.ShapeDtypeStruct((nseq, D_SSD, SSD_N), F32)
    return pl.pallas_call(
        functools.partial(_ssd_kernel, seq=seq, has_h0=has_h0),
        grid=(nseq,), in_specs=in_specs,
        out_specs=[pl.BlockSpec((seq, D_SSD), lambda s: (s, 0)), st_spec, st_spec],
        out_shape=[jax.ShapeDtypeStruct((nseq * seq, D_SSD), BF16), st_shape, st_shape],
        scratch_shapes=[pltpu.VMEM((seq + 2 * SUBLANES, D_SSD), F32), pltpu.VMEM((seq + 2 * SUBLANES, 512), F32),
                        pltpu.VMEM((seq, D_SSD), F32), pltpu.VMEM((seq, 512), F32),
                        pltpu.VMEM((seq, LANES), F32), pltpu.VMEM((seq, LANES), F32),
                        pltpu.VMEM((seq, D_SSD), F32), pltpu.VMEM((seq, D_SSD), F32),
                        pltpu.VMEM((D_SSD, SSD_N), F32), pltpu.VMEM((D_SSD, SSD_N), F32)],
        compiler_params=_cparams(1),
        name=f"ssd{seq}",
    )(*args)


def _place_halves(tile, kv_in_high):
    lo = lax.broadcasted_iota(jnp.int32, tile.shape, 1) < ATT_HD
    swapped = pltpu.roll(tile, ATT_HD, 1)
    if kv_in_high:
        return jnp.where(lo, swapped, 0.0), jnp.where(lo, 0.0, tile)
    return jnp.where(lo, tile, 0.0), jnp.where(lo, 0.0, swapped)


def _place_rows(vt, kv_in_high):
    head = vt[ATT_HD:, :] if kv_in_high else vt[:ATT_HD, :]
    z = jnp.zeros_like(head)
    return jnp.concatenate([head, z], axis=0), jnp.concatenate([z, head], axis=0)


LOG2E = 1.4426950408889634
SCORE_SCALE = ATT_SCALE * LOG2E


def _sink_attend_t(score_parts, value_parts, sink2):
    m = sink2
    for s in score_parts:
        m = jnp.maximum(m, jnp.max(s, axis=0, keepdims=True))
    denom = jnp.exp2(sink2 - m)
    out = None
    for s, v in zip(score_parts, value_parts):
        p = jnp.exp2(s - m)
        denom = denom + jnp.sum(p, axis=0, keepdims=True)
        o = _mm(v, p)
        out = o if out is None else out + o
    return out * (1.0 / denom)


def _attn_schedule(n, scores, attend):
    scores(0)
    for j in range(n):
        if j + 1 < n:
            scores(j + 1)
        attend(j)


def _ctx_attn_kernel(q_ref, k_ref, v_ref, sink_ref, o_ref, kt_ref, vt_ref, s_a, s_b):
    sink2 = sink_ref[...] * LOG2E
    bufs = (s_a, s_b)
    half = SEQ // 2
    for t in range(ATT_KV_DIM // LANES):
        cols = slice(t * LANES, (t + 1) * LANES)
        kt_ref[cols, :] = jnp.transpose(k_ref[:, cols].astype(F32))
        vt_ref[cols, :] = jnp.transpose(v_ref[:, cols].astype(F32))

    def kv_tile(ref, j):
        return ref[:, (j // 2) * LANES:(j // 2 + 1) * LANES].astype(F32), (j % 2 == 1)

    def scores(j):
        k_lo, k_hi = _place_halves(*kv_tile(k_ref, j))
        qst = jnp.concatenate([q_ref[:, qt * LANES:(qt + 1) * LANES] for qt in (2 * j, 2 * j + 1)], axis=0)
        bufs[j % 2][...] = _mm_nt(jnp.concatenate([k_lo, k_hi], axis=0), qst) * SCORE_SCALE

    def attend(j):
        src = bufs[j % 2]
        v, high = kv_tile(v_ref, j)
        vts = _place_rows(jnp.transpose(v), high)
        for ql in range(2):
            qt = 2 * j + ql
            for qh in range(2):
                cols = slice(ql * SEQ + qh * half, ql * SEQ + (qh + 1) * half)
                acc = None
                for s, vv in enumerate(vts):
                    o = _sink_attend_t([src[s * SEQ:(s + 1) * SEQ, cols]], [vv], sink2[:, 2 * qt + s:2 * qt + s + 1])
                    acc = o if acc is None else acc + o
                o_ref[qh * half:(qh + 1) * half, qt * LANES:(qt + 1) * LANES] = jnp.transpose(acc).astype(o_ref.dtype)

    _attn_schedule(ATT_KV, scores, attend)


def _ctx_attn(p0, sink):
    def cols(width, start):
        return pl.BlockSpec((SEQ, width), lambda b: (b, start // width))

    sbuf = pltpu.VMEM((2 * SEQ, 2 * SEQ), F32)
    return pl.pallas_call(
        _ctx_attn_kernel,
        grid=(BATCH,),
        in_specs=[cols(D, P0_Q), cols(ATT_KV_DIM, P0_K), cols(ATT_KV_DIM, P0_V),
                  pl.BlockSpec((1, LANES), lambda b: (0, 0))],
        out_specs=[pl.BlockSpec((SEQ, D), lambda b: (b, 0))]
        + [pl.BlockSpec((None, ATT_KV_DIM, SEQ), lambda b: (b, 0, 0))] * 2,
        out_shape=[jax.ShapeDtypeStruct((T_CTX, D), BF16)] + [jax.ShapeDtypeStruct((BATCH, ATT_KV_DIM, SEQ), F32)] * 2,
        scratch_shapes=[sbuf, sbuf],
        compiler_params=_cparams(1),
        name="ctx_attn",
    )(p0, p0, p0, sink)


def _rope_tables():
    quarter = ATT_HD // 4
    t = np.arange(DEC_SEQ)
    lane = np.arange(LANES)
    inv = ROPE_BASE ** (-(lane % quarter).astype(np.float64) / quarter)
    pos = np.where(((lane % ATT_HD) < ATT_HD // 2)[None, :], (t // GRID_W)[:, None], (t % GRID_W)[:, None])
    ang = pos * inv[None, :]
    first = ((lane % (2 * quarter)) < quarter)[None, :]
    cos, sin = np.cos(ang), np.sin(ang)
    return (jnp.asarray(cos, F32), jnp.asarray(np.where(first, -sin, 0.0), F32),
            jnp.asarray(np.where(first, 0.0, sin), F32))


def _rope(x, cos, sa, sb):
    quarter = ATT_HD // 4
    return x * cos + pltpu.roll(x, LANES - quarter, 1) * sa + pltpu.roll(x, quarter, 1) * sb


def _lat_attn_kernel(q_ref, kp_ref, kc_ref, kn_ref, vp_ref, vc_ref, vn_ref, ck_ref, cv_ref,
                     cos_ref, sa_ref, sb_ref, sink_ref, o_ref, c_a, c_b, w_a, w_b):
    blk = pl.program_id(1)
    nb = pl.num_programs(1)
    B = ATT_BLOCK
    sink2 = sink_ref[...] * LOG2E
    cbufs, wbufs = (c_a, c_b), (w_a, w_b)

    def tables(b):
        r0 = pl.multiple_of(b * B, B)
        return cos_ref[pl.ds(r0, B), :], sa_ref[pl.ds(r0, B), :], sb_ref[pl.ds(r0, B), :]

    tq = tables(blk)
    tk = [tables(jnp.maximum(blk - 1, 0)), tq, tables(jnp.minimum(blk + 1, nb - 1))]
    kabs = (blk - 1) * B + lax.broadcasted_iota(jnp.int32, (3 * B, B), 0)
    qpos = blk * B + lax.broadcasted_iota(jnp.int32, (3 * B, B), 1)
    ok = (jnp.abs(qpos - kabs) <= WINDOW) & (kabs >= 0) & (kabs < nb * B)
    ok = jnp.concatenate([ok, ok], axis=1)
    ok = jnp.concatenate([ok, ok], axis=0)

    def scores(j):
        high = (j % 2 == 1)
        sl = slice((j // 2) * LANES, (j // 2 + 1) * LANES)
        kw = jnp.concatenate([_rope(r[:, sl].astype(F32), *tb) for r, tb in zip((kp_ref, kc_ref, kn_ref), tk)], axis=0)
        qs = [q_ref[:, qt * LANES:(qt + 1) * LANES].astype(F32) for qt in (2 * j, 2 * j + 1)]
        q_plain = jnp.concatenate(qs, axis=0)
        q_rope = jnp.concatenate([_rope(q, *tq) for q in qs], axis=0)
        ck = jnp.transpose(ck_ref[sl, :])
        cbufs[j % 2][...] = _mm_nt(jnp.concatenate(_place_halves(ck, high), axis=0), q_plain) * SCORE_SCALE
        win = _mm_nt(jnp.concatenate(_place_halves(kw, high), axis=0), q_rope) * SCORE_SCALE
        wbufs[j % 2][...] = jnp.where(ok, win, -jnp.inf)

    def attend(j):
        high = (j % 2 == 1)
        sl = slice((j // 2) * LANES, (j // 2 + 1) * LANES)
        vw = jnp.concatenate([jnp.transpose(r[:, sl].astype(F32)) for r in (vp_ref, vc_ref, vn_ref)], axis=1)
        vts = _place_rows(vw, high)
        cvts = _place_rows(cv_ref[sl, :], high)
        csrc, wsrc = cbufs[j % 2], wbufs[j % 2]
        for ql in range(2):
            qt = 2 * j + ql
            cols = slice(ql * B, (ql + 1) * B)
            acc = None
            for s in range(2):
                parts = [csrc[s * PAST_LEN:(s + 1) * PAST_LEN, cols], wsrc[s * 3 * B:(s + 1) * 3 * B, cols]]
                o = _sink_attend_t(parts, [cvts[s], vts[s]], sink2[:, 2 * qt + s:2 * qt + s + 1])
                acc = o if acc is None else acc + o
            o_ref[:, qt * LANES:(qt + 1) * LANES] = jnp.transpose(acc).astype(o_ref.dtype)

    _attn_schedule(ATT_KV, scores, attend)


def _lat_attn(p0, ck, cv, sink):
    nb = DEC_SEQ // ATT_BLOCK
    base = T_CTX // ATT_BLOCK

    def kv(start, shift):
        return pl.BlockSpec((ATT_BLOCK, ATT_KV_DIM),
                            lambda b, i: (base + b * nb + jnp.clip(i + shift, 0, nb - 1), start // ATT_KV_DIM))

    def full(shape):
        return pl.BlockSpec(shape, lambda b, i: (0,) * len(shape))

    cache = pl.BlockSpec((None, ATT_KV_DIM, PAST_LEN), lambda b, i: (b, 0, 0))
    cos, sa, sb = _rope_tables()
    return pl.pallas_call(
        _lat_attn_kernel,
        grid=(DEC_BATCH, nb),
        in_specs=[pl.BlockSpec((ATT_BLOCK, D), lambda b, i: (base + b * nb + i, P0_Q // D)),
                  kv(P0_K, -1), kv(P0_K, 0), kv(P0_K, 1), kv(P0_V, -1), kv(P0_V, 0), kv(P0_V, 1),
                  cache, cache, full((DEC_SEQ, LANES)), full((DEC_SEQ, LANES)), full((DEC_SEQ, LANES)),
                  full((1, LANES))],
        out_specs=pl.BlockSpec((ATT_BLOCK, D), lambda b, i: (b * nb + i, 0)),
        out_shape=jax.ShapeDtypeStruct((T_LAT, D), BF16),
        scratch_shapes=[pltpu.VMEM((2 * PAST_LEN, 2 * ATT_BLOCK), F32)] * 2
        + [pltpu.VMEM((2 * 3 * ATT_BLOCK, 2 * ATT_BLOCK), F32)] * 2,
        compiler_params=_cparams(2),
        name="lat_attn",
    )(p0, p0, p0, p0, p0, p0, p0, ck, cv, cos, sa, sb, sink)


def _log_sigmoid(x):
    return jnp.minimum(x, 0.0) - jnp.log(1.0 + jnp.exp(-jnp.abs(x)))


def _gla_kernel(*refs, seq, has_s0):
    if has_s0:
        (q_ref, k_ref, v_ref, r_ref, lr_ref, w2f_ref, w2b_ref, bf_ref, bb_ref, nw_ref, s0f_ref, s0b_ref,
         y_ref, sf_ref, sb_ref, gf, gb, yf, yb, stf, stb) = refs
    else:
        (q_ref, k_ref, v_ref, r_ref, lr_ref, w2f_ref, w2b_ref, bf_ref, bb_ref, nw_ref,
         y_ref, sf_ref, sb_ref, gf, gb, yf, yb, stf, stb) = refs
    C = GLA_C
    nc = seq // C
    lr = lr_ref[...]
    gf[...] = _log_sigmoid(_mm(lr, w2f_ref[...]) + bf_ref[...]) / GLA_GATE_NORM
    gb[...] = _log_sigmoid(_mm(lr, w2b_ref[...]) + bb_ref[...]) / GLA_GATE_NORM
    for h in range(GLA_HEADS):
        rows = slice(h * GLA_DV, (h + 1) * GLA_DV)
        if has_s0:
            stf[rows, :] = jnp.transpose(s0f_ref[h * GLA_DK:(h + 1) * GLA_DK, :])
            stb[rows, :] = jnp.transpose(s0b_ref[h * GLA_DK:(h + 1) * GLA_DK, :])
        else:
            stf[rows, :] = jnp.zeros((GLA_DV, GLA_DK), F32)
            stb[rows, :] = jnp.zeros((GLA_DV, GLA_DK), F32)

    row = lax.broadcasted_iota(jnp.int32, (C, C), 0)
    col = lax.broadcasted_iota(jnp.int32, (C, C), 1)
    qscale = GLA_DK ** -0.5

    def chunk(c, fwd):
        g_scr, y_scr, st = (gf, yf, stf) if fwd else (gb, yb, stb)
        r0 = pl.multiple_of(c * C, C)
        g = g_scr[pl.ds(r0, C), :]
        cs = _cumsum_rows(g, C)
        total = cs[C - 1:C, :]
        q = q_ref[pl.ds(r0, C), :].astype(F32) * qscale
        k = k_ref[pl.ds(r0, C), :].astype(F32)
        v = v_ref[pl.ds(r0, C), :]
        if fwd:
            qs, ks, ke = q * jnp.exp(cs), k * jnp.exp(-cs), k * jnp.exp(total - cs)
            keep = col <= row
        else:
            ex = cs - g
            qs, ks, ke = q * jnp.exp(total - ex), k * jnp.exp(ex - total), k * jnp.exp(ex)
            keep = col >= row
        dec = jnp.exp(total)
        for h in range(GLA_HEADS):
            kc = slice(h * GLA_DK, (h + 1) * GLA_DK)
            vc = slice(h * GLA_DV, (h + 1) * GLA_DV)
            s_t = st[vc, :]
            att = jnp.where(keep, _mm_nt(qs[:, kc], ks[:, kc]), 0.0)
            y_scr[pl.ds(r0, C), vc] = _mm(att, v[:, vc]) + _mm_nt(qs[:, kc], s_t)
            st[vc, :] = dec[:, kc] * s_t + _mm_tn(v[:, vc], ke[:, kc])

    def body(i, carry):
        chunk(i, True)
        chunk(nc - 1 - i, False)
        return carry

    lax.fori_loop(0, nc, body, 0, unroll=4)

    nw = nw_ref[...]
    for blk in range(seq // LANES):
        rs = slice(blk * LANES, (blk + 1) * LANES)
        y = yf[rs, :] + yb[rs, :]
        gate = _silu(r_ref[rs, :].astype(F32))
        for h in range(GLA_HEADS):
            vc = slice(h * GLA_DV, (h + 1) * GLA_DV)
            y_ref[rs, vc] = (_rms(y[:, vc], nw) * gate[:, vc]).astype(y_ref.dtype)
    for h in range(GLA_HEADS):
        rows = slice(h * GLA_DV, (h + 1) * GLA_DV)
        sf_ref[h * GLA_DK:(h + 1) * GLA_DK, :] = jnp.transpose(stf[rows, :])
        sb_ref[h * GLA_DK:(h + 1) * GLA_DK, :] = jnp.transpose(stb[rows, :])


def _gla(p1, p1lr, tok0, nseq, seq, w2f, w2b, bgf, bgb, nw, s0f=None, s0b=None):
    has_s0 = s0f is not None
    b0 = tok0 // seq
    dk_all, dv_all = GLA_HEADS * GLA_DK, GLA_HEADS * GLA_DV

    def cols(width, start):
        return pl.BlockSpec((seq, width), lambda s: (b0 + s, start // width))

    def full(shape):
        return pl.BlockSpec(shape, lambda s: (0,) * len(shape))

    in_specs = [cols(dk_all, P1_Q), cols(dk_all, P1_K), cols(dv_all, P1_V), cols(dv_all, P1_R), cols(LANES, 0),
                full((LANES, dk_all)), full((LANES, dk_all)), full((1, dk_all)), full((1, dk_all)), full((1, GLA_DV))]
    args = [p1, p1, p1, p1, p1lr, w2f, w2b, bgf.reshape(1, -1), bgb.reshape(1, -1), nw.reshape(1, -1)]
    st_spec = pl.BlockSpec((None, dk_all, GLA_DV), lambda s: (s, 0, 0))
    if has_s0:
        in_specs += [st_spec, st_spec]
        args += [s0f, s0b]
    st_shape = jax.ShapeDtypeStruct((nseq, dk_all, GLA_DV), F32)
    return pl.pallas_call(
        functools.partial(_gla_kernel, seq=seq, has_s0=has_s0),
        grid=(nseq,), in_specs=in_specs,
        out_specs=[pl.BlockSpec((seq, dv_all), lambda s: (s, 0)), st_spec, st_spec],
        out_shape=[jax.ShapeDtypeStruct((nseq * seq, dv_all), BF16), st_shape, st_shape],
        scratch_shapes=[pltpu.VMEM((seq, dk_all), F32), pltpu.VMEM((seq, dk_all), F32),
                        pltpu.VMEM((seq, dv_all), F32), pltpu.VMEM((seq, dv_all), F32),
                        pltpu.VMEM((dv_all, GLA_DK), F32), pltpu.VMEM((dv_all, GLA_DK), F32)],
        compiler_params=_cparams(1),
        name=f"gla{seq}",
    )(*args)


ROUTE_TM = 512
ROUTE_SUB = 256
ROUTE_ROWS = 32


def _split_bf16(x):
    hi = x.astype(BF16)
    return hi, (x - hi.astype(F32)).astype(BF16)


def _outproj_kernel(*refs, n_in, dual_x):
    y_refs = refs[:2 * n_in]
    n_x = 2 if dual_x else 1
    x_refs = refs[2 * n_in + 1:2 * n_in + 1 + n_x]
    w_ref = refs[2 * n_in]
    (g1_ref, sh_ref, sc_ref, nw_ref, wr_ref, br_ref,
     xo_ref, tr_ref, route_ref, cnt_ref, w_scr, wr_hl, carry, o_scr) = refs[2 * n_in + 1 + n_x:]
    is_ctx = pl.program_id(0) < T_CTX // ROUTE_TM

    @pl.when(pl.program_id(0) == 0)
    def _():
        w_scr[...] = w_ref[...].astype(BF16)
        hi, lo = _split_bf16(jnp.transpose(wr_ref[...]))
        wr_hl[0:LANES, :] = hi
        wr_hl[LANES:2 * LANES, :] = lo
        carry[...] = jnp.zeros(carry.shape, F32)

    o = None
    for i in range(n_in):
        y = jnp.where(is_ctx, y_refs[2 * i][...], y_refs[2 * i + 1][...])
        d = jnp.dot(y, w_scr[i * D:(i + 1) * D, :], preferred_element_type=F32)
        o = d if o is None else o + d
    o_scr[...] = o
    for sub in range(ROUTE_TM // ROUTE_SUB):
        _outproj_subtile(sub, is_ctx, o_scr, x_refs, dual_x, g1_ref, sh_ref, sc_ref, nw_ref, br_ref,
                         xo_ref, tr_ref, route_ref, cnt_ref, wr_hl, carry)


def _outproj_subtile(sub, is_ctx, o_scr, x_refs, dual_x, g1_ref, sh_ref, sc_ref, nw_ref, br_ref,
                     xo_ref, tr_ref, route_ref, cnt_ref, wr_hl, carry):
    tm = ROUTE_SUB
    rows = slice(sub * tm, (sub + 1) * tm)
    x_in = jnp.where(is_ctx, x_refs[0][rows, :], x_refs[1][rows, :]) if dual_x else x_refs[0][rows, :]
    x = x_in + g1_ref[0] * o_scr[rows, :]
    xo_ref[rows, :] = x
    t = _rms(x, nw_ref[...]) * (1.0 + sc_ref[0]) + sh_ref[0]
    for k in range(D // LANES):
        tr_ref[pl.ds(sub * tm * SUBLANES + k, tm, stride=SUBLANES), :] = t[:, k * LANES:(k + 1) * LANES]

    t_hi, t_lo = _split_bf16(t)
    lg = _mm_nt(wr_hl[...], t_hi)
    nr = ROUTE_ROWS
    logit = lg[0:nr, :] + lg[LANES:LANES + nr, :] + _mm_nt(wr_hl[0:LANES, :], t_lo)[0:nr, :] + br_ref[0:nr, :]
    rowf = lax.broadcasted_iota(jnp.int32, (nr, tm), 0).astype(F32)
    neg = -jnp.inf

    def first_argmax(v, vmax):
        return jnp.min(jnp.where(v == vmax, rowf, float(LANES)), axis=0, keepdims=True)

    gl = jnp.where(rowf < N_GROUPS, logit, neg)
    gmax = jnp.max(gl, axis=0, keepdims=True)
    gsel = first_argmax(gl, gmax)
    gprob = 1.0 / jnp.sum(jnp.exp(gl - gmax), axis=0, keepdims=True)
    first = N_GROUPS + EXP_PER_GROUP * gsel
    el = jnp.where((rowf >= first) & (rowf < first + EXP_PER_GROUP), logit, neg)
    m1 = jnp.max(el, axis=0, keepdims=True)
    i1 = first_argmax(el, m1)
    el2 = jnp.where(rowf == i1, neg, el)
    m2 = jnp.max(el2, axis=0, keepdims=True)
    i2 = first_argmax(el2, m2)
    e2 = jnp.exp(m2 - m1)
    c1 = gprob / (1.0 + e2)
    c2 = gprob * e2 / (1.0 + e2)
    x1 = i1 - N_GROUPS
    x2 = i2 - N_GROUPS

    erow = rowf
    hot = ((erow == x1) | (erow == x2)).astype(F32)
    tri = (lax.broadcasted_iota(jnp.int32, (tm, tm), 0) < lax.broadcasted_iota(jnp.int32, (tm, tm), 1))
    before = _mm(hot, tri.astype(F32)) + carry[...]
    r1 = jnp.sum(jnp.where(erow == x1, before, 0.0), axis=0, keepdims=True)
    r2 = jnp.sum(jnp.where(erow == x2, before, 0.0), axis=0, keepdims=True)
    total = carry[...] + _mm(hot, jnp.ones((tm, tm), F32))
    carry[...] = total
    cnt_ref[...] = total[0:N_EXPERTS, 0:LANES]
    row8 = lax.broadcasted_iota(jnp.int32, (SUBLANES, tm), 0)
    out = jnp.zeros((SUBLANES, tm), F32)
    for k, v in enumerate((x1, x2, c1, c2, r1, r2)):
        out = jnp.where(row8 == k, jnp.broadcast_to(v, (SUBLANES, tm)), out)
    route_ref[:, rows] = out


def _outproj_route(ys, w_out, xs, mods, layer, norm_w, w_router, b_router):
    tm = ROUTE_TM
    n_in = len(ys) // 2
    dual_x = len(xs) == 2
    kdim = w_out.shape[0]

    def full(shape):
        return pl.BlockSpec(shape, lambda i: (0,) * len(shape))

    tile = pl.BlockSpec((tm, D), lambda i: (i, 0))
    pair = list(_ctx_lat_specs(tm))
    in_specs = (pair * n_in + [full((kdim, D))] + (pair if dual_x else [tile])
                + [_mod_spec(layer, 2, tm), _mod_spec(layer, 3, tm), _mod_spec(layer, 4, tm),
                   full((1, D)), full((D, LANES)), full((LANES, ROUTE_SUB))])
    cnt = jax.ShapeDtypeStruct((N_EXPERTS, LANES), F32)
    return pl.pallas_call(
        functools.partial(_outproj_kernel, n_in=n_in, dual_x=dual_x),
        grid=(T // tm,), in_specs=in_specs,
        out_specs=[tile, pl.BlockSpec((tm * SUBLANES, LANES), lambda i: (i, 0)),
                   pl.BlockSpec((SUBLANES, tm), lambda i: (0, i)), full(cnt.shape)],
        out_shape=[jax.ShapeDtypeStruct((T, D), F32), jax.ShapeDtypeStruct((T * SUBLANES, LANES), F32),
                   jax.ShapeDtypeStruct((SUBLANES, T), F32), cnt],
        scratch_shapes=[pltpu.VMEM((kdim, D), BF16), pltpu.VMEM((2 * LANES, D), BF16),
                        pltpu.VMEM((ROUTE_ROWS, ROUTE_SUB), F32), pltpu.VMEM((tm, D), F32)],
        compiler_params=_cparams(1),
        name=f"outproj{layer}",
    )(*ys, w_out, *xs, mods, mods, mods, norm_w.reshape(1, D), w_router,
      jnp.broadcast_to(b_router.reshape(LANES, 1), (LANES, ROUTE_SUB)))


def _moe_meta(counts):
    tm = MOE_TM
    experts = jnp.arange(N_EXPERTS, dtype=jnp.int32)
    counts = jnp.max(counts, axis=1).astype(jnp.int32)
    padded = ((counts + tm - 1) // tm) * tm
    ends = jnp.cumsum(padded)
    tile_start = jnp.arange(MOE_TILES, dtype=jnp.int32) * tm
    te = jnp.sum((tile_start[:, None] >= ends[None, :]).astype(jnp.int32), axis=1)
    last = jnp.max(jnp.where(counts > 0, experts, 0))
    meta = jnp.concatenate([jnp.minimum(te, last), ends[-1:] // tm]).astype(jnp.int32)
    starts = ends - padded
    pads = jnp.concatenate([starts + counts, ends[-1:], ends, jnp.full((1,), MOE_ROWS)]).astype(jnp.int32)
    return starts.astype(jnp.int32), pads, meta


def _expert_changed(meta_ref, j):
    return (j == 0) | (meta_ref[j] != meta_ref[jnp.maximum(j - 1, 0)])


def _moe_up_kernel(pos1_ref, pos2_ref, pads_ref, meta_ref, tr_hbm, wg_ref, wu_ref, a_ref, rowmap_ref,
                   tr_scr, g0, g1, wg_bf, wu_bf, sem):
    j = pl.program_id(0)
    tm = MOE_TM
    ntiles = meta_ref[MOE_TILES]

    def gather(tile, dst):
        for mi in range(tm):
            tok = jnp.minimum(rowmap_ref[tile * tm + mi] >> 1, T - 1)
            dst[mi * SUBLANES:(mi + 1) * SUBLANES, :] = tr_scr[pl.ds(pl.multiple_of(tok * SUBLANES, SUBLANES), SUBLANES), :]

    @pl.when(j == 0)
    def _():
        load = pltpu.make_async_copy(tr_hbm, tr_scr, sem)
        load.start()

        def clear(c, carry):
            for i in range(SUBLANES):
                rowmap_ref[c * SUBLANES + i] = 2 * T
            return carry
        for k in range(N_EXPERTS + 1):
            lax.fori_loop(pads_ref[k] // SUBLANES, pads_ref[N_EXPERTS + 1 + k] // SUBLANES, clear, 0)

        def place(t, carry):
            rowmap_ref[pos1_ref[t]] = 2 * t
            rowmap_ref[pos2_ref[t]] = 2 * t + 1
            return carry
        lax.fori_loop(0, T, place, 0, unroll=8)
        load.wait()
        gather(0, g0)

    def compute(cur, nxt):
        gather(jnp.minimum(j + 1, ntiles - 1), nxt)
        x = _tokmajor_to_std(cur, tm).astype(BF16)
        g = jnp.dot(x, wg_bf[...], preferred_element_type=F32)
        u = jnp.dot(x, wu_bf[...], preferred_element_type=F32)
        a_ref[...] = (_silu(g) * u).astype(a_ref.dtype)

    @pl.when(j < ntiles)
    def _():
        @pl.when(_expert_changed(meta_ref, j))
        def _():
            wg_bf[...] = wg_ref[...].astype(BF16)
            wu_bf[...] = wu_ref[...].astype(BF16)

        pl.when(j % 2 == 0)(functools.partial(compute, g0, g1))
        pl.when(j % 2 == 1)(functools.partial(compute, g1, g0))

    @pl.when(j >= ntiles)
    def _():
        a_ref[...] = jnp.zeros(a_ref.shape, a_ref.dtype)


def _moe_down_kernel(rowmap_ref, cpair_ref, meta_ref, a_ref, wd_ref, *rest, layer, final):
    if final:
        x_hbm, mods_ref, fw_ref, out_c, out_l, acc, y0, y1, wd_bf, xin, xout, io_sems = rest
    else:
        out_hbm, acc, y0, y1, wd_bf = rest
    j = pl.program_id(0)
    tm = MOE_TM
    zrows = 512
    ntiles = meta_ref[MOE_TILES]

    @pl.when(j == 0)
    def _():
        def zero(i, carry):
            acc[pl.ds(pl.multiple_of(i * zrows, zrows), zrows), :] = jnp.zeros((zrows, LANES), F32)
            return carry
        lax.fori_loop(0, ACC_ROWS // zrows, zero, 0)

    def matmul(dst):
        y = jnp.dot(a_ref[...], wd_bf[...], preferred_element_type=F32)
        for k in range(D // LANES):
            dst[pl.ds(k, tm, stride=SUBLANES), :] = y[:, k * LANES:(k + 1) * LANES]

    def scatter(tile, src):
        for b in range(tm // SUBLANES):
            ents = [rowmap_ref[tile * tm + b * SUBLANES + i] for i in range(SUBLANES)]
            offs = [pl.multiple_of((e >> 1) * SUBLANES, SUBLANES) for e in ents]
            olds = [acc[pl.ds(o, SUBLANES), :] for o in offs]
            for i, o in enumerate(offs):
                r = (b * SUBLANES + i) * SUBLANES
                acc[pl.ds(o, SUBLANES), :] = olds[i] + cpair_ref[ents[i]] * src[r:r + SUBLANES, :]

    has_mm = j < ntiles
    has_sc = (j >= 1) & (j <= ntiles)

    @pl.when(has_mm)
    def _():
        @pl.when(_expert_changed(meta_ref, j))
        def _():
            wd_bf[...] = wd_ref[...].astype(BF16)

    for par, (cur, prev) in enumerate(((y0, y1), (y1, y0))):
        mine = (j % 2) == par

        @pl.when(mine & has_mm & has_sc)
        def _():
            matmul(cur)
            scatter(j - 1, prev)

        @pl.when(mine & has_mm & jnp.logical_not(has_sc))
        def _():
            matmul(cur)

        @pl.when(mine & jnp.logical_not(has_mm) & has_sc)
        def _():
            scatter(j - 1, prev)

    @pl.when(j == pl.num_programs(0) - 1)
    def _():
        if not final:
            pltpu.sync_copy(acc.at[0:T * SUBLANES, :], out_hbm)
            return
        ft = MERGE_TM
        n_t = T // ft
        n_c = T_CTX // ft

        def fetch(i):
            return pltpu.make_async_copy(x_hbm.at[i * ft:(i + 1) * ft, :], xin.at[i % 2], io_sems.at[i % 2])

        def flush(i):
            dst = out_c.at[i * ft:(i + 1) * ft, :] if i < n_c else out_l.at[(i - n_c) * ft:(i - n_c + 1) * ft, :]
            return pltpu.make_async_copy(xout.at[i % 2], dst, io_sems.at[2 + i % 2])

        fetch(0).start()
        for i in range(n_t):
            if i + 1 < n_t:
                fetch(i + 1).start()
            fetch(i).wait()
            moe = jnp.concatenate([acc[pl.ds(i * ft * SUBLANES + k, ft, stride=SUBLANES), :]
                                   for k in range(D // LANES)], axis=1)
            row = 0 if i * ft < T_CTX else 1 + (i * ft - T_CTX) // DEC_SEQ
            x = xin[i % 2] + mods_ref[(layer * SUBLANES + row) * 6 + 5] * moe
            if i >= 2:
                flush(i - 2).wait()
            xout[i % 2] = _rms(x, fw_ref[...])
            flush(i).start()
        for i in range(max(n_t - 2, 0), n_t):
            flush(i).wait()


def _moe(tr, route_t, counts, layer, w_gate, w_up, w_down, xmid=None, mods=None, final_w=None):
    tm = MOE_TM
    starts, pads, meta = _moe_meta(counts)
    experts = jnp.arange(N_EXPERTS, dtype=jnp.int32)

    def position(e, r):
        sel = e.astype(jnp.int32)[:, None] == experts[None, :]
        return jnp.sum(jnp.where(sel, starts[None, :], 0), axis=1) + r.astype(jnp.int32)

    pos1 = position(route_t[0], route_t[4])
    pos2 = position(route_t[1], route_t[5])
    cpair = jnp.concatenate([jnp.stack([route_t[2], route_t[3]], axis=1).reshape(2 * T), jnp.zeros((2,), F32)])

    def wspec(shape, n):
        return pl.BlockSpec((None, None) + shape,
                            lambda j, *pre: (layer, pre[n - 1][jnp.minimum(j, MOE_TILES - 1)], 0, 0))

    gscr = pltpu.VMEM((tm * SUBLANES, LANES), F32)
    act, rowmap = pl.pallas_call(
        _moe_up_kernel,
        grid_spec=pltpu.PrefetchScalarGridSpec(
            num_scalar_prefetch=4, grid=(MOE_TILES,),
            in_specs=[pl.BlockSpec(memory_space=pl.ANY), wspec((D, D_EXPERT), 4), wspec((D, D_EXPERT), 4)],
            out_specs=[pl.BlockSpec((tm, D_EXPERT), lambda j, *pre: (j, 0)), pl.BlockSpec(memory_space=pltpu.SMEM)],
            scratch_shapes=[pltpu.VMEM((T * SUBLANES, LANES), F32), gscr, gscr,
                            pltpu.VMEM((D, D_EXPERT), BF16), pltpu.VMEM((D, D_EXPERT), BF16),
                            pltpu.SemaphoreType.DMA(())]),
        out_shape=[jax.ShapeDtypeStruct((MOE_ROWS, D_EXPERT), BF16), jax.ShapeDtypeStruct((MOE_ROWS,), jnp.int32)],
        compiler_params=_cparams(1),
        name=f"moe_up{layer}",
    )(pos1, pos2, pads, meta, tr, w_gate, w_up)

    final = final_w is not None
    any_spec = pl.BlockSpec(memory_space=pl.ANY)
    in_specs = [pl.BlockSpec((tm, D_EXPERT), lambda j, *pre: (jnp.minimum(j, MOE_TILES - 1), 0)),
                wspec((D_EXPERT, D), 3)]
    args = [rowmap, cpair, meta, act, w_down]
    scratch = [pltpu.VMEM((ACC_ROWS, LANES), F32), gscr, gscr, pltpu.VMEM((D_EXPERT, D), BF16)]
    if final:
        stage = pltpu.VMEM((2, MERGE_TM, D), F32)
        in_specs += [any_spec, pl.BlockSpec(mods.shape, lambda j, *pre: (0, 0, 0)),
                     pl.BlockSpec((1, D), lambda j, *pre: (0, 0))]
        args += [xmid, mods, final_w.reshape(1, D)]
        scratch += [stage, stage, pltpu.SemaphoreType.DMA((4,))]
        out_shape = [jax.ShapeDtypeStruct((T_CTX, D), F32), jax.ShapeDtypeStruct((T_LAT, D), F32)]
    else:
        out_shape = [jax.ShapeDtypeStruct((T * SUBLANES, LANES), F32)]
    return pl.pallas_call(
        functools.partial(_moe_down_kernel, layer=layer, final=final),
        grid_spec=pltpu.PrefetchScalarGridSpec(
            num_scalar_prefetch=3, grid=(MOE_TILES + 1,), in_specs=in_specs,
            out_specs=[any_spec] * len(out_shape), scratch_shapes=scratch),
        out_shape=out_shape,
        compiler_params=_cparams(1),
        name=f"moe_down{layer}",
    )(*args)


def _pad_lanes(v):
    return jnp.pad(v.astype(F32), (0, LANES - v.shape[0])).reshape(1, LANES)


def kernel(x_prompt, x_sample, cache_k_attn, cache_v_attn, state_ssd_fwd, state_ssd_bwd, state_gla_fwd, state_gla_bwd, c, c_ctx, w_ada, b_ada, norm_mix_w, norm_ffn_w, w_in_even, conv_w, conv_b, dt_bias_fwd, dt_bias_bwd, a_log_fwd, a_log_bwd, d_skip, ssd_norm_w, attn_sink, w_out_even, w_in_odd, w_gk2_fwd, b_gk_fwd, w_gk2_bwd, b_gk_bwd, gla_norm_w, w_out_odd, w_router_group, b_router_group, w_router_expert, b_router_expert, w_gate_exp, w_up_exp, w_down_exp, final_norm_w):
    depth = w_ada.shape[0]
    assert depth == 2 and x_prompt.shape == (BATCH, SEQ, D) and x_sample.shape == (DEC_BATCH, DEC_SEQ, D)

    cond8 = jnp.concatenate([c_ctx[None, :], c, jnp.zeros((SUBLANES - 1 - DEC_BATCH, D), F32)], axis=0)
    mods = _adaln(cond8, w_ada, b_ada).reshape(depth * SUBLANES * 6, 1, D)
    xs0 = (x_prompt.reshape(T_CTX, D), x_sample.reshape(T_LAT, D))

    def router_params(i):
        wr = jnp.concatenate([w_router_group[i], w_router_expert[i],
                              jnp.zeros((D, LANES - N_GROUPS - N_EXPERTS), F32)], axis=1)
        return wr, _pad_lanes(jnp.concatenate([b_router_group[i], b_router_expert[i]]))

    n_zxbc = 2 * D_SSD + 2 * SSD_GROUPS * SSD_N
    n_dt = 2 * SSD_HEADS
    half = PROJ_TN // 2
    r_bc, r_q = 2 * D_SSD, n_zxbc + n_dt
    r_kv = r_q + D
    rows0 = [(0, half), (D_SSD, D_SSD + half), (r_q, r_q + half), (r_bc, r_kv)]
    p0, p0dt = _modproj(xs0, mods, 0, norm_mix_w[0], jnp.transpose(w_in_even[0]), rows0, n_zxbc)

    dtb = _pad_lanes(jnp.concatenate([dt_bias_fwd[0], dt_bias_bwd[0]]))
    alog = _pad_lanes(jnp.concatenate([a_log_fwd[0], a_log_bwd[0]]))
    dskip = jnp.repeat(d_skip[0], SSD_P).reshape(1, D_SSD)
    ssd_args = (conv_w[0], conv_b[0], dtb, alog, dskip, ssd_norm_w[0])
    y_ssd_c, ssd_f, ssd_b = _ssd(p0, p0dt, 0, BATCH, SEQ, *ssd_args)
    y_ssd_l, _, _ = _ssd(p0, p0dt, T_CTX, DEC_BATCH, DEC_SEQ, *ssd_args,
                         h0f=state_ssd_fwd[:, 0].reshape(DEC_BATCH, D_SSD, SSD_N),
                         h0b=state_ssd_bwd[:, 0].reshape(DEC_BATCH, D_SSD, SSD_N))
    sink = _pad_lanes(attn_sink[0])
    y_att_c, new_kt, new_vt = _ctx_attn(p0, sink)
    def cache_in(t):
        return jnp.transpose(t[:, 0], (0, 2, 3, 1)).reshape(DEC_BATCH, ATT_KV_DIM, PAST_LEN)

    y_att_l = _lat_attn(p0, cache_in(cache_k_attn), cache_in(cache_v_attn), sink)
    xmid0, tr0, route0, cnt0 = _outproj_route([y_ssd_c, y_ssd_l, y_att_c, y_att_l], w_out_even[0], xs0, mods, 0,
                                              norm_ffn_w[0], *router_params(0))
    (moe0,) = _moe(tr0, route0, cnt0, 0, w_gate_exp, w_up_exp, w_down_exp)

    dk_all = GLA_HEADS * GLA_DK
    n_qkvr = 2 * dk_all + 2 * GLA_HEADS * GLA_DV
    n_odd = w_in_odd.shape[2]
    p1, p1lr, x1 = _modproj(xmid0, mods, 1, norm_mix_w[1], jnp.transpose(w_in_odd[0]),
                            [(k * PROJ_TN, k * PROJ_TN + half) for k in range(n_qkvr // PROJ_TN)], n_odd - LANES,
                            moe=moe0)
    lr0 = LANES - 2 * GLA_LOWRANK
    w2f = jnp.zeros((LANES, dk_all), F32).at[lr0:lr0 + GLA_LOWRANK].set(w_gk2_fwd[0])
    w2b = jnp.zeros((LANES, dk_all), F32).at[lr0 + GLA_LOWRANK:].set(w_gk2_bwd[0])
    gla_args = (w2f, w2b, b_gk_fwd[0], b_gk_bwd[0], gla_norm_w[0])
    y_gla_c, gla_f, gla_b = _gla(p1, p1lr, 0, BATCH, SEQ, *gla_args)
    y_gla_l, _, _ = _gla(p1, p1lr, T_CTX, DEC_BATCH, DEC_SEQ, *gla_args,
                         s0f=state_gla_fwd[:, 0].reshape(DEC_BATCH, dk_all, GLA_DV),
                         s0b=state_gla_bwd[:, 0].reshape(DEC_BATCH, dk_all, GLA_DV))
    xmid1, tr1, route1, cnt1 = _outproj_route([y_gla_c, y_gla_l], w_out_odd[0], (x1,), mods, 1,
                                              norm_ffn_w[1], *router_params(1))
    y_c, y_l = _moe(tr1, route1, cnt1, 1, w_gate_exp, w_up_exp, w_down_exp, xmid1, mods, final_w=final_norm_w)

    y_prompt = y_c.reshape(BATCH, SEQ, D)
    y_sample = y_l.reshape(DEC_BATCH, DEC_SEQ, D)
    def cache_out(t):
        return jnp.transpose(t.reshape(BATCH, 1, ATT_KV, ATT_HD, SEQ), (0, 1, 4, 2, 3))

    new_k, new_v = cache_out(new_kt), cache_out(new_vt)
    return (y_prompt, y_sample, new_k, new_v,
            ssd_f.reshape(BATCH, 1, SSD_HEADS, SSD_P, SSD_N), ssd_b.reshape(BATCH, 1, SSD_HEADS, SSD_P, SSD_N),
            gla_f.reshape(BATCH, 1, GLA_HEADS, GLA_DK, GLA_DV), gla_b.reshape(BATCH, 1, GLA_HEADS, GLA_DK, GLA_DV))
```

```python
import functools
import math

import numpy as np
import jax
import jax.numpy as jnp
from jax import lax
from jax.experimental import pallas as pl
from jax.experimental.pallas import tpu as pltpu

F32 = jnp.float32
BF16 = jnp.bfloat16

D = 1024
BATCH, SEQ = 16, 256
DEC_BATCH, DEC_SEQ = 2, 1024
PAST_LEN = 512
GRID_W = 64
EPS = 1e-6
T_CTX = BATCH * SEQ
T_LAT = DEC_BATCH * DEC_SEQ
T = T_CTX + T_LAT

SSD_HEADS, SSD_P, SSD_N, SSD_GROUPS = 16, 64, 128, 2
SSD_CONV = 5
SSD_L = 128
D_SSD = SSD_HEADS * SSD_P
HEADS_PER_GROUP = SSD_HEADS // SSD_GROUPS
GROUP_W = HEADS_PER_GROUP * SSD_P

ATT_HEADS, ATT_KV, ATT_HD = 16, 4, 64
ATT_KV_DIM = ATT_KV * ATT_HD
WINDOW = 128
ATT_BLOCK = 128
ATT_SCALE = ATT_HD ** -0.5
ROPE_BASE = 10000.0

GLA_HEADS, GLA_DK, GLA_DV = 4, 128, 256
GLA_C = 64
GLA_GATE_NORM = 16.0
GLA_LOWRANK = 16

N_GROUPS, EXP_PER_GROUP = 4, 4
N_EXPERTS = 16
D_EXPERT = 512

LANES = 128
SUBLANES = 8
VMEM_LIMIT = 56 * 1024 * 1024

P0_Z, P0_X, P0_Q, P0_BC, P0_K, P0_V = 0, 1024, 2048, 3072, 3584, 3840
P0_W = 4096
P1_Q, P1_K, P1_V, P1_R = 0, 512, 1024, 2048
P1_W = 3072

MOE_TM = 512
MOE_TILES = (2 * T) // MOE_TM + N_EXPERTS
MOE_ROWS = MOE_TILES * MOE_TM
ACC_ROWS = (T + SUBLANES) * SUBLANES
MERGE_TM = 512


def _cparams(n_axes, vmem=VMEM_LIMIT):
    return pltpu.CompilerParams(dimension_semantics=("arbitrary",) * n_axes, vmem_limit_bytes=vmem)


def _silu(x):
    return x / (1.0 + jnp.exp(-x))


def _softplus(x):
    return jnp.maximum(x, 0.0) + jnp.log(1.0 + jnp.exp(-jnp.abs(x)))


def _mm(a, b):
    return jnp.dot(a.astype(BF16), b.astype(BF16), preferred_element_type=F32)


def _mm_nt(a, b):
    return lax.dot_general(a.astype(BF16), b.astype(BF16), (((1,), (1,)), ((), ())),
                           preferred_element_type=F32)


def _mm_tn(a, b):
    return lax.dot_general(a.astype(BF16), b.astype(BF16), (((0,), (0,)), ((), ())),
                           preferred_element_type=F32)


def _rms(x, w):
    return x * lax.rsqrt(jnp.mean(x * x, axis=-1, keepdims=True) + EPS) * w


def _cumsum_rows(x, n):
    row = lax.broadcasted_iota(jnp.int32, x.shape, 0)
    s = 1
    while s < n:
        x = x + jnp.where(row >= s, pltpu.roll(x, s, 0), 0.0)
        s *= 2
    return x


def _mod_row(tok0):
    return jnp.where(tok0 < T_CTX, 0, 1 + (tok0 - T_CTX) // DEC_SEQ)


ADA_TN = 1536


def _adaln_kernel(c_ref, w_ref, b_ref, o_ref):
    s = _silu(c_ref[...])
    o_ref[0] = _mm(s, w_ref[0]) + b_ref[0]


def _adaln(cond8, w_ada, b_ada):
    depth = w_ada.shape[0]
    return pl.pallas_call(
        _adaln_kernel,
        grid=(depth, 6 * D // ADA_TN),
        in_specs=[
            pl.BlockSpec((SUBLANES, D), lambda l, j: (0, 0)),
            pl.BlockSpec((1, D, ADA_TN), lambda l, j: (l, 0, j)),
            pl.BlockSpec((1, 1, ADA_TN), lambda l, j: (l, 0, j)),
        ],
        out_specs=pl.BlockSpec((1, SUBLANES, ADA_TN), lambda l, j: (l, 0, j)),
        out_shape=jax.ShapeDtypeStruct((depth, SUBLANES, 6 * D), F32),
        compiler_params=_cparams(2),
        name="adaln",
    )(cond8, w_ada, b_ada.reshape(depth, 1, 6 * D))


def _mod_spec(layer, chunk, tm, tile_of=lambda i, *_: i):
    return pl.BlockSpec((1, 1, D), lambda *g: ((layer * SUBLANES + _mod_row(tile_of(*g) * tm)) * 6 + chunk, 0, 0))


def _tokmajor_to_std(ref, tm):
    return jnp.concatenate([ref[pl.ds(k, tm, stride=SUBLANES), :] for k in range(D // LANES)], axis=1)


PROJ_TM = 1024
PROJ_TN = 1024


def _ctx_lat_specs(tm, width=D):
    n_ctx = T_CTX // tm
    return (pl.BlockSpec((tm, width), lambda i, *_: (jnp.minimum(i, n_ctx - 1), 0)),
            pl.BlockSpec((tm, width), lambda i, *_: (jnp.maximum(i - n_ctx, 0), 0)))


def _modproj_kernel(*refs, dual_x):
    it = iter(refs)
    if dual_x:
        xc_ref, xl_ref = next(it), next(it)
    else:
        x_ref, moe_ref, g2_ref = next(it), next(it), next(it)
    sh_ref, sc_ref, nw_ref, wlo_ref, whi_ref, ws_ref, o_ref, os_ref = (next(it) for _ in range(8))
    xo_ref = None if dual_x else next(it)
    h_all, w_bf = next(it), next(it)
    j, i = pl.program_id(0), pl.program_id(1)
    tm = PROJ_TM
    rows = pl.ds(pl.multiple_of(i * tm, tm), tm)

    @pl.when(j == 0)
    def _():
        if dual_x:
            x = jnp.where(i < T_CTX // tm, xc_ref[...], xl_ref[...])
        else:
            x = x_ref[...] + g2_ref[0] * _tokmajor_to_std(moe_ref, tm)
            xo_ref[...] = x
        h = (_rms(x, nw_ref[...]) * (1.0 + sc_ref[0]) + sh_ref[0]).astype(BF16)
        h_all[rows, :] = h
        os_ref[...] = _mm_nt(h, ws_ref[...])

    @pl.when(i == 0)
    def _():
        half = wlo_ref.shape[0]
        w_bf[0:half, :] = wlo_ref[...].astype(BF16)
        w_bf[half:2 * half, :] = whi_ref[...].astype(BF16)

    o_ref[...] = _mm_nt(h_all[rows, :], w_bf[...]).astype(o_ref.dtype)


def _modproj(xs, mods, layer, norm_w, wt, tile_rows, small_row, moe=None):
    tm, tn = PROJ_TM, PROJ_TN
    dual_x = moe is None
    n_tiles = len(tile_rows)
    n_i, n_ctx = T // tm, T_CTX // tm

    def w_row(side):
        def index(j, i):
            r = jnp.int32(tile_rows[0][side])
            for k in range(1, n_tiles):
                r = jnp.where(j == k, tile_rows[k][side], r)
            return pl.multiple_of(r, SUBLANES), 0
        return index

    def tok(j, i):
        return jnp.where(j == 0, i, n_i - 1)

    tile = pl.BlockSpec((tm, D), lambda j, i: (tok(j, i), 0))
    if dual_x:
        in_specs = [pl.BlockSpec((tm, D), lambda j, i: (jnp.minimum(tok(j, i), n_ctx - 1), 0)),
                    pl.BlockSpec((tm, D), lambda j, i: (jnp.maximum(tok(j, i) - n_ctx, 0), 0))]
        args = list(xs)
    else:
        in_specs = [tile, pl.BlockSpec((tm * SUBLANES, LANES), lambda j, i: (tok(j, i), 0)),
                    _mod_spec(layer - 1, 5, tm, tok)]
        args = [xs, moe, mods]
    in_specs += [_mod_spec(layer, 0, tm, tok), _mod_spec(layer, 1, tm, tok), pl.BlockSpec((1, D), lambda j, i: (0, 0)),
                 pl.BlockSpec((pl.Element(tn // 2), pl.Element(D)), w_row(0)),
                 pl.BlockSpec((pl.Element(tn // 2), pl.Element(D)), w_row(1)),
                 pl.BlockSpec((pl.Element(LANES), pl.Element(D)), lambda j, i: (small_row, 0))]
    args += [mods, mods, norm_w.reshape(1, D), wt, wt, wt]
    out_specs = [pl.BlockSpec((tm, tn), lambda j, i: (i, j)),
                 pl.BlockSpec((tm, LANES), lambda j, i: (tok(j, i), 0))]
    out_shape = [jax.ShapeDtypeStruct((T, n_tiles * tn), BF16), jax.ShapeDtypeStruct((T, LANES), F32)]
    if not dual_x:
        out_specs.append(tile)
        out_shape.append(jax.ShapeDtypeStruct((T, D), F32))
    return pl.pallas_call(
        functools.partial(_modproj_kernel, dual_x=dual_x),
        grid=(n_tiles, n_i), in_specs=in_specs, out_specs=out_specs, out_shape=out_shape,
        scratch_shapes=[pltpu.VMEM((T, D), BF16), pltpu.VMEM((tn, D), BF16)],
        compiler_params=_cparams(2),
        name=f"modproj{layer}",
    )(*args)


def _expand_heads(v, off):
    hi = (lax.broadcasted_iota(jnp.int32, (v.shape[0], LANES), 1) >= SSD_P).astype(jnp.int32)
    tiles = [jnp.take_along_axis(v, hi + (off + 2 * q), axis=1) for q in range(SSD_HEADS // 2)]
    return jnp.concatenate(tiles, axis=1)


def _ssd_kernel(*refs, seq, has_h0):
    if has_h0:
        (z_ref, x_ref, bc_ref, dt_ref, cwx_ref, cwbc_ref, cbx_ref, cbbc_ref, dtb_ref, alog_ref, dsk_ref,
         nw_ref, h0f_ref, h0b_ref, y_ref, sf_ref, sb_ref,
         xpad, bcpad, xc, bcc, a_scr, dt_scr, yf, yb, hf, hb) = refs
    else:
        (z_ref, x_ref, bc_ref, dt_ref, cwx_ref, cwbc_ref, cbx_ref, cbbc_ref, dtb_ref, alog_ref, dsk_ref,
         nw_ref, y_ref, sf_ref, sb_ref,
         xpad, bcpad, xc, bcc, a_scr, dt_scr, yf, yb, hf, hb) = refs
    L = SSD_L
    nc = seq // L
    pad = SUBLANES
    half = SSD_CONV // 2

    for buf, src, cw, cb, dst in ((xpad, x_ref, cwx_ref, cbx_ref, xc), (bcpad, bc_ref, cwbc_ref, cbbc_ref, bcc)):
        width = buf.shape[1]
        buf[0:pad, :] = jnp.zeros((pad, width), F32)
        buf[pad + seq:2 * pad + seq, :] = jnp.zeros((pad, width), F32)
        buf[pad:pad + seq, :] = src[...].astype(F32)
        for blk in range(nc):
            acc = jnp.broadcast_to(cb[...], (L, width))
            for j in range(SSD_CONV):
                r0 = pad - half + j + blk * L
                acc = acc + cw[j:j + 1, :] * buf[r0:r0 + L, :]
            dst[blk * L:(blk + 1) * L, :] = _silu(acc)

    lane = lax.broadcasted_iota(jnp.int32, (seq, LANES), 1)
    dts = jnp.where(lane < 2 * SSD_HEADS, _softplus(dt_ref[...] + dtb_ref[...]), 0.0)
    dt_scr[...] = dts
    a_scr[...] = dts * (-jnp.exp(alog_ref[...]))

    if has_h0:
        hf[...] = h0f_ref[...]
        hb[...] = h0b_ref[...]
    else:
        hf[...] = jnp.zeros(hf.shape, F32)
        hb[...] = jnp.zeros(hb.shape, F32)

    row = lax.broadcasted_iota(jnp.int32, (L, L), 0)
    col = lax.broadcasted_iota(jnp.int32, (L, L), 1)
    lane_l = lax.broadcasted_iota(jnp.int32, (L, LANES), 1)
    lo_half = lane_l < SSD_P

    def chunk(c, fwd, h_scr):
        off = 0 if fwd else SSD_HEADS
        r0 = pl.multiple_of(c * L, L)
        a = a_scr[pl.ds(r0, L), :]
        dt = dt_scr[pl.ds(r0, L), :]
        cs = _cumsum_rows(a, L)
        total = cs[L - 1:L, :]
        if fwd:
            u = cs
            rvec = jnp.exp(cs)
            ed = jnp.exp(total - cs) * dt
            keep = col <= row
        else:
            ex = cs - a
            u = -ex
            rvec = jnp.exp(total - ex)
            ed = jnp.exp(ex) * dt
            keep = col >= row
        ut = jnp.transpose(u)
        dtt = jnp.transpose(dt)
        tcol = jnp.transpose(jnp.broadcast_to(total, (L, LANES)))[:, 0:1]
        rexp = _expand_heads(rvec, off)
        edexp = _expand_heads(ed, off)
        x = xc[pl.ds(r0, L), :]
        bc = bcc[pl.ds(r0, L), :]
        outs = []
        for g in range(SSD_GROUPS):
            bg = bc[:, g * SSD_N:(g + 1) * SSD_N]
            cg = bc[:, SSD_GROUPS * SSD_N + g * SSD_N:SSD_GROUPS * SSD_N + (g + 1) * SSD_N]
            cbm = _mm_nt(cg, bg)
            hg = h_scr[g * GROUP_W:(g + 1) * GROUP_W, :]
            xg = x[:, g * GROUP_W:(g + 1) * GROUP_W]
            y_off = _mm_nt(cg, hg) * rexp[:, g * GROUP_W:(g + 1) * GROUP_W]
            tiles = []
            for p in range(HEADS_PER_GROUP // 2):
                xt = xg[:, p * LANES:(p + 1) * LANES]
                acc = None
                for s in range(2):
                    h = off + g * HEADS_PER_GROUP + 2 * p + s
                    seg = u[:, h:h + 1] - ut[h:h + 1, :]
                    m = cbm * jnp.exp(jnp.where(keep, seg, -jnp.inf)) * dtt[h:h + 1, :]
                    xm = jnp.where(lo_half if s == 0 else jnp.logical_not(lo_half), xt, 0.0)
                    d = _mm(m, xm)
                    acc = d if acc is None else acc + d
                tiles.append(acc)
            outs.append(y_off + jnp.concatenate(tiles, axis=1))
            decs = []
            for hh in range(HEADS_PER_GROUP):
                h = off + g * HEADS_PER_GROUP + hh
                decs.append(jnp.broadcast_to(jnp.exp(tcol[h:h + 1, :]), (SSD_P, SSD_N)))
            dec = jnp.concatenate(decs, axis=0)
            h_scr[g * GROUP_W:(g + 1) * GROUP_W, :] = dec * hg + _mm_tn(xg * edexp[:, g * GROUP_W:(g + 1) * GROUP_W], bg)
        return r0, jnp.concatenate(outs, axis=1)

    def body(i, carry):
        r0, y = chunk(i, True, hf)
        yf[pl.ds(r0, L), :] = y
        r0, y = chunk(nc - 1 - i, False, hb)
        yb[pl.ds(r0, L), :] = y
        return carry

    lax.fori_loop(0, nc, body, 0)
    sf_ref[...] = hf[...]
    sb_ref[...] = hb[...]

    for blk in range(nc):
        rs = slice(blk * L, (blk + 1) * L)
        y = yf[rs, :] + yb[rs, :] + dsk_ref[...] * xc[rs, :]
        y = y * _silu(z_ref[rs, :].astype(F32))
        y_ref[rs, :] = _rms(y, nw_ref[...]).astype(y_ref.dtype)


def _ssd(p0, p0dt, tok0, nseq, seq, cw, cb, dtb, alog, dskip, nw, h0f=None, h0b=None):
    has_h0 = h0f is not None
    b0 = tok0 // seq

    def cols(width, start):
        return pl.BlockSpec((seq, width), lambda s: (b0 + s, start // width))

    def full(shape):
        return pl.BlockSpec(shape, lambda s: (0,) * len(shape))

    in_specs = [cols(D_SSD, P0_Z), cols(D_SSD, P0_X), cols(512, P0_BC), cols(LANES, 0),
                full((SSD_CONV, D_SSD)), full((SSD_CONV, 512)), full((1, D_SSD)), full((1, 512)),
                full((1, LANES)), full((1, LANES)), full((1, D_SSD)), full((1, D_SSD))]
    args = [p0, p0, p0, p0dt, cw[:, :D_SSD], cw[:, D_SSD:], cb[:D_SSD].reshape(1, -1), cb[D_SSD:].reshape(1, -1),
            dtb, alog, dskip, nw.reshape(1, -1)]
    st_spec = pl.BlockSpec((None, D_SSD, SSD_N), lambda s: (s, 0, 0))
    if has_h0:
        in_specs += [st_spec, st_spec]
        args += [h0f, h0b]
    st_shape = jax.ShapeDtypeStruct((nseq, D_SSD, SSD_N), F32)
    return pl.pallas_call(
        functools.partial(_ssd_kernel, seq=seq, has_h0=has_h0),
        grid=(nseq,), in_specs=in_specs,
        out_specs=[pl.BlockSpec((seq, D_SSD), lambda s: (s, 0)), st_spec, st_spec],
        out_shape=[jax.ShapeDtypeStruct((nseq * seq, D_SSD), BF16), st_shape, st_shape],
        scratch_shapes=[pltpu.VMEM((seq + 2 * SUBLANES, D_SSD), F32), pltpu.VMEM((seq + 2 * SUBLANES, 512), F32),
                        pltpu.VMEM((seq, D_SSD), F32), pltpu.VMEM((seq, 512), F32),
                        pltpu.VMEM((seq, LANES), F32), pltpu.VMEM((seq, LANES), F32),
                        pltpu.VMEM((seq, D_SSD), F32), pltpu.VMEM((seq, D_SSD), F32),
                        pltpu.VMEM((D_SSD, SSD_N), F32), pltpu.VMEM((D_SSD, SSD_N), F32)],
        compiler_params=_cparams(1),
        name=f"ssd{seq}",
    )(*args)


def _place_halves(tile, kv_in_high):
    lo = lax.broadcasted_iota(jnp.int32, tile.shape, 1) < ATT_HD
    swapped = pltpu.roll(tile, ATT_HD, 1)
    if kv_in_high:
        return jnp.where(lo, swapped, 0.0), jnp.where(lo, 0.0, tile)
    return jnp.where(lo, tile, 0.0), jnp.where(lo, 0.0, swapped)


def _place_rows(vt, kv_in_high):
    head = vt[ATT_HD:, :] if kv_in_high else vt[:ATT_HD, :]
    z = jnp.zeros_like(head)
    return jnp.concatenate([head, z], axis=0), jnp.concatenate([z, head], axis=0)


LOG2E = 1.4426950408889634
SCORE_SCALE = ATT_SCALE * LOG2E


def _sink_attend_t(score_parts, value_parts, sink2):
    m = sink2
    for s in score_parts:
        m = jnp.maximum(m, jnp.max(s, axis=0, keepdims=True))
    denom = jnp.exp2(sink2 - m)
    out = None
    for s, v in zip(score_parts, value_parts):
        p = jnp.exp2(s - m)
        denom = denom + jnp.sum(p, axis=0, keepdims=True)
        o = _mm(v, p)
        out = o if out is None else out + o
    return out * (1.0 / denom)


def _attn_schedule(n, scores, attend):
    scores(0)
    for j in range(n):
        if j + 1 < n:
            scores(j + 1)
        attend(j)


def _ctx_attn_kernel(q_ref, k_ref, v_ref, sink_ref, o_ref, kt_ref, vt_ref, s_a, s_b):
    sink2 = sink_ref[...] * LOG2E
    bufs = (s_a, s_b)
    half = SEQ // 2
    for t in range(ATT_KV_DIM // LANES):
        cols = slice(t * LANES, (t + 1) * LANES)
        kt_ref[cols, :] = jnp.transpose(k_ref[:, cols].astype(F32))
        vt_ref[cols, :] = jnp.transpose(v_ref[:, cols].astype(F32))

    def kv_tile(ref, j):
        return ref[:, (j // 2) * LANES:(j // 2 + 1) * LANES].astype(F32), (j % 2 == 1)

    def scores(j):
        k_lo, k_hi = _place_halves(*kv_tile(k_ref, j))
        qst = jnp.concatenate([q_ref[:, qt * LANES:(qt + 1) * LANES] for qt in (2 * j, 2 * j + 1)], axis=0)
        bufs[j % 2][...] = _mm_nt(jnp.concatenate([k_lo, k_hi], axis=0), qst) * SCORE_SCALE

    def attend(j):
        src = bufs[j % 2]
        v, high = kv_tile(v_ref, j)
        vts = _place_rows(jnp.transpose(v), high)
        for ql in range(2):
            qt = 2 * j + ql
            for qh in range(2):
                cols = slice(ql * SEQ + qh * half, ql * SEQ + (qh + 1) * half)
                acc = None
                for s, vv in enumerate(vts):
                    o = _sink_attend_t([src[s * SEQ:(s + 1) * SEQ, cols]], [vv], sink2[:, 2 * qt + s:2 * qt + s + 1])
                    acc = o if acc is None else acc + o
                o_ref[qh * half:(qh + 1) * half, qt * LANES:(qt + 1) * LANES] = jnp.transpose(acc).astype(o_ref.dtype)

    _attn_schedule(ATT_KV, scores, attend)


def _ctx_attn(p0, sink):
    def cols(width, start):
        return pl.BlockSpec((SEQ, width), lambda b: (b, start // width))

    sbuf = pltpu.VMEM((2 * SEQ, 2 * SEQ), F32)
    return pl.pallas_call(
        _ctx_attn_kernel,
        grid=(BATCH,),
        in_specs=[cols(D, P0_Q), cols(ATT_KV_DIM, P0_K), cols(ATT_KV_DIM, P0_V),
                  pl.BlockSpec((1, LANES), lambda b: (0, 0))],
        out_specs=[pl.BlockSpec((SEQ, D), lambda b: (b, 0))]
        + [pl.BlockSpec((None, ATT_KV_DIM, SEQ), lambda b: (b, 0, 0))] * 2,
        out_shape=[jax.ShapeDtypeStruct((T_CTX, D), BF16)] + [jax.ShapeDtypeStruct((BATCH, ATT_KV_DIM, SEQ), F32)] * 2,
        scratch_shapes=[sbuf, sbuf],
        compiler_params=_cparams(1),
        name="ctx_attn",
    )(p0, p0, p0, sink)


def _rope_tables():
    quarter = ATT_HD // 4
    t = np.arange(DEC_SEQ)
    lane = np.arange(LANES)
    inv = ROPE_BASE ** (-(lane % quarter).astype(np.float64) / quarter)
    pos = np.where(((lane % ATT_HD) < ATT_HD // 2)[None, :], (t // GRID_W)[:, None], (t % GRID_W)[:, None])
    ang = pos * inv[None, :]
    first = ((lane % (2 * quarter)) < quarter)[None, :]
    cos, sin = np.cos(ang), np.sin(ang)
    return (jnp.asarray(cos, F32), jnp.asarray(np.where(first, -sin, 0.0), F32),
            jnp.asarray(np.where(first, 0.0, sin), F32))


def _rope(x, cos, sa, sb):
    quarter = ATT_HD // 4
    return x * cos + pltpu.roll(x, LANES - quarter, 1) * sa + pltpu.roll(x, quarter, 1) * sb


def _lat_attn_kernel(q_ref, kp_ref, kc_ref, kn_ref, vp_ref, vc_ref, vn_ref, ck_ref, cv_ref,
                     cos_ref, sa_ref, sb_ref, sink_ref, o_ref, c_a, c_b, w_a, w_b):
    blk = pl.program_id(1)
    nb = pl.num_programs(1)
    B = ATT_BLOCK
    sink2 = sink_ref[...] * LOG2E
    cbufs, wbufs = (c_a, c_b), (w_a, w_b)

    def tables(b):
        r0 = pl.multiple_of(b * B, B)
        return cos_ref[pl.ds(r0, B), :], sa_ref[pl.ds(r0, B), :], sb_ref[pl.ds(r0, B), :]

    tq = tables(blk)
    tk = [tables(jnp.maximum(blk - 1, 0)), tq, tables(jnp.minimum(blk + 1, nb - 1))]
    kabs = (blk - 1) * B + lax.broadcasted_iota(jnp.int32, (3 * B, B), 0)
    qpos = blk * B + lax.broadcasted_iota(jnp.int32, (3 * B, B), 1)
    ok = (jnp.abs(qpos - kabs) <= WINDOW) & (kabs >= 0) & (kabs < nb * B)
    ok = jnp.concatenate([ok, ok], axis=1)
    ok = jnp.concatenate([ok, ok], axis=0)

    def scores(j):
        high = (j % 2 == 1)
        sl = slice((j // 2) * LANES, (j // 2 + 1) * LANES)
        kw = jnp.concatenate([_rope(r[:, sl].astype(F32), *tb) for r, tb in zip((kp_ref, kc_ref, kn_ref), tk)], axis=0)
        qs = [q_ref[:, qt * LANES:(qt + 1) * LANES].astype(F32) for qt in (2 * j, 2 * j + 1)]
        q_plain = jnp.concatenate(qs, axis=0)
        q_rope = jnp.concatenate([_rope(q, *tq) for q in qs], axis=0)
        ck = jnp.transpose(ck_ref[sl, :])
        cbufs[j % 2][...] = _mm_nt(jnp.concatenate(_place_halves(ck, high), axis=0), q_plain) * SCORE_SCALE
        win = _mm_nt(jnp.concatenate(_place_halves(kw, high), axis=0), q_rope) * SCORE_SCALE
        wbufs[j % 2][...] = jnp.where(ok, win, -jnp.inf)

    def attend(j):
        high = (j % 2 == 1)
        sl = slice((j // 2) * LANES, (j // 2 + 1) * LANES)
        vw = jnp.concatenate([jnp.transpose(r[:, sl].astype(F32)) for r in (vp_ref, vc_ref, vn_ref)], axis=1)
        vts = _place_rows(vw, high)
        cvts = _place_rows(cv_ref[sl, :], high)
        csrc, wsrc = cbufs[j % 2], wbufs[j % 2]
        for ql in range(2):
            qt = 2 * j + ql
            cols = slice(ql * B, (ql + 1) * B)
            acc = None
            for s in range(2):
                parts = [csrc[s * PAST_LEN:(s + 1) * PAST_LEN, cols], wsrc[s * 3 * B:(s + 1) * 3 * B, cols]]
                o = _sink_attend_t(parts, [cvts[s], vts[s]], sink2[:, 2 * qt + s:2 * qt + s + 1])
                acc = o if acc is None else acc + o
            o_ref[:, qt * LANES:(qt + 1) * LANES] = jnp.transpose(acc).astype(o_ref.dtype)

    _attn_schedule(ATT_KV, scores, attend)


def _lat_attn(p0, ck, cv, sink):
    nb = DEC_SEQ // ATT_BLOCK
    base = T_CTX // ATT_BLOCK

    def kv(start, shift):
        return pl.BlockSpec((ATT_BLOCK, ATT_KV_DIM),
                            lambda b, i: (base + b * nb + jnp.clip(i + shift, 0, nb - 1), start // ATT_KV_DIM))

    def full(shape):
        return pl.BlockSpec(shape, lambda b, i: (0,) * len(shape))

    cache = pl.BlockSpec((None, ATT_KV_DIM, PAST_LEN), lambda b, i: (b, 0, 0))
    cos, sa, sb = _rope_tables()
    return pl.pallas_call(
        _lat_attn_kernel,
        grid=(DEC_BATCH, nb),
        in_specs=[pl.BlockSpec((ATT_BLOCK, D), lambda b, i: (base + b * nb + i, P0_Q // D)),
                  kv(P0_K, -1), kv(P0_K, 0), kv(P0_K, 1), kv(P0_V, -1), kv(P0_V, 0), kv(P0_V, 1),
                  cache, cache, full((DEC_SEQ, LANES)), full((DEC_SEQ, LANES)), full((DEC_SEQ, LANES)),
                  full((1, LANES))],
        out_specs=pl.BlockSpec((ATT_BLOCK, D), lambda b, i: (b * nb + i, 0)),
        out_shape=jax.ShapeDtypeStruct((T_LAT, D), BF16),
        scratch_shapes=[pltpu.VMEM((2 * PAST_LEN, 2 * ATT_BLOCK), F32)] * 2
        + [pltpu.VMEM((2 * 3 * ATT_BLOCK, 2 * ATT_BLOCK), F32)] * 2,
        compiler_params=_cparams(2),
        name="lat_attn",
    )(p0, p0, p0, p0, p0, p0, p0, ck, cv, cos, sa, sb, sink)


def _log_sigmoid(x):
    return jnp.minimum(x, 0.0) - jnp.log(1.0 + jnp.exp(-jnp.abs(x)))


def _gla_kernel(*refs, seq, has_s0):
    if has_s0:
        (q_ref, k_ref, v_ref, r_ref, lr_ref, w2f_ref, w2b_ref, bf_ref, bb_ref, nw_ref, s0f_ref, s0b_ref,
         y_ref, sf_ref, sb_ref, gf, gb, yf, yb, stf, stb) = refs
    else:
        (q_ref, k_ref, v_ref, r_ref, lr_ref, w2f_ref, w2b_ref, bf_ref, bb_ref, nw_ref,
         y_ref, sf_ref, sb_ref, gf, gb, yf, yb, stf, stb) = refs
    C = GLA_C
    nc = seq // C
    lr = lr_ref[...]
    gf[...] = _log_sigmoid(_mm(lr, w2f_ref[...]) + bf_ref[...]) / GLA_GATE_NORM
    gb[...] = _log_sigmoid(_mm(lr, w2b_ref[...]) + bb_ref[...]) / GLA_GATE_NORM
    for h in range(GLA_HEADS):
        rows = slice(h * GLA_DV, (h + 1) * GLA_DV)
        if has_s0:
            stf[rows, :] = jnp.transpose(s0f_ref[h * GLA_DK:(h + 1) * GLA_DK, :])
            stb[rows, :] = jnp.transpose(s0b_ref[h * GLA_DK:(h + 1) * GLA_DK, :])
        else:
            stf[rows, :] = jnp.zeros((GLA_DV, GLA_DK), F32)
            stb[rows, :] = jnp.zeros((GLA_DV, GLA_DK), F32)

    row = lax.broadcasted_iota(jnp.int32, (C, C), 0)
    col = lax.broadcasted_iota(jnp.int32, (C, C), 1)
    qscale = GLA_DK ** -0.5

    def chunk(c, fwd):
        g_scr, y_scr, st = (gf, yf, stf) if fwd else (gb, yb, stb)
        r0 = pl.multiple_of(c * C, C)
        g = g_scr[pl.ds(r0, C), :]
        cs = _cumsum_rows(g, C)
        total = cs[C - 1:C, :]
        q = q_ref[pl.ds(r0, C), :].astype(F32) * qscale
        k = k_ref[pl.ds(r0, C), :].astype(F32)
        v = v_ref[pl.ds(r0, C), :]
        if fwd:
            qs, ks, ke = q * jnp.exp(cs), k * jnp.exp(-cs), k * jnp.exp(total - cs)
            keep = col <= row
        else:
            ex = cs - g
            qs, ks, ke = q * jnp.exp(total - ex), k * jnp.exp(ex - total), k * jnp.exp(ex)
            keep = col >= row
        dec = jnp.exp(total)
        for h in range(GLA_HEADS):
            kc = slice(h * GLA_DK, (h + 1) * GLA_DK)
            vc = slice(h * GLA_DV, (h + 1) * GLA_DV)
            s_t = st[vc, :]
            att = jnp.where(keep, _mm_nt(qs[:, kc], ks[:, kc]), 0.0)
            y_scr[pl.ds(r0, C), vc] = _mm(att, v[:, vc]) + _mm_nt(qs[:, kc], s_t)
            st[vc, :] = dec[:, kc] * s_t + _mm_tn(v[:, vc], ke[:, kc])

    def body(i, carry):
        chunk(i, True)
        chunk(nc - 1 - i, False)
        return carry

    lax.fori_loop(0, nc, body, 0, unroll=4)

    nw = nw_ref[...]
    for blk in range(seq // LANES):
        rs = slice(blk * LANES, (blk + 1) * LANES)
        y = yf[rs, :] + yb[rs, :]
        gate = _silu(r_ref[rs, :].astype(F32))
        for h in range(GLA_HEADS):
            vc = slice(h * GLA_DV, (h + 1) * GLA_DV)
            y_ref[rs, vc] = (_rms(y[:, vc], nw) * gate[:, vc]).astype(y_ref.dtype)
    for h in range(GLA_HEADS):
        rows = slice(h * GLA_DV, (h + 1) * GLA_DV)
        sf_ref[h * GLA_DK:(h + 1) * GLA_DK, :] = jnp.transpose(stf[rows, :])
        sb_ref[h * GLA_DK:(h + 1) * GLA_DK, :] = jnp.transpose(stb[rows, :])


def _gla(p1, p1lr, tok0, nseq, seq, w2f, w2b, bgf, bgb, nw, s0f=None, s0b=None):
    has_s0 = s0f is not None
    b0 = tok0 // seq
    dk_all, dv_all = GLA_HEADS * GLA_DK, GLA_HEADS * GLA_DV

    def cols(width, start):
        return pl.BlockSpec((seq, width), lambda s: (b0 + s, start // width))

    def full(shape):
        return pl.BlockSpec(shape, lambda s: (0,) * len(shape))

    in_specs = [cols(dk_all, P1_Q), cols(dk_all, P1_K), cols(dv_all, P1_V), cols(dv_all, P1_R), cols(LANES, 0),
                full((LANES, dk_all)), full((LANES, dk_all)), full((1, dk_all)), full((1, dk_all)), full((1, GLA_DV))]
    args = [p1, p1, p1, p1, p1lr, w2f, w2b, bgf.reshape(1, -1), bgb.reshape(1, -1), nw.reshape(1, -1)]
    st_spec = pl.BlockSpec((None, dk_all, GLA_DV), lambda s: (s, 0, 0))
    if has_s0:
        in_specs += [st_spec, st_spec]
        args += [s0f, s0b]
    st_shape = jax.ShapeDtypeStruct((nseq, dk_all, GLA_DV), F32)
    return pl.pallas_call(
        functools.partial(_gla_kernel, seq=seq, has_s0=has_s0),
        grid=(nseq,), in_specs=in_specs,
        out_specs=[pl.BlockSpec((seq, dv_all), lambda s: (s, 0)), st_spec, st_spec],
        out_shape=[jax.ShapeDtypeStruct((nseq * seq, dv_all), BF16), st_shape, st_shape],
        scratch_shapes=[pltpu.VMEM((seq, dk_all), F32), pltpu.VMEM((seq, dk_all), F32),
                        pltpu.VMEM((seq, dv_all), F32), pltpu.VMEM((seq, dv_all), F32),
                        pltpu.VMEM((dv_all, GLA_DK), F32), pltpu.VMEM((dv_all, GLA_DK), F32)],
        compiler_params=_cparams(1),
        name=f"gla{seq}",
    )(*args)


ROUTE_TM = 512
ROUTE_SUB = 256
ROUTE_ROWS = 32


def _split_bf16(x):
    hi = x.astype(BF16)
    return hi, (x - hi.astype(F32)).astype(BF16)


def _outproj_kernel(*refs, n_in, dual_x):
    y_refs = refs[:2 * n_in]
    n_x = 2 if dual_x else 1
    x_refs = refs[2 * n_in + 1:2 * n_in + 1 + n_x]
    w_ref = refs[2 * n_in]
    (g1_ref, sh_ref, sc_ref, nw_ref, wr_ref, br_ref,
     xo_ref, tr_ref, route_ref, cnt_ref, w_scr, wr_hl, carry, o_a, o_b) = refs[2 * n_in + 1 + n_x:]
    s = pl.program_id(0)
    n_ctx = T_CTX // ROUTE_TM
    has_mm = s < pl.num_programs(0) - 1
    has_ep = s >= 1

    @pl.when(s == 0)
    def _():
        w_scr[...] = w_ref[...].astype(BF16)
        hi, lo = _split_bf16(jnp.transpose(wr_ref[...]))
        wr_hl[0:LANES, :] = hi
        wr_hl[LANES:2 * LANES, :] = lo
        carry[...] = jnp.zeros(carry.shape, F32)

    def project(dst):
        o = None
        for i in range(n_in):
            y = jnp.where(s < n_ctx, y_refs[2 * i][...], y_refs[2 * i + 1][...])
            d = jnp.dot(y, w_scr[i * D:(i + 1) * D, :], preferred_element_type=F32)
            o = d if o is None else o + d
        dst[...] = o

    def finish(src):
        for sub in range(ROUTE_TM // ROUTE_SUB):
            _outproj_subtile(sub, s - 1 < n_ctx, src, x_refs, dual_x, g1_ref, sh_ref, sc_ref, nw_ref, br_ref,
                             xo_ref, tr_ref, route_ref, cnt_ref, wr_hl, carry)

    for par, (cur, prev) in enumerate(((o_a, o_b), (o_b, o_a))):
        mine = (s % 2) == par

        @pl.when(mine & has_mm & has_ep)
        def _():
            project(cur)
            finish(prev)

        @pl.when(mine & has_mm & jnp.logical_not(has_ep))
        def _():
            project(cur)

        @pl.when(mine & jnp.logical_not(has_mm) & has_ep)
        def _():
            finish(prev)


def _outproj_subtile(sub, is_ctx, o_scr, x_refs, dual_x, g1_ref, sh_ref, sc_ref, nw_ref, br_ref,
                     xo_ref, tr_ref, route_ref, cnt_ref, wr_hl, carry):
    tm = ROUTE_SUB
    rows = slice(sub * tm, (sub + 1) * tm)
    x_in = jnp.where(is_ctx, x_refs[0][rows, :], x_refs[1][rows, :]) if dual_x else x_refs[0][rows, :]
    x = x_in + g1_ref[0] * o_scr[rows, :]
    xo_ref[rows, :] = x
    t = _rms(x, nw_ref[...]) * (1.0 + sc_ref[0]) + sh_ref[0]
    for k in range(D // LANES):
        tr_ref[pl.ds(sub * tm * SUBLANES + k, tm, stride=SUBLANES), :] = t[:, k * LANES:(k + 1) * LANES]

    t_hi, t_lo = _split_bf16(t)
    lg = _mm_nt(wr_hl[...], t_hi)
    nr = ROUTE_ROWS
    logit = lg[0:nr, :] + lg[LANES:LANES + nr, :] + _mm_nt(wr_hl[0:LANES, :], t_lo)[0:nr, :] + br_ref[0:nr, :]
    rowf = lax.broadcasted_iota(jnp.int32, (nr, tm), 0).astype(F32)
    neg = -jnp.inf

    def first_argmax(v, vmax):
        return jnp.min(jnp.where(v == vmax, rowf, float(LANES)), axis=0, keepdims=True)

    gl = jnp.where(rowf < N_GROUPS, logit, neg)
    gmax = jnp.max(gl, axis=0, keepdims=True)
    gsel = first_argmax(gl, gmax)
    gprob = 1.0 / jnp.sum(jnp.exp(gl - gmax), axis=0, keepdims=True)
    first = N_GROUPS + EXP_PER_GROUP * gsel
    el = jnp.where((rowf >= first) & (rowf < first + EXP_PER_GROUP), logit, neg)
    m1 = jnp.max(el, axis=0, keepdims=True)
    i1 = first_argmax(el, m1)
    el2 = jnp.where(rowf == i1, neg, el)
    m2 = jnp.max(el2, axis=0, keepdims=True)
    i2 = first_argmax(el2, m2)
    e2 = jnp.exp(m2 - m1)
    c1 = gprob / (1.0 + e2)
    c2 = gprob * e2 / (1.0 + e2)
    x1 = i1 - N_GROUPS
    x2 = i2 - N_GROUPS

    erow = rowf
    hot = ((erow == x1) | (erow == x2)).astype(F32)
    tri = (lax.broadcasted_iota(jnp.int32, (tm, tm), 0) < lax.broadcasted_iota(jnp.int32, (tm, tm), 1))
    before = _mm(hot, tri.astype(F32)) + carry[...]
    r1 = jnp.sum(jnp.where(erow == x1, before, 0.0), axis=0, keepdims=True)
    r2 = jnp.sum(jnp.where(erow == x2, before, 0.0), axis=0, keepdims=True)
    total = carry[...] + _mm(hot, jnp.ones((tm, tm), F32))
    carry[...] = total
    cnt_ref[...] = total[0:N_EXPERTS, 0:LANES]
    row8 = lax.broadcasted_iota(jnp.int32, (SUBLANES, tm), 0)
    out = jnp.zeros((SUBLANES, tm), F32)
    for k, v in enumerate((x1, x2, c1, c2, r1, r2)):
        out = jnp.where(row8 == k, jnp.broadcast_to(v, (SUBLANES, tm)), out)
    route_ref[:, rows] = out


def _outproj_route(ys, w_out, xs, mods, layer, norm_w, w_router, b_router):
    tm = ROUTE_TM
    n_in = len(ys) // 2
    dual_x = len(xs) == 2
    kdim = w_out.shape[0]

    def full(shape):
        return pl.BlockSpec(shape, lambda i: (0,) * len(shape))

    n_t, n_ctx = T // tm, T_CTX // tm

    def cur(s):
        return jnp.minimum(s, n_t - 1)

    def prv(s):
        return jnp.maximum(s - 1, 0)

    def pair(tile_of):
        return [pl.BlockSpec((tm, D), lambda s: (jnp.minimum(tile_of(s), n_ctx - 1), 0)),
                pl.BlockSpec((tm, D), lambda s: (jnp.maximum(tile_of(s) - n_ctx, 0), 0))]

    tile = pl.BlockSpec((tm, D), lambda s: (prv(s), 0))
    in_specs = (pair(cur) * n_in + [full((kdim, D))] + (pair(prv) if dual_x else [tile])
                + [_mod_spec(layer, 2, tm, prv), _mod_spec(layer, 3, tm, prv), _mod_spec(layer, 4, tm, prv),
                   full((1, D)), full((D, LANES)), full((LANES, ROUTE_SUB))])
    cnt = jax.ShapeDtypeStruct((N_EXPERTS, LANES), F32)
    oscr = pltpu.VMEM((tm, D), F32)
    return pl.pallas_call(
        functools.partial(_outproj_kernel, n_in=n_in, dual_x=dual_x),
        grid=(n_t + 1,), in_specs=in_specs,
        out_specs=[tile, pl.BlockSpec((tm * SUBLANES, LANES), lambda s: (prv(s), 0)),
                   pl.BlockSpec((SUBLANES, tm), lambda s: (0, prv(s))), full(cnt.shape)],
        out_shape=[jax.ShapeDtypeStruct((T, D), F32), jax.ShapeDtypeStruct((T * SUBLANES, LANES), F32),
                   jax.ShapeDtypeStruct((SUBLANES, T), F32), cnt],
        scratch_shapes=[pltpu.VMEM((kdim, D), BF16), pltpu.VMEM((2 * LANES, D), BF16),
                        pltpu.VMEM((ROUTE_ROWS, ROUTE_SUB), F32), oscr, oscr],
        compiler_params=_cparams(1),
        name=f"outproj{layer}",
    )(*ys, w_out, *xs, mods, mods, mods, norm_w.reshape(1, D), w_router,
      jnp.broadcast_to(b_router.reshape(LANES, 1), (LANES, ROUTE_SUB)))


def _moe_meta(counts):
    tm = MOE_TM
    experts = jnp.arange(N_EXPERTS, dtype=jnp.int32)
    counts = jnp.max(counts, axis=1).astype(jnp.int32)
    padded = ((counts + tm - 1) // tm) * tm
    ends = jnp.cumsum(padded)
    tile_start = jnp.arange(MOE_TILES, dtype=jnp.int32) * tm
    te = jnp.sum((tile_start[:, None] >= ends[None, :]).astype(jnp.int32), axis=1)
    last = jnp.max(jnp.where(counts > 0, experts, 0))
    meta = jnp.concatenate([jnp.minimum(te, last), ends[-1:] // tm]).astype(jnp.int32)
    starts = ends - padded
    pads = jnp.concatenate([starts + counts, ends[-1:], ends, jnp.full((1,), MOE_ROWS)]).astype(jnp.int32)
    return starts.astype(jnp.int32), pads, meta


def _expert_changed(meta_ref, j):
    return (j == 0) | (meta_ref[j] != meta_ref[jnp.maximum(j - 1, 0)])


def _moe_up_kernel(pos1_ref, pos2_ref, pads_ref, meta_ref, tr_hbm, wg_ref, wu_ref, a_ref, rowmap_ref,
                   tr_scr, g0, g1, wg_bf, wu_bf, sem):
    j = pl.program_id(0)
    tm = MOE_TM
    ntiles = meta_ref[MOE_TILES]

    def gather(tile, dst):
        for mi in range(tm):
            tok = jnp.minimum(rowmap_ref[tile * tm + mi] >> 1, T - 1)
            dst[mi * SUBLANES:(mi + 1) * SUBLANES, :] = tr_scr[pl.ds(pl.multiple_of(tok * SUBLANES, SUBLANES), SUBLANES), :]

    @pl.when(j == 0)
    def _():
        load = pltpu.make_async_copy(tr_hbm, tr_scr, sem)
        load.start()

        def clear(c, carry):
            for i in range(SUBLANES):
                rowmap_ref[c * SUBLANES + i] = 2 * T
            return carry
        for k in range(N_EXPERTS + 1):
            lax.fori_loop(pads_ref[k] // SUBLANES, pads_ref[N_EXPERTS + 1 + k] // SUBLANES, clear, 0)

        def place(t, carry):
            rowmap_ref[pos1_ref[t]] = 2 * t
            rowmap_ref[pos2_ref[t]] = 2 * t + 1
            return carry
        lax.fori_loop(0, T, place, 0, unroll=8)
        load.wait()
        gather(0, g0)

    def compute(cur, nxt):
        gather(jnp.minimum(j + 1, ntiles - 1), nxt)
        x = _tokmajor_to_std(cur, tm).astype(BF16)
        g = jnp.dot(x, wg_bf[...], preferred_element_type=F32)
        u = jnp.dot(x, wu_bf[...], preferred_element_type=F32)
        a_ref[...] = (_silu(g) * u).astype(a_ref.dtype)

    @pl.when(j < ntiles)
    def _():
        @pl.when(_expert_changed(meta_ref, j))
        def _():
            wg_bf[...] = wg_ref[...].astype(BF16)
            wu_bf[...] = wu_ref[...].astype(BF16)

        pl.when(j % 2 == 0)(functools.partial(compute, g0, g1))
        pl.when(j % 2 == 1)(functools.partial(compute, g1, g0))

    @pl.when(j >= ntiles)
    def _():
        a_ref[...] = jnp.zeros(a_ref.shape, a_ref.dtype)


def _moe_down_kernel(rowmap_ref, cpair_ref, meta_ref, a_ref, wd_ref, *rest, layer, final):
    if final:
        x_hbm, mods_ref, fw_ref, out_c, out_l, acc, y0, y1, wd_bf, xin, xout, io_sems = rest
    else:
        out_hbm, acc, y0, y1, wd_bf = rest
    j = pl.program_id(0)
    tm = MOE_TM
    zrows = 512
    ntiles = meta_ref[MOE_TILES]

    @pl.when(j == 0)
    def _():
        def zero(i, carry):
            acc[pl.ds(pl.multiple_of(i * zrows, zrows), zrows), :] = jnp.zeros((zrows, LANES), F32)
            return carry
        lax.fori_loop(0, ACC_ROWS // zrows, zero, 0)

    def matmul(dst):
        y = jnp.dot(a_ref[...], wd_bf[...], preferred_element_type=F32)
        for k in range(D // LANES):
            dst[pl.ds(k, tm, stride=SUBLANES), :] = y[:, k * LANES:(k + 1) * LANES]

    def scatter(tile, src):
        for b in range(tm // SUBLANES):
            ents = [rowmap_ref[tile * tm + b * SUBLANES + i] for i in range(SUBLANES)]
            offs = [pl.multiple_of((e >> 1) * SUBLANES, SUBLANES) for e in ents]
            olds = [acc[pl.ds(o, SUBLANES), :] for o in offs]
            for i, o in enumerate(offs):
                r = (b * SUBLANES + i) * SUBLANES
                acc[pl.ds(o, SUBLANES), :] = olds[i] + cpair_ref[ents[i]] * src[r:r + SUBLANES, :]

    has_mm = j < ntiles
    has_sc = (j >= 1) & (j <= ntiles)

    @pl.when(has_mm)
    def _():
        @pl.when(_expert_changed(meta_ref, j))
        def _():
            wd_bf[...] = wd_ref[...].astype(BF16)

    for par, (cur, prev) in enumerate(((y0, y1), (y1, y0))):
        mine = (j % 2) == par

        @pl.when(mine & has_mm & has_sc)
        def _():
            matmul(cur)
            scatter(j - 1, prev)

        @pl.when(mine & has_mm & jnp.logical_not(has_sc))
        def _():
            matmul(cur)

        @pl.when(mine & jnp.logical_not(has_mm) & has_sc)
        def _():
            scatter(j - 1, prev)

    @pl.when(j == pl.num_programs(0) - 1)
    def _():
        if not final:
            pltpu.sync_copy(acc.at[0:T * SUBLANES, :], out_hbm)
            return
        ft = MERGE_TM
        n_t = T // ft
        n_c = T_CTX // ft

        def fetch(i):
            return pltpu.make_async_copy(x_hbm.at[i * ft:(i + 1) * ft, :], xin.at[i % 2], io_sems.at[i % 2])

        def flush(i):
            dst = out_c.at[i * ft:(i + 1) * ft, :] if i < n_c else out_l.at[(i - n_c) * ft:(i - n_c + 1) * ft, :]
            return pltpu.make_async_copy(xout.at[i % 2], dst, io_sems.at[2 + i % 2])

        fetch(0).start()
        for i in range(n_t):
            if i + 1 < n_t:
                fetch(i + 1).start()
            fetch(i).wait()
            moe = jnp.concatenate([acc[pl.ds(i * ft * SUBLANES + k, ft, stride=SUBLANES), :]
                                   for k in range(D // LANES)], axis=1)
            row = 0 if i * ft < T_CTX else 1 + (i * ft - T_CTX) // DEC_SEQ
            x = xin[i % 2] + mods_ref[(layer * SUBLANES + row) * 6 + 5] * moe
            if i >= 2:
                flush(i - 2).wait()
            xout[i % 2] = _rms(x, fw_ref[...])
            flush(i).start()
        for i in range(max(n_t - 2, 0), n_t):
            flush(i).wait()


def _moe(tr, route_t, counts, layer, w_gate, w_up, w_down, xmid=None, mods=None, final_w=None):
    tm = MOE_TM
    starts, pads, meta = _moe_meta(counts)
    experts = jnp.arange(N_EXPERTS, dtype=jnp.int32)

    def position(e, r):
        sel = e.astype(jnp.int32)[:, None] == experts[None, :]
        return jnp.sum(jnp.where(sel, starts[None, :], 0), axis=1) + r.astype(jnp.int32)

    pos1 = position(route_t[0], route_t[4])
    pos2 = position(route_t[1], route_t[5])
    cpair = jnp.concatenate([jnp.stack([route_t[2], route_t[3]], axis=1).reshape(2 * T), jnp.zeros((2,), F32)])

    def wspec(shape, n):
        return pl.BlockSpec((None, None) + shape,
                            lambda j, *pre: (layer, pre[n - 1][jnp.minimum(j, MOE_TILES - 1)], 0, 0))

    gscr = pltpu.VMEM((tm * SUBLANES, LANES), F32)
    act, rowmap = pl.pallas_call(
        _moe_up_kernel,
        grid_spec=pltpu.PrefetchScalarGridSpec(
            num_scalar_prefetch=4, grid=(MOE_TILES,),
            in_specs=[pl.BlockSpec(memory_space=pl.ANY), wspec((D, D_EXPERT), 4), wspec((D, D_EXPERT), 4)],
            out_specs=[pl.BlockSpec((tm, D_EXPERT), lambda j, *pre: (j, 0)), pl.BlockSpec(memory_space=pltpu.SMEM)],
            scratch_shapes=[pltpu.VMEM((T * SUBLANES, LANES), F32), gscr, gscr,
                            pltpu.VMEM((D, D_EXPERT), BF16), pltpu.VMEM((D, D_EXPERT), BF16),
                            pltpu.SemaphoreType.DMA(())]),
        out_shape=[jax.ShapeDtypeStruct((MOE_ROWS, D_EXPERT), BF16), jax.ShapeDtypeStruct((MOE_ROWS,), jnp.int32)],
        compiler_params=_cparams(1),
        name=f"moe_up{layer}",
    )(pos1, pos2, pads, meta, tr, w_gate, w_up)

    final = final_w is not None
    any_spec = pl.BlockSpec(memory_space=pl.ANY)
    in_specs = [pl.BlockSpec((tm, D_EXPERT), lambda j, *pre: (jnp.minimum(j, MOE_TILES - 1), 0)),
                wspec((D_EXPERT, D), 3)]
    args = [rowmap, cpair, meta, act, w_down]
    scratch = [pltpu.VMEM((ACC_ROWS, LANES), F32), gscr, gscr, pltpu.VMEM((D_EXPERT, D), BF16)]
    if final:
        stage = pltpu.VMEM((2, MERGE_TM, D), F32)
        in_specs += [any_spec, pl.BlockSpec(mods.shape, lambda j, *pre: (0, 0, 0)),
                     pl.BlockSpec((1, D), lambda j, *pre: (0, 0))]
        args += [xmid, mods, final_w.reshape(1, D)]
        scratch += [stage, stage, pltpu.SemaphoreType.DMA((4,))]
        out_shape = [jax.ShapeDtypeStruct((T_CTX, D), F32), jax.ShapeDtypeStruct((T_LAT, D), F32)]
    else:
        out_shape = [jax.ShapeDtypeStruct((T * SUBLANES, LANES), F32)]
    return pl.pallas_call(
        functools.partial(_moe_down_kernel, layer=layer, final=final),
        grid_spec=pltpu.PrefetchScalarGridSpec(
            num_scalar_prefetch=3, grid=(MOE_TILES + 1,), in_specs=in_specs,
            out_specs=[any_spec] * len(out_shape), scratch_shapes=scratch),
        out_shape=out_shape,
        compiler_params=_cparams(1),
        name=f"moe_down{layer}",
    )(*args)


def _pad_lanes(v):
    return jnp.pad(v.astype(F32), (0, LANES - v.shape[0])).reshape(1, LANES)


def kernel(x_prompt, x_sample, cache_k_attn, cache_v_attn, state_ssd_fwd, state_ssd_bwd, state_gla_fwd, state_gla_bwd, c, c_ctx, w_ada, b_ada, norm_mix_w, norm_ffn_w, w_in_even, conv_w, conv_b, dt_bias_fwd, dt_bias_bwd, a_log_fwd, a_log_bwd, d_skip, ssd_norm_w, attn_sink, w_out_even, w_in_odd, w_gk2_fwd, b_gk_fwd, w_gk2_bwd, b_gk_bwd, gla_norm_w, w_out_odd, w_router_group, b_router_group, w_router_expert, b_router_expert, w_gate_exp, w_up_exp, w_down_exp, final_norm_w):
    depth = w_ada.shape[0]
    assert depth == 2 and x_prompt.shape == (BATCH, SEQ, D) and x_sample.shape == (DEC_BATCH, DEC_SEQ, D)

    cond8 = jnp.concatenate([c_ctx[None, :], c, jnp.zeros((SUBLANES - 1 - DEC_BATCH, D), F32)], axis=0)
    mods = _adaln(cond8, w_ada, b_ada).reshape(depth * SUBLANES * 6, 1, D)
    xs0 = (x_prompt.reshape(T_CTX, D), x_sample.reshape(T_LAT, D))

    def router_params(i):
        wr = jnp.concatenate([w_router_group[i], w_router_expert[i],
                              jnp.zeros((D, LANES - N_GROUPS - N_EXPERTS), F32)], axis=1)
        return wr, _pad_lanes(jnp.concatenate([b_router_group[i], b_router_expert[i]]))

    n_zxbc = 2 * D_SSD + 2 * SSD_GROUPS * SSD_N
    n_dt = 2 * SSD_HEADS
    half = PROJ_TN // 2
    r_bc, r_q = 2 * D_SSD, n_zxbc + n_dt
    r_kv = r_q + D
    rows0 = [(0, half), (D_SSD, D_SSD + half), (r_q, r_q + half), (r_bc, r_kv)]
    p0, p0dt = _modproj(xs0, mods, 0, norm_mix_w[0], jnp.transpose(w_in_even[0]), rows0, n_zxbc)

    dtb = _pad_lanes(jnp.concatenate([dt_bias_fwd[0], dt_bias_bwd[0]]))
    alog = _pad_lanes(jnp.concatenate([a_log_fwd[0], a_log_bwd[0]]))
    dskip = jnp.repeat(d_skip[0], SSD_P).reshape(1, D_SSD)
    ssd_args = (conv_w[0], conv_b[0], dtb, alog, dskip, ssd_norm_w[0])
    y_ssd_c, ssd_f, ssd_b = _ssd(p0, p0dt, 0, BATCH, SEQ, *ssd_args)
    y_ssd_l, _, _ = _ssd(p0, p0dt, T_CTX, DEC_BATCH, DEC_SEQ, *ssd_args,
                         h0f=state_ssd_fwd[:, 0].reshape(DEC_BATCH, D_SSD, SSD_N),
                         h0b=state_ssd_bwd[:, 0].reshape(DEC_BATCH, D_SSD, SSD_N))
    sink = _pad_lanes(attn_sink[0])
    y_att_c, new_kt, new_vt = _ctx_attn(p0, sink)
    def cache_in(t):
        return jnp.transpose(t[:, 0], (0, 2, 3, 1)).reshape(DEC_BATCH, ATT_KV_DIM, PAST_LEN)

    y_att_l = _lat_attn(p0, cache_in(cache_k_attn), cache_in(cache_v_attn), sink)
    xmid0, tr0, route0, cnt0 = _outproj_route([y_ssd_c, y_ssd_l, y_att_c, y_att_l], w_out_even[0], xs0, mods, 0,
                                              norm_ffn_w[0], *router_params(0))
    (moe0,) = _moe(tr0, route0, cnt0, 0, w_gate_exp, w_up_exp, w_down_exp)

    dk_all = GLA_HEADS * GLA_DK
    n_qkvr = 2 * dk_all + 2 * GLA_HEADS * GLA_DV
    n_odd = w_in_odd.shape[2]
    p1, p1lr, x1 = _modproj(xmid0, mods, 1, norm_mix_w[1], jnp.transpose(w_in_odd[0]),
                            [(k * PROJ_TN, k * PROJ_TN + half) for k in range(n_qkvr // PROJ_TN)], n_odd - LANES,
                            moe=moe0)
    lr0 = LANES - 2 * GLA_LOWRANK
    w2f = jnp.zeros((LANES, dk_all), F32).at[lr0:lr0 + GLA_LOWRANK].set(w_gk2_fwd[0])
    w2b = jnp.zeros((LANES, dk_all), F32).at[lr0 + GLA_LOWRANK:].set(w_gk2_bwd[0])
    gla_args = (w2f, w2b, b_gk_fwd[0], b_gk_bwd[0], gla_norm_w[0])
    y_gla_c, gla_f, gla_b = _gla(p1, p1lr, 0, BATCH, SEQ, *gla_args)
    y_gla_l, _, _ = _gla(p1, p1lr, T_CTX, DEC_BATCH, DEC_SEQ, *gla_args,
                         s0f=state_gla_fwd[:, 0].reshape(DEC_BATCH, dk_all, GLA_DV),
                         s0b=state_gla_bwd[:, 0].reshape(DEC_BATCH, dk_all, GLA_DV))
    xmid1, tr1, route1, cnt1 = _outproj_route([y_gla_c, y_gla_l], w_out_odd[0], (x1,), mods, 1,
                                              norm_ffn_w[1], *router_params(1))
    y_c, y_l = _moe(tr1, route1, cnt1, 1, w_gate_exp, w_up_exp, w_down_exp, xmid1, mods, final_w=final_norm_w)

    y_prompt = y_c.reshape(BATCH, SEQ, D)
    y_sample = y_l.reshape(DEC_BATCH, DEC_SEQ, D)
    def cache_out(t):
        return jnp.transpose(t.reshape(BATCH, 1, ATT_KV, ATT_HD, SEQ), (0, 1, 4, 2, 3))

    new_k, new_v = cache_out(new_kt), cache_out(new_vt)
    return (y_prompt, y_sample, new_k, new_v,
            ssd_f.reshape(BATCH, 1, SSD_HEADS, SSD_P, SSD_N), ssd_b.reshape(BATCH, 1, SSD_HEADS, SSD_P, SSD_N),
            gla_f.reshape(BATCH, 1, GLA_HEADS, GLA_DK, GLA_DV), gla_b.reshape(BATCH, 1, GLA_HEADS, GLA_DK, GLA_DV))
```

```python
import functools

import numpy as np
import jax
import jax.numpy as jnp
from jax import lax
from jax.experimental import pallas as pl
from jax.experimental.pallas import tpu as pltpu

F32 = jnp.float32
BF16 = jnp.bfloat16

D = 1024
BATCH, SEQ = 16, 256
DEC_BATCH, DEC_SEQ = 2, 1024
PAST_LEN = 512
GRID_W = 64
EPS = 1e-6
T_CTX = BATCH * SEQ
T_LAT = DEC_BATCH * DEC_SEQ
T = T_CTX + T_LAT

SSD_HEADS, SSD_P, SSD_N, SSD_GROUPS = 16, 64, 128, 2
SSD_CONV = 5
SSD_L = 128
D_SSD = SSD_HEADS * SSD_P
HEADS_PER_GROUP = SSD_HEADS // SSD_GROUPS
GROUP_W = HEADS_PER_GROUP * SSD_P

ATT_HEADS, ATT_KV, ATT_HD = 16, 4, 64
ATT_KV_DIM = ATT_KV * ATT_HD
WINDOW = 128
ATT_BLOCK = 128
ATT_SCALE = ATT_HD ** -0.5
ROPE_BASE = 10000.0

GLA_HEADS, GLA_DK, GLA_DV = 4, 128, 256
GLA_C = 64
GLA_GATE_NORM = 16.0
GLA_LOWRANK = 16

N_GROUPS, EXP_PER_GROUP = 4, 4
N_EXPERTS = 16
D_EXPERT = 512

LANES = 128
SUBLANES = 8
VMEM_LIMIT = 56 * 1024 * 1024

P0_Z, P0_X, P0_Q, P0_BC, P0_K, P0_V = 0, 1024, 2048, 3072, 3584, 3840
BC_W = 2 * SSD_GROUPS * SSD_N
P1_Q, P1_K, P1_V, P1_R = 0, 512, 1024, 2048

MOE_TM = 512
MOE_TILES = (2 * T) // MOE_TM + N_EXPERTS
MOE_ROWS = MOE_TILES * MOE_TM
ACC_ROWS = (T + SUBLANES) * SUBLANES
MERGE_TM = 512


def _cparams(n_axes, vmem=VMEM_LIMIT):
    return pltpu.CompilerParams(dimension_semantics=("arbitrary",) * n_axes, vmem_limit_bytes=vmem)


def _silu(x):
    return x / (1.0 + jnp.exp(-x))


def _softplus(x):
    return jnp.maximum(x, 0.0) + jnp.log(1.0 + jnp.exp(-jnp.abs(x)))


def _mm(a, b):
    return jnp.dot(a.astype(BF16), b.astype(BF16), preferred_element_type=F32)


def _mm_nt(a, b):
    return lax.dot_general(a.astype(BF16), b.astype(BF16), (((1,), (1,)), ((), ())),
                           preferred_element_type=F32)


def _mm_tn(a, b):
    return lax.dot_general(a.astype(BF16), b.astype(BF16), (((0,), (0,)), ((), ())),
                           preferred_element_type=F32)


def _rms(x, w):
    return x * lax.rsqrt(jnp.mean(x * x, axis=-1, keepdims=True) + EPS) * w


def _cumsum_rows(x, n):
    row = lax.broadcasted_iota(jnp.int32, x.shape, 0)
    s = 1
    while s < n:
        x = x + jnp.where(row >= s, pltpu.roll(x, s, 0), 0.0)
        s *= 2
    return x


def _mod_row(tok0):
    return jnp.where(tok0 < T_CTX, 0, 1 + (tok0 - T_CTX) // DEC_SEQ)


ADA_TN = 1536


def _adaln_kernel(c_ref, w_ref, b_ref, o_ref):
    s = _silu(c_ref[...])
    o_ref[0] = _mm(s, w_ref[0]) + b_ref[0]


def _adaln(cond8, w_ada, b_ada):
    depth = w_ada.shape[0]
    return pl.pallas_call(
        _adaln_kernel,
        grid=(depth, 6 * D // ADA_TN),
        in_specs=[
            pl.BlockSpec((SUBLANES, D), lambda l, j: (0, 0)),
            pl.BlockSpec((1, D, ADA_TN), lambda l, j: (l, 0, j)),
            pl.BlockSpec((1, 1, ADA_TN), lambda l, j: (l, 0, j)),
        ],
        out_specs=pl.BlockSpec((1, SUBLANES, ADA_TN), lambda l, j: (l, 0, j)),
        out_shape=jax.ShapeDtypeStruct((depth, SUBLANES, 6 * D), F32),
        compiler_params=_cparams(2),
        name="adaln",
    )(cond8, w_ada, b_ada.reshape(depth, 1, 6 * D))


def _mod_spec(layer, chunk, tm, tile_of=lambda i, *_: i):
    return pl.BlockSpec((1, 1, D), lambda *g: ((layer * SUBLANES + _mod_row(tile_of(*g) * tm)) * 6 + chunk, 0, 0))


def _tokmajor_to_std(ref, tm):
    return jnp.concatenate([ref[pl.ds(k, tm, stride=SUBLANES), :] for k in range(D // LANES)], axis=1)


PROJ_TM = 1024
PROJ_TN = 1024


def _ctx_lat_specs(tm, width=D):
    n_ctx = T_CTX // tm
    return (pl.BlockSpec((tm, width), lambda i, *_: (jnp.minimum(i, n_ctx - 1), 0)),
            pl.BlockSpec((tm, width), lambda i, *_: (jnp.maximum(i - n_ctx, 0), 0)))


def _modproj_kernel(*refs, dual_x):
    it = iter(refs)
    if dual_x:
        xc_ref, xl_ref = next(it), next(it)
    else:
        x_ref, moe_ref, g2_ref = next(it), next(it), next(it)
    sh_ref, sc_ref, nw_ref, wlo_ref, whi_ref, ws_ref, o_ref, os_ref = (next(it) for _ in range(8))
    xo_ref = None if dual_x else next(it)
    h_all, w_bf = next(it), next(it)
    j, i = pl.program_id(0), pl.program_id(1)
    tm = PROJ_TM
    rows = pl.ds(pl.multiple_of(i * tm, tm), tm)

    @pl.when(j == 0)
    def _():
        if dual_x:
            x = jnp.where(i < T_CTX // tm, xc_ref[...], xl_ref[...])
        else:
            x = x_ref[...] + g2_ref[0] * _tokmajor_to_std(moe_ref, tm)
            xo_ref[...] = x
        h = (_rms(x, nw_ref[...]) * (1.0 + sc_ref[0]) + sh_ref[0]).astype(BF16)
        h_all[rows, :] = h
        os_ref[...] = _mm_nt(h, ws_ref[...])

    @pl.when(i == 0)
    def _():
        half = wlo_ref.shape[0]
        w_bf[0:half, :] = wlo_ref[...].astype(BF16)
        w_bf[half:2 * half, :] = whi_ref[...].astype(BF16)

    o_ref[...] = _mm_nt(h_all[rows, :], w_bf[...]).astype(o_ref.dtype)


def _modproj(xs, mods, layer, norm_w, wt, tile_rows, small_row, moe=None):
    tm, tn = PROJ_TM, PROJ_TN
    dual_x = moe is None
    n_tiles = len(tile_rows)
    n_i, n_ctx = T // tm, T_CTX // tm

    def w_row(side):
        def index(j, i):
            r = jnp.int32(tile_rows[0][side])
            for k in range(1, n_tiles):
                r = jnp.where(j == k, tile_rows[k][side], r)
            return pl.multiple_of(r, SUBLANES), 0
        return index

    def tok(j, i):
        return jnp.where(j == 0, i, n_i - 1)

    tile = pl.BlockSpec((tm, D), lambda j, i: (tok(j, i), 0))
    if dual_x:
        in_specs = [pl.BlockSpec((tm, D), lambda j, i: (jnp.minimum(tok(j, i), n_ctx - 1), 0)),
                    pl.BlockSpec((tm, D), lambda j, i: (jnp.maximum(tok(j, i) - n_ctx, 0), 0))]
        args = list(xs)
    else:
        in_specs = [tile, pl.BlockSpec((tm * SUBLANES, LANES), lambda j, i: (tok(j, i), 0)),
                    _mod_spec(layer - 1, 5, tm, tok)]
        args = [xs, moe, mods]
    in_specs += [_mod_spec(layer, 0, tm, tok), _mod_spec(layer, 1, tm, tok), pl.BlockSpec((1, D), lambda j, i: (0, 0)),
                 pl.BlockSpec((pl.Element(tn // 2), pl.Element(D)), w_row(0)),
                 pl.BlockSpec((pl.Element(tn // 2), pl.Element(D)), w_row(1)),
                 pl.BlockSpec((pl.Element(LANES), pl.Element(D)), lambda j, i: (small_row, 0))]
    args += [mods, mods, norm_w.reshape(1, D), wt, wt, wt]
    out_specs = [pl.BlockSpec((tm, tn), lambda j, i: (i, j)),
                 pl.BlockSpec((tm, LANES), lambda j, i: (tok(j, i), 0))]
    out_shape = [jax.ShapeDtypeStruct((T, n_tiles * tn), BF16), jax.ShapeDtypeStruct((T, LANES), F32)]
    if not dual_x:
        out_specs.append(tile)
        out_shape.append(jax.ShapeDtypeStruct((T, D), F32))
    return pl.pallas_call(
        functools.partial(_modproj_kernel, dual_x=dual_x),
        grid=(n_tiles, n_i), in_specs=in_specs, out_specs=out_specs, out_shape=out_shape,
        scratch_shapes=[pltpu.VMEM((T, D), BF16), pltpu.VMEM((tn, D), BF16)],
        compiler_params=_cparams(2),
        name=f"modproj{layer}",
    )(*args)


def _expand_heads(v, off):
    hi = (lax.broadcasted_iota(jnp.int32, (v.shape[0], LANES), 1) >= SSD_P).astype(jnp.int32)
    tiles = [jnp.take_along_axis(v, hi + (off + 2 * q), axis=1) for q in range(SSD_HEADS // 2)]
    return jnp.concatenate(tiles, axis=1)


def _ssd_kernel(*refs, seq, has_h0):
    if has_h0:
        (z_ref, x_ref, bc_ref, dt_ref, cwx_ref, cwbc_ref, cbx_ref, cbbc_ref, dtb_ref, alog_ref, dsk_ref,
         nw_ref, h0f_ref, h0b_ref, y_ref, sf_ref, sb_ref,
         xpad, bcpad, xc, bcc, a_scr, dt_scr, yf, yb, hf, hb) = refs
    else:
        (z_ref, x_ref, bc_ref, dt_ref, cwx_ref, cwbc_ref, cbx_ref, cbbc_ref, dtb_ref, alog_ref, dsk_ref,
         nw_ref, y_ref, sf_ref, sb_ref,
         xpad, bcpad, xc, bcc, a_scr, dt_scr, yf, yb, hf, hb) = refs
    L = SSD_L
    nc = seq // L
    pad = SUBLANES
    half = SSD_CONV // 2

    for buf, src, cw, cb, dst in ((xpad, x_ref, cwx_ref, cbx_ref, xc), (bcpad, bc_ref, cwbc_ref, cbbc_ref, bcc)):
        width = buf.shape[1]
        buf[0:pad, :] = jnp.zeros((pad, width), F32)
        buf[pad + seq:2 * pad + seq, :] = jnp.zeros((pad, width), F32)
        buf[pad:pad + seq, :] = src[...].astype(F32)
        for blk in range(nc):
            acc = jnp.broadcast_to(cb[...], (L, width))
            for j in range(SSD_CONV):
                r0 = pad - half + j + blk * L
                acc = acc + cw[j:j + 1, :] * buf[r0:r0 + L, :]
            dst[blk * L:(blk + 1) * L, :] = _silu(acc)

    lane = lax.broadcasted_iota(jnp.int32, (seq, LANES), 1)
    dts = jnp.where(lane < 2 * SSD_HEADS, _softplus(dt_ref[...] + dtb_ref[...]), 0.0)
    dt_scr[...] = dts
    a_scr[...] = dts * (-jnp.exp(alog_ref[...]))

    if has_h0:
        hf[...] = h0f_ref[...]
        hb[...] = h0b_ref[...]
    else:
        hf[...] = jnp.zeros(hf.shape, F32)
        hb[...] = jnp.zeros(hb.shape, F32)

    row = lax.broadcasted_iota(jnp.int32, (L, L), 0)
    col = lax.broadcasted_iota(jnp.int32, (L, L), 1)
    lane_l = lax.broadcasted_iota(jnp.int32, (L, LANES), 1)
    lo_half = lane_l < SSD_P

    def chunk(c, fwd, h_scr):
        off = 0 if fwd else SSD_HEADS
        r0 = pl.multiple_of(c * L, L)
        a = a_scr[pl.ds(r0, L), :]
        dt = dt_scr[pl.ds(r0, L), :]
        cs = _cumsum_rows(a, L)
        total = cs[L - 1:L, :]
        if fwd:
            u = cs
            rvec = jnp.exp(cs)
            ed = jnp.exp(total - cs) * dt
            keep = col <= row
        else:
            ex = cs - a
            u = -ex
            rvec = jnp.exp(total - ex)
            ed = jnp.exp(ex) * dt
            keep = col >= row
        ut = jnp.transpose(u)
        dtt = jnp.transpose(dt)
        tcol = jnp.transpose(jnp.broadcast_to(total, (L, LANES)))[:, 0:1]
        rexp = _expand_heads(rvec, off)
        edexp = _expand_heads(ed, off)
        x = xc[pl.ds(r0, L), :]
        bc = bcc[pl.ds(r0, L), :]
        outs = []
        for g in range(SSD_GROUPS):
            bg = bc[:, g * SSD_N:(g + 1) * SSD_N]
            cg = bc[:, SSD_GROUPS * SSD_N + g * SSD_N:SSD_GROUPS * SSD_N + (g + 1) * SSD_N]
            cbm = _mm_nt(cg, bg)
            hg = h_scr[g * GROUP_W:(g + 1) * GROUP_W, :]
            xg = x[:, g * GROUP_W:(g + 1) * GROUP_W]
            y_off = _mm_nt(cg, hg) * rexp[:, g * GROUP_W:(g + 1) * GROUP_W]
            tiles = []
            for p in range(HEADS_PER_GROUP // 2):
                xt = xg[:, p * LANES:(p + 1) * LANES]
                acc = None
                for s in range(2):
                    h = off + g * HEADS_PER_GROUP + 2 * p + s
                    seg = u[:, h:h + 1] - ut[h:h + 1, :]
                    m = cbm * jnp.exp(jnp.where(keep, seg, -jnp.inf)) * dtt[h:h + 1, :]
                    xm = jnp.where(lo_half if s == 0 else jnp.logical_not(lo_half), xt, 0.0)
                    d = _mm(m, xm)
                    acc = d if acc is None else acc + d
                tiles.append(acc)
            outs.append(y_off + jnp.concatenate(tiles, axis=1))
            decs = []
            for hh in range(HEADS_PER_GROUP):
                h = off + g * HEADS_PER_GROUP + hh
                decs.append(jnp.broadcast_to(jnp.exp(tcol[h:h + 1, :]), (SSD_P, SSD_N)))
            dec = jnp.concatenate(decs, axis=0)
            h_scr[g * GROUP_W:(g + 1) * GROUP_W, :] = dec * hg + _mm_tn(xg * edexp[:, g * GROUP_W:(g + 1) * GROUP_W], bg)
        return r0, jnp.concatenate(outs, axis=1)

    def body(i, carry):
        r0, y = chunk(i, True, hf)
        yf[pl.ds(r0, L), :] = y
        r0, y = chunk(nc - 1 - i, False, hb)
        yb[pl.ds(r0, L), :] = y
        return carry

    lax.fori_loop(0, nc, body, 0)
    sf_ref[...] = hf[...]
    sb_ref[...] = hb[...]

    for blk in range(nc):
        rs = slice(blk * L, (blk + 1) * L)
        y = yf[rs, :] + yb[rs, :] + dsk_ref[...] * xc[rs, :]
        y = y * _silu(z_ref[rs, :].astype(F32))
        y_ref[rs, :] = _rms(y, nw_ref[...]).astype(y_ref.dtype)


def _ssd(p0, p0dt, tok0, nseq, seq, cw, cb, dtb, alog, dskip, nw, h0f=None, h0b=None):
    has_h0 = h0f is not None
    b0 = tok0 // seq

    def cols(width, start):
        return pl.BlockSpec((seq, width), lambda s: (b0 + s, start // width))

    def full(shape):
        return pl.BlockSpec(shape, lambda s: (0,) * len(shape))

    in_specs = [cols(D_SSD, P0_Z), cols(D_SSD, P0_X), cols(BC_W, P0_BC), cols(LANES, 0),
                full((SSD_CONV, D_SSD)), full((SSD_CONV, BC_W)), full((1, D_SSD)), full((1, BC_W)),
                full((1, LANES)), full((1, LANES)), full((1, D_SSD)), full((1, D_SSD))]
    args = [p0, p0, p0, p0dt, cw[:, :D_SSD], cw[:, D_SSD:], cb[:D_SSD].reshape(1, -1), cb[D_SSD:].reshape(1, -1),
            dtb, alog, dskip, nw.reshape(1, -1)]
    st_spec = pl.BlockSpec((None, D_SSD, SSD_N), lambda s: (s, 0, 0))
    if has_h0:
        in_specs += [st_spec, st_spec]
        args += [h0f, h0b]
    st_shape = jax.ShapeDtypeStruct((nseq, D_SSD, SSD_N), F32)
    return pl.pallas_call(
        functools.partial(_ssd_kernel, seq=seq, has_h0=has_h0),
        grid=(nseq,), in_specs=in_specs,
        out_specs=[pl.BlockSpec((seq, D_SSD), lambda s: (s, 0)), st_spec, st_spec],
        out_shape=[jax.ShapeDtypeStruct((nseq * seq, D_SSD), BF16), st_shape, st_shape],
        scratch_shapes=[pltpu.VMEM((seq + 2 * SUBLANES, D_SSD), F32), pltpu.VMEM((seq + 2 * SUBLANES, BC_W), F32),
                        pltpu.VMEM((seq, D_SSD), F32), pltpu.VMEM((seq, BC_W), F32),
                        pltpu.VMEM((seq, LANES), F32), pltpu.VMEM((seq, LANES), F32),
                        pltpu.VMEM((seq, D_SSD), F32), pltpu.VMEM((seq, D_SSD), F32),
                        pltpu.VMEM((D_SSD, SSD_N), F32), pltpu.VMEM((D_SSD, SSD_N), F32)],
        compiler_params=_cparams(1),
        name=f"ssd{seq}",
    )(*args)


def _place_halves(tile, kv_in_high):
    lo = lax.broadcasted_iota(jnp.int32, tile.shape, 1) < ATT_HD
    swapped = pltpu.roll(tile, ATT_HD, 1)
    if kv_in_high:
        return jnp.where(lo, swapped, 0.0), jnp.where(lo, 0.0, tile)
    return jnp.where(lo, tile, 0.0), jnp.where(lo, 0.0, swapped)


def _place_rows(vt, kv_in_high):
    head = vt[ATT_HD:, :] if kv_in_high else vt[:ATT_HD, :]
    z = jnp.zeros_like(head)
    return jnp.concatenate([head, z], axis=0), jnp.concatenate([z, head], axis=0)


LOG2E = 1.4426950408889634
SCORE_SCALE = ATT_SCALE * LOG2E


def _sink_attend_t(score_parts, value_parts, sink2):
    m = sink2
    for s in score_parts:
        m = jnp.maximum(m, jnp.max(s, axis=0, keepdims=True))
    denom = jnp.exp2(sink2 - m)
    out = None
    for s, v in zip(score_parts, value_parts):
        p = jnp.exp2(s - m)
        denom = denom + jnp.sum(p, axis=0, keepdims=True)
        o = _mm(v, p)
        out = o if out is None else out + o
    return out * (1.0 / denom)


def _attn_schedule(n, scores, attend):
    scores(0)
    for j in range(n):
        if j + 1 < n:
            scores(j + 1)
        attend(j)


def _ctx_attn_kernel(q_ref, k_ref, v_ref, sink_ref, o_ref, kt_ref, vt_ref, s_a, s_b):
    sink2 = sink_ref[...] * LOG2E
    bufs = (s_a, s_b)
    half = SEQ // 2
    for t in range(ATT_KV_DIM // LANES):
        cols = slice(t * LANES, (t + 1) * LANES)
        kt_ref[cols, :] = jnp.transpose(k_ref[:, cols].astype(F32))
        vt_ref[cols, :] = jnp.transpose(v_ref[:, cols].astype(F32))

    def kv_tile(ref, j):
        return ref[:, (j // 2) * LANES:(j // 2 + 1) * LANES].astype(F32), (j % 2 == 1)

    def scores(j):
        k_lo, k_hi = _place_halves(*kv_tile(k_ref, j))
        qst = jnp.concatenate([q_ref[:, qt * LANES:(qt + 1) * LANES] for qt in (2 * j, 2 * j + 1)], axis=0)
        bufs[j % 2][...] = _mm_nt(jnp.concatenate([k_lo, k_hi], axis=0), qst) * SCORE_SCALE

    def attend(j):
        src = bufs[j % 2]
        v, high = kv_tile(v_ref, j)
        vts = _place_rows(jnp.transpose(v), high)
        for ql in range(2):
            qt = 2 * j + ql
            for qh in range(2):
                cols = slice(ql * SEQ + qh * half, ql * SEQ + (qh + 1) * half)
                acc = None
                for s, vv in enumerate(vts):
                    o = _sink_attend_t([src[s * SEQ:(s + 1) * SEQ, cols]], [vv], sink2[:, 2 * qt + s:2 * qt + s + 1])
                    acc = o if acc is None else acc + o
                o_ref[qh * half:(qh + 1) * half, qt * LANES:(qt + 1) * LANES] = jnp.transpose(acc).astype(o_ref.dtype)

    _attn_schedule(ATT_KV, scores, attend)


def _ctx_attn(p0, sink):
    def cols(width, start):
        return pl.BlockSpec((SEQ, width), lambda b: (b, start // width))

    sbuf = pltpu.VMEM((2 * SEQ, 2 * SEQ), F32)
    return pl.pallas_call(
        _ctx_attn_kernel,
        grid=(BATCH,),
        in_specs=[cols(D, P0_Q), cols(ATT_KV_DIM, P0_K), cols(ATT_KV_DIM, P0_V),
                  pl.BlockSpec((1, LANES), lambda b: (0, 0))],
        out_specs=[pl.BlockSpec((SEQ, D), lambda b: (b, 0))]
        + [pl.BlockSpec((None, ATT_KV_DIM, SEQ), lambda b: (b, 0, 0))] * 2,
        out_shape=[jax.ShapeDtypeStruct((T_CTX, D), BF16)] + [jax.ShapeDtypeStruct((BATCH, ATT_KV_DIM, SEQ), F32)] * 2,
        scratch_shapes=[sbuf, sbuf],
        compiler_params=_cparams(1),
        name="ctx_attn",
    )(p0, p0, p0, sink)


def _rope_tables():
    quarter = ATT_HD // 4
    t = np.arange(DEC_SEQ)
    lane = np.arange(LANES)
    inv = ROPE_BASE ** (-(lane % quarter).astype(np.float64) / quarter)
    pos = np.where(((lane % ATT_HD) < ATT_HD // 2)[None, :], (t // GRID_W)[:, None], (t % GRID_W)[:, None])
    ang = pos * inv[None, :]
    first = ((lane % (2 * quarter)) < quarter)[None, :]
    cos, sin = np.cos(ang), np.sin(ang)
    return (jnp.asarray(cos, F32), jnp.asarray(np.where(first, -sin, 0.0), F32),
            jnp.asarray(np.where(first, 0.0, sin), F32))


def _rope(x, cos, sa, sb):
    quarter = ATT_HD // 4
    return x * cos + pltpu.roll(x, LANES - quarter, 1) * sa + pltpu.roll(x, quarter, 1) * sb


def _lat_attn_kernel(q_ref, kp_ref, kc_ref, kn_ref, vp_ref, vc_ref, vn_ref, ck_ref, cv_ref,
                     cos_ref, sa_ref, sb_ref, sink_ref, o_ref, c_a, c_b, w_a, w_b):
    blk = pl.program_id(1)
    nb = pl.num_programs(1)
    B = ATT_BLOCK
    sink2 = sink_ref[...] * LOG2E
    cbufs, wbufs = (c_a, c_b), (w_a, w_b)

    def tables(b):
        r0 = pl.multiple_of(b * B, B)
        return cos_ref[pl.ds(r0, B), :], sa_ref[pl.ds(r0, B), :], sb_ref[pl.ds(r0, B), :]

    tq = tables(blk)
    tk = [tables(jnp.maximum(blk - 1, 0)), tq, tables(jnp.minimum(blk + 1, nb - 1))]
    kabs = (blk - 1) * B + lax.broadcasted_iota(jnp.int32, (3 * B, B), 0)
    qpos = blk * B + lax.broadcasted_iota(jnp.int32, (3 * B, B), 1)
    ok = (jnp.abs(qpos - kabs) <= WINDOW) & (kabs >= 0) & (kabs < nb * B)
    ok = jnp.concatenate([ok, ok], axis=1)
    ok = jnp.concatenate([ok, ok], axis=0)

    def scores(j):
        high = (j % 2 == 1)
        sl = slice((j // 2) * LANES, (j // 2 + 1) * LANES)
        kw = jnp.concatenate([_rope(r[:, sl].astype(F32), *tb) for r, tb in zip((kp_ref, kc_ref, kn_ref), tk)], axis=0)
        qs = [q_ref[:, qt * LANES:(qt + 1) * LANES].astype(F32) for qt in (2 * j, 2 * j + 1)]
        q_plain = jnp.concatenate(qs, axis=0)
        q_rope = jnp.concatenate([_rope(q, *tq) for q in qs], axis=0)
        ck = jnp.transpose(ck_ref[sl, :])
        cbufs[j % 2][...] = _mm_nt(jnp.concatenate(_place_halves(ck, high), axis=0), q_plain) * SCORE_SCALE
        win = _mm_nt(jnp.concatenate(_place_halves(kw, high), axis=0), q_rope) * SCORE_SCALE
        wbufs[j % 2][...] = jnp.where(ok, win, -jnp.inf)

    def attend(j):
        high = (j % 2 == 1)
        sl = slice((j // 2) * LANES, (j // 2 + 1) * LANES)
        vw = jnp.concatenate([jnp.transpose(r[:, sl].astype(F32)) for r in (vp_ref, vc_ref, vn_ref)], axis=1)
        vts = _place_rows(vw, high)
        cvts = _place_rows(cv_ref[sl, :], high)
        csrc, wsrc = cbufs[j % 2], wbufs[j % 2]
        for ql in range(2):
            qt = 2 * j + ql
            cols = slice(ql * B, (ql + 1) * B)
            acc = None
            for s in range(2):
                parts = [csrc[s * PAST_LEN:(s + 1) * PAST_LEN, cols], wsrc[s * 3 * B:(s + 1) * 3 * B, cols]]
                o = _sink_attend_t(parts, [cvts[s], vts[s]], sink2[:, 2 * qt + s:2 * qt + s + 1])
                acc = o if acc is None else acc + o
            o_ref[:, qt * LANES:(qt + 1) * LANES] = jnp.transpose(acc).astype(o_ref.dtype)

    _attn_schedule(ATT_KV, scores, attend)


def _lat_attn(p0, ck, cv, sink):
    nb = DEC_SEQ // ATT_BLOCK
    base = T_CTX // ATT_BLOCK

    def kv(start, shift):
        return pl.BlockSpec((ATT_BLOCK, ATT_KV_DIM),
                            lambda b, i: (base + b * nb + jnp.clip(i + shift, 0, nb - 1), start // ATT_KV_DIM))

    def full(shape):
        return pl.BlockSpec(shape, lambda b, i: (0,) * len(shape))

    cache = pl.BlockSpec((None, ATT_KV_DIM, PAST_LEN), lambda b, i: (b, 0, 0))
    cos, sa, sb = _rope_tables()
    return pl.pallas_call(
        _lat_attn_kernel,
        grid=(DEC_BATCH, nb),
        in_specs=[pl.BlockSpec((ATT_BLOCK, D), lambda b, i: (base + b * nb + i, P0_Q // D)),
                  kv(P0_K, -1), kv(P0_K, 0), kv(P0_K, 1), kv(P0_V, -1), kv(P0_V, 0), kv(P0_V, 1),
                  cache, cache, full((DEC_SEQ, LANES)), full((DEC_SEQ, LANES)), full((DEC_SEQ, LANES)),
                  full((1, LANES))],
        out_specs=pl.BlockSpec((ATT_BLOCK, D), lambda b, i: (b * nb + i, 0)),
        out_shape=jax.ShapeDtypeStruct((T_LAT, D), BF16),
        scratch_shapes=[pltpu.VMEM((2 * PAST_LEN, 2 * ATT_BLOCK), F32)] * 2
        + [pltpu.VMEM((2 * 3 * ATT_BLOCK, 2 * ATT_BLOCK), F32)] * 2,
        compiler_params=_cparams(2),
        name="lat_attn",
    )(p0, p0, p0, p0, p0, p0, p0, ck, cv, cos, sa, sb, sink)


def _log_sigmoid(x):
    return jnp.minimum(x, 0.0) - jnp.log(1.0 + jnp.exp(-jnp.abs(x)))


def _gla_kernel(*refs, seq, has_s0):
    if has_s0:
        (q_ref, k_ref, v_ref, r_ref, lr_ref, w2f_ref, w2b_ref, bf_ref, bb_ref, nw_ref, s0f_ref, s0b_ref,
         y_ref, sf_ref, sb_ref, gf, gb, yf, yb, stf, stb) = refs
    else:
        (q_ref, k_ref, v_ref, r_ref, lr_ref, w2f_ref, w2b_ref, bf_ref, bb_ref, nw_ref,
         y_ref, sf_ref, sb_ref, gf, gb, yf, yb, stf, stb) = refs
    C = GLA_C
    nc = seq // C
    lr = lr_ref[...]
    gf[...] = _log_sigmoid(_mm(lr, w2f_ref[...]) + bf_ref[...]) / GLA_GATE_NORM
    gb[...] = _log_sigmoid(_mm(lr, w2b_ref[...]) + bb_ref[...]) / GLA_GATE_NORM
    for h in range(GLA_HEADS):
        rows = slice(h * GLA_DV, (h + 1) * GLA_DV)
        if has_s0:
            stf[rows, :] = jnp.transpose(s0f_ref[h * GLA_DK:(h + 1) * GLA_DK, :])
            stb[rows, :] = jnp.transpose(s0b_ref[h * GLA_DK:(h + 1) * GLA_DK, :])
        else:
            stf[rows, :] = jnp.zeros((GLA_DV, GLA_DK), F32)
            stb[rows, :] = jnp.zeros((GLA_DV, GLA_DK), F32)

    row = lax.broadcasted_iota(jnp.int32, (C, C), 0)
    col = lax.broadcasted_iota(jnp.int32, (C, C), 1)
    qscale = GLA_DK ** -0.5

    def chunk(c, fwd):
        g_scr, y_scr, st = (gf, yf, stf) if fwd else (gb, yb, stb)
        r0 = pl.multiple_of(c * C, C)
        g = g_scr[pl.ds(r0, C), :]
        cs = _cumsum_rows(g, C)
        total = cs[C - 1:C, :]
        q = q_ref[pl.ds(r0, C), :].astype(F32) * qscale
        k = k_ref[pl.ds(r0, C), :].astype(F32)
        v = v_ref[pl.ds(r0, C), :]
        if fwd:
            qs, ks, ke = q * jnp.exp(cs), k * jnp.exp(-cs), k * jnp.exp(total - cs)
            keep = col <= row
        else:
            ex = cs - g
            qs, ks, ke = q * jnp.exp(total - ex), k * jnp.exp(ex - total), k * jnp.exp(ex)
            keep = col >= row
        dec = jnp.exp(total)
        for h in range(GLA_HEADS):
            kc = slice(h * GLA_DK, (h + 1) * GLA_DK)
            vc = slice(h * GLA_DV, (h + 1) * GLA_DV)
            s_t = st[vc, :]
            att = jnp.where(keep, _mm_nt(qs[:, kc], ks[:, kc]), 0.0)
            y_scr[pl.ds(r0, C), vc] = _mm(att, v[:, vc]) + _mm_nt(qs[:, kc], s_t)
            st[vc, :] = dec[:, kc] * s_t + _mm_tn(v[:, vc], ke[:, kc])

    def body(i, carry):
        chunk(i, True)
        chunk(nc - 1 - i, False)
        return carry

    lax.fori_loop(0, nc, body, 0, unroll=4)

    nw = nw_ref[...]
    for blk in range(seq // LANES):
        rs = slice(blk * LANES, (blk + 1) * LANES)
        y = yf[rs, :] + yb[rs, :]
        gate = _silu(r_ref[rs, :].astype(F32))
        for h in range(GLA_HEADS):
            vc = slice(h * GLA_DV, (h + 1) * GLA_DV)
            y_ref[rs, vc] = (_rms(y[:, vc], nw) * gate[:, vc]).astype(y_ref.dtype)
    for h in range(GLA_HEADS):
        rows = slice(h * GLA_DV, (h + 1) * GLA_DV)
        sf_ref[h * GLA_DK:(h + 1) * GLA_DK, :] = jnp.transpose(stf[rows, :])
        sb_ref[h * GLA_DK:(h + 1) * GLA_DK, :] = jnp.transpose(stb[rows, :])


def _gla(p1, p1lr, tok0, nseq, seq, w2f, w2b, bgf, bgb, nw, s0f=None, s0b=None):
    has_s0 = s0f is not None
    b0 = tok0 // seq
    dk_all, dv_all = GLA_HEADS * GLA_DK, GLA_HEADS * GLA_DV

    def cols(width, start):
        return pl.BlockSpec((seq, width), lambda s: (b0 + s, start // width))

    def full(shape):
        return pl.BlockSpec(shape, lambda s: (0,) * len(shape))

    in_specs = [cols(dk_all, P1_Q), cols(dk_all, P1_K), cols(dv_all, P1_V), cols(dv_all, P1_R), cols(LANES, 0),
                full((LANES, dk_all)), full((LANES, dk_all)), full((1, dk_all)), full((1, dk_all)), full((1, GLA_DV))]
    args = [p1, p1, p1, p1, p1lr, w2f, w2b, bgf.reshape(1, -1), bgb.reshape(1, -1), nw.reshape(1, -1)]
    st_spec = pl.BlockSpec((None, dk_all, GLA_DV), lambda s: (s, 0, 0))
    if has_s0:
        in_specs += [st_spec, st_spec]
        args += [s0f, s0b]
    st_shape = jax.ShapeDtypeStruct((nseq, dk_all, GLA_DV), F32)
    return pl.pallas_call(
        functools.partial(_gla_kernel, seq=seq, has_s0=has_s0),
        grid=(nseq,), in_specs=in_specs,
        out_specs=[pl.BlockSpec((seq, dv_all), lambda s: (s, 0)), st_spec, st_spec],
        out_shape=[jax.ShapeDtypeStruct((nseq * seq, dv_all), BF16), st_shape, st_shape],
        scratch_shapes=[pltpu.VMEM((seq, dk_all), F32), pltpu.VMEM((seq, dk_all), F32),
                        pltpu.VMEM((seq, dv_all), F32), pltpu.VMEM((seq, dv_all), F32),
                        pltpu.VMEM((dv_all, GLA_DK), F32), pltpu.VMEM((dv_all, GLA_DK), F32)],
        compiler_params=_cparams(1),
        name=f"gla{seq}",
    )(*args)


ROUTE_TM = 512
ROUTE_SUB = 256
ROUTE_ROWS = 32


def _split_bf16(x):
    hi = x.astype(BF16)
    return hi, (x - hi.astype(F32)).astype(BF16)


def _outproj_kernel(*refs, n_in, dual_x):
    y_refs = refs[:2 * n_in]
    n_x = 2 if dual_x else 1
    x_refs = refs[2 * n_in + 1:2 * n_in + 1 + n_x]
    w_ref = refs[2 * n_in]
    (g1_ref, sh_ref, sc_ref, nw_ref, wr_ref, br_ref,
     xo_ref, tr_ref, route_ref, cnt_ref, w_scr, wr_hl, carry, o_scr) = refs[2 * n_in + 1 + n_x:]
    is_ctx = pl.program_id(0) < T_CTX // ROUTE_TM

    @pl.when(pl.program_id(0) == 0)
    def _():
        w_scr[...] = w_ref[...].astype(BF16)
        hi, lo = _split_bf16(jnp.transpose(wr_ref[...]))
        wr_hl[0:LANES, :] = hi
        wr_hl[LANES:2 * LANES, :] = lo
        carry[...] = jnp.zeros(carry.shape, F32)

    o = None
    for i in range(n_in):
        y = jnp.where(is_ctx, y_refs[2 * i][...], y_refs[2 * i + 1][...])
        d = jnp.dot(y, w_scr[i * D:(i + 1) * D, :], preferred_element_type=F32)
        o = d if o is None else o + d
    o_scr[...] = o
    for sub in range(ROUTE_TM // ROUTE_SUB):
        _outproj_subtile(sub, is_ctx, o_scr, x_refs, dual_x, g1_ref, sh_ref, sc_ref, nw_ref, br_ref,
                         xo_ref, tr_ref, route_ref, cnt_ref, wr_hl, carry)


def _outproj_subtile(sub, is_ctx, o_scr, x_refs, dual_x, g1_ref, sh_ref, sc_ref, nw_ref, br_ref,
                     xo_ref, tr_ref, route_ref, cnt_ref, wr_hl, carry):
    tm = ROUTE_SUB
    rows = slice(sub * tm, (sub + 1) * tm)
    x_in = jnp.where(is_ctx, x_refs[0][rows, :], x_refs[1][rows, :]) if dual_x else x_refs[0][rows, :]
    x = x_in + g1_ref[0] * o_scr[rows, :]
    xo_ref[rows, :] = x
    t = _rms(x, nw_ref[...]) * (1.0 + sc_ref[0]) + sh_ref[0]
    for k in range(D // LANES):
        tr_ref[pl.ds(sub * tm * SUBLANES + k, tm, stride=SUBLANES), :] = t[:, k * LANES:(k + 1) * LANES]

    t_hi, t_lo = _split_bf16(t)
    lg = _mm_nt(wr_hl[...], t_hi)
    nr = ROUTE_ROWS
    logit = lg[0:nr, :] + lg[LANES:LANES + nr, :] + _mm_nt(wr_hl[0:LANES, :], t_lo)[0:nr, :] + br_ref[0:nr, :]
    rowf = lax.broadcasted_iota(jnp.int32, (nr, tm), 0).astype(F32)
    neg = -jnp.inf

    def first_argmax(v, vmax):
        return jnp.min(jnp.where(v == vmax, rowf, float(LANES)), axis=0, keepdims=True)

    gl = jnp.where(rowf < N_GROUPS, logit, neg)
    gmax = jnp.max(gl, axis=0, keepdims=True)
    gsel = first_argmax(gl, gmax)
    gprob = 1.0 / jnp.sum(jnp.exp(gl - gmax), axis=0, keepdims=True)
    first = N_GROUPS + EXP_PER_GROUP * gsel
    el = jnp.where((rowf >= first) & (rowf < first + EXP_PER_GROUP), logit, neg)
    m1 = jnp.max(el, axis=0, keepdims=True)
    i1 = first_argmax(el, m1)
    el2 = jnp.where(rowf == i1, neg, el)
    m2 = jnp.max(el2, axis=0, keepdims=True)
    i2 = first_argmax(el2, m2)
    e2 = jnp.exp(m2 - m1)
    c1 = gprob / (1.0 + e2)
    c2 = gprob * e2 / (1.0 + e2)
    x1 = i1 - N_GROUPS
    x2 = i2 - N_GROUPS

    erow = rowf
    hot = ((erow == x1) | (erow == x2)).astype(F32)
    tri = (lax.broadcasted_iota(jnp.int32, (tm, tm), 0) < lax.broadcasted_iota(jnp.int32, (tm, tm), 1))
    before = _mm(hot, tri.astype(F32)) + carry[...]
    r1 = jnp.sum(jnp.where(erow == x1, before, 0.0), axis=0, keepdims=True)
    r2 = jnp.sum(jnp.where(erow == x2, before, 0.0), axis=0, keepdims=True)
    total = carry[...] + _mm(hot, jnp.ones((tm, tm), F32))
    carry[...] = total
    cnt_ref[...] = total[0:N_EXPERTS, 0:LANES]
    row8 = lax.broadcasted_iota(jnp.int32, (SUBLANES, tm), 0)
    out = jnp.zeros((SUBLANES, tm), F32)
    for k, v in enumerate((x1, x2, c1, c2, r1, r2)):
        out = jnp.where(row8 == k, jnp.broadcast_to(v, (SUBLANES, tm)), out)
    route_ref[:, rows] = out


def _outproj_route(ys, w_out, xs, mods, layer, norm_w, w_router, b_router):
    tm = ROUTE_TM
    n_in = len(ys) // 2
    dual_x = len(xs) == 2
    kdim = w_out.shape[0]

    def full(shape):
        return pl.BlockSpec(shape, lambda i: (0,) * len(shape))

    tile = pl.BlockSpec((tm, D), lambda i: (i, 0))
    pair = list(_ctx_lat_specs(tm))
    in_specs = (pair * n_in + [full((kdim, D))] + (pair if dual_x else [tile])
                + [_mod_spec(layer, 2, tm), _mod_spec(layer, 3, tm), _mod_spec(layer, 4, tm),
                   full((1, D)), full((D, LANES)), full((LANES, ROUTE_SUB))])
    cnt = jax.ShapeDtypeStruct((N_EXPERTS, LANES), F32)
    return pl.pallas_call(
        functools.partial(_outproj_kernel, n_in=n_in, dual_x=dual_x),
        grid=(T // tm,), in_specs=in_specs,
        out_specs=[tile, pl.BlockSpec((tm * SUBLANES, LANES), lambda i: (i, 0)),
                   pl.BlockSpec((SUBLANES, tm), lambda i: (0, i)), full(cnt.shape)],
        out_shape=[jax.ShapeDtypeStruct((T, D), F32), jax.ShapeDtypeStruct((T * SUBLANES, LANES), F32),
                   jax.ShapeDtypeStruct((SUBLANES, T), F32), cnt],
        scratch_shapes=[pltpu.VMEM((kdim, D), BF16), pltpu.VMEM((2 * LANES, D), BF16),
                        pltpu.VMEM((ROUTE_ROWS, ROUTE_SUB), F32), pltpu.VMEM((tm, D), F32)],
        compiler_params=_cparams(1),
        name=f"outproj{layer}",
    )(*ys, w_out, *xs, mods, mods, mods, norm_w.reshape(1, D), w_router,
      jnp.broadcast_to(b_router.reshape(LANES, 1), (LANES, ROUTE_SUB)))


def _moe_meta(counts):
    tm = MOE_TM
    experts = jnp.arange(N_EXPERTS, dtype=jnp.int32)
    counts = jnp.max(counts, axis=1).astype(jnp.int32)
    padded = ((counts + tm - 1) // tm) * tm
    ends = jnp.cumsum(padded)
    tile_start = jnp.arange(MOE_TILES, dtype=jnp.int32) * tm
    te = jnp.sum((tile_start[:, None] >= ends[None, :]).astype(jnp.int32), axis=1)
    last = jnp.max(jnp.where(counts > 0, experts, 0))
    meta = jnp.concatenate([jnp.minimum(te, last), ends[-1:] // tm]).astype(jnp.int32)
    starts = ends - padded
    pads = jnp.concatenate([starts + counts, ends[-1:], ends, jnp.full((1,), MOE_ROWS)]).astype(jnp.int32)
    return starts.astype(jnp.int32), pads, meta


def _expert_changed(meta_ref, j):
    return (j == 0) | (meta_ref[j] != meta_ref[jnp.maximum(j - 1, 0)])


def _moe_up_kernel(pos1_ref, pos2_ref, pads_ref, meta_ref, tr_hbm, wg_ref, wu_ref, a_ref, rowmap_ref,
                   tr_scr, g0, g1, wg_bf, wu_bf, sem):
    j = pl.program_id(0)
    tm = MOE_TM
    ntiles = meta_ref[MOE_TILES]

    def gather(tile, dst):
        for mi in range(tm):
            tok = jnp.minimum(rowmap_ref[tile * tm + mi] >> 1, T - 1)
            dst[mi * SUBLANES:(mi + 1) * SUBLANES, :] = tr_scr[pl.ds(pl.multiple_of(tok * SUBLANES, SUBLANES), SUBLANES), :]

    @pl.when(j == 0)
    def _():
        load = pltpu.make_async_copy(tr_hbm, tr_scr, sem)
        load.start()

        def clear(c, carry):
            for i in range(SUBLANES):
                rowmap_ref[c * SUBLANES + i] = 2 * T
            return carry
        for k in range(N_EXPERTS + 1):
            lax.fori_loop(pads_ref[k] // SUBLANES, pads_ref[N_EXPERTS + 1 + k] // SUBLANES, clear, 0)

        def place(t, carry):
            rowmap_ref[pos1_ref[t]] = 2 * t
            rowmap_ref[pos2_ref[t]] = 2 * t + 1
            return carry
        lax.fori_loop(0, T, place, 0, unroll=8)
        load.wait()
        gather(0, g0)

    def compute(cur, nxt):
        gather(jnp.minimum(j + 1, ntiles - 1), nxt)
        x = _tokmajor_to_std(cur, tm).astype(BF16)
        g = jnp.dot(x, wg_bf[...], preferred_element_type=F32)
        u = jnp.dot(x, wu_bf[...], preferred_element_type=F32)
        a_ref[...] = (_silu(g) * u).astype(a_ref.dtype)

    @pl.when(j < ntiles)
    def _():
        @pl.when(_expert_changed(meta_ref, j))
        def _():
            wg_bf[...] = wg_ref[...].astype(BF16)
            wu_bf[...] = wu_ref[...].astype(BF16)

        pl.when(j % 2 == 0)(functools.partial(compute, g0, g1))
        pl.when(j % 2 == 1)(functools.partial(compute, g1, g0))

    @pl.when(j >= ntiles)
    def _():
        a_ref[...] = jnp.zeros(a_ref.shape, a_ref.dtype)


def _moe_down_kernel(rowmap_ref, cpair_ref, meta_ref, a_ref, wd_ref, *rest, layer, final):
    if final:
        x_hbm, mods_ref, fw_ref, out_c, out_l, acc, y0, y1, wd_bf, xin, xout, io_sems = rest
    else:
        out_hbm, acc, y0, y1, wd_bf = rest
    j = pl.program_id(0)
    tm = MOE_TM
    zrows = MERGE_TM
    ntiles = meta_ref[MOE_TILES]

    @pl.when(j == 0)
    def _():
        def zero(i, carry):
            acc[pl.ds(pl.multiple_of(i * zrows, zrows), zrows), :] = jnp.zeros((zrows, LANES), F32)
            return carry
        lax.fori_loop(0, ACC_ROWS // zrows, zero, 0)

    def matmul(dst):
        y = jnp.dot(a_ref[...], wd_bf[...], preferred_element_type=F32)
        for k in range(D // LANES):
            dst[pl.ds(k, tm, stride=SUBLANES), :] = y[:, k * LANES:(k + 1) * LANES]

    def scatter(tile, src):
        for b in range(tm // SUBLANES):
            ents = [rowmap_ref[tile * tm + b * SUBLANES + i] for i in range(SUBLANES)]
            offs = [pl.multiple_of((e >> 1) * SUBLANES, SUBLANES) for e in ents]
            olds = [acc[pl.ds(o, SUBLANES), :] for o in offs]
            for i, o in enumerate(offs):
                r = (b * SUBLANES + i) * SUBLANES
                acc[pl.ds(o, SUBLANES), :] = olds[i] + cpair_ref[ents[i]] * src[r:r + SUBLANES, :]

    has_mm = j < ntiles
    has_sc = (j >= 1) & (j <= ntiles)

    @pl.when(has_mm)
    def _():
        @pl.when(_expert_changed(meta_ref, j))
        def _():
            wd_bf[...] = wd_ref[...].astype(BF16)

    for par, (cur, prev) in enumerate(((y0, y1), (y1, y0))):
        mine = (j % 2) == par

        @pl.when(mine & has_mm & has_sc)
        def _():
            matmul(cur)
            scatter(j - 1, prev)

        @pl.when(mine & has_mm & jnp.logical_not(has_sc))
        def _():
            matmul(cur)

        @pl.when(mine & jnp.logical_not(has_mm) & has_sc)
        def _():
            scatter(j - 1, prev)

    @pl.when(j == pl.num_programs(0) - 1)
    def _():
        if not final:
            pltpu.sync_copy(acc.at[0:T * SUBLANES, :], out_hbm)
            return
        ft = MERGE_TM
        n_t = T // ft
        n_c = T_CTX // ft

        def fetch(i):
            return pltpu.make_async_copy(x_hbm.at[i * ft:(i + 1) * ft, :], xin.at[i % 2], io_sems.at[i % 2])

        def flush(i):
            dst = out_c.at[i * ft:(i + 1) * ft, :] if i < n_c else out_l.at[(i - n_c) * ft:(i - n_c + 1) * ft, :]
            return pltpu.make_async_copy(xout.at[i % 2], dst, io_sems.at[2 + i % 2])

        fetch(0).start()
        for i in range(n_t):
            if i + 1 < n_t:
                fetch(i + 1).start()
            fetch(i).wait()
            moe = jnp.concatenate([acc[pl.ds(i * ft * SUBLANES + k, ft, stride=SUBLANES), :]
                                   for k in range(D // LANES)], axis=1)
            row = 0 if i * ft < T_CTX else 1 + (i * ft - T_CTX) // DEC_SEQ
            x = xin[i % 2] + mods_ref[(layer * SUBLANES + row) * 6 + 5] * moe
            if i >= 2:
                flush(i - 2).wait()
            xout[i % 2] = _rms(x, fw_ref[...])
            flush(i).start()
        for i in range(max(n_t - 2, 0), n_t):
            flush(i).wait()


def _moe(tr, route_t, counts, layer, w_gate, w_up, w_down, xmid=None, mods=None, final_w=None):
    tm = MOE_TM
    starts, pads, meta = _moe_meta(counts)
    experts = jnp.arange(N_EXPERTS, dtype=jnp.int32)

    def position(e, r):
        sel = e.astype(jnp.int32)[:, None] == experts[None, :]
        return jnp.sum(jnp.where(sel, starts[None, :], 0), axis=1) + r.astype(jnp.int32)

    pos1 = position(route_t[0], route_t[4])
    pos2 = position(route_t[1], route_t[5])
    cpair = jnp.concatenate([jnp.stack([route_t[2], route_t[3]], axis=1).reshape(2 * T), jnp.zeros((2,), F32)])

    def wspec(shape, n):
        return pl.BlockSpec((None, None) + shape,
                            lambda j, *pre: (layer, pre[n - 1][jnp.minimum(j, MOE_TILES - 1)], 0, 0))

    gscr = pltpu.VMEM((tm * SUBLANES, LANES), F32)
    act, rowmap = pl.pallas_call(
        _moe_up_kernel,
        grid_spec=pltpu.PrefetchScalarGridSpec(
            num_scalar_prefetch=4, grid=(MOE_TILES,),
            in_specs=[pl.BlockSpec(memory_space=pl.ANY), wspec((D, D_EXPERT), 4), wspec((D, D_EXPERT), 4)],
            out_specs=[pl.BlockSpec((tm, D_EXPERT), lambda j, *pre: (j, 0)), pl.BlockSpec(memory_space=pltpu.SMEM)],
            scratch_shapes=[pltpu.VMEM((T * SUBLANES, LANES), F32), gscr, gscr,
                            pltpu.VMEM((D, D_EXPERT), BF16), pltpu.VMEM((D, D_EXPERT), BF16),
                            pltpu.SemaphoreType.DMA(())]),
        out_shape=[jax.ShapeDtypeStruct((MOE_ROWS, D_EXPERT), BF16), jax.ShapeDtypeStruct((MOE_ROWS,), jnp.int32)],
        compiler_params=_cparams(1),
        name=f"moe_up{layer}",
    )(pos1, pos2, pads, meta, tr, w_gate, w_up)

    final = final_w is not None
    any_spec = pl.BlockSpec(memory_space=pl.ANY)
    in_specs = [pl.BlockSpec((tm, D_EXPERT), lambda j, *pre: (jnp.minimum(j, MOE_TILES - 1), 0)),
                wspec((D_EXPERT, D), 3)]
    args = [rowmap, cpair, meta, act, w_down]
    scratch = [pltpu.VMEM((ACC_ROWS, LANES), F32), gscr, gscr, pltpu.VMEM((D_EXPERT, D), BF16)]
    if final:
        stage = pltpu.VMEM((2, MERGE_TM, D), F32)
        in_specs += [any_spec, pl.BlockSpec(mods.shape, lambda j, *pre: (0, 0, 0)),
                     pl.BlockSpec((1, D), lambda j, *pre: (0, 0))]
        args += [xmid, mods, final_w.reshape(1, D)]
        scratch += [stage, stage, pltpu.SemaphoreType.DMA((4,))]
        out_shape = [jax.ShapeDtypeStruct((T_CTX, D), F32), jax.ShapeDtypeStruct((T_LAT, D), F32)]
    else:
        out_shape = [jax.ShapeDtypeStruct((T * SUBLANES, LANES), F32)]
    return pl.pallas_call(
        functools.partial(_moe_down_kernel, layer=layer, final=final),
        grid_spec=pltpu.PrefetchScalarGridSpec(
            num_scalar_prefetch=3, grid=(MOE_TILES + 1,), in_specs=in_specs,
            out_specs=[any_spec] * len(out_shape), scratch_shapes=scratch),
        out_shape=out_shape,
        compiler_params=_cparams(1),
        name=f"moe_down{layer}",
    )(*args)


def _pad_lanes(v):
    return jnp.pad(v.astype(F32), (0, LANES - v.shape[0])).reshape(1, LANES)


def kernel(x_prompt, x_sample, cache_k_attn, cache_v_attn, state_ssd_fwd, state_ssd_bwd, state_gla_fwd, state_gla_bwd, c, c_ctx, w_ada, b_ada, norm_mix_w, norm_ffn_w, w_in_even, conv_w, conv_b, dt_bias_fwd, dt_bias_bwd, a_log_fwd, a_log_bwd, d_skip, ssd_norm_w, attn_sink, w_out_even, w_in_odd, w_gk2_fwd, b_gk_fwd, w_gk2_bwd, b_gk_bwd, gla_norm_w, w_out_odd, w_router_group, b_router_group, w_router_expert, b_router_expert, w_gate_exp, w_up_exp, w_down_exp, final_norm_w):
    depth = w_ada.shape[0]
    assert depth == 2 and x_prompt.shape == (BATCH, SEQ, D) and x_sample.shape == (DEC_BATCH, DEC_SEQ, D)

    cond8 = jnp.concatenate([c_ctx[None, :], c, jnp.zeros((SUBLANES - 1 - DEC_BATCH, D), F32)], axis=0)
    mods = _adaln(cond8, w_ada, b_ada).reshape(depth * SUBLANES * 6, 1, D)
    xs0 = (x_prompt.reshape(T_CTX, D), x_sample.reshape(T_LAT, D))

    def router_params(i):
        wr = jnp.concatenate([w_router_group[i], w_router_expert[i],
                              jnp.zeros((D, LANES - N_GROUPS - N_EXPERTS), F32)], axis=1)
        return wr, _pad_lanes(jnp.concatenate([b_router_group[i], b_router_expert[i]]))

    n_zxbc = 2 * D_SSD + 2 * SSD_GROUPS * SSD_N
    n_dt = 2 * SSD_HEADS
    half = PROJ_TN // 2
    r_bc, r_q = 2 * D_SSD, n_zxbc + n_dt
    r_kv = r_q + D
    rows0 = [(0, half), (D_SSD, D_SSD + half), (r_q, r_q + half), (r_bc, r_kv)]
    p0, p0dt = _modproj(xs0, mods, 0, norm_mix_w[0], jnp.transpose(w_in_even[0]), rows0, n_zxbc)

    dtb = _pad_lanes(jnp.concatenate([dt_bias_fwd[0], dt_bias_bwd[0]]))
    alog = _pad_lanes(jnp.concatenate([a_log_fwd[0], a_log_bwd[0]]))
    dskip = jnp.repeat(d_skip[0], SSD_P).reshape(1, D_SSD)
    ssd_args = (conv_w[0], conv_b[0], dtb, alog, dskip, ssd_norm_w[0])
    y_ssd_c, ssd_f, ssd_b = _ssd(p0, p0dt, 0, BATCH, SEQ, *ssd_args)
    y_ssd_l, _, _ = _ssd(p0, p0dt, T_CTX, DEC_BATCH, DEC_SEQ, *ssd_args,
                         h0f=state_ssd_fwd[:, 0].reshape(DEC_BATCH, D_SSD, SSD_N),
                         h0b=state_ssd_bwd[:, 0].reshape(DEC_BATCH, D_SSD, SSD_N))
    sink = _pad_lanes(attn_sink[0])
    y_att_c, new_kt, new_vt = _ctx_attn(p0, sink)
    def cache_in(t):
        return jnp.transpose(t[:, 0], (0, 2, 3, 1)).reshape(DEC_BATCH, ATT_KV_DIM, PAST_LEN)

    y_att_l = _lat_attn(p0, cache_in(cache_k_attn), cache_in(cache_v_attn), sink)
    xmid0, tr0, route0, cnt0 = _outproj_route([y_ssd_c, y_ssd_l, y_att_c, y_att_l], w_out_even[0], xs0, mods, 0,
                                              norm_ffn_w[0], *router_params(0))
    (moe0,) = _moe(tr0, route0, cnt0, 0, w_gate_exp, w_up_exp, w_down_exp)

    dk_all = GLA_HEADS * GLA_DK
    n_qkvr = 2 * dk_all + 2 * GLA_HEADS * GLA_DV
    n_odd = w_in_odd.shape[2]
    p1, p1lr, x1 = _modproj(xmid0, mods, 1, norm_mix_w[1], jnp.transpose(w_in_odd[0]),
                            [(k * PROJ_TN, k * PROJ_TN + half) for k in range(n_qkvr // PROJ_TN)], n_odd - LANES,
                            moe=moe0)
    lr0 = LANES - 2 * GLA_LOWRANK
    w2f = jnp.zeros((LANES, dk_all), F32).at[lr0:lr0 + GLA_LOWRANK].set(w_gk2_fwd[0])
    w2b = jnp.zeros((LANES, dk_all), F32).at[lr0 + GLA_LOWRANK:].set(w_gk2_bwd[0])
    gla_args = (w2f, w2b, b_gk_fwd[0], b_gk_bwd[0], gla_norm_w[0])
    y_gla_c, gla_f, gla_b = _gla(p1, p1lr, 0, BATCH, SEQ, *gla_args)
    y_gla_l, _, _ = _gla(p1, p1lr, T_CTX, DEC_BATCH, DEC_SEQ, *gla_args,
                         s0f=state_gla_fwd[:, 0].reshape(DEC_BATCH, dk_all, GLA_DV),
                         s0b=state_gla_bwd[:, 0].reshape(DEC_BATCH, dk_all, GLA_DV))
    xmid1, tr1, route1, cnt1 = _outproj_route([y_gla_c, y_gla_l], w_out_odd[0], (x1,), mods, 1,
                                              norm_ffn_w[1], *router_params(1))
    y_c, y_l = _moe(tr1, route1, cnt1, 1, w_gate_exp, w_up_exp, w_down_exp, xmid1, mods, final_w=final_norm_w)

    y_prompt = y_c.reshape(BATCH, SEQ, D)
    y_sample = y_l.reshape(DEC_BATCH, DEC_SEQ, D)
    def cache_out(t):
        return jnp.transpose(t.reshape(BATCH, 1, ATT_KV, ATT_HD, SEQ), (0, 1, 4, 2, 3))

    new_k, new_v = cache_out(new_kt), cache_out(new_vt)
    return (y_prompt, y_sample, new_k, new_v,
            ssd_f.reshape(BATCH, 1, SSD_HEADS, SSD_P, SSD_N), ssd_b.reshape(BATCH, 1, SSD_HEADS, SSD_P, SSD_N),
            gla_f.reshape(BATCH, 1, GLA_HEADS, GLA_DK, GLA_DV), gla_b.reshape(BATCH, 1, GLA_HEADS, GLA_DK, GLA_DV))
```

```python
import functools

import numpy as np
import jax
import jax.numpy as jnp
from jax import lax
from jax.experimental import pallas as pl
from jax.experimental.pallas import tpu as pltpu

F32 = jnp.float32
BF16 = jnp.bfloat16

D = 1024
BATCH, SEQ = 16, 256
DEC_BATCH, DEC_SEQ = 2, 1024
PAST_LEN = 512
GRID_W = 64
EPS = 1e-6
T_CTX = BATCH * SEQ
T_LAT = DEC_BATCH * DEC_SEQ
T = T_CTX + T_LAT

SSD_HEADS, SSD_P, SSD_N, SSD_GROUPS = 16, 64, 128, 2
SSD_CONV = 5
SSD_L = 128
D_SSD = SSD_HEADS * SSD_P
HEADS_PER_GROUP = SSD_HEADS // SSD_GROUPS
GROUP_W = HEADS_PER_GROUP * SSD_P

ATT_HEADS, ATT_KV, ATT_HD = 16, 4, 64
ATT_KV_DIM = ATT_KV * ATT_HD
WINDOW = 128
ATT_BLOCK = 128
ATT_SCALE = ATT_HD ** -0.5
ROPE_BASE = 10000.0

GLA_HEADS, GLA_DK, GLA_DV = 4, 128, 256
GLA_C = 64
GLA_GATE_NORM = 16.0
GLA_LOWRANK = 16

N_GROUPS, EXP_PER_GROUP = 4, 4
N_EXPERTS = 16
D_EXPERT = 512

LANES = 128
SUBLANES = 8
VMEM_LIMIT = 56 * 1024 * 1024

P0_Z, P0_X, P0_Q, P0_BC, P0_K, P0_V = 0, 1024, 2048, 3072, 3584, 3840
BC_W = 2 * SSD_GROUPS * SSD_N
P1_Q, P1_K, P1_V, P1_R = 0, 512, 1024, 2048

MOE_TM = 512
MOE_TILES = (2 * T) // MOE_TM + N_EXPERTS
MOE_ROWS = MOE_TILES * MOE_TM
ACC_ROWS = (T + SUBLANES) * SUBLANES
MERGE_TM = 512


def _cparams(n_axes, vmem=VMEM_LIMIT):
    return pltpu.CompilerParams(dimension_semantics=("arbitrary",) * n_axes, vmem_limit_bytes=vmem)


def _silu(x):
    return x / (1.0 + jnp.exp(-x))


def _softplus(x):
    return jnp.maximum(x, 0.0) + jnp.log(1.0 + jnp.exp(-jnp.abs(x)))


def _mm(a, b):
    return jnp.dot(a.astype(BF16), b.astype(BF16), preferred_element_type=F32)


def _mm_nt(a, b):
    return lax.dot_general(a.astype(BF16), b.astype(BF16), (((1,), (1,)), ((), ())),
                           preferred_element_type=F32)


def _mm_tn(a, b):
    return lax.dot_general(a.astype(BF16), b.astype(BF16), (((0,), (0,)), ((), ())),
                           preferred_element_type=F32)


def _rms(x, w):
    return x * lax.rsqrt(jnp.mean(x * x, axis=-1, keepdims=True) + EPS) * w


def _cumsum_rows(x, n):
    row = lax.broadcasted_iota(jnp.int32, x.shape, 0)
    s = 1
    while s < n:
        x = x + jnp.where(row >= s, pltpu.roll(x, s, 0), 0.0)
        s *= 2
    return x


def _mod_row(tok0):
    return jnp.where(tok0 < T_CTX, 0, 1 + (tok0 - T_CTX) // DEC_SEQ)


ADA_TN = 1536


def _adaln_kernel(c_ref, w_ref, b_ref, o_ref):
    s = _silu(c_ref[...])
    o_ref[0] = _mm(s, w_ref[0]) + b_ref[0]


def _adaln(cond8, w_ada, b_ada):
    depth = w_ada.shape[0]
    return pl.pallas_call(
        _adaln_kernel,
        grid=(depth, 6 * D // ADA_TN),
        in_specs=[
            pl.BlockSpec((SUBLANES, D), lambda l, j: (0, 0)),
            pl.BlockSpec((1, D, ADA_TN), lambda l, j: (l, 0, j)),
            pl.BlockSpec((1, 1, ADA_TN), lambda l, j: (l, 0, j)),
        ],
        out_specs=pl.BlockSpec((1, SUBLANES, ADA_TN), lambda l, j: (l, 0, j)),
        out_shape=jax.ShapeDtypeStruct((depth, SUBLANES, 6 * D), F32),
        compiler_params=_cparams(2),
        name="adaln",
    )(cond8, w_ada, b_ada.reshape(depth, 1, 6 * D))


def _mod_spec(layer, chunk, tm, tile_of=lambda i, *_: i):
    return pl.BlockSpec((1, 1, D), lambda *g: ((layer * SUBLANES + _mod_row(tile_of(*g) * tm)) * 6 + chunk, 0, 0))


def _tokmajor_to_std(ref, tm):
    return jnp.concatenate([ref[pl.ds(k, tm, stride=SUBLANES), :] for k in range(D // LANES)], axis=1)


PROJ_TM = 1024
PROJ_TN = 1024


def _ctx_lat_specs(tm, width=D):
    n_ctx = T_CTX // tm
    return (pl.BlockSpec((tm, width), lambda i, *_: (jnp.minimum(i, n_ctx - 1), 0)),
            pl.BlockSpec((tm, width), lambda i, *_: (jnp.maximum(i - n_ctx, 0), 0)))


def _modproj_kernel(*refs, dual_x):
    it = iter(refs)
    if dual_x:
        xc_ref, xl_ref = next(it), next(it)
    else:
        x_ref, moe_ref, g2_ref = next(it), next(it), next(it)
    sh_ref, sc_ref, nw_ref, wlo_ref, whi_ref, ws_ref, o_ref, os_ref = (next(it) for _ in range(8))
    xo_ref = None if dual_x else next(it)
    h_all, w_bf = next(it), next(it)
    j, i = pl.program_id(0), pl.program_id(1)
    tm = PROJ_TM
    rows = pl.ds(pl.multiple_of(i * tm, tm), tm)

    @pl.when(j == 0)
    def _():
        if dual_x:
            x = jnp.where(i < T_CTX // tm, xc_ref[...], xl_ref[...])
        else:
            x = x_ref[...] + g2_ref[0] * _tokmajor_to_std(moe_ref, tm)
            xo_ref[...] = x
        h = (_rms(x, nw_ref[...]) * (1.0 + sc_ref[0]) + sh_ref[0]).astype(BF16)
        h_all[rows, :] = h
        os_ref[...] = _mm_nt(h, ws_ref[...])

    @pl.when(i == 0)
    def _():
        half = wlo_ref.shape[0]
        w_bf[0:half, :] = wlo_ref[...].astype(BF16)
        w_bf[half:2 * half, :] = whi_ref[...].astype(BF16)

    o_ref[...] = _mm_nt(h_all[rows, :], w_bf[...]).astype(o_ref.dtype)


def _modproj(xs, mods, layer, norm_w, wt, tile_rows, small_row, moe=None):
    tm, tn = PROJ_TM, PROJ_TN
    dual_x = moe is None
    n_tiles = len(tile_rows)
    n_i, n_ctx = T // tm, T_CTX // tm

    def w_row(side):
        def index(j, i):
            r = jnp.int32(tile_rows[0][side])
            for k in range(1, n_tiles):
                r = jnp.where(j == k, tile_rows[k][side], r)
            return pl.multiple_of(r, SUBLANES), 0
        return index

    def tok(j, i):
        return jnp.where(j == 0, i, n_i - 1)

    tile = pl.BlockSpec((tm, D), lambda j, i: (tok(j, i), 0))
    if dual_x:
        in_specs = [pl.BlockSpec((tm, D), lambda j, i: (jnp.minimum(tok(j, i), n_ctx - 1), 0)),
                    pl.BlockSpec((tm, D), lambda j, i: (jnp.maximum(tok(j, i) - n_ctx, 0), 0))]
        args = list(xs)
    else:
        in_specs = [tile, pl.BlockSpec((tm * SUBLANES, LANES), lambda j, i: (tok(j, i), 0)),
                    _mod_spec(layer - 1, 5, tm, tok)]
        args = [xs, moe, mods]
    in_specs += [_mod_spec(layer, 0, tm, tok), _mod_spec(layer, 1, tm, tok), pl.BlockSpec((1, D), lambda j, i: (0, 0)),
                 pl.BlockSpec((pl.Element(tn // 2), pl.Element(D)), w_row(0)),
                 pl.BlockSpec((pl.Element(tn // 2), pl.Element(D)), w_row(1)),
                 pl.BlockSpec((pl.Element(LANES), pl.Element(D)), lambda j, i: (small_row, 0))]
    args += [mods, mods, norm_w.reshape(1, D), wt, wt, wt]
    out_specs = [pl.BlockSpec((tm, tn), lambda j, i: (i, j)),
                 pl.BlockSpec((tm, LANES), lambda j, i: (tok(j, i), 0))]
    out_shape = [jax.ShapeDtypeStruct((T, n_tiles * tn), BF16), jax.ShapeDtypeStruct((T, LANES), F32)]
    if not dual_x:
        out_specs.append(tile)
        out_shape.append(jax.ShapeDtypeStruct((T, D), F32))
    return pl.pallas_call(
        functools.partial(_modproj_kernel, dual_x=dual_x),
        grid=(n_tiles, n_i), in_specs=in_specs, out_specs=out_specs, out_shape=out_shape,
        scratch_shapes=[pltpu.VMEM((T, D), BF16), pltpu.VMEM((tn, D), BF16)],
        compiler_params=_cparams(2),
        name=f"modproj{layer}",
    )(*args)


def _expand_heads(v, off):
    hi = (lax.broadcasted_iota(jnp.int32, (v.shape[0], LANES), 1) >= SSD_P).astype(jnp.int32)
    tiles = [jnp.take_along_axis(v, hi + (off + 2 * q), axis=1) for q in range(SSD_HEADS // 2)]
    return jnp.concatenate(tiles, axis=1)


def _ssd_kernel(*refs, seq, has_h0):
    if has_h0:
        (z_ref, x_ref, bc_ref, dt_ref, cwx_ref, cwbc_ref, cbx_ref, cbbc_ref, dtb_ref, alog_ref, dsk_ref,
         nw_ref, h0f_ref, h0b_ref, y_ref, sf_ref, sb_ref,
         xpad, bcpad, xc, bcc, a_scr, dt_scr, yf, yb, hf, hb) = refs
    else:
        (z_ref, x_ref, bc_ref, dt_ref, cwx_ref, cwbc_ref, cbx_ref, cbbc_ref, dtb_ref, alog_ref, dsk_ref,
         nw_ref, y_ref, sf_ref, sb_ref,
         xpad, bcpad, xc, bcc, a_scr, dt_scr, yf, yb, hf, hb) = refs
    L = SSD_L
    nc = seq // L
    pad = SUBLANES
    half = SSD_CONV // 2

    for buf, src, cw, cb, dst in ((xpad, x_ref, cwx_ref, cbx_ref, xc), (bcpad, bc_ref, cwbc_ref, cbbc_ref, bcc)):
        width = buf.shape[1]
        buf[0:pad, :] = jnp.zeros((pad, width), F32)
        buf[pad + seq:2 * pad + seq, :] = jnp.zeros((pad, width), F32)
        buf[pad:pad + seq, :] = src[...].astype(F32)
        for blk in range(nc):
            acc = jnp.broadcast_to(cb[...], (L, width))
            for j in range(SSD_CONV):
                r0 = pad - half + j + blk * L
                acc = acc + cw[j:j + 1, :] * buf[r0:r0 + L, :]
            dst[blk * L:(blk + 1) * L, :] = _silu(acc)

    lane = lax.broadcasted_iota(jnp.int32, (seq, LANES), 1)
    dts = jnp.where(lane < 2 * SSD_HEADS, _softplus(dt_ref[...] + dtb_ref[...]), 0.0)
    dt_scr[...] = dts
    a_scr[...] = dts * (-jnp.exp(alog_ref[...]))

    if has_h0:
        hf[...] = h0f_ref[...]
        hb[...] = h0b_ref[...]
    else:
        hf[...] = jnp.zeros(hf.shape, F32)
        hb[...] = jnp.zeros(hb.shape, F32)

    row = lax.broadcasted_iota(jnp.int32, (L, L), 0)
    col = lax.broadcasted_iota(jnp.int32, (L, L), 1)
    lane_l = lax.broadcasted_iota(jnp.int32, (L, LANES), 1)
    lo_half = lane_l < SSD_P

    def chunk(c, fwd, h_scr):
        off = 0 if fwd else SSD_HEADS
        r0 = pl.multiple_of(c * L, L)
        a = a_scr[pl.ds(r0, L), :]
        dt = dt_scr[pl.ds(r0, L), :]
        cs = _cumsum_rows(a, L)
        total = cs[L - 1:L, :]
        if fwd:
            u = cs
            rvec = jnp.exp(cs)
            ed = jnp.exp(total - cs) * dt
            keep = col <= row
        else:
            ex = cs - a
            u = -ex
            rvec = jnp.exp(total - ex)
            ed = jnp.exp(ex) * dt
            keep = col >= row
        ut = jnp.transpose(u)
        dtt = jnp.transpose(dt)
        tcol = jnp.transpose(jnp.broadcast_to(total, (L, LANES)))[:, 0:1]
        rexp = _expand_heads(rvec, off)
        edexp = _expand_heads(ed, off)
        x = xc[pl.ds(r0, L), :]
        bc = bcc[pl.ds(r0, L), :]
        outs = []
        for g in range(SSD_GROUPS):
            bg = bc[:, g * SSD_N:(g + 1) * SSD_N]
            cg = bc[:, SSD_GROUPS * SSD_N + g * SSD_N:SSD_GROUPS * SSD_N + (g + 1) * SSD_N]
            cbm = _mm_nt(cg, bg)
            hg = h_scr[g * GROUP_W:(g + 1) * GROUP_W, :]
            xg = x[:, g * GROUP_W:(g + 1) * GROUP_W]
            y_off = _mm_nt(cg, hg) * rexp[:, g * GROUP_W:(g + 1) * GROUP_W]
            tiles = []
            for p in range(HEADS_PER_GROUP // 2):
                xt = xg[:, p * LANES:(p + 1) * LANES]
                acc = None
                for s in range(2):
                    h = off + g * HEADS_PER_GROUP + 2 * p + s
                    seg = u[:, h:h + 1] - ut[h:h + 1, :]
                    m = cbm * jnp.exp(jnp.where(keep, seg, -jnp.inf)) * dtt[h:h + 1, :]
                    xm = jnp.where(lo_half if s == 0 else jnp.logical_not(lo_half), xt, 0.0)
                    d = _mm(m, xm)
                    acc = d if acc is None else acc + d
                tiles.append(acc)
            outs.append(y_off + jnp.concatenate(tiles, axis=1))
            decs = []
            for hh in range(HEADS_PER_GROUP):
                h = off + g * HEADS_PER_GROUP + hh
                decs.append(jnp.broadcast_to(jnp.exp(tcol[h:h + 1, :]), (SSD_P, SSD_N)))
            dec = jnp.concatenate(decs, axis=0)
            h_scr[g * GROUP_W:(g + 1) * GROUP_W, :] = dec * hg + _mm_tn(xg * edexp[:, g * GROUP_W:(g + 1) * GROUP_W], bg)
        return r0, jnp.concatenate(outs, axis=1)

    def body(i, carry):
        r0, y = chunk(i, True, hf)
        yf[pl.ds(r0, L), :] = y
        r0, y = chunk(nc - 1 - i, False, hb)
        yb[pl.ds(r0, L), :] = y
        return carry

    lax.fori_loop(0, nc, body, 0, unroll=2)
    sf_ref[...] = hf[...]
    sb_ref[...] = hb[...]

    for blk in range(nc):
        rs = slice(blk * L, (blk + 1) * L)
        y = yf[rs, :] + yb[rs, :] + dsk_ref[...] * xc[rs, :]
        y = y * _silu(z_ref[rs, :].astype(F32))
        y_ref[rs, :] = _rms(y, nw_ref[...]).astype(y_ref.dtype)


def _ssd(p0, p0dt, tok0, nseq, seq, cw, cb, dtb, alog, dskip, nw, h0f=None, h0b=None):
    has_h0 = h0f is not None
    b0 = tok0 // seq

    def cols(width, start):
        return pl.BlockSpec((seq, width), lambda s: (b0 + s, start // width))

    def full(shape):
        return pl.BlockSpec(shape, lambda s: (0,) * len(shape))

    in_specs = [cols(D_SSD, P0_Z), cols(D_SSD, P0_X), cols(BC_W, P0_BC), cols(LANES, 0),
                full((SSD_CONV, D_SSD)), full((SSD_CONV, BC_W)), full((1, D_SSD)), full((1, BC_W)),
                full((1, LANES)), full((1, LANES)), full((1, D_SSD)), full((1, D_SSD))]
    args = [p0, p0, p0, p0dt, cw[:, :D_SSD], cw[:, D_SSD:], cb[:D_SSD].reshape(1, -1), cb[D_SSD:].reshape(1, -1),
            dtb, alog, dskip, nw.reshape(1, -1)]
    st_spec = pl.BlockSpec((None, D_SSD, SSD_N), lambda s: (s, 0, 0))
    if has_h0:
        in_specs += [st_spec, st_spec]
        args += [h0f, h0b]
    st_shape = jax.ShapeDtypeStruct((nseq, D_SSD, SSD_N), F32)
    return pl.pallas_call(
        functools.partial(_ssd_kernel, seq=seq, has_h0=has_h0),
        grid=(nseq,), in_specs=in_specs,
        out_specs=[pl.BlockSpec((seq, D_SSD), lambda s: (s, 0)), st_spec, st_spec],
        out_shape=[jax.ShapeDtypeStruct((nseq * seq, D_SSD), BF16), st_shape, st_shape],
        scratch_shapes=[pltpu.VMEM((seq + 2 * SUBLANES, D_SSD), F32), pltpu.VMEM((seq + 2 * SUBLANES, BC_W), F32),
                        pltpu.VMEM((seq, D_SSD), F32), pltpu.VMEM((seq, BC_W), F32),
                        pltpu.VMEM((seq, LANES), F32), pltpu.VMEM((seq, LANES), F32),
                        pltpu.VMEM((seq, D_SSD), F32), pltpu.VMEM((seq, D_SSD), F32),
                        pltpu.VMEM((D_SSD, SSD_N), F32), pltpu.VMEM((D_SSD, SSD_N), F32)],
        compiler_params=_cparams(1),
        name=f"ssd{seq}",
    )(*args)


def _place_halves(tile, kv_in_high):
    lo = lax.broadcasted_iota(jnp.int32, tile.shape, 1) < ATT_HD
    swapped = pltpu.roll(tile, ATT_HD, 1)
    if kv_in_high:
        return jnp.where(lo, swapped, 0.0), jnp.where(lo, 0.0, tile)
    return jnp.where(lo, tile, 0.0), jnp.where(lo, 0.0, swapped)


def _place_rows(vt, kv_in_high):
    head = vt[ATT_HD:, :] if kv_in_high else vt[:ATT_HD, :]
    z = jnp.zeros_like(head)
    return jnp.concatenate([head, z], axis=0), jnp.concatenate([z, head], axis=0)


LOG2E = 1.4426950408889634
SCORE_SCALE = ATT_SCALE * LOG2E


def _sink_attend_t(score_parts, value_parts, sink2):
    m = sink2
    for s in score_parts:
        m = jnp.maximum(m, jnp.max(s, axis=0, keepdims=True))
    denom = jnp.exp2(sink2 - m)
    out = None
    for s, v in zip(score_parts, value_parts):
        p = jnp.exp2(s - m)
        denom = denom + jnp.sum(p, axis=0, keepdims=True)
        o = _mm(v, p)
        out = o if out is None else out + o
    return out * (1.0 / denom)


def _attn_schedule(n, scores, attend):
    scores(0)
    for j in range(n):
        if j + 1 < n:
            scores(j + 1)
        attend(j)


def _ctx_attn_kernel(q_ref, k_ref, v_ref, sink_ref, o_ref, kt_ref, vt_ref, s_a, s_b):
    sink2 = sink_ref[...] * LOG2E
    bufs = (s_a, s_b)
    half = SEQ // 2
    for t in range(ATT_KV_DIM // LANES):
        cols = slice(t * LANES, (t + 1) * LANES)
        kt_ref[cols, :] = jnp.transpose(k_ref[:, cols].astype(F32))
        vt_ref[cols, :] = jnp.transpose(v_ref[:, cols].astype(F32))

    def kv_tile(ref, j):
        return ref[:, (j // 2) * LANES:(j // 2 + 1) * LANES].astype(F32), (j % 2 == 1)

    def scores(j):
        k_lo, k_hi = _place_halves(*kv_tile(k_ref, j))
        qst = jnp.concatenate([q_ref[:, qt * LANES:(qt + 1) * LANES] for qt in (2 * j, 2 * j + 1)], axis=0)
        bufs[j % 2][...] = _mm_nt(jnp.concatenate([k_lo, k_hi], axis=0), qst) * SCORE_SCALE

    def attend(j):
        src = bufs[j % 2]
        v, high = kv_tile(v_ref, j)
        vts = _place_rows(jnp.transpose(v), high)
        for ql in range(2):
            qt = 2 * j + ql
            for qh in range(2):
                cols = slice(ql * SEQ + qh * half, ql * SEQ + (qh + 1) * half)
                acc = None
                for s, vv in enumerate(vts):
                    o = _sink_attend_t([src[s * SEQ:(s + 1) * SEQ, cols]], [vv], sink2[:, 2 * qt + s:2 * qt + s + 1])
                    acc = o if acc is None else acc + o
                o_ref[qh * half:(qh + 1) * half, qt * LANES:(qt + 1) * LANES] = jnp.transpose(acc).astype(o_ref.dtype)

    _attn_schedule(ATT_KV, scores, attend)


def _ctx_attn(p0, sink):
    def cols(width, start):
        return pl.BlockSpec((SEQ, width), lambda b: (b, start // width))

    sbuf = pltpu.VMEM((2 * SEQ, 2 * SEQ), F32)
    return pl.pallas_call(
        _ctx_attn_kernel,
        grid=(BATCH,),
        in_specs=[cols(D, P0_Q), cols(ATT_KV_DIM, P0_K), cols(ATT_KV_DIM, P0_V),
                  pl.BlockSpec((1, LANES), lambda b: (0, 0))],
        out_specs=[pl.BlockSpec((SEQ, D), lambda b: (b, 0))]
        + [pl.BlockSpec((None, ATT_KV_DIM, SEQ), lambda b: (b, 0, 0))] * 2,
        out_shape=[jax.ShapeDtypeStruct((T_CTX, D), BF16)] + [jax.ShapeDtypeStruct((BATCH, ATT_KV_DIM, SEQ), F32)] * 2,
        scratch_shapes=[sbuf, sbuf],
        compiler_params=_cparams(1),
        name="ctx_attn",
    )(p0, p0, p0, sink)


def _rope_tables():
    quarter = ATT_HD // 4
    t = np.arange(DEC_SEQ)
    lane = np.arange(LANES)
    inv = ROPE_BASE ** (-(lane % quarter).astype(np.float64) / quarter)
    pos = np.where(((lane % ATT_HD) < ATT_HD // 2)[None, :], (t // GRID_W)[:, None], (t % GRID_W)[:, None])
    ang = pos * inv[None, :]
    first = ((lane % (2 * quarter)) < quarter)[None, :]
    cos, sin = np.cos(ang), np.sin(ang)
    return (jnp.asarray(cos, F32), jnp.asarray(np.where(first, -sin, 0.0), F32),
            jnp.asarray(np.where(first, 0.0, sin), F32))


def _rope(x, cos, sa, sb):
    quarter = ATT_HD // 4
    return x * cos + pltpu.roll(x, LANES - quarter, 1) * sa + pltpu.roll(x, quarter, 1) * sb


def _lat_attn_kernel(q_ref, kp_ref, kc_ref, kn_ref, vp_ref, vc_ref, vn_ref, ck_ref, cv_ref,
                     cos_ref, sa_ref, sb_ref, sink_ref, o_ref, c_a, c_b, w_a, w_b):
    blk = pl.program_id(1)
    nb = pl.num_programs(1)
    B = ATT_BLOCK
    sink2 = sink_ref[...] * LOG2E
    cbufs, wbufs = (c_a, c_b), (w_a, w_b)

    def tables(b):
        r0 = pl.multiple_of(b * B, B)
        return cos_ref[pl.ds(r0, B), :], sa_ref[pl.ds(r0, B), :], sb_ref[pl.ds(r0, B), :]

    tq = tables(blk)
    tk = [tables(jnp.maximum(blk - 1, 0)), tq, tables(jnp.minimum(blk + 1, nb - 1))]
    kabs = (blk - 1) * B + lax.broadcasted_iota(jnp.int32, (3 * B, B), 0)
    qpos = blk * B + lax.broadcasted_iota(jnp.int32, (3 * B, B), 1)
    ok = (jnp.abs(qpos - kabs) <= WINDOW) & (kabs >= 0) & (kabs < nb * B)
    ok = jnp.concatenate([ok, ok], axis=1)
    ok = jnp.concatenate([ok, ok], axis=0)

    def scores(j):
        high = (j % 2 == 1)
        sl = slice((j // 2) * LANES, (j // 2 + 1) * LANES)
        kw = jnp.concatenate([_rope(r[:, sl].astype(F32), *tb) for r, tb in zip((kp_ref, kc_ref, kn_ref), tk)], axis=0)
        qs = [q_ref[:, qt * LANES:(qt + 1) * LANES].astype(F32) for qt in (2 * j, 2 * j + 1)]
        q_plain = jnp.concatenate(qs, axis=0)
        q_rope = jnp.concatenate([_rope(q, *tq) for q in qs], axis=0)
        ck = jnp.transpose(ck_ref[sl, :])
        cbufs[j % 2][...] = _mm_nt(jnp.concatenate(_place_halves(ck, high), axis=0), q_plain) * SCORE_SCALE
        win = _mm_nt(jnp.concatenate(_place_halves(kw, high), axis=0), q_rope) * SCORE_SCALE
        wbufs[j % 2][...] = jnp.where(ok, win, -jnp.inf)

    def attend(j):
        high = (j % 2 == 1)
        sl = slice((j // 2) * LANES, (j // 2 + 1) * LANES)
        vw = jnp.concatenate([jnp.transpose(r[:, sl].astype(F32)) for r in (vp_ref, vc_ref, vn_ref)], axis=1)
        vts = _place_rows(vw, high)
        cvts = _place_rows(cv_ref[sl, :], high)
        csrc, wsrc = cbufs[j % 2], wbufs[j % 2]
        for ql in range(2):
            qt = 2 * j + ql
            cols = slice(ql * B, (ql + 1) * B)
            acc = None
            for s in range(2):
                parts = [csrc[s * PAST_LEN:(s + 1) * PAST_LEN, cols], wsrc[s * 3 * B:(s + 1) * 3 * B, cols]]
                o = _sink_attend_t(parts, [cvts[s], vts[s]], sink2[:, 2 * qt + s:2 * qt + s + 1])
                acc = o if acc is None else acc + o
            o_ref[:, qt * LANES:(qt + 1) * LANES] = jnp.transpose(acc).astype(o_ref.dtype)

    _attn_schedule(ATT_KV, scores, attend)


def _lat_attn(p0, ck, cv, sink):
    nb = DEC_SEQ // ATT_BLOCK
    base = T_CTX // ATT_BLOCK

    def kv(start, shift):
        return pl.BlockSpec((ATT_BLOCK, ATT_KV_DIM),
                            lambda b, i: (base + b * nb + jnp.clip(i + shift, 0, nb - 1), start // ATT_KV_DIM))

    def full(shape):
        return pl.BlockSpec(shape, lambda b, i: (0,) * len(shape))

    cache = pl.BlockSpec((None, ATT_KV_DIM, PAST_LEN), lambda b, i: (b, 0, 0))
    cos, sa, sb = _rope_tables()
    return pl.pallas_call(
        _lat_attn_kernel,
        grid=(DEC_BATCH, nb),
        in_specs=[pl.BlockSpec((ATT_BLOCK, D), lambda b, i: (base + b * nb + i, P0_Q // D)),
                  kv(P0_K, -1), kv(P0_K, 0), kv(P0_K, 1), kv(P0_V, -1), kv(P0_V, 0), kv(P0_V, 1),
                  cache, cache, full((DEC_SEQ, LANES)), full((DEC_SEQ, LANES)), full((DEC_SEQ, LANES)),
                  full((1, LANES))],
        out_specs=pl.BlockSpec((ATT_BLOCK, D), lambda b, i: (b * nb + i, 0)),
        out_shape=jax.ShapeDtypeStruct((T_LAT, D), BF16),
        scratch_shapes=[pltpu.VMEM((2 * PAST_LEN, 2 * ATT_BLOCK), F32)] * 2
        + [pltpu.VMEM((2 * 3 * ATT_BLOCK, 2 * ATT_BLOCK), F32)] * 2,
        compiler_params=_cparams(2),
        name="lat_attn",
    )(p0, p0, p0, p0, p0, p0, p0, ck, cv, cos, sa, sb, sink)


def _log_sigmoid(x):
    return jnp.minimum(x, 0.0) - jnp.log(1.0 + jnp.exp(-jnp.abs(x)))


def _gla_kernel(*refs, seq, has_s0):
    if has_s0:
        (q_ref, k_ref, v_ref, r_ref, lr_ref, w2f_ref, w2b_ref, bf_ref, bb_ref, nw_ref, s0f_ref, s0b_ref,
         y_ref, sf_ref, sb_ref, gf, gb, yf, yb, stf, stb) = refs
    else:
        (q_ref, k_ref, v_ref, r_ref, lr_ref, w2f_ref, w2b_ref, bf_ref, bb_ref, nw_ref,
         y_ref, sf_ref, sb_ref, gf, gb, yf, yb, stf, stb) = refs
    C = GLA_C
    nc = seq // C
    lr = lr_ref[...]
    gf[...] = _log_sigmoid(_mm(lr, w2f_ref[...]) + bf_ref[...]) / GLA_GATE_NORM
    gb[...] = _log_sigmoid(_mm(lr, w2b_ref[...]) + bb_ref[...]) / GLA_GATE_NORM
    for h in range(GLA_HEADS):
        rows = slice(h * GLA_DV, (h + 1) * GLA_DV)
        if has_s0:
            stf[rows, :] = jnp.transpose(s0f_ref[h * GLA_DK:(h + 1) * GLA_DK, :])
            stb[rows, :] = jnp.transpose(s0b_ref[h * GLA_DK:(h + 1) * GLA_DK, :])
        else:
            stf[rows, :] = jnp.zeros((GLA_DV, GLA_DK), F32)
            stb[rows, :] = jnp.zeros((GLA_DV, GLA_DK), F32)

    row = lax.broadcasted_iota(jnp.int32, (C, C), 0)
    col = lax.broadcasted_iota(jnp.int32, (C, C), 1)
    qscale = GLA_DK ** -0.5

    def chunk(c, fwd):
        g_scr, y_scr, st = (gf, yf, stf) if fwd else (gb, yb, stb)
        r0 = pl.multiple_of(c * C, C)
        g = g_scr[pl.ds(r0, C), :]
        cs = _cumsum_rows(g, C)
        total = cs[C - 1:C, :]
        q = q_ref[pl.ds(r0, C), :].astype(F32) * qscale
        k = k_ref[pl.ds(r0, C), :].astype(F32)
        v = v_ref[pl.ds(r0, C), :]
        if fwd:
            qs, ks, ke = q * jnp.exp(cs), k * jnp.exp(-cs), k * jnp.exp(total - cs)
            keep = col <= row
        else:
            ex = cs - g
            qs, ks, ke = q * jnp.exp(total - ex), k * jnp.exp(ex - total), k * jnp.exp(ex)
            keep = col >= row
        dec = jnp.exp(total)
        for h in range(GLA_HEADS):
            kc = slice(h * GLA_DK, (h + 1) * GLA_DK)
            vc = slice(h * GLA_DV, (h + 1) * GLA_DV)
            s_t = st[vc, :]
            att = jnp.where(keep, _mm_nt(qs[:, kc], ks[:, kc]), 0.0)
            y_scr[pl.ds(r0, C), vc] = _mm(att, v[:, vc]) + _mm_nt(qs[:, kc], s_t)
            st[vc, :] = dec[:, kc] * s_t + _mm_tn(v[:, vc], ke[:, kc])

    def body(i, carry):
        chunk(i, True)
        chunk(nc - 1 - i, False)
        return carry

    lax.fori_loop(0, nc, body, 0, unroll=4)

    nw = nw_ref[...]
    for blk in range(seq // LANES):
        rs = slice(blk * LANES, (blk + 1) * LANES)
        y = yf[rs, :] + yb[rs, :]
        gate = _silu(r_ref[rs, :].astype(F32))
        for h in range(GLA_HEADS):
            vc = slice(h * GLA_DV, (h + 1) * GLA_DV)
            y_ref[rs, vc] = (_rms(y[:, vc], nw) * gate[:, vc]).astype(y_ref.dtype)
    for h in range(GLA_HEADS):
        rows = slice(h * GLA_DV, (h + 1) * GLA_DV)
        sf_ref[h * GLA_DK:(h + 1) * GLA_DK, :] = jnp.transpose(stf[rows, :])
        sb_ref[h * GLA_DK:(h + 1) * GLA_DK, :] = jnp.transpose(stb[rows, :])


def _gla(p1, p1lr, tok0, nseq, seq, w2f, w2b, bgf, bgb, nw, s0f=None, s0b=None):
    has_s0 = s0f is not None
    b0 = tok0 // seq
    dk_all, dv_all = GLA_HEADS * GLA_DK, GLA_HEADS * GLA_DV

    def cols(width, start):
        return pl.BlockSpec((seq, width), lambda s: (b0 + s, start // width))

    def full(shape):
        return pl.BlockSpec(shape, lambda s: (0,) * len(shape))

    in_specs = [cols(dk_all, P1_Q), cols(dk_all, P1_K), cols(dv_all, P1_V), cols(dv_all, P1_R), cols(LANES, 0),
                full((LANES, dk_all)), full((LANES, dk_all)), full((1, dk_all)), full((1, dk_all)), full((1, GLA_DV))]
    args = [p1, p1, p1, p1, p1lr, w2f, w2b, bgf.reshape(1, -1), bgb.reshape(1, -1), nw.reshape(1, -1)]
    st_spec = pl.BlockSpec((None, dk_all, GLA_DV), lambda s: (s, 0, 0))
    if has_s0:
        in_specs += [st_spec, st_spec]
        args += [s0f, s0b]
    st_shape = jax.ShapeDtypeStruct((nseq, dk_all, GLA_DV), F32)
    return pl.pallas_call(
        functools.partial(_gla_kernel, seq=seq, has_s0=has_s0),
        grid=(nseq,), in_specs=in_specs,
        out_specs=[pl.BlockSpec((seq, dv_all), lambda s: (s, 0)), st_spec, st_spec],
        out_shape=[jax.ShapeDtypeStruct((nseq * seq, dv_all), BF16), st_shape, st_shape],
        scratch_shapes=[pltpu.VMEM((seq, dk_all), F32), pltpu.VMEM((seq, dk_all), F32),
                        pltpu.VMEM((seq, dv_all), F32), pltpu.VMEM((seq, dv_all), F32),
                        pltpu.VMEM((dv_all, GLA_DK), F32), pltpu.VMEM((dv_all, GLA_DK), F32)],
        compiler_params=_cparams(1),
        name=f"gla{seq}",
    )(*args)


ROUTE_TM = 512
ROUTE_SUB = 256
ROUTE_ROWS = 32


def _split_bf16(x):
    hi = x.astype(BF16)
    return hi, (x - hi.astype(F32)).astype(BF16)


def _outproj_kernel(*refs, n_in, dual_x):
    y_refs = refs[:2 * n_in]
    n_x = 2 if dual_x else 1
    x_refs = refs[2 * n_in + 1:2 * n_in + 1 + n_x]
    w_ref = refs[2 * n_in]
    (g1_ref, sh_ref, sc_ref, nw_ref, wr_ref, br_ref,
     xo_ref, tr_ref, route_ref, cnt_ref, w_scr, wr_hl, carry, o_scr) = refs[2 * n_in + 1 + n_x:]
    is_ctx = pl.program_id(0) < T_CTX // ROUTE_TM

    @pl.when(pl.program_id(0) == 0)
    def _():
        w_scr[...] = w_ref[...].astype(BF16)
        hi, lo = _split_bf16(jnp.transpose(wr_ref[...]))
        wr_hl[0:LANES, :] = hi
        wr_hl[LANES:2 * LANES, :] = lo
        carry[...] = jnp.zeros(carry.shape, F32)

    o = None
    for i in range(n_in):
        y = jnp.where(is_ctx, y_refs[2 * i][...], y_refs[2 * i + 1][...])
        d = jnp.dot(y, w_scr[i * D:(i + 1) * D, :], preferred_element_type=F32)
        o = d if o is None else o + d
    o_scr[...] = o
    for sub in range(ROUTE_TM // ROUTE_SUB):
        _outproj_subtile(sub, is_ctx, o_scr, x_refs, dual_x, g1_ref, sh_ref, sc_ref, nw_ref, br_ref,
                         xo_ref, tr_ref, route_ref, cnt_ref, wr_hl, carry)


def _outproj_subtile(sub, is_ctx, o_scr, x_refs, dual_x, g1_ref, sh_ref, sc_ref, nw_ref, br_ref,
                     xo_ref, tr_ref, route_ref, cnt_ref, wr_hl, carry):
    tm = ROUTE_SUB
    rows = slice(sub * tm, (sub + 1) * tm)
    x_in = jnp.where(is_ctx, x_refs[0][rows, :], x_refs[1][rows, :]) if dual_x else x_refs[0][rows, :]
    x = x_in + g1_ref[0] * o_scr[rows, :]
    xo_ref[rows, :] = x
    t = _rms(x, nw_ref[...]) * (1.0 + sc_ref[0]) + sh_ref[0]
    for k in range(D // LANES):
        tr_ref[pl.ds(sub * tm * SUBLANES + k, tm, stride=SUBLANES), :] = t[:, k * LANES:(k + 1) * LANES]

    t_hi, t_lo = _split_bf16(t)
    lg = _mm_nt(wr_hl[...], t_hi)
    nr = ROUTE_ROWS
    logit = lg[0:nr, :] + lg[LANES:LANES + nr, :] + _mm_nt(wr_hl[0:LANES, :], t_lo)[0:nr, :] + br_ref[0:nr, :]
    rowf = lax.broadcasted_iota(jnp.int32, (nr, tm), 0).astype(F32)
    neg = -jnp.inf

    def first_argmax(v, vmax):
        return jnp.min(jnp.where(v == vmax, rowf, float(LANES)), axis=0, keepdims=True)

    gl = jnp.where(rowf < N_GROUPS, logit, neg)
    gmax = jnp.max(gl, axis=0, keepdims=True)
    gsel = first_argmax(gl, gmax)
    gprob = 1.0 / jnp.sum(jnp.exp(gl - gmax), axis=0, keepdims=True)
    first = N_GROUPS + EXP_PER_GROUP * gsel
    el = jnp.where((rowf >= first) & (rowf < first + EXP_PER_GROUP), logit, neg)
    m1 = jnp.max(el, axis=0, keepdims=True)
    i1 = first_argmax(el, m1)
    el2 = jnp.where(rowf == i1, neg, el)
    m2 = jnp.max(el2, axis=0, keepdims=True)
    i2 = first_argmax(el2, m2)
    e2 = jnp.exp(m2 - m1)
    c1 = gprob / (1.0 + e2)
    c2 = gprob * e2 / (1.0 + e2)
    x1 = i1 - N_GROUPS
    x2 = i2 - N_GROUPS

    erow = rowf
    hot = ((erow == x1) | (erow == x2)).astype(F32)
    tri = (lax.broadcasted_iota(jnp.int32, (tm, tm), 0) < lax.broadcasted_iota(jnp.int32, (tm, tm), 1))
    before = _mm(hot, tri.astype(F32)) + carry[...]
    r1 = jnp.sum(jnp.where(erow == x1, before, 0.0), axis=0, keepdims=True)
    r2 = jnp.sum(jnp.where(erow == x2, before, 0.0), axis=0, keepdims=True)
    total = carry[...] + _mm(hot, jnp.ones((tm, tm), F32))
    carry[...] = total
    cnt_ref[...] = total[0:N_EXPERTS, 0:LANES]
    row8 = lax.broadcasted_iota(jnp.int32, (SUBLANES, tm), 0)
    out = jnp.zeros((SUBLANES, tm), F32)
    for k, v in enumerate((x1, x2, c1, c2, r1, r2)):
        out = jnp.where(row8 == k, jnp.broadcast_to(v, (SUBLANES, tm)), out)
    route_ref[:, rows] = out


def _outproj_route(ys, w_out, xs, mods, layer, norm_w, w_router, b_router):
    tm = ROUTE_TM
    n_in = len(ys) // 2
    dual_x = len(xs) == 2
    kdim = w_out.shape[0]

    def full(shape):
        return pl.BlockSpec(shape, lambda i: (0,) * len(shape))

    tile = pl.BlockSpec((tm, D), lambda i: (i, 0))
    pair = list(_ctx_lat_specs(tm))
    in_specs = (pair * n_in + [full((kdim, D))] + (pair if dual_x else [tile])
                + [_mod_spec(layer, 2, tm), _mod_spec(layer, 3, tm), _mod_spec(layer, 4, tm),
                   full((1, D)), full((D, LANES)), full((LANES, ROUTE_SUB))])
    cnt = jax.ShapeDtypeStruct((N_EXPERTS, LANES), F32)
    return pl.pallas_call(
        functools.partial(_outproj_kernel, n_in=n_in, dual_x=dual_x),
        grid=(T // tm,), in_specs=in_specs,
        out_specs=[tile, pl.BlockSpec((tm * SUBLANES, LANES), lambda i: (i, 0)),
                   pl.BlockSpec((SUBLANES, tm), lambda i: (0, i)), full(cnt.shape)],
        out_shape=[jax.ShapeDtypeStruct((T, D), F32), jax.ShapeDtypeStruct((T * SUBLANES, LANES), F32),
                   jax.ShapeDtypeStruct((SUBLANES, T), F32), cnt],
        scratch_shapes=[pltpu.VMEM((kdim, D), BF16), pltpu.VMEM((2 * LANES, D), BF16),
                        pltpu.VMEM((ROUTE_ROWS, ROUTE_SUB), F32), pltpu.VMEM((tm, D), F32)],
        compiler_params=_cparams(1),
        name=f"outproj{layer}",
    )(*ys, w_out, *xs, mods, mods, mods, norm_w.reshape(1, D), w_router,
      jnp.broadcast_to(b_router.reshape(LANES, 1), (LANES, ROUTE_SUB)))


def _moe_meta(counts):
    tm = MOE_TM
    experts = jnp.arange(N_EXPERTS, dtype=jnp.int32)
    counts = jnp.max(counts, axis=1).astype(jnp.int32)
    padded = ((counts + tm - 1) // tm) * tm
    ends = jnp.cumsum(padded)
    tile_start = jnp.arange(MOE_TILES, dtype=jnp.int32) * tm
    te = jnp.sum((tile_start[:, None] >= ends[None, :]).astype(jnp.int32), axis=1)
    last = jnp.max(jnp.where(counts > 0, experts, 0))
    meta = jnp.concatenate([jnp.minimum(te, last), ends[-1:] // tm]).astype(jnp.int32)
    starts = ends - padded
    pads = jnp.concatenate([starts + counts, ends[-1:], ends, jnp.full((1,), MOE_ROWS)]).astype(jnp.int32)
    return starts.astype(jnp.int32), pads, meta


def _expert_changed(meta_ref, j):
    return (j == 0) | (meta_ref[j] != meta_ref[jnp.maximum(j - 1, 0)])


def _moe_up_kernel(pos1_ref, pos2_ref, pads_ref, meta_ref, tr_hbm, wg_ref, wu_ref, a_ref, rowmap_ref,
                   tr_scr, g0, g1, wg_bf, wu_bf, sem):
    j = pl.program_id(0)
    tm = MOE_TM
    ntiles = meta_ref[MOE_TILES]

    def gather(tile, dst):
        for mi in range(tm):
            tok = jnp.minimum(rowmap_ref[tile * tm + mi] >> 1, T - 1)
            dst[mi * SUBLANES:(mi + 1) * SUBLANES, :] = tr_scr[pl.ds(pl.multiple_of(tok * SUBLANES, SUBLANES), SUBLANES), :]

    @pl.when(j == 0)
    def _():
        load = pltpu.make_async_copy(tr_hbm, tr_scr, sem)
        load.start()

        def clear(c, carry):
            for i in range(SUBLANES):
                rowmap_ref[c * SUBLANES + i] = 2 * T
            return carry
        for k in range(N_EXPERTS + 1):
            lax.fori_loop(pads_ref[k] // SUBLANES, pads_ref[N_EXPERTS + 1 + k] // SUBLANES, clear, 0)

        def place(t, carry):
            rowmap_ref[pos1_ref[t]] = 2 * t
            rowmap_ref[pos2_ref[t]] = 2 * t + 1
            return carry
        lax.fori_loop(0, T, place, 0, unroll=8)
        load.wait()
        gather(0, g0)

    def compute(cur, nxt):
        gather(jnp.minimum(j + 1, ntiles - 1), nxt)
        x = _tokmajor_to_std(cur, tm).astype(BF16)
        g = jnp.dot(x, wg_bf[...], preferred_element_type=F32)
        u = jnp.dot(x, wu_bf[...], preferred_element_type=F32)
        a_ref[...] = (_silu(g) * u).astype(a_ref.dtype)

    @pl.when(j < ntiles)
    def _():
        @pl.when(_expert_changed(meta_ref, j))
        def _():
            wg_bf[...] = wg_ref[...].astype(BF16)
            wu_bf[...] = wu_ref[...].astype(BF16)

        pl.when(j % 2 == 0)(functools.partial(compute, g0, g1))
        pl.when(j % 2 == 1)(functools.partial(compute, g1, g0))

    @pl.when(j >= ntiles)
    def _():
        a_ref[...] = jnp.zeros(a_ref.shape, a_ref.dtype)


def _moe_down_kernel(rowmap_ref, cpair_ref, meta_ref, a_ref, wd_ref, *rest, layer, final):
    if final:
        x_hbm, mods_ref, fw_ref, out_c, out_l, acc, y0, y1, wd_bf, xin, xout, io_sems = rest
    else:
        out_hbm, acc, y0, y1, wd_bf = rest
    j = pl.program_id(0)
    tm = MOE_TM
    zrows = MERGE_TM
    ntiles = meta_ref[MOE_TILES]

    @pl.when(j == 0)
    def _():
        def zero(i, carry):
            acc[pl.ds(pl.multiple_of(i * zrows, zrows), zrows), :] = jnp.zeros((zrows, LANES), F32)
            return carry
        lax.fori_loop(0, ACC_ROWS // zrows, zero, 0)

    def matmul(dst):
        y = jnp.dot(a_ref[...], wd_bf[...], preferred_element_type=F32)
        for k in range(D // LANES):
            dst[pl.ds(k, tm, stride=SUBLANES), :] = y[:, k * LANES:(k + 1) * LANES]

    def scatter(tile, src):
        for b in range(tm // SUBLANES):
            ents = [rowmap_ref[tile * tm + b * SUBLANES + i] for i in range(SUBLANES)]
            offs = [pl.multiple_of((e >> 1) * SUBLANES, SUBLANES) for e in ents]
            olds = [acc[pl.ds(o, SUBLANES), :] for o in offs]
            for i, o in enumerate(offs):
                r = (b * SUBLANES + i) * SUBLANES
                acc[pl.ds(o, SUBLANES), :] = olds[i] + cpair_ref[ents[i]] * src[r:r + SUBLANES, :]

    has_mm = j < ntiles
    has_sc = (j >= 1) & (j <= ntiles)

    @pl.when(has_mm)
    def _():
        @pl.when(_expert_changed(meta_ref, j))
        def _():
            wd_bf[...] = wd_ref[...].astype(BF16)

    for par, (cur, prev) in enumerate(((y0, y1), (y1, y0))):
        mine = (j % 2) == par

        @pl.when(mine & has_mm & has_sc)
        def _():
            matmul(cur)
            scatter(j - 1, prev)

        @pl.when(mine & has_mm & jnp.logical_not(has_sc))
        def _():
            matmul(cur)

        @pl.when(mine & jnp.logical_not(has_mm) & has_sc)
        def _():
            scatter(j - 1, prev)

    @pl.when(j == pl.num_programs(0) - 1)
    def _():
        if not final:
            pltpu.sync_copy(acc.at[0:T * SUBLANES, :], out_hbm)
            return
        ft = MERGE_TM
        n_t = T // ft
        n_c = T_CTX // ft

        def fetch(i):
            return pltpu.make_async_copy(x_hbm.at[i * ft:(i + 1) * ft, :], xin.at[i % 2], io_sems.at[i % 2])

        def flush(i):
            dst = out_c.at[i * ft:(i + 1) * ft, :] if i < n_c else out_l.at[(i - n_c) * ft:(i - n_c + 1) * ft, :]
            return pltpu.make_async_copy(xout.at[i % 2], dst, io_sems.at[2 + i % 2])

        fetch(0).start()
        for i in range(n_t):
            if i + 1 < n_t:
                fetch(i + 1).start()
            fetch(i).wait()
            moe = jnp.concatenate([acc[pl.ds(i * ft * SUBLANES + k, ft, stride=SUBLANES), :]
                                   for k in range(D // LANES)], axis=1)
            row = 0 if i * ft < T_CTX else 1 + (i * ft - T_CTX) // DEC_SEQ
            x = xin[i % 2] + mods_ref[(layer * SUBLANES + row) * 6 + 5] * moe
            if i >= 2:
                flush(i - 2).wait()
            xout[i % 2] = _rms(x, fw_ref[...])
            flush(i).start()
        for i in range(max(n_t - 2, 0), n_t):
            flush(i).wait()


def _moe(tr, route_t, counts, layer, w_gate, w_up, w_down, xmid=None, mods=None, final_w=None):
    tm = MOE_TM
    starts, pads, meta = _moe_meta(counts)
    experts = jnp.arange(N_EXPERTS, dtype=jnp.int32)

    def position(e, r):
        sel = e.astype(jnp.int32)[:, None] == experts[None, :]
        return jnp.sum(jnp.where(sel, starts[None, :], 0), axis=1) + r.astype(jnp.int32)

    pos1 = position(route_t[0], route_t[4])
    pos2 = position(route_t[1], route_t[5])
    cpair = jnp.concatenate([jnp.stack([route_t[2], route_t[3]], axis=1).reshape(2 * T), jnp.zeros((2,), F32)])

    def wspec(shape, n):
        return pl.BlockSpec((None, None) + shape,
                            lambda j, *pre: (layer, pre[n - 1][jnp.minimum(j, MOE_TILES - 1)], 0, 0))

    gscr = pltpu.VMEM((tm * SUBLANES, LANES), F32)
    act, rowmap = pl.pallas_call(
        _moe_up_kernel,
        grid_spec=pltpu.PrefetchScalarGridSpec(
            num_scalar_prefetch=4, grid=(MOE_TILES,),
            in_specs=[pl.BlockSpec(memory_space=pl.ANY), wspec((D, D_EXPERT), 4), wspec((D, D_EXPERT), 4)],
            out_specs=[pl.BlockSpec((tm, D_EXPERT), lambda j, *pre: (j, 0)), pl.BlockSpec(memory_space=pltpu.SMEM)],
            scratch_shapes=[pltpu.VMEM((T * SUBLANES, LANES), F32), gscr, gscr,
                            pltpu.VMEM((D, D_EXPERT), BF16), pltpu.VMEM((D, D_EXPERT), BF16),
                            pltpu.SemaphoreType.DMA(())]),
        out_shape=[jax.ShapeDtypeStruct((MOE_ROWS, D_EXPERT), BF16), jax.ShapeDtypeStruct((MOE_ROWS,), jnp.int32)],
        compiler_params=_cparams(1),
        name=f"moe_up{layer}",
    )(pos1, pos2, pads, meta, tr, w_gate, w_up)

    final = final_w is not None
    any_spec = pl.BlockSpec(memory_space=pl.ANY)
    in_specs = [pl.BlockSpec((tm, D_EXPERT), lambda j, *pre: (jnp.minimum(j, MOE_TILES - 1), 0)),
                wspec((D_EXPERT, D), 3)]
    args = [rowmap, cpair, meta, act, w_down]
    scratch = [pltpu.VMEM((ACC_ROWS, LANES), F32), gscr, gscr, pltpu.VMEM((D_EXPERT, D), BF16)]
    if final:
        stage = pltpu.VMEM((2, MERGE_TM, D), F32)
        in_specs += [any_spec, pl.BlockSpec(mods.shape, lambda j, *pre: (0, 0, 0)),
                     pl.BlockSpec((1, D), lambda j, *pre: (0, 0))]
        args += [xmid, mods, final_w.reshape(1, D)]
        scratch += [stage, stage, pltpu.SemaphoreType.DMA((4,))]
        out_shape = [jax.ShapeDtypeStruct((T_CTX, D), F32), jax.ShapeDtypeStruct((T_LAT, D), F32)]
    else:
        out_shape = [jax.ShapeDtypeStruct((T * SUBLANES, LANES), F32)]
    return pl.pallas_call(
        functools.partial(_moe_down_kernel, layer=layer, final=final),
        grid_spec=pltpu.PrefetchScalarGridSpec(
            num_scalar_prefetch=3, grid=(MOE_TILES + 1,), in_specs=in_specs,
            out_specs=[any_spec] * len(out_shape), scratch_shapes=scratch),
        out_shape=out_shape,
        compiler_params=_cparams(1),
        name=f"moe_down{layer}",
    )(*args)


def _pad_lanes(v):
    return jnp.pad(v.astype(F32), (0, LANES - v.shape[0])).reshape(1, LANES)


def kernel(x_prompt, x_sample, cache_k_attn, cache_v_attn, state_ssd_fwd, state_ssd_bwd, state_gla_fwd, state_gla_bwd, c, c_ctx, w_ada, b_ada, norm_mix_w, norm_ffn_w, w_in_even, conv_w, conv_b, dt_bias_fwd, dt_bias_bwd, a_log_fwd, a_log_bwd, d_skip, ssd_norm_w, attn_sink, w_out_even, w_in_odd, w_gk2_fwd, b_gk_fwd, w_gk2_bwd, b_gk_bwd, gla_norm_w, w_out_odd, w_router_group, b_router_group, w_router_expert, b_router_expert, w_gate_exp, w_up_exp, w_down_exp, final_norm_w):
    depth = w_ada.shape[0]
    assert depth == 2 and x_prompt.shape == (BATCH, SEQ, D) and x_sample.shape == (DEC_BATCH, DEC_SEQ, D)

    cond8 = jnp.concatenate([c_ctx[None, :], c, jnp.zeros((SUBLANES - 1 - DEC_BATCH, D), F32)], axis=0)
    mods = _adaln(cond8, w_ada, b_ada).reshape(depth * SUBLANES * 6, 1, D)
    xs0 = (x_prompt.reshape(T_CTX, D), x_sample.reshape(T_LAT, D))

    def router_params(i):
        wr = jnp.concatenate([w_router_group[i], w_router_expert[i],
                              jnp.zeros((D, LANES - N_GROUPS - N_EXPERTS), F32)], axis=1)
        return wr, _pad_lanes(jnp.concatenate([b_router_group[i], b_router_expert[i]]))

    n_zxbc = 2 * D_SSD + 2 * SSD_GROUPS * SSD_N
    n_dt = 2 * SSD_HEADS
    half = PROJ_TN // 2
    r_bc, r_q = 2 * D_SSD, n_zxbc + n_dt
    r_kv = r_q + D
    rows0 = [(0, half), (D_SSD, D_SSD + half), (r_q, r_q + half), (r_bc, r_kv)]
    p0, p0dt = _modproj(xs0, mods, 0, norm_mix_w[0], jnp.transpose(w_in_even[0]), rows0, n_zxbc)

    dtb = _pad_lanes(jnp.concatenate([dt_bias_fwd[0], dt_bias_bwd[0]]))
    alog = _pad_lanes(jnp.concatenate([a_log_fwd[0], a_log_bwd[0]]))
    dskip = jnp.repeat(d_skip[0], SSD_P).reshape(1, D_SSD)
    ssd_args = (conv_w[0], conv_b[0], dtb, alog, dskip, ssd_norm_w[0])
    y_ssd_c, ssd_f, ssd_b = _ssd(p0, p0dt, 0, BATCH, SEQ, *ssd_args)
    y_ssd_l, _, _ = _ssd(p0, p0dt, T_CTX, DEC_BATCH, DEC_SEQ, *ssd_args,
                         h0f=state_ssd_fwd[:, 0].reshape(DEC_BATCH, D_SSD, SSD_N),
                         h0b=state_ssd_bwd[:, 0].reshape(DEC_BATCH, D_SSD, SSD_N))
    sink = _pad_lanes(attn_sink[0])
    y_att_c, new_kt, new_vt = _ctx_attn(p0, sink)
    def cache_in(t):
        return jnp.transpose(t[:, 0], (0, 2, 3, 1)).reshape(DEC_BATCH, ATT_KV_DIM, PAST_LEN)

    y_att_l = _lat_attn(p0, cache_in(cache_k_attn), cache_in(cache_v_attn), sink)
    xmid0, tr0, route0, cnt0 = _outproj_route([y_ssd_c, y_ssd_l, y_att_c, y_att_l], w_out_even[0], xs0, mods, 0,
                                              norm_ffn_w[0], *router_params(0))
    (moe0,) = _moe(tr0, route0, cnt0, 0, w_gate_exp, w_up_exp, w_down_exp)

    dk_all = GLA_HEADS * GLA_DK
    n_qkvr = 2 * dk_all + 2 * GLA_HEADS * GLA_DV
    n_odd = w_in_odd.shape[2]
    p1, p1lr, x1 = _modproj(xmid0, mods, 1, norm_mix_w[1], jnp.transpose(w_in_odd[0]),
                            [(k * PROJ_TN, k * PROJ_TN + half) for k in range(n_qkvr // PROJ_TN)], n_odd - LANES,
                            moe=moe0)
    lr0 = LANES - 2 * GLA_LOWRANK
    w2f = jnp.zeros((LANES, dk_all), F32).at[lr0:lr0 + GLA_LOWRANK].set(w_gk2_fwd[0])
    w2b = jnp.zeros((LANES, dk_all), F32).at[lr0 + GLA_LOWRANK:].set(w_gk2_bwd[0])
    gla_args = (w2f, w2b, b_gk_fwd[0], b_gk_bwd[0], gla_norm_w[0])
    y_gla_c, gla_f, gla_b = _gla(p1, p1lr, 0, BATCH, SEQ, *gla_args)
    y_gla_l, _, _ = _gla(p1, p1lr, T_CTX, DEC_BATCH, DEC_SEQ, *gla_args,
                         s0f=state_gla_fwd[:, 0].reshape(DEC_BATCH, dk_all, GLA_DV),
                         s0b=state_gla_bwd[:, 0].reshape(DEC_BATCH, dk_all, GLA_DV))
    xmid1, tr1, route1, cnt1 = _outproj_route([y_gla_c, y_gla_l], w_out_odd[0], (x1,), mods, 1,
                                              norm_ffn_w[1], *router_params(1))
    y_c, y_l = _moe(tr1, route1, cnt1, 1, w_gate_exp, w_up_exp, w_down_exp, xmid1, mods, final_w=final_norm_w)

    y_prompt = y_c.reshape(BATCH, SEQ, D)
    y_sample = y_l.reshape(DEC_BATCH, DEC_SEQ, D)
    def cache_out(t):
        return jnp.transpose(t.reshape(BATCH, 1, ATT_KV, ATT_HD, SEQ), (0, 1, 4, 2, 3))

    new_k, new_v = cache_out(new_kt), cache_out(new_vt)
    return (y_prompt, y_sample, new_k, new_v,
            ssd_f.reshape(BATCH, 1, SSD_HEADS, SSD_P, SSD_N), ssd_b.reshape(BATCH, 1, SSD_HEADS, SSD_P, SSD_N),
            gla_f.reshape(BATCH, 1, GLA_HEADS, GLA_DK, GLA_DV), gla_b.reshape(BATCH, 1, GLA_HEADS, GLA_DK, GLA_DV))
```

```python
import functools

import numpy as np
import jax
import jax.numpy as jnp
from jax import lax
from jax.experimental import pallas as pl
from jax.experimental.pallas import tpu as pltpu

F32 = jnp.float32
BF16 = jnp.bfloat16

D = 1024
BATCH, SEQ = 16, 256
DEC_BATCH, DEC_SEQ = 2, 1024
PAST_LEN = 512
GRID_W = 64
EPS = 1e-6
T_CTX = BATCH * SEQ
T_LAT = DEC_BATCH * DEC_SEQ
T = T_CTX + T_LAT

SSD_HEADS, SSD_P, SSD_N, SSD_GROUPS = 16, 64, 128, 2
SSD_CONV = 5
SSD_L = 128
D_SSD = SSD_HEADS * SSD_P
HEADS_PER_GROUP = SSD_HEADS // SSD_GROUPS
GROUP_W = HEADS_PER_GROUP * SSD_P

ATT_HEADS, ATT_KV, ATT_HD = 16, 4, 64
ATT_KV_DIM = ATT_KV * ATT_HD
WINDOW = 128
ATT_BLOCK = 128
ATT_SCALE = ATT_HD ** -0.5
ROPE_BASE = 10000.0

GLA_HEADS, GLA_DK, GLA_DV = 4, 128, 256
GLA_C = 64
GLA_GATE_NORM = 16.0
GLA_LOWRANK = 16

N_GROUPS, EXP_PER_GROUP = 4, 4
N_EXPERTS = 16
D_EXPERT = 512

LANES = 128
SUBLANES = 8
VMEM_LIMIT = 56 * 1024 * 1024

P0_Z, P0_X, P0_Q, P0_BC, P0_K, P0_V = 0, 1024, 2048, 3072, 3584, 3840
BC_W = 2 * SSD_GROUPS * SSD_N
P1_Q, P1_K, P1_V, P1_R = 0, 512, 1024, 2048

MOE_TM = 512
MOE_TILES = (2 * T) // MOE_TM + N_EXPERTS
MOE_ROWS = MOE_TILES * MOE_TM
ACC_ROWS = (T + SUBLANES) * SUBLANES
MERGE_TM = 512


def _cparams(n_axes, vmem=VMEM_LIMIT):
    return pltpu.CompilerParams(dimension_semantics=("arbitrary",) * n_axes, vmem_limit_bytes=vmem)


def _silu(x):
    return x / (1.0 + jnp.exp(-x))


def _softplus(x):
    return jnp.maximum(x, 0.0) + jnp.log(1.0 + jnp.exp(-jnp.abs(x)))


def _mm(a, b):
    return jnp.dot(a.astype(BF16), b.astype(BF16), preferred_element_type=F32)


def _mm_nt(a, b):
    return lax.dot_general(a.astype(BF16), b.astype(BF16), (((1,), (1,)), ((), ())),
                           preferred_element_type=F32)


def _mm_tn(a, b):
    return lax.dot_general(a.astype(BF16), b.astype(BF16), (((0,), (0,)), ((), ())),
                           preferred_element_type=F32)


def _rms(x, w):
    return x * lax.rsqrt(jnp.mean(x * x, axis=-1, keepdims=True) + EPS) * w


def _cumsum_rows(x, n):
    row = lax.broadcasted_iota(jnp.int32, x.shape, 0)
    s = 1
    while s < n:
        x = x + jnp.where(row >= s, pltpu.roll(x, s, 0), 0.0)
        s *= 2
    return x


def _mod_row(tok0):
    return jnp.where(tok0 < T_CTX, 0, 1 + (tok0 - T_CTX) // DEC_SEQ)


ADA_TN = 1536


def _adaln_kernel(c_ref, w_ref, b_ref, o_ref):
    s = _silu(c_ref[...])
    o_ref[0] = _mm(s, w_ref[0]) + b_ref[0]


def _adaln(cond8, w_ada, b_ada):
    depth = w_ada.shape[0]
    return pl.pallas_call(
        _adaln_kernel,
        grid=(depth, 6 * D // ADA_TN),
        in_specs=[
            pl.BlockSpec((SUBLANES, D), lambda l, j: (0, 0)),
            pl.BlockSpec((1, D, ADA_TN), lambda l, j: (l, 0, j)),
            pl.BlockSpec((1, 1, ADA_TN), lambda l, j: (l, 0, j)),
        ],
        out_specs=pl.BlockSpec((1, SUBLANES, ADA_TN), lambda l, j: (l, 0, j)),
        out_shape=jax.ShapeDtypeStruct((depth, SUBLANES, 6 * D), F32),
        compiler_params=_cparams(2),
        name="adaln",
    )(cond8, w_ada, b_ada.reshape(depth, 1, 6 * D))


def _mod_spec(layer, chunk, tm, tile_of=lambda i, *_: i):
    return pl.BlockSpec((1, 1, D), lambda *g: ((layer * SUBLANES + _mod_row(tile_of(*g) * tm)) * 6 + chunk, 0, 0))


def _tokmajor_to_std(ref, tm):
    return jnp.concatenate([ref[pl.ds(k, tm, stride=SUBLANES), :] for k in range(D // LANES)], axis=1)


PROJ_TM = 1024
PROJ_TN = 1024


def _ctx_lat_specs(tm, width=D):
    n_ctx = T_CTX // tm
    return (pl.BlockSpec((tm, width), lambda i, *_: (jnp.minimum(i, n_ctx - 1), 0)),
            pl.BlockSpec((tm, width), lambda i, *_: (jnp.maximum(i - n_ctx, 0), 0)))


def _modproj_kernel(*refs, dual_x):
    it = iter(refs)
    if dual_x:
        xc_ref, xl_ref = next(it), next(it)
    else:
        x_ref, moe_ref, g2_ref = next(it), next(it), next(it)
    sh_ref, sc_ref, nw_ref, wlo_ref, whi_ref, ws_ref, o_ref, os_ref = (next(it) for _ in range(8))
    xo_ref = None if dual_x else next(it)
    h_all, w_bf = next(it), next(it)
    j, i = pl.program_id(0), pl.program_id(1)
    tm = PROJ_TM
    rows = pl.ds(pl.multiple_of(i * tm, tm), tm)

    @pl.when(j == 0)
    def _():
        if dual_x:
            x = jnp.where(i < T_CTX // tm, xc_ref[...], xl_ref[...])
        else:
            x = x_ref[...] + g2_ref[0] * _tokmajor_to_std(moe_ref, tm)
            xo_ref[...] = x
        h = (_rms(x, nw_ref[...]) * (1.0 + sc_ref[0]) + sh_ref[0]).astype(BF16)
        h_all[rows, :] = h
        os_ref[...] = _mm_nt(h, ws_ref[...])

    @pl.when(i == 0)
    def _():
        half = wlo_ref.shape[0]
        w_bf[0:half, :] = wlo_ref[...].astype(BF16)
        w_bf[half:2 * half, :] = whi_ref[...].astype(BF16)

    o_ref[...] = _mm_nt(h_all[rows, :], w_bf[...]).astype(o_ref.dtype)


def _modproj(xs, mods, layer, norm_w, wt, tile_rows, small_row, moe=None):
    tm, tn = PROJ_TM, PROJ_TN
    dual_x = moe is None
    n_tiles = len(tile_rows)
    n_i, n_ctx = T // tm, T_CTX // tm

    def w_row(side):
        def index(j, i):
            r = jnp.int32(tile_rows[0][side])
            for k in range(1, n_tiles):
                r = jnp.where(j == k, tile_rows[k][side], r)
            return pl.multiple_of(r, SUBLANES), 0
        return index

    def tok(j, i):
        return jnp.where(j == 0, i, n_i - 1)

    tile = pl.BlockSpec((tm, D), lambda j, i: (tok(j, i), 0))
    if dual_x:
        in_specs = [pl.BlockSpec((tm, D), lambda j, i: (jnp.minimum(tok(j, i), n_ctx - 1), 0)),
                    pl.BlockSpec((tm, D), lambda j, i: (jnp.maximum(tok(j, i) - n_ctx, 0), 0))]
        args = list(xs)
    else:
        in_specs = [tile, pl.BlockSpec((tm * SUBLANES, LANES), lambda j, i: (tok(j, i), 0)),
                    _mod_spec(layer - 1, 5, tm, tok)]
        args = [xs, moe, mods]
    in_specs += [_mod_spec(layer, 0, tm, tok), _mod_spec(layer, 1, tm, tok), pl.BlockSpec((1, D), lambda j, i: (0, 0)),
                 pl.BlockSpec((pl.Element(tn // 2), pl.Element(D)), w_row(0)),
                 pl.BlockSpec((pl.Element(tn // 2), pl.Element(D)), w_row(1)),
                 pl.BlockSpec((pl.Element(LANES), pl.Element(D)), lambda j, i: (small_row, 0))]
    args += [mods, mods, norm_w.reshape(1, D), wt, wt, wt]
    out_specs = [pl.BlockSpec((tm, tn), lambda j, i: (i, j)),
                 pl.BlockSpec((tm, LANES), lambda j, i: (tok(j, i), 0))]
    out_shape = [jax.ShapeDtypeStruct((T, n_tiles * tn), BF16), jax.ShapeDtypeStruct((T, LANES), F32)]
    if not dual_x:
        out_specs.append(tile)
        out_shape.append(jax.ShapeDtypeStruct((T, D), F32))
    return pl.pallas_call(
        functools.partial(_modproj_kernel, dual_x=dual_x),
        grid=(n_tiles, n_i), in_specs=in_specs, out_specs=out_specs, out_shape=out_shape,
        scratch_shapes=[pltpu.VMEM((T, D), BF16), pltpu.VMEM((tn, D), BF16)],
        compiler_params=_cparams(2),
        name=f"modproj{layer}",
    )(*args)


def _expand_heads(v, off):
    hi = (lax.broadcasted_iota(jnp.int32, (v.shape[0], LANES), 1) >= SSD_P).astype(jnp.int32)
    tiles = [jnp.take_along_axis(v, hi + (off + 2 * q), axis=1) for q in range(SSD_HEADS // 2)]
    return jnp.concatenate(tiles, axis=1)


def _ssd_kernel(*refs, seq, has_h0):
    if has_h0:
        (z_ref, x_ref, bc_ref, dt_ref, cwx_ref, cwbc_ref, cbx_ref, cbbc_ref, dtb_ref, alog_ref, dsk_ref,
         nw_ref, h0f_ref, h0b_ref, y_ref, sf_ref, sb_ref,
         xpad, bcpad, xc, bcc, a_scr, dt_scr, yf, yb, hf, hb) = refs
    else:
        (z_ref, x_ref, bc_ref, dt_ref, cwx_ref, cwbc_ref, cbx_ref, cbbc_ref, dtb_ref, alog_ref, dsk_ref,
         nw_ref, y_ref, sf_ref, sb_ref,
         xpad, bcpad, xc, bcc, a_scr, dt_scr, yf, yb, hf, hb) = refs
    L = SSD_L
    nc = seq // L
    pad = SUBLANES
    half = SSD_CONV // 2

    for buf, src, cw, cb, dst in ((xpad, x_ref, cwx_ref, cbx_ref, xc), (bcpad, bc_ref, cwbc_ref, cbbc_ref, bcc)):
        width = buf.shape[1]
        buf[0:pad, :] = jnp.zeros((pad, width), F32)
        buf[pad + seq:2 * pad + seq, :] = jnp.zeros((pad, width), F32)
        buf[pad:pad + seq, :] = src[...].astype(F32)
        for blk in range(nc):
            acc = jnp.broadcast_to(cb[...], (L, width))
            for j in range(SSD_CONV):
                r0 = pad - half + j + blk * L
                acc = acc + cw[j:j + 1, :] * buf[r0:r0 + L, :]
            dst[blk * L:(blk + 1) * L, :] = _silu(acc)

    lane = lax.broadcasted_iota(jnp.int32, (seq, LANES), 1)
    dts = jnp.where(lane < 2 * SSD_HEADS, _softplus(dt_ref[...] + dtb_ref[...]), 0.0)
    dt_scr[...] = dts
    a_scr[...] = dts * (-jnp.exp(alog_ref[...]))

    if has_h0:
        hf[...] = h0f_ref[...]
        hb[...] = h0b_ref[...]
    else:
        hf[...] = jnp.zeros(hf.shape, F32)
        hb[...] = jnp.zeros(hb.shape, F32)

    row = lax.broadcasted_iota(jnp.int32, (L, L), 0)
    col = lax.broadcasted_iota(jnp.int32, (L, L), 1)
    lane_l = lax.broadcasted_iota(jnp.int32, (L, LANES), 1)
    lo_half = lane_l < SSD_P

    def chunk(c, fwd, h_scr):
        off = 0 if fwd else SSD_HEADS
        r0 = pl.multiple_of(c * L, L)
        a = a_scr[pl.ds(r0, L), :]
        dt = dt_scr[pl.ds(r0, L), :]
        cs = _cumsum_rows(a, L)
        total = cs[L - 1:L, :]
        if fwd:
            u = cs
            rvec = jnp.exp(cs)
            ed = jnp.exp(total - cs) * dt
            keep = col <= row
        else:
            ex = cs - a
            u = -ex
            rvec = jnp.exp(total - ex)
            ed = jnp.exp(ex) * dt
            keep = col >= row
        ut = jnp.transpose(u)
        dtt = jnp.transpose(dt)
        tcol = jnp.transpose(jnp.broadcast_to(total, (L, LANES)))[:, 0:1]
        rexp = _expand_heads(rvec, off)
        edexp = _expand_heads(ed, off)
        x = xc[pl.ds(r0, L), :]
        bc = bcc[pl.ds(r0, L), :]
        outs = []
        for g in range(SSD_GROUPS):
            bg = bc[:, g * SSD_N:(g + 1) * SSD_N]
            cg = bc[:, SSD_GROUPS * SSD_N + g * SSD_N:SSD_GROUPS * SSD_N + (g + 1) * SSD_N]
            cbm = _mm_nt(cg, bg)
            hg = h_scr[g * GROUP_W:(g + 1) * GROUP_W, :]
            xg = x[:, g * GROUP_W:(g + 1) * GROUP_W]
            y_off = _mm_nt(cg, hg) * rexp[:, g * GROUP_W:(g + 1) * GROUP_W]
            tiles = []
            for p in range(HEADS_PER_GROUP // 2):
                xt = xg[:, p * LANES:(p + 1) * LANES]
                acc = None
                for s in range(2):
                    h = off + g * HEADS_PER_GROUP + 2 * p + s
                    seg = u[:, h:h + 1] - ut[h:h + 1, :]
                    m = cbm * jnp.exp(jnp.where(keep, seg, -jnp.inf)) * dtt[h:h + 1, :]
                    xm = jnp.where(lo_half if s == 0 else jnp.logical_not(lo_half), xt, 0.0)
                    d = _mm(m, xm)
                    acc = d if acc is None else acc + d
                tiles.append(acc)
            outs.append(y_off + jnp.concatenate(tiles, axis=1))
            decs = []
            for hh in range(HEADS_PER_GROUP):
                h = off + g * HEADS_PER_GROUP + hh
                decs.append(jnp.broadcast_to(jnp.exp(tcol[h:h + 1, :]), (SSD_P, SSD_N)))
            dec = jnp.concatenate(decs, axis=0)
            h_scr[g * GROUP_W:(g + 1) * GROUP_W, :] = dec * hg + _mm_tn(xg * edexp[:, g * GROUP_W:(g + 1) * GROUP_W], bg)
        return r0, jnp.concatenate(outs, axis=1)

    def body(i, carry):
        r0, y = chunk(i, True, hf)
        yf[pl.ds(r0, L), :] = y
        r0, y = chunk(nc - 1 - i, False, hb)
        yb[pl.ds(r0, L), :] = y
        return carry

    lax.fori_loop(0, nc, body, 0, unroll=True)
    sf_ref[...] = hf[...]
    sb_ref[...] = hb[...]

    for blk in range(nc):
        rs = slice(blk * L, (blk + 1) * L)
        y = yf[rs, :] + yb[rs, :] + dsk_ref[...] * xc[rs, :]
        y = y * _silu(z_ref[rs, :].astype(F32))
        y_ref[rs, :] = _rms(y, nw_ref[...]).astype(y_ref.dtype)


def _ssd(p0, p0dt, tok0, nseq, seq, cw, cb, dtb, alog, dskip, nw, h0f=None, h0b=None):
    has_h0 = h0f is not None
    b0 = tok0 // seq

    def cols(width, start):
        return pl.BlockSpec((seq, width), lambda s: (b0 + s, start // width))

    def full(shape):
        return pl.BlockSpec(shape, lambda s: (0,) * len(shape))

    in_specs = [cols(D_SSD, P0_Z), cols(D_SSD, P0_X), cols(BC_W, P0_BC), cols(LANES, 0),
                full((SSD_CONV, D_SSD)), full((SSD_CONV, BC_W)), full((1, D_SSD)), full((1, BC_W)),
                full((1, LANES)), full((1, LANES)), full((1, D_SSD)), full((1, D_SSD))]
    args = [p0, p0, p0, p0dt, cw[:, :D_SSD], cw[:, D_SSD:], cb[:D_SSD].reshape(1, -1), cb[D_SSD:].reshape(1, -1),
            dtb, alog, dskip, nw.reshape(1, -1)]
    st_spec = pl.BlockSpec((None, D_SSD, SSD_N), lambda s: (s, 0, 0))
    if has_h0:
        in_specs += [st_spec, st_spec]
        args += [h0f, h0b]
    st_shape = jax.ShapeDtypeStruct((nseq, D_SSD, SSD_N), F32)
    return pl.pallas_call(
        functools.partial(_ssd_kernel, seq=seq, has_h0=has_h0),
        grid=(nseq,), in_specs=in_specs,
        out_specs=[pl.BlockSpec((seq, D_SSD), lambda s: (s, 0)), st_spec, st_spec],
        out_shape=[jax.ShapeDtypeStruct((nseq * seq, D_SSD), BF16), st_shape, st_shape],
        scratch_shapes=[pltpu.VMEM((seq + 2 * SUBLANES, D_SSD), F32), pltpu.VMEM((seq + 2 * SUBLANES, BC_W), F32),
                        pltpu.VMEM((seq, D_SSD), F32), pltpu.VMEM((seq, BC_W), F32),
                        pltpu.VMEM((seq, LANES), F32), pltpu.VMEM((seq, LANES), F32),
                        pltpu.VMEM((seq, D_SSD), F32), pltpu.VMEM((seq, D_SSD), F32),
                        pltpu.VMEM((D_SSD, SSD_N), F32), pltpu.VMEM((D_SSD, SSD_N), F32)],
        compiler_params=_cparams(1),
        name=f"ssd{seq}",
    )(*args)


def _place_halves(tile, kv_in_high):
    lo = lax.broadcasted_iota(jnp.int32, tile.shape, 1) < ATT_HD
    swapped = pltpu.roll(tile, ATT_HD, 1)
    if kv_in_high:
        return jnp.where(lo, swapped, 0.0), jnp.where(lo, 0.0, tile)
    return jnp.where(lo, tile, 0.0), jnp.where(lo, 0.0, swapped)


def _place_rows(vt, kv_in_high):
    head = vt[ATT_HD:, :] if kv_in_high else vt[:ATT_HD, :]
    z = jnp.zeros_like(head)
    return jnp.concatenate([head, z], axis=0), jnp.concatenate([z, head], axis=0)


LOG2E = 1.4426950408889634
SCORE_SCALE = ATT_SCALE * LOG2E


def _sink_attend_t(score_parts, value_parts, sink2):
    m = sink2
    for s in score_parts:
        m = jnp.maximum(m, jnp.max(s, axis=0, keepdims=True))
    denom = jnp.exp2(sink2 - m)
    out = None
    for s, v in zip(score_parts, value_parts):
        p = jnp.exp2(s - m)
        denom = denom + jnp.sum(p, axis=0, keepdims=True)
        o = _mm(v, p)
        out = o if out is None else out + o
    return out * (1.0 / denom)


def _attn_schedule(n, scores, attend):
    scores(0)
    for j in range(n):
        if j + 1 < n:
            scores(j + 1)
        attend(j)


def _ctx_attn_kernel(q_ref, k_ref, v_ref, sink_ref, o_ref, kt_ref, vt_ref, s_a, s_b):
    sink2 = sink_ref[...] * LOG2E
    bufs = (s_a, s_b)
    half = SEQ // 2
    for t in range(ATT_KV_DIM // LANES):
        cols = slice(t * LANES, (t + 1) * LANES)
        kt_ref[cols, :] = jnp.transpose(k_ref[:, cols].astype(F32))
        vt_ref[cols, :] = jnp.transpose(v_ref[:, cols].astype(F32))

    def kv_tile(ref, j):
        return ref[:, (j // 2) * LANES:(j // 2 + 1) * LANES].astype(F32), (j % 2 == 1)

    def scores(j):
        k_lo, k_hi = _place_halves(*kv_tile(k_ref, j))
        qst = jnp.concatenate([q_ref[:, qt * LANES:(qt + 1) * LANES] for qt in (2 * j, 2 * j + 1)], axis=0)
        bufs[j % 2][...] = _mm_nt(jnp.concatenate([k_lo, k_hi], axis=0), qst) * SCORE_SCALE

    def attend(j):
        src = bufs[j % 2]
        v, high = kv_tile(v_ref, j)
        vts = _place_rows(jnp.transpose(v), high)
        for ql in range(2):
            qt = 2 * j + ql
            for qh in range(2):
                cols = slice(ql * SEQ + qh * half, ql * SEQ + (qh + 1) * half)
                acc = None
                for s, vv in enumerate(vts):
                    o = _sink_attend_t([src[s * SEQ:(s + 1) * SEQ, cols]], [vv], sink2[:, 2 * qt + s:2 * qt + s + 1])
                    acc = o if acc is None else acc + o
                o_ref[qh * half:(qh + 1) * half, qt * LANES:(qt + 1) * LANES] = jnp.transpose(acc).astype(o_ref.dtype)

    _attn_schedule(ATT_KV, scores, attend)


def _ctx_attn(p0, sink):
    def cols(width, start):
        return pl.BlockSpec((SEQ, width), lambda b: (b, start // width))

    sbuf = pltpu.VMEM((2 * SEQ, 2 * SEQ), F32)
    return pl.pallas_call(
        _ctx_attn_kernel,
        grid=(BATCH,),
        in_specs=[cols(D, P0_Q), cols(ATT_KV_DIM, P0_K), cols(ATT_KV_DIM, P0_V),
                  pl.BlockSpec((1, LANES), lambda b: (0, 0))],
        out_specs=[pl.BlockSpec((SEQ, D), lambda b: (b, 0))]
        + [pl.BlockSpec((None, ATT_KV_DIM, SEQ), lambda b: (b, 0, 0))] * 2,
        out_shape=[jax.ShapeDtypeStruct((T_CTX, D), BF16)] + [jax.ShapeDtypeStruct((BATCH, ATT_KV_DIM, SEQ), F32)] * 2,
        scratch_shapes=[sbuf, sbuf],
        compiler_params=_cparams(1),
        name="ctx_attn",
    )(p0, p0, p0, sink)


def _rope_tables():
    quarter = ATT_HD // 4
    t = np.arange(DEC_SEQ)
    lane = np.arange(LANES)
    inv = ROPE_BASE ** (-(lane % quarter).astype(np.float64) / quarter)
    pos = np.where(((lane % ATT_HD) < ATT_HD // 2)[None, :], (t // GRID_W)[:, None], (t % GRID_W)[:, None])
    ang = pos * inv[None, :]
    first = ((lane % (2 * quarter)) < quarter)[None, :]
    cos, sin = np.cos(ang), np.sin(ang)
    return (jnp.asarray(cos, F32), jnp.asarray(np.where(first, -sin, 0.0), F32),
            jnp.asarray(np.where(first, 0.0, sin), F32))


def _rope(x, cos, sa, sb):
    quarter = ATT_HD // 4
    return x * cos + pltpu.roll(x, LANES - quarter, 1) * sa + pltpu.roll(x, quarter, 1) * sb


def _lat_attn_kernel(q_ref, kp_ref, kc_ref, kn_ref, vp_ref, vc_ref, vn_ref, ck_ref, cv_ref,
                     cos_ref, sa_ref, sb_ref, sink_ref, o_ref, c_a, c_b, w_a, w_b):
    blk = pl.program_id(1)
    nb = pl.num_programs(1)
    B = ATT_BLOCK
    sink2 = sink_ref[...] * LOG2E
    cbufs, wbufs = (c_a, c_b), (w_a, w_b)

    def tables(b):
        r0 = pl.multiple_of(b * B, B)
        return cos_ref[pl.ds(r0, B), :], sa_ref[pl.ds(r0, B), :], sb_ref[pl.ds(r0, B), :]

    tq = tables(blk)
    tk = [tables(jnp.maximum(blk - 1, 0)), tq, tables(jnp.minimum(blk + 1, nb - 1))]
    kabs = (blk - 1) * B + lax.broadcasted_iota(jnp.int32, (3 * B, B), 0)
    qpos = blk * B + lax.broadcasted_iota(jnp.int32, (3 * B, B), 1)
    ok = (jnp.abs(qpos - kabs) <= WINDOW) & (kabs >= 0) & (kabs < nb * B)
    ok = jnp.concatenate([ok, ok], axis=1)
    ok = jnp.concatenate([ok, ok], axis=0)

    def scores(j):
        high = (j % 2 == 1)
        sl = slice((j // 2) * LANES, (j // 2 + 1) * LANES)
        kw = jnp.concatenate([_rope(r[:, sl].astype(F32), *tb) for r, tb in zip((kp_ref, kc_ref, kn_ref), tk)], axis=0)
        qs = [q_ref[:, qt * LANES:(qt + 1) * LANES].astype(F32) for qt in (2 * j, 2 * j + 1)]
        q_plain = jnp.concatenate(qs, axis=0)
        q_rope = jnp.concatenate([_rope(q, *tq) for q in qs], axis=0)
        ck = jnp.transpose(ck_ref[sl, :])
        cbufs[j % 2][...] = _mm_nt(jnp.concatenate(_place_halves(ck, high), axis=0), q_plain) * SCORE_SCALE
        win = _mm_nt(jnp.concatenate(_place_halves(kw, high), axis=0), q_rope) * SCORE_SCALE
        wbufs[j % 2][...] = jnp.where(ok, win, -jnp.inf)

    def attend(j):
        high = (j % 2 == 1)
        sl = slice((j // 2) * LANES, (j // 2 + 1) * LANES)
        vw = jnp.concatenate([jnp.transpose(r[:, sl].astype(F32)) for r in (vp_ref, vc_ref, vn_ref)], axis=1)
        vts = _place_rows(vw, high)
        cvts = _place_rows(cv_ref[sl, :], high)
        csrc, wsrc = cbufs[j % 2], wbufs[j % 2]
        for ql in range(2):
            qt = 2 * j + ql
            cols = slice(ql * B, (ql + 1) * B)
            acc = None
            for s in range(2):
                parts = [csrc[s * PAST_LEN:(s + 1) * PAST_LEN, cols], wsrc[s * 3 * B:(s + 1) * 3 * B, cols]]
                o = _sink_attend_t(parts, [cvts[s], vts[s]], sink2[:, 2 * qt + s:2 * qt + s + 1])
                acc = o if acc is None else acc + o
            o_ref[:, qt * LANES:(qt + 1) * LANES] = jnp.transpose(acc).astype(o_ref.dtype)

    _attn_schedule(ATT_KV, scores, attend)


def _lat_attn(p0, ck, cv, sink):
    nb = DEC_SEQ // ATT_BLOCK
    base = T_CTX // ATT_BLOCK

    def kv(start, shift):
        return pl.BlockSpec((ATT_BLOCK, ATT_KV_DIM),
                            lambda b, i: (base + b * nb + jnp.clip(i + shift, 0, nb - 1), start // ATT_KV_DIM))

    def full(shape):
        return pl.BlockSpec(shape, lambda b, i: (0,) * len(shape))

    cache = pl.BlockSpec((None, ATT_KV_DIM, PAST_LEN), lambda b, i: (b, 0, 0))
    cos, sa, sb = _rope_tables()
    return pl.pallas_call(
        _lat_attn_kernel,
        grid=(DEC_BATCH, nb),
        in_specs=[pl.BlockSpec((ATT_BLOCK, D), lambda b, i: (base + b * nb + i, P0_Q // D)),
                  kv(P0_K, -1), kv(P0_K, 0), kv(P0_K, 1), kv(P0_V, -1), kv(P0_V, 0), kv(P0_V, 1),
                  cache, cache, full((DEC_SEQ, LANES)), full((DEC_SEQ, LANES)), full((DEC_SEQ, LANES)),
                  full((1, LANES))],
        out_specs=pl.BlockSpec((ATT_BLOCK, D), lambda b, i: (b * nb + i, 0)),
        out_shape=jax.ShapeDtypeStruct((T_LAT, D), BF16),
        scratch_shapes=[pltpu.VMEM((2 * PAST_LEN, 2 * ATT_BLOCK), F32)] * 2
        + [pltpu.VMEM((2 * 3 * ATT_BLOCK, 2 * ATT_BLOCK), F32)] * 2,
        compiler_params=_cparams(2),
        name="lat_attn",
    )(p0, p0, p0, p0, p0, p0, p0, ck, cv, cos, sa, sb, sink)


def _log_sigmoid(x):
    return jnp.minimum(x, 0.0) - jnp.log(1.0 + jnp.exp(-jnp.abs(x)))


def _gla_kernel(*refs, seq, has_s0):
    if has_s0:
        (q_ref, k_ref, v_ref, r_ref, lr_ref, w2f_ref, w2b_ref, bf_ref, bb_ref, nw_ref, s0f_ref, s0b_ref,
         y_ref, sf_ref, sb_ref, gf, gb, yf, yb, stf, stb) = refs
    else:
        (q_ref, k_ref, v_ref, r_ref, lr_ref, w2f_ref, w2b_ref, bf_ref, bb_ref, nw_ref,
         y_ref, sf_ref, sb_ref, gf, gb, yf, yb, stf, stb) = refs
    C = GLA_C
    nc = seq // C
    lr = lr_ref[...]
    gf[...] = _log_sigmoid(_mm(lr, w2f_ref[...]) + bf_ref[...]) / GLA_GATE_NORM
    gb[...] = _log_sigmoid(_mm(lr, w2b_ref[...]) + bb_ref[...]) / GLA_GATE_NORM
    for h in range(GLA_HEADS):
        rows = slice(h * GLA_DV, (h + 1) * GLA_DV)
        if has_s0:
            stf[rows, :] = jnp.transpose(s0f_ref[h * GLA_DK:(h + 1) * GLA_DK, :])
            stb[rows, :] = jnp.transpose(s0b_ref[h * GLA_DK:(h + 1) * GLA_DK, :])
        else:
            stf[rows, :] = jnp.zeros((GLA_DV, GLA_DK), F32)
            stb[rows, :] = jnp.zeros((GLA_DV, GLA_DK), F32)

    row = lax.broadcasted_iota(jnp.int32, (C, C), 0)
    col = lax.broadcasted_iota(jnp.int32, (C, C), 1)
    qscale = GLA_DK ** -0.5

    def chunk(c, fwd):
        g_scr, y_scr, st = (gf, yf, stf) if fwd else (gb, yb, stb)
        r0 = pl.multiple_of(c * C, C)
        g = g_scr[pl.ds(r0, C), :]
        cs = _cumsum_rows(g, C)
        total = cs[C - 1:C, :]
        q = q_ref[pl.ds(r0, C), :].astype(F32) * qscale
        k = k_ref[pl.ds(r0, C), :].astype(F32)
        v = v_ref[pl.ds(r0, C), :]
        if fwd:
            qs, ks, ke = q * jnp.exp(cs), k * jnp.exp(-cs), k * jnp.exp(total - cs)
            keep = col <= row
        else:
            ex = cs - g
            qs, ks, ke = q * jnp.exp(total - ex), k * jnp.exp(ex - total), k * jnp.exp(ex)
            keep = col >= row
        dec = jnp.exp(total)
        for h in range(GLA_HEADS):
            kc = slice(h * GLA_DK, (h + 1) * GLA_DK)
            vc = slice(h * GLA_DV, (h + 1) * GLA_DV)
            s_t = st[vc, :]
            att = jnp.where(keep, _mm_nt(qs[:, kc], ks[:, kc]), 0.0)
            y_scr[pl.ds(r0, C), vc] = _mm(att, v[:, vc]) + _mm_nt(qs[:, kc], s_t)
            st[vc, :] = dec[:, kc] * s_t + _mm_tn(v[:, vc], ke[:, kc])

    def body(i, carry):
        chunk(i, True)
        chunk(nc - 1 - i, False)
        return carry

    lax.fori_loop(0, nc, body, 0, unroll=4)

    nw = nw_ref[...]
    for blk in range(seq // LANES):
        rs = slice(blk * LANES, (blk + 1) * LANES)
        y = yf[rs, :] + yb[rs, :]
        gate = _silu(r_ref[rs, :].astype(F32))
        for h in range(GLA_HEADS):
            vc = slice(h * GLA_DV, (h + 1) * GLA_DV)
            y_ref[rs, vc] = (_rms(y[:, vc], nw) * gate[:, vc]).astype(y_ref.dtype)
    for h in range(GLA_HEADS):
        rows = slice(h * GLA_DV, (h + 1) * GLA_DV)
        sf_ref[h * GLA_DK:(h + 1) * GLA_DK, :] = jnp.transpose(stf[rows, :])
        sb_ref[h * GLA_DK:(h + 1) * GLA_DK, :] = jnp.transpose(stb[rows, :])


def _gla(p1, p1lr, tok0, nseq, seq, w2f, w2b, bgf, bgb, nw, s0f=None, s0b=None):
    has_s0 = s0f is not None
    b0 = tok0 // seq
    dk_all, dv_all = GLA_HEADS * GLA_DK, GLA_HEADS * GLA_DV

    def cols(width, start):
        return pl.BlockSpec((seq, width), lambda s: (b0 + s, start // width))

    def full(shape):
        return pl.BlockSpec(shape, lambda s: (0,) * len(shape))

    in_specs = [cols(dk_all, P1_Q), cols(dk_all, P1_K), cols(dv_all, P1_V), cols(dv_all, P1_R), cols(LANES, 0),
                full((LANES, dk_all)), full((LANES, dk_all)), full((1, dk_all)), full((1, dk_all)), full((1, GLA_DV))]
    args = [p1, p1, p1, p1, p1lr, w2f, w2b, bgf.reshape(1, -1), bgb.reshape(1, -1), nw.reshape(1, -1)]
    st_spec = pl.BlockSpec((None, dk_all, GLA_DV), lambda s: (s, 0, 0))
    if has_s0:
        in_specs += [st_spec, st_spec]
        args += [s0f, s0b]
    st_shape = jax.ShapeDtypeStruct((nseq, dk_all, GLA_DV), F32)
    return pl.pallas_call(
        functools.partial(_gla_kernel, seq=seq, has_s0=has_s0),
        grid=(nseq,), in_specs=in_specs,
        out_specs=[pl.BlockSpec((seq, dv_all), lambda s: (s, 0)), st_spec, st_spec],
        out_shape=[jax.ShapeDtypeStruct((nseq * seq, dv_all), BF16), st_shape, st_shape],
        scratch_shapes=[pltpu.VMEM((seq, dk_all), F32), pltpu.VMEM((seq, dk_all), F32),
                        pltpu.VMEM((seq, dv_all), F32), pltpu.VMEM((seq, dv_all), F32),
                        pltpu.VMEM((dv_all, GLA_DK), F32), pltpu.VMEM((dv_all, GLA_DK), F32)],
        compiler_params=_cparams(1),
        name=f"gla{seq}",
    )(*args)


ROUTE_TM = 512
ROUTE_SUB = 256
ROUTE_ROWS = 32


def _split_bf16(x):
    hi = x.astype(BF16)
    return hi, (x - hi.astype(F32)).astype(BF16)


def _outproj_kernel(*refs, n_in, dual_x):
    y_refs = refs[:2 * n_in]
    n_x = 2 if dual_x else 1
    x_refs = refs[2 * n_in + 1:2 * n_in + 1 + n_x]
    w_ref = refs[2 * n_in]
    (g1_ref, sh_ref, sc_ref, nw_ref, wr_ref, br_ref,
     xo_ref, tr_ref, route_ref, cnt_ref, w_scr, wr_hl, carry, o_scr) = refs[2 * n_in + 1 + n_x:]
    is_ctx = pl.program_id(0) < T_CTX // ROUTE_TM

    @pl.when(pl.program_id(0) == 0)
    def _():
        w_scr[...] = w_ref[...].astype(BF16)
        hi, lo = _split_bf16(jnp.transpose(wr_ref[...]))
        wr_hl[0:LANES, :] = hi
        wr_hl[LANES:2 * LANES, :] = lo
        carry[...] = jnp.zeros(carry.shape, F32)

    o = None
    for i in range(n_in):
        y = jnp.where(is_ctx, y_refs[2 * i][...], y_refs[2 * i + 1][...])
        d = jnp.dot(y, w_scr[i * D:(i + 1) * D, :], preferred_element_type=F32)
        o = d if o is None else o + d
    o_scr[...] = o
    for sub in range(ROUTE_TM // ROUTE_SUB):
        _outproj_subtile(sub, is_ctx, o_scr, x_refs, dual_x, g1_ref, sh_ref, sc_ref, nw_ref, br_ref,
                         xo_ref, tr_ref, route_ref, cnt_ref, wr_hl, carry)


def _outproj_subtile(sub, is_ctx, o_scr, x_refs, dual_x, g1_ref, sh_ref, sc_ref, nw_ref, br_ref,
                     xo_ref, tr_ref, route_ref, cnt_ref, wr_hl, carry):
    tm = ROUTE_SUB
    rows = slice(sub * tm, (sub + 1) * tm)
    x_in = jnp.where(is_ctx, x_refs[0][rows, :], x_refs[1][rows, :]) if dual_x else x_refs[0][rows, :]
    x = x_in + g1_ref[0] * o_scr[rows, :]
    xo_ref[rows, :] = x
    t = _rms(x, nw_ref[...]) * (1.0 + sc_ref[0]) + sh_ref[0]
    for k in range(D // LANES):
        tr_ref[pl.ds(sub * tm * SUBLANES + k, tm, stride=SUBLANES), :] = t[:, k * LANES:(k + 1) * LANES]

    t_hi, t_lo = _split_bf16(t)
    lg = _mm_nt(wr_hl[...], t_hi)
    nr = ROUTE_ROWS
    logit = lg[0:nr, :] + lg[LANES:LANES + nr, :] + _mm_nt(wr_hl[0:LANES, :], t_lo)[0:nr, :] + br_ref[0:nr, :]
    rowf = lax.broadcasted_iota(jnp.int32, (nr, tm), 0).astype(F32)
    neg = -jnp.inf

    def first_argmax(v, vmax):
        return jnp.min(jnp.where(v == vmax, rowf, float(LANES)), axis=0, keepdims=True)

    gl = jnp.where(rowf < N_GROUPS, logit, neg)
    gmax = jnp.max(gl, axis=0, keepdims=True)
    gsel = first_argmax(gl, gmax)
    gprob = 1.0 / jnp.sum(jnp.exp(gl - gmax), axis=0, keepdims=True)
    first = N_GROUPS + EXP_PER_GROUP * gsel
    el = jnp.where((rowf >= first) & (rowf < first + EXP_PER_GROUP), logit, neg)
    m1 = jnp.max(el, axis=0, keepdims=True)
    i1 = first_argmax(el, m1)
    el2 = jnp.where(rowf == i1, neg, el)
    m2 = jnp.max(el2, axis=0, keepdims=True)
    i2 = first_argmax(el2, m2)
    e2 = jnp.exp(m2 - m1)
    c1 = gprob / (1.0 + e2)
    c2 = gprob * e2 / (1.0 + e2)
    x1 = i1 - N_GROUPS
    x2 = i2 - N_GROUPS

    erow = rowf
    hot = ((erow == x1) | (erow == x2)).astype(F32)
    tri = (lax.broadcasted_iota(jnp.int32, (tm, tm), 0) < lax.broadcasted_iota(jnp.int32, (tm, tm), 1))
    before = _mm(hot, tri.astype(F32)) + carry[...]
    r1 = jnp.sum(jnp.where(erow == x1, before, 0.0), axis=0, keepdims=True)
    r2 = jnp.sum(jnp.where(erow == x2, before, 0.0), axis=0, keepdims=True)
    total = carry[...] + _mm(hot, jnp.ones((tm, tm), F32))
    carry[...] = total
    cnt_ref[...] = total[0:N_EXPERTS, 0:LANES]
    row8 = lax.broadcasted_iota(jnp.int32, (SUBLANES, tm), 0)
    out = jnp.zeros((SUBLANES, tm), F32)
    for k, v in enumerate((x1, x2, c1, c2, r1, r2)):
        out = jnp.where(row8 == k, jnp.broadcast_to(v, (SUBLANES, tm)), out)
    route_ref[:, rows] = out


def _outproj_route(ys, w_out, xs, mods, layer, norm_w, w_router, b_router):
    tm = ROUTE_TM
    n_in = len(ys) // 2
    dual_x = len(xs) == 2
    kdim = w_out.shape[0]

    def full(shape):
        return pl.BlockSpec(shape, lambda i: (0,) * len(shape))

    tile = pl.BlockSpec((tm, D), lambda i: (i, 0))
    pair = list(_ctx_lat_specs(tm))
    in_specs = (pair * n_in + [full((kdim, D))] + (pair if dual_x else [tile])
                + [_mod_spec(layer, 2, tm), _mod_spec(layer, 3, tm), _mod_spec(layer, 4, tm),
                   full((1, D)), full((D, LANES)), full((LANES, ROUTE_SUB))])
    cnt = jax.ShapeDtypeStruct((N_EXPERTS, LANES), F32)
    return pl.pallas_call(
        functools.partial(_outproj_kernel, n_in=n_in, dual_x=dual_x),
        grid=(T // tm,), in_specs=in_specs,
        out_specs=[tile, pl.BlockSpec((tm * SUBLANES, LANES), lambda i: (i, 0)),
                   pl.BlockSpec((SUBLANES, tm), lambda i: (0, i)), full(cnt.shape)],
        out_shape=[jax.ShapeDtypeStruct((T, D), F32), jax.ShapeDtypeStruct((T * SUBLANES, LANES), F32),
                   jax.ShapeDtypeStruct((SUBLANES, T), F32), cnt],
        scratch_shapes=[pltpu.VMEM((kdim, D), BF16), pltpu.VMEM((2 * LANES, D), BF16),
                        pltpu.VMEM((ROUTE_ROWS, ROUTE_SUB), F32), pltpu.VMEM((tm, D), F32)],
        compiler_params=_cparams(1),
        name=f"outproj{layer}",
    )(*ys, w_out, *xs, mods, mods, mods, norm_w.reshape(1, D), w_router,
      jnp.broadcast_to(b_router.reshape(LANES, 1), (LANES, ROUTE_SUB)))


def _moe_meta(counts):
    tm = MOE_TM
    experts = jnp.arange(N_EXPERTS, dtype=jnp.int32)
    counts = jnp.max(counts, axis=1).astype(jnp.int32)
    padded = ((counts + tm - 1) // tm) * tm
    ends = jnp.cumsum(padded)
    tile_start = jnp.arange(MOE_TILES, dtype=jnp.int32) * tm
    te = jnp.sum((tile_start[:, None] >= ends[None, :]).astype(jnp.int32), axis=1)
    last = jnp.max(jnp.where(counts > 0, experts, 0))
    meta = jnp.concatenate([jnp.minimum(te, last), ends[-1:] // tm]).astype(jnp.int32)
    starts = ends - padded
    pads = jnp.concatenate([starts + counts, ends[-1:], ends, jnp.full((1,), MOE_ROWS)]).astype(jnp.int32)
    return starts.astype(jnp.int32), pads, meta


def _expert_changed(meta_ref, j):
    return (j == 0) | (meta_ref[j] != meta_ref[jnp.maximum(j - 1, 0)])


def _moe_up_kernel(pos1_ref, pos2_ref, pads_ref, meta_ref, tr_hbm, wg_ref, wu_ref, a_ref, rowmap_ref,
                   tr_scr, g0, g1, wg_bf, wu_bf, sem):
    j = pl.program_id(0)
    tm = MOE_TM
    ntiles = meta_ref[MOE_TILES]

    def gather(tile, dst):
        for mi in range(tm):
            tok = jnp.minimum(rowmap_ref[tile * tm + mi] >> 1, T - 1)
            dst[mi * SUBLANES:(mi + 1) * SUBLANES, :] = tr_scr[pl.ds(pl.multiple_of(tok * SUBLANES, SUBLANES), SUBLANES), :]

    @pl.when(j == 0)
    def _():
        load = pltpu.make_async_copy(tr_hbm, tr_scr, sem)
        load.start()

        def clear(c, carry):
            for i in range(SUBLANES):
                rowmap_ref[c * SUBLANES + i] = 2 * T
            return carry
        for k in range(N_EXPERTS + 1):
            lax.fori_loop(pads_ref[k] // SUBLANES, pads_ref[N_EXPERTS + 1 + k] // SUBLANES, clear, 0)

        def place(t, carry):
            rowmap_ref[pos1_ref[t]] = 2 * t
            rowmap_ref[pos2_ref[t]] = 2 * t + 1
            return carry
        lax.fori_loop(0, T, place, 0, unroll=8)
        load.wait()
        gather(0, g0)

    def compute(cur, nxt):
        gather(jnp.minimum(j + 1, ntiles - 1), nxt)
        x = _tokmajor_to_std(cur, tm).astype(BF16)
        g = jnp.dot(x, wg_bf[...], preferred_element_type=F32)
        u = jnp.dot(x, wu_bf[...], preferred_element_type=F32)
        a_ref[...] = (_silu(g) * u).astype(a_ref.dtype)

    @pl.when(j < ntiles)
    def _():
        @pl.when(_expert_changed(meta_ref, j))
        def _():
            wg_bf[...] = wg_ref[...].astype(BF16)
            wu_bf[...] = wu_ref[...].astype(BF16)

        pl.when(j % 2 == 0)(functools.partial(compute, g0, g1))
        pl.when(j % 2 == 1)(functools.partial(compute, g1, g0))

    @pl.when(j >= ntiles)
    def _():
        a_ref[...] = jnp.zeros(a_ref.shape, a_ref.dtype)


def _moe_down_kernel(rowmap_ref, cpair_ref, meta_ref, a_ref, wd_ref, *rest, layer, final):
    if final:
        x_hbm, mods_ref, fw_ref, out_c, out_l, acc, y0, y1, wd_bf, xin, xout, io_sems = rest
    else:
        out_hbm, acc, y0, y1, wd_bf = rest
    j = pl.program_id(0)
    tm = MOE_TM
    zrows = MERGE_TM
    ntiles = meta_ref[MOE_TILES]

    @pl.when(j == 0)
    def _():
        def zero(i, carry):
            acc[pl.ds(pl.multiple_of(i * zrows, zrows), zrows), :] = jnp.zeros((zrows, LANES), F32)
            return carry
        lax.fori_loop(0, ACC_ROWS // zrows, zero, 0)

    def matmul(dst):
        y = jnp.dot(a_ref[...], wd_bf[...], preferred_element_type=F32)
        for k in range(D // LANES):
            dst[pl.ds(k, tm, stride=SUBLANES), :] = y[:, k * LANES:(k + 1) * LANES]

    def scatter(tile, src):
        for b in range(tm // SUBLANES):
            ents = [rowmap_ref[tile * tm + b * SUBLANES + i] for i in range(SUBLANES)]
            offs = [pl.multiple_of((e >> 1) * SUBLANES, SUBLANES) for e in ents]
            olds = [acc[pl.ds(o, SUBLANES), :] for o in offs]
            for i, o in enumerate(offs):
                r = (b * SUBLANES + i) * SUBLANES
                acc[pl.ds(o, SUBLANES), :] = olds[i] + cpair_ref[ents[i]] * src[r:r + SUBLANES, :]

    has_mm = j < ntiles
    has_sc = (j >= 1) & (j <= ntiles)

    @pl.when(has_mm)
    def _():
        @pl.when(_expert_changed(meta_ref, j))
        def _():
            wd_bf[...] = wd_ref[...].astype(BF16)

    for par, (cur, prev) in enumerate(((y0, y1), (y1, y0))):
        mine = (j % 2) == par

        @pl.when(mine & has_mm & has_sc)
        def _():
            matmul(cur)
            scatter(j - 1, prev)

        @pl.when(mine & has_mm & jnp.logical_not(has_sc))
        def _():
            matmul(cur)

        @pl.when(mine & jnp.logical_not(has_mm) & has_sc)
        def _():
            scatter(j - 1, prev)

    @pl.when(j == pl.num_programs(0) - 1)
    def _():
        if not final:
            pltpu.sync_copy(acc.at[0:T * SUBLANES, :], out_hbm)
            return
        ft = MERGE_TM
        n_t = T // ft
        n_c = T_CTX // ft

        def fetch(i):
            return pltpu.make_async_copy(x_hbm.at[i * ft:(i + 1) * ft, :], xin.at[i % 2], io_sems.at[i % 2])

        def flush(i):
            dst = out_c.at[i * ft:(i + 1) * ft, :] if i < n_c else out_l.at[(i - n_c) * ft:(i - n_c + 1) * ft, :]
            return pltpu.make_async_copy(xout.at[i % 2], dst, io_sems.at[2 + i % 2])

        fetch(0).start()
        for i in range(n_t):
            if i + 1 < n_t:
                fetch(i + 1).start()
            fetch(i).wait()
            moe = jnp.concatenate([acc[pl.ds(i * ft * SUBLANES + k, ft, stride=SUBLANES), :]
                                   for k in range(D // LANES)], axis=1)
            row = 0 if i * ft < T_CTX else 1 + (i * ft - T_CTX) // DEC_SEQ
            x = xin[i % 2] + mods_ref[(layer * SUBLANES + row) * 6 + 5] * moe
            if i >= 2:
                flush(i - 2).wait()
            xout[i % 2] = _rms(x, fw_ref[...])
            flush(i).start()
        for i in range(max(n_t - 2, 0), n_t):
            flush(i).wait()


def _moe(tr, route_t, counts, layer, w_gate, w_up, w_down, xmid=None, mods=None, final_w=None):
    tm = MOE_TM
    starts, pads, meta = _moe_meta(counts)
    experts = jnp.arange(N_EXPERTS, dtype=jnp.int32)

    def position(e, r):
        sel = e.astype(jnp.int32)[:, None] == experts[None, :]
        return jnp.sum(jnp.where(sel, starts[None, :], 0), axis=1) + r.astype(jnp.int32)

    pos1 = position(route_t[0], route_t[4])
    pos2 = position(route_t[1], route_t[5])
    cpair = jnp.concatenate([jnp.stack([route_t[2], route_t[3]], axis=1).reshape(2 * T), jnp.zeros((2,), F32)])

    def wspec(shape, n):
        return pl.BlockSpec((None, None) + shape,
                            lambda j, *pre: (layer, pre[n - 1][jnp.minimum(j, MOE_TILES - 1)], 0, 0))

    gscr = pltpu.VMEM((tm * SUBLANES, LANES), F32)
    act, rowmap = pl.pallas_call(
        _moe_up_kernel,
        grid_spec=pltpu.PrefetchScalarGridSpec(
            num_scalar_prefetch=4, grid=(MOE_TILES,),
            in_specs=[pl.BlockSpec(memory_space=pl.ANY), wspec((D, D_EXPERT), 4), wspec((D, D_EXPERT), 4)],
            out_specs=[pl.BlockSpec((tm, D_EXPERT), lambda j, *pre: (j, 0)), pl.BlockSpec(memory_space=pltpu.SMEM)],
            scratch_shapes=[pltpu.VMEM((T * SUBLANES, LANES), F32), gscr, gscr,
                            pltpu.VMEM((D, D_EXPERT), BF16), pltpu.VMEM((D, D_EXPERT), BF16),
                            pltpu.SemaphoreType.DMA(())]),
        out_shape=[jax.ShapeDtypeStruct((MOE_ROWS, D_EXPERT), BF16), jax.ShapeDtypeStruct((MOE_ROWS,), jnp.int32)],
        compiler_params=_cparams(1),
        name=f"moe_up{layer}",
    )(pos1, pos2, pads, meta, tr, w_gate, w_up)

    final = final_w is not None
    any_spec = pl.BlockSpec(memory_space=pl.ANY)
    in_specs = [pl.BlockSpec((tm, D_EXPERT), lambda j, *pre: (jnp.minimum(j, MOE_TILES - 1), 0)),
                wspec((D_EXPERT, D), 3)]
    args = [rowmap, cpair, meta, act, w_down]
    scratch = [pltpu.VMEM((ACC_ROWS, LANES), F32), gscr, gscr, pltpu.VMEM((D_EXPERT, D), BF16)]
    if final:
        stage = pltpu.VMEM((2, MERGE_TM, D), F32)
        in_specs += [any_spec, pl.BlockSpec(mods.shape, lambda j, *pre: (0, 0, 0)),
                     pl.BlockSpec((1, D), lambda j, *pre: (0, 0))]
        args += [xmid, mods, final_w.reshape(1, D)]
        scratch += [stage, stage, pltpu.SemaphoreType.DMA((4,))]
        out_shape = [jax.ShapeDtypeStruct((T_CTX, D), F32), jax.ShapeDtypeStruct((T_LAT, D), F32)]
    else:
        out_shape = [jax.ShapeDtypeStruct((T * SUBLANES, LANES), F32)]
    return pl.pallas_call(
        functools.partial(_moe_down_kernel, layer=layer, final=final),
        grid_spec=pltpu.PrefetchScalarGridSpec(
            num_scalar_prefetch=3, grid=(MOE_TILES + 1,), in_specs=in_specs,
            out_specs=[any_spec] * len(out_shape), scratch_shapes=scratch),
        out_shape=out_shape,
        compiler_params=_cparams(1),
        name=f"moe_down{layer}",
    )(*args)


def _pad_lanes(v):
    return jnp.pad(v.astype(F32), (0, LANES - v.shape[0])).reshape(1, LANES)


def kernel(x_prompt, x_sample, cache_k_attn, cache_v_attn, state_ssd_fwd, state_ssd_bwd, state_gla_fwd, state_gla_bwd, c, c_ctx, w_ada, b_ada, norm_mix_w, norm_ffn_w, w_in_even, conv_w, conv_b, dt_bias_fwd, dt_bias_bwd, a_log_fwd, a_log_bwd, d_skip, ssd_norm_w, attn_sink, w_out_even, w_in_odd, w_gk2_fwd, b_gk_fwd, w_gk2_bwd, b_gk_bwd, gla_norm_w, w_out_odd, w_router_group, b_router_group, w_router_expert, b_router_expert, w_gate_exp, w_up_exp, w_down_exp, final_norm_w):
    depth = w_ada.shape[0]
    assert depth == 2 and x_prompt.shape == (BATCH, SEQ, D) and x_sample.shape == (DEC_BATCH, DEC_SEQ, D)

    cond8 = jnp.concatenate([c_ctx[None, :], c, jnp.zeros((SUBLANES - 1 - DEC_BATCH, D), F32)], axis=0)
    mods = _adaln(cond8, w_ada, b_ada).reshape(depth * SUBLANES * 6, 1, D)
    xs0 = (x_prompt.reshape(T_CTX, D), x_sample.reshape(T_LAT, D))

    def router_params(i):
        wr = jnp.concatenate([w_router_group[i], w_router_expert[i],
                              jnp.zeros((D, LANES - N_GROUPS - N_EXPERTS), F32)], axis=1)
        return wr, _pad_lanes(jnp.concatenate([b_router_group[i], b_router_expert[i]]))

    n_zxbc = 2 * D_SSD + 2 * SSD_GROUPS * SSD_N
    n_dt = 2 * SSD_HEADS
    half = PROJ_TN // 2
    r_bc, r_q = 2 * D_SSD, n_zxbc + n_dt
    r_kv = r_q + D
    rows0 = [(0, half), (D_SSD, D_SSD + half), (r_q, r_q + half), (r_bc, r_kv)]
    p0, p0dt = _modproj(xs0, mods, 0, norm_mix_w[0], jnp.transpose(w_in_even[0]), rows0, n_zxbc)

    dtb = _pad_lanes(jnp.concatenate([dt_bias_fwd[0], dt_bias_bwd[0]]))
    alog = _pad_lanes(jnp.concatenate([a_log_fwd[0], a_log_bwd[0]]))
    dskip = jnp.repeat(d_skip[0], SSD_P).reshape(1, D_SSD)
    ssd_args = (conv_w[0], conv_b[0], dtb, alog, dskip, ssd_norm_w[0])
    y_ssd_c, ssd_f, ssd_b = _ssd(p0, p0dt, 0, BATCH, SEQ, *ssd_args)
    y_ssd_l, _, _ = _ssd(p0, p0dt, T_CTX, DEC_BATCH, DEC_SEQ, *ssd_args,
                         h0f=state_ssd_fwd[:, 0].reshape(DEC_BATCH, D_SSD, SSD_N),
                         h0b=state_ssd_bwd[:, 0].reshape(DEC_BATCH, D_SSD, SSD_N))
    sink = _pad_lanes(attn_sink[0])
    y_att_c, new_kt, new_vt = _ctx_attn(p0, sink)
    def cache_in(t):
        return jnp.transpose(t[:, 0], (0, 2, 3, 1)).reshape(DEC_BATCH, ATT_KV_DIM, PAST_LEN)

    y_att_l = _lat_attn(p0, cache_in(cache_k_attn), cache_in(cache_v_attn), sink)
    xmid0, tr0, route0, cnt0 = _outproj_route([y_ssd_c, y_ssd_l, y_att_c, y_att_l], w_out_even[0], xs0, mods, 0,
                                              norm_ffn_w[0], *router_params(0))
    (moe0,) = _moe(tr0, route0, cnt0, 0, w_gate_exp, w_up_exp, w_down_exp)

    dk_all = GLA_HEADS * GLA_DK
    n_qkvr = 2 * dk_all + 2 * GLA_HEADS * GLA_DV
    n_odd = w_in_odd.shape[2]
    p1, p1lr, x1 = _modproj(xmid0, mods, 1, norm_mix_w[1], jnp.transpose(w_in_odd[0]),
                            [(k * PROJ_TN, k * PROJ_TN + half) for k in range(n_qkvr // PROJ_TN)], n_odd - LANES,
                            moe=moe0)
    lr0 = LANES - 2 * GLA_LOWRANK
    w2f = jnp.zeros((LANES, dk_all), F32).at[lr0:lr0 + GLA_LOWRANK].set(w_gk2_fwd[0])
    w2b = jnp.zeros((LANES, dk_all), F32).at[lr0 + GLA_LOWRANK:].set(w_gk2_bwd[0])
    gla_args = (w2f, w2b, b_gk_fwd[0], b_gk_bwd[0], gla_norm_w[0])
    y_gla_c, gla_f, gla_b = _gla(p1, p1lr, 0, BATCH, SEQ, *gla_args)
    y_gla_l, _, _ = _gla(p1, p1lr, T_CTX, DEC_BATCH, DEC_SEQ, *gla_args,
                         s0f=state_gla_fwd[:, 0].reshape(DEC_BATCH, dk_all, GLA_DV),
                         s0b=state_gla_bwd[:, 0].reshape(DEC_BATCH, dk_all, GLA_DV))
    xmid1, tr1, route1, cnt1 = _outproj_route([y_gla_c, y_gla_l], w_out_odd[0], (x1,), mods, 1,
                                              norm_ffn_w[1], *router_params(1))
    y_c, y_l = _moe(tr1, route1, cnt1, 1, w_gate_exp, w_up_exp, w_down_exp, xmid1, mods, final_w=final_norm_w)

    y_prompt = y_c.reshape(BATCH, SEQ, D)
    y_sample = y_l.reshape(DEC_BATCH, DEC_SEQ, D)
    def cache_out(t):
        return jnp.transpose(t.reshape(BATCH, 1, ATT_KV, ATT_HD, SEQ), (0, 1, 4, 2, 3))

    new_k, new_v = cache_out(new_kt), cache_out(new_vt)
    return (y_prompt, y_sample, new_k, new_v,
            ssd_f.reshape(BATCH, 1, SSD_HEADS, SSD_P, SSD_N), ssd_b.reshape(BATCH, 1, SSD_HEADS, SSD_P, SSD_N),
            gla_f.reshape(BATCH, 1, GLA_HEADS, GLA_DK, GLA_DV), gla_b.reshape(BATCH, 1, GLA_HEADS, GLA_DK, GLA_DV))
```

```python
import functools

import numpy as np
import jax
import jax.numpy as jnp
from jax import lax
from jax.experimental import pallas as pl
from jax.experimental.pallas import tpu as pltpu

F32 = jnp.float32
BF16 = jnp.bfloat16

D = 1024
BATCH, SEQ = 16, 256
DEC_BATCH, DEC_SEQ = 2, 1024
PAST_LEN = 512
GRID_W = 64
EPS = 1e-6
T_CTX = BATCH * SEQ
T_LAT = DEC_BATCH * DEC_SEQ
T = T_CTX + T_LAT

SSD_HEADS, SSD_P, SSD_N, SSD_GROUPS = 16, 64, 128, 2
SSD_CONV = 5
SSD_L = 128
D_SSD = SSD_HEADS * SSD_P
HEADS_PER_GROUP = SSD_HEADS // SSD_GROUPS
GROUP_W = HEADS_PER_GROUP * SSD_P

ATT_HEADS, ATT_KV, ATT_HD = 16, 4, 64
ATT_KV_DIM = ATT_KV * ATT_HD
WINDOW = 128
ATT_BLOCK = 128
ATT_SCALE = ATT_HD ** -0.5
ROPE_BASE = 10000.0

GLA_HEADS, GLA_DK, GLA_DV = 4, 128, 256
GLA_C = 64
GLA_GATE_NORM = 16.0
GLA_LOWRANK = 16

N_GROUPS, EXP_PER_GROUP = 4, 4
N_EXPERTS = 16
D_EXPERT = 512

LANES = 128
SUBLANES = 8
VMEM_LIMIT = 56 * 1024 * 1024

P0_Z, P0_X, P0_Q, P0_BC, P0_K, P0_V = 0, 1024, 2048, 3072, 3584, 3840
BC_W = 2 * SSD_GROUPS * SSD_N
P1_Q, P1_K, P1_V, P1_R = 0, 512, 1024, 2048

MOE_TM = 512
MOE_TILES = (2 * T) // MOE_TM + N_EXPERTS
MOE_ROWS = MOE_TILES * MOE_TM
ACC_ROWS = (T + SUBLANES) * SUBLANES
MERGE_TM = 512


def _cparams(n_axes, vmem=VMEM_LIMIT):
    return pltpu.CompilerParams(dimension_semantics=("arbitrary",) * n_axes, vmem_limit_bytes=vmem)


def _silu(x):
    return x / (1.0 + jnp.exp(-x))


def _softplus(x):
    return jnp.maximum(x, 0.0) + jnp.log(1.0 + jnp.exp(-jnp.abs(x)))


def _mm(a, b):
    return jnp.dot(a.astype(BF16), b.astype(BF16), preferred_element_type=F32)


def _mm_nt(a, b):
    return lax.dot_general(a.astype(BF16), b.astype(BF16), (((1,), (1,)), ((), ())),
                           preferred_element_type=F32)


def _mm_tn(a, b):
    return lax.dot_general(a.astype(BF16), b.astype(BF16), (((0,), (0,)), ((), ())),
                           preferred_element_type=F32)


def _rms(x, w):
    return x * lax.rsqrt(jnp.mean(x * x, axis=-1, keepdims=True) + EPS) * w


def _cumsum_rows(x, n):
    row = lax.broadcasted_iota(jnp.int32, x.shape, 0)
    s = 1
    while s < n:
        x = x + jnp.where(row >= s, pltpu.roll(x, s, 0), 0.0)
        s *= 2
    return x


def _mod_row(tok0):
    return jnp.where(tok0 < T_CTX, 0, 1 + (tok0 - T_CTX) // DEC_SEQ)


ADA_TN = 1536


def _adaln_kernel(c_ref, w_ref, b_ref, o_ref):
    s = _silu(c_ref[...])
    o_ref[0] = _mm(s, w_ref[0]) + b_ref[0]


def _adaln(cond8, w_ada, b_ada):
    depth = w_ada.shape[0]
    return pl.pallas_call(
        _adaln_kernel,
        grid=(depth, 6 * D // ADA_TN),
        in_specs=[
            pl.BlockSpec((SUBLANES, D), lambda l, j: (0, 0)),
            pl.BlockSpec((1, D, ADA_TN), lambda l, j: (l, 0, j)),
            pl.BlockSpec((1, 1, ADA_TN), lambda l, j: (l, 0, j)),
        ],
        out_specs=pl.BlockSpec((1, SUBLANES, ADA_TN), lambda l, j: (l, 0, j)),
        out_shape=jax.ShapeDtypeStruct((depth, SUBLANES, 6 * D), F32),
        compiler_params=_cparams(2),
        name="adaln",
    )(cond8, w_ada, b_ada.reshape(depth, 1, 6 * D))


def _mod_spec(layer, chunk, tm, tile_of=lambda i, *_: i):
    return pl.BlockSpec((1, 1, D), lambda *g: ((layer * SUBLANES + _mod_row(tile_of(*g) * tm)) * 6 + chunk, 0, 0))


def _tokmajor_to_std(ref, tm):
    return jnp.concatenate([ref[pl.ds(k, tm, stride=SUBLANES), :] for k in range(D // LANES)], axis=1)


PROJ_TM = 1024
PROJ_TN = 1024
PROJ_SUB = 256


def _ctx_lat_specs(tm, width=D):
    n_ctx = T_CTX // tm
    return (pl.BlockSpec((tm, width), lambda i, *_: (jnp.minimum(i, n_ctx - 1), 0)),
            pl.BlockSpec((tm, width), lambda i, *_: (jnp.maximum(i - n_ctx, 0), 0)))


def _modproj_kernel(*refs, dual_x):
    it = iter(refs)
    if dual_x:
        xc_ref, xl_ref = next(it), next(it)
    else:
        x_ref, moe_ref, g2_ref = next(it), next(it), next(it)
    sh_ref, sc_ref, nw_ref, wlo_ref, whi_ref, ws_ref, o_ref, os_ref = (next(it) for _ in range(8))
    xo_ref = None if dual_x else next(it)
    h_all, w_bf = next(it), next(it)
    j, i = pl.program_id(0), pl.program_id(1)
    tm = PROJ_TM
    rows = pl.ds(pl.multiple_of(i * tm, tm), tm)

    @pl.when(i == 0)
    def _():
        half = wlo_ref.shape[0]
        w_bf[0:half, :] = wlo_ref[...].astype(BF16)
        w_bf[half:2 * half, :] = whi_ref[...].astype(BF16)

    @pl.when(j == 0)
    def _():
        ts = PROJ_SUB
        for sub in range(tm // ts):
            rs = slice(sub * ts, (sub + 1) * ts)
            if dual_x:
                x = jnp.where(i < T_CTX // tm, xc_ref[rs, :], xl_ref[rs, :])
            else:
                moe = jnp.concatenate([moe_ref[pl.ds(sub * ts * SUBLANES + k, ts, stride=SUBLANES), :]
                                       for k in range(D // LANES)], axis=1)
                x = x_ref[rs, :] + g2_ref[0] * moe
                xo_ref[rs, :] = x
            h = (_rms(x, nw_ref[...]) * (1.0 + sc_ref[0]) + sh_ref[0]).astype(BF16)
            h_all[pl.ds(pl.multiple_of(i * tm + sub * ts, ts), ts), :] = h
            os_ref[rs, :] = _mm_nt(h, ws_ref[...])
            o_ref[rs, :] = _mm_nt(h, w_bf[...]).astype(o_ref.dtype)

    @pl.when(j > 0)
    def _():
        o_ref[...] = _mm_nt(h_all[rows, :], w_bf[...]).astype(o_ref.dtype)


def _modproj(xs, mods, layer, norm_w, wt, tile_rows, small_row, moe=None):
    tm, tn = PROJ_TM, PROJ_TN
    dual_x = moe is None
    n_tiles = len(tile_rows)
    n_i, n_ctx = T // tm, T_CTX // tm

    def w_row(side):
        def index(j, i):
            r = jnp.int32(tile_rows[0][side])
            for k in range(1, n_tiles):
                r = jnp.where(j == k, tile_rows[k][side], r)
            return pl.multiple_of(r, SUBLANES), 0
        return index

    def tok(j, i):
        return jnp.where(j == 0, i, n_i - 1)

    tile = pl.BlockSpec((tm, D), lambda j, i: (tok(j, i), 0))
    if dual_x:
        in_specs = [pl.BlockSpec((tm, D), lambda j, i: (jnp.minimum(tok(j, i), n_ctx - 1), 0)),
                    pl.BlockSpec((tm, D), lambda j, i: (jnp.maximum(tok(j, i) - n_ctx, 0), 0))]
        args = list(xs)
    else:
        in_specs = [tile, pl.BlockSpec((tm * SUBLANES, LANES), lambda j, i: (tok(j, i), 0)),
                    _mod_spec(layer - 1, 5, tm, tok)]
        args = [xs, moe, mods]
    in_specs += [_mod_spec(layer, 0, tm, tok), _mod_spec(layer, 1, tm, tok), pl.BlockSpec((1, D), lambda j, i: (0, 0)),
                 pl.BlockSpec((pl.Element(tn // 2), pl.Element(D)), w_row(0)),
                 pl.BlockSpec((pl.Element(tn // 2), pl.Element(D)), w_row(1)),
                 pl.BlockSpec((pl.Element(LANES), pl.Element(D)), lambda j, i: (small_row, 0))]
    args += [mods, mods, norm_w.reshape(1, D), wt, wt, wt]
    out_specs = [pl.BlockSpec((tm, tn), lambda j, i: (i, j)),
                 pl.BlockSpec((tm, LANES), lambda j, i: (tok(j, i), 0))]
    out_shape = [jax.ShapeDtypeStruct((T, n_tiles * tn), BF16), jax.ShapeDtypeStruct((T, LANES), F32)]
    if not dual_x:
        out_specs.append(tile)
        out_shape.append(jax.ShapeDtypeStruct((T, D), F32))
    return pl.pallas_call(
        functools.partial(_modproj_kernel, dual_x=dual_x),
        grid=(n_tiles, n_i), in_specs=in_specs, out_specs=out_specs, out_shape=out_shape,
        scratch_shapes=[pltpu.VMEM((T, D), BF16), pltpu.VMEM((tn, D), BF16)],
        compiler_params=_cparams(2),
        name=f"modproj{layer}",
    )(*args)


def _expand_heads(v, off):
    hi = (lax.broadcasted_iota(jnp.int32, (v.shape[0], LANES), 1) >= SSD_P).astype(jnp.int32)
    tiles = [jnp.take_along_axis(v, hi + (off + 2 * q), axis=1) for q in range(SSD_HEADS // 2)]
    return jnp.concatenate(tiles, axis=1)


def _ssd_kernel(*refs, seq, has_h0):
    if has_h0:
        (z_ref, x_ref, bc_ref, dt_ref, cwx_ref, cwbc_ref, cbx_ref, cbbc_ref, dtb_ref, alog_ref, dsk_ref,
         nw_ref, h0f_ref, h0b_ref, y_ref, sf_ref, sb_ref,
         xpad, bcpad, xc, bcc, a_scr, dt_scr, yf, yb, hf, hb) = refs
    else:
        (z_ref, x_ref, bc_ref, dt_ref, cwx_ref, cwbc_ref, cbx_ref, cbbc_ref, dtb_ref, alog_ref, dsk_ref,
         nw_ref, y_ref, sf_ref, sb_ref,
         xpad, bcpad, xc, bcc, a_scr, dt_scr, yf, yb, hf, hb) = refs
    L = SSD_L
    nc = seq // L
    pad = SUBLANES
    half = SSD_CONV // 2

    for buf, src, cw, cb, dst in ((xpad, x_ref, cwx_ref, cbx_ref, xc), (bcpad, bc_ref, cwbc_ref, cbbc_ref, bcc)):
        width = buf.shape[1]
        buf[0:pad, :] = jnp.zeros((pad, width), F32)
        buf[pad + seq:2 * pad + seq, :] = jnp.zeros((pad, width), F32)
        buf[pad:pad + seq, :] = src[...].astype(F32)
        for blk in range(nc):
            acc = jnp.broadcast_to(cb[...], (L, width))
            for j in range(SSD_CONV):
                r0 = pad - half + j + blk * L
                acc = acc + cw[j:j + 1, :] * buf[r0:r0 + L, :]
            dst[blk * L:(blk + 1) * L, :] = _silu(acc)

    lane = lax.broadcasted_iota(jnp.int32, (seq, LANES), 1)
    dts = jnp.where(lane < 2 * SSD_HEADS, _softplus(dt_ref[...] + dtb_ref[...]), 0.0)
    dt_scr[...] = dts
    a_scr[...] = dts * (-jnp.exp(alog_ref[...]))

    if has_h0:
        hf[...] = h0f_ref[...]
        hb[...] = h0b_ref[...]
    else:
        hf[...] = jnp.zeros(hf.shape, F32)
        hb[...] = jnp.zeros(hb.shape, F32)

    row = lax.broadcasted_iota(jnp.int32, (L, L), 0)
    col = lax.broadcasted_iota(jnp.int32, (L, L), 1)
    lane_l = lax.broadcasted_iota(jnp.int32, (L, LANES), 1)
    lo_half = lane_l < SSD_P

    def chunk(c, fwd, h_scr):
        off = 0 if fwd else SSD_HEADS
        r0 = pl.multiple_of(c * L, L)
        a = a_scr[pl.ds(r0, L), :]
        dt = dt_scr[pl.ds(r0, L), :]
        cs = _cumsum_rows(a, L)
        total = cs[L - 1:L, :]
        if fwd:
            u = cs
            rvec = jnp.exp(cs)
            ed = jnp.exp(total - cs) * dt
            keep = col <= row
        else:
            ex = cs - a
            u = -ex
            rvec = jnp.exp(total - ex)
            ed = jnp.exp(ex) * dt
            keep = col >= row
        ut = jnp.transpose(u)
        dtt = jnp.transpose(dt)
        tcol = jnp.transpose(jnp.broadcast_to(total, (L, LANES)))[:, 0:1]
        rexp = _expand_heads(rvec, off)
        edexp = _expand_heads(ed, off)
        x = xc[pl.ds(r0, L), :]
        bc = bcc[pl.ds(r0, L), :]
        outs = []
        for g in range(SSD_GROUPS):
            bg = bc[:, g * SSD_N:(g + 1) * SSD_N]
            cg = bc[:, SSD_GROUPS * SSD_N + g * SSD_N:SSD_GROUPS * SSD_N + (g + 1) * SSD_N]
            cbm = _mm_nt(cg, bg)
            hg = h_scr[g * GROUP_W:(g + 1) * GROUP_W, :]
            xg = x[:, g * GROUP_W:(g + 1) * GROUP_W]
            y_off = _mm_nt(cg, hg) * rexp[:, g * GROUP_W:(g + 1) * GROUP_W]
            tiles = []
            for p in range(HEADS_PER_GROUP // 2):
                xt = xg[:, p * LANES:(p + 1) * LANES]
                acc = None
                for s in range(2):
                    h = off + g * HEADS_PER_GROUP + 2 * p + s
                    seg = u[:, h:h + 1] - ut[h:h + 1, :]
                    m = cbm * jnp.exp(jnp.where(keep, seg, -jnp.inf)) * dtt[h:h + 1, :]
                    xm = jnp.where(lo_half if s == 0 else jnp.logical_not(lo_half), xt, 0.0)
                    d = _mm(m, xm)
                    acc = d if acc is None else acc + d
                tiles.append(acc)
            outs.append(y_off + jnp.concatenate(tiles, axis=1))
            decs = []
            for hh in range(HEADS_PER_GROUP):
                h = off + g * HEADS_PER_GROUP + hh
                decs.append(jnp.broadcast_to(jnp.exp(tcol[h:h + 1, :]), (SSD_P, SSD_N)))
            dec = jnp.concatenate(decs, axis=0)
            h_scr[g * GROUP_W:(g + 1) * GROUP_W, :] = dec * hg + _mm_tn(xg * edexp[:, g * GROUP_W:(g + 1) * GROUP_W], bg)
        return r0, jnp.concatenate(outs, axis=1)

    def body(i, carry):
        r0, y = chunk(i, True, hf)
        yf[pl.ds(r0, L), :] = y
        r0, y = chunk(nc - 1 - i, False, hb)
        yb[pl.ds(r0, L), :] = y
        return carry

    lax.fori_loop(0, nc, body, 0, unroll=True)
    sf_ref[...] = hf[...]
    sb_ref[...] = hb[...]

    for blk in range(nc):
        rs = slice(blk * L, (blk + 1) * L)
        y = yf[rs, :] + yb[rs, :] + dsk_ref[...] * xc[rs, :]
        y = y * _silu(z_ref[rs, :].astype(F32))
        y_ref[rs, :] = _rms(y, nw_ref[...]).astype(y_ref.dtype)


def _ssd(p0, p0dt, tok0, nseq, seq, cw, cb, dtb, alog, dskip, nw, h0f=None, h0b=None):
    has_h0 = h0f is not None
    b0 = tok0 // seq

    def cols(width, start):
        return pl.BlockSpec((seq, width), lambda s: (b0 + s, start // width))

    def full(shape):
        return pl.BlockSpec(shape, lambda s: (0,) * len(shape))

    in_specs = [cols(D_SSD, P0_Z), cols(D_SSD, P0_X), cols(BC_W, P0_BC), cols(LANES, 0),
                full((SSD_CONV, D_SSD)), full((SSD_CONV, BC_W)), full((1, D_SSD)), full((1, BC_W)),
                full((1, LANES)), full((1, LANES)), full((1, D_SSD)), full((1, D_SSD))]
    args = [p0, p0, p0, p0dt, cw[:, :D_SSD], cw[:, D_SSD:], cb[:D_SSD].reshape(1, -1), cb[D_SSD:].reshape(1, -1),
            dtb, alog, dskip, nw.reshape(1, -1)]
    st_spec = pl.BlockSpec((None, D_SSD, SSD_N), lambda s: (s, 0, 0))
    if has_h0:
        in_specs += [st_spec, st_spec]
        args += [h0f, h0b]
    st_shape = jax.ShapeDtypeStruct((nseq, D_SSD, SSD_N), F32)
    return pl.pallas_call(
        functools.partial(_ssd_kernel, seq=seq, has_h0=has_h0),
        grid=(nseq,), in_specs=in_specs,
        out_specs=[pl.BlockSpec((seq, D_SSD), lambda s: (s, 0)), st_spec, st_spec],
        out_shape=[jax.ShapeDtypeStruct((nseq * seq, D_SSD), BF16), st_shape, st_shape],
        scratch_shapes=[pltpu.VMEM((seq + 2 * SUBLANES, D_SSD), F32), pltpu.VMEM((seq + 2 * SUBLANES, BC_W), F32),
                        pltpu.VMEM((seq, D_SSD), F32), pltpu.VMEM((seq, BC_W), F32),
                        pltpu.VMEM((seq, LANES), F32), pltpu.VMEM((seq, LANES), F32),
                        pltpu.VMEM((seq, D_SSD), F32), pltpu.VMEM((seq, D_SSD), F32),
                        pltpu.VMEM((D_SSD, SSD_N), F32), pltpu.VMEM((D_SSD, SSD_N), F32)],
        compiler_params=_cparams(1),
        name=f"ssd{seq}",
    )(*args)


def _place_halves(tile, kv_in_high):
    lo = lax.broadcasted_iota(jnp.int32, tile.shape, 1) < ATT_HD
    swapped = pltpu.roll(tile, ATT_HD, 1)
    if kv_in_high:
        return jnp.where(lo, swapped, 0.0), jnp.where(lo, 0.0, tile)
    return jnp.where(lo, tile, 0.0), jnp.where(lo, 0.0, swapped)


def _place_rows(vt, kv_in_high):
    head = vt[ATT_HD:, :] if kv_in_high else vt[:ATT_HD, :]
    z = jnp.zeros_like(head)
    return jnp.concatenate([head, z], axis=0), jnp.concatenate([z, head], axis=0)


LOG2E = 1.4426950408889634
SCORE_SCALE = ATT_SCALE * LOG2E


def _sink_attend_t(score_parts, value_parts, sink2):
    m = sink2
    for s in score_parts:
        m = jnp.maximum(m, jnp.max(s, axis=0, keepdims=True))
    denom = jnp.exp2(sink2 - m)
    out = None
    for s, v in zip(score_parts, value_parts):
        p = jnp.exp2(s - m)
        denom = denom + jnp.sum(p, axis=0, keepdims=True)
        o = _mm(v, p)
        out = o if out is None else out + o
    return out * (1.0 / denom)


def _attn_schedule(n, scores, attend):
    scores(0)
    for j in range(n):
        if j + 1 < n:
            scores(j + 1)
        attend(j)


def _ctx_attn_kernel(q_ref, k_ref, v_ref, sink_ref, o_ref, kt_ref, vt_ref, s_a, s_b):
    sink2 = sink_ref[...] * LOG2E
    bufs = (s_a, s_b)
    half = SEQ // 2
    for t in range(ATT_KV_DIM // LANES):
        cols = slice(t * LANES, (t + 1) * LANES)
        kt_ref[cols, :] = jnp.transpose(k_ref[:, cols].astype(F32))
        vt_ref[cols, :] = jnp.transpose(v_ref[:, cols].astype(F32))

    def kv_tile(ref, j):
        return ref[:, (j // 2) * LANES:(j // 2 + 1) * LANES].astype(F32), (j % 2 == 1)

    def scores(j):
        k_lo, k_hi = _place_halves(*kv_tile(k_ref, j))
        qst = jnp.concatenate([q_ref[:, qt * LANES:(qt + 1) * LANES] for qt in (2 * j, 2 * j + 1)], axis=0)
        bufs[j % 2][...] = _mm_nt(jnp.concatenate([k_lo, k_hi], axis=0), qst) * SCORE_SCALE

    def attend(j):
        src = bufs[j % 2]
        v, high = kv_tile(v_ref, j)
        vts = _place_rows(jnp.transpose(v), high)
        for ql in range(2):
            qt = 2 * j + ql
            for qh in range(2):
                cols = slice(ql * SEQ + qh * half, ql * SEQ + (qh + 1) * half)
                acc = None
                for s, vv in enumerate(vts):
                    o = _sink_attend_t([src[s * SEQ:(s + 1) * SEQ, cols]], [vv], sink2[:, 2 * qt + s:2 * qt + s + 1])
                    acc = o if acc is None else acc + o
                o_ref[qh * half:(qh + 1) * half, qt * LANES:(qt + 1) * LANES] = jnp.transpose(acc).astype(o_ref.dtype)

    _attn_schedule(ATT_KV, scores, attend)


def _ctx_attn(p0, sink):
    def cols(width, start):
        return pl.BlockSpec((SEQ, width), lambda b: (b, start // width))

    sbuf = pltpu.VMEM((2 * SEQ, 2 * SEQ), F32)
    return pl.pallas_call(
        _ctx_attn_kernel,
        grid=(BATCH,),
        in_specs=[cols(D, P0_Q), cols(ATT_KV_DIM, P0_K), cols(ATT_KV_DIM, P0_V),
                  pl.BlockSpec((1, LANES), lambda b: (0, 0))],
        out_specs=[pl.BlockSpec((SEQ, D), lambda b: (b, 0))]
        + [pl.BlockSpec((None, ATT_KV_DIM, SEQ), lambda b: (b, 0, 0))] * 2,
        out_shape=[jax.ShapeDtypeStruct((T_CTX, D), BF16)] + [jax.ShapeDtypeStruct((BATCH, ATT_KV_DIM, SEQ), F32)] * 2,
        scratch_shapes=[sbuf, sbuf],
        compiler_params=_cparams(1),
        name="ctx_attn",
    )(p0, p0, p0, sink)


def _rope_tables():
    quarter = ATT_HD // 4
    t = np.arange(DEC_SEQ)
    lane = np.arange(LANES)
    inv = ROPE_BASE ** (-(lane % quarter).astype(np.float64) / quarter)
    pos = np.where(((lane % ATT_HD) < ATT_HD // 2)[None, :], (t // GRID_W)[:, None], (t % GRID_W)[:, None])
    ang = pos * inv[None, :]
    first = ((lane % (2 * quarter)) < quarter)[None, :]
    cos, sin = np.cos(ang), np.sin(ang)
    return (jnp.asarray(cos, F32), jnp.asarray(np.where(first, -sin, 0.0), F32),
            jnp.asarray(np.where(first, 0.0, sin), F32))


def _rope(x, cos, sa, sb):
    quarter = ATT_HD // 4
    return x * cos + pltpu.roll(x, LANES - quarter, 1) * sa + pltpu.roll(x, quarter, 1) * sb


def _lat_attn_kernel(q_ref, kp_ref, kc_ref, kn_ref, vp_ref, vc_ref, vn_ref, ck_ref, cv_ref,
                     cos_ref, sa_ref, sb_ref, sink_ref, o_ref, c_a, c_b, w_a, w_b):
    blk = pl.program_id(1)
    nb = pl.num_programs(1)
    B = ATT_BLOCK
    sink2 = sink_ref[...] * LOG2E
    cbufs, wbufs = (c_a, c_b), (w_a, w_b)

    def tables(b):
        r0 = pl.multiple_of(b * B, B)
        return cos_ref[pl.ds(r0, B), :], sa_ref[pl.ds(r0, B), :], sb_ref[pl.ds(r0, B), :]

    tq = tables(blk)
    tk = [tables(jnp.maximum(blk - 1, 0)), tq, tables(jnp.minimum(blk + 1, nb - 1))]
    kabs = (blk - 1) * B + lax.broadcasted_iota(jnp.int32, (3 * B, B), 0)
    qpos = blk * B + lax.broadcasted_iota(jnp.int32, (3 * B, B), 1)
    ok = (jnp.abs(qpos - kabs) <= WINDOW) & (kabs >= 0) & (kabs < nb * B)
    ok = jnp.concatenate([ok, ok], axis=1)
    ok = jnp.concatenate([ok, ok], axis=0)

    def scores(j):
        high = (j % 2 == 1)
        sl = slice((j // 2) * LANES, (j // 2 + 1) * LANES)
        kw = jnp.concatenate([_rope(r[:, sl].astype(F32), *tb) for r, tb in zip((kp_ref, kc_ref, kn_ref), tk)], axis=0)
        qs = [q_ref[:, qt * LANES:(qt + 1) * LANES].astype(F32) for qt in (2 * j, 2 * j + 1)]
        q_plain = jnp.concatenate(qs, axis=0)
        q_rope = jnp.concatenate([_rope(q, *tq) for q in qs], axis=0)
        ck = jnp.transpose(ck_ref[sl, :])
        cbufs[j % 2][...] = _mm_nt(jnp.concatenate(_place_halves(ck, high), axis=0), q_plain) * SCORE_SCALE
        win = _mm_nt(jnp.concatenate(_place_halves(kw, high), axis=0), q_rope) * SCORE_SCALE
        wbufs[j % 2][...] = jnp.where(ok, win, -jnp.inf)

    def attend(j):
        high = (j % 2 == 1)
        sl = slice((j // 2) * LANES, (j // 2 + 1) * LANES)
        vw = jnp.concatenate([jnp.transpose(r[:, sl].astype(F32)) for r in (vp_ref, vc_ref, vn_ref)], axis=1)
        vts = _place_rows(vw, high)
        cvts = _place_rows(cv_ref[sl, :], high)
        csrc, wsrc = cbufs[j % 2], wbufs[j % 2]
        for ql in range(2):
            qt = 2 * j + ql
            cols = slice(ql * B, (ql + 1) * B)
            acc = None
            for s in range(2):
                parts = [csrc[s * PAST_LEN:(s + 1) * PAST_LEN, cols], wsrc[s * 3 * B:(s + 1) * 3 * B, cols]]
                o = _sink_attend_t(parts, [cvts[s], vts[s]], sink2[:, 2 * qt + s:2 * qt + s + 1])
                acc = o if acc is None else acc + o
            o_ref[:, qt * LANES:(qt + 1) * LANES] = jnp.transpose(acc).astype(o_ref.dtype)

    _attn_schedule(ATT_KV, scores, attend)


def _lat_attn(p0, ck, cv, sink):
    nb = DEC_SEQ // ATT_BLOCK
    base = T_CTX // ATT_BLOCK

    def kv(start, shift):
        return pl.BlockSpec((ATT_BLOCK, ATT_KV_DIM),
                            lambda b, i: (base + b * nb + jnp.clip(i + shift, 0, nb - 1), start // ATT_KV_DIM))

    def full(shape):
        return pl.BlockSpec(shape, lambda b, i: (0,) * len(shape))

    cache = pl.BlockSpec((None, ATT_KV_DIM, PAST_LEN), lambda b, i: (b, 0, 0))
    cos, sa, sb = _rope_tables()
    return pl.pallas_call(
        _lat_attn_kernel,
        grid=(DEC_BATCH, nb),
        in_specs=[pl.BlockSpec((ATT_BLOCK, D), lambda b, i: (base + b * nb + i, P0_Q // D)),
                  kv(P0_K, -1), kv(P0_K, 0), kv(P0_K, 1), kv(P0_V, -1), kv(P0_V, 0), kv(P0_V, 1),
                  cache, cache, full((DEC_SEQ, LANES)), full((DEC_SEQ, LANES)), full((DEC_SEQ, LANES)),
                  full((1, LANES))],
        out_specs=pl.BlockSpec((ATT_BLOCK, D), lambda b, i: (b * nb + i, 0)),
        out_shape=jax.ShapeDtypeStruct((T_LAT, D), BF16),
        scratch_shapes=[pltpu.VMEM((2 * PAST_LEN, 2 * ATT_BLOCK), F32)] * 2
        + [pltpu.VMEM((2 * 3 * ATT_BLOCK, 2 * ATT_BLOCK), F32)] * 2,
        compiler_params=_cparams(2),
        name="lat_attn",
    )(p0, p0, p0, p0, p0, p0, p0, ck, cv, cos, sa, sb, sink)


def _log_sigmoid(x):
    return jnp.minimum(x, 0.0) - jnp.log(1.0 + jnp.exp(-jnp.abs(x)))


def _gla_kernel(*refs, seq, has_s0):
    if has_s0:
        (q_ref, k_ref, v_ref, r_ref, lr_ref, w2f_ref, w2b_ref, bf_ref, bb_ref, nw_ref, s0f_ref, s0b_ref,
         y_ref, sf_ref, sb_ref, gf, gb, yf, yb, stf, stb) = refs
    else:
        (q_ref, k_ref, v_ref, r_ref, lr_ref, w2f_ref, w2b_ref, bf_ref, bb_ref, nw_ref,
         y_ref, sf_ref, sb_ref, gf, gb, yf, yb, stf, stb) = refs
    C = GLA_C
    nc = seq // C
    lr = lr_ref[...]
    gf[...] = _log_sigmoid(_mm(lr, w2f_ref[...]) + bf_ref[...]) / GLA_GATE_NORM
    gb[...] = _log_sigmoid(_mm(lr, w2b_ref[...]) + bb_ref[...]) / GLA_GATE_NORM
    for h in range(GLA_HEADS):
        rows = slice(h * GLA_DV, (h + 1) * GLA_DV)
        if has_s0:
            stf[rows, :] = jnp.transpose(s0f_ref[h * GLA_DK:(h + 1) * GLA_DK, :])
            stb[rows, :] = jnp.transpose(s0b_ref[h * GLA_DK:(h + 1) * GLA_DK, :])
        else:
            stf[rows, :] = jnp.zeros((GLA_DV, GLA_DK), F32)
            stb[rows, :] = jnp.zeros((GLA_DV, GLA_DK), F32)

    row = lax.broadcasted_iota(jnp.int32, (C, C), 0)
    col = lax.broadcasted_iota(jnp.int32, (C, C), 1)
    qscale = GLA_DK ** -0.5

    def chunk(c, fwd):
        g_scr, y_scr, st = (gf, yf, stf) if fwd else (gb, yb, stb)
        r0 = pl.multiple_of(c * C, C)
        g = g_scr[pl.ds(r0, C), :]
        cs = _cumsum_rows(g, C)
        total = cs[C - 1:C, :]
        q = q_ref[pl.ds(r0, C), :].astype(F32) * qscale
        k = k_ref[pl.ds(r0, C), :].astype(F32)
        v = v_ref[pl.ds(r0, C), :]
        if fwd:
            qs, ks, ke = q * jnp.exp(cs), k * jnp.exp(-cs), k * jnp.exp(total - cs)
            keep = col <= row
        else:
            ex = cs - g
            qs, ks, ke = q * jnp.exp(total - ex), k * jnp.exp(ex - total), k * jnp.exp(ex)
            keep = col >= row
        dec = jnp.exp(total)
        for h in range(GLA_HEADS):
            kc = slice(h * GLA_DK, (h + 1) * GLA_DK)
            vc = slice(h * GLA_DV, (h + 1) * GLA_DV)
            s_t = st[vc, :]
            att = jnp.where(keep, _mm_nt(qs[:, kc], ks[:, kc]), 0.0)
            y_scr[pl.ds(r0, C), vc] = _mm(att, v[:, vc]) + _mm_nt(qs[:, kc], s_t)
            st[vc, :] = dec[:, kc] * s_t + _mm_tn(v[:, vc], ke[:, kc])

    def body(i, carry):
        chunk(i, True)
        chunk(nc - 1 - i, False)
        return carry

    lax.fori_loop(0, nc, body, 0, unroll=4)

    nw = nw_ref[...]
    for blk in range(seq // LANES):
        rs = slice(blk * LANES, (blk + 1) * LANES)
        y = yf[rs, :] + yb[rs, :]
        gate = _silu(r_ref[rs, :].astype(F32))
        for h in range(GLA_HEADS):
            vc = slice(h * GLA_DV, (h + 1) * GLA_DV)
            y_ref[rs, vc] = (_rms(y[:, vc], nw) * gate[:, vc]).astype(y_ref.dtype)
    for h in range(GLA_HEADS):
        rows = slice(h * GLA_DV, (h + 1) * GLA_DV)
        sf_ref[h * GLA_DK:(h + 1) * GLA_DK, :] = jnp.transpose(stf[rows, :])
        sb_ref[h * GLA_DK:(h + 1) * GLA_DK, :] = jnp.transpose(stb[rows, :])


def _gla(p1, p1lr, tok0, nseq, seq, w2f, w2b, bgf, bgb, nw, s0f=None, s0b=None):
    has_s0 = s0f is not None
    b0 = tok0 // seq
    dk_all, dv_all = GLA_HEADS * GLA_DK, GLA_HEADS * GLA_DV

    def cols(width, start):
        return pl.BlockSpec((seq, width), lambda s: (b0 + s, start // width))

    def full(shape):
        return pl.BlockSpec(shape, lambda s: (0,) * len(shape))

    in_specs = [cols(dk_all, P1_Q), cols(dk_all, P1_K), cols(dv_all, P1_V), cols(dv_all, P1_R), cols(LANES, 0),
                full((LANES, dk_all)), full((LANES, dk_all)), full((1, dk_all)), full((1, dk_all)), full((1, GLA_DV))]
    args = [p1, p1, p1, p1, p1lr, w2f, w2b, bgf.reshape(1, -1), bgb.reshape(1, -1), nw.reshape(1, -1)]
    st_spec = pl.BlockSpec((None, dk_all, GLA_DV), lambda s: (s, 0, 0))
    if has_s0:
        in_specs += [st_spec, st_spec]
        args += [s0f, s0b]
    st_shape = jax.ShapeDtypeStruct((nseq, dk_all, GLA_DV), F32)
    return pl.pallas_call(
        functools.partial(_gla_kernel, seq=seq, has_s0=has_s0),
        grid=(nseq,), in_specs=in_specs,
        out_specs=[pl.BlockSpec((seq, dv_all), lambda s: (s, 0)), st_spec, st_spec],
        out_shape=[jax.ShapeDtypeStruct((nseq * seq, dv_all), BF16), st_shape, st_shape],
        scratch_shapes=[pltpu.VMEM((seq, dk_all), F32), pltpu.VMEM((seq, dk_all), F32),
                        pltpu.VMEM((seq, dv_all), F32), pltpu.VMEM((seq, dv_all), F32),
                        pltpu.VMEM((dv_all, GLA_DK), F32), pltpu.VMEM((dv_all, GLA_DK), F32)],
        compiler_params=_cparams(1),
        name=f"gla{seq}",
    )(*args)


ROUTE_TM = 512
ROUTE_SUB = 256
ROUTE_ROWS = 32


def _split_bf16(x):
    hi = x.astype(BF16)
    return hi, (x - hi.astype(F32)).astype(BF16)


def _outproj_kernel(*refs, n_in, dual_x):
    y_refs = refs[:2 * n_in]
    n_x = 2 if dual_x else 1
    x_refs = refs[2 * n_in + 1:2 * n_in + 1 + n_x]
    w_ref = refs[2 * n_in]
    (g1_ref, sh_ref, sc_ref, nw_ref, wr_ref, br_ref,
     xo_ref, tr_ref, route_ref, cnt_ref, w_scr, wr_hl, carry, o_scr) = refs[2 * n_in + 1 + n_x:]
    is_ctx = pl.program_id(0) < T_CTX // ROUTE_TM

    @pl.when(pl.program_id(0) == 0)
    def _():
        w_scr[...] = w_ref[...].astype(BF16)
        hi, lo = _split_bf16(jnp.transpose(wr_ref[...]))
        wr_hl[0:LANES, :] = hi
        wr_hl[LANES:2 * LANES, :] = lo
        carry[...] = jnp.zeros(carry.shape, F32)

    o = None
    for i in range(n_in):
        y = jnp.where(is_ctx, y_refs[2 * i][...], y_refs[2 * i + 1][...])
        d = jnp.dot(y, w_scr[i * D:(i + 1) * D, :], preferred_element_type=F32)
        o = d if o is None else o + d
    o_scr[...] = o
    for sub in range(ROUTE_TM // ROUTE_SUB):
        _outproj_subtile(sub, is_ctx, o_scr, x_refs, dual_x, g1_ref, sh_ref, sc_ref, nw_ref, br_ref,
                         xo_ref, tr_ref, route_ref, cnt_ref, wr_hl, carry)


def _outproj_subtile(sub, is_ctx, o_scr, x_refs, dual_x, g1_ref, sh_ref, sc_ref, nw_ref, br_ref,
                     xo_ref, tr_ref, route_ref, cnt_ref, wr_hl, carry):
    tm = ROUTE_SUB
    rows = slice(sub * tm, (sub + 1) * tm)
    x_in = jnp.where(is_ctx, x_refs[0][rows, :], x_refs[1][rows, :]) if dual_x else x_refs[0][rows, :]
    x = x_in + g1_ref[0] * o_scr[rows, :]
    xo_ref[rows, :] = x
    t = _rms(x, nw_ref[...]) * (1.0 + sc_ref[0]) + sh_ref[0]
    for k in range(D // LANES):
        tr_ref[pl.ds(sub * tm * SUBLANES + k, tm, stride=SUBLANES), :] = t[:, k * LANES:(k + 1) * LANES]

    t_hi, t_lo = _split_bf16(t)
    lg = _mm_nt(wr_hl[...], t_hi)
    nr = ROUTE_ROWS
    logit = lg[0:nr, :] + lg[LANES:LANES + nr, :] + _mm_nt(wr_hl[0:LANES, :], t_lo)[0:nr, :] + br_ref[0:nr, :]
    rowf = lax.broadcasted_iota(jnp.int32, (nr, tm), 0).astype(F32)
    neg = -jnp.inf

    def first_argmax(v, vmax):
        return jnp.min(jnp.where(v == vmax, rowf, float(LANES)), axis=0, keepdims=True)

    gl = jnp.where(rowf < N_GROUPS, logit, neg)
    gmax = jnp.max(gl, axis=0, keepdims=True)
    gsel = first_argmax(gl, gmax)
    gprob = 1.0 / jnp.sum(jnp.exp(gl - gmax), axis=0, keepdims=True)
    first = N_GROUPS + EXP_PER_GROUP * gsel
    el = jnp.where((rowf >= first) & (rowf < first + EXP_PER_GROUP), logit, neg)
    m1 = jnp.max(el, axis=0, keepdims=True)
    i1 = first_argmax(el, m1)
    el2 = jnp.where(rowf == i1, neg, el)
    m2 = jnp.max(el2, axis=0, keepdims=True)
    i2 = first_argmax(el2, m2)
    e2 = jnp.exp(m2 - m1)
    c1 = gprob / (1.0 + e2)
    c2 = gprob * e2 / (1.0 + e2)
    x1 = i1 - N_GROUPS
    x2 = i2 - N_GROUPS

    erow = rowf
    hot = ((erow == x1) | (erow == x2)).astype(F32)
    tri = (lax.broadcasted_iota(jnp.int32, (tm, tm), 0) < lax.broadcasted_iota(jnp.int32, (tm, tm), 1))
    before = _mm(hot, tri.astype(F32)) + carry[...]
    r1 = jnp.sum(jnp.where(erow == x1, before, 0.0), axis=0, keepdims=True)
    r2 = jnp.sum(jnp.where(erow == x2, before, 0.0), axis=0, keepdims=True)
    total = carry[...] + _mm(hot, jnp.ones((tm, tm), F32))
    carry[...] = total
    cnt_ref[...] = total[0:N_EXPERTS, 0:LANES]
    row8 = lax.broadcasted_iota(jnp.int32, (SUBLANES, tm), 0)
    out = jnp.zeros((SUBLANES, tm), F32)
    for k, v in enumerate((x1, x2, c1, c2, r1, r2)):
        out = jnp.where(row8 == k, jnp.broadcast_to(v, (SUBLANES, tm)), out)
    route_ref[:, rows] = out


def _outproj_route(ys, w_out, xs, mods, layer, norm_w, w_router, b_router):
    tm = ROUTE_TM
    n_in = len(ys) // 2
    dual_x = len(xs) == 2
    kdim = w_out.shape[0]

    def full(shape):
        return pl.BlockSpec(shape, lambda i: (0,) * len(shape))

    tile = pl.BlockSpec((tm, D), lambda i: (i, 0))
    pair = list(_ctx_lat_specs(tm))
    in_specs = (pair * n_in + [full((kdim, D))] + (pair if dual_x else [tile])
                + [_mod_spec(layer, 2, tm), _mod_spec(layer, 3, tm), _mod_spec(layer, 4, tm),
                   full((1, D)), full((D, LANES)), full((LANES, ROUTE_SUB))])
    cnt = jax.ShapeDtypeStruct((N_EXPERTS, LANES), F32)
    return pl.pallas_call(
        functools.partial(_outproj_kernel, n_in=n_in, dual_x=dual_x),
        grid=(T // tm,), in_specs=in_specs,
        out_specs=[tile, pl.BlockSpec((tm * SUBLANES, LANES), lambda i: (i, 0)),
                   pl.BlockSpec((SUBLANES, tm), lambda i: (0, i)), full(cnt.shape)],
        out_shape=[jax.ShapeDtypeStruct((T, D), F32), jax.ShapeDtypeStruct((T * SUBLANES, LANES), F32),
                   jax.ShapeDtypeStruct((SUBLANES, T), F32), cnt],
        scratch_shapes=[pltpu.VMEM((kdim, D), BF16), pltpu.VMEM((2 * LANES, D), BF16),
                        pltpu.VMEM((ROUTE_ROWS, ROUTE_SUB), F32), pltpu.VMEM((tm, D), F32)],
        compiler_params=_cparams(1),
        name=f"outproj{layer}",
    )(*ys, w_out, *xs, mods, mods, mods, norm_w.reshape(1, D), w_router,
      jnp.broadcast_to(b_router.reshape(LANES, 1), (LANES, ROUTE_SUB)))


def _moe_meta(counts):
    tm = MOE_TM
    experts = jnp.arange(N_EXPERTS, dtype=jnp.int32)
    counts = jnp.max(counts, axis=1).astype(jnp.int32)
    padded = ((counts + tm - 1) // tm) * tm
    ends = jnp.cumsum(padded)
    tile_start = jnp.arange(MOE_TILES, dtype=jnp.int32) * tm
    te = jnp.sum((tile_start[:, None] >= ends[None, :]).astype(jnp.int32), axis=1)
    last = jnp.max(jnp.where(counts > 0, experts, 0))
    meta = jnp.concatenate([jnp.minimum(te, last), ends[-1:] // tm]).astype(jnp.int32)
    starts = ends - padded
    pads = jnp.concatenate([starts + counts, ends[-1:], ends, jnp.full((1,), MOE_ROWS)]).astype(jnp.int32)
    return starts.astype(jnp.int32), pads, meta


def _expert_changed(meta_ref, j):
    return (j == 0) | (meta_ref[j] != meta_ref[jnp.maximum(j - 1, 0)])


def _moe_up_kernel(pos1_ref, pos2_ref, pads_ref, meta_ref, tr_hbm, wg_ref, wu_ref, a_ref, rowmap_ref,
                   tr_scr, g0, g1, wg_bf, wu_bf, sem):
    j = pl.program_id(0)
    tm = MOE_TM
    ntiles = meta_ref[MOE_TILES]

    def gather(tile, dst):
        for mi in range(tm):
            tok = jnp.minimum(rowmap_ref[tile * tm + mi] >> 1, T - 1)
            dst[mi * SUBLANES:(mi + 1) * SUBLANES, :] = tr_scr[pl.ds(pl.multiple_of(tok * SUBLANES, SUBLANES), SUBLANES), :]

    @pl.when(j == 0)
    def _():
        load = pltpu.make_async_copy(tr_hbm, tr_scr, sem)
        load.start()

        def clear(c, carry):
            for i in range(SUBLANES):
                rowmap_ref[c * SUBLANES + i] = 2 * T
            return carry
        for k in range(N_EXPERTS + 1):
            lax.fori_loop(pads_ref[k] // SUBLANES, pads_ref[N_EXPERTS + 1 + k] // SUBLANES, clear, 0)

        def place(t, carry):
            rowmap_ref[pos1_ref[t]] = 2 * t
            rowmap_ref[pos2_ref[t]] = 2 * t + 1
            return carry
        lax.fori_loop(0, T, place, 0, unroll=8)
        load.wait()
        gather(0, g0)

    def compute(cur, nxt):
        gather(jnp.minimum(j + 1, ntiles - 1), nxt)
        x = _tokmajor_to_std(cur, tm).astype(BF16)
        g = jnp.dot(x, wg_bf[...], preferred_element_type=F32)
        u = jnp.dot(x, wu_bf[...], preferred_element_type=F32)
        a_ref[...] = (_silu(g) * u).astype(a_ref.dtype)

    @pl.when(j < ntiles)
    def _():
        @pl.when(_expert_changed(meta_ref, j))
        def _():
            wg_bf[...] = wg_ref[...].astype(BF16)
            wu_bf[...] = wu_ref[...].astype(BF16)

        pl.when(j % 2 == 0)(functools.partial(compute, g0, g1))
        pl.when(j % 2 == 1)(functools.partial(compute, g1, g0))

    @pl.when(j >= ntiles)
    def _():
        a_ref[...] = jnp.zeros(a_ref.shape, a_ref.dtype)


def _moe_down_kernel(rowmap_ref, cpair_ref, meta_ref, a_ref, wd_ref, *rest, layer, final):
    if final:
        x_hbm, mods_ref, fw_ref, out_c, out_l, acc, y0, y1, wd_bf, xin, xout, io_sems = rest
    else:
        out_hbm, acc, y0, y1, wd_bf = rest
    j = pl.program_id(0)
    tm = MOE_TM
    zrows = MERGE_TM
    ntiles = meta_ref[MOE_TILES]

    @pl.when(j == 0)
    def _():
        def zero(i, carry):
            acc[pl.ds(pl.multiple_of(i * zrows, zrows), zrows), :] = jnp.zeros((zrows, LANES), F32)
            return carry
        lax.fori_loop(0, ACC_ROWS // zrows, zero, 0)

    def matmul(dst):
        y = jnp.dot(a_ref[...], wd_bf[...], preferred_element_type=F32)
        for k in range(D // LANES):
            dst[pl.ds(k, tm, stride=SUBLANES), :] = y[:, k * LANES:(k + 1) * LANES]

    def scatter(tile, src):
        for b in range(tm // SUBLANES):
            ents = [rowmap_ref[tile * tm + b * SUBLANES + i] for i in range(SUBLANES)]
            offs = [pl.multiple_of((e >> 1) * SUBLANES, SUBLANES) for e in ents]
            olds = [acc[pl.ds(o, SUBLANES), :] for o in offs]
            for i, o in enumerate(offs):
                r = (b * SUBLANES + i) * SUBLANES
                acc[pl.ds(o, SUBLANES), :] = olds[i] + cpair_ref[ents[i]] * src[r:r + SUBLANES, :]

    has_mm = j < ntiles
    has_sc = (j >= 1) & (j <= ntiles)

    @pl.when(has_mm)
    def _():
        @pl.when(_expert_changed(meta_ref, j))
        def _():
            wd_bf[...] = wd_ref[...].astype(BF16)

    for par, (cur, prev) in enumerate(((y0, y1), (y1, y0))):
        mine = (j % 2) == par

        @pl.when(mine & has_mm & has_sc)
        def _():
            matmul(cur)
            scatter(j - 1, prev)

        @pl.when(mine & has_mm & jnp.logical_not(has_sc))
        def _():
            matmul(cur)

        @pl.when(mine & jnp.logical_not(has_mm) & has_sc)
        def _():
            scatter(j - 1, prev)

    @pl.when(j == pl.num_programs(0) - 1)
    def _():
        if not final:
            pltpu.sync_copy(acc.at[0:T * SUBLANES, :], out_hbm)
            return
        ft = MERGE_TM
        n_t = T // ft
        n_c = T_CTX // ft

        def fetch(i):
            return pltpu.make_async_copy(x_hbm.at[i * ft:(i + 1) * ft, :], xin.at[i % 2], io_sems.at[i % 2])

        def flush(i):
            dst = out_c.at[i * ft:(i + 1) * ft, :] if i < n_c else out_l.at[(i - n_c) * ft:(i - n_c + 1) * ft, :]
            return pltpu.make_async_copy(xout.at[i % 2], dst, io_sems.at[2 + i % 2])

        fetch(0).start()
        for i in range(n_t):
            if i + 1 < n_t:
                fetch(i + 1).start()
            fetch(i).wait()
            moe = jnp.concatenate([acc[pl.ds(i * ft * SUBLANES + k, ft, stride=SUBLANES), :]
                                   for k in range(D // LANES)], axis=1)
            row = 0 if i * ft < T_CTX else 1 + (i * ft - T_CTX) // DEC_SEQ
            x = xin[i % 2] + mods_ref[(layer * SUBLANES + row) * 6 + 5] * moe
            if i >= 2:
                flush(i - 2).wait()
            xout[i % 2] = _rms(x, fw_ref[...])
            flush(i).start()
        for i in range(max(n_t - 2, 0), n_t):
            flush(i).wait()


def _moe(tr, route_t, counts, layer, w_gate, w_up, w_down, xmid=None, mods=None, final_w=None):
    tm = MOE_TM
    starts, pads, meta = _moe_meta(counts)
    experts = jnp.arange(N_EXPERTS, dtype=jnp.int32)

    def position(e, r):
        sel = e.astype(jnp.int32)[:, None] == experts[None, :]
        return jnp.sum(jnp.where(sel, starts[None, :], 0), axis=1) + r.astype(jnp.int32)

    pos1 = position(route_t[0], route_t[4])
    pos2 = position(route_t[1], route_t[5])
    cpair = jnp.concatenate([jnp.stack([route_t[2], route_t[3]], axis=1).reshape(2 * T), jnp.zeros((2,), F32)])

    def wspec(shape, n):
        return pl.BlockSpec((None, None) + shape,
                            lambda j, *pre: (layer, pre[n - 1][jnp.minimum(j, MOE_TILES - 1)], 0, 0))

    gscr = pltpu.VMEM((tm * SUBLANES, LANES), F32)
    act, rowmap = pl.pallas_call(
        _moe_up_kernel,
        grid_spec=pltpu.PrefetchScalarGridSpec(
            num_scalar_prefetch=4, grid=(MOE_TILES,),
            in_specs=[pl.BlockSpec(memory_space=pl.ANY), wspec((D, D_EXPERT), 4), wspec((D, D_EXPERT), 4)],
            out_specs=[pl.BlockSpec((tm, D_EXPERT), lambda j, *pre: (j, 0)), pl.BlockSpec(memory_space=pltpu.SMEM)],
            scratch_shapes=[pltpu.VMEM((T * SUBLANES, LANES), F32), gscr, gscr,
                            pltpu.VMEM((D, D_EXPERT), BF16), pltpu.VMEM((D, D_EXPERT), BF16),
                            pltpu.SemaphoreType.DMA(())]),
        out_shape=[jax.ShapeDtypeStruct((MOE_ROWS, D_EXPERT), BF16), jax.ShapeDtypeStruct((MOE_ROWS,), jnp.int32)],
        compiler_params=_cparams(1),
        name=f"moe_up{layer}",
    )(pos1, pos2, pads, meta, tr, w_gate, w_up)

    final = final_w is not None
    any_spec = pl.BlockSpec(memory_space=pl.ANY)
    in_specs = [pl.BlockSpec((tm, D_EXPERT), lambda j, *pre: (jnp.minimum(j, MOE_TILES - 1), 0)),
                wspec((D_EXPERT, D), 3)]
    args = [rowmap, cpair, meta, act, w_down]
    scratch = [pltpu.VMEM((ACC_ROWS, LANES), F32), gscr, gscr, pltpu.VMEM((D_EXPERT, D), BF16)]
    if final:
        stage = pltpu.VMEM((2, MERGE_TM, D), F32)
        in_specs += [any_spec, pl.BlockSpec(mods.shape, lambda j, *pre: (0, 0, 0)),
                     pl.BlockSpec((1, D), lambda j, *pre: (0, 0))]
        args += [xmid, mods, final_w.reshape(1, D)]
        scratch += [stage, stage, pltpu.SemaphoreType.DMA((4,))]
        out_shape = [jax.ShapeDtypeStruct((T_CTX, D), F32), jax.ShapeDtypeStruct((T_LAT, D), F32)]
    else:
        out_shape = [jax.ShapeDtypeStruct((T * SUBLANES, LANES), F32)]
    return pl.pallas_call(
        functools.partial(_moe_down_kernel, layer=layer, final=final),
        grid_spec=pltpu.PrefetchScalarGridSpec(
            num_scalar_prefetch=3, grid=(MOE_TILES + 1,), in_specs=in_specs,
            out_specs=[any_spec] * len(out_shape), scratch_shapes=scratch),
        out_shape=out_shape,
        compiler_params=_cparams(1),
        name=f"moe_down{layer}",
    )(*args)


def _pad_lanes(v):
    return jnp.pad(v.astype(F32), (0, LANES - v.shape[0])).reshape(1, LANES)


def kernel(x_prompt, x_sample, cache_k_attn, cache_v_attn, state_ssd_fwd, state_ssd_bwd, state_gla_fwd, state_gla_bwd, c, c_ctx, w_ada, b_ada, norm_mix_w, norm_ffn_w, w_in_even, conv_w, conv_b, dt_bias_fwd, dt_bias_bwd, a_log_fwd, a_log_bwd, d_skip, ssd_norm_w, attn_sink, w_out_even, w_in_odd, w_gk2_fwd, b_gk_fwd, w_gk2_bwd, b_gk_bwd, gla_norm_w, w_out_odd, w_router_group, b_router_group, w_router_expert, b_router_expert, w_gate_exp, w_up_exp, w_down_exp, final_norm_w):
    depth = w_ada.shape[0]
    assert depth == 2 and x_prompt.shape == (BATCH, SEQ, D) and x_sample.shape == (DEC_BATCH, DEC_SEQ, D)

    cond8 = jnp.concatenate([c_ctx[None, :], c, jnp.zeros((SUBLANES - 1 - DEC_BATCH, D), F32)], axis=0)
    mods = _adaln(cond8, w_ada, b_ada).reshape(depth * SUBLANES * 6, 1, D)
    xs0 = (x_prompt.reshape(T_CTX, D), x_sample.reshape(T_LAT, D))

    def router_params(i):
        wr = jnp.concatenate([w_router_group[i], w_router_expert[i],
                              jnp.zeros((D, LANES - N_GROUPS - N_EXPERTS), F32)], axis=1)
        return wr, _pad_lanes(jnp.concatenate([b_router_group[i], b_router_expert[i]]))

    n_zxbc = 2 * D_SSD + 2 * SSD_GROUPS * SSD_N
    n_dt = 2 * SSD_HEADS
    half = PROJ_TN // 2
    r_bc, r_q = 2 * D_SSD, n_zxbc + n_dt
    r_kv = r_q + D
    rows0 = [(0, half), (D_SSD, D_SSD + half), (r_q, r_q + half), (r_bc, r_kv)]
    p0, p0dt = _modproj(xs0, mods, 0, norm_mix_w[0], jnp.transpose(w_in_even[0]), rows0, n_zxbc)

    dtb = _pad_lanes(jnp.concatenate([dt_bias_fwd[0], dt_bias_bwd[0]]))
    alog = _pad_lanes(jnp.concatenate([a_log_fwd[0], a_log_bwd[0]]))
    dskip = jnp.repeat(d_skip[0], SSD_P).reshape(1, D_SSD)
    ssd_args = (conv_w[0], conv_b[0], dtb, alog, dskip, ssd_norm_w[0])
    y_ssd_c, ssd_f, ssd_b = _ssd(p0, p0dt, 0, BATCH, SEQ, *ssd_args)
    y_ssd_l, _, _ = _ssd(p0, p0dt, T_CTX, DEC_BATCH, DEC_SEQ, *ssd_args,
                         h0f=state_ssd_fwd[:, 0].reshape(DEC_BATCH, D_SSD, SSD_N),
                         h0b=state_ssd_bwd[:, 0].reshape(DEC_BATCH, D_SSD, SSD_N))
    sink = _pad_lanes(attn_sink[0])
    y_att_c, new_kt, new_vt = _ctx_attn(p0, sink)
    def cache_in(t):
        return jnp.transpose(t[:, 0], (0, 2, 3, 1)).reshape(DEC_BATCH, ATT_KV_DIM, PAST_LEN)

    y_att_l = _lat_attn(p0, cache_in(cache_k_attn), cache_in(cache_v_attn), sink)
    xmid0, tr0, route0, cnt0 = _outproj_route([y_ssd_c, y_ssd_l, y_att_c, y_att_l], w_out_even[0], xs0, mods, 0,
                                              norm_ffn_w[0], *router_params(0))
    (moe0,) = _moe(tr0, route0, cnt0, 0, w_gate_exp, w_up_exp, w_down_exp)

    dk_all = GLA_HEADS * GLA_DK
    n_qkvr = 2 * dk_all + 2 * GLA_HEADS * GLA_DV
    n_odd = w_in_odd.shape[2]
    p1, p1lr, x1 = _modproj(xmid0, mods, 1, norm_mix_w[1], jnp.transpose(w_in_odd[0]),
                            [(k * PROJ_TN, k * PROJ_TN + half) for k in range(n_qkvr // PROJ_TN)], n_odd - LANES,
                            moe=moe0)
    lr0 = LANES - 2 * GLA_LOWRANK
    w2f = jnp.zeros((LANES, dk_all), F32).at[lr0:lr0 + GLA_LOWRANK].set(w_gk2_fwd[0])
    w2b = jnp.zeros((LANES, dk_all), F32).at[lr0 + GLA_LOWRANK:].set(w_gk2_bwd[0])
    gla_args = (w2f, w2b, b_gk_fwd[0], b_gk_bwd[0], gla_norm_w[0])
    y_gla_c, gla_f, gla_b = _gla(p1, p1lr, 0, BATCH, SEQ, *gla_args)
    y_gla_l, _, _ = _gla(p1, p1lr, T_CTX, DEC_BATCH, DEC_SEQ, *gla_args,
                         s0f=state_gla_fwd[:, 0].reshape(DEC_BATCH, dk_all, GLA_DV),
                         s0b=state_gla_bwd[:, 0].reshape(DEC_BATCH, dk_all, GLA_DV))
    xmid1, tr1, route1, cnt1 = _outproj_route([y_gla_c, y_gla_l], w_out_odd[0], (x1,), mods, 1,
                                              norm_ffn_w[1], *router_params(1))
    y_c, y_l = _moe(tr1, route1, cnt1, 1, w_gate_exp, w_up_exp, w_down_exp, xmid1, mods, final_w=final_norm_w)

    y_prompt = y_c.reshape(BATCH, SEQ, D)
    y_sample = y_l.reshape(DEC_BATCH, DEC_SEQ, D)
    def cache_out(t):
        return jnp.transpose(t.reshape(BATCH, 1, ATT_KV, ATT_HD, SEQ), (0, 1, 4, 2, 3))

    new_k, new_v = cache_out(new_kt), cache_out(new_vt)
    return (y_prompt, y_sample, new_k, new_v,
            ssd_f.reshape(BATCH, 1, SSD_HEADS, SSD_P, SSD_N), ssd_b.reshape(BATCH, 1, SSD_HEADS, SSD_P, SSD_N),
            gla_f.reshape(BATCH, 1, GLA_HEADS, GLA_DK, GLA_DV), gla_b.reshape(BATCH, 1, GLA_HEADS, GLA_DK, GLA_DV))
```
